```python
import math
import jax, jax.numpy as jnp
from jax import lax
import numpy as np

D_MODEL = 1024
BATCH = 8
SEQ = 8192
DEPTH = 1

D_MIX = D_MODEL
MLA_HEADS = 8
MLA_NOPE = 64
MLA_ROPE = 32
MLA_V = 64
MLA_WIDTH = MLA_HEADS * MLA_V
Q_LORA = 256
KV_LORA = 128
ROPE_THETA = 10000.0
Q_BLOCK = 128
CHUNK = 128
G_HEADS = 8
G_WIDTH = D_MIX - MLA_WIDTH
G_HEAD_DIM = G_WIDTH // G_HEADS
D_IN = Q_LORA + KV_LORA + MLA_ROPE + MLA_WIDTH + 3 * G_WIDTH
DN_ALPHA = (2.0 * DEPTH) ** 0.25
DN_BETA = (8.0 * DEPTH) ** -0.25
EPS = 1e-5

kernel_name = "hybrid_mla_gmlp_parallel_deepnorm"


def _rmsnorm(x, g):
    xf = x.astype(jnp.float32)
    y = xf * lax.rsqrt(jnp.mean(xf * xf, axis=-1, keepdims=True) + EPS)
    return (y * g.astype(jnp.float32)).astype(x.dtype)


def _layernorm(x, g, b):
    xf = x.astype(jnp.float32)
    mu = jnp.mean(xf, axis=-1, keepdims=True)
    var = jnp.mean(jnp.square(xf - mu), axis=-1, keepdims=True)
    y = (xf - mu) * lax.rsqrt(var + EPS)
    return (y * g.astype(jnp.float32) + b.astype(jnp.float32)).astype(x.dtype)


def _rope(t, positions):
    half = MLA_ROPE // 2
    inv_freq = 1.0 / (ROPE_THETA ** (jnp.arange(half, dtype=jnp.float32) / half))
    ang = positions.astype(jnp.float32)[..., None] * inv_freq
    cos = jnp.cos(ang)[:, :, None, :].astype(t.dtype)
    sin = jnp.sin(ang)[:, :, None, :].astype(t.dtype)
    t1, t2 = t[..., :half], t[..., half:]
    return jnp.concatenate([t1 * cos - t2 * sin, t1 * sin + t2 * cos], axis=-1)


def _causal_attention(q, k, v):
    b, s, h, dqk = q.shape
    dv = v.shape[-1]
    nb = s // Q_BLOCK
    scale = 1.0 / math.sqrt(dqk)
    qb = q.reshape(b, nb, Q_BLOCK, h, dqk).transpose(1, 0, 2, 3, 4)
    kpos = jnp.arange(s)

    def one_block(args):
        qi, i = args
        sc = jnp.einsum('bqhd,bkhd->bhqk', qi, k).astype(jnp.float32) * scale
        qpos = i * Q_BLOCK + jnp.arange(Q_BLOCK)
        mask = kpos[None, :] <= qpos[:, None]
        sc = jnp.where(mask[None, None], sc, -jnp.inf)
        p = jax.nn.softmax(sc, axis=-1).astype(v.dtype)
        return jnp.einsum('bhqk,bkhd->bqhd', p, v)

    o = lax.map(one_block, (qb, jnp.arange(nb)))
    return o.transpose(1, 0, 2, 3, 4).reshape(b, s, h, dv)


def _fwd_setup_inputs(seed: int = 0) -> dict:
    key = jax.random.key(seed)
    ks = jax.random.split(key, 16)
    f32 = jnp.float32
    x = jax.random.normal(ks[0], (BATCH, SEQ, D_MODEL), f32)
    positions = jnp.broadcast_to(jnp.arange(SEQ, dtype=jnp.int32)[None, :], (BATCH, SEQ))
    w_in = jax.random.normal(ks[1], (D_MODEL, D_IN), f32) * D_MODEL ** -0.5
    q_norm_g = 1.0 + 0.02 * jax.random.normal(ks[2], (Q_LORA,), f32)
    w_uq = jax.random.normal(ks[3], (Q_LORA, MLA_HEADS * (MLA_NOPE + MLA_ROPE)), f32) * Q_LORA ** -0.5
    kv_norm_g = 1.0 + 0.02 * jax.random.normal(ks[4], (KV_LORA,), f32)
    w_ukv = jax.random.normal(ks[5], (KV_LORA, MLA_HEADS * (MLA_NOPE + MLA_V)), f32) * KV_LORA ** -0.5
    sgu_norm_g = 1.0 + 0.02 * jax.random.normal(ks[6], (G_WIDTH,), f32)
    sgu_norm_b = 0.02 * jax.random.normal(ks[7], (G_WIDTH,), f32)
    w_spatial = jax.random.normal(ks[8], (G_HEADS, CHUNK, CHUNK), f32) * CHUNK ** -0.5
    b_spatial = 1.0 + 0.02 * jax.random.normal(ks[9], (G_HEADS, CHUNK), f32)
    w_out = jax.random.normal(ks[10], (D_MIX, D_MODEL), f32) * (D_MIX ** -0.5) * DN_BETA
    ln_g = 1.0 + 0.02 * jax.random.normal(ks[11], (D_MODEL,), f32)
    ln_b = 0.02 * jax.random.normal(ks[12], (D_MODEL,), f32)
    return {"x": x, "positions": positions, "w_in": w_in, "q_norm_g": q_norm_g,
            "w_uq": w_uq, "kv_norm_g": kv_norm_g, "w_ukv": w_ukv,
            "sgu_norm_g": sgu_norm_g, "sgu_norm_b": sgu_norm_b,
            "w_spatial": w_spatial, "b_spatial": b_spatial, "w_out": w_out,
            "ln_g": ln_g, "ln_b": ln_b}


def _hybrid_mixer(h, positions, w_in, q_norm_g, w_uq, kv_norm_g, w_ukv,
                  sgu_norm_g, sgu_norm_b, w_spatial, b_spatial, w_out):
    b, s, _ = h.shape
    proj = jnp.einsum('bsd,de->bse', h, w_in)
    splits = np.cumsum([Q_LORA, KV_LORA, MLA_ROPE, MLA_WIDTH, G_WIDTH, G_WIDTH]).tolist()
    c_q, c_kv, k_rope, z_a, u, v = jnp.split(proj, splits, axis=-1)[:6]
    z_b = proj[..., splits[-1]:]

    q = jnp.einsum('bsr,re->bse', _rmsnorm(c_q, q_norm_g), w_uq)
    q = q.reshape(b, s, MLA_HEADS, MLA_NOPE + MLA_ROPE)
    q_nope, q_rope = q[..., :MLA_NOPE], _rope(q[..., MLA_NOPE:], positions)
    kv = jnp.einsum('bsr,re->bse', _rmsnorm(c_kv, kv_norm_g), w_ukv)
    kv = kv.reshape(b, s, MLA_HEADS, MLA_NOPE + MLA_V)
    k_nope, val = kv[..., :MLA_NOPE], kv[..., MLA_NOPE:]
    k_r = jnp.broadcast_to(_rope(k_rope[:, :, None, :], positions), (b, s, MLA_HEADS, MLA_ROPE))
    qf = jnp.concatenate([q_nope, q_rope], axis=-1)
    kf = jnp.concatenate([k_nope, k_r], axis=-1)
    attn = _causal_attention(qf, kf, val).reshape(b, s, MLA_WIDTH)
    out_a = attn * jax.nn.silu(z_a)

    u = jax.nn.gelu(u, approximate=False)
    v = _layernorm(jax.nn.gelu(v, approximate=False), sgu_norm_g, sgu_norm_b)
    nc = s // CHUNK
    vc = v.reshape(b, nc, CHUNK, G_HEADS, G_HEAD_DIM)
    causal = jnp.tril(jnp.ones((CHUNK, CHUNK), dtype=bool))
    w_s = jnp.where(causal[None], w_spatial, 0.0).astype(v.dtype)
    sv = jnp.einsum('hts,bcshd->bcthd', w_s, vc) + b_spatial.T[None, None, :, :, None]
    sgu = u * sv.reshape(b, s, G_WIDTH)
    out_b = sgu * jax.nn.silu(z_b)

    merged = jnp.concatenate([out_a, out_b], axis=-1)
    return jnp.einsum('bse,ed->bsd', merged, w_out)


def _fwd_reference(x, positions, w_in, q_norm_g, w_uq, kv_norm_g, w_ukv,
              sgu_norm_g, sgu_norm_b, w_spatial, b_spatial, w_out, ln_g, ln_b):
    h = x
    for _ in range(DEPTH):
        y = _hybrid_mixer(h, positions, w_in, q_norm_g, w_uq, kv_norm_g, w_ukv,
                          sgu_norm_g, sgu_norm_b, w_spatial, b_spatial, w_out)
        h = _layernorm(DN_ALPHA * h + y, ln_g, ln_b)
    return h


import jax as _jax
import jax.numpy as _jnp

TWIN_FORMAT = 'train_step'
FWD_PARAMS = ['x', 'positions', 'w_in', 'q_norm_g', 'w_uq', 'kv_norm_g', 'w_ukv', 'sgu_norm_g', 'sgu_norm_b', 'w_spatial', 'b_spatial', 'w_out', 'ln_g', 'ln_b']
TWIN_WEIGHTS = ['w_in', 'q_norm_g', 'w_uq', 'kv_norm_g', 'w_ukv', 'sgu_norm_g', 'sgu_norm_b', 'w_spatial', 'b_spatial', 'w_out', 'ln_g', 'ln_b']
TWIN_DIFF_INPUT = 'x'
TWIN_INPUTS = ['x', 'positions', 'w_in', 'q_norm_g', 'w_uq', 'kv_norm_g', 'w_ukv', 'sgu_norm_g', 'sgu_norm_b', 'w_spatial', 'b_spatial', 'w_out', 'ln_g', 'ln_b', 'loss_target', 'm_w_in', 'm_q_norm_g', 'm_w_uq', 'm_kv_norm_g', 'm_w_ukv', 'm_sgu_norm_g', 'm_sgu_norm_b', 'm_w_spatial', 'm_b_spatial', 'm_w_out', 'm_ln_g', 'm_ln_b', 'v_w_in', 'v_q_norm_g', 'v_w_uq', 'v_kv_norm_g', 'v_w_ukv', 'v_sgu_norm_g', 'v_sgu_norm_b', 'v_w_spatial', 'v_b_spatial', 'v_w_out', 'v_ln_g', 'v_ln_b']
TWIN_OUTPUTS = ['loss', 'grad_x', 'grad_w_in', 'grad_q_norm_g', 'grad_w_uq', 'grad_kv_norm_g', 'grad_w_ukv', 'grad_sgu_norm_g', 'grad_sgu_norm_b', 'grad_w_spatial', 'grad_b_spatial', 'grad_w_out', 'grad_ln_g', 'grad_ln_b', 'delta_w_in', 'delta_q_norm_g', 'delta_w_uq', 'delta_kv_norm_g', 'delta_w_ukv', 'delta_sgu_norm_g', 'delta_sgu_norm_b', 'delta_w_spatial', 'delta_b_spatial', 'delta_w_out', 'delta_ln_g', 'delta_ln_b', 'new_m_w_in', 'new_m_q_norm_g', 'new_m_w_uq', 'new_m_kv_norm_g', 'new_m_w_ukv', 'new_m_sgu_norm_g', 'new_m_sgu_norm_b', 'new_m_w_spatial', 'new_m_b_spatial', 'new_m_w_out', 'new_m_ln_g', 'new_m_ln_b', 'new_v_w_in', 'new_v_q_norm_g', 'new_v_w_uq', 'new_v_kv_norm_g', 'new_v_w_ukv', 'new_v_sgu_norm_g', 'new_v_sgu_norm_b', 'new_v_w_spatial', 'new_v_b_spatial', 'new_v_w_out', 'new_v_ln_g', 'new_v_ln_b']
TWIN_LEAF_KINDS = {'loss': 'loss', 'grad_x': 'grad_x', 'grad_w_in': 'grad_w', 'grad_q_norm_g': 'grad_w', 'grad_w_uq': 'grad_w', 'grad_kv_norm_g': 'grad_w', 'grad_w_ukv': 'grad_w', 'grad_sgu_norm_g': 'grad_w', 'grad_sgu_norm_b': 'grad_w', 'grad_w_spatial': 'grad_w', 'grad_b_spatial': 'grad_w', 'grad_w_out': 'grad_w', 'grad_ln_g': 'grad_w', 'grad_ln_b': 'grad_w', 'delta_w_in': 'delta_w', 'delta_q_norm_g': 'delta_w', 'delta_w_uq': 'delta_w', 'delta_kv_norm_g': 'delta_w', 'delta_w_ukv': 'delta_w', 'delta_sgu_norm_g': 'delta_w', 'delta_sgu_norm_b': 'delta_w', 'delta_w_spatial': 'delta_w', 'delta_b_spatial': 'delta_w', 'delta_w_out': 'delta_w', 'delta_ln_g': 'delta_w', 'delta_ln_b': 'delta_w', 'new_m_w_in': 'new_m', 'new_m_q_norm_g': 'new_m', 'new_m_w_uq': 'new_m', 'new_m_kv_norm_g': 'new_m', 'new_m_w_ukv': 'new_m', 'new_m_sgu_norm_g': 'new_m', 'new_m_sgu_norm_b': 'new_m', 'new_m_w_spatial': 'new_m', 'new_m_b_spatial': 'new_m', 'new_m_w_out': 'new_m', 'new_m_ln_g': 'new_m', 'new_m_ln_b': 'new_m', 'new_v_w_in': 'new_v', 'new_v_q_norm_g': 'new_v', 'new_v_w_uq': 'new_v', 'new_v_kv_norm_g': 'new_v', 'new_v_w_ukv': 'new_v', 'new_v_sgu_norm_g': 'new_v', 'new_v_sgu_norm_b': 'new_v', 'new_v_w_spatial': 'new_v', 'new_v_b_spatial': 'new_v', 'new_v_w_out': 'new_v', 'new_v_ln_g': 'new_v', 'new_v_ln_b': 'new_v'}


def _forward(args):
    return _fwd_reference(*[args[k] for k in FWD_PARAMS])


def _output_shape():
    def fwd():
        inp = _fwd_setup_inputs(0)
        return _fwd_reference(*[inp[k] for k in FWD_PARAMS])
    out = _jax.eval_shape(fwd)
    return out.shape, out.dtype

N_MICROBATCH = 1
ADAM_LR = 0.001
ADAM_B1 = 0.9
ADAM_B2 = 0.999
ADAM_EPS = 1e-08
ADAM_WD = 0.01
ADAM_STEP = 10
PER_EXAMPLE_BATCH_AXIS = {'x': 0, 'positions': 0, 'loss_target': 0}
SHARED_INPUTS = []
_WEIGHT_DTYPES = {'w_in': _jnp.float32, 'q_norm_g': _jnp.float32, 'w_uq': _jnp.float32, 'kv_norm_g': _jnp.float32, 'w_ukv': _jnp.float32, 'sgu_norm_g': _jnp.float32, 'sgu_norm_b': _jnp.float32, 'w_spatial': _jnp.float32, 'b_spatial': _jnp.float32, 'w_out': _jnp.float32, 'ln_g': _jnp.float32, 'ln_b': _jnp.float32}
MOMENT_SCALE = {'w_in': 4.498701e-02, 'q_norm_g': 1.852776e-02, 'w_uq': 1.129073e-02, 'kv_norm_g': 4.771848e-02, 'w_ukv': 1.440992e-02, 'sgu_norm_g': 3.696424e-02, 'sgu_norm_b': 3.383369e-02, 'w_spatial': 2.406897e-02, 'b_spatial': 3.466188e-02, 'w_out': 7.291079e-02, 'ln_g': 6.402326e+01, 'ln_b': 1.433255e+00}


def _to_microbatches(a, axis):
    t = _jnp.moveaxis(a, axis, 0)
    t = t.reshape((N_MICROBATCH, t.shape[0] // N_MICROBATCH) + t.shape[1:])
    return _jnp.moveaxis(t, 1, axis + 1)


def setup_inputs(seed: int = 0) -> dict:
    inp = _fwd_setup_inputs(seed)
    key = _jax.random.fold_in(_jax.random.key(seed), 7919)
    shape, _ = _output_shape()
    out = dict(inp)
    out["loss_target"] = _jax.random.normal(_jax.random.fold_in(key, 0), shape, _jnp.float32)
    for i, name in enumerate(TWIN_WEIGHTS):
        w = inp[name].astype(_jnp.float32)
        if MOMENT_SCALE is None:
            s = _jnp.sqrt(_jnp.mean(_jnp.square(w)) + 1e-30)
        else:
            s = MOMENT_SCALE[name]
        km, kv = _jax.random.split(_jax.random.fold_in(key, i + 1))
        out[name] = w
        out["m_" + name] = s * _jax.random.normal(km, w.shape, _jnp.float32)
        out["v_" + name] = (s * s) * _jax.random.uniform(kv, w.shape, _jnp.float32, 0.5, 1.5)
    if N_MICROBATCH > 1:
        for name, axis in PER_EXAMPLE_BATCH_AXIS.items():
            out[name] = _to_microbatches(out[name], axis)
    return {'x': out['x'], 'positions': out['positions'], 'w_in': out['w_in'], 'q_norm_g': out['q_norm_g'], 'w_uq': out['w_uq'], 'kv_norm_g': out['kv_norm_g'], 'w_ukv': out['w_ukv'], 'sgu_norm_g': out['sgu_norm_g'], 'sgu_norm_b': out['sgu_norm_b'], 'w_spatial': out['w_spatial'], 'b_spatial': out['b_spatial'], 'w_out': out['w_out'], 'ln_g': out['ln_g'], 'ln_b': out['ln_b'], 'loss_target': out['loss_target'], 'm_w_in': out['m_w_in'], 'm_q_norm_g': out['m_q_norm_g'], 'm_w_uq': out['m_w_uq'], 'm_kv_norm_g': out['m_kv_norm_g'], 'm_w_ukv': out['m_w_ukv'], 'm_sgu_norm_g': out['m_sgu_norm_g'], 'm_sgu_norm_b': out['m_sgu_norm_b'], 'm_w_spatial': out['m_w_spatial'], 'm_b_spatial': out['m_b_spatial'], 'm_w_out': out['m_w_out'], 'm_ln_g': out['m_ln_g'], 'm_ln_b': out['m_ln_b'], 'v_w_in': out['v_w_in'], 'v_q_norm_g': out['v_q_norm_g'], 'v_w_uq': out['v_w_uq'], 'v_kv_norm_g': out['v_kv_norm_g'], 'v_w_ukv': out['v_w_ukv'], 'v_sgu_norm_g': out['v_sgu_norm_g'], 'v_sgu_norm_b': out['v_sgu_norm_b'], 'v_w_spatial': out['v_w_spatial'], 'v_b_spatial': out['v_b_spatial'], 'v_w_out': out['v_w_out'], 'v_ln_g': out['v_ln_g'], 'v_ln_b': out['v_ln_b']}


def _loss(weights, diff, rest, loss_target):
    with _jax.named_scope("forward"):
        args = {**rest, TWIN_DIFF_INPUT: diff, **{k: w.astype(_WEIGHT_DTYPES[k]) for k, w in weights.items()}}
        y = _forward(args)
    with _jax.named_scope("loss_head"):
        err = _jnp.square(y.astype(_jnp.float32) - loss_target)
        return 0.5 * _jnp.sum(_jnp.mean(err, axis=-1)) if err.ndim else 0.5 * err


def _adamw(w, g, m, v):
    m = ADAM_B1 * m + (1.0 - ADAM_B1) * g
    v = ADAM_B2 * v + (1.0 - ADAM_B2) * _jnp.square(g)
    m_hat = m / (1.0 - ADAM_B1 ** ADAM_STEP)
    v_hat = v / (1.0 - ADAM_B2 ** ADAM_STEP)
    delta = -ADAM_LR * (m_hat / (_jnp.sqrt(v_hat) + ADAM_EPS) + ADAM_WD * w)
    return delta, m, v


def reference(x, positions, w_in, q_norm_g, w_uq, kv_norm_g, w_ukv, sgu_norm_g, sgu_norm_b, w_spatial, b_spatial, w_out, ln_g, ln_b, loss_target, m_w_in, m_q_norm_g, m_w_uq, m_kv_norm_g, m_w_ukv, m_sgu_norm_g, m_sgu_norm_b, m_w_spatial, m_b_spatial, m_w_out, m_ln_g, m_ln_b, v_w_in, v_q_norm_g, v_w_uq, v_kv_norm_g, v_w_ukv, v_sgu_norm_g, v_sgu_norm_b, v_w_spatial, v_b_spatial, v_w_out, v_ln_g, v_ln_b):
    given = dict(x=x, positions=positions, w_in=w_in, q_norm_g=q_norm_g, w_uq=w_uq, kv_norm_g=kv_norm_g, w_ukv=w_ukv, sgu_norm_g=sgu_norm_g, sgu_norm_b=sgu_norm_b, w_spatial=w_spatial, b_spatial=b_spatial, w_out=w_out, ln_g=ln_g, ln_b=ln_b, loss_target=loss_target, m_w_in=m_w_in, m_q_norm_g=m_q_norm_g, m_w_uq=m_w_uq, m_kv_norm_g=m_kv_norm_g, m_w_ukv=m_w_ukv, m_sgu_norm_g=m_sgu_norm_g, m_sgu_norm_b=m_sgu_norm_b, m_w_spatial=m_w_spatial, m_b_spatial=m_b_spatial, m_w_out=m_w_out, m_ln_g=m_ln_g, m_ln_b=m_ln_b, v_w_in=v_w_in, v_q_norm_g=v_q_norm_g, v_w_uq=v_w_uq, v_kv_norm_g=v_kv_norm_g, v_w_ukv=v_w_ukv, v_sgu_norm_g=v_sgu_norm_g, v_sgu_norm_b=v_sgu_norm_b, v_w_spatial=v_w_spatial, v_b_spatial=v_b_spatial, v_w_out=v_w_out, v_ln_g=v_ln_g, v_ln_b=v_ln_b)
    weights = {n: given[n] for n in TWIN_WEIGHTS}
    shared = {n: given[n] for n in SHARED_INPUTS}
    per_example = {n: given[n] for n in ['x', 'positions']}
    grad_fn = _jax.value_and_grad(_loss, argnums=(0, 1))

    def one_microbatch(ex, loss_target):
        ex = dict(ex)
        diff = ex.pop(TWIN_DIFF_INPUT)
        return grad_fn(weights, diff, {**shared, **ex}, loss_target)

    if N_MICROBATCH == 1:
        loss, (grad_w, grad_x) = one_microbatch(per_example, given["loss_target"])
    else:
        def body(carry, xs):
            loss_sum, grad_sum = carry
            l_k, (gw_k, gx_k) = one_microbatch(xs[0], xs[1])
            with _jax.named_scope("update"):
                return (loss_sum + l_k, _jax.tree.map(_jnp.add, grad_sum, gw_k)), gx_k

        init = (_jnp.zeros((), _jnp.float32), _jax.tree.map(_jnp.zeros_like, weights))
        (loss, grad_w), grad_x = _jax.lax.scan(body, init, (per_example, given["loss_target"]))
    with _jax.named_scope("update"):
        delta_w, new_m, new_v = {}, {}, {}
        for n in TWIN_WEIGHTS:
            delta_w[n], new_m[n], new_v[n] = _adamw(weights[n], grad_w[n], given["m_" + n], given["v_" + n])
    return (loss, grad_x, *[grad_w[n] for n in TWIN_WEIGHTS], *[delta_w[n] for n in TWIN_WEIGHTS],
            *[new_m[n] for n in TWIN_WEIGHTS], *[new_v[n] for n in TWIN_WEIGHTS])
```

```python
import functools
import math

import jax
import jax.numpy as jnp
from jax import lax
from jax.experimental import pallas as pl
from jax.experimental.pallas import tpu as pltpu

F32 = jnp.float32
BF16 = jnp.bfloat16

D_MODEL = 1024
Q_LORA = 256
KV_LORA = 128
HEADS = 8
NOPE = 64
ROPE = 32
VDIM = 64
MLA_W = HEADS * VDIM
GW = 512
CHUNK = 128
HP = 128
PAIRS = HEADS // 2
D_IN = 2464
D_INR = 2560
ROPE_THETA = 10000.0
DN_ALPHA = 2.0 ** 0.25
EPS = 1e-5
SCALE = 1.0 / math.sqrt(NOPE + ROPE)
INV_SQRT2 = 0.7071067811865476
INV_SQRT_2PI = 0.3989422804014327

ADAM_LR = 0.001
ADAM_B1 = 0.9
ADAM_B2 = 0.999
ADAM_EPS = 1e-08
ADAM_WD = 0.01
ADAM_STEP = 10

PV_QG, PV_KVG, PV_SG, PV_SB, PV_INVF, PV_M1, PV_M2, PV_LNG, PV_LNB = range(9)
PV_ROWS = 16

BIG_ROWS = 7616
HALF_ROWS = BIG_ROWS // 2
SMALL_ROWS = 1088
PART_ROWS = SMALL_ROWS // 8
MESH = pl.DeviceIdType.MESH

TOK_TILE = 256
ATT_BLK = 256
VMEM_LIMIT = 56 * 1024 * 1024


def _dot(a, b):
    return jnp.dot(a, b, preferred_element_type=F32)


def _dot_nt(a, b):
    return lax.dot_general(a, b, (((1,), (1,)), ((), ())), preferred_element_type=F32)


def _dot_tn(a, b):
    return lax.dot_general(a, b, (((0,), (0,)), ((), ())), preferred_element_type=F32)


def _sigmoid(z):
    return 1.0 / (1.0 + jnp.exp(-z))


def _gelu(x):
    return 0.5 * x * (1.0 + lax.erf(x * INV_SQRT2))


def _gelu_grad(x):
    return 0.5 * (1.0 + lax.erf(x * INV_SQRT2)) + x * (INV_SQRT_2PI * jnp.exp(-0.5 * x * x))


def _rms_stats(x):
    r = lax.rsqrt(jnp.mean(x * x, axis=-1, keepdims=True) + EPS)
    return x * r, r


def _rms_bwd(dy, g, xh, r):
    dyg = dy * g
    return r * (dyg - xh * jnp.mean(dyg * xh, axis=-1, keepdims=True))


def _ln_stats(x):
    mu = jnp.mean(x, axis=-1, keepdims=True)
    xc = x - mu
    r = lax.rsqrt(jnp.mean(xc * xc, axis=-1, keepdims=True) + EPS)
    return xc * r, r


def _ln_bwd(dy, g, xh, r):
    dxh = dy * g
    return r * (dxh - jnp.mean(dxh, axis=-1, keepdims=True) - xh * jnp.mean(dxh * xh, axis=-1, keepdims=True))


def _rope_fwd(t, c, s1, s2):
    return t * c + pltpu.roll(t, HP - 16, 1) * s1 + pltpu.roll(t, 16, 1) * s2


def _rope_bwd(d, c, s1, s2):
    return d * c + pltpu.roll(d * s1, 16, 1) + pltpu.roll(d * s2, HP - 16, 1)


def _lane_lt64(shape):
    return lax.broadcasted_iota(jnp.int32, shape, len(shape) - 1) < 64


def _spatial_mix(w_ref, src, dst_ref, rows):
    for c in range(rows // CHUNK):
        for p in range(PAIRS):
            blk = src[c * CHUNK:(c + 1) * CHUNK, p * HP:(p + 1) * HP]
            a = _dot(w_ref[2 * p], blk)
            b = _dot(w_ref[2 * p + 1], blk)
            dst_ref[c * CHUNK:(c + 1) * CHUNK, p * HP:(p + 1) * HP] = jnp.where(_lane_lt64(a.shape), a, b)


def _gmlp_fwd(u_pre, v_pre, zb, sg, sb, wt_ref, bsp_ref, sv_ref, rows):
    u = _gelu(u_pre)
    xh, r = _ln_stats(_gelu(v_pre))
    vln = (xh * sg + sb).astype(BF16)
    _spatial_mix(wt_ref, vln, sv_ref, rows)
    bias = bsp_ref[...]
    svb = sv_ref[...] + jnp.concatenate([bias] * (rows // CHUNK), axis=0)
    sig = _sigmoid(zb)
    return u, xh, r, vln, svb, sig


def _weight_gather(slab):
    def body(slab_ref, out_ref, send_sems, recv_sems):
        x, y, c = lax.axis_index("x"), lax.axis_index("y"), lax.axis_index("c")
        j = 2 * x + y
        out_ref[j] = slab_ref[...].astype(BF16)
        peers = [(1 - x, y), (x, 1 - y), (1 - x, 1 - y)]

        def copy(k, blk, to):
            return pltpu.make_async_remote_copy(
                src_ref=out_ref.at[blk], dst_ref=out_ref.at[blk],
                send_sem=send_sems.at[k], recv_sem=recv_sems.at[k],
                device_id=to, device_id_type=MESH)

        sends = [copy(k, j, (px, py, c)) for k, (px, py) in enumerate(peers)]
        for cp in sends:
            cp.start()
        for k, (px, py) in enumerate(peers):
            copy(k, 2 * px + py, (px, py, c)).wait_recv()
        for cp in sends:
            cp.wait_send()

    return pl.pallas_call(
        body, name="weight_gather",
        out_shape=jax.ShapeDtypeStruct((4, BIG_ROWS, 128), BF16),
        in_specs=[pl.BlockSpec(memory_space=pltpu.VMEM)],
        out_specs=pl.BlockSpec(memory_space=pltpu.VMEM),
        scratch_shapes=[pltpu.SemaphoreType.DMA((3,)), pltpu.SemaphoreType.DMA((3,))],
        compiler_params=pltpu.CompilerParams(vmem_limit_bytes=VMEM_LIMIT),
    )(slab)


def _fwd_pre(x, pos, win, wuq, wkv, pvec):
    seq = x.shape[0]
    t = TOK_TILE

    def body(x_ref, pos_ref, win_ref, wuq_ref, wkv_ref, pv_ref,
             cq_o, ckv_o, gate_o, q_o, k_o, v_o, cs_o):
        proj = _dot(x_ref[...].astype(BF16), win_ref[...])
        cq = proj[:, 0:256]
        ckv = proj[:, 256:384]
        kr = proj[:, 384:512]
        cq_o[...] = cq
        ckv_o[...] = ckv
        gate_o[...] = proj[:, 512:D_INR]

        ang = pos_ref[...].astype(F32) * pv_ref[PV_INVF:PV_INVF + 1, 0:HP]
        cos = jnp.cos(ang)
        sin = jnp.sin(ang)
        cs_o[:, 0:HP] = cos
        cs_o[:, HP:2 * HP] = sin
        s1 = sin * pv_ref[PV_M1:PV_M1 + 1, 0:HP]
        s2 = sin * pv_ref[PV_M2:PV_M2 + 1, 0:HP]

        cqh, _ = _rms_stats(cq)
        q_all = _dot((cqh * pv_ref[PV_QG:PV_QG + 1, 0:Q_LORA]).astype(BF16), wuq_ref[...])
        ckvh, _ = _rms_stats(ckv)
        kv_all = _dot((ckvh * pv_ref[PV_KVG:PV_KVG + 1, 0:KV_LORA]).astype(BF16), wkv_ref[...])
        krr = _rope_fwd(kr, cos, s1, s2)
        for h in range(HEADS):
            sl = slice(h * HP, (h + 1) * HP)
            q_o[:, sl] = _rope_fwd(q_all[:, sl], cos, s1, s2).astype(BF16)
            k_o[:, sl] = (kv_all[:, sl] + krr).astype(BF16)
        v_o[...] = kv_all[:, HEADS * HP:].astype(BF16)

    tile = lambda w: pl.BlockSpec((t, w), lambda i: (i, 0))
    full = lambda a: pl.BlockSpec(a.shape, lambda i: (0,) * a.ndim)
    outs = [(Q_LORA, F32), (KV_LORA, F32), (2048, F32), (HEADS * HP, BF16), (HEADS * HP, BF16),
            (MLA_W, BF16), (2 * HP, F32)]
    return pl.pallas_call(
        body, name="fwd_pre", grid=(seq // t,),
        in_specs=[tile(D_MODEL), tile(1), full(win), full(wuq), full(wkv), full(pvec)],
        out_specs=[tile(w) for w, _ in outs],
        out_shape=[jax.ShapeDtypeStruct((seq, w), d) for w, d in outs],
        compiler_params=pltpu.CompilerParams(dimension_semantics=("arbitrary",), vmem_limit_bytes=VMEM_LIMIT),
    )(x, pos, win, wuq, wkv, pvec)


def _attn_fwd(q, k, v):
    seq = q.shape[0]
    b = ATT_BLK
    nq = seq // b

    def body(q_ref, k_ref, v_ref, o_o, lse_o):
        i = pl.program_id(1)
        qs = [q_ref[:, a * HP:(a + 1) * HP] for a in range(2)]

        def step(j, carry, masked):
            rows = pl.ds(pl.multiple_of(j * b, b), b)
            vb = v_ref[rows, :]
            out = []
            for a in range(2):
                m, l, acc = carry[a]
                s = _dot_nt(qs[a], k_ref[rows, a * HP:(a + 1) * HP]) * SCALE
                if masked:
                    qi = lax.broadcasted_iota(jnp.int32, s.shape, 0)
                    ki = lax.broadcasted_iota(jnp.int32, s.shape, 1)
                    s = jnp.where(ki <= qi, s, -jnp.inf)
                m_new = jnp.maximum(m, jnp.max(s, axis=-1, keepdims=True))
                alpha = jnp.exp(m - m_new)
                p = jnp.exp(s - m_new)
                l = alpha * l + jnp.sum(p, axis=-1, keepdims=True)
                acc = alpha * acc + _dot(p.astype(BF16), vb)
                out.append((m_new, l, acc))
            return tuple(out)

        init = tuple((jnp.full((b, 1), -jnp.inf, F32), jnp.zeros((b, 1), F32), jnp.zeros((b, HP), F32))
                     for _ in range(2))
        carry = lax.fori_loop(0, i, lambda j, cr: step(j, cr, False), init)
        carry = step(i, carry, True)
        first = _lane_lt64((b, HP))
        (m0, l0, a0), (m1, l1, a1) = carry
        o_o[...] = jnp.where(first, a0 / l0, a1 / l1)
        lse = jnp.where(first, m0 + jnp.log(l0), m1 + jnp.log(l1))
        lse_t = lse.T
        lse_o[0:1, :] = lse_t[0:1, :]
        lse_o[1:2, :] = lse_t[64:65, :]

    return pl.pallas_call(
        body, name="attn_fwd", grid=(PAIRS, nq),
        in_specs=[pl.BlockSpec((b, 2 * HP), lambda p, i: (i, p)),
                  pl.BlockSpec((seq, 2 * HP), lambda p, i: (0, p)),
                  pl.BlockSpec((seq, HP), lambda p, i: (0, p))],
        out_specs=[pl.BlockSpec((b, HP), lambda p, i: (i, p)),
                   pl.BlockSpec((None, 2, b), lambda p, i: (p, 0, i))],
        out_shape=[jax.ShapeDtypeStruct((seq, MLA_W), F32),
                   jax.ShapeDtypeStruct((PAIRS, 2, seq), F32)],
        compiler_params=pltpu.CompilerParams(dimension_semantics=("arbitrary", "arbitrary"),
                                             vmem_limit_bytes=VMEM_LIMIT),
    )(q, k, v)


def _post(x, tgt, o, gate, wout, pvec, wt, wtt, bsp):
    seq = x.shape[0]
    t = TOK_TILE
    nt = seq // t

    def body(x_ref, tgt_ref, o_ref, gate_ref, wout_ref, pv_ref, wt_ref, wtt_ref, bsp_ref,
             dh2_o, do_o, dgate_o, gwout_o, gwsp_o, gbsp_o, vec_o, sv_ref, dvln_ref, bacc_ref):
        i = pl.program_id(0)

        @pl.when(i == 0)
        def _():
            gwout_o[...] = jnp.zeros_like(gwout_o)
            gwsp_o[...] = jnp.zeros_like(gwsp_o)
            vec_o[...] = jnp.zeros_like(vec_o)
            bacc_ref[...] = jnp.zeros_like(bacc_ref)

        za = gate_ref[:, 0:512]
        u_pre = gate_ref[:, 512:1024]
        v_pre = gate_ref[:, 1024:1536]
        zb = gate_ref[:, 1536:2048]
        sg = pv_ref[PV_SG:PV_SG + 1, 0:GW]
        sb = pv_ref[PV_SB:PV_SB + 1, 0:GW]
        lng = pv_ref[PV_LNG:PV_LNG + 1, :]
        lnb = pv_ref[PV_LNB:PV_LNB + 1, :]
        o = o_ref[...]

        sig_a = _sigmoid(za)
        silu_a = za * sig_a
        u, xh, r, vln, svb, sig_b = _gmlp_fwd(u_pre, v_pre, zb, sg, sb, wt_ref, bsp_ref, sv_ref, t)
        silu_b = zb * sig_b
        sgu = u * svb
        merged = jnp.concatenate([o * silu_a, sgu * silu_b], axis=1).astype(BF16)
        h2 = DN_ALPHA * x_ref[...] + _dot(merged, wout_ref[...])
        xh2, r2 = _ln_stats(h2)
        err = xh2 * lng + lnb - tgt_ref[...]
        d_out = err * (1.0 / D_MODEL)
        vec_o[0:1, :] += jnp.sum(d_out * xh2, axis=0, keepdims=True)
        vec_o[1:2, :] += jnp.sum(d_out, axis=0, keepdims=True)
        vec_o[4:5, :] += jnp.sum(err * err, axis=0, keepdims=True) * (0.5 / D_MODEL)

        d_h2 = _ln_bwd(d_out, lng, xh2, r2)
        dh2_o[...] = d_h2
        dh2b = d_h2.astype(BF16)
        gwout_o[...] += _dot_tn(merged, dh2b)
        d_m = _dot_nt(dh2b, wout_ref[...])
        d_oa = d_m[:, 0:512]
        d_ob = d_m[:, 512:1024]
        do_o[...] = (d_oa * silu_a).astype(BF16)
        dgate_o[:, 0:512] = (d_oa * o * (sig_a * (1.0 + za * (1.0 - sig_a)))).astype(BF16)
        dgate_o[:, 1536:2048] = (d_ob * sgu * (sig_b * (1.0 + zb * (1.0 - sig_b)))).astype(BF16)
        d_sgu = d_ob * silu_b
        dgate_o[:, 512:1024] = (d_sgu * svb * _gelu_grad(u_pre)).astype(BF16)
        d_sv = d_sgu * u
        acc = bacc_ref[...]
        for c in range(t // CHUNK):
            acc = acc + d_sv[c * CHUNK:(c + 1) * CHUNK, :]
        bacc_ref[...] = acc
        d_svb = d_sv.astype(BF16)
        for c in range(t // CHUNK):
            for p in range(PAIRS):
                blk = d_svb[c * CHUNK:(c + 1) * CHUNK, p * HP:(p + 1) * HP]
                vblk = vln[c * CHUNK:(c + 1) * CHUNK, p * HP:(p + 1) * HP]
                first = _lane_lt64(blk.shape)
                gwsp_o[2 * p] += _dot_nt(jnp.where(first, blk, jnp.zeros_like(blk)), vblk)
                gwsp_o[2 * p + 1] += _dot_nt(jnp.where(first, jnp.zeros_like(blk), blk), vblk)
        _spatial_mix(wtt_ref, d_svb, dvln_ref, t)
        d_vln = dvln_ref[...]
        vec_o[2:3, 0:GW] += jnp.sum(d_vln * xh, axis=0, keepdims=True)
        vec_o[3:4, 0:GW] += jnp.sum(d_vln, axis=0, keepdims=True)
        dgate_o[:, 1024:1536] = (_ln_bwd(d_vln, sg, xh, r) * _gelu_grad(v_pre)).astype(BF16)

        @pl.when(i == nt - 1)
        def _():
            tri = (lax.broadcasted_iota(jnp.int32, (CHUNK, CHUNK), 1)
                   <= lax.broadcasted_iota(jnp.int32, (CHUNK, CHUNK), 0))
            for h in range(HEADS):
                gwsp_o[h] = jnp.where(tri, gwsp_o[h], 0.0)
            lane = lax.broadcasted_iota(jnp.int32, (CHUNK, HP), 1)
            res = jnp.zeros((CHUNK, HP), F32)
            for h in range(HEADS):
                p, a = divmod(h, 2)
                blk = bacc_ref[:, p * HP:(p + 1) * HP]
                part = jnp.where(_lane_lt64(blk.shape) == (a == 0), blk, 0.0)
                res = jnp.where(lane == h, jnp.sum(part, axis=-1, keepdims=True), res)
            gbsp_o[...] = res

    tile = lambda w: pl.BlockSpec((t, w), lambda i: (i, 0))
    full = lambda a: pl.BlockSpec(a.shape, lambda i: (0,) * a.ndim)
    const = lambda s: pl.BlockSpec(s, lambda i: (0,) * len(s))
    return pl.pallas_call(
        body, name="post", grid=(nt,),
        in_specs=[tile(D_MODEL), tile(D_MODEL), tile(MLA_W), tile(2048), full(wout), full(pvec),
                  full(wt), full(wtt), full(bsp)],
        out_specs=[tile(D_MODEL), tile(MLA_W), tile(2048), const((D_MODEL, D_MODEL)),
                   const((HEADS, CHUNK, CHUNK)), const((CHUNK, HP)), const((8, D_MODEL))],
        out_shape=[jax.ShapeDtypeStruct((seq, D_MODEL), F32), jax.ShapeDtypeStruct((seq, MLA_W), BF16),
                   jax.ShapeDtypeStruct((seq, 2048), BF16), jax.ShapeDtypeStruct((D_MODEL, D_MODEL), F32),
                   jax.ShapeDtypeStruct((HEADS, CHUNK, CHUNK), F32), jax.ShapeDtypeStruct((CHUNK, HP), F32),
                   jax.ShapeDtypeStruct((8, D_MODEL), F32)],
        scratch_shapes=[pltpu.VMEM((t, GW), F32), pltpu.VMEM((t, GW), F32), pltpu.VMEM((CHUNK, GW), F32)],
        compiler_params=pltpu.CompilerParams(dimension_semantics=("arbitrary",), vmem_limit_bytes=VMEM_LIMIT),
    )(x, tgt, o, gate, wout, pvec, wt, wtt, bsp)


def _attn_bwd(q, k, v, do, o, lse, cs, pvec):
    seq = q.shape[0]
    b = ATT_BLK
    nq = seq // b

    def body(q_ref, k_ref, v_ref, do_ref, o_ref, lse_ref, cs_ref, pv_ref, dq_o, dk_o, dv_o, dk_acc, dv_acc):
        i = pl.program_id(1)

        @pl.when(i == 0)
        def _():
            dk_acc[...] = jnp.zeros_like(dk_acc)
            dv_acc[...] = jnp.zeros_like(dv_acc)

        first = _lane_lt64((b, HP))
        do = do_ref[...]
        zero = jnp.zeros_like(do)
        dos = [jnp.where(first, do, zero), jnp.where(first, zero, do)]
        prod_t = (do.astype(F32) * o_ref[...]).T
        deltas = [jnp.sum(prod_t[0:64, :], axis=0, keepdims=True),
                  jnp.sum(prod_t[64:128, :], axis=0, keepdims=True)]
        lses = [lse_ref[0:1, :], lse_ref[1:2, :]]
        qs = [q_ref[:, a * HP:(a + 1) * HP] for a in range(2)]

        def step(j, dqs, masked):
            rows = pl.ds(pl.multiple_of(j * b, b), b)
            vb = v_ref[rows, :]
            new_dq = []
            dvs = []
            for a in range(2):
                kb = k_ref[rows, a * HP:(a + 1) * HP]
                st = _dot_nt(kb, qs[a]) * SCALE
                pt = jnp.exp(st - lses[a])
                if masked:
                    ki = lax.broadcasted_iota(jnp.int32, st.shape, 0)
                    qi = lax.broadcasted_iota(jnp.int32, st.shape, 1)
                    pt = jnp.where(ki <= qi, pt, 0.0)
                dvs.append(_dot(pt.astype(BF16), do))
                dpt = _dot_nt(vb, dos[a])
                dst = (pt * (dpt - deltas[a]) * SCALE).astype(BF16)
                dk_acc[rows, a * HP:(a + 1) * HP] += _dot(dst, qs[a])
                new_dq.append(dqs[a] + _dot_tn(dst, kb))
            dv_acc[rows, :] += jnp.where(first, dvs[0], dvs[1])
            return tuple(new_dq)

        init = (jnp.zeros((b, HP), F32), jnp.zeros((b, HP), F32))
        dqs = lax.fori_loop(0, i, lambda j, cr: step(j, cr, False), init)
        dqs = step(i, dqs, True)
        cos = cs_ref[:, 0:HP]
        sin = cs_ref[:, HP:2 * HP]
        s1 = sin * pv_ref[PV_M1:PV_M1 + 1, 0:HP]
        s2 = sin * pv_ref[PV_M2:PV_M2 + 1, 0:HP]
        for a in range(2):
            dq_o[:, a * HP:(a + 1) * HP] = _rope_bwd(dqs[a], cos, s1, s2).astype(BF16)

        @pl.when(i == nq - 1)
        def _():
            dk_o[...] = dk_acc[...].astype(BF16)
            dv_o[...] = dv_acc[...].astype(BF16)

    return pl.pallas_call(
        body, name="attn_bwd", grid=(PAIRS, nq),
        in_specs=[pl.BlockSpec((b, 2 * HP), lambda p, i: (i, p)),
                  pl.BlockSpec((seq, 2 * HP), lambda p, i: (0, p)),
                  pl.BlockSpec((seq, HP), lambda p, i: (0, p)),
                  pl.BlockSpec((b, HP), lambda p, i: (i, p)),
                  pl.BlockSpec((b, HP), lambda p, i: (i, p)),
                  pl.BlockSpec((None, 2, b), lambda p, i: (p, 0, i)),
                  pl.BlockSpec((b, 2 * HP), lambda p, i: (i, 0)),
                  pl.BlockSpec(pvec.shape, lambda p, i: (0, 0))],
        out_specs=[pl.BlockSpec((b, 2 * HP), lambda p, i: (i, p)),
                   pl.BlockSpec((seq, 2 * HP), lambda p, i: (0, p)),
                   pl.BlockSpec((seq, HP), lambda p, i: (0, p))],
        out_shape=[jax.ShapeDtypeStruct((seq, HEADS * HP), BF16),
                   jax.ShapeDtypeStruct((seq, HEADS * HP), BF16),
                   jax.ShapeDtypeStruct((seq, MLA_W), BF16)],
        scratch_shapes=[pltpu.VMEM((seq, 2 * HP), F32), pltpu.VMEM((seq, HP), F32)],
        compiler_params=pltpu.CompilerParams(dimension_semantics=("arbitrary", "arbitrary"),
                                             vmem_limit_bytes=VMEM_LIMIT),
    )(q, k, v, do, o, lse, cs, pvec)


def _bwd_pre(x, dh2, cq, ckv, cs, dq, dk, dv, dgate, win, wuq, wkv, pvec):
    seq = x.shape[0]
    t = TOK_TILE

    def body(x_ref, dh2_ref, cq_ref, ckv_ref, cs_ref, dq_ref, dk_ref, dv_ref, dgate_ref,
             win_ref, wuq_ref, wkv_ref, pv_ref, gx_o, gwin_o, gwuq_o, gwkv_o, vec_o):
        i = pl.program_id(0)

        @pl.when(i == 0)
        def _():
            gwin_o[...] = jnp.zeros_like(gwin_o)
            gwuq_o[...] = jnp.zeros_like(gwuq_o)
            gwkv_o[...] = jnp.zeros_like(gwkv_o)
            vec_o[...] = jnp.zeros_like(vec_o)

        qg = pv_ref[PV_QG:PV_QG + 1, 0:Q_LORA]
        kvg = pv_ref[PV_KVG:PV_KVG + 1, 0:KV_LORA]
        dq = dq_ref[...]
        cqh, rq = _rms_stats(cq_ref[...])
        d_cqn = _dot_nt(dq, wuq_ref[...])
        gwuq_o[...] += _dot_tn((cqh * qg).astype(BF16), dq)
        vec_o[0:1, 0:Q_LORA] += jnp.sum(d_cqn * cqh, axis=0, keepdims=True)
        d_cq = _rms_bwd(d_cqn, qg, cqh, rq)

        dk = dk_ref[...]
        dkv = jnp.concatenate([dk, dv_ref[...]], axis=1)
        ckvh, rkv = _rms_stats(ckv_ref[...])
        d_ckvn = _dot_nt(dkv, wkv_ref[...])
        gwkv_o[...] += _dot_tn((ckvh * kvg).astype(BF16), dkv)
        vec_o[1:2, 0:KV_LORA] += jnp.sum(d_ckvn * ckvh, axis=0, keepdims=True)
        d_ckv = _rms_bwd(d_ckvn, kvg, ckvh, rkv)

        dks = dk[:, 0:HP].astype(F32)
        for h in range(1, HEADS):
            dks = dks + dk[:, h * HP:(h + 1) * HP].astype(F32)
        cos = cs_ref[:, 0:HP]
        sin = cs_ref[:, HP:2 * HP]
        d_kr = _rope_bwd(dks, cos, sin * pv_ref[PV_M1:PV_M1 + 1, 0:HP], sin * pv_ref[PV_M2:PV_M2 + 1, 0:HP])

        d_proj = jnp.concatenate([d_cq.astype(BF16), d_ckv.astype(BF16), d_kr.astype(BF16), dgate_ref[...]], axis=1)
        gwin_o[...] += _dot_tn(x_ref[...].astype(BF16), d_proj)
        gx_o[...] = DN_ALPHA * dh2_ref[...] + _dot_nt(d_proj, win_ref[...])

    tile = lambda w: pl.BlockSpec((t, w), lambda i: (i, 0))
    full = lambda a: pl.BlockSpec(a.shape, lambda i: (0,) * a.ndim)
    const = lambda s: pl.BlockSpec(s, lambda i: (0,) * len(s))
    return pl.pallas_call(
        body, name="bwd_pre", grid=(seq // t,),
        in_specs=[tile(D_MODEL), tile(D_MODEL), tile(Q_LORA), tile(KV_LORA), tile(2 * HP), tile(HEADS * HP),
                  tile(HEADS * HP), tile(MLA_W), tile(2048), full(win), full(wuq), full(wkv), full(pvec)],
        out_specs=[tile(D_MODEL), const((D_MODEL, D_INR)), const((Q_LORA, HEADS * HP)),
                   const((KV_LORA, HEADS * HP + MLA_W)), const((8, Q_LORA))],
        out_shape=[jax.ShapeDtypeStruct((seq, D_MODEL), F32), jax.ShapeDtypeStruct((D_MODEL, D_INR), F32),
                   jax.ShapeDtypeStruct((Q_LORA, HEADS * HP), F32),
                   jax.ShapeDtypeStruct((KV_LORA, HEADS * HP + MLA_W), F32),
                   jax.ShapeDtypeStruct((8, Q_LORA), F32)],
        compiler_params=pltpu.CompilerParams(dimension_semantics=("arbitrary",), vmem_limit_bytes=VMEM_LIMIT),
    )(x, dh2, cq, ckv, cs, dq, dk, dv, dgate, win, wuq, wkv, pvec)


def _grad_reduce(gbig, gsmall):
    def body(gb_ref, gs_ref, ob_ref, os_ref, r1, r1s, r2, r2s, send_sems, recv_sems):
        x, y, c = lax.axis_index("x"), lax.axis_index("y"), lax.axis_index("c")
        j = 2 * x + y
        sib = (x, y, 1 - c)
        mine = pl.ds(pl.multiple_of(c * HALF_ROWS, 8), HALF_ROWS)
        theirs = pl.ds(pl.multiple_of((1 - c) * HALF_ROWS, 8), HALF_ROWS)

        def part(chip, core):
            return pl.ds(pl.multiple_of((2 * chip + core) * PART_ROWS, 8), PART_ROWS)

        def copy(k, src, dst, to):
            return pltpu.make_async_remote_copy(
                src_ref=src, dst_ref=dst, send_sem=send_sems.at[k], recv_sem=recv_sems.at[k],
                device_id=to, device_id_type=MESH)

        l1 = [copy(blk, gb_ref.at[blk, theirs], r1.at[blk], sib) for blk in range(4)]
        l1 += [copy(4 + blk, gs_ref.at[part(blk, 1 - c)], r1s.at[blk], sib) for blk in range(4)]
        for cp in l1:
            cp.start()
        for blk in range(4):
            copy(blk, gb_ref.at[blk, mine], r1.at[blk], sib).wait_recv()
            copy(4 + blk, gs_ref.at[part(blk, c)], r1s.at[blk], sib).wait_recv()
        for blk in range(4):
            r1[blk] = gb_ref[blk, mine] + r1[blk]
            r1s[blk] = gs_ref[part(blk, c)] + r1s[blk]

        chips = [(1 - x, y), (x, 1 - y), (1 - x, 1 - y)]
        l2 = []
        for k, (px, py) in enumerate(chips):
            l2.append(copy(8 + k, r1.at[2 * px + py], r2.at[k], (px, py, c)))
            l2.append(copy(11 + k, r1s.at[2 * px + py], r2s.at[k], (px, py, c)))
        for cp in l2:
            cp.start()
        for k in range(3):
            copy(8 + k, r1.at[0], r2.at[k], sib).wait_recv()
            copy(11 + k, r1s.at[0], r2s.at[k], sib).wait_recv()
        ob_ref[mine] = ((r1[j] + r2[0]) + r2[1]) + r2[2]
        os_ref[part(j, c)] = ((r1s[j] + r2s[0]) + r2s[1]) + r2s[2]

        l3 = [copy(14, ob_ref.at[mine], ob_ref.at[mine], sib)]
        others = [(x, y, 1 - c)] + [(px, py, pc) for (px, py) in chips for pc in (c, 1 - c)]
        for k, to in enumerate(others):
            l3.append(copy(15 + k, os_ref.at[part(j, c)], os_ref.at[part(j, c)], to))
        for cp in l3:
            cp.start()
        copy(14, ob_ref.at[theirs], ob_ref.at[theirs], sib).wait_recv()
        for k, (px, py, pc) in enumerate(others):
            landed = os_ref.at[part(2 * px + py, pc)]
            copy(15 + k, landed, landed, (px, py, pc)).wait_recv()
        for cp in l1 + l2 + l3:
            cp.wait_send()

    n_sem = 22
    return pl.pallas_call(
        body, name="grad_reduce",
        out_shape=[jax.ShapeDtypeStruct((BIG_ROWS, 128), F32), jax.ShapeDtypeStruct((SMALL_ROWS, 128), F32)],
        in_specs=[pl.BlockSpec(memory_space=pltpu.VMEM), pl.BlockSpec(memory_space=pltpu.VMEM)],
        out_specs=[pl.BlockSpec(memory_space=pltpu.VMEM), pl.BlockSpec(memory_space=pltpu.VMEM)],
        scratch_shapes=[pltpu.VMEM((4, HALF_ROWS, 128), F32), pltpu.VMEM((4, PART_ROWS, 128), F32),
                        pltpu.VMEM((3, HALF_ROWS, 128), F32), pltpu.VMEM((3, PART_ROWS, 128), F32),
                        pltpu.SemaphoreType.DMA((n_sem,)), pltpu.SemaphoreType.DMA((n_sem,))],
        compiler_params=pltpu.CompilerParams(vmem_limit_bytes=VMEM_LIMIT),
    )(gbig, gsmall)


def _adamw(g, w, m, v):
    rows = g.shape[0]
    t = rows // 8

    def body(g_ref, w_ref, m_ref, v_ref, d_o, m_o, v_o):
        gr = g_ref[...]
        m_new = ADAM_B1 * m_ref[...] + (1.0 - ADAM_B1) * gr
        v_new = ADAM_B2 * v_ref[...] + (1.0 - ADAM_B2) * (gr * gr)
        m_hat = m_new / (1.0 - ADAM_B1 ** ADAM_STEP)
        v_hat = v_new / (1.0 - ADAM_B2 ** ADAM_STEP)
        d_o[...] = -ADAM_LR * (m_hat / (jnp.sqrt(v_hat) + ADAM_EPS) + ADAM_WD * w_ref[...])
        m_o[...] = m_new
        v_o[...] = v_new

    spec = pl.BlockSpec((t, 128), lambda i: (i, 0))
    return pl.pallas_call(
        body, name="adamw", grid=(8,),
        in_specs=[spec] * 4, out_specs=[spec] * 3,
        out_shape=[jax.ShapeDtypeStruct((rows, 128), F32)] * 3,
        compiler_params=pltpu.CompilerParams(dimension_semantics=("arbitrary",)),
    )(g, w, m, v)


def _rows(a):
    return a.reshape(-1, 128)


def _rows8(a):
    r = _rows(a)
    return jnp.pad(r, ((0, (-r.shape[0]) % 8), (0, 0)))


def _pack_big(w_in, w_uq, w_ukv, w_out):
    return jnp.concatenate([_rows(w_in), _rows(w_uq), _rows(w_ukv), _rows(w_out)], axis=0)


def _unpack_big(s):
    return (s[0:4928].reshape(D_MODEL, 616), s[4928:5312].reshape(Q_LORA, 192),
            s[5312:5568].reshape(KV_LORA, 256), s[5568:7616].reshape(256, D_MODEL))


def _pack_small(w_sp, b_sp, qg, kvg, sg, sb, lng, lnb, last):
    return jnp.concatenate([_rows(w_sp), _rows8(b_sp), _rows8(qg), _rows8(kvg), _rows8(sg), _rows8(sb),
                            _rows8(lng), _rows8(lnb), last], axis=0)


def _unpack_small(s):
    return (s[0:1024].reshape(HEADS, CHUNK, CHUNK), s[1024:1032].reshape(HEADS, CHUNK),
            s[1032:1034].reshape(Q_LORA), s[1040:1041].reshape(KV_LORA), s[1048:1052].reshape(GW),
            s[1056:1060].reshape(GW), s[1064:1072].reshape(D_MODEL), s[1072:1080].reshape(D_MODEL))


def kernel(x, positions, w_in, q_norm_g, w_uq, kv_norm_g, w_ukv, sgu_norm_g, sgu_norm_b, w_spatial, b_spatial, w_out, ln_g, ln_b, loss_target, m_w_in, m_q_norm_g, m_w_uq, m_kv_norm_g, m_w_ukv, m_sgu_norm_g, m_sgu_norm_b, m_w_spatial, m_b_spatial, m_w_out, m_ln_g, m_ln_b, v_w_in, v_q_norm_g, v_w_uq, v_kv_norm_g, v_w_ukv, v_sgu_norm_g, v_sgu_norm_b, v_w_spatial, v_b_spatial, v_w_out, v_ln_g, v_ln_b):
    seq = x.shape[1]
    x2 = x.reshape(seq, D_MODEL)
    tgt = loss_target.reshape(seq, D_MODEL)
    pos = positions.reshape(seq, 1)
    zeros8 = jnp.zeros((8, 128), F32)

    w_slab = _pack_big(w_in, w_uq, w_ukv, w_out)
    gathered = _weight_gather(w_slab)
    blocks = [_unpack_big(gathered[jb]) for jb in range(4)]
    w_in_f = jnp.concatenate([b_[0] for b_ in blocks], axis=1)
    w_uq_f = jnp.concatenate([b_[1] for b_ in blocks], axis=1)
    w_ukv_f = jnp.concatenate([b_[2] for b_ in blocks], axis=1)
    wout = jnp.concatenate([b_[3] for b_ in blocks], axis=0)
    zc = lambda n: jnp.zeros((D_MODEL, n), BF16)
    win = jnp.concatenate([w_in_f[:, 0:384], zc(64), w_in_f[:, 384:416], zc(32), w_in_f[:, 416:D_IN]], axis=1)
    wuq = jnp.pad(w_uq_f.reshape(Q_LORA, HEADS, NOPE + ROPE), ((0, 0), (0, 0), (0, HP - NOPE - ROPE)))
    wuq = wuq.reshape(Q_LORA, HEADS * HP)
    ukv = w_ukv_f.reshape(KV_LORA, HEADS, NOPE + VDIM)
    wk = jnp.pad(ukv[:, :, 0:NOPE], ((0, 0), (0, 0), (0, HP - NOPE))).reshape(KV_LORA, HEADS * HP)
    wkv = jnp.concatenate([wk, ukv[:, :, NOPE:].reshape(KV_LORA, MLA_W)], axis=1)

    lane = jnp.arange(HP)
    half = ROPE // 2
    inv_freq = 1.0 / (ROPE_THETA ** (jnp.arange(half, dtype=F32) / half))
    in_rope = (lane >= NOPE) & (lane < NOPE + ROPE)
    invf = jnp.where(in_rope, inv_freq[(lane - NOPE) % half], 0.0)
    m1 = jnp.where((lane >= NOPE) & (lane < NOPE + half), -1.0, 0.0)
    m2 = jnp.where((lane >= NOPE + half) & (lane < NOPE + ROPE), 1.0, 0.0)
    row = lambda a: jnp.pad(a.astype(F32), (0, D_MODEL - a.shape[0]))
    pvec = jnp.stack([row(q_norm_g), row(kv_norm_g), row(sgu_norm_g), row(sgu_norm_b), row(invf), row(m1),
                      row(m2), row(ln_g), row(ln_b)] + [jnp.zeros((D_MODEL,), F32)] * (PV_ROWS - 9))
    tri = jnp.tril(jnp.ones((CHUNK, CHUNK), dtype=bool))
    wt = jnp.where(tri[None], w_spatial, 0.0).astype(BF16)
    wtt = jnp.swapaxes(wt, 1, 2)
    bsp = jnp.repeat(b_spatial.T, VDIM, axis=1)

    cq, ckv, gate, q, k, v, cs = _fwd_pre(x2, pos, win, wuq, wkv, pvec)
    o, lse = _attn_fwd(q, k, v)
    dh2, do, dgate, g_wout, g_wsp, g_bsp, vec3 = _post(x2, tgt, o, gate, wout, pvec, wt, wtt, bsp)
    dq, dk, dv = _attn_bwd(q, k, v, do, o, lse, cs, pvec)
    gx, g_win, g_wuq, g_wkv, vec5 = _bwd_pre(x2, dh2, cq, ckv, cs, dq, dk, dv, dgate, win, wuq, wkv, pvec)

    g_win_f = jnp.concatenate([g_win[:, 0:384], g_win[:, 448:480], g_win[:, 512:D_INR]], axis=1)
    g_wuq_f = g_wuq.reshape(Q_LORA, HEADS, HP)[:, :, 0:NOPE + ROPE].reshape(Q_LORA, HEADS * (NOPE + ROPE))
    g_k = g_wkv[:, 0:HEADS * HP].reshape(KV_LORA, HEADS, HP)[:, :, 0:NOPE]
    g_v = g_wkv[:, HEADS * HP:].reshape(KV_LORA, HEADS, VDIM)
    g_wukv_f = jnp.concatenate([g_k, g_v], axis=2).reshape(KV_LORA, HEADS * (NOPE + VDIM))
    gbig = jnp.stack([_pack_big(g_win_f[:, 616 * jb:616 * (jb + 1)], g_wuq_f[:, 192 * jb:192 * (jb + 1)],
                                g_wukv_f[:, 256 * jb:256 * (jb + 1)], g_wout[256 * jb:256 * (jb + 1), :])
                      for jb in range(4)])
    loss_rows = jnp.pad(jnp.sum(vec3[4]).reshape(1, 1), ((0, 7), (0, 127)))
    gsmall = _pack_small(g_wsp, g_bsp[:, 0:HEADS].T, vec5[0], vec5[1, 0:KV_LORA], vec3[2, 0:GW], vec3[3, 0:GW],
                         vec3[0], vec3[1], loss_rows)
    rbig, rsmall = _grad_reduce(gbig, gsmall)

    g_all = jnp.concatenate([rbig, rsmall], axis=0)
    pack_all = lambda a: jnp.concatenate(
        [_pack_big(a[0], a[2], a[4], a[9]), _pack_small(a[7], a[8], a[1], a[3], a[5], a[6], a[10], a[11], zeros8)],
        axis=0)
    ws = (w_in, q_norm_g, w_uq, kv_norm_g, w_ukv, sgu_norm_g, sgu_norm_b, w_spatial, b_spatial, w_out, ln_g, ln_b)
    ms = (m_w_in, m_q_norm_g, m_w_uq, m_kv_norm_g, m_w_ukv, m_sgu_norm_g, m_sgu_norm_b, m_w_spatial, m_b_spatial,
          m_w_out, m_ln_g, m_ln_b)
    vs = (v_w_in, v_q_norm_g, v_w_uq, v_kv_norm_g, v_w_ukv, v_sgu_norm_g, v_sgu_norm_b, v_w_spatial, v_b_spatial,
          v_w_out, v_ln_g, v_ln_b)
    delta, m_new, v_new = _adamw(g_all, pack_all(ws), pack_all(ms), pack_all(vs))

    def unpack_all(s):
        b_in, b_uq, b_ukv, b_out = _unpack_big(s[0:BIG_ROWS])
        wsp, bs, qg, kvg, sg, sb, lng, lnb = _unpack_small(s[BIG_ROWS:])
        return [b_in, qg, b_uq, kvg, b_ukv, sg, sb, wsp, bs, b_out, lng, lnb]

    loss = rsmall[1080, 0]
    return (loss, gx.reshape(1, seq, D_MODEL), *unpack_all(g_all), *unpack_all(delta), *unpack_all(m_new),
            *unpack_all(v_new))
```

```python
import functools
import math

import jax
import jax.numpy as jnp
from jax import lax
from jax.experimental import pallas as pl
from jax.experimental.pallas import tpu as pltpu

F32 = jnp.float32
BF16 = jnp.bfloat16

D_MODEL = 1024
Q_LORA = 256
KV_LORA = 128
HEADS = 8
NOPE = 64
ROPE = 32
VDIM = 64
MLA_W = HEADS * VDIM
GW = 512
CHUNK = 128
HP = 128
PAIRS = HEADS // 2
D_IN = 2464
D_INR = 2560
ROPE_THETA = 10000.0
DN_ALPHA = 2.0 ** 0.25
EPS = 1e-5
SCALE = 1.0 / math.sqrt(NOPE + ROPE)
SCALE_LOG2E = SCALE * 1.4426950408889634
INV_SQRT2 = 0.7071067811865476
INV_SQRT_2PI = 0.3989422804014327

ADAM_LR = 0.001
ADAM_B1 = 0.9
ADAM_B2 = 0.999
ADAM_EPS = 1e-08
ADAM_WD = 0.01
ADAM_STEP = 10

PV_QG, PV_KVG, PV_SG, PV_SB, PV_INVF, PV_M1, PV_M2, PV_LNG, PV_LNB = range(9)
PV_ROWS = 16

BIG_ROWS = 7616
HALF_ROWS = BIG_ROWS // 2
SMALL_ROWS = 1088
PART_ROWS = SMALL_ROWS // 8
MESH = pl.DeviceIdType.MESH

TOK_TILE = 256
ATT_BLK = 512
VMEM_LIMIT = 56 * 1024 * 1024


def _dot(a, b):
    return jnp.dot(a, b, preferred_element_type=F32)


def _dot_nt(a, b):
    return lax.dot_general(a, b, (((1,), (1,)), ((), ())), preferred_element_type=F32)


def _dot_tn(a, b):
    return lax.dot_general(a, b, (((0,), (0,)), ((), ())), preferred_element_type=F32)


def _sigmoid(z):
    return 1.0 / (1.0 + jnp.exp(-z))


def _gelu(x):
    return 0.5 * x * (1.0 + lax.erf(x * INV_SQRT2))


def _gelu_grad(x):
    return 0.5 * (1.0 + lax.erf(x * INV_SQRT2)) + x * (INV_SQRT_2PI * jnp.exp(-0.5 * x * x))


def _rms_stats(x):
    r = lax.rsqrt(jnp.mean(x * x, axis=-1, keepdims=True) + EPS)
    return x * r, r


def _rms_bwd(dy, g, xh, r):
    dyg = dy * g
    return r * (dyg - xh * jnp.mean(dyg * xh, axis=-1, keepdims=True))


def _ln_stats(x):
    mu = jnp.mean(x, axis=-1, keepdims=True)
    xc = x - mu
    r = lax.rsqrt(jnp.mean(xc * xc, axis=-1, keepdims=True) + EPS)
    return xc * r, r


def _ln_bwd(dy, g, xh, r):
    dxh = dy * g
    return r * (dxh - jnp.mean(dxh, axis=-1, keepdims=True) - xh * jnp.mean(dxh * xh, axis=-1, keepdims=True))


def _rope_fwd(t, c, s1, s2):
    return t * c + pltpu.roll(t, HP - 16, 1) * s1 + pltpu.roll(t, 16, 1) * s2


def _rope_bwd(d, c, s1, s2):
    return d * c + pltpu.roll(d * s1, 16, 1) + pltpu.roll(d * s2, HP - 16, 1)


def _lane_lt64(shape):
    return lax.broadcasted_iota(jnp.int32, shape, len(shape) - 1) < 64


def _spatial_mix(w_ref, src, dst_ref, rows):
    for c in range(rows // CHUNK):
        for p in range(PAIRS):
            blk = src[c * CHUNK:(c + 1) * CHUNK, p * HP:(p + 1) * HP]
            a = _dot(w_ref[2 * p], blk)
            b = _dot(w_ref[2 * p + 1], blk)
            dst_ref[c * CHUNK:(c + 1) * CHUNK, p * HP:(p + 1) * HP] = jnp.where(_lane_lt64(a.shape), a, b)


def _gmlp_fwd(u_pre, v_pre, zb, sg, sb, wt_ref, bsp_ref, sv_ref, rows):
    u = _gelu(u_pre)
    xh, r = _ln_stats(_gelu(v_pre))
    vln = (xh * sg + sb).astype(BF16)
    _spatial_mix(wt_ref, vln, sv_ref, rows)
    bias = bsp_ref[...]
    svb = sv_ref[...] + jnp.concatenate([bias] * (rows // CHUNK), axis=0)
    sig = _sigmoid(zb)
    return u, xh, r, vln, svb, sig


def _weight_gather(slab):
    def body(slab_ref, out_ref, send_sems, recv_sems):
        x, y, c = lax.axis_index("x"), lax.axis_index("y"), lax.axis_index("c")
        j = 2 * x + y
        out_ref[j] = slab_ref[...].astype(BF16)
        peers = [(1 - x, y), (x, 1 - y), (1 - x, 1 - y)]

        def copy(k, blk, to):
            return pltpu.make_async_remote_copy(
                src_ref=out_ref.at[blk], dst_ref=out_ref.at[blk],
                send_sem=send_sems.at[k], recv_sem=recv_sems.at[k],
                device_id=to, device_id_type=MESH)

        sends = [copy(k, j, (px, py, c)) for k, (px, py) in enumerate(peers)]
        for cp in sends:
            cp.start()
        for k, (px, py) in enumerate(peers):
            copy(k, 2 * px + py, (px, py, c)).wait_recv()
        for cp in sends:
            cp.wait_send()

    return pl.pallas_call(
        body, name="weight_gather",
        out_shape=jax.ShapeDtypeStruct((4, BIG_ROWS, 128), BF16),
        in_specs=[pl.BlockSpec(memory_space=pltpu.VMEM)],
        out_specs=pl.BlockSpec(memory_space=pltpu.VMEM),
        scratch_shapes=[pltpu.SemaphoreType.DMA((3,)), pltpu.SemaphoreType.DMA((3,))],
        compiler_params=pltpu.CompilerParams(vmem_limit_bytes=VMEM_LIMIT),
    )(slab)


def _fwd_pre(x, pos, win, wuq, wkv, pvec):
    seq = x.shape[0]
    t = TOK_TILE

    def body(x_ref, pos_ref, win_ref, wuq_ref, wkv_ref, pv_ref,
             cq_o, ckv_o, gate_o, q_o, k_o, v_o, cs_o):
        proj = _dot(x_ref[...].astype(BF16), win_ref[...])
        cq = proj[:, 0:256]
        ckv = proj[:, 256:384]
        kr = proj[:, 384:512]
        cq_o[...] = cq
        ckv_o[...] = ckv
        gate_o[...] = proj[:, 512:D_INR]

        ang = pos_ref[...].astype(F32) * pv_ref[PV_INVF:PV_INVF + 1, 0:HP]
        cos = jnp.cos(ang)
        sin = jnp.sin(ang)
        cs_o[:, 0:HP] = cos
        cs_o[:, HP:2 * HP] = sin
        s1 = sin * pv_ref[PV_M1:PV_M1 + 1, 0:HP]
        s2 = sin * pv_ref[PV_M2:PV_M2 + 1, 0:HP]

        cqh, _ = _rms_stats(cq)
        q_all = _dot((cqh * pv_ref[PV_QG:PV_QG + 1, 0:Q_LORA]).astype(BF16), wuq_ref[...])
        ckvh, _ = _rms_stats(ckv)
        kv_all = _dot((ckvh * pv_ref[PV_KVG:PV_KVG + 1, 0:KV_LORA]).astype(BF16), wkv_ref[...])
        krr = _rope_fwd(kr, cos, s1, s2)
        for h in range(HEADS):
            sl = slice(h * HP, (h + 1) * HP)
            q_o[:, sl] = _rope_fwd(q_all[:, sl], cos, s1, s2).astype(BF16)
            k_o[:, sl] = (kv_all[:, sl] + krr).astype(BF16)
        v_o[...] = kv_all[:, HEADS * HP:].astype(BF16)

    tile = lambda w: pl.BlockSpec((t, w), lambda i: (i, 0))
    full = lambda a: pl.BlockSpec(a.shape, lambda i: (0,) * a.ndim)
    outs = [(Q_LORA, F32), (KV_LORA, F32), (2048, F32), (HEADS * HP, BF16), (HEADS * HP, BF16),
            (MLA_W, BF16), (2 * HP, F32)]
    return pl.pallas_call(
        body, name="fwd_pre", grid=(seq // t,),
        in_specs=[tile(D_MODEL), tile(1), full(win), full(wuq), full(wkv), full(pvec)],
        out_specs=[tile(w) for w, _ in outs],
        out_shape=[jax.ShapeDtypeStruct((seq, w), d) for w, d in outs],
        compiler_params=pltpu.CompilerParams(dimension_semantics=("arbitrary",), vmem_limit_bytes=VMEM_LIMIT),
    )(x, pos, win, wuq, wkv, pvec)


def _attn_fwd(q, k, v):
    seq = q.shape[0]
    b = ATT_BLK
    nq = seq // b

    def body(q_ref, k_ref, v_ref, o_o, lse_o, m_ref, l_ref, acc_ref):
        i = pl.program_id(1)
        m_ref[...] = jnp.full(m_ref.shape, -jnp.inf, F32)
        l_ref[...] = jnp.zeros(l_ref.shape, F32)
        acc_ref[...] = jnp.zeros(acc_ref.shape, F32)

        def step(j, masked):
            rows = pl.ds(pl.multiple_of(j * b, b), b)
            raw = [_dot_nt(q_ref[:, a * HP:(a + 1) * HP], k_ref[rows, a * HP:(a + 1) * HP]) for a in range(2)]
            vb = v_ref[rows, :]
            for a in range(2):
                s = raw[a] * SCALE_LOG2E
                if masked:
                    qi = lax.broadcasted_iota(jnp.int32, s.shape, 0)
                    ki = lax.broadcasted_iota(jnp.int32, s.shape, 1)
                    s = jnp.where(ki <= qi, s, -jnp.inf)
                m_prev = m_ref[a]
                m_new = jnp.maximum(m_prev, jnp.max(s, axis=-1, keepdims=True))
                alpha = jnp.exp2(m_prev - m_new)
                p = jnp.exp2(s - jnp.tile(m_new, (1, b // HP)))
                l_ref[a] = alpha * l_ref[a] + jnp.sum(p, axis=-1, keepdims=True)
                acc_ref[a] = alpha * acc_ref[a] + _dot(p.astype(BF16), vb)
                m_ref[a] = m_new

        def loop_body(j, carry):
            step(j, False)
            return carry

        lax.fori_loop(0, i, loop_body, 0)
        step(i, True)
        first = _lane_lt64((b, HP))
        o_o[...] = jnp.where(first, acc_ref[0] / l_ref[0], acc_ref[1] / l_ref[1])
        lse_t = jnp.where(first, m_ref[0] + jnp.log2(l_ref[0]), m_ref[1] + jnp.log2(l_ref[1])).T
        lse_o[0:1, :] = lse_t[0:1, :]
        lse_o[1:2, :] = lse_t[64:65, :]

    return pl.pallas_call(
        body, name="attn_fwd", grid=(PAIRS, nq),
        in_specs=[pl.BlockSpec((b, 2 * HP), lambda p, i: (i, p)),
                  pl.BlockSpec((seq, 2 * HP), lambda p, i: (0, p)),
                  pl.BlockSpec((seq, HP), lambda p, i: (0, p))],
        out_specs=[pl.BlockSpec((b, HP), lambda p, i: (i, p)),
                   pl.BlockSpec((None, 2, b), lambda p, i: (p, 0, i))],
        out_shape=[jax.ShapeDtypeStruct((seq, MLA_W), F32),
                   jax.ShapeDtypeStruct((PAIRS, 2, seq), F32)],
        scratch_shapes=[pltpu.VMEM((2, b, HP), F32)] * 3,
        compiler_params=pltpu.CompilerParams(dimension_semantics=("arbitrary", "arbitrary"),
                                             vmem_limit_bytes=VMEM_LIMIT),
    )(q, k, v)


def _post(x, tgt, o, gate, wout, pvec, wt, wtt, bsp):
    seq = x.shape[0]
    t = TOK_TILE
    nt = seq // t

    def body(x_ref, tgt_ref, o_ref, gate_ref, wout_ref, pv_ref, wt_ref, wtt_ref, bsp_ref,
             dh2_o, do_o, dgate_o, gwout_o, gwsp_o, gbsp_o, vec_o, sv_ref, dvln_ref, bacc_ref):
        i = pl.program_id(0)

        @pl.when(i == 0)
        def _():
            gwout_o[...] = jnp.zeros_like(gwout_o)
            gwsp_o[...] = jnp.zeros_like(gwsp_o)
            vec_o[...] = jnp.zeros_like(vec_o)
            bacc_ref[...] = jnp.zeros_like(bacc_ref)

        za = gate_ref[:, 0:512]
        u_pre = gate_ref[:, 512:1024]
        v_pre = gate_ref[:, 1024:1536]
        zb = gate_ref[:, 1536:2048]
        sg = pv_ref[PV_SG:PV_SG + 1, 0:GW]
        sb = pv_ref[PV_SB:PV_SB + 1, 0:GW]
        lng = pv_ref[PV_LNG:PV_LNG + 1, :]
        lnb = pv_ref[PV_LNB:PV_LNB + 1, :]
        o = o_ref[...]

        sig_a = _sigmoid(za)
        silu_a = za * sig_a
        u, xh, r, vln, svb, sig_b = _gmlp_fwd(u_pre, v_pre, zb, sg, sb, wt_ref, bsp_ref, sv_ref, t)
        silu_b = zb * sig_b
        sgu = u * svb
        merged = jnp.concatenate([o * silu_a, sgu * silu_b], axis=1).astype(BF16)
        h2 = DN_ALPHA * x_ref[...] + _dot(merged, wout_ref[...])
        xh2, r2 = _ln_stats(h2)
        err = xh2 * lng + lnb - tgt_ref[...]
        d_out = err * (1.0 / D_MODEL)
        vec_o[0:1, :] += jnp.sum(d_out * xh2, axis=0, keepdims=True)
        vec_o[1:2, :] += jnp.sum(d_out, axis=0, keepdims=True)
        vec_o[4:5, :] += jnp.sum(err * err, axis=0, keepdims=True) * (0.5 / D_MODEL)

        d_h2 = _ln_bwd(d_out, lng, xh2, r2)
        dh2_o[...] = d_h2
        dh2b = d_h2.astype(BF16)
        gwout_o[...] += _dot_tn(merged, dh2b)
        d_m = _dot_nt(dh2b, wout_ref[...])
        d_oa = d_m[:, 0:512]
        d_ob = d_m[:, 512:1024]
        do_o[...] = (d_oa * silu_a).astype(BF16)
        dgate_o[:, 0:512] = (d_oa * o * (sig_a * (1.0 + za * (1.0 - sig_a)))).astype(BF16)
        dgate_o[:, 1536:2048] = (d_ob * sgu * (sig_b * (1.0 + zb * (1.0 - sig_b)))).astype(BF16)
        d_sgu = d_ob * silu_b
        dgate_o[:, 512:1024] = (d_sgu * svb * _gelu_grad(u_pre)).astype(BF16)
        d_sv = d_sgu * u
        acc = bacc_ref[...]
        for c in range(t // CHUNK):
            acc = acc + d_sv[c * CHUNK:(c + 1) * CHUNK, :]
        bacc_ref[...] = acc
        d_svb = d_sv.astype(BF16)
        for c in range(t // CHUNK):
            for p in range(PAIRS):
                blk = d_svb[c * CHUNK:(c + 1) * CHUNK, p * HP:(p + 1) * HP]
                vblk = vln[c * CHUNK:(c + 1) * CHUNK, p * HP:(p + 1) * HP]
                first = _lane_lt64(blk.shape)
                gwsp_o[2 * p] += _dot_nt(jnp.where(first, blk, jnp.zeros_like(blk)), vblk)
                gwsp_o[2 * p + 1] += _dot_nt(jnp.where(first, jnp.zeros_like(blk), blk), vblk)
        _spatial_mix(wtt_ref, d_svb, dvln_ref, t)
        d_vln = dvln_ref[...]
        vec_o[2:3, 0:GW] += jnp.sum(d_vln * xh, axis=0, keepdims=True)
        vec_o[3:4, 0:GW] += jnp.sum(d_vln, axis=0, keepdims=True)
        dgate_o[:, 1024:1536] = (_ln_bwd(d_vln, sg, xh, r) * _gelu_grad(v_pre)).astype(BF16)

        @pl.when(i == nt - 1)
        def _():
            tri = (lax.broadcasted_iota(jnp.int32, (CHUNK, CHUNK), 1)
                   <= lax.broadcasted_iota(jnp.int32, (CHUNK, CHUNK), 0))
            for h in range(HEADS):
                gwsp_o[h] = jnp.where(tri, gwsp_o[h], 0.0)
            lane = lax.broadcasted_iota(jnp.int32, (CHUNK, HP), 1)
            res = jnp.zeros((CHUNK, HP), F32)
            for h in range(HEADS):
                p, a = divmod(h, 2)
                blk = bacc_ref[:, p * HP:(p + 1) * HP]
                part = jnp.where(_lane_lt64(blk.shape) == (a == 0), blk, 0.0)
                res = jnp.where(lane == h, jnp.sum(part, axis=-1, keepdims=True), res)
            gbsp_o[...] = res

    tile = lambda w: pl.BlockSpec((t, w), lambda i: (i, 0))
    full = lambda a: pl.BlockSpec(a.shape, lambda i: (0,) * a.ndim)
    const = lambda s: pl.BlockSpec(s, lambda i: (0,) * len(s))
    return pl.pallas_call(
        body, name="post", grid=(nt,),
        in_specs=[tile(D_MODEL), tile(D_MODEL), tile(MLA_W), tile(2048), full(wout), full(pvec),
                  full(wt), full(wtt), full(bsp)],
        out_specs=[tile(D_MODEL), tile(MLA_W), tile(2048), const((D_MODEL, D_MODEL)),
                   const((HEADS, CHUNK, CHUNK)), const((CHUNK, HP)), const((8, D_MODEL))],
        out_shape=[jax.ShapeDtypeStruct((seq, D_MODEL), F32), jax.ShapeDtypeStruct((seq, MLA_W), BF16),
                   jax.ShapeDtypeStruct((seq, 2048), BF16), jax.ShapeDtypeStruct((D_MODEL, D_MODEL), F32),
                   jax.ShapeDtypeStruct((HEADS, CHUNK, CHUNK), F32), jax.ShapeDtypeStruct((CHUNK, HP), F32),
                   jax.ShapeDtypeStruct((8, D_MODEL), F32)],
        scratch_shapes=[pltpu.VMEM((t, GW), F32), pltpu.VMEM((t, GW), F32), pltpu.VMEM((CHUNK, GW), F32)],
        compiler_params=pltpu.CompilerParams(dimension_semantics=("arbitrary",), vmem_limit_bytes=VMEM_LIMIT),
    )(x, tgt, o, gate, wout, pvec, wt, wtt, bsp)


def _attn_bwd(q, k, v, do, o, lse, cs, pvec):
    seq = q.shape[0]
    b = ATT_BLK
    nq = seq // b

    def body(q_ref, k_ref, v_ref, do_ref, o_ref, lse_ref, cs_ref, pv_ref, dq_o, dk_o, dv_o, dk_acc, dv_acc):
        i = pl.program_id(1)

        @pl.when(i == 0)
        def _():
            dk_acc[...] = jnp.zeros_like(dk_acc)
            dv_acc[...] = jnp.zeros_like(dv_acc)

        first = _lane_lt64((b, HP))
        do = do_ref[...]
        zero = jnp.zeros_like(do)
        dos = [jnp.where(first, do, zero), jnp.where(first, zero, do)]
        prod_t = (do.astype(F32) * o_ref[...]).T
        deltas = [jnp.sum(prod_t[0:64, :], axis=0, keepdims=True),
                  jnp.sum(prod_t[64:128, :], axis=0, keepdims=True)]
        lses = [lse_ref[0:1, :], lse_ref[1:2, :]]
        qs = [q_ref[:, a * HP:(a + 1) * HP] for a in range(2)]

        def step(j, dqs, masked):
            rows = pl.ds(pl.multiple_of(j * b, b), b)
            vb = v_ref[rows, :]
            new_dq = []
            dvs = []
            for a in range(2):
                kb = k_ref[rows, a * HP:(a + 1) * HP]
                st = _dot_nt(kb, qs[a]) * SCALE_LOG2E
                pt = jnp.exp2(st - lses[a])
                if masked:
                    ki = lax.broadcasted_iota(jnp.int32, st.shape, 0)
                    qi = lax.broadcasted_iota(jnp.int32, st.shape, 1)
                    pt = jnp.where(ki <= qi, pt, 0.0)
                dvs.append(_dot(pt.astype(BF16), do))
                dpt = _dot_nt(vb, dos[a])
                dst = (pt * (dpt - deltas[a]) * SCALE).astype(BF16)
                dk_acc[rows, a * HP:(a + 1) * HP] += _dot(dst, qs[a])
                new_dq.append(dqs[a] + _dot_tn(dst, kb))
            dv_acc[rows, :] += jnp.where(first, dvs[0], dvs[1])
            return tuple(new_dq)

        init = (jnp.zeros((b, HP), F32), jnp.zeros((b, HP), F32))
        dqs = lax.fori_loop(0, i, lambda j, cr: step(j, cr, False), init)
        dqs = step(i, dqs, True)
        cos = cs_ref[:, 0:HP]
        sin = cs_ref[:, HP:2 * HP]
        s1 = sin * pv_ref[PV_M1:PV_M1 + 1, 0:HP]
        s2 = sin * pv_ref[PV_M2:PV_M2 + 1, 0:HP]
        for a in range(2):
            dq_o[:, a * HP:(a + 1) * HP] = _rope_bwd(dqs[a], cos, s1, s2).astype(BF16)

        @pl.when(i == nq - 1)
        def _():
            dk_o[...] = dk_acc[...].astype(BF16)
            dv_o[...] = dv_acc[...].astype(BF16)

    return pl.pallas_call(
        body, name="attn_bwd", grid=(PAIRS, nq),
        in_specs=[pl.BlockSpec((b, 2 * HP), lambda p, i: (i, p)),
                  pl.BlockSpec((seq, 2 * HP), lambda p, i: (0, p)),
                  pl.BlockSpec((seq, HP), lambda p, i: (0, p)),
                  pl.BlockSpec((b, HP), lambda p, i: (i, p)),
                  pl.BlockSpec((b, HP), lambda p, i: (i, p)),
                  pl.BlockSpec((None, 2, b), lambda p, i: (p, 0, i)),
                  pl.BlockSpec((b, 2 * HP), lambda p, i: (i, 0)),
                  pl.BlockSpec(pvec.shape, lambda p, i: (0, 0))],
        out_specs=[pl.BlockSpec((b, 2 * HP), lambda p, i: (i, p)),
                   pl.BlockSpec((seq, 2 * HP), lambda p, i: (0, p)),
                   pl.BlockSpec((seq, HP), lambda p, i: (0, p))],
        out_shape=[jax.ShapeDtypeStruct((seq, HEADS * HP), BF16),
                   jax.ShapeDtypeStruct((seq, HEADS * HP), BF16),
                   jax.ShapeDtypeStruct((seq, MLA_W), BF16)],
        scratch_shapes=[pltpu.VMEM((seq, 2 * HP), F32), pltpu.VMEM((seq, HP), F32)],
        compiler_params=pltpu.CompilerParams(dimension_semantics=("arbitrary", "arbitrary"),
                                             vmem_limit_bytes=VMEM_LIMIT),
    )(q, k, v, do, o, lse, cs, pvec)


def _bwd_pre(x, dh2, cq, ckv, cs, dq, dk, dv, dgate, win, wuq, wkv, pvec):
    seq = x.shape[0]
    t = TOK_TILE

    def body(x_ref, dh2_ref, cq_ref, ckv_ref, cs_ref, dq_ref, dk_ref, dv_ref, dgate_ref,
             win_ref, wuq_ref, wkv_ref, pv_ref, gx_o, gwin_o, gwuq_o, gwkv_o, vec_o):
        i = pl.program_id(0)

        @pl.when(i == 0)
        def _():
            gwin_o[...] = jnp.zeros_like(gwin_o)
            gwuq_o[...] = jnp.zeros_like(gwuq_o)
            gwkv_o[...] = jnp.zeros_like(gwkv_o)
            vec_o[...] = jnp.zeros_like(vec_o)

        qg = pv_ref[PV_QG:PV_QG + 1, 0:Q_LORA]
        kvg = pv_ref[PV_KVG:PV_KVG + 1, 0:KV_LORA]
        dq = dq_ref[...]
        cqh, rq = _rms_stats(cq_ref[...])
        d_cqn = _dot_nt(dq, wuq_ref[...])
        gwuq_o[...] += _dot_tn((cqh * qg).astype(BF16), dq)
        vec_o[0:1, 0:Q_LORA] += jnp.sum(d_cqn * cqh, axis=0, keepdims=True)
        d_cq = _rms_bwd(d_cqn, qg, cqh, rq)

        dk = dk_ref[...]
        dkv = jnp.concatenate([dk, dv_ref[...]], axis=1)
        ckvh, rkv = _rms_stats(ckv_ref[...])
        d_ckvn = _dot_nt(dkv, wkv_ref[...])
        gwkv_o[...] += _dot_tn((ckvh * kvg).astype(BF16), dkv)
        vec_o[1:2, 0:KV_LORA] += jnp.sum(d_ckvn * ckvh, axis=0, keepdims=True)
        d_ckv = _rms_bwd(d_ckvn, kvg, ckvh, rkv)

        dks = dk[:, 0:HP].astype(F32)
        for h in range(1, HEADS):
            dks = dks + dk[:, h * HP:(h + 1) * HP].astype(F32)
        cos = cs_ref[:, 0:HP]
        sin = cs_ref[:, HP:2 * HP]
        d_kr = _rope_bwd(dks, cos, sin * pv_ref[PV_M1:PV_M1 + 1, 0:HP], sin * pv_ref[PV_M2:PV_M2 + 1, 0:HP])

        d_proj = jnp.concatenate([d_cq.astype(BF16), d_ckv.astype(BF16), d_kr.astype(BF16), dgate_ref[...]], axis=1)
        gwin_o[...] += _dot_tn(x_ref[...].astype(BF16), d_proj)
        gx_o[...] = DN_ALPHA * dh2_ref[...] + _dot_nt(d_proj, win_ref[...])

    tile = lambda w: pl.BlockSpec((t, w), lambda i: (i, 0))
    full = lambda a: pl.BlockSpec(a.shape, lambda i: (0,) * a.ndim)
    const = lambda s: pl.BlockSpec(s, lambda i: (0,) * len(s))
    return pl.pallas_call(
        body, name="bwd_pre", grid=(seq // t,),
        in_specs=[tile(D_MODEL), tile(D_MODEL), tile(Q_LORA), tile(KV_LORA), tile(2 * HP), tile(HEADS * HP),
                  tile(HEADS * HP), tile(MLA_W), tile(2048), full(win), full(wuq), full(wkv), full(pvec)],
        out_specs=[tile(D_MODEL), const((D_MODEL, D_INR)), const((Q_LORA, HEADS * HP)),
                   const((KV_LORA, HEADS * HP + MLA_W)), const((8, Q_LORA))],
        out_shape=[jax.ShapeDtypeStruct((seq, D_MODEL), F32), jax.ShapeDtypeStruct((D_MODEL, D_INR), F32),
                   jax.ShapeDtypeStruct((Q_LORA, HEADS * HP), F32),
                   jax.ShapeDtypeStruct((KV_LORA, HEADS * HP + MLA_W), F32),
                   jax.ShapeDtypeStruct((8, Q_LORA), F32)],
        compiler_params=pltpu.CompilerParams(dimension_semantics=("arbitrary",), vmem_limit_bytes=VMEM_LIMIT),
    )(x, dh2, cq, ckv, cs, dq, dk, dv, dgate, win, wuq, wkv, pvec)


def _grad_reduce(gbig, gsmall):
    def body(gb_ref, gs_ref, ob_ref, os_ref, r1, r1s, r2, r2s, send_sems, recv_sems):
        x, y, c = lax.axis_index("x"), lax.axis_index("y"), lax.axis_index("c")
        j = 2 * x + y
        sib = (x, y, 1 - c)
        mine = pl.ds(pl.multiple_of(c * HALF_ROWS, 8), HALF_ROWS)
        theirs = pl.ds(pl.multiple_of((1 - c) * HALF_ROWS, 8), HALF_ROWS)

        def part(chip, core):
            return pl.ds(pl.multiple_of((2 * chip + core) * PART_ROWS, 8), PART_ROWS)

        def copy(k, src, dst, to):
            return pltpu.make_async_remote_copy(
                src_ref=src, dst_ref=dst, send_sem=send_sems.at[k], recv_sem=recv_sems.at[k],
                device_id=to, device_id_type=MESH)

        l1 = [copy(blk, gb_ref.at[blk, theirs], r1.at[blk], sib) for blk in range(4)]
        l1 += [copy(4 + blk, gs_ref.at[part(blk, 1 - c)], r1s.at[blk], sib) for blk in range(4)]
        for cp in l1:
            cp.start()
        for blk in range(4):
            copy(blk, gb_ref.at[blk, mine], r1.at[blk], sib).wait_recv()
            copy(4 + blk, gs_ref.at[part(blk, c)], r1s.at[blk], sib).wait_recv()
        for blk in range(4):
            r1[blk] = gb_ref[blk, mine] + r1[blk]
            r1s[blk] = gs_ref[part(blk, c)] + r1s[blk]

        chips = [(1 - x, y), (x, 1 - y), (1 - x, 1 - y)]
        l2 = []
        for k, (px, py) in enumerate(chips):
            l2.append(copy(8 + k, r1.at[2 * px + py], r2.at[k], (px, py, c)))
            l2.append(copy(11 + k, r1s.at[2 * px + py], r2s.at[k], (px, py, c)))
        for cp in l2:
            cp.start()
        for k in range(3):
            copy(8 + k, r1.at[0], r2.at[k], sib).wait_recv()
            copy(11 + k, r1s.at[0], r2s.at[k], sib).wait_recv()
        ob_ref[mine] = ((r1[j] + r2[0]) + r2[1]) + r2[2]
        os_ref[part(j, c)] = ((r1s[j] + r2s[0]) + r2s[1]) + r2s[2]

        l3 = [copy(14, ob_ref.at[mine], ob_ref.at[mine], sib)]
        others = [(x, y, 1 - c)] + [(px, py, pc) for (px, py) in chips for pc in (c, 1 - c)]
        for k, to in enumerate(others):
            l3.append(copy(15 + k, os_ref.at[part(j, c)], os_ref.at[part(j, c)], to))
        for cp in l3:
            cp.start()
        copy(14, ob_ref.at[theirs], ob_ref.at[theirs], sib).wait_recv()
        for k, (px, py, pc) in enumerate(others):
            landed = os_ref.at[part(2 * px + py, pc)]
            copy(15 + k, landed, landed, (px, py, pc)).wait_recv()
        for cp in l1 + l2 + l3:
            cp.wait_send()

    n_sem = 22
    return pl.pallas_call(
        body, name="grad_reduce",
        out_shape=[jax.ShapeDtypeStruct((BIG_ROWS, 128), F32), jax.ShapeDtypeStruct((SMALL_ROWS, 128), F32)],
        in_specs=[pl.BlockSpec(memory_space=pltpu.VMEM), pl.BlockSpec(memory_space=pltpu.VMEM)],
        out_specs=[pl.BlockSpec(memory_space=pltpu.VMEM), pl.BlockSpec(memory_space=pltpu.VMEM)],
        scratch_shapes=[pltpu.VMEM((4, HALF_ROWS, 128), F32), pltpu.VMEM((4, PART_ROWS, 128), F32),
                        pltpu.VMEM((3, HALF_ROWS, 128), F32), pltpu.VMEM((3, PART_ROWS, 128), F32),
                        pltpu.SemaphoreType.DMA((n_sem,)), pltpu.SemaphoreType.DMA((n_sem,))],
        compiler_params=pltpu.CompilerParams(vmem_limit_bytes=VMEM_LIMIT),
    )(gbig, gsmall)


def _adamw(g, w, m, v):
    rows = g.shape[0]
    t = rows // 8

    def body(g_ref, w_ref, m_ref, v_ref, d_o, m_o, v_o):
        gr = g_ref[...]
        m_new = ADAM_B1 * m_ref[...] + (1.0 - ADAM_B1) * gr
        v_new = ADAM_B2 * v_ref[...] + (1.0 - ADAM_B2) * (gr * gr)
        m_hat = m_new / (1.0 - ADAM_B1 ** ADAM_STEP)
        v_hat = v_new / (1.0 - ADAM_B2 ** ADAM_STEP)
        d_o[...] = -ADAM_LR * (m_hat / (jnp.sqrt(v_hat) + ADAM_EPS) + ADAM_WD * w_ref[...])
        m_o[...] = m_new
        v_o[...] = v_new

    spec = pl.BlockSpec((t, 128), lambda i: (i, 0))
    return pl.pallas_call(
        body, name="adamw", grid=(8,),
        in_specs=[spec] * 4, out_specs=[spec] * 3,
        out_shape=[jax.ShapeDtypeStruct((rows, 128), F32)] * 3,
        compiler_params=pltpu.CompilerParams(dimension_semantics=("arbitrary",)),
    )(g, w, m, v)


def _rows(a):
    return a.reshape(-1, 128)


def _rows8(a):
    r = _rows(a)
    return jnp.pad(r, ((0, (-r.shape[0]) % 8), (0, 0)))


def _pack_big(w_in, w_uq, w_ukv, w_out):
    return jnp.concatenate([_rows(w_in), _rows(w_uq), _rows(w_ukv), _rows(w_out)], axis=0)


def _unpack_big(s):
    return (s[0:4928].reshape(D_MODEL, 616), s[4928:5312].reshape(Q_LORA, 192),
            s[5312:5568].reshape(KV_LORA, 256), s[5568:7616].reshape(256, D_MODEL))


def _pack_small(w_sp, b_sp, qg, kvg, sg, sb, lng, lnb, last):
    return jnp.concatenate([_rows(w_sp), _rows8(b_sp), _rows8(qg), _rows8(kvg), _rows8(sg), _rows8(sb),
                            _rows8(lng), _rows8(lnb), last], axis=0)


def _unpack_small(s):
    return (s[0:1024].reshape(HEADS, CHUNK, CHUNK), s[1024:1032].reshape(HEADS, CHUNK),
            s[1032:1034].reshape(Q_LORA), s[1040:1041].reshape(KV_LORA), s[1048:1052].reshape(GW),
            s[1056:1060].reshape(GW), s[1064:1072].reshape(D_MODEL), s[1072:1080].reshape(D_MODEL))


def kernel(x, positions, w_in, q_norm_g, w_uq, kv_norm_g, w_ukv, sgu_norm_g, sgu_norm_b, w_spatial, b_spatial, w_out, ln_g, ln_b, loss_target, m_w_in, m_q_norm_g, m_w_uq, m_kv_norm_g, m_w_ukv, m_sgu_norm_g, m_sgu_norm_b, m_w_spatial, m_b_spatial, m_w_out, m_ln_g, m_ln_b, v_w_in, v_q_norm_g, v_w_uq, v_kv_norm_g, v_w_ukv, v_sgu_norm_g, v_sgu_norm_b, v_w_spatial, v_b_spatial, v_w_out, v_ln_g, v_ln_b):
    seq = x.shape[1]
    x2 = x.reshape(seq, D_MODEL)
    tgt = loss_target.reshape(seq, D_MODEL)
    pos = positions.reshape(seq, 1)
    zeros8 = jnp.zeros((8, 128), F32)

    w_slab = _pack_big(w_in, w_uq, w_ukv, w_out)
    gathered = _weight_gather(w_slab)
    blocks = [_unpack_big(gathered[jb]) for jb in range(4)]
    w_in_f = jnp.concatenate([b_[0] for b_ in blocks], axis=1)
    w_uq_f = jnp.concatenate([b_[1] for b_ in blocks], axis=1)
    w_ukv_f = jnp.concatenate([b_[2] for b_ in blocks], axis=1)
    wout = jnp.concatenate([b_[3] for b_ in blocks], axis=0)
    zc = lambda n: jnp.zeros((D_MODEL, n), BF16)
    win = jnp.concatenate([w_in_f[:, 0:384], zc(64), w_in_f[:, 384:416], zc(32), w_in_f[:, 416:D_IN]], axis=1)
    wuq = jnp.pad(w_uq_f.reshape(Q_LORA, HEADS, NOPE + ROPE), ((0, 0), (0, 0), (0, HP - NOPE - ROPE)))
    wuq = wuq.reshape(Q_LORA, HEADS * HP)
    ukv = w_ukv_f.reshape(KV_LORA, HEADS, NOPE + VDIM)
    wk = jnp.pad(ukv[:, :, 0:NOPE], ((0, 0), (0, 0), (0, HP - NOPE))).reshape(KV_LORA, HEADS * HP)
    wkv = jnp.concatenate([wk, ukv[:, :, NOPE:].reshape(KV_LORA, MLA_W)], axis=1)

    lane = jnp.arange(HP)
    half = ROPE // 2
    inv_freq = 1.0 / (ROPE_THETA ** (jnp.arange(half, dtype=F32) / half))
    in_rope = (lane >= NOPE) & (lane < NOPE + ROPE)
    invf = jnp.where(in_rope, inv_freq[(lane - NOPE) % half], 0.0)
    m1 = jnp.where((lane >= NOPE) & (lane < NOPE + half), -1.0, 0.0)
    m2 = jnp.where((lane >= NOPE + half) & (lane < NOPE + ROPE), 1.0, 0.0)
    row = lambda a: jnp.pad(a.astype(F32), (0, D_MODEL - a.shape[0]))
    pvec = jnp.stack([row(q_norm_g), row(kv_norm_g), row(sgu_norm_g), row(sgu_norm_b), row(invf), row(m1),
                      row(m2), row(ln_g), row(ln_b)] + [jnp.zeros((D_MODEL,), F32)] * (PV_ROWS - 9))
    tri = jnp.tril(jnp.ones((CHUNK, CHUNK), dtype=bool))
    wt = jnp.where(tri[None], w_spatial, 0.0).astype(BF16)
    wtt = jnp.swapaxes(wt, 1, 2)
    bsp = jnp.repeat(b_spatial.T, VDIM, axis=1)

    cq, ckv, gate, q, k, v, cs = _fwd_pre(x2, pos, win, wuq, wkv, pvec)
    o, lse = _attn_fwd(q, k, v)
    dh2, do, dgate, g_wout, g_wsp, g_bsp, vec3 = _post(x2, tgt, o, gate, wout, pvec, wt, wtt, bsp)
    dq, dk, dv = _attn_bwd(q, k, v, do, o, lse, cs, pvec)
    gx, g_win, g_wuq, g_wkv, vec5 = _bwd_pre(x2, dh2, cq, ckv, cs, dq, dk, dv, dgate, win, wuq, wkv, pvec)

    g_win_f = jnp.concatenate([g_win[:, 0:384], g_win[:, 448:480], g_win[:, 512:D_INR]], axis=1)
    g_wuq_f = g_wuq.reshape(Q_LORA, HEADS, HP)[:, :, 0:NOPE + ROPE].reshape(Q_LORA, HEADS * (NOPE + ROPE))
    g_k = g_wkv[:, 0:HEADS * HP].reshape(KV_LORA, HEADS, HP)[:, :, 0:NOPE]
    g_v = g_wkv[:, HEADS * HP:].reshape(KV_LORA, HEADS, VDIM)
    g_wukv_f = jnp.concatenate([g_k, g_v], axis=2).reshape(KV_LORA, HEADS * (NOPE + VDIM))
    gbig = jnp.stack([_pack_big(g_win_f[:, 616 * jb:616 * (jb + 1)], g_wuq_f[:, 192 * jb:192 * (jb + 1)],
                                g_wukv_f[:, 256 * jb:256 * (jb + 1)], g_wout[256 * jb:256 * (jb + 1), :])
                      for jb in range(4)])
    loss_rows = jnp.pad(jnp.sum(vec3[4]).reshape(1, 1), ((0, 7), (0, 127)))
    gsmall = _pack_small(g_wsp, g_bsp[:, 0:HEADS].T, vec5[0], vec5[1, 0:KV_LORA], vec3[2, 0:GW], vec3[3, 0:GW],
                         vec3[0], vec3[1], loss_rows)
    rbig, rsmall = _grad_reduce(gbig, gsmall)

    g_all = jnp.concatenate([rbig, rsmall], axis=0)
    pack_all = lambda a: jnp.concatenate(
        [_pack_big(a[0], a[2], a[4], a[9]), _pack_small(a[7], a[8], a[1], a[3], a[5], a[6], a[10], a[11], zeros8)],
        axis=0)
    ws = (w_in, q_norm_g, w_uq, kv_norm_g, w_ukv, sgu_norm_g, sgu_norm_b, w_spatial, b_spatial, w_out, ln_g, ln_b)
    ms = (m_w_in, m_q_norm_g, m_w_uq, m_kv_norm_g, m_w_ukv, m_sgu_norm_g, m_sgu_norm_b, m_w_spatial, m_b_spatial,
          m_w_out, m_ln_g, m_ln_b)
    vs = (v_w_in, v_q_norm_g, v_w_uq, v_kv_norm_g, v_w_ukv, v_sgu_norm_g, v_sgu_norm_b, v_w_spatial, v_b_spatial,
          v_w_out, v_ln_g, v_ln_b)
    delta, m_new, v_new = _adamw(g_all, pack_all(ws), pack_all(ms), pack_all(vs))

    def unpack_all(s):
        b_in, b_uq, b_ukv, b_out = _unpack_big(s[0:BIG_ROWS])
        wsp, bs, qg, kvg, sg, sb, lng, lnb = _unpack_small(s[BIG_ROWS:])
        return [b_in, qg, b_uq, kvg, b_ukv, sg, sb, wsp, bs, b_out, lng, lnb]

    loss = rsmall[1080, 0]
    return (loss, gx.reshape(1, seq, D_MODEL), *unpack_all(g_all), *unpack_all(delta), *unpack_all(m_new),
            *unpack_all(v_new))
```

```python
import math

import jax
import jax.numpy as jnp
from jax import lax
from jax.experimental import pallas as pl
from jax.experimental.pallas import tpu as pltpu

F32 = jnp.float32
BF16 = jnp.bfloat16

D_MODEL = 1024
Q_LORA = 256
KV_LORA = 128
HEADS = 8
NOPE = 64
ROPE = 32
VDIM = 64
MLA_W = HEADS * VDIM
GW = 512
CHUNK = 128
HP = 128
PAIRS = HEADS // 2
D_IN = 2464
D_INR = 2560
ROPE_THETA = 10000.0
DN_ALPHA = 2.0 ** 0.25
EPS = 1e-5
SCALE = 1.0 / math.sqrt(NOPE + ROPE)
SCALE_LOG2E = SCALE * 1.4426950408889634
INV_SQRT2 = 0.7071067811865476
INV_SQRT_2PI = 0.3989422804014327

ADAM_LR = 0.001
ADAM_B1 = 0.9
ADAM_B2 = 0.999
ADAM_EPS = 1e-08
ADAM_WD = 0.01
ADAM_STEP = 10

PV_QG, PV_KVG, PV_SG, PV_SB, PV_INVF, PV_M1, PV_M2, PV_LNG, PV_LNB = range(9)
PV_ROWS = 16
GV_QG, GV_KVG, GV_SG, GV_SB, GV_LNG, GV_LNB, GV_LOSS = range(7)
GV_BSP = 8
GV_ROWS = 16

MESH = pl.DeviceIdType.MESH

TOK_TILE = 256
ATT_BLK = 512
VMEM_LIMIT = 56 * 1024 * 1024


def _dot(a, b):
    return jnp.dot(a, b, preferred_element_type=F32)


def _dot_nt(a, b):
    return lax.dot_general(a, b, (((1,), (1,)), ((), ())), preferred_element_type=F32)


def _dot_tn(a, b):
    return lax.dot_general(a, b, (((0,), (0,)), ((), ())), preferred_element_type=F32)


def _sigmoid(z):
    return 1.0 / (1.0 + jnp.exp(-z))


def _gelu(x):
    return 0.5 * x * (1.0 + lax.erf(x * INV_SQRT2))


def _gelu_grad(x):
    return 0.5 * (1.0 + lax.erf(x * INV_SQRT2)) + x * (INV_SQRT_2PI * jnp.exp(-0.5 * x * x))


def _rms_stats(x):
    r = lax.rsqrt(jnp.mean(x * x, axis=-1, keepdims=True) + EPS)
    return x * r, r


def _rms_bwd(dy, g, xh, r):
    dyg = dy * g
    return r * (dyg - xh * jnp.mean(dyg * xh, axis=-1, keepdims=True))


def _ln_stats(x):
    mu = jnp.mean(x, axis=-1, keepdims=True)
    xc = x - mu
    r = lax.rsqrt(jnp.mean(xc * xc, axis=-1, keepdims=True) + EPS)
    return xc * r, r


def _ln_bwd(dy, g, xh, r):
    dxh = dy * g
    return r * (dxh - jnp.mean(dxh, axis=-1, keepdims=True) - xh * jnp.mean(dxh * xh, axis=-1, keepdims=True))


def _rope_fwd(t, c, s1, s2):
    return t * c + pltpu.roll(t, HP - 16, 1) * s1 + pltpu.roll(t, 16, 1) * s2


def _rope_bwd(d, c, s1, s2):
    return d * c + pltpu.roll(d * s1, 16, 1) + pltpu.roll(d * s2, HP - 16, 1)


def _lane_lt64(shape):
    return lax.broadcasted_iota(jnp.int32, shape, len(shape) - 1) < 64


def _spatial_mix(w_ref, src, dst_ref, rows):
    for c in range(rows // CHUNK):
        for p in range(PAIRS):
            blk = src[c * CHUNK:(c + 1) * CHUNK, p * HP:(p + 1) * HP]
            a = _dot(w_ref[2 * p], blk)
            b = _dot(w_ref[2 * p + 1], blk)
            dst_ref[c * CHUNK:(c + 1) * CHUNK, p * HP:(p + 1) * HP] = jnp.where(_lane_lt64(a.shape), a, b)


def _gmlp_fwd(u_pre, v_pre, zb, sg, sb, wt_ref, bsp_ref, sv_ref, rows):
    u = _gelu(u_pre)
    xh, r = _ln_stats(_gelu(v_pre))
    vln = (xh * sg + sb).astype(BF16)
    _spatial_mix(wt_ref, vln, sv_ref, rows)
    bias = bsp_ref[...]
    svb = sv_ref[...] + jnp.concatenate([bias] * (rows // CHUNK), axis=0)
    sig = _sigmoid(zb)
    return u, xh, r, vln, svb, sig


def _weight_gather(shards):
    n_arr = len(shards)

    def body(*refs):
        ins, outs = refs[0:n_arr], refs[n_arr:2 * n_arr]
        send_sems, recv_sems = refs[2 * n_arr:]
        x, y, c = lax.axis_index("x"), lax.axis_index("y"), lax.axis_index("c")
        j = 2 * x + y
        sib = (x, y, 1 - c)
        chips = [(1 - x, y), (x, 1 - y), (1 - x, 1 - y)]
        for n in range(n_arr):
            outs[n][j] = ins[n][...].astype(BF16)

        def half(n, blk, core):
            r = shards[n].shape[0] // 2
            return outs[n].at[blk, pl.ds(pl.multiple_of(core * r, 16), r), :]

        def copy(k, ref, to):
            return pltpu.make_async_remote_copy(
                src_ref=ref, dst_ref=ref, send_sem=send_sems.at[k], recv_sem=recv_sems.at[k],
                device_id=to, device_id_type=MESH)

        first = [copy(6 * n + kk, half(n, j, c), (px, py, c))
                 for n in range(n_arr) for kk, (px, py) in enumerate(chips)]
        for cp in first:
            cp.start()
        passed = []
        for n in range(n_arr):
            for kk, (px, py) in enumerate(chips):
                landed = half(n, 2 * px + py, c)
                copy(6 * n + kk, landed, (px, py, c)).wait_recv()
                passed.append(copy(6 * n + 3 + kk, landed, sib))
                passed[-1].start()
        for n in range(n_arr):
            for kk, (px, py) in enumerate(chips):
                copy(6 * n + 3 + kk, half(n, 2 * px + py, 1 - c), sib).wait_recv()
        for cp in first + passed:
            cp.wait_send()

    vmem = pl.BlockSpec(memory_space=pltpu.VMEM)
    return pl.pallas_call(
        body, name="weight_gather",
        out_shape=[jax.ShapeDtypeStruct((4,) + a.shape, BF16) for a in shards],
        in_specs=[vmem] * n_arr, out_specs=[vmem] * n_arr,
        scratch_shapes=[pltpu.SemaphoreType.DMA((6 * n_arr,)), pltpu.SemaphoreType.DMA((6 * n_arr,))],
        compiler_params=pltpu.CompilerParams(vmem_limit_bytes=VMEM_LIMIT),
    )(*shards)


def _fwd_pre(x, pos, win, wuq, wkv, pvec):
    seq = x.shape[0]
    t = TOK_TILE

    def body(x_ref, pos_ref, win_ref, wuq_ref, wkv_ref, pv_ref,
             cq_o, ckv_o, gate_o, q_o, k_o, v_o, cs_o):
        proj = _dot(x_ref[...].astype(BF16), win_ref[...])
        cq = proj[:, 0:256]
        ckv = proj[:, 256:384]
        kr = proj[:, 384:512]
        cq_o[...] = cq
        ckv_o[...] = ckv
        gate_o[...] = proj[:, 512:D_INR]

        ang = pos_ref[...].astype(F32) * pv_ref[PV_INVF:PV_INVF + 1, 0:HP]
        cos = jnp.cos(ang)
        sin = jnp.sin(ang)
        cs_o[:, 0:HP] = cos
        cs_o[:, HP:2 * HP] = sin
        s1 = sin * pv_ref[PV_M1:PV_M1 + 1, 0:HP]
        s2 = sin * pv_ref[PV_M2:PV_M2 + 1, 0:HP]

        cqh, _ = _rms_stats(cq)
        q_all = _dot((cqh * pv_ref[PV_QG:PV_QG + 1, 0:Q_LORA]).astype(BF16), wuq_ref[...])
        ckvh, _ = _rms_stats(ckv)
        kv_all = _dot((ckvh * pv_ref[PV_KVG:PV_KVG + 1, 0:KV_LORA]).astype(BF16), wkv_ref[...])
        krr = _rope_fwd(kr, cos, s1, s2)
        for h in range(HEADS):
            sl = slice(h * HP, (h + 1) * HP)
            q_o[:, sl] = _rope_fwd(q_all[:, sl], cos, s1, s2).astype(BF16)
            k_o[:, sl] = (kv_all[:, sl] + krr).astype(BF16)
        v_o[...] = kv_all[:, HEADS * HP:].astype(BF16)

    tile = lambda w: pl.BlockSpec((t, w), lambda i: (i, 0))
    full = lambda a: pl.BlockSpec(a.shape, lambda i: (0,) * a.ndim)
    outs = [(Q_LORA, F32), (KV_LORA, F32), (2048, F32), (HEADS * HP, BF16), (HEADS * HP, BF16),
            (MLA_W, BF16), (2 * HP, F32)]
    return pl.pallas_call(
        body, name="fwd_pre", grid=(seq // t,),
        in_specs=[tile(D_MODEL), tile(1), full(win), full(wuq), full(wkv), full(pvec)],
        out_specs=[tile(w) for w, _ in outs],
        out_shape=[jax.ShapeDtypeStruct((seq, w), d) for w, d in outs],
        compiler_params=pltpu.CompilerParams(dimension_semantics=("arbitrary",), vmem_limit_bytes=VMEM_LIMIT),
    )(x, pos, win, wuq, wkv, pvec)


def _attn_fwd(q, k, v):
    seq = q.shape[0]
    b = ATT_BLK
    nq = seq // b

    def body(q_ref, k_ref, v_ref, o_o, lse_o, m_ref, l_ref, acc_ref):
        i = pl.program_id(1)
        m_ref[...] = jnp.full(m_ref.shape, -jnp.inf, F32)
        l_ref[...] = jnp.zeros(l_ref.shape, F32)
        acc_ref[...] = jnp.zeros(acc_ref.shape, F32)

        def step(j, masked):
            rows = pl.ds(pl.multiple_of(j * b, b), b)
            raw = [_dot_nt(q_ref[:, a * HP:(a + 1) * HP], k_ref[rows, a * HP:(a + 1) * HP]) for a in range(2)]
            vb = v_ref[rows, :]
            for a in range(2):
                s = raw[a] * SCALE_LOG2E
                if masked:
                    qi = lax.broadcasted_iota(jnp.int32, s.shape, 0)
                    ki = lax.broadcasted_iota(jnp.int32, s.shape, 1)
                    s = jnp.where(ki <= qi, s, -jnp.inf)
                m_prev = m_ref[a]
                m_new = jnp.maximum(m_prev, jnp.max(s, axis=-1, keepdims=True))
                alpha = jnp.exp2(m_prev - m_new)
                p = jnp.exp2(s - jnp.tile(m_new, (1, b // HP)))
                l_ref[a] = alpha * l_ref[a] + jnp.sum(p, axis=-1, keepdims=True)
                acc_ref[a] = alpha * acc_ref[a] + _dot(p.astype(BF16), vb)
                m_ref[a] = m_new

        def loop_body(j, carry):
            step(j, False)
            return carry

        lax.fori_loop(0, i, loop_body, 0)
        step(i, True)
        first = _lane_lt64((b, HP))
        o_o[...] = jnp.where(first, acc_ref[0] / l_ref[0], acc_ref[1] / l_ref[1])
        lse_t = jnp.where(first, m_ref[0] + jnp.log2(l_ref[0]), m_ref[1] + jnp.log2(l_ref[1])).T
        lse_o[0:1, :] = lse_t[0:1, :]
        lse_o[1:2, :] = lse_t[64:65, :]

    return pl.pallas_call(
        body, name="attn_fwd", grid=(PAIRS, nq),
        in_specs=[pl.BlockSpec((b, 2 * HP), lambda p, i: (i, p)),
                  pl.BlockSpec((seq, 2 * HP), lambda p, i: (0, p)),
                  pl.BlockSpec((seq, HP), lambda p, i: (0, p))],
        out_specs=[pl.BlockSpec((b, HP), lambda p, i: (i, p)),
                   pl.BlockSpec((None, 2, b), lambda p, i: (p, 0, i))],
        out_shape=[jax.ShapeDtypeStruct((seq, MLA_W), F32),
                   jax.ShapeDtypeStruct((PAIRS, 2, seq), F32)],
        scratch_shapes=[pltpu.VMEM((2, b, HP), F32)] * 3,
        compiler_params=pltpu.CompilerParams(dimension_semantics=("arbitrary", "arbitrary"),
                                             vmem_limit_bytes=VMEM_LIMIT),
    )(q, k, v)


def _post(x, tgt, o, gate, wout, pvec, wt, wtt, bsp):
    seq = x.shape[0]
    t = TOK_TILE
    nt = seq // t

    def body(x_ref, tgt_ref, o_ref, gate_ref, wout_ref, pv_ref, wt_ref, wtt_ref, bsp_ref,
             dh2_o, do_o, dgate_o, gwout_o, gwsp_o, vec_o, sv_ref, dvln_ref, bacc_ref):
        i = pl.program_id(0)

        @pl.when(i == 0)
        def _():
            gwout_o[...] = jnp.zeros_like(gwout_o)
            gwsp_o[...] = jnp.zeros_like(gwsp_o)
            vec_o[...] = jnp.zeros_like(vec_o)
            bacc_ref[...] = jnp.zeros_like(bacc_ref)

        za = gate_ref[:, 0:512]
        u_pre = gate_ref[:, 512:1024]
        v_pre = gate_ref[:, 1024:1536]
        zb = gate_ref[:, 1536:2048]
        sg = pv_ref[PV_SG:PV_SG + 1, 0:GW]
        sb = pv_ref[PV_SB:PV_SB + 1, 0:GW]
        lng = pv_ref[PV_LNG:PV_LNG + 1, :]
        lnb = pv_ref[PV_LNB:PV_LNB + 1, :]
        o = o_ref[...]

        sig_a = _sigmoid(za)
        silu_a = za * sig_a
        u, xh, r, vln, svb, sig_b = _gmlp_fwd(u_pre, v_pre, zb, sg, sb, wt_ref, bsp_ref, sv_ref, t)
        silu_b = zb * sig_b
        sgu = u * svb
        merged = jnp.concatenate([o * silu_a, sgu * silu_b], axis=1).astype(BF16)
        h2 = DN_ALPHA * x_ref[...] + _dot(merged, wout_ref[...])
        xh2, r2 = _ln_stats(h2)
        err = xh2 * lng + lnb - tgt_ref[...]
        d_out = err * (1.0 / D_MODEL)
        vec_o[GV_LNG:GV_LNG + 1, :] += jnp.sum(d_out * xh2, axis=0, keepdims=True)
        vec_o[GV_LNB:GV_LNB + 1, :] += jnp.sum(d_out, axis=0, keepdims=True)
        vec_o[GV_LOSS:GV_LOSS + 1, :] += jnp.sum(err * err, axis=0, keepdims=True) * (0.5 / D_MODEL)

        d_h2 = _ln_bwd(d_out, lng, xh2, r2)
        dh2_o[...] = d_h2
        dh2b = d_h2.astype(BF16)
        gwout_o[...] += _dot_tn(merged, dh2b)
        d_m = _dot_nt(dh2b, wout_ref[...])
        d_oa = d_m[:, 0:512]
        d_ob = d_m[:, 512:1024]
        do_o[...] = (d_oa * silu_a).astype(BF16)
        dgate_o[:, 0:512] = (d_oa * o * (sig_a * (1.0 + za * (1.0 - sig_a)))).astype(BF16)
        dgate_o[:, 1536:2048] = (d_ob * sgu * (sig_b * (1.0 + zb * (1.0 - sig_b)))).astype(BF16)
        d_sgu = d_ob * silu_b
        dgate_o[:, 512:1024] = (d_sgu * svb * _gelu_grad(u_pre)).astype(BF16)
        d_sv = d_sgu * u
        acc = bacc_ref[...]
        for c in range(t // CHUNK):
            acc = acc + d_sv[c * CHUNK:(c + 1) * CHUNK, :]
        bacc_ref[...] = acc
        d_svb = d_sv.astype(BF16)
        for c in range(t // CHUNK):
            for p in range(PAIRS):
                blk = d_svb[c * CHUNK:(c + 1) * CHUNK, p * HP:(p + 1) * HP]
                vblk = vln[c * CHUNK:(c + 1) * CHUNK, p * HP:(p + 1) * HP]
                first = _lane_lt64(blk.shape)
                gwsp_o[2 * p] += _dot_nt(jnp.where(first, blk, jnp.zeros_like(blk)), vblk)
                gwsp_o[2 * p + 1] += _dot_nt(jnp.where(first, jnp.zeros_like(blk), blk), vblk)
        _spatial_mix(wtt_ref, d_svb, dvln_ref, t)
        d_vln = dvln_ref[...]
        vec_o[GV_SG:GV_SG + 1, 0:GW] += jnp.sum(d_vln * xh, axis=0, keepdims=True)
        vec_o[GV_SB:GV_SB + 1, 0:GW] += jnp.sum(d_vln, axis=0, keepdims=True)
        dgate_o[:, 1024:1536] = (_ln_bwd(d_vln, sg, xh, r) * _gelu_grad(v_pre)).astype(BF16)

        @pl.when(i == nt - 1)
        def _():
            tri = (lax.broadcasted_iota(jnp.int32, (CHUNK, CHUNK), 1)
                   <= lax.broadcasted_iota(jnp.int32, (CHUNK, CHUNK), 0))
            for h in range(HEADS):
                gwsp_o[h] = jnp.where(tri, gwsp_o[h], 0.0)
            lane = lax.broadcasted_iota(jnp.int32, (CHUNK, HP), 1)
            res = jnp.zeros((CHUNK, HP), F32)
            for h in range(HEADS):
                p, a = divmod(h, 2)
                blk = bacc_ref[:, p * HP:(p + 1) * HP]
                part = jnp.where(_lane_lt64(blk.shape) == (a == 0), blk, 0.0)
                res = jnp.where(lane == h, jnp.sum(part, axis=-1, keepdims=True), res)
            vec_o[GV_BSP:GV_BSP + HEADS, 0:HP] = res.T[0:HEADS, :]
            lane1 = lax.broadcasted_iota(jnp.int32, (1, D_MODEL), 1)
            total = jnp.sum(vec_o[GV_LOSS:GV_LOSS + 1, :], axis=-1, keepdims=True)
            vec_o[GV_LOSS:GV_LOSS + 1, :] = jnp.where(lane1 == 0, total, 0.0)

    tile = lambda w: pl.BlockSpec((t, w), lambda i: (i, 0))
    full = lambda a: pl.BlockSpec(a.shape, lambda i: (0,) * a.ndim)
    const = lambda s: pl.BlockSpec(s, lambda i: (0,) * len(s))
    return pl.pallas_call(
        body, name="post", grid=(nt,),
        in_specs=[tile(D_MODEL), tile(D_MODEL), tile(MLA_W), tile(2048), full(wout), full(pvec),
                  full(wt), full(wtt), full(bsp)],
        out_specs=[tile(D_MODEL), tile(MLA_W), tile(2048), const((D_MODEL, D_MODEL)),
                   const((HEADS, CHUNK, CHUNK)), const((GV_ROWS, D_MODEL))],
        out_shape=[jax.ShapeDtypeStruct((seq, D_MODEL), F32), jax.ShapeDtypeStruct((seq, MLA_W), BF16),
                   jax.ShapeDtypeStruct((seq, 2048), BF16), jax.ShapeDtypeStruct((D_MODEL, D_MODEL), F32),
                   jax.ShapeDtypeStruct((HEADS, CHUNK, CHUNK), F32), jax.ShapeDtypeStruct((GV_ROWS, D_MODEL), F32)],
        scratch_shapes=[pltpu.VMEM((t, GW), F32), pltpu.VMEM((t, GW), F32), pltpu.VMEM((CHUNK, GW), F32)],
        compiler_params=pltpu.CompilerParams(dimension_semantics=("arbitrary",), vmem_limit_bytes=VMEM_LIMIT),
    )(x, tgt, o, gate, wout, pvec, wt, wtt, bsp)


def _attn_bwd(q, k, v, do, o, lse, cs, pvec):
    seq = q.shape[0]
    b = ATT_BLK
    nq = seq // b

    def body(q_ref, k_ref, v_ref, do_ref, o_ref, lse_ref, cs_ref, pv_ref, dq_o, dk_o, dv_o, dk_acc, dv_acc):
        i = pl.program_id(1)

        @pl.when(i == 0)
        def _():
            dk_acc[...] = jnp.zeros_like(dk_acc)
            dv_acc[...] = jnp.zeros_like(dv_acc)

        first = _lane_lt64((b, HP))
        do = do_ref[...]
        zero = jnp.zeros_like(do)
        dos = [jnp.where(first, do, zero), jnp.where(first, zero, do)]
        prod_t = (do.astype(F32) * o_ref[...]).T
        deltas = [jnp.sum(prod_t[0:64, :], axis=0, keepdims=True),
                  jnp.sum(prod_t[64:128, :], axis=0, keepdims=True)]
        lses = [lse_ref[0:1, :], lse_ref[1:2, :]]
        qs = [q_ref[:, a * HP:(a + 1) * HP] for a in range(2)]

        def step(j, dqs, masked):
            rows = pl.ds(pl.multiple_of(j * b, b), b)
            vb = v_ref[rows, :]
            new_dq = []
            dvs = []
            for a in range(2):
                kb = k_ref[rows, a * HP:(a + 1) * HP]
                st = _dot_nt(kb, qs[a]) * SCALE_LOG2E
                pt = jnp.exp2(st - lses[a])
                if masked:
                    ki = lax.broadcasted_iota(jnp.int32, st.shape, 0)
                    qi = lax.broadcasted_iota(jnp.int32, st.shape, 1)
                    pt = jnp.where(ki <= qi, pt, 0.0)
                dvs.append(_dot(pt.astype(BF16), do))
                dpt = _dot_nt(vb, dos[a])
                dst = (pt * (dpt - deltas[a]) * SCALE).astype(BF16)
                dk_acc[rows, a * HP:(a + 1) * HP] += _dot(dst, qs[a])
                new_dq.append(dqs[a] + _dot_tn(dst, kb))
            dv_acc[rows, :] += jnp.where(first, dvs[0], dvs[1])
            return tuple(new_dq)

        init = (jnp.zeros((b, HP), F32), jnp.zeros((b, HP), F32))
        dqs = lax.fori_loop(0, i, lambda j, cr: step(j, cr, False), init)
        dqs = step(i, dqs, True)
        cos = cs_ref[:, 0:HP]
        sin = cs_ref[:, HP:2 * HP]
        s1 = sin * pv_ref[PV_M1:PV_M1 + 1, 0:HP]
        s2 = sin * pv_ref[PV_M2:PV_M2 + 1, 0:HP]
        for a in range(2):
            dq_o[:, a * HP:(a + 1) * HP] = _rope_bwd(dqs[a], cos, s1, s2).astype(BF16)

        @pl.when(i == nq - 1)
        def _():
            dk_o[...] = dk_acc[...].astype(BF16)
            dv_o[...] = dv_acc[...].astype(BF16)

    return pl.pallas_call(
        body, name="attn_bwd", grid=(PAIRS, nq),
        in_specs=[pl.BlockSpec((b, 2 * HP), lambda p, i: (i, p)),
                  pl.BlockSpec((seq, 2 * HP), lambda p, i: (0, p)),
                  pl.BlockSpec((seq, HP), lambda p, i: (0, p)),
                  pl.BlockSpec((b, HP), lambda p, i: (i, p)),
                  pl.BlockSpec((b, HP), lambda p, i: (i, p)),
                  pl.BlockSpec((None, 2, b), lambda p, i: (p, 0, i)),
                  pl.BlockSpec((b, 2 * HP), lambda p, i: (i, 0)),
                  pl.BlockSpec(pvec.shape, lambda p, i: (0, 0))],
        out_specs=[pl.BlockSpec((b, 2 * HP), lambda p, i: (i, p)),
                   pl.BlockSpec((seq, 2 * HP), lambda p, i: (0, p)),
                   pl.BlockSpec((seq, HP), lambda p, i: (0, p))],
        out_shape=[jax.ShapeDtypeStruct((seq, HEADS * HP), BF16),
                   jax.ShapeDtypeStruct((seq, HEADS * HP), BF16),
                   jax.ShapeDtypeStruct((seq, MLA_W), BF16)],
        scratch_shapes=[pltpu.VMEM((seq, 2 * HP), F32), pltpu.VMEM((seq, HP), F32)],
        compiler_params=pltpu.CompilerParams(dimension_semantics=("arbitrary", "arbitrary"),
                                             vmem_limit_bytes=VMEM_LIMIT),
    )(q, k, v, do, o, lse, cs, pvec)


def _bwd_pre(x, dh2, cq, ckv, cs, dq, dk, dv, dgate, win, wuq, wkv, pvec, gvec):
    seq = x.shape[0]
    t = TOK_TILE

    def body(x_ref, dh2_ref, cq_ref, ckv_ref, cs_ref, dq_ref, dk_ref, dv_ref, dgate_ref,
             win_ref, wuq_ref, wkv_ref, pv_ref, gv_ref, gx_o, gwin_o, gwuq_o, gwkv_o, vec_o):
        i = pl.program_id(0)

        @pl.when(i == 0)
        def _():
            gwin_o[...] = jnp.zeros_like(gwin_o)
            gwuq_o[...] = jnp.zeros_like(gwuq_o)
            gwkv_o[...] = jnp.zeros_like(gwkv_o)
            vec_o[...] = gv_ref[...]

        qg = pv_ref[PV_QG:PV_QG + 1, 0:Q_LORA]
        kvg = pv_ref[PV_KVG:PV_KVG + 1, 0:KV_LORA]
        dq = dq_ref[...]
        cqh, rq = _rms_stats(cq_ref[...])
        d_cqn = _dot_nt(dq, wuq_ref[...])
        gwuq_o[...] += _dot_tn((cqh * qg).astype(BF16), dq)
        vec_o[GV_QG:GV_QG + 1, 0:Q_LORA] += jnp.sum(d_cqn * cqh, axis=0, keepdims=True)
        d_cq = _rms_bwd(d_cqn, qg, cqh, rq)

        dk = dk_ref[...]
        dkv = jnp.concatenate([dk, dv_ref[...]], axis=1)
        ckvh, rkv = _rms_stats(ckv_ref[...])
        d_ckvn = _dot_nt(dkv, wkv_ref[...])
        gwkv_o[...] += _dot_tn((ckvh * kvg).astype(BF16), dkv)
        vec_o[GV_KVG:GV_KVG + 1, 0:KV_LORA] += jnp.sum(d_ckvn * ckvh, axis=0, keepdims=True)
        d_ckv = _rms_bwd(d_ckvn, kvg, ckvh, rkv)

        dks = dk[:, 0:HP].astype(F32)
        for h in range(1, HEADS):
            dks = dks + dk[:, h * HP:(h + 1) * HP].astype(F32)
        cos = cs_ref[:, 0:HP]
        sin = cs_ref[:, HP:2 * HP]
        d_kr = _rope_bwd(dks, cos, sin * pv_ref[PV_M1:PV_M1 + 1, 0:HP], sin * pv_ref[PV_M2:PV_M2 + 1, 0:HP])

        d_proj = jnp.concatenate([d_cq.astype(BF16), d_ckv.astype(BF16), d_kr.astype(BF16), dgate_ref[...]], axis=1)
        gwin_o[...] += _dot_tn(x_ref[...].astype(BF16), d_proj)
        gx_o[...] = DN_ALPHA * dh2_ref[...] + _dot_nt(d_proj, win_ref[...])

    tile = lambda w: pl.BlockSpec((t, w), lambda i: (i, 0))
    full = lambda a: pl.BlockSpec(a.shape, lambda i: (0,) * a.ndim)
    const = lambda s: pl.BlockSpec(s, lambda i: (0,) * len(s))
    return pl.pallas_call(
        body, name="bwd_pre", grid=(seq // t,),
        in_specs=[tile(D_MODEL), tile(D_MODEL), tile(Q_LORA), tile(KV_LORA), tile(2 * HP), tile(HEADS * HP),
                  tile(HEADS * HP), tile(MLA_W), tile(2048), full(win), full(wuq), full(wkv), full(pvec), full(gvec)],
        out_specs=[tile(D_MODEL), const((D_MODEL, D_INR)), const((Q_LORA, HEADS * HP)),
                   const((KV_LORA, HEADS * HP + MLA_W)), const((GV_ROWS, D_MODEL))],
        out_shape=[jax.ShapeDtypeStruct((seq, D_MODEL), F32), jax.ShapeDtypeStruct((D_MODEL, D_INR), F32),
                   jax.ShapeDtypeStruct((Q_LORA, HEADS * HP), F32),
                   jax.ShapeDtypeStruct((KV_LORA, HEADS * HP + MLA_W), F32),
                   jax.ShapeDtypeStruct((GV_ROWS, D_MODEL), F32)],
        compiler_params=pltpu.CompilerParams(dimension_semantics=("arbitrary",), vmem_limit_bytes=VMEM_LIMIT),
    )(x, dh2, cq, ckv, cs, dq, dk, dv, dgate, win, wuq, wkv, pvec, gvec)


def _grad_reduce(gs, gvec):
    n_arr = len(gs)
    n_big = n_arr - 1
    k1 = lambda n, blk: 4 * n + blk
    k2 = lambda n, kk: 4 * n_arr + 3 * n + kk
    k3 = lambda n: 7 * n_arr + n
    k3w = lambda k: 7 * n_arr + n_big + k
    kv = lambda k: 7 * n_arr + n_big + 7 + k
    n_sem = 7 * n_arr + n_big + 14

    def body(*refs):
        g, gv = refs[0:n_arr], refs[n_arr]
        outs, ov = refs[n_arr + 1:2 * n_arr + 1], refs[2 * n_arr + 1]
        r1 = refs[2 * n_arr + 2:3 * n_arr + 2]
        r2 = refs[3 * n_arr + 2:4 * n_arr + 2]
        vbuf, send_sems, recv_sems = refs[4 * n_arr + 2:]
        x, y, c = lax.axis_index("x"), lax.axis_index("y"), lax.axis_index("c")
        j = 2 * x + y
        me = 2 * j + c
        sib = (x, y, 1 - c)
        chips = [(1 - x, y), (x, 1 - y), (1 - x, 1 - y)]
        others = [sib] + [(px, py, pc) for (px, py) in chips for pc in (c, 1 - c)]

        def copy(k, src, dst, to):
            return pltpu.make_async_remote_copy(
                src_ref=src, dst_ref=dst, send_sem=send_sems.at[k], recv_sem=recv_sems.at[k],
                device_id=to, device_id_type=MESH)

        l1 = [copy(k1(n, blk), g[n].at[blk, 1 - c], r1[n].at[blk], sib) for n in range(n_arr) for blk in range(4)]
        lv = [copy(kv(k), gv, vbuf.at[me], to) for k, to in enumerate(others)]
        for cp in l1 + lv:
            cp.start()
        for n in range(n_arr):
            for blk in range(4):
                copy(k1(n, blk), g[n].at[blk, c], r1[n].at[blk], sib).wait_recv()
        for n in range(n_arr):
            for blk in range(4):
                r1[n][blk] = g[n][blk, c] + r1[n][blk]

        l2 = [copy(k2(n, kk), r1[n].at[2 * px + py], r2[n].at[kk], (px, py, c))
              for n in range(n_arr) for kk, (px, py) in enumerate(chips)]
        for cp in l2:
            cp.start()
        for n in range(n_arr):
            for kk in range(3):
                copy(k2(n, kk), r1[n].at[0], r2[n].at[kk], sib).wait_recv()
        for n in range(n_arr):
            red = ((r1[n][j] + r2[n][0]) + r2[n][1]) + r2[n][2]
            if n < n_big:
                outs[n][c] = red
            else:
                outs[n][j, c] = red

        l3 = [copy(k3(n), outs[n].at[c], outs[n].at[c], sib) for n in range(n_big)]
        l3 += [copy(k3w(k), outs[n_big].at[j, c], outs[n_big].at[j, c], to) for k, to in enumerate(others)]
        for cp in l3:
            cp.start()
        for n in range(n_big):
            copy(k3(n), outs[n].at[1 - c], outs[n].at[1 - c], sib).wait_recv()
        for k, (px, py, pc) in enumerate(others):
            landed = outs[n_big].at[2 * px + py, pc]
            copy(k3w(k), landed, landed, (px, py, pc)).wait_recv()
            copy(kv(k), gv, vbuf.at[4 * px + 2 * py + pc], (px, py, pc)).wait_recv()
        vbuf[me] = gv[...]
        total = vbuf[0]
        for d in range(1, 8):
            total = total + vbuf[d]
        ov[...] = total
        for cp in l1 + lv + l2 + l3:
            cp.wait_send()

    vmem = pl.BlockSpec(memory_space=pltpu.VMEM)
    half_shapes = [a.shape[2:] for a in gs]
    out_shape = [jax.ShapeDtypeStruct((2,) + s, F32) for s in half_shapes[:n_big]]
    out_shape += [jax.ShapeDtypeStruct((4, 2) + half_shapes[n_big], F32), jax.ShapeDtypeStruct(gvec.shape, F32)]
    scratch = [pltpu.VMEM((4,) + s, F32) for s in half_shapes] + [pltpu.VMEM((3,) + s, F32) for s in half_shapes]
    scratch += [pltpu.VMEM((8,) + gvec.shape, F32), pltpu.SemaphoreType.DMA((n_sem,)), pltpu.SemaphoreType.DMA((n_sem,))]
    return pl.pallas_call(
        body, name="grad_reduce", out_shape=out_shape,
        in_specs=[vmem] * (n_arr + 1), out_specs=[vmem] * (n_arr + 1), scratch_shapes=scratch,
        compiler_params=pltpu.CompilerParams(vmem_limit_bytes=VMEM_LIMIT),
    )(*gs, gvec)


SMALL_ROWS = ((GV_QG, 1, Q_LORA), (GV_KVG, 1, KV_LORA), (GV_SG, 1, GW), (GV_SB, 1, GW),
              (GV_LNG, 1, D_MODEL), (GV_LNB, 1, D_MODEL), (GV_BSP, HEADS, CHUNK))


def _adam_update(g, w, m, v):
    m_new = ADAM_B1 * m + (1.0 - ADAM_B1) * g
    v_new = ADAM_B2 * v + (1.0 - ADAM_B2) * (g * g)
    m_hat = m_new / (1.0 - ADAM_B1 ** ADAM_STEP)
    v_hat = v_new / (1.0 - ADAM_B2 ** ADAM_STEP)
    return -ADAM_LR * (m_hat / (jnp.sqrt(v_hat) + ADAM_EPS) + ADAM_WD * w), m_new, v_new


def _adamw(g_big, w_big, m_big, v_big, gvec, w_small, m_small, v_small):
    nb, ns = len(g_big), len(w_small)

    def body(*refs):
        it = iter(refs)
        take = lambda n: [next(it) for _ in range(n)]
        g_b, w_b, m_b, v_b = take(nb), take(nb), take(nb), take(nb)
        gv = next(it)
        w_s, m_s, v_s = take(ns), take(ns), take(ns)
        d_bo, m_bo, v_bo = take(nb), take(nb), take(nb)
        g_so, d_so, m_so, v_so = take(ns), take(ns), take(ns), take(ns)
        for n in range(nb):
            d_bo[n][...], m_bo[n][...], v_bo[n][...] = _adam_update(g_b[n][...], w_b[n][...], m_b[n][...], v_b[n][...])
        for n, (row, nrow, width) in enumerate(SMALL_ROWS):
            gs = gv[row:row + nrow, 0:width]
            g_so[n][...] = gs
            d_so[n][...], m_so[n][...], v_so[n][...] = _adam_update(gs, w_s[n][...], m_s[n][...], v_s[n][...])

    vmem = pl.BlockSpec(memory_space=pltpu.VMEM)
    big = [jax.ShapeDtypeStruct(a.shape, F32) for a in w_big]
    small = [jax.ShapeDtypeStruct(a.shape, F32) for a in w_small]
    return pl.pallas_call(
        body, name="adamw", out_shape=big * 3 + small * 4,
        in_specs=[vmem] * (4 * nb + 1 + 3 * ns), out_specs=[vmem] * (3 * nb + 4 * ns),
        compiler_params=pltpu.CompilerParams(vmem_limit_bytes=VMEM_LIMIT),
    )(*g_big, *w_big, *m_big, *v_big, gvec, *w_small, *m_small, *v_small)


def kernel(x, positions, w_in, q_norm_g, w_uq, kv_norm_g, w_ukv, sgu_norm_g, sgu_norm_b, w_spatial, b_spatial, w_out, ln_g, ln_b, loss_target, m_w_in, m_q_norm_g, m_w_uq, m_kv_norm_g, m_w_ukv, m_sgu_norm_g, m_sgu_norm_b, m_w_spatial, m_b_spatial, m_w_out, m_ln_g, m_ln_b, v_w_in, v_q_norm_g, v_w_uq, v_kv_norm_g, v_w_ukv, v_sgu_norm_g, v_sgu_norm_b, v_w_spatial, v_b_spatial, v_w_out, v_ln_g, v_ln_b):
    seq = x.shape[1]
    x2 = x.reshape(seq, D_MODEL)
    tgt = loss_target.reshape(seq, D_MODEL)
    pos = positions.reshape(seq, 1)

    a_in, a_uq, a_ukv, a_out = _weight_gather([w_in, w_uq, w_ukv, w_out])
    w_in_f = jnp.swapaxes(a_in, 0, 1).reshape(D_MODEL, D_IN)
    w_uq_f = jnp.swapaxes(a_uq, 0, 1).reshape(Q_LORA, HEADS * (NOPE + ROPE))
    w_ukv_f = jnp.swapaxes(a_ukv, 0, 1).reshape(KV_LORA, HEADS * (NOPE + VDIM))
    wout = a_out.reshape(D_MODEL, D_MODEL)
    zc = lambda n: jnp.zeros((D_MODEL, n), BF16)
    win = jnp.concatenate([w_in_f[:, 0:384], zc(64), w_in_f[:, 384:416], zc(32), w_in_f[:, 416:D_IN]], axis=1)
    wuq = jnp.pad(w_uq_f.reshape(Q_LORA, HEADS, NOPE + ROPE), ((0, 0), (0, 0), (0, HP - NOPE - ROPE)))
    wuq = wuq.reshape(Q_LORA, HEADS * HP)
    ukv = w_ukv_f.reshape(KV_LORA, HEADS, NOPE + VDIM)
    wk = jnp.pad(ukv[:, :, 0:NOPE], ((0, 0), (0, 0), (0, HP - NOPE))).reshape(KV_LORA, HEADS * HP)
    wkv = jnp.concatenate([wk, ukv[:, :, NOPE:].reshape(KV_LORA, MLA_W)], axis=1)

    lane = jnp.arange(HP)
    half = ROPE // 2
    inv_freq = 1.0 / (ROPE_THETA ** (jnp.arange(half, dtype=F32) / half))
    in_rope = (lane >= NOPE) & (lane < NOPE + ROPE)
    invf = jnp.where(in_rope, inv_freq[(lane - NOPE) % half], 0.0)
    m1 = jnp.where((lane >= NOPE) & (lane < NOPE + half), -1.0, 0.0)
    m2 = jnp.where((lane >= NOPE + half) & (lane < NOPE + ROPE), 1.0, 0.0)
    row = lambda a: jnp.pad(a.astype(F32), (0, D_MODEL - a.shape[0]))
    pvec = jnp.stack([row(q_norm_g), row(kv_norm_g), row(sgu_norm_g), row(sgu_norm_b), row(invf), row(m1),
                      row(m2), row(ln_g), row(ln_b)] + [jnp.zeros((D_MODEL,), F32)] * (PV_ROWS - 9))
    tri = jnp.tril(jnp.ones((CHUNK, CHUNK), dtype=bool))
    wt = jnp.where(tri[None], w_spatial, 0.0).astype(BF16)
    wtt = jnp.swapaxes(wt, 1, 2)
    bsp = jnp.repeat(b_spatial.T, VDIM, axis=1)

    cq, ckv, gate, q, k, v, cs = _fwd_pre(x2, pos, win, wuq, wkv, pvec)
    o, lse = _attn_fwd(q, k, v)
    dh2, do, dgate, g_wout, g_wsp, gvec = _post(x2, tgt, o, gate, wout, pvec, wt, wtt, bsp)
    dq, dk, dv = _attn_bwd(q, k, v, do, o, lse, cs, pvec)
    gx, g_win, g_wuq, g_wkv, gvec = _bwd_pre(x2, dh2, cq, ckv, cs, dq, dk, dv, dgate, win, wuq, wkv, pvec, gvec)

    g_win_f = jnp.concatenate([g_win[:, 0:384], g_win[:, 448:480], g_win[:, 512:D_INR]], axis=1)
    g_wuq_f = g_wuq.reshape(Q_LORA, HEADS, HP)[:, :, 0:NOPE + ROPE].reshape(Q_LORA, HEADS * (NOPE + ROPE))
    g_k = g_wkv[:, 0:HEADS * HP].reshape(KV_LORA, HEADS, HP)[:, :, 0:NOPE]
    g_v = g_wkv[:, HEADS * HP:].reshape(KV_LORA, HEADS, VDIM)
    g_wukv_f = jnp.concatenate([g_k, g_v], axis=2).reshape(KV_LORA, HEADS * (NOPE + VDIM))

    def by_chip(a):
        rows, cols = a.shape[0], a.shape[1] // 4
        return jnp.swapaxes(a.reshape(rows, 4, cols), 0, 1).reshape(4, 2, rows // 2, cols)

    gs = [by_chip(g_win_f), by_chip(g_wuq_f), by_chip(g_wukv_f), g_wout.reshape(4, 2, 128, D_MODEL),
          g_wsp.reshape(4, 2, CHUNK, CHUNK)]
    r_in, r_uq, r_ukv, r_out, r_wsp, r_vec = _grad_reduce(gs, gvec)

    g_big = [r_in.reshape(w_in.shape), r_uq.reshape(w_uq.shape), r_ukv.reshape(w_ukv.shape),
             r_out.reshape(w_out.shape), r_wsp.reshape(w_spatial.shape)]
    small = lambda qg, kvg, sg, sb, lng, lnb, bs: [qg.reshape(1, -1), kvg.reshape(1, -1), sg.reshape(1, -1),
                                                   sb.reshape(1, -1), lng.reshape(1, -1), lnb.reshape(1, -1), bs]
    res = _adamw(g_big, [w_in, w_uq, w_ukv, w_out, w_spatial], [m_w_in, m_w_uq, m_w_ukv, m_w_out, m_w_spatial],
                 [v_w_in, v_w_uq, v_w_ukv, v_w_out, v_w_spatial], r_vec,
                 small(q_norm_g, kv_norm_g, sgu_norm_g, sgu_norm_b, ln_g, ln_b, b_spatial),
                 small(m_q_norm_g, m_kv_norm_g, m_sgu_norm_g, m_sgu_norm_b, m_ln_g, m_ln_b, m_b_spatial),
                 small(v_q_norm_g, v_kv_norm_g, v_sgu_norm_g, v_sgu_norm_b, v_ln_g, v_ln_b, v_b_spatial))

    def ordered(big, sm):
        vec = lambda n: sm[n].reshape(-1)
        return [big[0], vec(0), big[1], vec(1), big[2], vec(2), vec(3), big[4], sm[6], big[3], vec(4), vec(5)]

    loss = r_vec[GV_LOSS, 0]
    return (loss, gx.reshape(1, seq, D_MODEL), *ordered(g_big, res[15:22]), *ordered(res[0:5], res[22:29]),
            *ordered(res[5:10], res[29:36]), *ordered(res[10:15], res[36:43]))
```

```python
import math

import jax
import jax.numpy as jnp
from jax import lax
from jax.experimental import pallas as pl
from jax.experimental.pallas import tpu as pltpu

F32 = jnp.float32
BF16 = jnp.bfloat16

D_MODEL = 1024
Q_LORA = 256
KV_LORA = 128
HEADS = 8
NOPE = 64
ROPE = 32
VDIM = 64
MLA_W = HEADS * VDIM
GW = 512
CHUNK = 128
HP = 128
PAIRS = HEADS // 2
D_IN = 2464
D_INR = 2560
ROPE_THETA = 10000.0
DN_ALPHA = 2.0 ** 0.25
EPS = 1e-5
SCALE = 1.0 / math.sqrt(NOPE + ROPE)
SCALE_LOG2E = SCALE * 1.4426950408889634
INV_SQRT2 = 0.7071067811865476
INV_SQRT_2PI = 0.3989422804014327

ADAM_LR = 0.001
ADAM_B1 = 0.9
ADAM_B2 = 0.999
ADAM_EPS = 1e-08
ADAM_WD = 0.01
ADAM_STEP = 10

PV_QG, PV_KVG, PV_SG, PV_SB, PV_INVF, PV_M1, PV_M2, PV_LNG, PV_LNB = range(9)
PV_ROWS = 16
GV_QG, GV_KVG, GV_SG, GV_SB, GV_LNG, GV_LNB, GV_LOSS = range(7)
GV_BSP = 8
GV_ROWS = 16

MESH = pl.DeviceIdType.MESH

TOK_TILE = 256
ATT_BLK = 512
VMEM_LIMIT = 56 * 1024 * 1024


def _dot(a, b):
    return jnp.dot(a, b, preferred_element_type=F32)


def _dot_nt(a, b):
    return lax.dot_general(a, b, (((1,), (1,)), ((), ())), preferred_element_type=F32)


def _dot_tn(a, b):
    return lax.dot_general(a, b, (((0,), (0,)), ((), ())), preferred_element_type=F32)


def _sigmoid(z):
    return 1.0 / (1.0 + jnp.exp(-z))


def _gelu(x):
    return 0.5 * x * (1.0 + lax.erf(x * INV_SQRT2))


def _gelu_grad(x):
    return 0.5 * (1.0 + lax.erf(x * INV_SQRT2)) + x * (INV_SQRT_2PI * jnp.exp(-0.5 * x * x))


def _rms_stats(x):
    r = lax.rsqrt(jnp.mean(x * x, axis=-1, keepdims=True) + EPS)
    return x * r, r


def _rms_bwd(dy, g, xh, r):
    dyg = dy * g
    return r * (dyg - xh * jnp.mean(dyg * xh, axis=-1, keepdims=True))


def _ln_stats(x):
    mu = jnp.mean(x, axis=-1, keepdims=True)
    xc = x - mu
    r = lax.rsqrt(jnp.mean(xc * xc, axis=-1, keepdims=True) + EPS)
    return xc * r, r


def _ln_bwd(dy, g, xh, r):
    dxh = dy * g
    return r * (dxh - jnp.mean(dxh, axis=-1, keepdims=True) - xh * jnp.mean(dxh * xh, axis=-1, keepdims=True))


def _rope_fwd(t, c, s1, s2):
    return t * c + pltpu.roll(t, HP - 16, 1) * s1 + pltpu.roll(t, 16, 1) * s2


def _rope_bwd(d, c, s1, s2):
    return d * c + pltpu.roll(d * s1, 16, 1) + pltpu.roll(d * s2, HP - 16, 1)


def _lane_lt64(shape):
    return lax.broadcasted_iota(jnp.int32, shape, len(shape) - 1) < 64


def _spatial_mix(w_ref, src, dst_ref, rows):
    for c in range(rows // CHUNK):
        for p in range(PAIRS):
            blk = src[c * CHUNK:(c + 1) * CHUNK, p * HP:(p + 1) * HP]
            a = _dot(w_ref[2 * p], blk)
            b = _dot(w_ref[2 * p + 1], blk)
            dst_ref[c * CHUNK:(c + 1) * CHUNK, p * HP:(p + 1) * HP] = jnp.where(_lane_lt64(a.shape), a, b)


def _gmlp_fwd(u_pre, v_pre, zb, sg, sb, wt_ref, bsp_ref, sv_ref, rows):
    u = _gelu(u_pre)
    xh, r = _ln_stats(_gelu(v_pre))
    vln = (xh * sg + sb).astype(BF16)
    _spatial_mix(wt_ref, vln, sv_ref, rows)
    bias = bsp_ref[...]
    svb = sv_ref[...] + jnp.concatenate([bias] * (rows // CHUNK), axis=0)
    sig = _sigmoid(zb)
    return u, xh, r, vln, svb, sig


def _weight_gather(shards):
    n_arr = len(shards)

    def body(*refs):
        ins, outs = refs[0:n_arr], refs[n_arr:2 * n_arr]
        send_sems, recv_sems = refs[2 * n_arr:]
        x, y, c = lax.axis_index("x"), lax.axis_index("y"), lax.axis_index("c")
        j = 2 * x + y
        sib = (x, y, 1 - c)
        chips = [(1 - x, y), (x, 1 - y), (1 - x, 1 - y)]
        for n in range(n_arr):
            outs[n][j] = ins[n][...].astype(BF16)

        def half(n, blk, core):
            r = shards[n].shape[0] // 2
            return outs[n].at[blk, pl.ds(pl.multiple_of(core * r, 16), r), :]

        def copy(k, ref, to):
            return pltpu.make_async_remote_copy(
                src_ref=ref, dst_ref=ref, send_sem=send_sems.at[k], recv_sem=recv_sems.at[k],
                device_id=to, device_id_type=MESH)

        first = [copy(6 * n + kk, half(n, j, c), (px, py, c))
                 for n in range(n_arr) for kk, (px, py) in enumerate(chips)]
        for cp in first:
            cp.start()
        passed = []
        for n in range(n_arr):
            for kk, (px, py) in enumerate(chips):
                landed = half(n, 2 * px + py, c)
                copy(6 * n + kk, landed, (px, py, c)).wait_recv()
                passed.append(copy(6 * n + 3 + kk, landed, sib))
                passed[-1].start()
        for n in range(n_arr):
            for kk, (px, py) in enumerate(chips):
                copy(6 * n + 3 + kk, half(n, 2 * px + py, 1 - c), sib).wait_recv()
        for cp in first + passed:
            cp.wait_send()

    vmem = pl.BlockSpec(memory_space=pltpu.VMEM)
    return pl.pallas_call(
        body, name="weight_gather",
        out_shape=[jax.ShapeDtypeStruct((4,) + a.shape, BF16) for a in shards],
        in_specs=[vmem] * n_arr, out_specs=[vmem] * n_arr,
        scratch_shapes=[pltpu.SemaphoreType.DMA((6 * n_arr,)), pltpu.SemaphoreType.DMA((6 * n_arr,))],
        compiler_params=pltpu.CompilerParams(vmem_limit_bytes=VMEM_LIMIT),
    )(*shards)


def _fwd_pre(x, pos, win, wuq, wkv, pvec):
    seq = x.shape[0]
    t = TOK_TILE

    def body(x_ref, pos_ref, win_ref, wuq_ref, wkv_ref, pv_ref,
             cq_o, ckv_o, gate_o, q_o, k_o, v_o, vt_o, cs_o):
        proj = _dot(x_ref[...].astype(BF16), win_ref[...])
        cq = proj[:, 0:256]
        ckv = proj[:, 256:384]
        kr = proj[:, 384:512]
        cq_o[...] = cq
        ckv_o[...] = ckv
        gate_o[...] = proj[:, 512:D_INR]

        ang = pos_ref[...].astype(F32) * pv_ref[PV_INVF:PV_INVF + 1, 0:HP]
        cos = jnp.cos(ang)
        sin = jnp.sin(ang)
        cs_o[:, 0:HP] = cos
        cs_o[:, HP:2 * HP] = sin
        s1 = sin * pv_ref[PV_M1:PV_M1 + 1, 0:HP]
        s2 = sin * pv_ref[PV_M2:PV_M2 + 1, 0:HP]

        cqh, _ = _rms_stats(cq)
        q_all = _dot((cqh * pv_ref[PV_QG:PV_QG + 1, 0:Q_LORA]).astype(BF16), wuq_ref[...])
        ckvh, _ = _rms_stats(ckv)
        kv_all = _dot((ckvh * pv_ref[PV_KVG:PV_KVG + 1, 0:KV_LORA]).astype(BF16), wkv_ref[...])
        krr = _rope_fwd(kr, cos, s1, s2)
        for h in range(HEADS):
            sl = slice(h * HP, (h + 1) * HP)
            q_o[:, sl] = (_rope_fwd(q_all[:, sl], cos, s1, s2) * SCALE_LOG2E).astype(BF16)
            k_o[:, sl] = (kv_all[:, sl] + krr).astype(BF16)
        val = kv_all[:, HEADS * HP:].astype(BF16)
        v_o[...] = val
        vt_o[...] = val.T

    tile = lambda w: pl.BlockSpec((t, w), lambda i: (i, 0))
    full = lambda a: pl.BlockSpec(a.shape, lambda i: (0,) * a.ndim)
    outs = [(Q_LORA, F32), (KV_LORA, F32), (2048, F32), (HEADS * HP, BF16), (HEADS * HP, BF16), (MLA_W, BF16)]
    per_blk = ATT_BLK // t
    out_specs = [tile(w) for w, _ in outs]
    out_specs += [pl.BlockSpec((None, MLA_W, t), lambda i: (i // per_blk, 0, i % per_blk)), tile(2 * HP)]
    out_shape = [jax.ShapeDtypeStruct((seq, w), d) for w, d in outs]
    out_shape += [jax.ShapeDtypeStruct((seq // ATT_BLK, MLA_W, ATT_BLK), BF16), jax.ShapeDtypeStruct((seq, 2 * HP), F32)]
    return pl.pallas_call(
        body, name="fwd_pre", grid=(seq // t,),
        in_specs=[tile(D_MODEL), tile(1), full(win), full(wuq), full(wkv), full(pvec)],
        out_specs=out_specs, out_shape=out_shape,
        compiler_params=pltpu.CompilerParams(dimension_semantics=("arbitrary",), vmem_limit_bytes=VMEM_LIMIT),
    )(x, pos, win, wuq, wkv, pvec)


def _attn_fwd(q, k, vt):
    seq = q.shape[0]
    b = ATT_BLK
    nq = seq // b

    def body(q_ref, k_ref, vt_ref, o_o, lse_o, m_ref, l_ref, acc_ref, s_even, s_odd):
        i = pl.program_id(1)
        m_ref[...] = jnp.full(m_ref.shape, -jnp.inf, F32)
        l_ref[...] = jnp.zeros(l_ref.shape, F32)
        acc_ref[...] = jnp.zeros(acc_ref.shape, F32)

        def scores(j, s_ref):
            rows = pl.ds(pl.multiple_of(j * b, b), b)
            for a in range(2):
                s_ref[a] = _dot_nt(k_ref[rows, a * HP:(a + 1) * HP], q_ref[:, a * HP:(a + 1) * HP])

        def consume(j, s_ref, masked):
            vt_blk = vt_ref[j]
            for a in range(2):
                st = s_ref[a]
                if masked:
                    ki = lax.broadcasted_iota(jnp.int32, st.shape, 0)
                    qi = lax.broadcasted_iota(jnp.int32, st.shape, 1)
                    st = jnp.where(ki <= qi, st, -jnp.inf)
                m_prev = m_ref[a:a + 1, :]
                m_new = jnp.maximum(m_prev, jnp.max(st, axis=0, keepdims=True))
                alpha = jnp.exp2(m_prev - m_new)
                pt = jnp.exp2(st - m_new)
                l_ref[a:a + 1, :] = alpha * l_ref[a:a + 1, :] + jnp.sum(pt, axis=0, keepdims=True)
                acc_ref[a] = alpha * acc_ref[a] + _dot(vt_blk, pt.astype(BF16))
                m_ref[a:a + 1, :] = m_new

        scores(0, s_even)

        def loop_body(jj, carry):
            scores(2 * jj + 1, s_odd)
            consume(2 * jj, s_even, False)
            scores(2 * jj + 2, s_even)
            consume(2 * jj + 1, s_odd, False)
            return carry

        lax.fori_loop(0, i // 2, loop_body, 0)

        @pl.when(i % 2 == 0)
        def _():
            consume(i, s_even, True)

        @pl.when(i % 2 == 1)
        def _():
            scores(i, s_odd)
            consume(i - 1, s_even, False)
            consume(i, s_odd, True)

        top = lax.broadcasted_iota(jnp.int32, (HP, b), 0) < 64
        o_o[...] = jnp.where(top, acc_ref[0] / l_ref[0:1, :], acc_ref[1] / l_ref[1:2, :]).T
        lse_o[...] = m_ref[0:2, :] + jnp.log2(l_ref[0:2, :])

    return pl.pallas_call(
        body, name="attn_fwd", grid=(PAIRS, nq),
        in_specs=[pl.BlockSpec((b, 2 * HP), lambda p, i: (i, p)),
                  pl.BlockSpec((seq, 2 * HP), lambda p, i: (0, p)),
                  pl.BlockSpec((nq, HP, b), lambda p, i: (0, p, 0))],
        out_specs=[pl.BlockSpec((b, HP), lambda p, i: (i, p)),
                   pl.BlockSpec((None, 2, b), lambda p, i: (p, 0, i))],
        out_shape=[jax.ShapeDtypeStruct((seq, MLA_W), F32),
                   jax.ShapeDtypeStruct((PAIRS, 2, seq), F32)],
        scratch_shapes=[pltpu.VMEM((8, b), F32), pltpu.VMEM((8, b), F32), pltpu.VMEM((2, HP, b), F32),
                        pltpu.VMEM((2, b, b), F32), pltpu.VMEM((2, b, b), F32)],
        compiler_params=pltpu.CompilerParams(dimension_semantics=("arbitrary", "arbitrary"),
                                             vmem_limit_bytes=VMEM_LIMIT),
    )(q, k, vt)


def _post(x, tgt, o, gate, wout, pvec, wt, wtt, bsp):
    seq = x.shape[0]
    t = TOK_TILE
    nt = seq // t

    def body(x_ref, tgt_ref, o_ref, gate_ref, wout_ref, pv_ref, wt_ref, wtt_ref, bsp_ref,
             dh2_o, do_o, dgate_o, gwout_o, gwsp_o, vec_o, sv_ref, dvln_ref, bacc_ref):
        i = pl.program_id(0)

        @pl.when(i == 0)
        def _():
            gwout_o[...] = jnp.zeros_like(gwout_o)
            gwsp_o[...] = jnp.zeros_like(gwsp_o)
            vec_o[...] = jnp.zeros_like(vec_o)
            bacc_ref[...] = jnp.zeros_like(bacc_ref)

        za = gate_ref[:, 0:512]
        u_pre = gate_ref[:, 512:1024]
        v_pre = gate_ref[:, 1024:1536]
        zb = gate_ref[:, 1536:2048]
        sg = pv_ref[PV_SG:PV_SG + 1, 0:GW]
        sb = pv_ref[PV_SB:PV_SB + 1, 0:GW]
        lng = pv_ref[PV_LNG:PV_LNG + 1, :]
        lnb = pv_ref[PV_LNB:PV_LNB + 1, :]
        o = o_ref[...]

        sig_a = _sigmoid(za)
        silu_a = za * sig_a
        u, xh, r, vln, svb, sig_b = _gmlp_fwd(u_pre, v_pre, zb, sg, sb, wt_ref, bsp_ref, sv_ref, t)
        silu_b = zb * sig_b
        sgu = u * svb
        merged = jnp.concatenate([o * silu_a, sgu * silu_b], axis=1).astype(BF16)
        h2 = DN_ALPHA * x_ref[...] + _dot(merged, wout_ref[...])
        xh2, r2 = _ln_stats(h2)
        err = xh2 * lng + lnb - tgt_ref[...]
        d_out = err * (1.0 / D_MODEL)
        vec_o[GV_LNG:GV_LNG + 1, :] += jnp.sum(d_out * xh2, axis=0, keepdims=True)
        vec_o[GV_LNB:GV_LNB + 1, :] += jnp.sum(d_out, axis=0, keepdims=True)
        vec_o[GV_LOSS:GV_LOSS + 1, :] += jnp.sum(err * err, axis=0, keepdims=True) * (0.5 / D_MODEL)

        d_h2 = _ln_bwd(d_out, lng, xh2, r2)
        dh2_o[...] = d_h2
        dh2b = d_h2.astype(BF16)
        gwout_o[...] += _dot_tn(merged, dh2b)
        d_m = _dot_nt(dh2b, wout_ref[...])
        d_oa = d_m[:, 0:512]
        d_ob = d_m[:, 512:1024]
        do_o[...] = (d_oa * silu_a).astype(BF16)
        dgate_o[:, 0:512] = (d_oa * o * (sig_a * (1.0 + za * (1.0 - sig_a)))).astype(BF16)
        dgate_o[:, 1536:2048] = (d_ob * sgu * (sig_b * (1.0 + zb * (1.0 - sig_b)))).astype(BF16)
        d_sgu = d_ob * silu_b
        dgate_o[:, 512:1024] = (d_sgu * svb * _gelu_grad(u_pre)).astype(BF16)
        d_sv = d_sgu * u
        acc = bacc_ref[...]
        for c in range(t // CHUNK):
            acc = acc + d_sv[c * CHUNK:(c + 1) * CHUNK, :]
        bacc_ref[...] = acc
        d_svb = d_sv.astype(BF16)
        for c in range(t // CHUNK):
            for p in range(PAIRS):
                blk = d_svb[c * CHUNK:(c + 1) * CHUNK, p * HP:(p + 1) * HP]
                vblk = vln[c * CHUNK:(c + 1) * CHUNK, p * HP:(p + 1) * HP]
                first = _lane_lt64(blk.shape)
                gwsp_o[2 * p] += _dot_nt(jnp.where(first, blk, jnp.zeros_like(blk)), vblk)
                gwsp_o[2 * p + 1] += _dot_nt(jnp.where(first, jnp.zeros_like(blk), blk), vblk)
        _spatial_mix(wtt_ref, d_svb, dvln_ref, t)
        d_vln = dvln_ref[...]
        vec_o[GV_SG:GV_SG + 1, 0:GW] += jnp.sum(d_vln * xh, axis=0, keepdims=True)
        vec_o[GV_SB:GV_SB + 1, 0:GW] += jnp.sum(d_vln, axis=0, keepdims=True)
        dgate_o[:, 1024:1536] = (_ln_bwd(d_vln, sg, xh, r) * _gelu_grad(v_pre)).astype(BF16)

        @pl.when(i == nt - 1)
        def _():
            tri = (lax.broadcasted_iota(jnp.int32, (CHUNK, CHUNK), 1)
                   <= lax.broadcasted_iota(jnp.int32, (CHUNK, CHUNK), 0))
            for h in range(HEADS):
                gwsp_o[h] = jnp.where(tri, gwsp_o[h], 0.0)
            lane = lax.broadcasted_iota(jnp.int32, (CHUNK, HP), 1)
            res = jnp.zeros((CHUNK, HP), F32)
            for h in range(HEADS):
                p, a = divmod(h, 2)
                blk = bacc_ref[:, p * HP:(p + 1) * HP]
                part = jnp.where(_lane_lt64(blk.shape) == (a == 0), blk, 0.0)
                res = jnp.where(lane == h, jnp.sum(part, axis=-1, keepdims=True), res)
            vec_o[GV_BSP:GV_BSP + HEADS, 0:HP] = res.T[0:HEADS, :]
            lane1 = lax.broadcasted_iota(jnp.int32, (1, D_MODEL), 1)
            total = jnp.sum(vec_o[GV_LOSS:GV_LOSS + 1, :], axis=-1, keepdims=True)
            vec_o[GV_LOSS:GV_LOSS + 1, :] = jnp.where(lane1 == 0, total, 0.0)

    tile = lambda w: pl.BlockSpec((t, w), lambda i: (i, 0))
    full = lambda a: pl.BlockSpec(a.shape, lambda i: (0,) * a.ndim)
    const = lambda s: pl.BlockSpec(s, lambda i: (0,) * len(s))
    return pl.pallas_call(
        body, name="post", grid=(nt,),
        in_specs=[tile(D_MODEL), tile(D_MODEL), tile(MLA_W), tile(2048), full(wout), full(pvec),
                  full(wt), full(wtt), full(bsp)],
        out_specs=[tile(D_MODEL), tile(MLA_W), tile(2048), const((D_MODEL, D_MODEL)),
                   const((HEADS, CHUNK, CHUNK)), const((GV_ROWS, D_MODEL))],
        out_shape=[jax.ShapeDtypeStruct((seq, D_MODEL), F32), jax.ShapeDtypeStruct((seq, MLA_W), BF16),
                   jax.ShapeDtypeStruct((seq, 2048), BF16), jax.ShapeDtypeStruct((D_MODEL, D_MODEL), F32),
                   jax.ShapeDtypeStruct((HEADS, CHUNK, CHUNK), F32), jax.ShapeDtypeStruct((GV_ROWS, D_MODEL), F32)],
        scratch_shapes=[pltpu.VMEM((t, GW), F32), pltpu.VMEM((t, GW), F32), pltpu.VMEM((CHUNK, GW), F32)],
        compiler_params=pltpu.CompilerParams(dimension_semantics=("arbitrary",), vmem_limit_bytes=VMEM_LIMIT),
    )(x, tgt, o, gate, wout, pvec, wt, wtt, bsp)


def _attn_bwd(q, k, v, do, o, lse, cs, pvec):
    seq = q.shape[0]
    b = ATT_BLK
    nq = seq // b

    def body(q_ref, k_ref, v_ref, do_ref, o_ref, lse_ref, cs_ref, pv_ref, dq_o, dk_o, dv_o, dk_acc, dv_acc):
        i = pl.program_id(1)

        @pl.when(i == 0)
        def _():
            dk_acc[...] = jnp.zeros_like(dk_acc)
            dv_acc[...] = jnp.zeros_like(dv_acc)

        first = _lane_lt64((b, HP))
        do = do_ref[...]
        zero = jnp.zeros_like(do)
        dos = [jnp.where(first, do, zero), jnp.where(first, zero, do)]
        prod_t = (do.astype(F32) * o_ref[...]).T
        deltas = [jnp.sum(prod_t[0:64, :], axis=0, keepdims=True),
                  jnp.sum(prod_t[64:128, :], axis=0, keepdims=True)]
        lses = [lse_ref[0:1, :], lse_ref[1:2, :]]
        qs = [q_ref[:, a * HP:(a + 1) * HP] for a in range(2)]

        def step(j, dqs, masked):
            rows = pl.ds(pl.multiple_of(j * b, b), b)
            vb = v_ref[rows, :]
            new_dq = []
            dvs = []
            for a in range(2):
                kb = k_ref[rows, a * HP:(a + 1) * HP]
                pt = jnp.exp2(_dot_nt(kb, qs[a]) - lses[a])
                if masked:
                    ki = lax.broadcasted_iota(jnp.int32, pt.shape, 0)
                    qi = lax.broadcasted_iota(jnp.int32, pt.shape, 1)
                    pt = jnp.where(ki <= qi, pt, 0.0)
                dvs.append(_dot(pt.astype(BF16), do))
                dpt = _dot_nt(vb, dos[a])
                dst = (pt * (dpt - deltas[a])).astype(BF16)
                dk_acc[rows, a * HP:(a + 1) * HP] += _dot(dst, qs[a])
                new_dq.append(dqs[a] + _dot_tn(dst, kb))
            dv_acc[rows, :] += jnp.where(first, dvs[0], dvs[1])
            return tuple(new_dq)

        init = (jnp.zeros((b, HP), F32), jnp.zeros((b, HP), F32))
        dqs = lax.fori_loop(0, i, lambda j, cr: step(j, cr, False), init)
        dqs = step(i, dqs, True)
        cos = cs_ref[:, 0:HP]
        sin = cs_ref[:, HP:2 * HP]
        s1 = sin * pv_ref[PV_M1:PV_M1 + 1, 0:HP]
        s2 = sin * pv_ref[PV_M2:PV_M2 + 1, 0:HP]
        for a in range(2):
            dq_o[:, a * HP:(a + 1) * HP] = _rope_bwd(dqs[a] * SCALE, cos, s1, s2).astype(BF16)

        @pl.when(i == nq - 1)
        def _():
            dk_o[...] = (dk_acc[...] * (SCALE / SCALE_LOG2E)).astype(BF16)
            dv_o[...] = dv_acc[...].astype(BF16)

    return pl.pallas_call(
        body, name="attn_bwd", grid=(PAIRS, nq),
        in_specs=[pl.BlockSpec((b, 2 * HP), lambda p, i: (i, p)),
                  pl.BlockSpec((seq, 2 * HP), lambda p, i: (0, p)),
                  pl.BlockSpec((seq, HP), lambda p, i: (0, p)),
                  pl.BlockSpec((b, HP), lambda p, i: (i, p)),
                  pl.BlockSpec((b, HP), lambda p, i: (i, p)),
                  pl.BlockSpec((None, 2, b), lambda p, i: (p, 0, i)),
                  pl.BlockSpec((b, 2 * HP), lambda p, i: (i, 0)),
                  pl.BlockSpec(pvec.shape, lambda p, i: (0, 0))],
        out_specs=[pl.BlockSpec((b, 2 * HP), lambda p, i: (i, p)),
                   pl.BlockSpec((seq, 2 * HP), lambda p, i: (0, p)),
                   pl.BlockSpec((seq, HP), lambda p, i: (0, p))],
        out_shape=[jax.ShapeDtypeStruct((seq, HEADS * HP), BF16),
                   jax.ShapeDtypeStruct((seq, HEADS * HP), BF16),
                   jax.ShapeDtypeStruct((seq, MLA_W), BF16)],
        scratch_shapes=[pltpu.VMEM((seq, 2 * HP), F32), pltpu.VMEM((seq, HP), F32)],
        compiler_params=pltpu.CompilerParams(dimension_semantics=("arbitrary", "arbitrary"),
                                             vmem_limit_bytes=VMEM_LIMIT),
    )(q, k, v, do, o, lse, cs, pvec)


def _bwd_pre(x, dh2, cq, ckv, cs, dq, dk, dv, dgate, win, wuq, wkv, pvec, gvec):
    seq = x.shape[0]
    t = TOK_TILE

    def body(x_ref, dh2_ref, cq_ref, ckv_ref, cs_ref, dq_ref, dk_ref, dv_ref, dgate_ref,
             win_ref, wuq_ref, wkv_ref, pv_ref, gv_ref, gx_o, gwin_o, gwuq_o, gwkv_o, vec_o):
        i = pl.program_id(0)

        @pl.when(i == 0)
        def _():
            gwin_o[...] = jnp.zeros_like(gwin_o)
            gwuq_o[...] = jnp.zeros_like(gwuq_o)
            gwkv_o[...] = jnp.zeros_like(gwkv_o)
            vec_o[...] = gv_ref[...]

        qg = pv_ref[PV_QG:PV_QG + 1, 0:Q_LORA]
        kvg = pv_ref[PV_KVG:PV_KVG + 1, 0:KV_LORA]
        dq = dq_ref[...]
        cqh, rq = _rms_stats(cq_ref[...])
        d_cqn = _dot_nt(dq, wuq_ref[...])
        gwuq_o[...] += _dot_tn((cqh * qg).astype(BF16), dq)
        vec_o[GV_QG:GV_QG + 1, 0:Q_LORA] += jnp.sum(d_cqn * cqh, axis=0, keepdims=True)
        d_cq = _rms_bwd(d_cqn, qg, cqh, rq)

        dk = dk_ref[...]
        dkv = jnp.concatenate([dk, dv_ref[...]], axis=1)
        ckvh, rkv = _rms_stats(ckv_ref[...])
        d_ckvn = _dot_nt(dkv, wkv_ref[...])
        gwkv_o[...] += _dot_tn((ckvh * kvg).astype(BF16), dkv)
        vec_o[GV_KVG:GV_KVG + 1, 0:KV_LORA] += jnp.sum(d_ckvn * ckvh, axis=0, keepdims=True)
        d_ckv = _rms_bwd(d_ckvn, kvg, ckvh, rkv)

        dks = dk[:, 0:HP].astype(F32)
        for h in range(1, HEADS):
            dks = dks + dk[:, h * HP:(h + 1) * HP].astype(F32)
        cos = cs_ref[:, 0:HP]
        sin = cs_ref[:, HP:2 * HP]
        d_kr = _rope_bwd(dks, cos, sin * pv_ref[PV_M1:PV_M1 + 1, 0:HP], sin * pv_ref[PV_M2:PV_M2 + 1, 0:HP])

        d_proj = jnp.concatenate([d_cq.astype(BF16), d_ckv.astype(BF16), d_kr.astype(BF16), dgate_ref[...]], axis=1)
        gwin_o[...] += _dot_tn(x_ref[...].astype(BF16), d_proj)
        gx_o[...] = DN_ALPHA * dh2_ref[...] + _dot_nt(d_proj, win_ref[...])

    tile = lambda w: pl.BlockSpec((t, w), lambda i: (i, 0))
    full = lambda a: pl.BlockSpec(a.shape, lambda i: (0,) * a.ndim)
    const = lambda s: pl.BlockSpec(s, lambda i: (0,) * len(s))
    return pl.pallas_call(
        body, name="bwd_pre", grid=(seq // t,),
        in_specs=[tile(D_MODEL), tile(D_MODEL), tile(Q_LORA), tile(KV_LORA), tile(2 * HP), tile(HEADS * HP),
                  tile(HEADS * HP), tile(MLA_W), tile(2048), full(win), full(wuq), full(wkv), full(pvec), full(gvec)],
        out_specs=[tile(D_MODEL), const((D_MODEL, D_INR)), const((Q_LORA, HEADS * HP)),
                   const((KV_LORA, HEADS * HP + MLA_W)), const((GV_ROWS, D_MODEL))],
        out_shape=[jax.ShapeDtypeStruct((seq, D_MODEL), F32), jax.ShapeDtypeStruct((D_MODEL, D_INR), F32),
                   jax.ShapeDtypeStruct((Q_LORA, HEADS * HP), F32),
                   jax.ShapeDtypeStruct((KV_LORA, HEADS * HP + MLA_W), F32),
                   jax.ShapeDtypeStruct((GV_ROWS, D_MODEL), F32)],
        compiler_params=pltpu.CompilerParams(dimension_semantics=("arbitrary",), vmem_limit_bytes=VMEM_LIMIT),
    )(x, dh2, cq, ckv, cs, dq, dk, dv, dgate, win, wuq, wkv, pvec, gvec)


def _grad_reduce(gs, gvec):
    n_arr = len(gs)
    n_big = n_arr - 1
    k1 = lambda n, blk: 4 * n + blk
    k2 = lambda n, kk: 4 * n_arr + 3 * n + kk
    k3 = lambda n: 7 * n_arr + n
    k3w = lambda k: 7 * n_arr + n_big + k
    kv = lambda k: 7 * n_arr + n_big + 7 + k
    n_sem = 7 * n_arr + n_big + 14

    def body(*refs):
        g, gv = refs[0:n_arr], refs[n_arr]
        outs, ov = refs[n_arr + 1:2 * n_arr + 1], refs[2 * n_arr + 1]
        r1 = refs[2 * n_arr + 2:3 * n_arr + 2]
        r2 = refs[3 * n_arr + 2:4 * n_arr + 2]
        vbuf, send_sems, recv_sems = refs[4 * n_arr + 2:]
        x, y, c = lax.axis_index("x"), lax.axis_index("y"), lax.axis_index("c")
        j = 2 * x + y
        me = 2 * j + c
        sib = (x, y, 1 - c)
        chips = [(1 - x, y), (x, 1 - y), (1 - x, 1 - y)]
        others = [sib] + [(px, py, pc) for (px, py) in chips for pc in (c, 1 - c)]

        def copy(k, src, dst, to):
            return pltpu.make_async_remote_copy(
                src_ref=src, dst_ref=dst, send_sem=send_sems.at[k], recv_sem=recv_sems.at[k],
                device_id=to, device_id_type=MESH)

        l1 = [copy(k1(n, blk), g[n].at[blk, 1 - c], r1[n].at[blk], sib) for n in range(n_arr) for blk in range(4)]
        lv = [copy(kv(k), gv, vbuf.at[me], to) for k, to in enumerate(others)]
        for cp in l1 + lv:
            cp.start()
        for n in range(n_arr):
            for blk in range(4):
                copy(k1(n, blk), g[n].at[blk, c], r1[n].at[blk], sib).wait_recv()
        for n in range(n_arr):
            for blk in range(4):
                r1[n][blk] = g[n][blk, c] + r1[n][blk]

        l2 = [copy(k2(n, kk), r1[n].at[2 * px + py], r2[n].at[kk], (px, py, c))
              for n in range(n_arr) for kk, (px, py) in enumerate(chips)]
        for cp in l2:
            cp.start()
        for n in range(n_arr):
            for kk in range(3):
                copy(k2(n, kk), r1[n].at[0], r2[n].at[kk], sib).wait_recv()
        for n in range(n_arr):
            red = ((r1[n][j] + r2[n][0]) + r2[n][1]) + r2[n][2]
            if n < n_big:
                outs[n][c] = red
            else:
                outs[n][j, c] = red

        l3 = [copy(k3(n), outs[n].at[c], outs[n].at[c], sib) for n in range(n_big)]
        l3 += [copy(k3w(k), outs[n_big].at[j, c], outs[n_big].at[j, c], to) for k, to in enumerate(others)]
        for cp in l3:
            cp.start()
        for n in range(n_big):
            copy(k3(n), outs[n].at[1 - c], outs[n].at[1 - c], sib).wait_recv()
        for k, (px, py, pc) in enumerate(others):
            landed = outs[n_big].at[2 * px + py, pc]
            copy(k3w(k), landed, landed, (px, py, pc)).wait_recv()
            copy(kv(k), gv, vbuf.at[4 * px + 2 * py + pc], (px, py, pc)).wait_recv()
        vbuf[me] = gv[...]
        total = vbuf[0]
        for d in range(1, 8):
            total = total + vbuf[d]
        ov[...] = total
        for cp in l1 + lv + l2 + l3:
            cp.wait_send()

    vmem = pl.BlockSpec(memory_space=pltpu.VMEM)
    half_shapes = [a.shape[2:] for a in gs]
    out_shape = [jax.ShapeDtypeStruct((2,) + s, F32) for s in half_shapes[:n_big]]
    out_shape += [jax.ShapeDtypeStruct((4, 2) + half_shapes[n_big], F32), jax.ShapeDtypeStruct(gvec.shape, F32)]
    scratch = [pltpu.VMEM((4,) + s, F32) for s in half_shapes] + [pltpu.VMEM((3,) + s, F32) for s in half_shapes]
    scratch += [pltpu.VMEM((8,) + gvec.shape, F32), pltpu.SemaphoreType.DMA((n_sem,)), pltpu.SemaphoreType.DMA((n_sem,))]
    return pl.pallas_call(
        body, name="grad_reduce", out_shape=out_shape,
        in_specs=[vmem] * (n_arr + 1), out_specs=[vmem] * (n_arr + 1), scratch_shapes=scratch,
        compiler_params=pltpu.CompilerParams(vmem_limit_bytes=VMEM_LIMIT),
    )(*gs, gvec)


SMALL_ROWS = ((GV_QG, 1, Q_LORA), (GV_KVG, 1, KV_LORA), (GV_SG, 1, GW), (GV_SB, 1, GW),
              (GV_LNG, 1, D_MODEL), (GV_LNB, 1, D_MODEL), (GV_BSP, HEADS, CHUNK))


def _adam_update(g, w, m, v):
    m_new = ADAM_B1 * m + (1.0 - ADAM_B1) * g
    v_new = ADAM_B2 * v + (1.0 - ADAM_B2) * (g * g)
    m_hat = m_new / (1.0 - ADAM_B1 ** ADAM_STEP)
    v_hat = v_new / (1.0 - ADAM_B2 ** ADAM_STEP)
    return -ADAM_LR * (m_hat / (jnp.sqrt(v_hat) + ADAM_EPS) + ADAM_WD * w), m_new, v_new


def _adamw(g_big, w_big, m_big, v_big, gvec, w_small, m_small, v_small):
    nb, ns = len(g_big), len(w_small)

    def body(*refs):
        it = iter(refs)
        take = lambda n: [next(it) for _ in range(n)]
        g_b, w_b, m_b, v_b = take(nb), take(nb), take(nb), take(nb)
        gv = next(it)
        w_s, m_s, v_s = take(ns), take(ns), take(ns)
        d_bo, m_bo, v_bo = take(nb), take(nb), take(nb)
        g_so, d_so, m_so, v_so = take(ns), take(ns), take(ns), take(ns)
        for n in range(nb):
            d_bo[n][...], m_bo[n][...], v_bo[n][...] = _adam_update(g_b[n][...], w_b[n][...], m_b[n][...], v_b[n][...])
        for n, (row, nrow, width) in enumerate(SMALL_ROWS):
            gs = gv[row:row + nrow, 0:width]
            g_so[n][...] = gs
            d_so[n][...], m_so[n][...], v_so[n][...] = _adam_update(gs, w_s[n][...], m_s[n][...], v_s[n][...])

    vmem = pl.BlockSpec(memory_space=pltpu.VMEM)
    big = [jax.ShapeDtypeStruct(a.shape, F32) for a in w_big]
    small = [jax.ShapeDtypeStruct(a.shape, F32) for a in w_small]
    return pl.pallas_call(
        body, name="adamw", out_shape=big * 3 + small * 4,
        in_specs=[vmem] * (4 * nb + 1 + 3 * ns), out_specs=[vmem] * (3 * nb + 4 * ns),
        compiler_params=pltpu.CompilerParams(vmem_limit_bytes=VMEM_LIMIT),
    )(*g_big, *w_big, *m_big, *v_big, gvec, *w_small, *m_small, *v_small)


def kernel(x, positions, w_in, q_norm_g, w_uq, kv_norm_g, w_ukv, sgu_norm_g, sgu_norm_b, w_spatial, b_spatial, w_out, ln_g, ln_b, loss_target, m_w_in, m_q_norm_g, m_w_uq, m_kv_norm_g, m_w_ukv, m_sgu_norm_g, m_sgu_norm_b, m_w_spatial, m_b_spatial, m_w_out, m_ln_g, m_ln_b, v_w_in, v_q_norm_g, v_w_uq, v_kv_norm_g, v_w_ukv, v_sgu_norm_g, v_sgu_norm_b, v_w_spatial, v_b_spatial, v_w_out, v_ln_g, v_ln_b):
    seq = x.shape[1]
    x2 = x.reshape(seq, D_MODEL)
    tgt = loss_target.reshape(seq, D_MODEL)
    pos = positions.reshape(seq, 1)

    a_in, a_uq, a_ukv, a_out = _weight_gather([w_in, w_uq, w_ukv, w_out])
    w_in_f = jnp.swapaxes(a_in, 0, 1).reshape(D_MODEL, D_IN)
    w_uq_f = jnp.swapaxes(a_uq, 0, 1).reshape(Q_LORA, HEADS * (NOPE + ROPE))
    w_ukv_f = jnp.swapaxes(a_ukv, 0, 1).reshape(KV_LORA, HEADS * (NOPE + VDIM))
    wout = a_out.reshape(D_MODEL, D_MODEL)
    zc = lambda n: jnp.zeros((D_MODEL, n), BF16)
    win = jnp.concatenate([w_in_f[:, 0:384], zc(64), w_in_f[:, 384:416], zc(32), w_in_f[:, 416:D_IN]], axis=1)
    wuq = jnp.pad(w_uq_f.reshape(Q_LORA, HEADS, NOPE + ROPE), ((0, 0), (0, 0), (0, HP - NOPE - ROPE)))
    wuq = wuq.reshape(Q_LORA, HEADS * HP)
    ukv = w_ukv_f.reshape(KV_LORA, HEADS, NOPE + VDIM)
    wk = jnp.pad(ukv[:, :, 0:NOPE], ((0, 0), (0, 0), (0, HP - NOPE))).reshape(KV_LORA, HEADS * HP)
    wkv = jnp.concatenate([wk, ukv[:, :, NOPE:].reshape(KV_LORA, MLA_W)], axis=1)

    lane = jnp.arange(HP)
    half = ROPE // 2
    inv_freq = 1.0 / (ROPE_THETA ** (jnp.arange(half, dtype=F32) / half))
    in_rope = (lane >= NOPE) & (lane < NOPE + ROPE)
    invf = jnp.where(in_rope, inv_freq[(lane - NOPE) % half], 0.0)
    m1 = jnp.where((lane >= NOPE) & (lane < NOPE + half), -1.0, 0.0)
    m2 = jnp.where((lane >= NOPE + half) & (lane < NOPE + ROPE), 1.0, 0.0)
    row = lambda a: jnp.pad(a.astype(F32), (0, D_MODEL - a.shape[0]))
    pvec = jnp.stack([row(q_norm_g), row(kv_norm_g), row(sgu_norm_g), row(sgu_norm_b), row(invf), row(m1),
                      row(m2), row(ln_g), row(ln_b)] + [jnp.zeros((D_MODEL,), F32)] * (PV_ROWS - 9))
    tri = jnp.tril(jnp.ones((CHUNK, CHUNK), dtype=bool))
    wt = jnp.where(tri[None], w_spatial, 0.0).astype(BF16)
    wtt = jnp.swapaxes(wt, 1, 2)
    bsp = jnp.repeat(b_spatial.T, VDIM, axis=1)

    cq, ckv, gate, q, k, v, vt, cs = _fwd_pre(x2, pos, win, wuq, wkv, pvec)
    o, lse = _attn_fwd(q, k, vt)
    dh2, do, dgate, g_wout, g_wsp, gvec = _post(x2, tgt, o, gate, wout, pvec, wt, wtt, bsp)
    dq, dk, dv = _attn_bwd(q, k, v, do, o, lse, cs, pvec)
    gx, g_win, g_wuq, g_wkv, gvec = _bwd_pre(x2, dh2, cq, ckv, cs, dq, dk, dv, dgate, win, wuq, wkv, pvec, gvec)

    g_win_f = jnp.concatenate([g_win[:, 0:384], g_win[:, 448:480], g_win[:, 512:D_INR]], axis=1)
    g_wuq_f = g_wuq.reshape(Q_LORA, HEADS, HP)[:, :, 0:NOPE + ROPE].reshape(Q_LORA, HEADS * (NOPE + ROPE))
    g_k = g_wkv[:, 0:HEADS * HP].reshape(KV_LORA, HEADS, HP)[:, :, 0:NOPE]
    g_v = g_wkv[:, HEADS * HP:].reshape(KV_LORA, HEADS, VDIM)
    g_wukv_f = jnp.concatenate([g_k, g_v], axis=2).reshape(KV_LORA, HEADS * (NOPE + VDIM))

    def by_chip(a):
        rows, cols = a.shape[0], a.shape[1] // 4
        return jnp.swapaxes(a.reshape(rows, 4, cols), 0, 1).reshape(4, 2, rows // 2, cols)

    gs = [by_chip(g_win_f), by_chip(g_wuq_f), by_chip(g_wukv_f), g_wout.reshape(4, 2, 128, D_MODEL),
          g_wsp.reshape(4, 2, CHUNK, CHUNK)]
    r_in, r_uq, r_ukv, r_out, r_wsp, r_vec = _grad_reduce(gs, gvec)

    g_big = [r_in.reshape(w_in.shape), r_uq.reshape(w_uq.shape), r_ukv.reshape(w_ukv.shape),
             r_out.reshape(w_out.shape), r_wsp.reshape(w_spatial.shape)]
    small = lambda qg, kvg, sg, sb, lng, lnb, bs: [qg.reshape(1, -1), kvg.reshape(1, -1), sg.reshape(1, -1),
                                                   sb.reshape(1, -1), lng.reshape(1, -1), lnb.reshape(1, -1), bs]
    res = _adamw(g_big, [w_in, w_uq, w_ukv, w_out, w_spatial], [m_w_in, m_w_uq, m_w_ukv, m_w_out, m_w_spatial],
                 [v_w_in, v_w_uq, v_w_ukv, v_w_out, v_w_spatial], r_vec,
                 small(q_norm_g, kv_norm_g, sgu_norm_g, sgu_norm_b, ln_g, ln_b, b_spatial),
                 small(m_q_norm_g, m_kv_norm_g, m_sgu_norm_g, m_sgu_norm_b, m_ln_g, m_ln_b, m_b_spatial),
                 small(v_q_norm_g, v_kv_norm_g, v_sgu_norm_g, v_sgu_norm_b, v_ln_g, v_ln_b, v_b_spatial))

    def ordered(big, sm):
        vec = lambda n: sm[n].reshape(-1)
        return [big[0], vec(0), big[1], vec(1), big[2], vec(2), vec(3), big[4], sm[6], big[3], vec(4), vec(5)]

    loss = r_vec[GV_LOSS, 0]
    return (loss, gx.reshape(1, seq, D_MODEL), *ordered(g_big, res[15:22]), *ordered(res[0:5], res[22:29]),
            *ordered(res[5:10], res[29:36]), *ordered(res[10:15], res[36:43]))
```

```python
import math

import jax
import jax.numpy as jnp
from jax import lax
from jax.experimental import pallas as pl
from jax.experimental.pallas import tpu as pltpu

F32 = jnp.float32
BF16 = jnp.bfloat16

D_MODEL = 1024
Q_LORA = 256
KV_LORA = 128
HEADS = 8
NOPE = 64
ROPE = 32
VDIM = 64
MLA_W = HEADS * VDIM
GW = 512
CHUNK = 128
HP = 128
PAIRS = HEADS // 2
D_IN = 2464
D_INR = 2560
ROPE_THETA = 10000.0
DN_ALPHA = 2.0 ** 0.25
EPS = 1e-5
SCALE = 1.0 / math.sqrt(NOPE + ROPE)
SCALE_LOG2E = SCALE * 1.4426950408889634
INV_SQRT2 = 0.7071067811865476
INV_SQRT_2PI = 0.3989422804014327

ADAM_LR = 0.001
ADAM_B1 = 0.9
ADAM_B2 = 0.999
ADAM_EPS = 1e-08
ADAM_WD = 0.01
ADAM_STEP = 10

PV_QG, PV_KVG, PV_SG, PV_SB, PV_INVF, PV_M1, PV_M2, PV_LNG, PV_LNB = range(9)
PV_ROWS = 16
GV_QG, GV_KVG, GV_SG, GV_SB, GV_LNG, GV_LNB, GV_LOSS = range(7)
GV_BSP = 8
GV_ROWS = 16

MESH = pl.DeviceIdType.MESH

TOK_TILE = 256
ATT_BLK = 512
VMEM_LIMIT = 56 * 1024 * 1024


def _dot(a, b):
    return jnp.dot(a, b, preferred_element_type=F32)


def _dot_nt(a, b):
    return lax.dot_general(a, b, (((1,), (1,)), ((), ())), preferred_element_type=F32)


def _dot_tn(a, b):
    return lax.dot_general(a, b, (((0,), (0,)), ((), ())), preferred_element_type=F32)


def _sigmoid(z):
    return 1.0 / (1.0 + jnp.exp(-z))


def _gelu(x):
    return 0.5 * x * (1.0 + lax.erf(x * INV_SQRT2))


def _gelu_grad(x):
    return 0.5 * (1.0 + lax.erf(x * INV_SQRT2)) + x * (INV_SQRT_2PI * jnp.exp(-0.5 * x * x))


def _rms_stats(x):
    r = lax.rsqrt(jnp.mean(x * x, axis=-1, keepdims=True) + EPS)
    return x * r, r


def _rms_bwd(dy, g, xh, r):
    dyg = dy * g
    return r * (dyg - xh * jnp.mean(dyg * xh, axis=-1, keepdims=True))


def _ln_stats(x):
    mu = jnp.mean(x, axis=-1, keepdims=True)
    xc = x - mu
    r = lax.rsqrt(jnp.mean(xc * xc, axis=-1, keepdims=True) + EPS)
    return xc * r, r


def _ln_bwd(dy, g, xh, r):
    dxh = dy * g
    return r * (dxh - jnp.mean(dxh, axis=-1, keepdims=True) - xh * jnp.mean(dxh * xh, axis=-1, keepdims=True))


def _rope_fwd(t, c, s1, s2):
    return t * c + pltpu.roll(t, HP - 16, 1) * s1 + pltpu.roll(t, 16, 1) * s2


def _rope_bwd(d, c, s1, s2):
    return d * c + pltpu.roll(d * s1, 16, 1) + pltpu.roll(d * s2, HP - 16, 1)


def _lane_lt64(shape):
    return lax.broadcasted_iota(jnp.int32, shape, len(shape) - 1) < 64


def _spatial_mix(w_ref, src, dst_ref, rows):
    for c in range(rows // CHUNK):
        for p in range(PAIRS):
            blk = src[c * CHUNK:(c + 1) * CHUNK, p * HP:(p + 1) * HP]
            a = _dot(w_ref[2 * p], blk)
            b = _dot(w_ref[2 * p + 1], blk)
            dst_ref[c * CHUNK:(c + 1) * CHUNK, p * HP:(p + 1) * HP] = jnp.where(_lane_lt64(a.shape), a, b)


def _gmlp_fwd(u_pre, v_pre, zb, sg, sb, wt_ref, bsp_ref, sv_ref, rows):
    u = _gelu(u_pre)
    xh, r = _ln_stats(_gelu(v_pre))
    vln = (xh * sg + sb).astype(BF16)
    _spatial_mix(wt_ref, vln, sv_ref, rows)
    bias = bsp_ref[...]
    svb = sv_ref[...] + jnp.concatenate([bias] * (rows // CHUNK), axis=0)
    sig = _sigmoid(zb)
    return u, xh, r, vln, svb, sig


def _weight_gather(shards):
    n_arr = len(shards)

    def body(*refs):
        ins, outs = refs[0:n_arr], refs[n_arr:2 * n_arr]
        send_sems, recv_sems = refs[2 * n_arr:]
        x, y, c = lax.axis_index("x"), lax.axis_index("y"), lax.axis_index("c")
        j = 2 * x + y
        sib = (x, y, 1 - c)
        chips = [(1 - x, y), (x, 1 - y), (1 - x, 1 - y)]
        for n in range(n_arr):
            outs[n][j] = ins[n][...].astype(BF16)

        def half(n, blk, core):
            r = shards[n].shape[0] // 2
            return outs[n].at[blk, pl.ds(pl.multiple_of(core * r, 16), r), :]

        def copy(k, ref, to):
            return pltpu.make_async_remote_copy(
                src_ref=ref, dst_ref=ref, send_sem=send_sems.at[k], recv_sem=recv_sems.at[k],
                device_id=to, device_id_type=MESH)

        first = [copy(6 * n + kk, half(n, j, c), (px, py, c))
                 for n in range(n_arr) for kk, (px, py) in enumerate(chips)]
        for cp in first:
            cp.start()
        passed = []
        for n in range(n_arr):
            for kk, (px, py) in enumerate(chips):
                landed = half(n, 2 * px + py, c)
                copy(6 * n + kk, landed, (px, py, c)).wait_recv()
                passed.append(copy(6 * n + 3 + kk, landed, sib))
                passed[-1].start()
        for n in range(n_arr):
            for kk, (px, py) in enumerate(chips):
                copy(6 * n + 3 + kk, half(n, 2 * px + py, 1 - c), sib).wait_recv()
        for cp in first + passed:
            cp.wait_send()

    vmem = pl.BlockSpec(memory_space=pltpu.VMEM)
    return pl.pallas_call(
        body, name="weight_gather",
        out_shape=[jax.ShapeDtypeStruct((4,) + a.shape, BF16) for a in shards],
        in_specs=[vmem] * n_arr, out_specs=[vmem] * n_arr,
        scratch_shapes=[pltpu.SemaphoreType.DMA((6 * n_arr,)), pltpu.SemaphoreType.DMA((6 * n_arr,))],
        compiler_params=pltpu.CompilerParams(vmem_limit_bytes=VMEM_LIMIT),
    )(*shards)


def _fwd_pre(x, pos, win, wuq, wkv, pvec):
    seq = x.shape[0]
    t = TOK_TILE

    def body(x_ref, pos_ref, win_ref, wuq_ref, wkv_ref, pv_ref,
             cq_o, ckv_o, gate_o, q_o, k_o, v_o, qt_o, kt_o, vt_o, cs_o):
        proj = _dot(x_ref[...].astype(BF16), win_ref[...])
        cq = proj[:, 0:256]
        ckv = proj[:, 256:384]
        kr = proj[:, 384:512]
        cq_o[...] = cq
        ckv_o[...] = ckv
        gate_o[...] = proj[:, 512:D_INR]

        ang = pos_ref[...].astype(F32) * pv_ref[PV_INVF:PV_INVF + 1, 0:HP]
        cos = jnp.cos(ang)
        sin = jnp.sin(ang)
        cs_o[:, 0:HP] = cos
        cs_o[:, HP:2 * HP] = sin
        s1 = sin * pv_ref[PV_M1:PV_M1 + 1, 0:HP]
        s2 = sin * pv_ref[PV_M2:PV_M2 + 1, 0:HP]

        cqh, _ = _rms_stats(cq)
        q_all = _dot((cqh * pv_ref[PV_QG:PV_QG + 1, 0:Q_LORA]).astype(BF16), wuq_ref[...])
        ckvh, _ = _rms_stats(ckv)
        kv_all = _dot((ckvh * pv_ref[PV_KVG:PV_KVG + 1, 0:KV_LORA]).astype(BF16), wkv_ref[...])
        krr = _rope_fwd(kr, cos, s1, s2)
        for h in range(HEADS):
            sl = slice(h * HP, (h + 1) * HP)
            q_o[:, sl] = (_rope_fwd(q_all[:, sl], cos, s1, s2) * SCALE_LOG2E).astype(BF16)
            k_o[:, sl] = (kv_all[:, sl] + krr).astype(BF16)
        val = kv_all[:, HEADS * HP:].astype(BF16)
        v_o[...] = val
        vt_o[...] = val.T
        qt_o[...] = q_o[...].T
        kt_o[...] = k_o[...].T

    tile = lambda w: pl.BlockSpec((t, w), lambda i: (i, 0))
    full = lambda a: pl.BlockSpec(a.shape, lambda i: (0,) * a.ndim)
    outs = [(Q_LORA, F32), (KV_LORA, F32), (2048, F32), (HEADS * HP, BF16), (HEADS * HP, BF16), (MLA_W, BF16)]
    per_blk = ATT_BLK // t
    out_specs = [tile(w) for w, _ in outs]
    by_blk = lambda w: pl.BlockSpec((None, w, t), lambda i: (i // per_blk, 0, i % per_blk))
    out_specs += [pl.BlockSpec((HEADS * HP, t), lambda i: (0, i)), by_blk(HEADS * HP), by_blk(MLA_W), tile(2 * HP)]
    out_shape = [jax.ShapeDtypeStruct((seq, w), d) for w, d in outs]
    out_shape += [jax.ShapeDtypeStruct((HEADS * HP, seq), BF16),
                  jax.ShapeDtypeStruct((seq // ATT_BLK, HEADS * HP, ATT_BLK), BF16),
                  jax.ShapeDtypeStruct((seq // ATT_BLK, MLA_W, ATT_BLK), BF16), jax.ShapeDtypeStruct((seq, 2 * HP), F32)]
    return pl.pallas_call(
        body, name="fwd_pre", grid=(seq // t,),
        in_specs=[tile(D_MODEL), tile(1), full(win), full(wuq), full(wkv), full(pvec)],
        out_specs=out_specs, out_shape=out_shape,
        compiler_params=pltpu.CompilerParams(dimension_semantics=("arbitrary",), vmem_limit_bytes=VMEM_LIMIT),
    )(x, pos, win, wuq, wkv, pvec)


def _attn_fwd(q, k, vt):
    seq = q.shape[0]
    b = ATT_BLK
    nq = seq // b

    def body(q_ref, k_ref, vt_ref, o_o, lse_o, m_ref, l_ref, acc_ref, s_even, s_odd):
        i = pl.program_id(1)
        m_ref[...] = jnp.full(m_ref.shape, -jnp.inf, F32)
        l_ref[...] = jnp.zeros(l_ref.shape, F32)
        acc_ref[...] = jnp.zeros(acc_ref.shape, F32)

        def scores(j, s_ref):
            rows = pl.ds(pl.multiple_of(j * b, b), b)
            for a in range(2):
                s_ref[a] = _dot_nt(k_ref[rows, a * HP:(a + 1) * HP], q_ref[:, a * HP:(a + 1) * HP])

        def consume(j, s_ref, masked):
            vt_blk = vt_ref[j]
            for a in range(2):
                st = s_ref[a]
                if masked:
                    ki = lax.broadcasted_iota(jnp.int32, st.shape, 0)
                    qi = lax.broadcasted_iota(jnp.int32, st.shape, 1)
                    st = jnp.where(ki <= qi, st, -jnp.inf)
                m_prev = m_ref[a:a + 1, :]
                m_new = jnp.maximum(m_prev, jnp.max(st, axis=0, keepdims=True))
                alpha = jnp.exp2(m_prev - m_new)
                pt = jnp.exp2(st - m_new)
                l_ref[a:a + 1, :] = alpha * l_ref[a:a + 1, :] + jnp.sum(pt, axis=0, keepdims=True)
                acc_ref[a] = alpha * acc_ref[a] + _dot(vt_blk, pt.astype(BF16))
                m_ref[a:a + 1, :] = m_new

        scores(0, s_even)

        def loop_body(jj, carry):
            scores(2 * jj + 1, s_odd)
            consume(2 * jj, s_even, False)
            scores(2 * jj + 2, s_even)
            consume(2 * jj + 1, s_odd, False)
            return carry

        lax.fori_loop(0, i // 2, loop_body, 0)

        @pl.when(i % 2 == 0)
        def _():
            consume(i, s_even, True)

        @pl.when(i % 2 == 1)
        def _():
            scores(i, s_odd)
            consume(i - 1, s_even, False)
            consume(i, s_odd, True)

        top = lax.broadcasted_iota(jnp.int32, (HP, b), 0) < 64
        o_o[...] = jnp.where(top, acc_ref[0] / l_ref[0:1, :], acc_ref[1] / l_ref[1:2, :]).T
        lse_o[...] = m_ref[0:2, :] + jnp.log2(l_ref[0:2, :])

    return pl.pallas_call(
        body, name="attn_fwd", grid=(PAIRS, nq),
        in_specs=[pl.BlockSpec((b, 2 * HP), lambda p, i: (i, p)),
                  pl.BlockSpec((seq, 2 * HP), lambda p, i: (0, p)),
                  pl.BlockSpec((nq, HP, b), lambda p, i: (0, p, 0))],
        out_specs=[pl.BlockSpec((b, HP), lambda p, i: (i, p)),
                   pl.BlockSpec((None, 2, b), lambda p, i: (p, 0, i))],
        out_shape=[jax.ShapeDtypeStruct((seq, MLA_W), F32),
                   jax.ShapeDtypeStruct((PAIRS, 2, seq), F32)],
        scratch_shapes=[pltpu.VMEM((8, b), F32), pltpu.VMEM((8, b), F32), pltpu.VMEM((2, HP, b), F32),
                        pltpu.VMEM((2, b, b), F32), pltpu.VMEM((2, b, b), F32)],
        compiler_params=pltpu.CompilerParams(dimension_semantics=("arbitrary", "arbitrary"),
                                             vmem_limit_bytes=VMEM_LIMIT),
    )(q, k, vt)


def _post(x, tgt, o, gate, wout, pvec, wt, wtt, bsp):
    seq = x.shape[0]
    t = TOK_TILE
    nt = seq // t

    def body(x_ref, tgt_ref, o_ref, gate_ref, wout_ref, pv_ref, wt_ref, wtt_ref, bsp_ref,
             dh2_o, dot_o, dl_o, dgate_o, gwout_o, gwsp_o, vec_o, sv_ref, dvln_ref, bacc_ref):
        i = pl.program_id(0)

        @pl.when(i == 0)
        def _():
            gwout_o[...] = jnp.zeros_like(gwout_o)
            gwsp_o[...] = jnp.zeros_like(gwsp_o)
            vec_o[...] = jnp.zeros_like(vec_o)
            bacc_ref[...] = jnp.zeros_like(bacc_ref)

        za = gate_ref[:, 0:512]
        u_pre = gate_ref[:, 512:1024]
        v_pre = gate_ref[:, 1024:1536]
        zb = gate_ref[:, 1536:2048]
        sg = pv_ref[PV_SG:PV_SG + 1, 0:GW]
        sb = pv_ref[PV_SB:PV_SB + 1, 0:GW]
        lng = pv_ref[PV_LNG:PV_LNG + 1, :]
        lnb = pv_ref[PV_LNB:PV_LNB + 1, :]
        o = o_ref[...]

        sig_a = _sigmoid(za)
        silu_a = za * sig_a
        u, xh, r, vln, svb, sig_b = _gmlp_fwd(u_pre, v_pre, zb, sg, sb, wt_ref, bsp_ref, sv_ref, t)
        silu_b = zb * sig_b
        sgu = u * svb
        merged = jnp.concatenate([o * silu_a, sgu * silu_b], axis=1).astype(BF16)
        h2 = DN_ALPHA * x_ref[...] + _dot(merged, wout_ref[...])
        xh2, r2 = _ln_stats(h2)
        err = xh2 * lng + lnb - tgt_ref[...]
        d_out = err * (1.0 / D_MODEL)
        vec_o[GV_LNG:GV_LNG + 1, :] += jnp.sum(d_out * xh2, axis=0, keepdims=True)
        vec_o[GV_LNB:GV_LNB + 1, :] += jnp.sum(d_out, axis=0, keepdims=True)
        vec_o[GV_LOSS:GV_LOSS + 1, :] += jnp.sum(err * err, axis=0, keepdims=True) * (0.5 / D_MODEL)

        d_h2 = _ln_bwd(d_out, lng, xh2, r2)
        dh2_o[...] = d_h2
        dh2b = d_h2.astype(BF16)
        gwout_o[...] += _dot_tn(merged, dh2b)
        d_m = _dot_nt(dh2b, wout_ref[...])
        d_oa = d_m[:, 0:512]
        d_ob = d_m[:, 512:1024]
        d_attn = d_oa * silu_a
        dot_o[...] = d_attn.astype(BF16).T
        prod_t = (d_attn * o).T
        for h in range(HEADS):
            dl_o[h:h + 1, :] = jnp.sum(prod_t[h * VDIM:(h + 1) * VDIM, :], axis=0, keepdims=True)
        dgate_o[:, 0:512] = (d_oa * o * (sig_a * (1.0 + za * (1.0 - sig_a)))).astype(BF16)
        dgate_o[:, 1536:2048] = (d_ob * sgu * (sig_b * (1.0 + zb * (1.0 - sig_b)))).astype(BF16)
        d_sgu = d_ob * silu_b
        dgate_o[:, 512:1024] = (d_sgu * svb * _gelu_grad(u_pre)).astype(BF16)
        d_sv = d_sgu * u
        acc = bacc_ref[...]
        for c in range(t // CHUNK):
            acc = acc + d_sv[c * CHUNK:(c + 1) * CHUNK, :]
        bacc_ref[...] = acc
        d_svb = d_sv.astype(BF16)
        for c in range(t // CHUNK):
            for p in range(PAIRS):
                blk = d_svb[c * CHUNK:(c + 1) * CHUNK, p * HP:(p + 1) * HP]
                vblk = vln[c * CHUNK:(c + 1) * CHUNK, p * HP:(p + 1) * HP]
                first = _lane_lt64(blk.shape)
                gwsp_o[2 * p] += _dot_nt(jnp.where(first, blk, jnp.zeros_like(blk)), vblk)
                gwsp_o[2 * p + 1] += _dot_nt(jnp.where(first, jnp.zeros_like(blk), blk), vblk)
        _spatial_mix(wtt_ref, d_svb, dvln_ref, t)
        d_vln = dvln_ref[...]
        vec_o[GV_SG:GV_SG + 1, 0:GW] += jnp.sum(d_vln * xh, axis=0, keepdims=True)
        vec_o[GV_SB:GV_SB + 1, 0:GW] += jnp.sum(d_vln, axis=0, keepdims=True)
        dgate_o[:, 1024:1536] = (_ln_bwd(d_vln, sg, xh, r) * _gelu_grad(v_pre)).astype(BF16)

        @pl.when(i == nt - 1)
        def _():
            tri = (lax.broadcasted_iota(jnp.int32, (CHUNK, CHUNK), 1)
                   <= lax.broadcasted_iota(jnp.int32, (CHUNK, CHUNK), 0))
            for h in range(HEADS):
                gwsp_o[h] = jnp.where(tri, gwsp_o[h], 0.0)
            lane = lax.broadcasted_iota(jnp.int32, (CHUNK, HP), 1)
            res = jnp.zeros((CHUNK, HP), F32)
            for h in range(HEADS):
                p, a = divmod(h, 2)
                blk = bacc_ref[:, p * HP:(p + 1) * HP]
                part = jnp.where(_lane_lt64(blk.shape) == (a == 0), blk, 0.0)
                res = jnp.where(lane == h, jnp.sum(part, axis=-1, keepdims=True), res)
            vec_o[GV_BSP:GV_BSP + HEADS, 0:HP] = res.T[0:HEADS, :]
            lane1 = lax.broadcasted_iota(jnp.int32, (1, D_MODEL), 1)
            total = jnp.sum(vec_o[GV_LOSS:GV_LOSS + 1, :], axis=-1, keepdims=True)
            vec_o[GV_LOSS:GV_LOSS + 1, :] = jnp.where(lane1 == 0, total, 0.0)

    tile = lambda w: pl.BlockSpec((t, w), lambda i: (i, 0))
    full = lambda a: pl.BlockSpec(a.shape, lambda i: (0,) * a.ndim)
    const = lambda s: pl.BlockSpec(s, lambda i: (0,) * len(s))
    return pl.pallas_call(
        body, name="post", grid=(nt,),
        in_specs=[tile(D_MODEL), tile(D_MODEL), tile(MLA_W), tile(2048), full(wout), full(pvec),
                  full(wt), full(wtt), full(bsp)],
        out_specs=[tile(D_MODEL), pl.BlockSpec((MLA_W, t), lambda i: (0, i)), pl.BlockSpec((HEADS, t), lambda i: (0, i)),
                   tile(2048), const((D_MODEL, D_MODEL)),
                   const((HEADS, CHUNK, CHUNK)), const((GV_ROWS, D_MODEL))],
        out_shape=[jax.ShapeDtypeStruct((seq, D_MODEL), F32), jax.ShapeDtypeStruct((MLA_W, seq), BF16),
                   jax.ShapeDtypeStruct((HEADS, seq), F32),
                   jax.ShapeDtypeStruct((seq, 2048), BF16), jax.ShapeDtypeStruct((D_MODEL, D_MODEL), F32),
                   jax.ShapeDtypeStruct((HEADS, CHUNK, CHUNK), F32), jax.ShapeDtypeStruct((GV_ROWS, D_MODEL), F32)],
        scratch_shapes=[pltpu.VMEM((t, GW), F32), pltpu.VMEM((t, GW), F32), pltpu.VMEM((CHUNK, GW), F32)],
        compiler_params=pltpu.CompilerParams(dimension_semantics=("arbitrary",), vmem_limit_bytes=VMEM_LIMIT),
    )(x, tgt, o, gate, wout, pvec, wt, wtt, bsp)


def _attn_bwd(qt, k, kt, v, dot, lse, delta, cs, pvec):
    seq = k.shape[0]
    b = ATT_BLK
    nq = seq // b

    def body(qt_ref, k_ref, kt_ref, v_ref, dot_ref, lse_ref, dl_ref, cs_ref, pv_ref,
             dq_o, dk_o, dv_o, dkt_acc, dvt_acc, dqt_acc, sd_even, sd_odd):
        i = pl.program_id(1)

        @pl.when(i == 0)
        def _():
            dkt_acc[...] = jnp.zeros_like(dkt_acc)
            dvt_acc[...] = jnp.zeros_like(dvt_acc)

        dqt_acc[...] = jnp.zeros_like(dqt_acc)
        top = lax.broadcasted_iota(jnp.int32, (HP, b), 0) < 64
        do_t = dot_ref[...]
        zero = jnp.zeros_like(do_t)
        do_ts = [jnp.where(top, do_t, zero), jnp.where(top, zero, do_t)]
        q_ts = [qt_ref[a * HP:(a + 1) * HP, :] for a in range(2)]

        def scores(j, sd_ref):
            rows = pl.ds(pl.multiple_of(j * b, b), b)
            vb = v_ref[rows, :]
            for a in range(2):
                sd_ref[a] = _dot(k_ref[rows, a * HP:(a + 1) * HP], q_ts[a])
                sd_ref[2 + a] = _dot(vb, do_ts[a])

        def consume(j, sd_ref, masked):
            dvts = []
            for a in range(2):
                pt = jnp.exp2(sd_ref[a] - lse_ref[a:a + 1, :])
                if masked:
                    ki = lax.broadcasted_iota(jnp.int32, pt.shape, 0)
                    qi = lax.broadcasted_iota(jnp.int32, pt.shape, 1)
                    pt = jnp.where(ki <= qi, pt, 0.0)
                dst = (pt * (sd_ref[2 + a] - dl_ref[a:a + 1, :])).astype(BF16)
                dvts.append(_dot_nt(do_t, pt.astype(BF16)))
                dkt_acc[j, a] += _dot_nt(q_ts[a], dst)
                dqt_acc[a] += _dot(kt_ref[j, a * HP:(a + 1) * HP, :], dst)
            dvt_acc[j] += jnp.where(top, dvts[0], dvts[1])

        scores(0, sd_even)

        def loop_body(jj, carry):
            scores(2 * jj + 1, sd_odd)
            consume(2 * jj, sd_even, False)
            scores(2 * jj + 2, sd_even)
            consume(2 * jj + 1, sd_odd, False)
            return carry

        lax.fori_loop(0, i // 2, loop_body, 0)

        @pl.when(i % 2 == 0)
        def _():
            consume(i, sd_even, True)

        @pl.when(i % 2 == 1)
        def _():
            scores(i, sd_odd)
            consume(i - 1, sd_even, False)
            consume(i, sd_odd, True)

        cos = cs_ref[:, 0:HP]
        sin = cs_ref[:, HP:2 * HP]
        s1 = sin * pv_ref[PV_M1:PV_M1 + 1, 0:HP]
        s2 = sin * pv_ref[PV_M2:PV_M2 + 1, 0:HP]
        for a in range(2):
            dq_o[:, a * HP:(a + 1) * HP] = _rope_bwd(dqt_acc[a].T * SCALE, cos, s1, s2).astype(BF16)

        @pl.when(i == nq - 1)
        def _():
            def flush(jb, carry):
                rows = pl.ds(pl.multiple_of(jb * b, b), b)
                for a in range(2):
                    dk_o[rows, a * HP:(a + 1) * HP] = (dkt_acc[jb, a].T * (SCALE / SCALE_LOG2E)).astype(BF16)
                dv_o[rows, :] = dvt_acc[jb].T.astype(BF16)
                return carry

            lax.fori_loop(0, nq, flush, 0)

    once = pl.Buffered(1)
    return pl.pallas_call(
        body, name="attn_bwd", grid=(PAIRS, nq),
        in_specs=[pl.BlockSpec((2 * HP, b), lambda p, i: (p, i)),
                  pl.BlockSpec((seq, 2 * HP), lambda p, i: (0, p), pipeline_mode=once),
                  pl.BlockSpec((nq, 2 * HP, b), lambda p, i: (0, p, 0), pipeline_mode=once),
                  pl.BlockSpec((seq, HP), lambda p, i: (0, p), pipeline_mode=once),
                  pl.BlockSpec((HP, b), lambda p, i: (p, i)),
                  pl.BlockSpec((None, 2, b), lambda p, i: (p, 0, i)),
                  pl.BlockSpec((None, 2, b), lambda p, i: (p, 0, i)),
                  pl.BlockSpec((b, 2 * HP), lambda p, i: (i, 0)),
                  pl.BlockSpec(pvec.shape, lambda p, i: (0, 0))],
        out_specs=[pl.BlockSpec((b, 2 * HP), lambda p, i: (i, p)),
                   pl.BlockSpec((seq, 2 * HP), lambda p, i: (0, p)),
                   pl.BlockSpec((seq, HP), lambda p, i: (0, p))],
        out_shape=[jax.ShapeDtypeStruct((seq, HEADS * HP), BF16),
                   jax.ShapeDtypeStruct((seq, HEADS * HP), BF16),
                   jax.ShapeDtypeStruct((seq, MLA_W), BF16)],
        scratch_shapes=[pltpu.VMEM((nq, 2, HP, b), F32), pltpu.VMEM((nq, HP, b), F32), pltpu.VMEM((2, HP, b), F32),
                        pltpu.VMEM((4, b, b), F32), pltpu.VMEM((4, b, b), F32)],
        compiler_params=pltpu.CompilerParams(dimension_semantics=("arbitrary", "arbitrary"),
                                             vmem_limit_bytes=VMEM_LIMIT),
    )(qt, k, kt, v, dot, lse, delta, cs, pvec)


def _bwd_pre(x, dh2, cq, ckv, cs, dq, dk, dv, dgate, win, wuq, wkv, pvec, gvec):
    seq = x.shape[0]
    t = TOK_TILE

    def body(x_ref, dh2_ref, cq_ref, ckv_ref, cs_ref, dq_ref, dk_ref, dv_ref, dgate_ref,
             win_ref, wuq_ref, wkv_ref, pv_ref, gv_ref, gx_o, gwin_o, gwuq_o, gwkv_o, vec_o):
        i = pl.program_id(0)

        @pl.when(i == 0)
        def _():
            gwin_o[...] = jnp.zeros_like(gwin_o)
            gwuq_o[...] = jnp.zeros_like(gwuq_o)
            gwkv_o[...] = jnp.zeros_like(gwkv_o)
            vec_o[...] = gv_ref[...]

        qg = pv_ref[PV_QG:PV_QG + 1, 0:Q_LORA]
        kvg = pv_ref[PV_KVG:PV_KVG + 1, 0:KV_LORA]
        dq = dq_ref[...]
        cqh, rq = _rms_stats(cq_ref[...])
        d_cqn = _dot_nt(dq, wuq_ref[...])
        gwuq_o[...] += _dot_tn((cqh * qg).astype(BF16), dq)
        vec_o[GV_QG:GV_QG + 1, 0:Q_LORA] += jnp.sum(d_cqn * cqh, axis=0, keepdims=True)
        d_cq = _rms_bwd(d_cqn, qg, cqh, rq)

        dk = dk_ref[...]
        dkv = jnp.concatenate([dk, dv_ref[...]], axis=1)
        ckvh, rkv = _rms_stats(ckv_ref[...])
        d_ckvn = _dot_nt(dkv, wkv_ref[...])
        gwkv_o[...] += _dot_tn((ckvh * kvg).astype(BF16), dkv)
        vec_o[GV_KVG:GV_KVG + 1, 0:KV_LORA] += jnp.sum(d_ckvn * ckvh, axis=0, keepdims=True)
        d_ckv = _rms_bwd(d_ckvn, kvg, ckvh, rkv)

        dks = dk[:, 0:HP].astype(F32)
        for h in range(1, HEADS):
            dks = dks + dk[:, h * HP:(h + 1) * HP].astype(F32)
        cos = cs_ref[:, 0:HP]
        sin = cs_ref[:, HP:2 * HP]
        d_kr = _rope_bwd(dks, cos, sin * pv_ref[PV_M1:PV_M1 + 1, 0:HP], sin * pv_ref[PV_M2:PV_M2 + 1, 0:HP])

        d_proj = jnp.concatenate([d_cq.astype(BF16), d_ckv.astype(BF16), d_kr.astype(BF16), dgate_ref[...]], axis=1)
        gwin_o[...] += _dot_tn(x_ref[...].astype(BF16), d_proj)
        gx_o[...] = DN_ALPHA * dh2_ref[...] + _dot_nt(d_proj, win_ref[...])

    tile = lambda w: pl.BlockSpec((t, w), lambda i: (i, 0))
    full = lambda a: pl.BlockSpec(a.shape, lambda i: (0,) * a.ndim)
    const = lambda s: pl.BlockSpec(s, lambda i: (0,) * len(s))
    return pl.pallas_call(
        body, name="bwd_pre", grid=(seq // t,),
        in_specs=[tile(D_MODEL), tile(D_MODEL), tile(Q_LORA), tile(KV_LORA), tile(2 * HP), tile(HEADS * HP),
                  tile(HEADS * HP), tile(MLA_W), tile(2048), full(win), full(wuq), full(wkv), full(pvec), full(gvec)],
        out_specs=[tile(D_MODEL), const((D_MODEL, D_INR)), const((Q_LORA, HEADS * HP)),
                   const((KV_LORA, HEADS * HP + MLA_W)), const((GV_ROWS, D_MODEL))],
        out_shape=[jax.ShapeDtypeStruct((seq, D_MODEL), F32), jax.ShapeDtypeStruct((D_MODEL, D_INR), F32),
                   jax.ShapeDtypeStruct((Q_LORA, HEADS * HP), F32),
                   jax.ShapeDtypeStruct((KV_LORA, HEADS * HP + MLA_W), F32),
                   jax.ShapeDtypeStruct((GV_ROWS, D_MODEL), F32)],
        compiler_params=pltpu.CompilerParams(dimension_semantics=("arbitrary",), vmem_limit_bytes=VMEM_LIMIT),
    )(x, dh2, cq, ckv, cs, dq, dk, dv, dgate, win, wuq, wkv, pvec, gvec)


def _grad_reduce(gs, gvec):
    n_arr = len(gs)
    n_big = n_arr - 1
    k1 = lambda n, blk: 4 * n + blk
    k2 = lambda n, kk: 4 * n_arr + 3 * n + kk
    k3 = lambda n: 7 * n_arr + n
    k3w = lambda k: 7 * n_arr + n_big + k
    kv = lambda k: 7 * n_arr + n_big + 7 + k
    n_sem = 7 * n_arr + n_big + 14

    def body(*refs):
        g, gv = refs[0:n_arr], refs[n_arr]
        outs, ov = refs[n_arr + 1:2 * n_arr + 1], refs[2 * n_arr + 1]
        r1 = refs[2 * n_arr + 2:3 * n_arr + 2]
        r2 = refs[3 * n_arr + 2:4 * n_arr + 2]
        vbuf, send_sems, recv_sems = refs[4 * n_arr + 2:]
        x, y, c = lax.axis_index("x"), lax.axis_index("y"), lax.axis_index("c")
        j = 2 * x + y
        me = 2 * j + c
        sib = (x, y, 1 - c)
        chips = [(1 - x, y), (x, 1 - y), (1 - x, 1 - y)]
        others = [sib] + [(px, py, pc) for (px, py) in chips for pc in (c, 1 - c)]

        def copy(k, src, dst, to):
            return pltpu.make_async_remote_copy(
                src_ref=src, dst_ref=dst, send_sem=send_sems.at[k], recv_sem=recv_sems.at[k],
                device_id=to, device_id_type=MESH)

        l1 = [copy(k1(n, blk), g[n].at[blk, 1 - c], r1[n].at[blk], sib) for n in range(n_arr) for blk in range(4)]
        lv = [copy(kv(k), gv, vbuf.at[me], to) for k, to in enumerate(others)]
        for cp in l1 + lv:
            cp.start()
        for n in range(n_arr):
            for blk in range(4):
                copy(k1(n, blk), g[n].at[blk, c], r1[n].at[blk], sib).wait_recv()
        for n in range(n_arr):
            for blk in range(4):
                r1[n][blk] = g[n][blk, c] + r1[n][blk]

        l2 = [copy(k2(n, kk), r1[n].at[2 * px + py], r2[n].at[kk], (px, py, c))
              for n in range(n_arr) for kk, (px, py) in enumerate(chips)]
        for cp in l2:
            cp.start()
        for n in range(n_arr):
            for kk in range(3):
                copy(k2(n, kk), r1[n].at[0], r2[n].at[kk], sib).wait_recv()
        for n in range(n_arr):
            red = ((r1[n][j] + r2[n][0]) + r2[n][1]) + r2[n][2]
            if n < n_big:
                outs[n][c] = red
            else:
                outs[n][j, c] = red

        l3 = [copy(k3(n), outs[n].at[c], outs[n].at[c], sib) for n in range(n_big)]
        l3 += [copy(k3w(k), outs[n_big].at[j, c], outs[n_big].at[j, c], to) for k, to in enumerate(others)]
        for cp in l3:
            cp.start()
        for n in range(n_big):
            copy(k3(n), outs[n].at[1 - c], outs[n].at[1 - c], sib).wait_recv()
        for k, (px, py, pc) in enumerate(others):
            landed = outs[n_big].at[2 * px + py, pc]
            copy(k3w(k), landed, landed, (px, py, pc)).wait_recv()
            copy(kv(k), gv, vbuf.at[4 * px + 2 * py + pc], (px, py, pc)).wait_recv()
        vbuf[me] = gv[...]
        total = vbuf[0]
        for d in range(1, 8):
            total = total + vbuf[d]
        ov[...] = total
        for cp in l1 + lv + l2 + l3:
            cp.wait_send()

    vmem = pl.BlockSpec(memory_space=pltpu.VMEM)
    half_shapes = [a.shape[2:] for a in gs]
    out_shape = [jax.ShapeDtypeStruct((2,) + s, F32) for s in half_shapes[:n_big]]
    out_shape += [jax.ShapeDtypeStruct((4, 2) + half_shapes[n_big], F32), jax.ShapeDtypeStruct(gvec.shape, F32)]
    scratch = [pltpu.VMEM((4,) + s, F32) for s in half_shapes] + [pltpu.VMEM((3,) + s, F32) for s in half_shapes]
    scratch += [pltpu.VMEM((8,) + gvec.shape, F32), pltpu.SemaphoreType.DMA((n_sem,)), pltpu.SemaphoreType.DMA((n_sem,))]
    return pl.pallas_call(
        body, name="grad_reduce", out_shape=out_shape,
        in_specs=[vmem] * (n_arr + 1), out_specs=[vmem] * (n_arr + 1), scratch_shapes=scratch,
        compiler_params=pltpu.CompilerParams(vmem_limit_bytes=VMEM_LIMIT),
    )(*gs, gvec)


SMALL_ROWS = ((GV_QG, 1, Q_LORA), (GV_KVG, 1, KV_LORA), (GV_SG, 1, GW), (GV_SB, 1, GW),
              (GV_LNG, 1, D_MODEL), (GV_LNB, 1, D_MODEL), (GV_BSP, HEADS, CHUNK))


def _adam_update(g, w, m, v):
    m_new = ADAM_B1 * m + (1.0 - ADAM_B1) * g
    v_new = ADAM_B2 * v + (1.0 - ADAM_B2) * (g * g)
    m_hat = m_new / (1.0 - ADAM_B1 ** ADAM_STEP)
    v_hat = v_new / (1.0 - ADAM_B2 ** ADAM_STEP)
    return -ADAM_LR * (m_hat / (jnp.sqrt(v_hat) + ADAM_EPS) + ADAM_WD * w), m_new, v_new


def _adamw(g_big, w_big, m_big, v_big, gvec, w_small, m_small, v_small):
    nb, ns = len(g_big), len(w_small)

    def body(*refs):
        it = iter(refs)
        take = lambda n: [next(it) for _ in range(n)]
        g_b, w_b, m_b, v_b = take(nb), take(nb), take(nb), take(nb)
        gv = next(it)
        w_s, m_s, v_s = take(ns), take(ns), take(ns)
        d_bo, m_bo, v_bo = take(nb), take(nb), take(nb)
        g_so, d_so, m_so, v_so = take(ns), take(ns), take(ns), take(ns)
        for n in range(nb):
            d_bo[n][...], m_bo[n][...], v_bo[n][...] = _adam_update(g_b[n][...], w_b[n][...], m_b[n][...], v_b[n][...])
        for n, (row, nrow, width) in enumerate(SMALL_ROWS):
            gs = gv[row:row + nrow, 0:width]
            g_so[n][...] = gs
            d_so[n][...], m_so[n][...], v_so[n][...] = _adam_update(gs, w_s[n][...], m_s[n][...], v_s[n][...])

    vmem = pl.BlockSpec(memory_space=pltpu.VMEM)
    big = [jax.ShapeDtypeStruct(a.shape, F32) for a in w_big]
    small = [jax.ShapeDtypeStruct(a.shape, F32) for a in w_small]
    return pl.pallas_call(
        body, name="adamw", out_shape=big * 3 + small * 4,
        in_specs=[vmem] * (4 * nb + 1 + 3 * ns), out_specs=[vmem] * (3 * nb + 4 * ns),
        compiler_params=pltpu.CompilerParams(vmem_limit_bytes=VMEM_LIMIT),
    )(*g_big, *w_big, *m_big, *v_big, gvec, *w_small, *m_small, *v_small)


def kernel(x, positions, w_in, q_norm_g, w_uq, kv_norm_g, w_ukv, sgu_norm_g, sgu_norm_b, w_spatial, b_spatial, w_out, ln_g, ln_b, loss_target, m_w_in, m_q_norm_g, m_w_uq, m_kv_norm_g, m_w_ukv, m_sgu_norm_g, m_sgu_norm_b, m_w_spatial, m_b_spatial, m_w_out, m_ln_g, m_ln_b, v_w_in, v_q_norm_g, v_w_uq, v_kv_norm_g, v_w_ukv, v_sgu_norm_g, v_sgu_norm_b, v_w_spatial, v_b_spatial, v_w_out, v_ln_g, v_ln_b):
    seq = x.shape[1]
    x2 = x.reshape(seq, D_MODEL)
    tgt = loss_target.reshape(seq, D_MODEL)
    pos = positions.reshape(seq, 1)

    a_in, a_uq, a_ukv, a_out = _weight_gather([w_in, w_uq, w_ukv, w_out])
    w_in_f = jnp.swapaxes(a_in, 0, 1).reshape(D_MODEL, D_IN)
    w_uq_f = jnp.swapaxes(a_uq, 0, 1).reshape(Q_LORA, HEADS * (NOPE + ROPE))
    w_ukv_f = jnp.swapaxes(a_ukv, 0, 1).reshape(KV_LORA, HEADS * (NOPE + VDIM))
    wout = a_out.reshape(D_MODEL, D_MODEL)
    zc = lambda n: jnp.zeros((D_MODEL, n), BF16)
    win = jnp.concatenate([w_in_f[:, 0:384], zc(64), w_in_f[:, 384:416], zc(32), w_in_f[:, 416:D_IN]], axis=1)
    wuq = jnp.pad(w_uq_f.reshape(Q_LORA, HEADS, NOPE + ROPE), ((0, 0), (0, 0), (0, HP - NOPE - ROPE)))
    wuq = wuq.reshape(Q_LORA, HEADS * HP)
    ukv = w_ukv_f.reshape(KV_LORA, HEADS, NOPE + VDIM)
    wk = jnp.pad(ukv[:, :, 0:NOPE], ((0, 0), (0, 0), (0, HP - NOPE))).reshape(KV_LORA, HEADS * HP)
    wkv = jnp.concatenate([wk, ukv[:, :, NOPE:].reshape(KV_LORA, MLA_W)], axis=1)

    lane = jnp.arange(HP)
    half = ROPE // 2
    inv_freq = 1.0 / (ROPE_THETA ** (jnp.arange(half, dtype=F32) / half))
    in_rope = (lane >= NOPE) & (lane < NOPE + ROPE)
    invf = jnp.where(in_rope, inv_freq[(lane - NOPE) % half], 0.0)
    m1 = jnp.where((lane >= NOPE) & (lane < NOPE + half), -1.0, 0.0)
    m2 = jnp.where((lane >= NOPE + half) & (lane < NOPE + ROPE), 1.0, 0.0)
    row = lambda a: jnp.pad(a.astype(F32), (0, D_MODEL - a.shape[0]))
    pvec = jnp.stack([row(q_norm_g), row(kv_norm_g), row(sgu_norm_g), row(sgu_norm_b), row(invf), row(m1),
                      row(m2), row(ln_g), row(ln_b)] + [jnp.zeros((D_MODEL,), F32)] * (PV_ROWS - 9))
    tri = jnp.tril(jnp.ones((CHUNK, CHUNK), dtype=bool))
    wt = jnp.where(tri[None], w_spatial, 0.0).astype(BF16)
    wtt = jnp.swapaxes(wt, 1, 2)
    bsp = jnp.repeat(b_spatial.T, VDIM, axis=1)

    cq, ckv, gate, q, k, v, qt, kt, vt, cs = _fwd_pre(x2, pos, win, wuq, wkv, pvec)
    o, lse = _attn_fwd(q, k, vt)
    dh2, do_t, delta, dgate, g_wout, g_wsp, gvec = _post(x2, tgt, o, gate, wout, pvec, wt, wtt, bsp)
    dq, dk, dv = _attn_bwd(qt, k, kt, v, do_t, lse, delta.reshape(PAIRS, 2, seq), cs, pvec)
    gx, g_win, g_wuq, g_wkv, gvec = _bwd_pre(x2, dh2, cq, ckv, cs, dq, dk, dv, dgate, win, wuq, wkv, pvec, gvec)

    g_win_f = jnp.concatenate([g_win[:, 0:384], g_win[:, 448:480], g_win[:, 512:D_INR]], axis=1)
    g_wuq_f = g_wuq.reshape(Q_LORA, HEADS, HP)[:, :, 0:NOPE + ROPE].reshape(Q_LORA, HEADS * (NOPE + ROPE))
    g_k = g_wkv[:, 0:HEADS * HP].reshape(KV_LORA, HEADS, HP)[:, :, 0:NOPE]
    g_v = g_wkv[:, HEADS * HP:].reshape(KV_LORA, HEADS, VDIM)
    g_wukv_f = jnp.concatenate([g_k, g_v], axis=2).reshape(KV_LORA, HEADS * (NOPE + VDIM))

    def by_chip(a):
        rows, cols = a.shape[0], a.shape[1] // 4
        return jnp.swapaxes(a.reshape(rows, 4, cols), 0, 1).reshape(4, 2, rows // 2, cols)

    gs = [by_chip(g_win_f), by_chip(g_wuq_f), by_chip(g_wukv_f), g_wout.reshape(4, 2, 128, D_MODEL),
          g_wsp.reshape(4, 2, CHUNK, CHUNK)]
    r_in, r_uq, r_ukv, r_out, r_wsp, r_vec = _grad_reduce(gs, gvec)

    g_big = [r_in.reshape(w_in.shape), r_uq.reshape(w_uq.shape), r_ukv.reshape(w_ukv.shape),
             r_out.reshape(w_out.shape), r_wsp.reshape(w_spatial.shape)]
    small = lambda qg, kvg, sg, sb, lng, lnb, bs: [qg.reshape(1, -1), kvg.reshape(1, -1), sg.reshape(1, -1),
                                                   sb.reshape(1, -1), lng.reshape(1, -1), lnb.reshape(1, -1), bs]
    res = _adamw(g_big, [w_in, w_uq, w_ukv, w_out, w_spatial], [m_w_in, m_w_uq, m_w_ukv, m_w_out, m_w_spatial],
                 [v_w_in, v_w_uq, v_w_ukv, v_w_out, v_w_spatial], r_vec,
                 small(q_norm_g, kv_norm_g, sgu_norm_g, sgu_norm_b, ln_g, ln_b, b_spatial),
                 small(m_q_norm_g, m_kv_norm_g, m_sgu_norm_g, m_sgu_norm_b, m_ln_g, m_ln_b, m_b_spatial),
                 small(v_q_norm_g, v_kv_norm_g, v_sgu_norm_g, v_sgu_norm_b, v_ln_g, v_ln_b, v_b_spatial))

    def ordered(big, sm):
        vec = lambda n: sm[n].reshape(-1)
        return [big[0], vec(0), big[1], vec(1), big[2], vec(2), vec(3), big[4], sm[6], big[3], vec(4), vec(5)]

    loss = r_vec[GV_LOSS, 0]
    return (loss, gx.reshape(1, seq, D_MODEL), *ordered(g_big, res[15:22]), *ordered(res[0:5], res[22:29]),
            *ordered(res[5:10], res[29:36]), *ordered(res[10:15], res[36:43]))
```

```python
import math

import jax
import jax.numpy as jnp
from jax import lax
from jax.experimental import pallas as pl
from jax.experimental.pallas import tpu as pltpu

F32 = jnp.float32
BF16 = jnp.bfloat16

D_MODEL = 1024
Q_LORA = 256
KV_LORA = 128
HEADS = 8
NOPE = 64
ROPE = 32
VDIM = 64
MLA_W = HEADS * VDIM
GW = 512
CHUNK = 128
HP = 128
PAIRS = HEADS // 2
D_IN = 2464
D_INR = 2560
ROPE_THETA = 10000.0
DN_ALPHA = 2.0 ** 0.25
EPS = 1e-5
SCALE = 1.0 / math.sqrt(NOPE + ROPE)
SCALE_LOG2E = SCALE * 1.4426950408889634
INV_SQRT2 = 0.7071067811865476
INV_SQRT_2PI = 0.3989422804014327

ADAM_LR = 0.001
ADAM_B1 = 0.9
ADAM_B2 = 0.999
ADAM_EPS = 1e-08
ADAM_WD = 0.01
ADAM_STEP = 10

PV_QG, PV_KVG, PV_SG, PV_SB, PV_INVF, PV_M1, PV_M2, PV_LNG, PV_LNB = range(9)
PV_ROWS = 16
GV_QG, GV_KVG, GV_SG, GV_SB, GV_LNG, GV_LNB, GV_LOSS = range(7)
GV_BSP = 8
GV_ROWS = 16

MESH = pl.DeviceIdType.MESH

FWD_TILE = 512
POST_TILE = 256
BWD_TILE = 512
ATT_BLK = 512
VMEM_LIMIT = 56 * 1024 * 1024


def _dot(a, b):
    return jnp.dot(a, b, preferred_element_type=F32)


def _dot_nt(a, b):
    return lax.dot_general(a, b, (((1,), (1,)), ((), ())), preferred_element_type=F32)


def _dot_tn(a, b):
    return lax.dot_general(a, b, (((0,), (0,)), ((), ())), preferred_element_type=F32)


def _sigmoid(z):
    return pl.reciprocal(1.0 + jnp.exp(-z), approx=True)


def _gelu_and_grad(x):
    cdf = 0.5 * (1.0 + lax.erf(x * INV_SQRT2))
    return x * cdf, cdf + x * (INV_SQRT_2PI * jnp.exp(-0.5 * x * x))


def _rms_stats(x):
    r = lax.rsqrt(jnp.mean(x * x, axis=-1, keepdims=True) + EPS)
    return x * r, r


def _rms_bwd(dy, g, xh, r):
    dyg = dy * g
    return r * (dyg - xh * jnp.mean(dyg * xh, axis=-1, keepdims=True))


def _ln_stats(x):
    mu = jnp.mean(x, axis=-1, keepdims=True)
    xc = x - mu
    r = lax.rsqrt(jnp.mean(xc * xc, axis=-1, keepdims=True) + EPS)
    return xc * r, r


def _ln_bwd(dy, g, xh, r):
    dxh = dy * g
    return r * (dxh - jnp.mean(dxh, axis=-1, keepdims=True) - xh * jnp.mean(dxh * xh, axis=-1, keepdims=True))


def _rope_fwd(t, c, s1, s2):
    return t * c + pltpu.roll(t, HP - 16, 1) * s1 + pltpu.roll(t, 16, 1) * s2


def _rope_bwd(d, c, s1, s2):
    return d * c + pltpu.roll(d * s1, 16, 1) + pltpu.roll(d * s2, HP - 16, 1)


def _lane_lt64(shape):
    return lax.broadcasted_iota(jnp.int32, shape, len(shape) - 1) < 64


def _spatial_mix(w_ref, src, dst_ref, rows):
    for c in range(rows // CHUNK):
        for p in range(PAIRS):
            blk = src[c * CHUNK:(c + 1) * CHUNK, p * HP:(p + 1) * HP]
            a = _dot(w_ref[2 * p], blk)
            b = _dot(w_ref[2 * p + 1], blk)
            dst_ref[c * CHUNK:(c + 1) * CHUNK, p * HP:(p + 1) * HP] = jnp.where(_lane_lt64(a.shape), a, b)


def _gmlp_fwd(u_pre, v_pre, zb, sg, sb, wt_ref, bsp_ref, sv_ref, rows):
    u, du = _gelu_and_grad(u_pre)
    gv, dgv = _gelu_and_grad(v_pre)
    xh, r = _ln_stats(gv)
    vln = (xh * sg + sb).astype(BF16)
    _spatial_mix(wt_ref, vln, sv_ref, rows)
    bias = bsp_ref[...]
    svb = sv_ref[...] + jnp.concatenate([bias] * (rows // CHUNK), axis=0)
    sig = _sigmoid(zb)
    return u, du, dgv, xh, r, vln, svb, sig


def _weight_gather(shards):
    n_arr = len(shards)

    def body(*refs):
        ins, outs = refs[0:n_arr], refs[n_arr:2 * n_arr]
        send_sems, recv_sems = refs[2 * n_arr:]
        x, y, c = lax.axis_index("x"), lax.axis_index("y"), lax.axis_index("c")
        j = 2 * x + y
        sib = (x, y, 1 - c)
        chips = [(1 - x, y), (x, 1 - y), (1 - x, 1 - y)]
        for n in range(n_arr):
            outs[n][j] = ins[n][...].astype(BF16)

        def half(n, blk, core):
            r = shards[n].shape[0] // 2
            return outs[n].at[blk, pl.ds(pl.multiple_of(core * r, 16), r), :]

        def copy(k, ref, to):
            return pltpu.make_async_remote_copy(
                src_ref=ref, dst_ref=ref, send_sem=send_sems.at[k], recv_sem=recv_sems.at[k],
                device_id=to, device_id_type=MESH)

        first = [copy(6 * n + kk, half(n, j, c), (px, py, c))
                 for n in range(n_arr) for kk, (px, py) in enumerate(chips)]
        for cp in first:
            cp.start()
        passed = []
        for n in range(n_arr):
            for kk, (px, py) in enumerate(chips):
                landed = half(n, 2 * px + py, c)
                copy(6 * n + kk, landed, (px, py, c)).wait_recv()
                passed.append(copy(6 * n + 3 + kk, landed, sib))
                passed[-1].start()
        for n in range(n_arr):
            for kk, (px, py) in enumerate(chips):
                copy(6 * n + 3 + kk, half(n, 2 * px + py, 1 - c), sib).wait_recv()
        for cp in first + passed:
            cp.wait_send()

    vmem = pl.BlockSpec(memory_space=pltpu.VMEM)
    return pl.pallas_call(
        body, name="weight_gather",
        out_shape=[jax.ShapeDtypeStruct((4,) + a.shape, BF16) for a in shards],
        in_specs=[vmem] * n_arr, out_specs=[vmem] * n_arr,
        scratch_shapes=[pltpu.SemaphoreType.DMA((6 * n_arr,)), pltpu.SemaphoreType.DMA((6 * n_arr,))],
        compiler_params=pltpu.CompilerParams(vmem_limit_bytes=VMEM_LIMIT),
    )(*shards)


def _fwd_pre(x, pos, win, wuq, wkv, pvec):
    seq = x.shape[0]
    t = FWD_TILE

    def body(x_ref, pos_ref, win_ref, wuq_ref, wkv_ref, pv_ref,
             cq_o, ckv_o, gate_o, q_o, k_o, v_o, vt_o, cs_o):
        proj = _dot(x_ref[...].astype(BF16), win_ref[...])
        cq = proj[:, 0:256]
        ckv = proj[:, 256:384]
        kr = proj[:, 384:512]
        cq_o[...] = cq
        ckv_o[...] = ckv
        gate_o[...] = proj[:, 512:D_INR]

        ang = pos_ref[...].astype(F32) * pv_ref[PV_INVF:PV_INVF + 1, 0:HP]
        cos = jnp.cos(ang)
        sin = jnp.sin(ang)
        cs_o[:, 0:HP] = cos
        cs_o[:, HP:2 * HP] = sin
        s1 = sin * pv_ref[PV_M1:PV_M1 + 1, 0:HP]
        s2 = sin * pv_ref[PV_M2:PV_M2 + 1, 0:HP]

        cqh, _ = _rms_stats(cq)
        q_all = _dot((cqh * pv_ref[PV_QG:PV_QG + 1, 0:Q_LORA]).astype(BF16), wuq_ref[...])
        ckvh, _ = _rms_stats(ckv)
        kv_all = _dot((ckvh * pv_ref[PV_KVG:PV_KVG + 1, 0:KV_LORA]).astype(BF16), wkv_ref[...])
        krr = _rope_fwd(kr, cos, s1, s2)
        for h in range(HEADS):
            sl = slice(h * HP, (h + 1) * HP)
            q_o[:, sl] = (_rope_fwd(q_all[:, sl], cos, s1, s2) * SCALE_LOG2E).astype(BF16)
            k_o[:, sl] = (kv_all[:, sl] + krr).astype(BF16)
        val = kv_all[:, HEADS * HP:].astype(BF16)
        v_o[...] = val
        vt_o[...] = val.T

    tile = lambda w: pl.BlockSpec((t, w), lambda i: (i, 0))
    full = lambda a: pl.BlockSpec(a.shape, lambda i: (0,) * a.ndim)
    outs = [(Q_LORA, F32), (KV_LORA, F32), (2048, F32), (HEADS * HP, BF16), (HEADS * HP, BF16), (MLA_W, BF16)]
    per_blk = ATT_BLK // t
    out_specs = [tile(w) for w, _ in outs]
    out_specs += [pl.BlockSpec((None, MLA_W, t), lambda i: (i // per_blk, 0, i % per_blk)), tile(2 * HP)]
    out_shape = [jax.ShapeDtypeStruct((seq, w), d) for w, d in outs]
    out_shape += [jax.ShapeDtypeStruct((seq // ATT_BLK, MLA_W, ATT_BLK), BF16), jax.ShapeDtypeStruct((seq, 2 * HP), F32)]
    return pl.pallas_call(
        body, name="fwd_pre", grid=(seq // t,),
        in_specs=[tile(D_MODEL), tile(1), full(win), full(wuq), full(wkv), full(pvec)],
        out_specs=out_specs, out_shape=out_shape,
        compiler_params=pltpu.CompilerParams(dimension_semantics=("arbitrary",), vmem_limit_bytes=VMEM_LIMIT),
    )(x, pos, win, wuq, wkv, pvec)


def _attn_fwd(q, k, vt):
    seq = q.shape[0]
    b = ATT_BLK
    nq = seq // b

    def body(q_ref, k_ref, vt_ref, o_o, lse_o, m_ref, l_ref, acc_ref, s_even, s_odd):
        i = pl.program_id(1)
        m_ref[...] = jnp.full(m_ref.shape, -jnp.inf, F32)
        l_ref[...] = jnp.zeros(l_ref.shape, F32)
        acc_ref[...] = jnp.zeros(acc_ref.shape, F32)

        def scores(j, s_ref):
            rows = pl.ds(pl.multiple_of(j * b, b), b)
            for a in range(2):
                s_ref[a] = _dot_nt(k_ref[rows, a * HP:(a + 1) * HP], q_ref[:, a * HP:(a + 1) * HP])

        def consume(j, s_ref, masked):
            vt_blk = vt_ref[j]
            for a in range(2):
                st = s_ref[a]
                if masked:
                    ki = lax.broadcasted_iota(jnp.int32, st.shape, 0)
                    qi = lax.broadcasted_iota(jnp.int32, st.shape, 1)
                    st = jnp.where(ki <= qi, st, -jnp.inf)
                m_prev = m_ref[a:a + 1, :]
                m_new = jnp.maximum(m_prev, jnp.max(st, axis=0, keepdims=True))
                alpha = jnp.exp2(m_prev - m_new)
                pt = jnp.exp2(st - m_new)
                l_ref[a:a + 1, :] = alpha * l_ref[a:a + 1, :] + jnp.sum(pt, axis=0, keepdims=True)
                acc_ref[a] = alpha * acc_ref[a] + _dot(vt_blk, pt.astype(BF16))
                m_ref[a:a + 1, :] = m_new

        scores(0, s_even)

        def loop_body(jj, carry):
            scores(2 * jj + 1, s_odd)
            consume(2 * jj, s_even, False)
            scores(2 * jj + 2, s_even)
            consume(2 * jj + 1, s_odd, False)
            return carry

        lax.fori_loop(0, i // 2, loop_body, 0)

        @pl.when(i % 2 == 0)
        def _():
            consume(i, s_even, True)

        @pl.when(i % 2 == 1)
        def _():
            scores(i, s_odd)
            consume(i - 1, s_even, False)
            consume(i, s_odd, True)

        top = lax.broadcasted_iota(jnp.int32, (HP, b), 0) < 64
        o_o[...] = jnp.where(top, acc_ref[0] / l_ref[0:1, :], acc_ref[1] / l_ref[1:2, :]).T
        lse_o[...] = m_ref[0:2, :] + jnp.log2(l_ref[0:2, :])

    return pl.pallas_call(
        body, name="attn_fwd", grid=(PAIRS, nq),
        in_specs=[pl.BlockSpec((b, 2 * HP), lambda p, i: (i, p)),
                  pl.BlockSpec((seq, 2 * HP), lambda p, i: (0, p)),
                  pl.BlockSpec((nq, HP, b), lambda p, i: (0, p, 0))],
        out_specs=[pl.BlockSpec((b, HP), lambda p, i: (i, p)),
                   pl.BlockSpec((None, 2, b), lambda p, i: (p, 0, i))],
        out_shape=[jax.ShapeDtypeStruct((seq, MLA_W), F32),
                   jax.ShapeDtypeStruct((PAIRS, 2, seq), F32)],
        scratch_shapes=[pltpu.VMEM((8, b), F32), pltpu.VMEM((8, b), F32), pltpu.VMEM((2, HP, b), F32),
                        pltpu.VMEM((2, b, b), F32), pltpu.VMEM((2, b, b), F32)],
        compiler_params=pltpu.CompilerParams(dimension_semantics=("arbitrary", "arbitrary"),
                                             vmem_limit_bytes=VMEM_LIMIT),
    )(q, k, vt)


def _post(x, tgt, o, gate, wout, pvec, wt, wtt, bsp):
    seq = x.shape[0]
    t = POST_TILE
    nt = seq // t

    def body(x_ref, tgt_ref, o_ref, gate_ref, wout_ref, pv_ref, wt_ref, wtt_ref, bsp_ref,
             dh2_o, do_o, dgate_o, gwout_o, gwsp_o, vec_o, sv_ref, dvln_ref, bacc_ref):
        i = pl.program_id(0)

        @pl.when(i == 0)
        def _():
            gwout_o[...] = jnp.zeros_like(gwout_o)
            gwsp_o[...] = jnp.zeros_like(gwsp_o)
            vec_o[...] = jnp.zeros_like(vec_o)
            bacc_ref[...] = jnp.zeros_like(bacc_ref)

        za = gate_ref[:, 0:512]
        u_pre = gate_ref[:, 512:1024]
        v_pre = gate_ref[:, 1024:1536]
        zb = gate_ref[:, 1536:2048]
        sg = pv_ref[PV_SG:PV_SG + 1, 0:GW]
        sb = pv_ref[PV_SB:PV_SB + 1, 0:GW]
        lng = pv_ref[PV_LNG:PV_LNG + 1, :]
        lnb = pv_ref[PV_LNB:PV_LNB + 1, :]
        o = o_ref[...]

        sig_a = _sigmoid(za)
        silu_a = za * sig_a
        u, du, dgv, xh, r, vln, svb, sig_b = _gmlp_fwd(u_pre, v_pre, zb, sg, sb, wt_ref, bsp_ref, sv_ref, t)
        silu_b = zb * sig_b
        sgu = u * svb
        merged = jnp.concatenate([o * silu_a, sgu * silu_b], axis=1).astype(BF16)
        h2 = DN_ALPHA * x_ref[...] + _dot(merged, wout_ref[...])
        xh2, r2 = _ln_stats(h2)
        err = xh2 * lng + lnb - tgt_ref[...]
        d_out = err * (1.0 / D_MODEL)
        vec_o[GV_LNG:GV_LNG + 1, :] += jnp.sum(d_out * xh2, axis=0, keepdims=True)
        vec_o[GV_LNB:GV_LNB + 1, :] += jnp.sum(d_out, axis=0, keepdims=True)
        vec_o[GV_LOSS:GV_LOSS + 1, :] += jnp.sum(err * err, axis=0, keepdims=True) * (0.5 / D_MODEL)

        d_h2 = _ln_bwd(d_out, lng, xh2, r2)
        dh2_o[...] = d_h2
        dh2b = d_h2.astype(BF16)
        gwout_o[...] += _dot_tn(merged, dh2b)
        d_m = _dot_nt(dh2b, wout_ref[...])
        d_oa = d_m[:, 0:512]
        d_ob = d_m[:, 512:1024]
        do_o[...] = (d_oa * silu_a).astype(BF16)
        dgate_o[:, 0:512] = (d_oa * o * (sig_a * (1.0 + za * (1.0 - sig_a)))).astype(BF16)
        dgate_o[:, 1536:2048] = (d_ob * sgu * (sig_b * (1.0 + zb * (1.0 - sig_b)))).astype(BF16)
        d_sgu = d_ob * silu_b
        dgate_o[:, 512:1024] = (d_sgu * svb * du).astype(BF16)
        d_sv = d_sgu * u
        acc = bacc_ref[...]
        for c in range(t // CHUNK):
            acc = acc + d_sv[c * CHUNK:(c + 1) * CHUNK, :]
        bacc_ref[...] = acc
        d_svb = d_sv.astype(BF16)
        for c in range(t // CHUNK):
            for p in range(PAIRS):
                blk = d_svb[c * CHUNK:(c + 1) * CHUNK, p * HP:(p + 1) * HP]
                vblk = vln[c * CHUNK:(c + 1) * CHUNK, p * HP:(p + 1) * HP]
                first = _lane_lt64(blk.shape)
                gwsp_o[2 * p] += _dot_nt(jnp.where(first, blk, jnp.zeros_like(blk)), vblk)
                gwsp_o[2 * p + 1] += _dot_nt(jnp.where(first, jnp.zeros_like(blk), blk), vblk)
        _spatial_mix(wtt_ref, d_svb, dvln_ref, t)
        d_vln = dvln_ref[...]
        vec_o[GV_SG:GV_SG + 1, 0:GW] += jnp.sum(d_vln * xh, axis=0, keepdims=True)
        vec_o[GV_SB:GV_SB + 1, 0:GW] += jnp.sum(d_vln, axis=0, keepdims=True)
        dgate_o[:, 1024:1536] = (_ln_bwd(d_vln, sg, xh, r) * dgv).astype(BF16)

        @pl.when(i == nt - 1)
        def _():
            tri = (lax.broadcasted_iota(jnp.int32, (CHUNK, CHUNK), 1)
                   <= lax.broadcasted_iota(jnp.int32, (CHUNK, CHUNK), 0))
            for h in range(HEADS):
                gwsp_o[h] = jnp.where(tri, gwsp_o[h], 0.0)
            lane = lax.broadcasted_iota(jnp.int32, (CHUNK, HP), 1)
            res = jnp.zeros((CHUNK, HP), F32)
            for h in range(HEADS):
                p, a = divmod(h, 2)
                blk = bacc_ref[:, p * HP:(p + 1) * HP]
                part = jnp.where(_lane_lt64(blk.shape) == (a == 0), blk, 0.0)
                res = jnp.where(lane == h, jnp.sum(part, axis=-1, keepdims=True), res)
            vec_o[GV_BSP:GV_BSP + HEADS, 0:HP] = res.T[0:HEADS, :]
            lane1 = lax.broadcasted_iota(jnp.int32, (1, D_MODEL), 1)
            total = jnp.sum(vec_o[GV_LOSS:GV_LOSS + 1, :], axis=-1, keepdims=True)
            vec_o[GV_LOSS:GV_LOSS + 1, :] = jnp.where(lane1 == 0, total, 0.0)

    tile = lambda w: pl.BlockSpec((t, w), lambda i: (i, 0))
    full = lambda a: pl.BlockSpec(a.shape, lambda i: (0,) * a.ndim)
    const = lambda s: pl.BlockSpec(s, lambda i: (0,) * len(s))
    return pl.pallas_call(
        body, name="post", grid=(nt,),
        in_specs=[tile(D_MODEL), tile(D_MODEL), tile(MLA_W), tile(2048), full(wout), full(pvec),
                  full(wt), full(wtt), full(bsp)],
        out_specs=[tile(D_MODEL), tile(MLA_W), tile(2048), const((D_MODEL, D_MODEL)),
                   const((HEADS, CHUNK, CHUNK)), const((GV_ROWS, D_MODEL))],
        out_shape=[jax.ShapeDtypeStruct((seq, D_MODEL), F32), jax.ShapeDtypeStruct((seq, MLA_W), BF16),
                   jax.ShapeDtypeStruct((seq, 2048), BF16), jax.ShapeDtypeStruct((D_MODEL, D_MODEL), F32),
                   jax.ShapeDtypeStruct((HEADS, CHUNK, CHUNK), F32), jax.ShapeDtypeStruct((GV_ROWS, D_MODEL), F32)],
        scratch_shapes=[pltpu.VMEM((t, GW), F32), pltpu.VMEM((t, GW), F32), pltpu.VMEM((CHUNK, GW), F32)],
        compiler_params=pltpu.CompilerParams(dimension_semantics=("arbitrary",), vmem_limit_bytes=VMEM_LIMIT),
    )(x, tgt, o, gate, wout, pvec, wt, wtt, bsp)


def _attn_bwd(q, k, v, do, o, lse, cs, pvec):
    seq = q.shape[0]
    b = ATT_BLK
    nq = seq // b

    def body(q_ref, k_ref, v_ref, do_ref, o_ref, lse_ref, cs_ref, pv_ref, dq_o, dk_o, dv_o, dk_acc, dv_acc):
        i = pl.program_id(1)

        @pl.when(i == 0)
        def _():
            dk_acc[...] = jnp.zeros_like(dk_acc)
            dv_acc[...] = jnp.zeros_like(dv_acc)

        first = _lane_lt64((b, HP))
        do = do_ref[...]
        zero = jnp.zeros_like(do)
        dos = [jnp.where(first, do, zero), jnp.where(first, zero, do)]
        prod_t = (do.astype(F32) * o_ref[...]).T
        deltas = [jnp.sum(prod_t[0:64, :], axis=0, keepdims=True),
                  jnp.sum(prod_t[64:128, :], axis=0, keepdims=True)]
        lses = [lse_ref[0:1, :], lse_ref[1:2, :]]
        qs = [q_ref[:, a * HP:(a + 1) * HP] for a in range(2)]

        def step(j, dqs, masked):
            rows = pl.ds(pl.multiple_of(j * b, b), b)
            vb = v_ref[rows, :]
            new_dq = []
            dvs = []
            for a in range(2):
                kb = k_ref[rows, a * HP:(a + 1) * HP]
                pt = jnp.exp2(_dot_nt(kb, qs[a]) - lses[a])
                if masked:
                    ki = lax.broadcasted_iota(jnp.int32, pt.shape, 0)
                    qi = lax.broadcasted_iota(jnp.int32, pt.shape, 1)
                    pt = jnp.where(ki <= qi, pt, 0.0)
                dvs.append(_dot(pt.astype(BF16), do))
                dpt = _dot_nt(vb, dos[a])
                dst = (pt * (dpt - deltas[a])).astype(BF16)
                dk_acc[rows, a * HP:(a + 1) * HP] += _dot(dst, qs[a])
                new_dq.append(dqs[a] + _dot_tn(dst, kb))
            dv_acc[rows, :] += jnp.where(first, dvs[0], dvs[1])
            return tuple(new_dq)

        init = (jnp.zeros((b, HP), F32), jnp.zeros((b, HP), F32))
        dqs = lax.fori_loop(0, i, lambda j, cr: step(j, cr, False), init)
        dqs = step(i, dqs, True)
        cos = cs_ref[:, 0:HP]
        sin = cs_ref[:, HP:2 * HP]
        s1 = sin * pv_ref[PV_M1:PV_M1 + 1, 0:HP]
        s2 = sin * pv_ref[PV_M2:PV_M2 + 1, 0:HP]
        for a in range(2):
            dq_o[:, a * HP:(a + 1) * HP] = _rope_bwd(dqs[a] * SCALE, cos, s1, s2).astype(BF16)

        @pl.when(i == nq - 1)
        def _():
            dk_o[...] = (dk_acc[...] * (SCALE / SCALE_LOG2E)).astype(BF16)
            dv_o[...] = dv_acc[...].astype(BF16)

    return pl.pallas_call(
        body, name="attn_bwd", grid=(PAIRS, nq),
        in_specs=[pl.BlockSpec((b, 2 * HP), lambda p, i: (i, p)),
                  pl.BlockSpec((seq, 2 * HP), lambda p, i: (0, p)),
                  pl.BlockSpec((seq, HP), lambda p, i: (0, p)),
                  pl.BlockSpec((b, HP), lambda p, i: (i, p)),
                  pl.BlockSpec((b, HP), lambda p, i: (i, p)),
                  pl.BlockSpec((None, 2, b), lambda p, i: (p, 0, i)),
                  pl.BlockSpec((b, 2 * HP), lambda p, i: (i, 0)),
                  pl.BlockSpec(pvec.shape, lambda p, i: (0, 0))],
        out_specs=[pl.BlockSpec((b, 2 * HP), lambda p, i: (i, p)),
                   pl.BlockSpec((seq, 2 * HP), lambda p, i: (0, p)),
                   pl.BlockSpec((seq, HP), lambda p, i: (0, p))],
        out_shape=[jax.ShapeDtypeStruct((seq, HEADS * HP), BF16),
                   jax.ShapeDtypeStruct((seq, HEADS * HP), BF16),
                   jax.ShapeDtypeStruct((seq, MLA_W), BF16)],
        scratch_shapes=[pltpu.VMEM((seq, 2 * HP), F32), pltpu.VMEM((seq, HP), F32)],
        compiler_params=pltpu.CompilerParams(dimension_semantics=("arbitrary", "arbitrary"),
                                             vmem_limit_bytes=VMEM_LIMIT),
    )(q, k, v, do, o, lse, cs, pvec)


def _bwd_pre(x, dh2, cq, ckv, cs, dq, dk, dv, dgate, win, wuq, wkv, pvec, gvec):
    seq = x.shape[0]
    t = BWD_TILE

    def body(x_ref, dh2_ref, cq_ref, ckv_ref, cs_ref, dq_ref, dk_ref, dv_ref, dgate_ref,
             win_ref, wuq_ref, wkv_ref, pv_ref, gv_ref, gx_o, gwin_o, gwuq_o, gwkv_o, vec_o):
        i = pl.program_id(0)

        @pl.when(i == 0)
        def _():
            gwin_o[...] = jnp.zeros_like(gwin_o)
            gwuq_o[...] = jnp.zeros_like(gwuq_o)
            gwkv_o[...] = jnp.zeros_like(gwkv_o)
            vec_o[...] = gv_ref[...]

        qg = pv_ref[PV_QG:PV_QG + 1, 0:Q_LORA]
        kvg = pv_ref[PV_KVG:PV_KVG + 1, 0:KV_LORA]
        dq = dq_ref[...]
        cqh, rq = _rms_stats(cq_ref[...])
        d_cqn = _dot_nt(dq, wuq_ref[...])
        gwuq_o[...] += _dot_tn((cqh * qg).astype(BF16), dq)
        vec_o[GV_QG:GV_QG + 1, 0:Q_LORA] += jnp.sum(d_cqn * cqh, axis=0, keepdims=True)
        d_cq = _rms_bwd(d_cqn, qg, cqh, rq)

        dk = dk_ref[...]
        dkv = jnp.concatenate([dk, dv_ref[...]], axis=1)
        ckvh, rkv = _rms_stats(ckv_ref[...])
        d_ckvn = _dot_nt(dkv, wkv_ref[...])
        gwkv_o[...] += _dot_tn((ckvh * kvg).astype(BF16), dkv)
        vec_o[GV_KVG:GV_KVG + 1, 0:KV_LORA] += jnp.sum(d_ckvn * ckvh, axis=0, keepdims=True)
        d_ckv = _rms_bwd(d_ckvn, kvg, ckvh, rkv)

        dks = dk[:, 0:HP].astype(F32)
        for h in range(1, HEADS):
            dks = dks + dk[:, h * HP:(h + 1) * HP].astype(F32)
        cos = cs_ref[:, 0:HP]
        sin = cs_ref[:, HP:2 * HP]
        d_kr = _rope_bwd(dks, cos, sin * pv_ref[PV_M1:PV_M1 + 1, 0:HP], sin * pv_ref[PV_M2:PV_M2 + 1, 0:HP])

        d_proj = jnp.concatenate([d_cq.astype(BF16), d_ckv.astype(BF16), d_kr.astype(BF16), dgate_ref[...]], axis=1)
        gwin_o[...] += _dot_tn(x_ref[...].astype(BF16), d_proj)
        gx_o[...] = DN_ALPHA * dh2_ref[...] + _dot_nt(d_proj, win_ref[...])

    tile = lambda w: pl.BlockSpec((t, w), lambda i: (i, 0))
    full = lambda a: pl.BlockSpec(a.shape, lambda i: (0,) * a.ndim)
    const = lambda s: pl.BlockSpec(s, lambda i: (0,) * len(s))
    return pl.pallas_call(
        body, name="bwd_pre", grid=(seq // t,),
        in_specs=[tile(D_MODEL), tile(D_MODEL), tile(Q_LORA), tile(KV_LORA), tile(2 * HP), tile(HEADS * HP),
                  tile(HEADS * HP), tile(MLA_W), tile(2048), full(win), full(wuq), full(wkv), full(pvec), full(gvec)],
        out_specs=[tile(D_MODEL), const((D_MODEL, D_INR)), const((Q_LORA, HEADS * HP)),
                   const((KV_LORA, HEADS * HP + MLA_W)), const((GV_ROWS, D_MODEL))],
        out_shape=[jax.ShapeDtypeStruct((seq, D_MODEL), F32), jax.ShapeDtypeStruct((D_MODEL, D_INR), F32),
                   jax.ShapeDtypeStruct((Q_LORA, HEADS * HP), F32),
                   jax.ShapeDtypeStruct((KV_LORA, HEADS * HP + MLA_W), F32),
                   jax.ShapeDtypeStruct((GV_ROWS, D_MODEL), F32)],
        compiler_params=pltpu.CompilerParams(dimension_semantics=("arbitrary",), vmem_limit_bytes=VMEM_LIMIT),
    )(x, dh2, cq, ckv, cs, dq, dk, dv, dgate, win, wuq, wkv, pvec, gvec)


def _grad_reduce(gs, gvec):
    n_arr = len(gs)
    n_big = n_arr - 1
    k1 = lambda n, blk: 4 * n + blk
    k2 = lambda n, kk: 4 * n_arr + 3 * n + kk
    k3 = lambda n: 7 * n_arr + n
    k3w = lambda k: 7 * n_arr + n_big + k
    kv = lambda k: 7 * n_arr + n_big + 7 + k
    n_sem = 7 * n_arr + n_big + 14

    def body(*refs):
        g, gv = refs[0:n_arr], refs[n_arr]
        outs, ov = refs[n_arr + 1:2 * n_arr + 1], refs[2 * n_arr + 1]
        r1 = refs[2 * n_arr + 2:3 * n_arr + 2]
        r2 = refs[3 * n_arr + 2:4 * n_arr + 2]
        s2 = refs[4 * n_arr + 2:5 * n_arr + 2]
        vbuf, send_sems, recv_sems = refs[5 * n_arr + 2:]
        x, y, c = lax.axis_index("x"), lax.axis_index("y"), lax.axis_index("c")
        j = 2 * x + y
        me = 2 * j + c
        sib = (x, y, 1 - c)
        chips = [(1 - x, y), (x, 1 - y), (1 - x, 1 - y)]
        others = [sib] + [(px, py, pc) for (px, py) in chips for pc in (c, 1 - c)]

        def copy(k, src, dst, to):
            return pltpu.make_async_remote_copy(
                src_ref=src, dst_ref=dst, send_sem=send_sems.at[k], recv_sem=recv_sems.at[k],
                device_id=to, device_id_type=MESH)

        l1 = [copy(k1(n, blk), g[n].at[blk, 1 - c], r1[n].at[blk], sib) for n in range(n_arr) for blk in range(4)]
        lv = [copy(kv(k), gv, vbuf.at[me], to) for k, to in enumerate(others)]
        for cp in l1 + lv:
            cp.start()
        for n in range(n_arr):
            for blk in range(4):
                copy(k1(n, blk), g[n].at[blk, c], r1[n].at[blk], sib).wait_recv()
        for n in range(n_arr):
            for blk in range(4):
                r1[n][blk] = g[n][blk, c] + r1[n][blk]
                s2[n][blk] = r1[n][blk].astype(BF16)

        l2 = [copy(k2(n, kk), s2[n].at[2 * px + py], r2[n].at[kk], (px, py, c))
              for n in range(n_arr) for kk, (px, py) in enumerate(chips)]
        for cp in l2:
            cp.start()
        for n in range(n_arr):
            for kk in range(3):
                copy(k2(n, kk), s2[n].at[0], r2[n].at[kk], sib).wait_recv()
        for n in range(n_arr):
            red = ((r1[n][j] + r2[n][0].astype(F32)) + r2[n][1].astype(F32)) + r2[n][2].astype(F32)
            if n < n_big:
                outs[n][c] = red
            else:
                outs[n][j, c] = red

        l3 = [copy(k3(n), outs[n].at[c], outs[n].at[c], sib) for n in range(n_big)]
        l3 += [copy(k3w(k), outs[n_big].at[j, c], outs[n_big].at[j, c], to) for k, to in enumerate(others)]
        for cp in l3:
            cp.start()
        for n in range(n_big):
            copy(k3(n), outs[n].at[1 - c], outs[n].at[1 - c], sib).wait_recv()
        for k, (px, py, pc) in enumerate(others):
            landed = outs[n_big].at[2 * px + py, pc]
            copy(k3w(k), landed, landed, (px, py, pc)).wait_recv()
            copy(kv(k), gv, vbuf.at[4 * px + 2 * py + pc], (px, py, pc)).wait_recv()
        vbuf[me] = gv[...]
        total = vbuf[0]
        for d in range(1, 8):
            total = total + vbuf[d]
        ov[...] = total
        for cp in l1 + lv + l2 + l3:
            cp.wait_send()

    vmem = pl.BlockSpec(memory_space=pltpu.VMEM)
    half_shapes = [a.shape[2:] for a in gs]
    out_shape = [jax.ShapeDtypeStruct((2,) + s, F32) for s in half_shapes[:n_big]]
    out_shape += [jax.ShapeDtypeStruct((4, 2) + half_shapes[n_big], F32), jax.ShapeDtypeStruct(gvec.shape, F32)]
    scratch = [pltpu.VMEM((4,) + s, F32) for s in half_shapes] + [pltpu.VMEM((3,) + s, BF16) for s in half_shapes]
    scratch += [pltpu.VMEM((4,) + s, BF16) for s in half_shapes]
    scratch += [pltpu.VMEM((8,) + gvec.shape, F32), pltpu.SemaphoreType.DMA((n_sem,)), pltpu.SemaphoreType.DMA((n_sem,))]
    return pl.pallas_call(
        body, name="grad_reduce", out_shape=out_shape,
        in_specs=[vmem] * (n_arr + 1), out_specs=[vmem] * (n_arr + 1), scratch_shapes=scratch,
        compiler_params=pltpu.CompilerParams(vmem_limit_bytes=VMEM_LIMIT),
    )(*gs, gvec)


SMALL_ROWS = ((GV_QG, 1, Q_LORA), (GV_KVG, 1, KV_LORA), (GV_SG, 1, GW), (GV_SB, 1, GW),
              (GV_LNG, 1, D_MODEL), (GV_LNB, 1, D_MODEL), (GV_BSP, HEADS, CHUNK))


def _adam_update(g, w, m, v):
    m_new = ADAM_B1 * m + (1.0 - ADAM_B1) * g
    v_new = ADAM_B2 * v + (1.0 - ADAM_B2) * (g * g)
    m_hat = m_new / (1.0 - ADAM_B1 ** ADAM_STEP)
    v_hat = v_new / (1.0 - ADAM_B2 ** ADAM_STEP)
    return -ADAM_LR * (m_hat / (jnp.sqrt(v_hat) + ADAM_EPS) + ADAM_WD * w), m_new, v_new


def _adamw(g_big, w_big, m_big, v_big, gvec, w_small, m_small, v_small):
    nb, ns = len(g_big), len(w_small)

    def body(*refs):
        it = iter(refs)
        take = lambda n: [next(it) for _ in range(n)]
        g_b, w_b, m_b, v_b = take(nb), take(nb), take(nb), take(nb)
        gv = next(it)
        w_s, m_s, v_s = take(ns), take(ns), take(ns)
        d_bo, m_bo, v_bo = take(nb), take(nb), take(nb)
        g_so, d_so, m_so, v_so = take(ns), take(ns), take(ns), take(ns)
        for n in range(nb):
            d_bo[n][...], m_bo[n][...], v_bo[n][...] = _adam_update(g_b[n][...], w_b[n][...], m_b[n][...], v_b[n][...])
        for n, (row, nrow, width) in enumerate(SMALL_ROWS):
            gs = gv[row:row + nrow, 0:width]
            g_so[n][...] = gs
            d_so[n][...], m_so[n][...], v_so[n][...] = _adam_update(gs, w_s[n][...], m_s[n][...], v_s[n][...])

    vmem = pl.BlockSpec(memory_space=pltpu.VMEM)
    big = [jax.ShapeDtypeStruct(a.shape, F32) for a in w_big]
    small = [jax.ShapeDtypeStruct(a.shape, F32) for a in w_small]
    return pl.pallas_call(
        body, name="adamw", out_shape=big * 3 + small * 4,
        in_specs=[vmem] * (4 * nb + 1 + 3 * ns), out_specs=[vmem] * (3 * nb + 4 * ns),
        compiler_params=pltpu.CompilerParams(vmem_limit_bytes=VMEM_LIMIT),
    )(*g_big, *w_big, *m_big, *v_big, gvec, *w_small, *m_small, *v_small)


def kernel(x, positions, w_in, q_norm_g, w_uq, kv_norm_g, w_ukv, sgu_norm_g, sgu_norm_b, w_spatial, b_spatial, w_out, ln_g, ln_b, loss_target, m_w_in, m_q_norm_g, m_w_uq, m_kv_norm_g, m_w_ukv, m_sgu_norm_g, m_sgu_norm_b, m_w_spatial, m_b_spatial, m_w_out, m_ln_g, m_ln_b, v_w_in, v_q_norm_g, v_w_uq, v_kv_norm_g, v_w_ukv, v_sgu_norm_g, v_sgu_norm_b, v_w_spatial, v_b_spatial, v_w_out, v_ln_g, v_ln_b):
    seq = x.shape[1]
    x2 = x.reshape(seq, D_MODEL)
    tgt = loss_target.reshape(seq, D_MODEL)
    pos = positions.reshape(seq, 1)

    a_in, a_uq, a_ukv, a_out = _weight_gather([w_in, w_uq, w_ukv, w_out])
    w_in_f = jnp.swapaxes(a_in, 0, 1).reshape(D_MODEL, D_IN)
    w_uq_f = jnp.swapaxes(a_uq, 0, 1).reshape(Q_LORA, HEADS * (NOPE + ROPE))
    w_ukv_f = jnp.swapaxes(a_ukv, 0, 1).reshape(KV_LORA, HEADS * (NOPE + VDIM))
    wout = a_out.reshape(D_MODEL, D_MODEL)
    zc = lambda n: jnp.zeros((D_MODEL, n), BF16)
    win = jnp.concatenate([w_in_f[:, 0:384], zc(64), w_in_f[:, 384:416], zc(32), w_in_f[:, 416:D_IN]], axis=1)
    wuq = jnp.pad(w_uq_f.reshape(Q_LORA, HEADS, NOPE + ROPE), ((0, 0), (0, 0), (0, HP - NOPE - ROPE)))
    wuq = wuq.reshape(Q_LORA, HEADS * HP)
    ukv = w_ukv_f.reshape(KV_LORA, HEADS, NOPE + VDIM)
    wk = jnp.pad(ukv[:, :, 0:NOPE], ((0, 0), (0, 0), (0, HP - NOPE))).reshape(KV_LORA, HEADS * HP)
    wkv = jnp.concatenate([wk, ukv[:, :, NOPE:].reshape(KV_LORA, MLA_W)], axis=1)

    lane = jnp.arange(HP)
    half = ROPE // 2
    inv_freq = 1.0 / (ROPE_THETA ** (jnp.arange(half, dtype=F32) / half))
    in_rope = (lane >= NOPE) & (lane < NOPE + ROPE)
    invf = jnp.where(in_rope, inv_freq[(lane - NOPE) % half], 0.0)
    m1 = jnp.where((lane >= NOPE) & (lane < NOPE + half), -1.0, 0.0)
    m2 = jnp.where((lane >= NOPE + half) & (lane < NOPE + ROPE), 1.0, 0.0)
    row = lambda a: jnp.pad(a.astype(F32), (0, D_MODEL - a.shape[0]))
    pvec = jnp.stack([row(q_norm_g), row(kv_norm_g), row(sgu_norm_g), row(sgu_norm_b), row(invf), row(m1),
                      row(m2), row(ln_g), row(ln_b)] + [jnp.zeros((D_MODEL,), F32)] * (PV_ROWS - 9))
    tri = jnp.tril(jnp.ones((CHUNK, CHUNK), dtype=bool))
    wt = jnp.where(tri[None], w_spatial, 0.0).astype(BF16)
    wtt = jnp.swapaxes(wt, 1, 2)
    bsp = jnp.repeat(b_spatial.T, VDIM, axis=1)

    cq, ckv, gate, q, k, v, vt, cs = _fwd_pre(x2, pos, win, wuq, wkv, pvec)
    o, lse = _attn_fwd(q, k, vt)
    dh2, do, dgate, g_wout, g_wsp, gvec = _post(x2, tgt, o, gate, wout, pvec, wt, wtt, bsp)
    dq, dk, dv = _attn_bwd(q, k, v, do, o, lse, cs, pvec)
    gx, g_win, g_wuq, g_wkv, gvec = _bwd_pre(x2, dh2, cq, ckv, cs, dq, dk, dv, dgate, win, wuq, wkv, pvec, gvec)

    g_win_f = jnp.concatenate([g_win[:, 0:384], g_win[:, 448:480], g_win[:, 512:D_INR]], axis=1)
    g_wuq_f = g_wuq.reshape(Q_LORA, HEADS, HP)[:, :, 0:NOPE + ROPE].reshape(Q_LORA, HEADS * (NOPE + ROPE))
    g_k = g_wkv[:, 0:HEADS * HP].reshape(KV_LORA, HEADS, HP)[:, :, 0:NOPE]
    g_v = g_wkv[:, HEADS * HP:].reshape(KV_LORA, HEADS, VDIM)
    g_wukv_f = jnp.concatenate([g_k, g_v], axis=2).reshape(KV_LORA, HEADS * (NOPE + VDIM))

    def by_chip(a):
        rows, cols = a.shape[0], a.shape[1] // 4
        return jnp.swapaxes(a.reshape(rows, 4, cols), 0, 1).reshape(4, 2, rows // 2, cols)

    gs = [by_chip(g_win_f), by_chip(g_wuq_f), by_chip(g_wukv_f), g_wout.reshape(4, 2, 128, D_MODEL),
          g_wsp.reshape(4, 2, CHUNK, CHUNK)]
    r_in, r_uq, r_ukv, r_out, r_wsp, r_vec = _grad_reduce(gs, gvec)

    g_big = [r_in.reshape(w_in.shape), r_uq.reshape(w_uq.shape), r_ukv.reshape(w_ukv.shape),
             r_out.reshape(w_out.shape), r_wsp.reshape(w_spatial.shape)]
    small = lambda qg, kvg, sg, sb, lng, lnb, bs: [qg.reshape(1, -1), kvg.reshape(1, -1), sg.reshape(1, -1),
                                                   sb.reshape(1, -1), lng.reshape(1, -1), lnb.reshape(1, -1), bs]
    res = _adamw(g_big, [w_in, w_uq, w_ukv, w_out, w_spatial], [m_w_in, m_w_uq, m_w_ukv, m_w_out, m_w_spatial],
                 [v_w_in, v_w_uq, v_w_ukv, v_w_out, v_w_spatial], r_vec,
                 small(q_norm_g, kv_norm_g, sgu_norm_g, sgu_norm_b, ln_g, ln_b, b_spatial),
                 small(m_q_norm_g, m_kv_norm_g, m_sgu_norm_g, m_sgu_norm_b, m_ln_g, m_ln_b, m_b_spatial),
                 small(v_q_norm_g, v_kv_norm_g, v_sgu_norm_g, v_sgu_norm_b, v_ln_g, v_ln_b, v_b_spatial))

    def ordered(big, sm):
        vec = lambda n: sm[n].reshape(-1)
        return [big[0], vec(0), big[1], vec(1), big[2], vec(2), vec(3), big[4], sm[6], big[3], vec(4), vec(5)]

    loss = r_vec[GV_LOSS, 0]
    return (loss, gx.reshape(1, seq, D_MODEL), *ordered(g_big, res[15:22]), *ordered(res[0:5], res[22:29]),
            *ordered(res[5:10], res[29:36]), *ordered(res[10:15], res[36:43]))
```

```python
import math

import jax
import jax.numpy as jnp
from jax import lax
from jax.experimental import pallas as pl
from jax.experimental.pallas import tpu as pltpu

F32 = jnp.float32
BF16 = jnp.bfloat16

D_MODEL = 1024
Q_LORA = 256
KV_LORA = 128
HEADS = 8
NOPE = 64
ROPE = 32
VDIM = 64
MLA_W = HEADS * VDIM
GW = 512
CHUNK = 128
HP = 128
PAIRS = HEADS // 2
D_IN = 2464
D_INR = 2560
ROPE_THETA = 10000.0
DN_ALPHA = 2.0 ** 0.25
EPS = 1e-5
SCALE = 1.0 / math.sqrt(NOPE + ROPE)
SCALE_LOG2E = SCALE * 1.4426950408889634
INV_SQRT2 = 0.7071067811865476
INV_SQRT_2PI = 0.3989422804014327

ADAM_LR = 0.001
ADAM_B1 = 0.9
ADAM_B2 = 0.999
ADAM_EPS = 1e-08
ADAM_WD = 0.01
ADAM_STEP = 10

PV_QG, PV_KVG, PV_SG, PV_SB, PV_INVF, PV_M1, PV_M2, PV_LNG, PV_LNB = range(9)
PV_ROWS = 16
GV_QG, GV_KVG, GV_SG, GV_SB, GV_LNG, GV_LNB, GV_LOSS = range(7)
GV_BSP = 8
GV_ROWS = 16

MESH = pl.DeviceIdType.MESH

FWD_TILE = 512
POST_TILE = 512
BWD_TILE = 512
ATT_BLK = 512
VMEM_LIMIT = 56 * 1024 * 1024


def _dot(a, b):
    return jnp.dot(a, b, preferred_element_type=F32)


def _dot_nt(a, b):
    return lax.dot_general(a, b, (((1,), (1,)), ((), ())), preferred_element_type=F32)


def _dot_tn(a, b):
    return lax.dot_general(a, b, (((0,), (0,)), ((), ())), preferred_element_type=F32)


def _sigmoid(z):
    return pl.reciprocal(1.0 + jnp.exp(-z), approx=True)


def _gelu_and_grad(x):
    cdf = 0.5 * (1.0 + lax.erf(x * INV_SQRT2))
    return x * cdf, cdf + x * (INV_SQRT_2PI * jnp.exp(-0.5 * x * x))


def _rms_stats(x):
    r = lax.rsqrt(jnp.mean(x * x, axis=-1, keepdims=True) + EPS)
    return x * r, r


def _rms_bwd(dy, g, xh, r):
    dyg = dy * g
    return r * (dyg - xh * jnp.mean(dyg * xh, axis=-1, keepdims=True))


def _ln_stats(x):
    mu = jnp.mean(x, axis=-1, keepdims=True)
    xc = x - mu
    r = lax.rsqrt(jnp.mean(xc * xc, axis=-1, keepdims=True) + EPS)
    return xc * r, r


def _ln_bwd(dy, g, xh, r):
    dxh = dy * g
    return r * (dxh - jnp.mean(dxh, axis=-1, keepdims=True) - xh * jnp.mean(dxh * xh, axis=-1, keepdims=True))


def _rope_fwd(t, c, s1, s2):
    return t * c + pltpu.roll(t, HP - 16, 1) * s1 + pltpu.roll(t, 16, 1) * s2


def _rope_bwd(d, c, s1, s2):
    return d * c + pltpu.roll(d * s1, 16, 1) + pltpu.roll(d * s2, HP - 16, 1)


def _lane_lt64(shape):
    return lax.broadcasted_iota(jnp.int32, shape, len(shape) - 1) < 64


def _spatial_mix(w_ref, src, dst_ref, rows):
    for c in range(rows // CHUNK):
        for p in range(PAIRS):
            blk = src[c * CHUNK:(c + 1) * CHUNK, p * HP:(p + 1) * HP]
            a = _dot(w_ref[2 * p], blk)
            b = _dot(w_ref[2 * p + 1], blk)
            dst_ref[c * CHUNK:(c + 1) * CHUNK, p * HP:(p + 1) * HP] = jnp.where(_lane_lt64(a.shape), a, b)


def _gmlp_fwd(u_pre, v_pre, zb, sg, sb, wt_ref, bsp_ref, sv_ref, rows):
    u, du = _gelu_and_grad(u_pre)
    gv, dgv = _gelu_and_grad(v_pre)
    xh, r = _ln_stats(gv)
    vln = (xh * sg + sb).astype(BF16)
    _spatial_mix(wt_ref, vln, sv_ref, rows)
    bias = bsp_ref[...]
    svb = sv_ref[...] + jnp.concatenate([bias] * (rows // CHUNK), axis=0)
    sig = _sigmoid(zb)
    return u, du, dgv, xh, r, vln, svb, sig


def _weight_gather(shards):
    n_arr = len(shards)

    def body(*refs):
        ins, outs = refs[0:n_arr], refs[n_arr:2 * n_arr]
        send_sems, recv_sems = refs[2 * n_arr:]
        x, y, c = lax.axis_index("x"), lax.axis_index("y"), lax.axis_index("c")
        j = 2 * x + y
        sib = (x, y, 1 - c)
        chips = [(1 - x, y), (x, 1 - y), (1 - x, 1 - y)]
        for n in range(n_arr):
            outs[n][j] = ins[n][...].astype(BF16)

        def half(n, blk, core):
            r = shards[n].shape[0] // 2
            return outs[n].at[blk, pl.ds(pl.multiple_of(core * r, 16), r), :]

        def copy(k, ref, to):
            return pltpu.make_async_remote_copy(
                src_ref=ref, dst_ref=ref, send_sem=send_sems.at[k], recv_sem=recv_sems.at[k],
                device_id=to, device_id_type=MESH)

        first = [copy(6 * n + kk, half(n, j, c), (px, py, c))
                 for n in range(n_arr) for kk, (px, py) in enumerate(chips)]
        for cp in first:
            cp.start()
        passed = []
        for n in range(n_arr):
            for kk, (px, py) in enumerate(chips):
                landed = half(n, 2 * px + py, c)
                copy(6 * n + kk, landed, (px, py, c)).wait_recv()
                passed.append(copy(6 * n + 3 + kk, landed, sib))
                passed[-1].start()
        for n in range(n_arr):
            for kk, (px, py) in enumerate(chips):
                copy(6 * n + 3 + kk, half(n, 2 * px + py, 1 - c), sib).wait_recv()
        for cp in first + passed:
            cp.wait_send()

    vmem = pl.BlockSpec(memory_space=pltpu.VMEM)
    return pl.pallas_call(
        body, name="weight_gather",
        out_shape=[jax.ShapeDtypeStruct((4,) + a.shape, BF16) for a in shards],
        in_specs=[vmem] * n_arr, out_specs=[vmem] * n_arr,
        scratch_shapes=[pltpu.SemaphoreType.DMA((6 * n_arr,)), pltpu.SemaphoreType.DMA((6 * n_arr,))],
        compiler_params=pltpu.CompilerParams(vmem_limit_bytes=VMEM_LIMIT),
    )(*shards)


def _fwd_pre(x, pos, win, wuq, wkv, pvec):
    seq = x.shape[0]
    t = FWD_TILE

    def body(x_ref, pos_ref, win_ref, wuq_ref, wkv_ref, pv_ref,
             cq_o, ckv_o, gate_o, q_o, k_o, v_o, vt_o, cs_o):
        proj = _dot(x_ref[...].astype(BF16), win_ref[...])
        cq = proj[:, 0:256]
        ckv = proj[:, 256:384]
        kr = proj[:, 384:512]
        cq_o[...] = cq
        ckv_o[...] = ckv
        gate_o[...] = proj[:, 512:D_INR].astype(BF16)

        ang = pos_ref[...].astype(F32) * pv_ref[PV_INVF:PV_INVF + 1, 0:HP]
        cos = jnp.cos(ang)
        sin = jnp.sin(ang)
        cs_o[:, 0:HP] = cos
        cs_o[:, HP:2 * HP] = sin
        s1 = sin * pv_ref[PV_M1:PV_M1 + 1, 0:HP]
        s2 = sin * pv_ref[PV_M2:PV_M2 + 1, 0:HP]

        cqh, _ = _rms_stats(cq)
        q_all = _dot((cqh * pv_ref[PV_QG:PV_QG + 1, 0:Q_LORA]).astype(BF16), wuq_ref[...])
        ckvh, _ = _rms_stats(ckv)
        kv_all = _dot((ckvh * pv_ref[PV_KVG:PV_KVG + 1, 0:KV_LORA]).astype(BF16), wkv_ref[...])
        krr = _rope_fwd(kr, cos, s1, s2)
        for h in range(HEADS):
            sl = slice(h * HP, (h + 1) * HP)
            q_o[:, sl] = (_rope_fwd(q_all[:, sl], cos, s1, s2) * SCALE_LOG2E).astype(BF16)
            k_o[:, sl] = (kv_all[:, sl] + krr).astype(BF16)
        val = kv_all[:, HEADS * HP:].astype(BF16)
        v_o[...] = val
        vt_o[...] = val.T

    tile = lambda w: pl.BlockSpec((t, w), lambda i: (i, 0))
    full = lambda a: pl.BlockSpec(a.shape, lambda i: (0,) * a.ndim)
    outs = [(Q_LORA, F32), (KV_LORA, F32), (2048, BF16), (HEADS * HP, BF16), (HEADS * HP, BF16), (MLA_W, BF16)]
    per_blk = ATT_BLK // t
    out_specs = [tile(w) for w, _ in outs]
    out_specs += [pl.BlockSpec((None, MLA_W, t), lambda i: (i // per_blk, 0, i % per_blk)), tile(2 * HP)]
    out_shape = [jax.ShapeDtypeStruct((seq, w), d) for w, d in outs]
    out_shape += [jax.ShapeDtypeStruct((seq // ATT_BLK, MLA_W, ATT_BLK), BF16), jax.ShapeDtypeStruct((seq, 2 * HP), F32)]
    return pl.pallas_call(
        body, name="fwd_pre", grid=(seq // t,),
        in_specs=[tile(D_MODEL), tile(1), full(win), full(wuq), full(wkv), full(pvec)],
        out_specs=out_specs, out_shape=out_shape,
        compiler_params=pltpu.CompilerParams(dimension_semantics=("arbitrary",), vmem_limit_bytes=VMEM_LIMIT),
    )(x, pos, win, wuq, wkv, pvec)


def _attn_fwd(q, k, vt):
    seq = q.shape[0]
    b = ATT_BLK
    nq = seq // b

    def body(q_ref, k_ref, vt_ref, o_o, lse_o, m_ref, l_ref, acc_ref, s_even, s_odd):
        i = pl.program_id(1)
        m_ref[...] = jnp.full(m_ref.shape, -jnp.inf, F32)
        l_ref[...] = jnp.zeros(l_ref.shape, F32)
        acc_ref[...] = jnp.zeros(acc_ref.shape, F32)

        def scores(j, s_ref):
            rows = pl.ds(pl.multiple_of(j * b, b), b)
            for a in range(2):
                s_ref[a] = _dot_nt(k_ref[rows, a * HP:(a + 1) * HP], q_ref[:, a * HP:(a + 1) * HP])

        def consume(j, s_ref, masked):
            vt_blk = vt_ref[j]
            for a in range(2):
                st = s_ref[a]
                if masked:
                    ki = lax.broadcasted_iota(jnp.int32, st.shape, 0)
                    qi = lax.broadcasted_iota(jnp.int32, st.shape, 1)
                    st = jnp.where(ki <= qi, st, -jnp.inf)
                m_prev = m_ref[a:a + 1, :]
                m_new = jnp.maximum(m_prev, jnp.max(st, axis=0, keepdims=True))
                alpha = jnp.exp2(m_prev - m_new)
                pt = jnp.exp2(st - m_new)
                l_ref[a:a + 1, :] = alpha * l_ref[a:a + 1, :] + jnp.sum(pt, axis=0, keepdims=True)
                acc_ref[a] = alpha * acc_ref[a] + _dot(vt_blk, pt.astype(BF16))
                m_ref[a:a + 1, :] = m_new

        scores(0, s_even)

        def loop_body(jj, carry):
            scores(2 * jj + 1, s_odd)
            consume(2 * jj, s_even, False)
            scores(2 * jj + 2, s_even)
            consume(2 * jj + 1, s_odd, False)
            return carry

        lax.fori_loop(0, i // 2, loop_body, 0)

        @pl.when(i % 2 == 0)
        def _():
            consume(i, s_even, True)

        @pl.when(i % 2 == 1)
        def _():
            scores(i, s_odd)
            consume(i - 1, s_even, False)
            consume(i, s_odd, True)

        top = lax.broadcasted_iota(jnp.int32, (HP, b), 0) < 64
        o_o[...] = jnp.where(top, acc_ref[0] / l_ref[0:1, :], acc_ref[1] / l_ref[1:2, :]).T
        lse_o[...] = m_ref[0:2, :] + jnp.log2(l_ref[0:2, :])

    return pl.pallas_call(
        body, name="attn_fwd", grid=(PAIRS, nq),
        in_specs=[pl.BlockSpec((b, 2 * HP), lambda p, i: (i, p)),
                  pl.BlockSpec((seq, 2 * HP), lambda p, i: (0, p)),
                  pl.BlockSpec((nq, HP, b), lambda p, i: (0, p, 0))],
        out_specs=[pl.BlockSpec((b, HP), lambda p, i: (i, p)),
                   pl.BlockSpec((None, 2, b), lambda p, i: (p, 0, i))],
        out_shape=[jax.ShapeDtypeStruct((seq, MLA_W), F32),
                   jax.ShapeDtypeStruct((PAIRS, 2, seq), F32)],
        scratch_shapes=[pltpu.VMEM((8, b), F32), pltpu.VMEM((8, b), F32), pltpu.VMEM((2, HP, b), F32),
                        pltpu.VMEM((2, b, b), F32), pltpu.VMEM((2, b, b), F32)],
        compiler_params=pltpu.CompilerParams(dimension_semantics=("arbitrary", "arbitrary"),
                                             vmem_limit_bytes=VMEM_LIMIT),
    )(q, k, vt)


def _post(x, tgt, o, gate, wout, pvec, wt, wtt, bsp):
    seq = x.shape[0]
    t = POST_TILE
    nt = seq // t

    def body(x_ref, tgt_ref, o_ref, gate_ref, wout_ref, pv_ref, wt_ref, wtt_ref, bsp_ref,
             dh2_o, do_o, dgate_o, gwout_o, gwsp_o, vec_o, sv_ref, dvln_ref, bacc_ref):
        i = pl.program_id(0)

        @pl.when(i == 0)
        def _():
            gwout_o[...] = jnp.zeros_like(gwout_o)
            gwsp_o[...] = jnp.zeros_like(gwsp_o)
            vec_o[...] = jnp.zeros_like(vec_o)
            bacc_ref[...] = jnp.zeros_like(bacc_ref)

        za = gate_ref[:, 0:512].astype(F32)
        u_pre = gate_ref[:, 512:1024].astype(F32)
        v_pre = gate_ref[:, 1024:1536].astype(F32)
        zb = gate_ref[:, 1536:2048].astype(F32)
        sg = pv_ref[PV_SG:PV_SG + 1, 0:GW]
        sb = pv_ref[PV_SB:PV_SB + 1, 0:GW]
        lng = pv_ref[PV_LNG:PV_LNG + 1, :]
        lnb = pv_ref[PV_LNB:PV_LNB + 1, :]
        o = o_ref[...]

        sig_a = _sigmoid(za)
        silu_a = za * sig_a
        u, du, dgv, xh, r, vln, svb, sig_b = _gmlp_fwd(u_pre, v_pre, zb, sg, sb, wt_ref, bsp_ref, sv_ref, t)
        silu_b = zb * sig_b
        sgu = u * svb
        merged = jnp.concatenate([o * silu_a, sgu * silu_b], axis=1).astype(BF16)
        h2 = DN_ALPHA * x_ref[...] + _dot(merged, wout_ref[...])
        xh2, r2 = _ln_stats(h2)
        err = xh2 * lng + lnb - tgt_ref[...]
        d_out = err * (1.0 / D_MODEL)
        vec_o[GV_LNG:GV_LNG + 1, :] += jnp.sum(d_out * xh2, axis=0, keepdims=True)
        vec_o[GV_LNB:GV_LNB + 1, :] += jnp.sum(d_out, axis=0, keepdims=True)
        vec_o[GV_LOSS:GV_LOSS + 1, :] += jnp.sum(err * err, axis=0, keepdims=True) * (0.5 / D_MODEL)

        d_h2 = _ln_bwd(d_out, lng, xh2, r2)
        dh2_o[...] = d_h2
        dh2b = d_h2.astype(BF16)
        gwout_o[...] += _dot_tn(merged, dh2b)
        d_m = _dot_nt(dh2b, wout_ref[...])
        d_oa = d_m[:, 0:512]
        d_ob = d_m[:, 512:1024]
        do_o[...] = (d_oa * silu_a).astype(BF16)
        dgate_o[:, 0:512] = (d_oa * o * (sig_a * (1.0 + za * (1.0 - sig_a)))).astype(BF16)
        dgate_o[:, 1536:2048] = (d_ob * sgu * (sig_b * (1.0 + zb * (1.0 - sig_b)))).astype(BF16)
        d_sgu = d_ob * silu_b
        dgate_o[:, 512:1024] = (d_sgu * svb * du).astype(BF16)
        d_sv = d_sgu * u
        acc = bacc_ref[...]
        for c in range(t // CHUNK):
            acc = acc + d_sv[c * CHUNK:(c + 1) * CHUNK, :]
        bacc_ref[...] = acc
        d_svb = d_sv.astype(BF16)
        for c in range(t // CHUNK):
            for p in range(PAIRS):
                blk = d_svb[c * CHUNK:(c + 1) * CHUNK, p * HP:(p + 1) * HP]
                vblk = vln[c * CHUNK:(c + 1) * CHUNK, p * HP:(p + 1) * HP]
                first = _lane_lt64(blk.shape)
                gwsp_o[2 * p] += _dot_nt(jnp.where(first, blk, jnp.zeros_like(blk)), vblk)
                gwsp_o[2 * p + 1] += _dot_nt(jnp.where(first, jnp.zeros_like(blk), blk), vblk)
        _spatial_mix(wtt_ref, d_svb, dvln_ref, t)
        d_vln = dvln_ref[...]
        vec_o[GV_SG:GV_SG + 1, 0:GW] += jnp.sum(d_vln * xh, axis=0, keepdims=True)
        vec_o[GV_SB:GV_SB + 1, 0:GW] += jnp.sum(d_vln, axis=0, keepdims=True)
        dgate_o[:, 1024:1536] = (_ln_bwd(d_vln, sg, xh, r) * dgv).astype(BF16)

        @pl.when(i == nt - 1)
        def _():
            tri = (lax.broadcasted_iota(jnp.int32, (CHUNK, CHUNK), 1)
                   <= lax.broadcasted_iota(jnp.int32, (CHUNK, CHUNK), 0))
            for h in range(HEADS):
                gwsp_o[h] = jnp.where(tri, gwsp_o[h], 0.0)
            lane = lax.broadcasted_iota(jnp.int32, (CHUNK, HP), 1)
            res = jnp.zeros((CHUNK, HP), F32)
            for h in range(HEADS):
                p, a = divmod(h, 2)
                blk = bacc_ref[:, p * HP:(p + 1) * HP]
                part = jnp.where(_lane_lt64(blk.shape) == (a == 0), blk, 0.0)
                res = jnp.where(lane == h, jnp.sum(part, axis=-1, keepdims=True), res)
            vec_o[GV_BSP:GV_BSP + HEADS, 0:HP] = res.T[0:HEADS, :]
            lane1 = lax.broadcasted_iota(jnp.int32, (1, D_MODEL), 1)
            total = jnp.sum(vec_o[GV_LOSS:GV_LOSS + 1, :], axis=-1, keepdims=True)
            vec_o[GV_LOSS:GV_LOSS + 1, :] = jnp.where(lane1 == 0, total, 0.0)

    tile = lambda w: pl.BlockSpec((t, w), lambda i: (i, 0))
    full = lambda a: pl.BlockSpec(a.shape, lambda i: (0,) * a.ndim)
    const = lambda s: pl.BlockSpec(s, lambda i: (0,) * len(s))
    return pl.pallas_call(
        body, name="post", grid=(nt,),
        in_specs=[tile(D_MODEL), tile(D_MODEL), tile(MLA_W), tile(2048), full(wout), full(pvec),
                  full(wt), full(wtt), full(bsp)],
        out_specs=[tile(D_MODEL), tile(MLA_W), tile(2048), const((D_MODEL, D_MODEL)),
                   const((HEADS, CHUNK, CHUNK)), const((GV_ROWS, D_MODEL))],
        out_shape=[jax.ShapeDtypeStruct((seq, D_MODEL), F32), jax.ShapeDtypeStruct((seq, MLA_W), BF16),
                   jax.ShapeDtypeStruct((seq, 2048), BF16), jax.ShapeDtypeStruct((D_MODEL, D_MODEL), F32),
                   jax.ShapeDtypeStruct((HEADS, CHUNK, CHUNK), F32), jax.ShapeDtypeStruct((GV_ROWS, D_MODEL), F32)],
        scratch_shapes=[pltpu.VMEM((t, GW), F32), pltpu.VMEM((t, GW), F32), pltpu.VMEM((CHUNK, GW), F32)],
        compiler_params=pltpu.CompilerParams(dimension_semantics=("arbitrary",), vmem_limit_bytes=VMEM_LIMIT),
    )(x, tgt, o, gate, wout, pvec, wt, wtt, bsp)


def _attn_bwd(q, k, v, do, o, lse, cs, pvec):
    seq = q.shape[0]
    b = ATT_BLK
    nq = seq // b

    def body(q_ref, k_ref, v_ref, do_ref, o_ref, lse_ref, cs_ref, pv_ref, dq_o, dk_o, dv_o, dk_acc, dv_acc):
        i = pl.program_id(1)

        @pl.when(i == 0)
        def _():
            dk_acc[...] = jnp.zeros_like(dk_acc)
            dv_acc[...] = jnp.zeros_like(dv_acc)

        first = _lane_lt64((b, HP))
        do = do_ref[...]
        zero = jnp.zeros_like(do)
        dos = [jnp.where(first, do, zero), jnp.where(first, zero, do)]
        prod_t = (do.astype(F32) * o_ref[...]).T
        deltas = [jnp.sum(prod_t[0:64, :], axis=0, keepdims=True),
                  jnp.sum(prod_t[64:128, :], axis=0, keepdims=True)]
        lses = [lse_ref[0:1, :], lse_ref[1:2, :]]
        qs = [q_ref[:, a * HP:(a + 1) * HP] for a in range(2)]

        def step(j, dqs, masked):
            rows = pl.ds(pl.multiple_of(j * b, b), b)
            vb = v_ref[rows, :]
            new_dq = []
            dvs = []
            for a in range(2):
                kb = k_ref[rows, a * HP:(a + 1) * HP]
                pt = jnp.exp2(_dot_nt(kb, qs[a]) - lses[a])
                if masked:
                    ki = lax.broadcasted_iota(jnp.int32, pt.shape, 0)
                    qi = lax.broadcasted_iota(jnp.int32, pt.shape, 1)
                    pt = jnp.where(ki <= qi, pt, 0.0)
                dvs.append(_dot(pt.astype(BF16), do))
                dpt = _dot_nt(vb, dos[a])
                dst = (pt * (dpt - deltas[a])).astype(BF16)
                dk_acc[rows, a * HP:(a + 1) * HP] += _dot(dst, qs[a])
                new_dq.append(dqs[a] + _dot_tn(dst, kb))
            dv_acc[rows, :] += jnp.where(first, dvs[0], dvs[1])
            return tuple(new_dq)

        init = (jnp.zeros((b, HP), F32), jnp.zeros((b, HP), F32))
        dqs = lax.fori_loop(0, i, lambda j, cr: step(j, cr, False), init)
        dqs = step(i, dqs, True)
        cos = cs_ref[:, 0:HP]
        sin = cs_ref[:, HP:2 * HP]
        s1 = sin * pv_ref[PV_M1:PV_M1 + 1, 0:HP]
        s2 = sin * pv_ref[PV_M2:PV_M2 + 1, 0:HP]
        for a in range(2):
            dq_o[:, a * HP:(a + 1) * HP] = _rope_bwd(dqs[a] * SCALE, cos, s1, s2).astype(BF16)

        @pl.when(i == nq - 1)
        def _():
            dk_o[...] = (dk_acc[...] * (SCALE / SCALE_LOG2E)).astype(BF16)
            dv_o[...] = dv_acc[...].astype(BF16)

    return pl.pallas_call(
        body, name="attn_bwd", grid=(PAIRS, nq),
        in_specs=[pl.BlockSpec((b, 2 * HP), lambda p, i: (i, p)),
                  pl.BlockSpec((seq, 2 * HP), lambda p, i: (0, p)),
                  pl.BlockSpec((seq, HP), lambda p, i: (0, p)),
                  pl.BlockSpec((b, HP), lambda p, i: (i, p)),
                  pl.BlockSpec((b, HP), lambda p, i: (i, p)),
                  pl.BlockSpec((None, 2, b), lambda p, i: (p, 0, i)),
                  pl.BlockSpec((b, 2 * HP), lambda p, i: (i, 0)),
                  pl.BlockSpec(pvec.shape, lambda p, i: (0, 0))],
        out_specs=[pl.BlockSpec((b, 2 * HP), lambda p, i: (i, p)),
                   pl.BlockSpec((seq, 2 * HP), lambda p, i: (0, p)),
                   pl.BlockSpec((seq, HP), lambda p, i: (0, p))],
        out_shape=[jax.ShapeDtypeStruct((seq, HEADS * HP), BF16),
                   jax.ShapeDtypeStruct((seq, HEADS * HP), BF16),
                   jax.ShapeDtypeStruct((seq, MLA_W), BF16)],
        scratch_shapes=[pltpu.VMEM((seq, 2 * HP), F32), pltpu.VMEM((seq, HP), F32)],
        compiler_params=pltpu.CompilerParams(dimension_semantics=("arbitrary", "arbitrary"),
                                             vmem_limit_bytes=VMEM_LIMIT),
    )(q, k, v, do, o, lse, cs, pvec)


def _bwd_pre(x, dh2, cq, ckv, cs, dq, dk, dv, dgate, win, wuq, wkv, pvec, gvec):
    seq = x.shape[0]
    t = BWD_TILE

    def body(x_ref, dh2_ref, cq_ref, ckv_ref, cs_ref, dq_ref, dk_ref, dv_ref, dgate_ref,
             win_ref, wuq_ref, wkv_ref, pv_ref, gv_ref, gx_o, gwin_o, gwuq_o, gwkv_o, vec_o):
        i = pl.program_id(0)

        @pl.when(i == 0)
        def _():
            gwin_o[...] = jnp.zeros_like(gwin_o)
            gwuq_o[...] = jnp.zeros_like(gwuq_o)
            gwkv_o[...] = jnp.zeros_like(gwkv_o)
            vec_o[...] = gv_ref[...]

        qg = pv_ref[PV_QG:PV_QG + 1, 0:Q_LORA]
        kvg = pv_ref[PV_KVG:PV_KVG + 1, 0:KV_LORA]
        dq = dq_ref[...]
        cqh, rq = _rms_stats(cq_ref[...])
        d_cqn = _dot_nt(dq, wuq_ref[...])
        gwuq_o[...] += _dot_tn((cqh * qg).astype(BF16), dq)
        vec_o[GV_QG:GV_QG + 1, 0:Q_LORA] += jnp.sum(d_cqn * cqh, axis=0, keepdims=True)
        d_cq = _rms_bwd(d_cqn, qg, cqh, rq)

        dk = dk_ref[...]
        dkv = jnp.concatenate([dk, dv_ref[...]], axis=1)
        ckvh, rkv = _rms_stats(ckv_ref[...])
        d_ckvn = _dot_nt(dkv, wkv_ref[...])
        gwkv_o[...] += _dot_tn((ckvh * kvg).astype(BF16), dkv)
        vec_o[GV_KVG:GV_KVG + 1, 0:KV_LORA] += jnp.sum(d_ckvn * ckvh, axis=0, keepdims=True)
        d_ckv = _rms_bwd(d_ckvn, kvg, ckvh, rkv)

        dks = dk[:, 0:HP].astype(F32)
        for h in range(1, HEADS):
            dks = dks + dk[:, h * HP:(h + 1) * HP].astype(F32)
        cos = cs_ref[:, 0:HP]
        sin = cs_ref[:, HP:2 * HP]
        d_kr = _rope_bwd(dks, cos, sin * pv_ref[PV_M1:PV_M1 + 1, 0:HP], sin * pv_ref[PV_M2:PV_M2 + 1, 0:HP])

        d_proj = jnp.concatenate([d_cq.astype(BF16), d_ckv.astype(BF16), d_kr.astype(BF16), dgate_ref[...]], axis=1)
        gwin_o[...] += _dot_tn(x_ref[...].astype(BF16), d_proj)
        gx_o[...] = DN_ALPHA * dh2_ref[...] + _dot_nt(d_proj, win_ref[...])

    tile = lambda w: pl.BlockSpec((t, w), lambda i: (i, 0))
    full = lambda a: pl.BlockSpec(a.shape, lambda i: (0,) * a.ndim)
    const = lambda s: pl.BlockSpec(s, lambda i: (0,) * len(s))
    return pl.pallas_call(
        body, name="bwd_pre", grid=(seq // t,),
        in_specs=[tile(D_MODEL), tile(D_MODEL), tile(Q_LORA), tile(KV_LORA), tile(2 * HP), tile(HEADS * HP),
                  tile(HEADS * HP), tile(MLA_W), tile(2048), full(win), full(wuq), full(wkv), full(pvec), full(gvec)],
        out_specs=[tile(D_MODEL), const((D_MODEL, D_INR)), const((Q_LORA, HEADS * HP)),
                   const((KV_LORA, HEADS * HP + MLA_W)), const((GV_ROWS, D_MODEL))],
        out_shape=[jax.ShapeDtypeStruct((seq, D_MODEL), F32), jax.ShapeDtypeStruct((D_MODEL, D_INR), F32),
                   jax.ShapeDtypeStruct((Q_LORA, HEADS * HP), F32),
                   jax.ShapeDtypeStruct((KV_LORA, HEADS * HP + MLA_W), F32),
                   jax.ShapeDtypeStruct((GV_ROWS, D_MODEL), F32)],
        compiler_params=pltpu.CompilerParams(dimension_semantics=("arbitrary",), vmem_limit_bytes=VMEM_LIMIT),
    )(x, dh2, cq, ckv, cs, dq, dk, dv, dgate, win, wuq, wkv, pvec, gvec)


def _grad_reduce(gs, gvec):
    n_arr = len(gs)
    n_big = n_arr - 1
    k1 = lambda n, blk: 4 * n + blk
    k2 = lambda n, kk: 4 * n_arr + 3 * n + kk
    k3 = lambda n: 7 * n_arr + n
    k3w = lambda k: 7 * n_arr + n_big + k
    kv = lambda k: 7 * n_arr + n_big + 7 + k
    n_sem = 7 * n_arr + n_big + 14

    def body(*refs):
        g, gv = refs[0:n_arr], refs[n_arr]
        outs, ov = refs[n_arr + 1:2 * n_arr + 1], refs[2 * n_arr + 1]
        r1 = refs[2 * n_arr + 2:3 * n_arr + 2]
        r2 = refs[3 * n_arr + 2:4 * n_arr + 2]
        s2 = refs[4 * n_arr + 2:5 * n_arr + 2]
        vbuf, send_sems, recv_sems = refs[5 * n_arr + 2:]
        x, y, c = lax.axis_index("x"), lax.axis_index("y"), lax.axis_index("c")
        j = 2 * x + y
        me = 2 * j + c
        sib = (x, y, 1 - c)
        chips = [(1 - x, y), (x, 1 - y), (1 - x, 1 - y)]
        others = [sib] + [(px, py, pc) for (px, py) in chips for pc in (c, 1 - c)]

        def copy(k, src, dst, to):
            return pltpu.make_async_remote_copy(
                src_ref=src, dst_ref=dst, send_sem=send_sems.at[k], recv_sem=recv_sems.at[k],
                device_id=to, device_id_type=MESH)

        l1 = [copy(k1(n, blk), g[n].at[blk, 1 - c], r1[n].at[blk], sib) for n in range(n_arr) for blk in range(4)]
        lv = [copy(kv(k), gv, vbuf.at[me], to) for k, to in enumerate(others)]
        for cp in l1 + lv:
            cp.start()
        for n in range(n_arr):
            for blk in range(4):
                copy(k1(n, blk), g[n].at[blk, c], r1[n].at[blk], sib).wait_recv()
        for n in range(n_arr):
            for blk in range(4):
                r1[n][blk] = g[n][blk, c] + r1[n][blk]
                s2[n][blk] = r1[n][blk].astype(BF16)

        l2 = [copy(k2(n, kk), s2[n].at[2 * px + py], r2[n].at[kk], (px, py, c))
              for n in range(n_arr) for kk, (px, py) in enumerate(chips)]
        for cp in l2:
            cp.start()
        for n in range(n_arr):
            for kk in range(3):
                copy(k2(n, kk), s2[n].at[0], r2[n].at[kk], sib).wait_recv()
        for n in range(n_arr):
            red = ((r1[n][j] + r2[n][0].astype(F32)) + r2[n][1].astype(F32)) + r2[n][2].astype(F32)
            if n < n_big:
                outs[n][c] = red
            else:
                outs[n][j, c] = red

        l3 = [copy(k3(n), outs[n].at[c], outs[n].at[c], sib) for n in range(n_big)]
        l3 += [copy(k3w(k), outs[n_big].at[j, c], outs[n_big].at[j, c], to) for k, to in enumerate(others)]
        for cp in l3:
            cp.start()
        for n in range(n_big):
            copy(k3(n), outs[n].at[1 - c], outs[n].at[1 - c], sib).wait_recv()
        for k, (px, py, pc) in enumerate(others):
            landed = outs[n_big].at[2 * px + py, pc]
            copy(k3w(k), landed, landed, (px, py, pc)).wait_recv()
            copy(kv(k), gv, vbuf.at[4 * px + 2 * py + pc], (px, py, pc)).wait_recv()
        vbuf[me] = gv[...]
        total = vbuf[0]
        for d in range(1, 8):
            total = total + vbuf[d]
        ov[...] = total
        for cp in l1 + lv + l2 + l3:
            cp.wait_send()

    vmem = pl.BlockSpec(memory_space=pltpu.VMEM)
    half_shapes = [a.shape[2:] for a in gs]
    out_shape = [jax.ShapeDtypeStruct((2,) + s, F32) for s in half_shapes[:n_big]]
    out_shape += [jax.ShapeDtypeStruct((4, 2) + half_shapes[n_big], F32), jax.ShapeDtypeStruct(gvec.shape, F32)]
    scratch = [pltpu.VMEM((4,) + s, F32) for s in half_shapes] + [pltpu.VMEM((3,) + s, BF16) for s in half_shapes]
    scratch += [pltpu.VMEM((4,) + s, BF16) for s in half_shapes]
    scratch += [pltpu.VMEM((8,) + gvec.shape, F32), pltpu.SemaphoreType.DMA((n_sem,)), pltpu.SemaphoreType.DMA((n_sem,))]
    return pl.pallas_call(
        body, name="grad_reduce", out_shape=out_shape,
        in_specs=[vmem] * (n_arr + 1), out_specs=[vmem] * (n_arr + 1), scratch_shapes=scratch,
        compiler_params=pltpu.CompilerParams(vmem_limit_bytes=VMEM_LIMIT),
    )(*gs, gvec)


SMALL_ROWS = ((GV_QG, 1, Q_LORA), (GV_KVG, 1, KV_LORA), (GV_SG, 1, GW), (GV_SB, 1, GW),
              (GV_LNG, 1, D_MODEL), (GV_LNB, 1, D_MODEL), (GV_BSP, HEADS, CHUNK))


def _adam_update(g, w, m, v):
    m_new = ADAM_B1 * m + (1.0 - ADAM_B1) * g
    v_new = ADAM_B2 * v + (1.0 - ADAM_B2) * (g * g)
    m_hat = m_new / (1.0 - ADAM_B1 ** ADAM_STEP)
    v_hat = v_new / (1.0 - ADAM_B2 ** ADAM_STEP)
    return -ADAM_LR * (m_hat / (jnp.sqrt(v_hat) + ADAM_EPS) + ADAM_WD * w), m_new, v_new


def _adamw(g_big, w_big, m_big, v_big, gvec, w_small, m_small, v_small):
    nb, ns = len(g_big), len(w_small)

    def body(*refs):
        it = iter(refs)
        take = lambda n: [next(it) for _ in range(n)]
        g_b, w_b, m_b, v_b = take(nb), take(nb), take(nb), take(nb)
        gv = next(it)
        w_s, m_s, v_s = take(ns), take(ns), take(ns)
        d_bo, m_bo, v_bo = take(nb), take(nb), take(nb)
        g_so, d_so, m_so, v_so = take(ns), take(ns), take(ns), take(ns)
        for n in range(nb):
            d_bo[n][...], m_bo[n][...], v_bo[n][...] = _adam_update(g_b[n][...], w_b[n][...], m_b[n][...], v_b[n][...])
        for n, (row, nrow, width) in enumerate(SMALL_ROWS):
            gs = gv[row:row + nrow, 0:width]
            g_so[n][...] = gs
            d_so[n][...], m_so[n][...], v_so[n][...] = _adam_update(gs, w_s[n][...], m_s[n][...], v_s[n][...])

    vmem = pl.BlockSpec(memory_space=pltpu.VMEM)
    big = [jax.ShapeDtypeStruct(a.shape, F32) for a in w_big]
    small = [jax.ShapeDtypeStruct(a.shape, F32) for a in w_small]
    return pl.pallas_call(
        body, name="adamw", out_shape=big * 3 + small * 4,
        in_specs=[vmem] * (4 * nb + 1 + 3 * ns), out_specs=[vmem] * (3 * nb + 4 * ns),
        compiler_params=pltpu.CompilerParams(vmem_limit_bytes=VMEM_LIMIT),
    )(*g_big, *w_big, *m_big, *v_big, gvec, *w_small, *m_small, *v_small)


def kernel(x, positions, w_in, q_norm_g, w_uq, kv_norm_g, w_ukv, sgu_norm_g, sgu_norm_b, w_spatial, b_spatial, w_out, ln_g, ln_b, loss_target, m_w_in, m_q_norm_g, m_w_uq, m_kv_norm_g, m_w_ukv, m_sgu_norm_g, m_sgu_norm_b, m_w_spatial, m_b_spatial, m_w_out, m_ln_g, m_ln_b, v_w_in, v_q_norm_g, v_w_uq, v_kv_norm_g, v_w_ukv, v_sgu_norm_g, v_sgu_norm_b, v_w_spatial, v_b_spatial, v_w_out, v_ln_g, v_ln_b):
    seq = x.shape[1]
    x2 = x.reshape(seq, D_MODEL)
    tgt = loss_target.reshape(seq, D_MODEL)
    pos = positions.reshape(seq, 1)

    a_in, a_uq, a_ukv, a_out = _weight_gather([w_in, w_uq, w_ukv, w_out])
    w_in_f = jnp.swapaxes(a_in, 0, 1).reshape(D_MODEL, D_IN)
    w_uq_f = jnp.swapaxes(a_uq, 0, 1).reshape(Q_LORA, HEADS * (NOPE + ROPE))
    w_ukv_f = jnp.swapaxes(a_ukv, 0, 1).reshape(KV_LORA, HEADS * (NOPE + VDIM))
    wout = a_out.reshape(D_MODEL, D_MODEL)
    zc = lambda n: jnp.zeros((D_MODEL, n), BF16)
    win = jnp.concatenate([w_in_f[:, 0:384], zc(64), w_in_f[:, 384:416], zc(32), w_in_f[:, 416:D_IN]], axis=1)
    wuq = jnp.pad(w_uq_f.reshape(Q_LORA, HEADS, NOPE + ROPE), ((0, 0), (0, 0), (0, HP - NOPE - ROPE)))
    wuq = wuq.reshape(Q_LORA, HEADS * HP)
    ukv = w_ukv_f.reshape(KV_LORA, HEADS, NOPE + VDIM)
    wk = jnp.pad(ukv[:, :, 0:NOPE], ((0, 0), (0, 0), (0, HP - NOPE))).reshape(KV_LORA, HEADS * HP)
    wkv = jnp.concatenate([wk, ukv[:, :, NOPE:].reshape(KV_LORA, MLA_W)], axis=1)

    lane = jnp.arange(HP)
    half = ROPE // 2
    inv_freq = 1.0 / (ROPE_THETA ** (jnp.arange(half, dtype=F32) / half))
    in_rope = (lane >= NOPE) & (lane < NOPE + ROPE)
    invf = jnp.where(in_rope, inv_freq[(lane - NOPE) % half], 0.0)
    m1 = jnp.where((lane >= NOPE) & (lane < NOPE + half), -1.0, 0.0)
    m2 = jnp.where((lane >= NOPE + half) & (lane < NOPE + ROPE), 1.0, 0.0)
    row = lambda a: jnp.pad(a.astype(F32), (0, D_MODEL - a.shape[0]))
    pvec = jnp.stack([row(q_norm_g), row(kv_norm_g), row(sgu_norm_g), row(sgu_norm_b), row(invf), row(m1),
                      row(m2), row(ln_g), row(ln_b)] + [jnp.zeros((D_MODEL,), F32)] * (PV_ROWS - 9))
    tri = jnp.tril(jnp.ones((CHUNK, CHUNK), dtype=bool))
    wt = jnp.where(tri[None], w_spatial, 0.0).astype(BF16)
    wtt = jnp.swapaxes(wt, 1, 2)
    bsp = jnp.repeat(b_spatial.T, VDIM, axis=1)

    cq, ckv, gate, q, k, v, vt, cs = _fwd_pre(x2, pos, win, wuq, wkv, pvec)
    o, lse = _attn_fwd(q, k, vt)
    dh2, do, dgate, g_wout, g_wsp, gvec = _post(x2, tgt, o, gate, wout, pvec, wt, wtt, bsp)
    dq, dk, dv = _attn_bwd(q, k, v, do, o, lse, cs, pvec)
    gx, g_win, g_wuq, g_wkv, gvec = _bwd_pre(x2, dh2, cq, ckv, cs, dq, dk, dv, dgate, win, wuq, wkv, pvec, gvec)

    g_win_f = jnp.concatenate([g_win[:, 0:384], g_win[:, 448:480], g_win[:, 512:D_INR]], axis=1)
    g_wuq_f = g_wuq.reshape(Q_LORA, HEADS, HP)[:, :, 0:NOPE + ROPE].reshape(Q_LORA, HEADS * (NOPE + ROPE))
    g_k = g_wkv[:, 0:HEADS * HP].reshape(KV_LORA, HEADS, HP)[:, :, 0:NOPE]
    g_v = g_wkv[:, HEADS * HP:].reshape(KV_LORA, HEADS, VDIM)
    g_wukv_f = jnp.concatenate([g_k, g_v], axis=2).reshape(KV_LORA, HEADS * (NOPE + VDIM))

    def by_chip(a):
        rows, cols = a.shape[0], a.shape[1] // 4
        return jnp.swapaxes(a.reshape(rows, 4, cols), 0, 1).reshape(4, 2, rows // 2, cols)

    gs = [by_chip(g_win_f), by_chip(g_wuq_f), by_chip(g_wukv_f), g_wout.reshape(4, 2, 128, D_MODEL),
          g_wsp.reshape(4, 2, CHUNK, CHUNK)]
    r_in, r_uq, r_ukv, r_out, r_wsp, r_vec = _grad_reduce(gs, gvec)

    g_big = [r_in.reshape(w_in.shape), r_uq.reshape(w_uq.shape), r_ukv.reshape(w_ukv.shape),
             r_out.reshape(w_out.shape), r_wsp.reshape(w_spatial.shape)]
    small = lambda qg, kvg, sg, sb, lng, lnb, bs: [qg.reshape(1, -1), kvg.reshape(1, -1), sg.reshape(1, -1),
                                                   sb.reshape(1, -1), lng.reshape(1, -1), lnb.reshape(1, -1), bs]
    res = _adamw(g_big, [w_in, w_uq, w_ukv, w_out, w_spatial], [m_w_in, m_w_uq, m_w_ukv, m_w_out, m_w_spatial],
                 [v_w_in, v_w_uq, v_w_ukv, v_w_out, v_w_spatial], r_vec,
                 small(q_norm_g, kv_norm_g, sgu_norm_g, sgu_norm_b, ln_g, ln_b, b_spatial),
                 small(m_q_norm_g, m_kv_norm_g, m_sgu_norm_g, m_sgu_norm_b, m_ln_g, m_ln_b, m_b_spatial),
                 small(v_q_norm_g, v_kv_norm_g, v_sgu_norm_g, v_sgu_norm_b, v_ln_g, v_ln_b, v_b_spatial))

    def ordered(big, sm):
        vec = lambda n: sm[n].reshape(-1)
        return [big[0], vec(0), big[1], vec(1), big[2], vec(2), vec(3), big[4], sm[6], big[3], vec(4), vec(5)]

    loss = r_vec[GV_LOSS, 0]
    return (loss, gx.reshape(1, seq, D_MODEL), *ordered(g_big, res[15:22]), *ordered(res[0:5], res[22:29]),
            *ordered(res[5:10], res[29:36]), *ordered(res[10:15], res[36:43]))
```

```python
import math

import jax
import jax.numpy as jnp
import numpy as np
from jax import lax
from jax.experimental import pallas as pl
from jax.experimental.pallas import tpu as pltpu

F32 = jnp.float32
BF16 = jnp.bfloat16

D_MODEL = 1024
Q_LORA = 256
KV_LORA = 128
HEADS = 8
NOPE = 64
ROPE = 32
VDIM = 64
MLA_W = HEADS * VDIM
GW = 512
CHUNK = 128
HP = 128
PAIRS = HEADS // 2
D_IN = 2464
D_INR = 2560
ROPE_THETA = 10000.0
DN_ALPHA = 2.0 ** 0.25
EPS = 1e-5
SCALE = 1.0 / math.sqrt(NOPE + ROPE)
SCALE_LOG2E = SCALE * 1.4426950408889634
INV_SQRT2 = 0.7071067811865476
INV_SQRT_2PI = 0.3989422804014327

ADAM_LR = 0.001
ADAM_B1 = 0.9
ADAM_B2 = 0.999
ADAM_EPS = 1e-08
ADAM_WD = 0.01
ADAM_STEP = 10

PV_QG, PV_KVG, PV_SG, PV_SB, PV_INVF, PV_M1, PV_M2, PV_LNG, PV_LNB = range(9)
PV_ROWS = 16
GV_QG, GV_KVG, GV_SG, GV_SB, GV_LNG, GV_LNB, GV_LOSS = range(7)
GV_BSP = 8
GV_ROWS = 16

MESH = pl.DeviceIdType.MESH

FWD_TILE = 512
POST_TILE = 512
BWD_TILE = 512
ATT_BLK = 512
ADAM_STEPS = 4
VMEM_LIMIT = 56 * 1024 * 1024


def _dot(a, b):
    return jnp.dot(a, b, preferred_element_type=F32)


def _dot_nt(a, b):
    return lax.dot_general(a, b, (((1,), (1,)), ((), ())), preferred_element_type=F32)


def _dot_tn(a, b):
    return lax.dot_general(a, b, (((0,), (0,)), ((), ())), preferred_element_type=F32)


def _sigmoid(z):
    return pl.reciprocal(1.0 + jnp.exp(-z), approx=True)


def _gelu_and_grad(x):
    cdf = 0.5 * (1.0 + lax.erf(x * INV_SQRT2))
    return x * cdf, cdf + x * (INV_SQRT_2PI * jnp.exp(-0.5 * x * x))


def _rms_stats(x):
    r = lax.rsqrt(jnp.mean(x * x, axis=-1, keepdims=True) + EPS)
    return x * r, r


def _rms_bwd(dy, g, xh, r):
    dyg = dy * g
    return r * (dyg - xh * jnp.mean(dyg * xh, axis=-1, keepdims=True))


def _ln_stats(x):
    mu = jnp.mean(x, axis=-1, keepdims=True)
    xc = x - mu
    r = lax.rsqrt(jnp.mean(xc * xc, axis=-1, keepdims=True) + EPS)
    return xc * r, r


def _ln_bwd(dy, g, xh, r):
    dxh = dy * g
    return r * (dxh - jnp.mean(dxh, axis=-1, keepdims=True) - xh * jnp.mean(dxh * xh, axis=-1, keepdims=True))


def _rope_fwd(t, c, s1, s2):
    return t * c + pltpu.roll(t, HP - 16, 1) * s1 + pltpu.roll(t, 16, 1) * s2


def _rope_bwd(d, c, s1, s2):
    return d * c + pltpu.roll(d * s1, 16, 1) + pltpu.roll(d * s2, HP - 16, 1)


def _lane_lt64(shape):
    return lax.broadcasted_iota(jnp.int32, shape, len(shape) - 1) < 64


def _spatial_mix(w_ref, src, dst_ref, rows):
    for c in range(rows // CHUNK):
        for p in range(PAIRS):
            blk = src[c * CHUNK:(c + 1) * CHUNK, p * HP:(p + 1) * HP]
            a = _dot(w_ref[2 * p], blk)
            b = _dot(w_ref[2 * p + 1], blk)
            dst_ref[c * CHUNK:(c + 1) * CHUNK, p * HP:(p + 1) * HP] = jnp.where(_lane_lt64(a.shape), a, b)


def _gmlp_fwd(u_pre, v_pre, zb, sg, sb, wt_ref, bsp_ref, sv_ref, rows):
    u, du = _gelu_and_grad(u_pre)
    gv, dgv = _gelu_and_grad(v_pre)
    xh, r = _ln_stats(gv)
    vln = (xh * sg + sb).astype(BF16)
    _spatial_mix(wt_ref, vln, sv_ref, rows)
    bias = bsp_ref[...]
    svb = sv_ref[...] + jnp.concatenate([bias] * (rows // CHUNK), axis=0)
    sig = _sigmoid(zb)
    return u, du, dgv, xh, r, vln, svb, sig


def _weight_gather(shards):
    n_arr = len(shards)

    def body(*refs):
        ins, outs = refs[0:n_arr], refs[n_arr:2 * n_arr]
        send_sems, recv_sems = refs[2 * n_arr:]
        x, y, c = lax.axis_index("x"), lax.axis_index("y"), lax.axis_index("c")
        j = 2 * x + y
        sib = (x, y, 1 - c)
        chips = [(1 - x, y), (x, 1 - y), (1 - x, 1 - y)]
        for n in range(n_arr):
            outs[n][j] = ins[n][...].astype(BF16)

        def half(n, blk, core):
            r = shards[n].shape[0] // 2
            return outs[n].at[blk, pl.ds(pl.multiple_of(core * r, 16), r), :]

        def copy(k, ref, to):
            return pltpu.make_async_remote_copy(
                src_ref=ref, dst_ref=ref, send_sem=send_sems.at[k], recv_sem=recv_sems.at[k],
                device_id=to, device_id_type=MESH)

        first = [copy(6 * n + kk, half(n, j, c), (px, py, c))
                 for n in range(n_arr) for kk, (px, py) in enumerate(chips)]
        for cp in first:
            cp.start()
        passed = []
        for n in range(n_arr):
            for kk, (px, py) in enumerate(chips):
                landed = half(n, 2 * px + py, c)
                copy(6 * n + kk, landed, (px, py, c)).wait_recv()
                passed.append(copy(6 * n + 3 + kk, landed, sib))
                passed[-1].start()
        for n in range(n_arr):
            for kk, (px, py) in enumerate(chips):
                copy(6 * n + 3 + kk, half(n, 2 * px + py, 1 - c), sib).wait_recv()
        for cp in first + passed:
            cp.wait_send()

    vmem = pl.BlockSpec(memory_space=pltpu.VMEM)
    return pl.pallas_call(
        body, name="weight_gather",
        out_shape=[jax.ShapeDtypeStruct((4,) + a.shape, BF16) for a in shards],
        in_specs=[vmem] * n_arr, out_specs=[vmem] * n_arr,
        scratch_shapes=[pltpu.SemaphoreType.DMA((6 * n_arr,)), pltpu.SemaphoreType.DMA((6 * n_arr,))],
        compiler_params=pltpu.CompilerParams(vmem_limit_bytes=VMEM_LIMIT),
    )(*shards)


def _fwd_pre(x, pos, win, wuq, wkv, pvec):
    seq = x.shape[0]
    t = FWD_TILE

    def body(x_ref, pos_ref, win_ref, wuq_ref, wkv_ref, pv_ref,
             cq_o, ckv_o, gate_o, q_o, k_o, v_o, vt_o, cs_o):
        proj = _dot(x_ref[...].astype(BF16), win_ref[...])
        cq = proj[:, 0:256]
        ckv = proj[:, 256:384]
        kr = proj[:, 384:512]
        cq_o[...] = cq
        ckv_o[...] = ckv
        gate_o[...] = proj[:, 512:D_INR].astype(BF16)

        ang = pos_ref[...].astype(F32) * pv_ref[PV_INVF:PV_INVF + 1, 0:HP]
        cos = jnp.cos(ang)
        sin = jnp.sin(ang)
        cs_o[:, 0:HP] = cos
        cs_o[:, HP:2 * HP] = sin
        s1 = sin * pv_ref[PV_M1:PV_M1 + 1, 0:HP]
        s2 = sin * pv_ref[PV_M2:PV_M2 + 1, 0:HP]

        cqh, _ = _rms_stats(cq)
        q_all = _dot((cqh * pv_ref[PV_QG:PV_QG + 1, 0:Q_LORA]).astype(BF16), wuq_ref[...])
        ckvh, _ = _rms_stats(ckv)
        kv_all = _dot((ckvh * pv_ref[PV_KVG:PV_KVG + 1, 0:KV_LORA]).astype(BF16), wkv_ref[...])
        krr = _rope_fwd(kr, cos, s1, s2)
        for h in range(HEADS):
            sl = slice(h * HP, (h + 1) * HP)
            q_o[:, sl] = (_rope_fwd(q_all[:, sl], cos, s1, s2) * SCALE_LOG2E).astype(BF16)
            k_o[:, sl] = (kv_all[:, sl] + krr).astype(BF16)
        val = kv_all[:, HEADS * HP:].astype(BF16)
        v_o[...] = val
        vt_o[...] = val.T

    tile = lambda w: pl.BlockSpec((t, w), lambda i: (i, 0))
    full = lambda a: pl.BlockSpec(a.shape, lambda i: (0,) * a.ndim)
    outs = [(Q_LORA, F32), (KV_LORA, F32), (2048, BF16), (HEADS * HP, BF16), (HEADS * HP, BF16), (MLA_W, BF16)]
    per_blk = ATT_BLK // t
    out_specs = [tile(w) for w, _ in outs]
    out_specs += [pl.BlockSpec((None, MLA_W, t), lambda i: (i // per_blk, 0, i % per_blk)), tile(2 * HP)]
    out_shape = [jax.ShapeDtypeStruct((seq, w), d) for w, d in outs]
    out_shape += [jax.ShapeDtypeStruct((seq // ATT_BLK, MLA_W, ATT_BLK), BF16), jax.ShapeDtypeStruct((seq, 2 * HP), F32)]
    return pl.pallas_call(
        body, name="fwd_pre", grid=(seq // t,),
        in_specs=[tile(D_MODEL), tile(1), full(win), full(wuq), full(wkv), full(pvec)],
        out_specs=out_specs, out_shape=out_shape,
        compiler_params=pltpu.CompilerParams(dimension_semantics=("arbitrary",), vmem_limit_bytes=VMEM_LIMIT),
    )(x, pos, win, wuq, wkv, pvec)


def _attn_fwd(q, k, vt):
    seq = q.shape[0]
    b = ATT_BLK
    nq = seq // b

    def body(q_ref, k_ref, vt_ref, o_o, lse_o, m_ref, l_ref, acc_ref, s_even, s_odd):
        i = pl.program_id(1)
        m_ref[...] = jnp.full(m_ref.shape, -jnp.inf, F32)
        l_ref[...] = jnp.zeros(l_ref.shape, F32)
        acc_ref[...] = jnp.zeros(acc_ref.shape, F32)

        def scores(j, s_ref):
            rows = pl.ds(pl.multiple_of(j * b, b), b)
            for a in range(2):
                s_ref[a] = _dot_nt(k_ref[rows, a * HP:(a + 1) * HP], q_ref[:, a * HP:(a + 1) * HP])

        def consume(j, s_ref, masked):
            vt_blk = vt_ref[j]
            for a in range(2):
                st = s_ref[a]
                if masked:
                    ki = lax.broadcasted_iota(jnp.int32, st.shape, 0)
                    qi = lax.broadcasted_iota(jnp.int32, st.shape, 1)
                    st = jnp.where(ki <= qi, st, -jnp.inf)
                m_prev = m_ref[a:a + 1, :]
                m_new = jnp.maximum(m_prev, jnp.max(st, axis=0, keepdims=True))
                alpha = jnp.exp2(m_prev - m_new)
                pt = jnp.exp2(st - m_new)
                l_ref[a:a + 1, :] = alpha * l_ref[a:a + 1, :] + jnp.sum(pt, axis=0, keepdims=True)
                acc_ref[a] = alpha * acc_ref[a] + _dot(vt_blk, pt.astype(BF16))
                m_ref[a:a + 1, :] = m_new

        scores(0, s_even)

        def loop_body(jj, carry):
            scores(2 * jj + 1, s_odd)
            consume(2 * jj, s_even, False)
            scores(2 * jj + 2, s_even)
            consume(2 * jj + 1, s_odd, False)
            return carry

        lax.fori_loop(0, i // 2, loop_body, 0)

        @pl.when(i % 2 == 0)
        def _():
            consume(i, s_even, True)

        @pl.when(i % 2 == 1)
        def _():
            scores(i, s_odd)
            consume(i - 1, s_even, False)
            consume(i, s_odd, True)

        top = lax.broadcasted_iota(jnp.int32, (HP, b), 0) < 64
        o_o[...] = jnp.where(top, acc_ref[0] / l_ref[0:1, :], acc_ref[1] / l_ref[1:2, :]).T
        lse_o[...] = m_ref[0:2, :] + jnp.log2(l_ref[0:2, :])

    return pl.pallas_call(
        body, name="attn_fwd", grid=(PAIRS, nq),
        in_specs=[pl.BlockSpec((b, 2 * HP), lambda p, i: (i, p)),
                  pl.BlockSpec((seq, 2 * HP), lambda p, i: (0, p)),
                  pl.BlockSpec((nq, HP, b), lambda p, i: (0, p, 0))],
        out_specs=[pl.BlockSpec((b, HP), lambda p, i: (i, p)),
                   pl.BlockSpec((None, 2, b), lambda p, i: (p, 0, i))],
        out_shape=[jax.ShapeDtypeStruct((seq, MLA_W), F32),
                   jax.ShapeDtypeStruct((PAIRS, 2, seq), F32)],
        scratch_shapes=[pltpu.VMEM((8, b), F32), pltpu.VMEM((8, b), F32), pltpu.VMEM((2, HP, b), F32),
                        pltpu.VMEM((2, b, b), F32), pltpu.VMEM((2, b, b), F32)],
        compiler_params=pltpu.CompilerParams(dimension_semantics=("arbitrary", "arbitrary"),
                                             vmem_limit_bytes=VMEM_LIMIT),
    )(q, k, vt)


def _post(x, tgt, o, gate, wout, pvec, wt, wtt, bsp):
    seq = x.shape[0]
    t = POST_TILE
    nt = seq // t

    def body(x_ref, tgt_ref, o_ref, gate_ref, wout_ref, pv_ref, wt_ref, wtt_ref, bsp_ref,
             dh2_o, do_o, dgate_o, gwout_o, gwsp_o, vec_o, sv_ref, dvln_ref, bacc_ref):
        i = pl.program_id(0)

        @pl.when(i == 0)
        def _():
            gwout_o[...] = jnp.zeros_like(gwout_o)
            gwsp_o[...] = jnp.zeros_like(gwsp_o)
            vec_o[...] = jnp.zeros_like(vec_o)
            bacc_ref[...] = jnp.zeros_like(bacc_ref)

        za = gate_ref[:, 0:512].astype(F32)
        u_pre = gate_ref[:, 512:1024].astype(F32)
        v_pre = gate_ref[:, 1024:1536].astype(F32)
        zb = gate_ref[:, 1536:2048].astype(F32)
        sg = pv_ref[PV_SG:PV_SG + 1, 0:GW]
        sb = pv_ref[PV_SB:PV_SB + 1, 0:GW]
        lng = pv_ref[PV_LNG:PV_LNG + 1, :]
        lnb = pv_ref[PV_LNB:PV_LNB + 1, :]
        o = o_ref[...]

        sig_a = _sigmoid(za)
        silu_a = za * sig_a
        u, du, dgv, xh, r, vln, svb, sig_b = _gmlp_fwd(u_pre, v_pre, zb, sg, sb, wt_ref, bsp_ref, sv_ref, t)
        silu_b = zb * sig_b
        sgu = u * svb
        merged = jnp.concatenate([o * silu_a, sgu * silu_b], axis=1).astype(BF16)
        h2 = DN_ALPHA * x_ref[...] + _dot(merged, wout_ref[...])
        xh2, r2 = _ln_stats(h2)
        err = xh2 * lng + lnb - tgt_ref[...]
        d_out = err * (1.0 / D_MODEL)
        vec_o[GV_LNG:GV_LNG + 1, :] += jnp.sum(d_out * xh2, axis=0, keepdims=True)
        vec_o[GV_LNB:GV_LNB + 1, :] += jnp.sum(d_out, axis=0, keepdims=True)
        vec_o[GV_LOSS:GV_LOSS + 1, :] += jnp.sum(err * err, axis=0, keepdims=True) * (0.5 / D_MODEL)

        d_h2 = _ln_bwd(d_out, lng, xh2, r2)
        dh2_o[...] = d_h2
        dh2b = d_h2.astype(BF16)
        gwout_o[...] += _dot_tn(merged, dh2b)
        d_m = _dot_nt(dh2b, wout_ref[...])
        d_oa = d_m[:, 0:512]
        d_ob = d_m[:, 512:1024]
        do_o[...] = (d_oa * silu_a).astype(BF16)
        dgate_o[:, 0:512] = (d_oa * o * (sig_a * (1.0 + za * (1.0 - sig_a)))).astype(BF16)
        dgate_o[:, 1536:2048] = (d_ob * sgu * (sig_b * (1.0 + zb * (1.0 - sig_b)))).astype(BF16)
        d_sgu = d_ob * silu_b
        dgate_o[:, 512:1024] = (d_sgu * svb * du).astype(BF16)
        d_sv = d_sgu * u
        acc = bacc_ref[...]
        for c in range(t // CHUNK):
            acc = acc + d_sv[c * CHUNK:(c + 1) * CHUNK, :]
        bacc_ref[...] = acc
        d_svb = d_sv.astype(BF16)
        for c in range(t // CHUNK):
            for p in range(PAIRS):
                blk = d_svb[c * CHUNK:(c + 1) * CHUNK, p * HP:(p + 1) * HP]
                vblk = vln[c * CHUNK:(c + 1) * CHUNK, p * HP:(p + 1) * HP]
                first = _lane_lt64(blk.shape)
                gwsp_o[2 * p] += _dot_nt(jnp.where(first, blk, jnp.zeros_like(blk)), vblk)
                gwsp_o[2 * p + 1] += _dot_nt(jnp.where(first, jnp.zeros_like(blk), blk), vblk)
        _spatial_mix(wtt_ref, d_svb, dvln_ref, t)
        d_vln = dvln_ref[...]
        vec_o[GV_SG:GV_SG + 1, 0:GW] += jnp.sum(d_vln * xh, axis=0, keepdims=True)
        vec_o[GV_SB:GV_SB + 1, 0:GW] += jnp.sum(d_vln, axis=0, keepdims=True)
        dgate_o[:, 1024:1536] = (_ln_bwd(d_vln, sg, xh, r) * dgv).astype(BF16)

        @pl.when(i == nt - 1)
        def _():
            tri = (lax.broadcasted_iota(jnp.int32, (CHUNK, CHUNK), 1)
                   <= lax.broadcasted_iota(jnp.int32, (CHUNK, CHUNK), 0))
            for h in range(HEADS):
                gwsp_o[h] = jnp.where(tri, gwsp_o[h], 0.0)
            lane = lax.broadcasted_iota(jnp.int32, (CHUNK, HP), 1)
            res = jnp.zeros((CHUNK, HP), F32)
            for h in range(HEADS):
                p, a = divmod(h, 2)
                blk = bacc_ref[:, p * HP:(p + 1) * HP]
                part = jnp.where(_lane_lt64(blk.shape) == (a == 0), blk, 0.0)
                res = jnp.where(lane == h, jnp.sum(part, axis=-1, keepdims=True), res)
            vec_o[GV_BSP:GV_BSP + HEADS, 0:HP] = res.T[0:HEADS, :]
            lane1 = lax.broadcasted_iota(jnp.int32, (1, D_MODEL), 1)
            total = jnp.sum(vec_o[GV_LOSS:GV_LOSS + 1, :], axis=-1, keepdims=True)
            vec_o[GV_LOSS:GV_LOSS + 1, :] = jnp.where(lane1 == 0, total, 0.0)

    tile = lambda w: pl.BlockSpec((t, w), lambda i: (i, 0))
    full = lambda a: pl.BlockSpec(a.shape, lambda i: (0,) * a.ndim)
    const = lambda s: pl.BlockSpec(s, lambda i: (0,) * len(s))
    return pl.pallas_call(
        body, name="post", grid=(nt,),
        in_specs=[tile(D_MODEL), tile(D_MODEL), tile(MLA_W), tile(2048), full(wout), full(pvec),
                  full(wt), full(wtt), full(bsp)],
        out_specs=[tile(D_MODEL), tile(MLA_W), tile(2048), const((D_MODEL, D_MODEL)),
                   const((HEADS, CHUNK, CHUNK)), const((GV_ROWS, D_MODEL))],
        out_shape=[jax.ShapeDtypeStruct((seq, D_MODEL), F32), jax.ShapeDtypeStruct((seq, MLA_W), BF16),
                   jax.ShapeDtypeStruct((seq, 2048), BF16), jax.ShapeDtypeStruct((D_MODEL, D_MODEL), F32),
                   jax.ShapeDtypeStruct((HEADS, CHUNK, CHUNK), F32), jax.ShapeDtypeStruct((GV_ROWS, D_MODEL), F32)],
        scratch_shapes=[pltpu.VMEM((t, GW), F32), pltpu.VMEM((t, GW), F32), pltpu.VMEM((CHUNK, GW), F32)],
        compiler_params=pltpu.CompilerParams(dimension_semantics=("arbitrary",), vmem_limit_bytes=VMEM_LIMIT),
    )(x, tgt, o, gate, wout, pvec, wt, wtt, bsp)


def _attn_bwd(q, k, v, do, o, lse, cs, pvec):
    seq = q.shape[0]
    b = ATT_BLK
    nq = seq // b

    def body(q_ref, k_ref, v_ref, do_ref, o_ref, lse_ref, cs_ref, pv_ref, dq_o, dk_o, dv_o, dk_acc, dv_acc):
        i = pl.program_id(1)

        @pl.when(i == 0)
        def _():
            dk_acc[...] = jnp.zeros_like(dk_acc)
            dv_acc[...] = jnp.zeros_like(dv_acc)

        first = _lane_lt64((b, HP))
        do = do_ref[...]
        zero = jnp.zeros_like(do)
        dos = [jnp.where(first, do, zero), jnp.where(first, zero, do)]
        prod_t = (do.astype(F32) * o_ref[...]).T
        deltas = [jnp.sum(prod_t[0:64, :], axis=0, keepdims=True),
                  jnp.sum(prod_t[64:128, :], axis=0, keepdims=True)]
        lses = [lse_ref[0:1, :], lse_ref[1:2, :]]
        qs = [q_ref[:, a * HP:(a + 1) * HP] for a in range(2)]

        def step(j, dqs, masked):
            rows = pl.ds(pl.multiple_of(j * b, b), b)
            vb = v_ref[rows, :]
            new_dq = []
            dvs = []
            for a in range(2):
                kb = k_ref[rows, a * HP:(a + 1) * HP]
                pt = jnp.exp2(_dot_nt(kb, qs[a]) - lses[a])
                if masked:
                    ki = lax.broadcasted_iota(jnp.int32, pt.shape, 0)
                    qi = lax.broadcasted_iota(jnp.int32, pt.shape, 1)
                    pt = jnp.where(ki <= qi, pt, 0.0)
                dvs.append(_dot(pt.astype(BF16), do))
                dpt = _dot_nt(vb, dos[a])
                dst = (pt * (dpt - deltas[a])).astype(BF16)
                dk_acc[rows, a * HP:(a + 1) * HP] += _dot(dst, qs[a])
                new_dq.append(dqs[a] + _dot_tn(dst, kb))
            dv_acc[rows, :] += jnp.where(first, dvs[0], dvs[1])
            return tuple(new_dq)

        init = (jnp.zeros((b, HP), F32), jnp.zeros((b, HP), F32))
        dqs = lax.fori_loop(0, i, lambda j, cr: step(j, cr, False), init)
        dqs = step(i, dqs, True)
        cos = cs_ref[:, 0:HP]
        sin = cs_ref[:, HP:2 * HP]
        s1 = sin * pv_ref[PV_M1:PV_M1 + 1, 0:HP]
        s2 = sin * pv_ref[PV_M2:PV_M2 + 1, 0:HP]
        for a in range(2):
            dq_o[:, a * HP:(a + 1) * HP] = _rope_bwd(dqs[a] * SCALE, cos, s1, s2).astype(BF16)

        @pl.when(i == nq - 1)
        def _():
            dk_o[...] = (dk_acc[...] * (SCALE / SCALE_LOG2E)).astype(BF16)
            dv_o[...] = dv_acc[...].astype(BF16)

    return pl.pallas_call(
        body, name="attn_bwd", grid=(PAIRS, nq),
        in_specs=[pl.BlockSpec((b, 2 * HP), lambda p, i: (i, p)),
                  pl.BlockSpec((seq, 2 * HP), lambda p, i: (0, p)),
                  pl.BlockSpec((seq, HP), lambda p, i: (0, p)),
                  pl.BlockSpec((b, HP), lambda p, i: (i, p)),
                  pl.BlockSpec((b, HP), lambda p, i: (i, p)),
                  pl.BlockSpec((None, 2, b), lambda p, i: (p, 0, i)),
                  pl.BlockSpec((b, 2 * HP), lambda p, i: (i, 0)),
                  pl.BlockSpec(pvec.shape, lambda p, i: (0, 0))],
        out_specs=[pl.BlockSpec((b, 2 * HP), lambda p, i: (i, p)),
                   pl.BlockSpec((seq, 2 * HP), lambda p, i: (0, p)),
                   pl.BlockSpec((seq, HP), lambda p, i: (0, p))],
        out_shape=[jax.ShapeDtypeStruct((seq, HEADS * HP), BF16),
                   jax.ShapeDtypeStruct((seq, HEADS * HP), BF16),
                   jax.ShapeDtypeStruct((seq, MLA_W), BF16)],
        scratch_shapes=[pltpu.VMEM((seq, 2 * HP), F32), pltpu.VMEM((seq, HP), F32)],
        compiler_params=pltpu.CompilerParams(dimension_semantics=("arbitrary", "arbitrary"),
                                             vmem_limit_bytes=VMEM_LIMIT),
    )(q, k, v, do, o, lse, cs, pvec)


def _bwd_pre(x, dh2, cq, ckv, cs, dq, dk, dv, dgate, win, wuq, wkv, pvec, gvec):
    seq = x.shape[0]
    t = BWD_TILE

    def body(x_ref, dh2_ref, cq_ref, ckv_ref, cs_ref, dq_ref, dk_ref, dv_ref, dgate_ref,
             win_ref, wuq_ref, wkv_ref, pv_ref, gv_ref, gx_o, gwin_o, gwuq_o, gwkv_o, vec_o):
        i = pl.program_id(0)

        @pl.when(i == 0)
        def _():
            gwin_o[...] = jnp.zeros_like(gwin_o)
            gwuq_o[...] = jnp.zeros_like(gwuq_o)
            gwkv_o[...] = jnp.zeros_like(gwkv_o)
            vec_o[...] = gv_ref[...]

        qg = pv_ref[PV_QG:PV_QG + 1, 0:Q_LORA]
        kvg = pv_ref[PV_KVG:PV_KVG + 1, 0:KV_LORA]
        dq = dq_ref[...]
        cqh, rq = _rms_stats(cq_ref[...])
        d_cqn = _dot_nt(dq, wuq_ref[...])
        gwuq_o[...] += _dot_tn((cqh * qg).astype(BF16), dq)
        vec_o[GV_QG:GV_QG + 1, 0:Q_LORA] += jnp.sum(d_cqn * cqh, axis=0, keepdims=True)
        d_cq = _rms_bwd(d_cqn, qg, cqh, rq)

        dk = dk_ref[...]
        dkv = jnp.concatenate([dk, dv_ref[...]], axis=1)
        ckvh, rkv = _rms_stats(ckv_ref[...])
        d_ckvn = _dot_nt(dkv, wkv_ref[...])
        gwkv_o[...] += _dot_tn((ckvh * kvg).astype(BF16), dkv)
        vec_o[GV_KVG:GV_KVG + 1, 0:KV_LORA] += jnp.sum(d_ckvn * ckvh, axis=0, keepdims=True)
        d_ckv = _rms_bwd(d_ckvn, kvg, ckvh, rkv)

        dks = dk[:, 0:HP].astype(F32)
        for h in range(1, HEADS):
            dks = dks + dk[:, h * HP:(h + 1) * HP].astype(F32)
        cos = cs_ref[:, 0:HP]
        sin = cs_ref[:, HP:2 * HP]
        d_kr = _rope_bwd(dks, cos, sin * pv_ref[PV_M1:PV_M1 + 1, 0:HP], sin * pv_ref[PV_M2:PV_M2 + 1, 0:HP])

        d_proj = jnp.concatenate([d_cq.astype(BF16), d_ckv.astype(BF16), d_kr.astype(BF16), dgate_ref[...]], axis=1)
        gwin_o[...] += _dot_tn(x_ref[...].astype(BF16), d_proj)
        gx_o[...] = DN_ALPHA * dh2_ref[...] + _dot_nt(d_proj, win_ref[...])

    tile = lambda w: pl.BlockSpec((t, w), lambda i: (i, 0))
    full = lambda a: pl.BlockSpec(a.shape, lambda i: (0,) * a.ndim)
    const = lambda s: pl.BlockSpec(s, lambda i: (0,) * len(s))
    return pl.pallas_call(
        body, name="bwd_pre", grid=(seq // t,),
        in_specs=[tile(D_MODEL), tile(D_MODEL), tile(Q_LORA), tile(KV_LORA), tile(2 * HP), tile(HEADS * HP),
                  tile(HEADS * HP), tile(MLA_W), tile(2048), full(win), full(wuq), full(wkv), full(pvec), full(gvec)],
        out_specs=[tile(D_MODEL), const((D_MODEL, D_INR)), const((Q_LORA, HEADS * HP)),
                   const((KV_LORA, HEADS * HP + MLA_W)), const((GV_ROWS, D_MODEL))],
        out_shape=[jax.ShapeDtypeStruct((seq, D_MODEL), F32), jax.ShapeDtypeStruct((D_MODEL, D_INR), F32),
                   jax.ShapeDtypeStruct((Q_LORA, HEADS * HP), F32),
                   jax.ShapeDtypeStruct((KV_LORA, HEADS * HP + MLA_W), F32),
                   jax.ShapeDtypeStruct((GV_ROWS, D_MODEL), F32)],
        compiler_params=pltpu.CompilerParams(dimension_semantics=("arbitrary",), vmem_limit_bytes=VMEM_LIMIT),
    )(x, dh2, cq, ckv, cs, dq, dk, dv, dgate, win, wuq, wkv, pvec, gvec)


def _grad_reduce(gs, gvec):
    n_arr = len(gs)
    n_big = n_arr - 1
    k1 = lambda n, blk: 4 * n + blk
    k2 = lambda n, kk: 4 * n_arr + 3 * n + kk
    k3 = lambda n: 7 * n_arr + n
    k3w = lambda k: 7 * n_arr + n_big + k
    kv = lambda k: 7 * n_arr + n_big + 7 + k
    n_sem = 7 * n_arr + n_big + 14

    def body(*refs):
        g, gv = refs[0:n_arr], refs[n_arr]
        outs, ov = refs[n_arr + 1:2 * n_arr + 1], refs[2 * n_arr + 1]
        r1 = refs[2 * n_arr + 2:3 * n_arr + 2]
        r2 = refs[3 * n_arr + 2:4 * n_arr + 2]
        s2 = refs[4 * n_arr + 2:5 * n_arr + 2]
        vbuf, send_sems, recv_sems = refs[5 * n_arr + 2:]
        x, y, c = lax.axis_index("x"), lax.axis_index("y"), lax.axis_index("c")
        j = 2 * x + y
        me = 2 * j + c
        sib = (x, y, 1 - c)
        chips = [(1 - x, y), (x, 1 - y), (1 - x, 1 - y)]
        others = [sib] + [(px, py, pc) for (px, py) in chips for pc in (c, 1 - c)]

        def copy(k, src, dst, to):
            return pltpu.make_async_remote_copy(
                src_ref=src, dst_ref=dst, send_sem=send_sems.at[k], recv_sem=recv_sems.at[k],
                device_id=to, device_id_type=MESH)

        l1 = [copy(k1(n, blk), g[n].at[blk, 1 - c], r1[n].at[blk], sib) for n in range(n_arr) for blk in range(4)]
        lv = [copy(kv(k), gv, vbuf.at[me], to) for k, to in enumerate(others)]
        for cp in l1 + lv:
            cp.start()
        for n in range(n_arr):
            for blk in range(4):
                copy(k1(n, blk), g[n].at[blk, c], r1[n].at[blk], sib).wait_recv()
        for n in range(n_arr):
            for blk in range(4):
                r1[n][blk] = g[n][blk, c] + r1[n][blk]
                s2[n][blk] = r1[n][blk].astype(BF16)

        l2 = [copy(k2(n, kk), s2[n].at[2 * px + py], r2[n].at[kk], (px, py, c))
              for n in range(n_arr) for kk, (px, py) in enumerate(chips)]
        for cp in l2:
            cp.start()
        for n in range(n_arr):
            for kk in range(3):
                copy(k2(n, kk), s2[n].at[0], r2[n].at[kk], sib).wait_recv()
        for n in range(n_arr):
            red = ((r1[n][j] + r2[n][0].astype(F32)) + r2[n][1].astype(F32)) + r2[n][2].astype(F32)
            if n < n_big:
                outs[n][c] = red
            else:
                outs[n][j, c] = red

        l3 = [copy(k3(n), outs[n].at[c], outs[n].at[c], sib) for n in range(n_big)]
        l3 += [copy(k3w(k), outs[n_big].at[j, c], outs[n_big].at[j, c], to) for k, to in enumerate(others)]
        for cp in l3:
            cp.start()
        for n in range(n_big):
            copy(k3(n), outs[n].at[1 - c], outs[n].at[1 - c], sib).wait_recv()
        for k, (px, py, pc) in enumerate(others):
            landed = outs[n_big].at[2 * px + py, pc]
            copy(k3w(k), landed, landed, (px, py, pc)).wait_recv()
            copy(kv(k), gv, vbuf.at[4 * px + 2 * py + pc], (px, py, pc)).wait_recv()
        vbuf[me] = gv[...]
        total = vbuf[0]
        for d in range(1, 8):
            total = total + vbuf[d]
        ov[...] = total
        for cp in l1 + lv + l2 + l3:
            cp.wait_send()

    vmem = pl.BlockSpec(memory_space=pltpu.VMEM)
    half_shapes = [a.shape[2:] for a in gs]
    out_shape = [jax.ShapeDtypeStruct((2,) + s, F32) for s in half_shapes[:n_big]]
    out_shape += [jax.ShapeDtypeStruct((4, 2) + half_shapes[n_big], F32), jax.ShapeDtypeStruct(gvec.shape, F32)]
    scratch = [pltpu.VMEM((4,) + s, F32) for s in half_shapes] + [pltpu.VMEM((3,) + s, BF16) for s in half_shapes]
    scratch += [pltpu.VMEM((4,) + s, BF16) for s in half_shapes]
    scratch += [pltpu.VMEM((8,) + gvec.shape, F32), pltpu.SemaphoreType.DMA((n_sem,)), pltpu.SemaphoreType.DMA((n_sem,))]
    return pl.pallas_call(
        body, name="grad_reduce", out_shape=out_shape,
        in_specs=[vmem] * (n_arr + 1), out_specs=[vmem] * (n_arr + 1), scratch_shapes=scratch,
        compiler_params=pltpu.CompilerParams(vmem_limit_bytes=VMEM_LIMIT),
    )(*gs, gvec)


SMALL_ROWS = ((GV_QG, 1, Q_LORA), (GV_KVG, 1, KV_LORA), (GV_SG, 1, GW), (GV_SB, 1, GW),
              (GV_LNG, 1, D_MODEL), (GV_LNB, 1, D_MODEL), (GV_BSP, HEADS, CHUNK))


def _adam_update(g, w, m, v):
    m_new = ADAM_B1 * m + (1.0 - ADAM_B1) * g
    v_new = ADAM_B2 * v + (1.0 - ADAM_B2) * (g * g)
    m_hat = m_new / (1.0 - ADAM_B1 ** ADAM_STEP)
    v_hat = v_new / (1.0 - ADAM_B2 ** ADAM_STEP)
    return -ADAM_LR * (m_hat / (jnp.sqrt(v_hat) + ADAM_EPS) + ADAM_WD * w), m_new, v_new


def _adamw(g_big, w_big, m_big, v_big, gvec, w_small, m_small, v_small):
    nb, ns = len(g_big), len(w_small)

    def body(*refs):
        it = iter(refs)
        take = lambda n: [next(it) for _ in range(n)]
        g_b, w_b, m_b, v_b = take(nb), take(nb), take(nb), take(nb)
        gv = next(it)
        w_s, m_s, v_s = take(ns), take(ns), take(ns)
        d_bo, m_bo, v_bo = take(nb), take(nb), take(nb)
        g_so, d_so, m_so, v_so = take(ns), take(ns), take(ns), take(ns)
        for n in range(nb):
            d_bo[n][...], m_bo[n][...], v_bo[n][...] = _adam_update(g_b[n][...], w_b[n][...], m_b[n][...], v_b[n][...])
        for n, (row, nrow, width) in enumerate(SMALL_ROWS):
            gs = gv[row:row + nrow, 0:width]
            g_so[n][...] = gs
            d_so[n][...], m_so[n][...], v_so[n][...] = _adam_update(gs, w_s[n][...], m_s[n][...], v_s[n][...])

    def rows(a):
        nd = a.ndim
        return pl.BlockSpec((a.shape[0] // ADAM_STEPS,) + a.shape[1:], lambda i: (i,) + (0,) * (nd - 1))

    def whole(a):
        nd = a.ndim
        return pl.BlockSpec(a.shape, lambda i: (0,) * nd)

    big = [jax.ShapeDtypeStruct(a.shape, F32) for a in w_big]
    small = [jax.ShapeDtypeStruct(a.shape, F32) for a in w_small]
    return pl.pallas_call(
        body, name="adamw", grid=(ADAM_STEPS,), out_shape=big * 3 + small * 4,
        in_specs=[rows(a) for a in g_big + w_big + m_big + v_big] + [whole(gvec)]
        + [whole(a) for a in w_small + m_small + v_small],
        out_specs=[rows(a) for a in w_big] * 3 + [whole(a) for a in w_small] * 4,
        compiler_params=pltpu.CompilerParams(dimension_semantics=("arbitrary",), vmem_limit_bytes=VMEM_LIMIT),
    )(*g_big, *w_big, *m_big, *v_big, gvec, *w_small, *m_small, *v_small)


def kernel(x, positions, w_in, q_norm_g, w_uq, kv_norm_g, w_ukv, sgu_norm_g, sgu_norm_b, w_spatial, b_spatial, w_out, ln_g, ln_b, loss_target, m_w_in, m_q_norm_g, m_w_uq, m_kv_norm_g, m_w_ukv, m_sgu_norm_g, m_sgu_norm_b, m_w_spatial, m_b_spatial, m_w_out, m_ln_g, m_ln_b, v_w_in, v_q_norm_g, v_w_uq, v_kv_norm_g, v_w_ukv, v_sgu_norm_g, v_sgu_norm_b, v_w_spatial, v_b_spatial, v_w_out, v_ln_g, v_ln_b):
    seq = x.shape[1]
    x2 = x.reshape(seq, D_MODEL)
    tgt = loss_target.reshape(seq, D_MODEL)
    pos = positions.reshape(seq, 1)

    a_in, a_uq, a_ukv, a_out = _weight_gather([w_in, w_uq, w_ukv, w_out])
    w_uq_f = jnp.swapaxes(a_uq, 0, 1).reshape(Q_LORA, HEADS * (NOPE + ROPE))
    w_ukv_f = jnp.swapaxes(a_ukv, 0, 1).reshape(KV_LORA, HEADS * (NOPE + VDIM))
    wout = a_out.reshape(D_MODEL, D_MODEL)
    zc = lambda n: jnp.zeros((D_MODEL, n), BF16)
    win = jnp.concatenate([a_in[0][:, 0:384], zc(64), a_in[0][:, 384:416], zc(32), a_in[0][:, 416:],
                           a_in[1], a_in[2], a_in[3]], axis=1)
    wuq = jnp.pad(w_uq_f.reshape(Q_LORA, HEADS, NOPE + ROPE), ((0, 0), (0, 0), (0, HP - NOPE - ROPE)))
    wuq = wuq.reshape(Q_LORA, HEADS * HP)
    ukv = w_ukv_f.reshape(KV_LORA, HEADS, NOPE + VDIM)
    wk = jnp.pad(ukv[:, :, 0:NOPE], ((0, 0), (0, 0), (0, HP - NOPE))).reshape(KV_LORA, HEADS * HP)
    wkv = jnp.concatenate([wk, ukv[:, :, NOPE:].reshape(KV_LORA, MLA_W)], axis=1)

    lane = np.arange(HP)
    half = ROPE // 2
    inv_freq = (1.0 / (ROPE_THETA ** (np.arange(half, dtype=np.float32) / half))).astype(np.float32)
    in_rope = (lane >= NOPE) & (lane < NOPE + ROPE)
    invf = jnp.asarray(np.where(in_rope, inv_freq[(lane - NOPE) % half], 0.0).astype(np.float32))
    m1 = jnp.asarray(np.where((lane >= NOPE) & (lane < NOPE + half), -1.0, 0.0).astype(np.float32))
    m2 = jnp.asarray(np.where((lane >= NOPE + half) & (lane < NOPE + ROPE), 1.0, 0.0).astype(np.float32))
    row = lambda a: jnp.pad(a.astype(F32), (0, D_MODEL - a.shape[0]))
    pvec = jnp.stack([row(q_norm_g), row(kv_norm_g), row(sgu_norm_g), row(sgu_norm_b), row(invf), row(m1),
                      row(m2), row(ln_g), row(ln_b)] + [jnp.zeros((D_MODEL,), F32)] * (PV_ROWS - 9))
    tri = jnp.tril(jnp.ones((CHUNK, CHUNK), dtype=bool))
    wt = jnp.where(tri[None], w_spatial, 0.0).astype(BF16)
    wtt = jnp.swapaxes(wt, 1, 2)
    bsp = jnp.repeat(b_spatial.T, VDIM, axis=1)

    cq, ckv, gate, q, k, v, vt, cs = _fwd_pre(x2, pos, win, wuq, wkv, pvec)
    o, lse = _attn_fwd(q, k, vt)
    dh2, do, dgate, g_wout, g_wsp, gvec = _post(x2, tgt, o, gate, wout, pvec, wt, wtt, bsp)
    dq, dk, dv = _attn_bwd(q, k, v, do, o, lse, cs, pvec)
    gx, g_win, g_wuq, g_wkv, gvec = _bwd_pre(x2, dh2, cq, ckv, cs, dq, dk, dv, dgate, win, wuq, wkv, pvec, gvec)

    first = D_INR - 3 * 616
    g_win_0 = jnp.concatenate([g_win[:, 0:384], g_win[:, 448:480], g_win[:, 512:first]], axis=1)
    g_win_b = jnp.stack([g_win_0] + [g_win[:, first + 616 * jb:first + 616 * (jb + 1)] for jb in range(3)])
    g_wuq_f = g_wuq.reshape(Q_LORA, HEADS, HP)[:, :, 0:NOPE + ROPE].reshape(Q_LORA, HEADS * (NOPE + ROPE))
    g_k = g_wkv[:, 0:HEADS * HP].reshape(KV_LORA, HEADS, HP)[:, :, 0:NOPE]
    g_v = g_wkv[:, HEADS * HP:].reshape(KV_LORA, HEADS, VDIM)
    g_wukv_f = jnp.concatenate([g_k, g_v], axis=2).reshape(KV_LORA, HEADS * (NOPE + VDIM))

    def by_chip(a):
        rows, cols = a.shape[0], a.shape[1] // 4
        return jnp.swapaxes(a.reshape(rows, 4, cols), 0, 1).reshape(4, 2, rows // 2, cols)

    gs = [g_win_b.reshape(4, 2, D_MODEL // 2, 616), by_chip(g_wuq_f), by_chip(g_wukv_f), g_wout.reshape(4, 2, 128, D_MODEL),
          g_wsp.reshape(4, 2, CHUNK, CHUNK)]
    r_in, r_uq, r_ukv, r_out, r_wsp, r_vec = _grad_reduce(gs, gvec)

    g_big = [r_in.reshape(w_in.shape), r_uq.reshape(w_uq.shape), r_ukv.reshape(w_ukv.shape),
             r_out.reshape(w_out.shape), r_wsp.reshape(w_spatial.shape)]
    small = lambda qg, kvg, sg, sb, lng, lnb, bs: [qg.reshape(1, -1), kvg.reshape(1, -1), sg.reshape(1, -1),
                                                   sb.reshape(1, -1), lng.reshape(1, -1), lnb.reshape(1, -1), bs]
    res = _adamw(g_big, [w_in, w_uq, w_ukv, w_out, w_spatial], [m_w_in, m_w_uq, m_w_ukv, m_w_out, m_w_spatial],
                 [v_w_in, v_w_uq, v_w_ukv, v_w_out, v_w_spatial], r_vec,
                 small(q_norm_g, kv_norm_g, sgu_norm_g, sgu_norm_b, ln_g, ln_b, b_spatial),
                 small(m_q_norm_g, m_kv_norm_g, m_sgu_norm_g, m_sgu_norm_b, m_ln_g, m_ln_b, m_b_spatial),
                 small(v_q_norm_g, v_kv_norm_g, v_sgu_norm_g, v_sgu_norm_b, v_ln_g, v_ln_b, v_b_spatial))

    def ordered(big, sm):
        vec = lambda n: sm[n].reshape(-1)
        return [big[0], vec(0), big[1], vec(1), big[2], vec(2), vec(3), big[4], sm[6], big[3], vec(4), vec(5)]

    loss = r_vec[GV_LOSS, 0]
    return (loss, gx.reshape(1, seq, D_MODEL), *ordered(g_big, res[15:22]), *ordered(res[0:5], res[22:29]),
            *ordered(res[5:10], res[29:36]), *ordered(res[10:15], res[36:43]))
```

```python
import math

import jax
import jax.numpy as jnp
import numpy as np
from jax import lax
from jax.experimental import pallas as pl
from jax.experimental.pallas import tpu as pltpu

F32 = jnp.float32
BF16 = jnp.bfloat16

D_MODEL = 1024
Q_LORA = 256
KV_LORA = 128
HEADS = 8
NOPE = 64
ROPE = 32
VDIM = 64
MLA_W = HEADS * VDIM
GW = 512
CHUNK = 128
HP = 128
PAIRS = HEADS // 2
D_IN = 2464
D_INR = 2560
ROPE_THETA = 10000.0
DN_ALPHA = 2.0 ** 0.25
EPS = 1e-5
SCALE = 1.0 / math.sqrt(NOPE + ROPE)
SCALE_LOG2E = SCALE * 1.4426950408889634
INV_SQRT2 = 0.7071067811865476
INV_SQRT_2PI = 0.3989422804014327

ADAM_LR = 0.001
ADAM_B1 = 0.9
ADAM_B2 = 0.999
ADAM_EPS = 1e-08
ADAM_WD = 0.01
ADAM_STEP = 10

PV_QG, PV_KVG, PV_SG, PV_SB, PV_INVF, PV_M1, PV_M2, PV_LNG, PV_LNB = range(9)
PV_ROWS = 16
GV_QG, GV_KVG, GV_SG, GV_SB, GV_LNG, GV_LNB, GV_LOSS = range(7)
GV_BSP = 8
GV_ROWS = 16

MESH = pl.DeviceIdType.MESH

FWD_TILE = 512
POST_TILE = 512
BWD_TILE = 512
ATT_BLK = 512
ADAM_STEPS = 4
VMEM_LIMIT = 56 * 1024 * 1024


def _dot(a, b):
    return jnp.dot(a, b, preferred_element_type=F32)


def _dot_nt(a, b):
    return lax.dot_general(a, b, (((1,), (1,)), ((), ())), preferred_element_type=F32)


def _dot_tn(a, b):
    return lax.dot_general(a, b, (((0,), (0,)), ((), ())), preferred_element_type=F32)


def _sigmoid(z):
    return pl.reciprocal(1.0 + jnp.exp(-z), approx=True)


def _gelu_and_grad(x):
    cdf = 0.5 * (1.0 + lax.erf(x * INV_SQRT2))
    return x * cdf, cdf + x * (INV_SQRT_2PI * jnp.exp(-0.5 * x * x))


def _rms_stats(x):
    r = lax.rsqrt(jnp.mean(x * x, axis=-1, keepdims=True) + EPS)
    return x * r, r


def _rms_bwd(dy, g, xh, r):
    dyg = dy * g
    return r * (dyg - xh * jnp.mean(dyg * xh, axis=-1, keepdims=True))


def _ln_stats(x):
    mu = jnp.mean(x, axis=-1, keepdims=True)
    xc = x - mu
    r = lax.rsqrt(jnp.mean(xc * xc, axis=-1, keepdims=True) + EPS)
    return xc * r, r


def _ln_bwd(dy, g, xh, r):
    dxh = dy * g
    return r * (dxh - jnp.mean(dxh, axis=-1, keepdims=True) - xh * jnp.mean(dxh * xh, axis=-1, keepdims=True))


def _rope_fwd(t, c, s1, s2):
    return t * c + pltpu.roll(t, HP - 16, 1) * s1 + pltpu.roll(t, 16, 1) * s2


def _rope_bwd(d, c, s1, s2):
    return d * c + pltpu.roll(d * s1, 16, 1) + pltpu.roll(d * s2, HP - 16, 1)


def _lane_lt64(shape):
    return lax.broadcasted_iota(jnp.int32, shape, len(shape) - 1) < 64


def _spatial_mix(w_ref, src, dst_ref, rows):
    for c in range(rows // CHUNK):
        for p in range(PAIRS):
            blk = src[c * CHUNK:(c + 1) * CHUNK, p * HP:(p + 1) * HP]
            a = _dot(w_ref[2 * p], blk)
            b = _dot(w_ref[2 * p + 1], blk)
            dst_ref[c * CHUNK:(c + 1) * CHUNK, p * HP:(p + 1) * HP] = jnp.where(_lane_lt64(a.shape), a, b)


def _gmlp_fwd(u_pre, v_pre, zb, sg, sb, wt_ref, bsp_ref, sv_ref, rows):
    u, du = _gelu_and_grad(u_pre)
    gv, dgv = _gelu_and_grad(v_pre)
    xh, r = _ln_stats(gv)
    vln = (xh * sg + sb).astype(BF16)
    _spatial_mix(wt_ref, vln, sv_ref, rows)
    bias = bsp_ref[...]
    svb = sv_ref[...] + jnp.concatenate([bias] * (rows // CHUNK), axis=0)
    sig = _sigmoid(zb)
    return u, du, dgv, xh, r, vln, svb, sig


def _weight_gather(shards):
    n_arr = len(shards)

    def body(*refs):
        ins, outs = refs[0:n_arr], refs[n_arr:2 * n_arr]
        send_sems, recv_sems = refs[2 * n_arr:]
        x, y, c = lax.axis_index("x"), lax.axis_index("y"), lax.axis_index("c")
        j = 2 * x + y
        sib = (x, y, 1 - c)
        chips = [(1 - x, y), (x, 1 - y), (1 - x, 1 - y)]
        for n in range(n_arr):
            outs[n][j] = ins[n][...].astype(BF16)

        def half(n, blk, core):
            r = shards[n].shape[0] // 2
            return outs[n].at[blk, pl.ds(pl.multiple_of(core * r, 16), r), :]

        def copy(k, ref, to):
            return pltpu.make_async_remote_copy(
                src_ref=ref, dst_ref=ref, send_sem=send_sems.at[k], recv_sem=recv_sems.at[k],
                device_id=to, device_id_type=MESH)

        first = [copy(6 * n + kk, half(n, j, c), (px, py, c))
                 for n in range(n_arr) for kk, (px, py) in enumerate(chips)]
        for cp in first:
            cp.start()
        passed = []
        for n in range(n_arr):
            for kk, (px, py) in enumerate(chips):
                landed = half(n, 2 * px + py, c)
                copy(6 * n + kk, landed, (px, py, c)).wait_recv()
                passed.append(copy(6 * n + 3 + kk, landed, sib))
                passed[-1].start()
        for n in range(n_arr):
            for kk, (px, py) in enumerate(chips):
                copy(6 * n + 3 + kk, half(n, 2 * px + py, 1 - c), sib).wait_recv()
        for cp in first + passed:
            cp.wait_send()

    vmem = pl.BlockSpec(memory_space=pltpu.VMEM)
    return pl.pallas_call(
        body, name="weight_gather",
        out_shape=[jax.ShapeDtypeStruct((4,) + a.shape, BF16) for a in shards],
        in_specs=[vmem] * n_arr, out_specs=[vmem] * n_arr,
        scratch_shapes=[pltpu.SemaphoreType.DMA((6 * n_arr,)), pltpu.SemaphoreType.DMA((6 * n_arr,))],
        compiler_params=pltpu.CompilerParams(vmem_limit_bytes=VMEM_LIMIT),
    )(*shards)


def _fwd_pre(x, pos, win, wuq, wkv, pvec):
    seq = x.shape[0]
    t = FWD_TILE

    def body(x_ref, pos_ref, win_ref, wuq_ref, wkv_ref, pv_ref,
             cq_o, ckv_o, gate_o, q_o, k_o, v_o, vt_o, cs_o):
        proj = _dot(x_ref[...].astype(BF16), win_ref[...])
        cq = proj[:, 0:256]
        ckv = proj[:, 256:384]
        kr = proj[:, 384:512]
        cq_o[...] = cq
        ckv_o[...] = ckv
        gate_o[...] = proj[:, 512:D_INR].astype(BF16)

        ang = pos_ref[...].astype(F32) * pv_ref[PV_INVF:PV_INVF + 1, 0:HP]
        cos = jnp.cos(ang)
        sin = jnp.sin(ang)
        cs_o[:, 0:HP] = cos
        cs_o[:, HP:2 * HP] = sin
        s1 = sin * pv_ref[PV_M1:PV_M1 + 1, 0:HP]
        s2 = sin * pv_ref[PV_M2:PV_M2 + 1, 0:HP]

        cqh, _ = _rms_stats(cq)
        q_all = _dot((cqh * pv_ref[PV_QG:PV_QG + 1, 0:Q_LORA]).astype(BF16), wuq_ref[...])
        ckvh, _ = _rms_stats(ckv)
        kv_all = _dot((ckvh * pv_ref[PV_KVG:PV_KVG + 1, 0:KV_LORA]).astype(BF16), wkv_ref[...])
        krr = _rope_fwd(kr, cos, s1, s2)
        for h in range(HEADS):
            sl = slice(h * HP, (h + 1) * HP)
            q_o[:, sl] = (_rope_fwd(q_all[:, sl], cos, s1, s2) * SCALE_LOG2E).astype(BF16)
            k_o[:, sl] = (kv_all[:, sl] + krr).astype(BF16)
        val = kv_all[:, HEADS * HP:].astype(BF16)
        v_o[...] = val
        vt_o[...] = val.T

    tile = lambda w: pl.BlockSpec((t, w), lambda i: (i, 0))
    full = lambda a: pl.BlockSpec(a.shape, lambda i: (0,) * a.ndim)
    outs = [(Q_LORA, F32), (KV_LORA, F32), (2048, BF16), (HEADS * HP, BF16), (HEADS * HP, BF16), (MLA_W, BF16)]
    per_blk = ATT_BLK // t
    out_specs = [tile(w) for w, _ in outs]
    out_specs += [pl.BlockSpec((None, MLA_W, t), lambda i: (i // per_blk, 0, i % per_blk)), tile(2 * HP)]
    out_shape = [jax.ShapeDtypeStruct((seq, w), d) for w, d in outs]
    out_shape += [jax.ShapeDtypeStruct((seq // ATT_BLK, MLA_W, ATT_BLK), BF16), jax.ShapeDtypeStruct((seq, 2 * HP), F32)]
    return pl.pallas_call(
        body, name="fwd_pre", grid=(seq // t,),
        in_specs=[tile(D_MODEL), tile(1), full(win), full(wuq), full(wkv), full(pvec)],
        out_specs=out_specs, out_shape=out_shape,
        compiler_params=pltpu.CompilerParams(dimension_semantics=("arbitrary",), vmem_limit_bytes=VMEM_LIMIT),
    )(x, pos, win, wuq, wkv, pvec)


def _attn_fwd(q, k, vt):
    seq = q.shape[0]
    b = ATT_BLK
    nq = seq // b
    assert nq % 2 == 0
    n_off = nq * (nq - 1) // 2

    def body(q_ref, k_ref, vt_ref, o_o, lse_o, m_ref, l_ref, acc_ref, s_even, s_odd):
        m_ref[...] = jnp.full(m_ref.shape, -jnp.inf, F32)
        l_ref[...] = jnp.zeros(l_ref.shape, F32)
        acc_ref[...] = jnp.zeros(acc_ref.shape, F32)

        def scores(i, j, s_ref):
            qrows = pl.ds(pl.multiple_of(i * b, b), b)
            krows = pl.ds(pl.multiple_of(j * b, b), b)
            for a in range(2):
                s_ref[a] = _dot_nt(k_ref[krows, a * HP:(a + 1) * HP], q_ref[qrows, a * HP:(a + 1) * HP])

        def consume(i, j, s_ref, masked):
            vt_blk = vt_ref[j]
            for a in range(2):
                st = s_ref[a]
                if masked:
                    ki = lax.broadcasted_iota(jnp.int32, st.shape, 0)
                    qi = lax.broadcasted_iota(jnp.int32, st.shape, 1)
                    st = jnp.where(ki <= qi, st, -jnp.inf)
                m_prev = m_ref[i, a:a + 1, :]
                m_new = jnp.maximum(m_prev, jnp.max(st, axis=0, keepdims=True))
                alpha = jnp.exp2(m_prev - m_new)
                pt = jnp.exp2(st - m_new)
                l_ref[i, a:a + 1, :] = alpha * l_ref[i, a:a + 1, :] + jnp.sum(pt, axis=0, keepdims=True)
                acc_ref[i, a] = alpha * acc_ref[i, a] + _dot(vt_blk, pt.astype(BF16))
                m_ref[i, a:a + 1, :] = m_new

        def after(i, j):
            wrap = j + 1 >= i
            return jnp.minimum(jnp.where(wrap, i + 1, i), nq - 1), jnp.where(wrap, 0, j + 1)

        if n_off > 0:
            scores(1, 0, s_even)

            def below(u, ij):
                i1, j1 = after(*ij)
                scores(i1, j1, s_odd)
                consume(ij[0], ij[1], s_even, False)
                i2, j2 = after(i1, j1)
                scores(i2, j2, s_even)
                consume(i1, j1, s_odd, False)
                return i2, j2

            last = lax.fori_loop(0, n_off // 2, below, (jnp.int32(1), jnp.int32(0)))
            if n_off % 2:
                consume(last[0], last[1], s_even, False)

        scores(0, 0, s_even)

        def diagonal(u, carry):
            i0 = 2 * u
            scores(i0 + 1, i0 + 1, s_odd)
            consume(i0, i0, s_even, True)
            i2 = jnp.minimum(i0 + 2, nq - 1)
            scores(i2, i2, s_even)
            consume(i0 + 1, i0 + 1, s_odd, True)
            return carry

        lax.fori_loop(0, nq // 2, diagonal, 0)
        top = lax.broadcasted_iota(jnp.int32, (HP, b), 0) < 64

        def finish(i, carry):
            rows = pl.ds(pl.multiple_of(i * b, b), b)
            o_o[rows, :] = jnp.where(top, acc_ref[i, 0] / l_ref[i, 0:1, :], acc_ref[i, 1] / l_ref[i, 1:2, :]).T
            lse_o[i] = m_ref[i, 0:2, :] + jnp.log2(l_ref[i, 0:2, :])
            return carry

        lax.fori_loop(0, nq, finish, 0)

    return pl.pallas_call(
        body, name="attn_fwd", grid=(PAIRS,),
        in_specs=[pl.BlockSpec((seq, 2 * HP), lambda p: (0, p)),
                  pl.BlockSpec((seq, 2 * HP), lambda p: (0, p)),
                  pl.BlockSpec((nq, HP, b), lambda p: (0, p, 0))],
        out_specs=[pl.BlockSpec((seq, HP), lambda p: (0, p)),
                   pl.BlockSpec((None, nq, 2, b), lambda p: (p, 0, 0, 0))],
        out_shape=[jax.ShapeDtypeStruct((seq, MLA_W), F32),
                   jax.ShapeDtypeStruct((PAIRS, nq, 2, b), F32)],
        scratch_shapes=[pltpu.VMEM((nq, 8, b), F32), pltpu.VMEM((nq, 8, b), F32), pltpu.VMEM((nq, 2, HP, b), F32),
                        pltpu.VMEM((2, b, b), F32), pltpu.VMEM((2, b, b), F32)],
        compiler_params=pltpu.CompilerParams(dimension_semantics=("arbitrary",), vmem_limit_bytes=VMEM_LIMIT),
    )(q, k, vt)


def _post(x, tgt, o, gate, wout, pvec, wt, wtt, bsp):
    seq = x.shape[0]
    t = POST_TILE
    nt = seq // t

    def body(x_ref, tgt_ref, o_ref, gate_ref, wout_ref, pv_ref, wt_ref, wtt_ref, bsp_ref,
             dh2_o, do_o, dgate_o, gwout_o, gwsp_o, vec_o, sv_ref, dvln_ref, bacc_ref):
        i = pl.program_id(0)

        @pl.when(i == 0)
        def _():
            gwout_o[...] = jnp.zeros_like(gwout_o)
            gwsp_o[...] = jnp.zeros_like(gwsp_o)
            vec_o[...] = jnp.zeros_like(vec_o)
            bacc_ref[...] = jnp.zeros_like(bacc_ref)

        za = gate_ref[:, 0:512].astype(F32)
        u_pre = gate_ref[:, 512:1024].astype(F32)
        v_pre = gate_ref[:, 1024:1536].astype(F32)
        zb = gate_ref[:, 1536:2048].astype(F32)
        sg = pv_ref[PV_SG:PV_SG + 1, 0:GW]
        sb = pv_ref[PV_SB:PV_SB + 1, 0:GW]
        lng = pv_ref[PV_LNG:PV_LNG + 1, :]
        lnb = pv_ref[PV_LNB:PV_LNB + 1, :]
        o = o_ref[...]

        sig_a = _sigmoid(za)
        silu_a = za * sig_a
        u, du, dgv, xh, r, vln, svb, sig_b = _gmlp_fwd(u_pre, v_pre, zb, sg, sb, wt_ref, bsp_ref, sv_ref, t)
        silu_b = zb * sig_b
        sgu = u * svb
        merged = jnp.concatenate([o * silu_a, sgu * silu_b], axis=1).astype(BF16)
        h2 = DN_ALPHA * x_ref[...] + _dot(merged, wout_ref[...])
        xh2, r2 = _ln_stats(h2)
        err = xh2 * lng + lnb - tgt_ref[...]
        d_out = err * (1.0 / D_MODEL)
        vec_o[GV_LNG:GV_LNG + 1, :] += jnp.sum(d_out * xh2, axis=0, keepdims=True)
        vec_o[GV_LNB:GV_LNB + 1, :] += jnp.sum(d_out, axis=0, keepdims=True)
        vec_o[GV_LOSS:GV_LOSS + 1, :] += jnp.sum(err * err, axis=0, keepdims=True) * (0.5 / D_MODEL)

        d_h2 = _ln_bwd(d_out, lng, xh2, r2)
        dh2_o[...] = d_h2
        dh2b = d_h2.astype(BF16)
        gwout_o[...] += _dot_tn(merged, dh2b)
        d_m = _dot_nt(dh2b, wout_ref[...])
        d_oa = d_m[:, 0:512]
        d_ob = d_m[:, 512:1024]
        do_o[...] = (d_oa * silu_a).astype(BF16)
        dgate_o[:, 0:512] = (d_oa * o * (sig_a * (1.0 + za * (1.0 - sig_a)))).astype(BF16)
        dgate_o[:, 1536:2048] = (d_ob * sgu * (sig_b * (1.0 + zb * (1.0 - sig_b)))).astype(BF16)
        d_sgu = d_ob * silu_b
        dgate_o[:, 512:1024] = (d_sgu * svb * du).astype(BF16)
        d_sv = d_sgu * u
        acc = bacc_ref[...]
        for c in range(t // CHUNK):
            acc = acc + d_sv[c * CHUNK:(c + 1) * CHUNK, :]
        bacc_ref[...] = acc
        d_svb = d_sv.astype(BF16)
        for c in range(t // CHUNK):
            for p in range(PAIRS):
                blk = d_svb[c * CHUNK:(c + 1) * CHUNK, p * HP:(p + 1) * HP]
                vblk = vln[c * CHUNK:(c + 1) * CHUNK, p * HP:(p + 1) * HP]
                first = _lane_lt64(blk.shape)
                gwsp_o[2 * p] += _dot_nt(jnp.where(first, blk, jnp.zeros_like(blk)), vblk)
                gwsp_o[2 * p + 1] += _dot_nt(jnp.where(first, jnp.zeros_like(blk), blk), vblk)
        _spatial_mix(wtt_ref, d_svb, dvln_ref, t)
        d_vln = dvln_ref[...]
        vec_o[GV_SG:GV_SG + 1, 0:GW] += jnp.sum(d_vln * xh, axis=0, keepdims=True)
        vec_o[GV_SB:GV_SB + 1, 0:GW] += jnp.sum(d_vln, axis=0, keepdims=True)
        dgate_o[:, 1024:1536] = (_ln_bwd(d_vln, sg, xh, r) * dgv).astype(BF16)

        @pl.when(i == nt - 1)
        def _():
            tri = (lax.broadcasted_iota(jnp.int32, (CHUNK, CHUNK), 1)
                   <= lax.broadcasted_iota(jnp.int32, (CHUNK, CHUNK), 0))
            for h in range(HEADS):
                gwsp_o[h] = jnp.where(tri, gwsp_o[h], 0.0)
            lane = lax.broadcasted_iota(jnp.int32, (CHUNK, HP), 1)
            res = jnp.zeros((CHUNK, HP), F32)
            for h in range(HEADS):
                p, a = divmod(h, 2)
                blk = bacc_ref[:, p * HP:(p + 1) * HP]
                part = jnp.where(_lane_lt64(blk.shape) == (a == 0), blk, 0.0)
                res = jnp.where(lane == h, jnp.sum(part, axis=-1, keepdims=True), res)
            vec_o[GV_BSP:GV_BSP + HEADS, 0:HP] = res.T[0:HEADS, :]
            lane1 = lax.broadcasted_iota(jnp.int32, (1, D_MODEL), 1)
            total = jnp.sum(vec_o[GV_LOSS:GV_LOSS + 1, :], axis=-1, keepdims=True)
            vec_o[GV_LOSS:GV_LOSS + 1, :] = jnp.where(lane1 == 0, total, 0.0)

    tile = lambda w: pl.BlockSpec((t, w), lambda i: (i, 0))
    full = lambda a: pl.BlockSpec(a.shape, lambda i: (0,) * a.ndim)
    const = lambda s: pl.BlockSpec(s, lambda i: (0,) * len(s))
    return pl.pallas_call(
        body, name="post", grid=(nt,),
        in_specs=[tile(D_MODEL), tile(D_MODEL), tile(MLA_W), tile(2048), full(wout), full(pvec),
                  full(wt), full(wtt), full(bsp)],
        out_specs=[tile(D_MODEL), tile(MLA_W), tile(2048), const((D_MODEL, D_MODEL)),
                   const((HEADS, CHUNK, CHUNK)), const((GV_ROWS, D_MODEL))],
        out_shape=[jax.ShapeDtypeStruct((seq, D_MODEL), F32), jax.ShapeDtypeStruct((seq, MLA_W), BF16),
                   jax.ShapeDtypeStruct((seq, 2048), BF16), jax.ShapeDtypeStruct((D_MODEL, D_MODEL), F32),
                   jax.ShapeDtypeStruct((HEADS, CHUNK, CHUNK), F32), jax.ShapeDtypeStruct((GV_ROWS, D_MODEL), F32)],
        scratch_shapes=[pltpu.VMEM((t, GW), F32), pltpu.VMEM((t, GW), F32), pltpu.VMEM((CHUNK, GW), F32)],
        compiler_params=pltpu.CompilerParams(dimension_semantics=("arbitrary",), vmem_limit_bytes=VMEM_LIMIT),
    )(x, tgt, o, gate, wout, pvec, wt, wtt, bsp)


def _attn_bwd(q, k, v, do, o, lse, cs, pvec):
    seq = q.shape[0]
    b = ATT_BLK
    nq = seq // b

    def body(q_ref, k_ref, v_ref, do_ref, o_ref, lse_ref, cs_ref, pv_ref, dq_o, dk_o, dv_o, dk_acc, dv_acc):
        i = pl.program_id(1)

        @pl.when(i == 0)
        def _():
            dk_acc[...] = jnp.zeros_like(dk_acc)
            dv_acc[...] = jnp.zeros_like(dv_acc)

        first = _lane_lt64((b, HP))
        do = do_ref[...]
        zero = jnp.zeros_like(do)
        dos = [jnp.where(first, do, zero), jnp.where(first, zero, do)]
        prod_t = (do.astype(F32) * o_ref[...]).T
        deltas = [jnp.sum(prod_t[0:64, :], axis=0, keepdims=True),
                  jnp.sum(prod_t[64:128, :], axis=0, keepdims=True)]
        lses = [lse_ref[0:1, :], lse_ref[1:2, :]]
        qs = [q_ref[:, a * HP:(a + 1) * HP] for a in range(2)]

        def step(j, dqs, masked):
            rows = pl.ds(pl.multiple_of(j * b, b), b)
            vb = v_ref[rows, :]
            new_dq = []
            dvs = []
            for a in range(2):
                kb = k_ref[rows, a * HP:(a + 1) * HP]
                pt = jnp.exp2(_dot_nt(kb, qs[a]) - lses[a])
                if masked:
                    ki = lax.broadcasted_iota(jnp.int32, pt.shape, 0)
                    qi = lax.broadcasted_iota(jnp.int32, pt.shape, 1)
                    pt = jnp.where(ki <= qi, pt, 0.0)
                dvs.append(_dot(pt.astype(BF16), do))
                dpt = _dot_nt(vb, dos[a])
                dst = (pt * (dpt - deltas[a])).astype(BF16)
                dk_acc[rows, a * HP:(a + 1) * HP] += _dot(dst, qs[a])
                new_dq.append(dqs[a] + _dot_tn(dst, kb))
            dv_acc[rows, :] += jnp.where(first, dvs[0], dvs[1])
            return tuple(new_dq)

        init = (jnp.zeros((b, HP), F32), jnp.zeros((b, HP), F32))
        dqs = lax.fori_loop(0, i, lambda j, cr: step(j, cr, False), init)
        dqs = step(i, dqs, True)
        cos = cs_ref[:, 0:HP]
        sin = cs_ref[:, HP:2 * HP]
        s1 = sin * pv_ref[PV_M1:PV_M1 + 1, 0:HP]
        s2 = sin * pv_ref[PV_M2:PV_M2 + 1, 0:HP]
        for a in range(2):
            dq_o[:, a * HP:(a + 1) * HP] = _rope_bwd(dqs[a] * SCALE, cos, s1, s2).astype(BF16)

        @pl.when(i == nq - 1)
        def _():
            dk_o[...] = (dk_acc[...] * (SCALE / SCALE_LOG2E)).astype(BF16)
            dv_o[...] = dv_acc[...].astype(BF16)

    return pl.pallas_call(
        body, name="attn_bwd", grid=(PAIRS, nq),
        in_specs=[pl.BlockSpec((b, 2 * HP), lambda p, i: (i, p)),
                  pl.BlockSpec((seq, 2 * HP), lambda p, i: (0, p)),
                  pl.BlockSpec((seq, HP), lambda p, i: (0, p)),
                  pl.BlockSpec((b, HP), lambda p, i: (i, p)),
                  pl.BlockSpec((b, HP), lambda p, i: (i, p)),
                  pl.BlockSpec((None, None, 2, b), lambda p, i: (p, i, 0, 0)),
                  pl.BlockSpec((b, 2 * HP), lambda p, i: (i, 0)),
                  pl.BlockSpec(pvec.shape, lambda p, i: (0, 0))],
        out_specs=[pl.BlockSpec((b, 2 * HP), lambda p, i: (i, p)),
                   pl.BlockSpec((seq, 2 * HP), lambda p, i: (0, p)),
                   pl.BlockSpec((seq, HP), lambda p, i: (0, p))],
        out_shape=[jax.ShapeDtypeStruct((seq, HEADS * HP), BF16),
                   jax.ShapeDtypeStruct((seq, HEADS * HP), BF16),
                   jax.ShapeDtypeStruct((seq, MLA_W), BF16)],
        scratch_shapes=[pltpu.VMEM((seq, 2 * HP), F32), pltpu.VMEM((seq, HP), F32)],
        compiler_params=pltpu.CompilerParams(dimension_semantics=("arbitrary", "arbitrary"),
                                             vmem_limit_bytes=VMEM_LIMIT),
    )(q, k, v, do, o, lse, cs, pvec)


def _bwd_pre(x, dh2, cq, ckv, cs, dq, dk, dv, dgate, win, wuq, wkv, pvec, gvec):
    seq = x.shape[0]
    t = BWD_TILE

    def body(x_ref, dh2_ref, cq_ref, ckv_ref, cs_ref, dq_ref, dk_ref, dv_ref, dgate_ref,
             win_ref, wuq_ref, wkv_ref, pv_ref, gv_ref, gx_o, gwin_o, gwuq_o, gwkv_o, vec_o):
        i = pl.program_id(0)

        @pl.when(i == 0)
        def _():
            gwin_o[...] = jnp.zeros_like(gwin_o)
            gwuq_o[...] = jnp.zeros_like(gwuq_o)
            gwkv_o[...] = jnp.zeros_like(gwkv_o)
            vec_o[...] = gv_ref[...]

        qg = pv_ref[PV_QG:PV_QG + 1, 0:Q_LORA]
        kvg = pv_ref[PV_KVG:PV_KVG + 1, 0:KV_LORA]
        dq = dq_ref[...]
        cqh, rq = _rms_stats(cq_ref[...])
        d_cqn = _dot_nt(dq, wuq_ref[...])
        gwuq_o[...] += _dot_tn((cqh * qg).astype(BF16), dq)
        vec_o[GV_QG:GV_QG + 1, 0:Q_LORA] += jnp.sum(d_cqn * cqh, axis=0, keepdims=True)
        d_cq = _rms_bwd(d_cqn, qg, cqh, rq)

        dk = dk_ref[...]
        dkv = jnp.concatenate([dk, dv_ref[...]], axis=1)
        ckvh, rkv = _rms_stats(ckv_ref[...])
        d_ckvn = _dot_nt(dkv, wkv_ref[...])
        gwkv_o[...] += _dot_tn((ckvh * kvg).astype(BF16), dkv)
        vec_o[GV_KVG:GV_KVG + 1, 0:KV_LORA] += jnp.sum(d_ckvn * ckvh, axis=0, keepdims=True)
        d_ckv = _rms_bwd(d_ckvn, kvg, ckvh, rkv)

        dks = dk[:, 0:HP].astype(F32)
        for h in range(1, HEADS):
            dks = dks + dk[:, h * HP:(h + 1) * HP].astype(F32)
        cos = cs_ref[:, 0:HP]
        sin = cs_ref[:, HP:2 * HP]
        d_kr = _rope_bwd(dks, cos, sin * pv_ref[PV_M1:PV_M1 + 1, 0:HP], sin * pv_ref[PV_M2:PV_M2 + 1, 0:HP])

        d_proj = jnp.concatenate([d_cq.astype(BF16), d_ckv.astype(BF16), d_kr.astype(BF16), dgate_ref[...]], axis=1)
        gwin_o[...] += _dot_tn(x_ref[...].astype(BF16), d_proj)
        gx_o[...] = DN_ALPHA * dh2_ref[...] + _dot_nt(d_proj, win_ref[...])

    tile = lambda w: pl.BlockSpec((t, w), lambda i: (i, 0))
    full = lambda a: pl.BlockSpec(a.shape, lambda i: (0,) * a.ndim)
    const = lambda s: pl.BlockSpec(s, lambda i: (0,) * len(s))
    return pl.pallas_call(
        body, name="bwd_pre", grid=(seq // t,),
        in_specs=[tile(D_MODEL), tile(D_MODEL), tile(Q_LORA), tile(KV_LORA), tile(2 * HP), tile(HEADS * HP),
                  tile(HEADS * HP), tile(MLA_W), tile(2048), full(win), full(wuq), full(wkv), full(pvec), full(gvec)],
        out_specs=[tile(D_MODEL), const((D_MODEL, D_INR)), const((Q_LORA, HEADS * HP)),
                   const((KV_LORA, HEADS * HP + MLA_W)), const((GV_ROWS, D_MODEL))],
        out_shape=[jax.ShapeDtypeStruct((seq, D_MODEL), F32), jax.ShapeDtypeStruct((D_MODEL, D_INR), F32),
                   jax.ShapeDtypeStruct((Q_LORA, HEADS * HP), F32),
                   jax.ShapeDtypeStruct((KV_LORA, HEADS * HP + MLA_W), F32),
                   jax.ShapeDtypeStruct((GV_ROWS, D_MODEL), F32)],
        compiler_params=pltpu.CompilerParams(dimension_semantics=("arbitrary",), vmem_limit_bytes=VMEM_LIMIT),
    )(x, dh2, cq, ckv, cs, dq, dk, dv, dgate, win, wuq, wkv, pvec, gvec)


def _grad_reduce(gs, gvec):
    n_arr = len(gs)
    n_big = n_arr - 1
    k1 = lambda n, blk: 4 * n + blk
    k2 = lambda n, kk: 4 * n_arr + 3 * n + kk
    k3 = lambda n: 7 * n_arr + n
    k3w = lambda k: 7 * n_arr + n_big + k
    kv = lambda k: 7 * n_arr + n_big + 7 + k
    n_sem = 7 * n_arr + n_big + 14

    def body(*refs):
        g, gv = refs[0:n_arr], refs[n_arr]
        outs, ov = refs[n_arr + 1:2 * n_arr + 1], refs[2 * n_arr + 1]
        r1 = refs[2 * n_arr + 2:3 * n_arr + 2]
        r2 = refs[3 * n_arr + 2:4 * n_arr + 2]
        s2 = refs[4 * n_arr + 2:5 * n_arr + 2]
        vbuf, send_sems, recv_sems = refs[5 * n_arr + 2:]
        x, y, c = lax.axis_index("x"), lax.axis_index("y"), lax.axis_index("c")
        j = 2 * x + y
        me = 2 * j + c
        sib = (x, y, 1 - c)
        chips = [(1 - x, y), (x, 1 - y), (1 - x, 1 - y)]
        others = [sib] + [(px, py, pc) for (px, py) in chips for pc in (c, 1 - c)]

        def copy(k, src, dst, to):
            return pltpu.make_async_remote_copy(
                src_ref=src, dst_ref=dst, send_sem=send_sems.at[k], recv_sem=recv_sems.at[k],
                device_id=to, device_id_type=MESH)

        l1 = [copy(k1(n, blk), g[n].at[blk, 1 - c], r1[n].at[blk], sib) for n in range(n_arr) for blk in range(4)]
        lv = [copy(kv(k), gv, vbuf.at[me], to) for k, to in enumerate(others)]
        for cp in l1 + lv:
            cp.start()
        for n in range(n_arr):
            for blk in range(4):
                copy(k1(n, blk), g[n].at[blk, c], r1[n].at[blk], sib).wait_recv()
        for n in range(n_arr):
            for blk in range(4):
                r1[n][blk] = g[n][blk, c] + r1[n][blk]
                s2[n][blk] = r1[n][blk].astype(BF16)

        l2 = [copy(k2(n, kk), s2[n].at[2 * px + py], r2[n].at[kk], (px, py, c))
              for n in range(n_arr) for kk, (px, py) in enumerate(chips)]
        for cp in l2:
            cp.start()
        for n in range(n_arr):
            for kk in range(3):
                copy(k2(n, kk), s2[n].at[0], r2[n].at[kk], sib).wait_recv()
        for n in range(n_arr):
            red = ((r1[n][j] + r2[n][0].astype(F32)) + r2[n][1].astype(F32)) + r2[n][2].astype(F32)
            if n < n_big:
                outs[n][c] = red
            else:
                outs[n][j, c] = red

        l3 = [copy(k3(n), outs[n].at[c], outs[n].at[c], sib) for n in range(n_big)]
        l3 += [copy(k3w(k), outs[n_big].at[j, c], outs[n_big].at[j, c], to) for k, to in enumerate(others)]
        for cp in l3:
            cp.start()
        for n in range(n_big):
            copy(k3(n), outs[n].at[1 - c], outs[n].at[1 - c], sib).wait_recv()
        for k, (px, py, pc) in enumerate(others):
            landed = outs[n_big].at[2 * px + py, pc]
            copy(k3w(k), landed, landed, (px, py, pc)).wait_recv()
            copy(kv(k), gv, vbuf.at[4 * px + 2 * py + pc], (px, py, pc)).wait_recv()
        vbuf[me] = gv[...]
        total = vbuf[0]
        for d in range(1, 8):
            total = total + vbuf[d]
        ov[...] = total
        for cp in l1 + lv + l2 + l3:
            cp.wait_send()

    vmem = pl.BlockSpec(memory_space=pltpu.VMEM)
    half_shapes = [a.shape[2:] for a in gs]
    out_shape = [jax.ShapeDtypeStruct((2,) + s, F32) for s in half_shapes[:n_big]]
    out_shape += [jax.ShapeDtypeStruct((4, 2) + half_shapes[n_big], F32), jax.ShapeDtypeStruct(gvec.shape, F32)]
    scratch = [pltpu.VMEM((4,) + s, F32) for s in half_shapes] + [pltpu.VMEM((3,) + s, BF16) for s in half_shapes]
    scratch += [pltpu.VMEM((4,) + s, BF16) for s in half_shapes]
    scratch += [pltpu.VMEM((8,) + gvec.shape, F32), pltpu.SemaphoreType.DMA((n_sem,)), pltpu.SemaphoreType.DMA((n_sem,))]
    return pl.pallas_call(
        body, name="grad_reduce", out_shape=out_shape,
        in_specs=[vmem] * (n_arr + 1), out_specs=[vmem] * (n_arr + 1), scratch_shapes=scratch,
        compiler_params=pltpu.CompilerParams(vmem_limit_bytes=VMEM_LIMIT),
    )(*gs, gvec)


SMALL_ROWS = ((GV_QG, 1, Q_LORA), (GV_KVG, 1, KV_LORA), (GV_SG, 1, GW), (GV_SB, 1, GW),
              (GV_LNG, 1, D_MODEL), (GV_LNB, 1, D_MODEL), (GV_BSP, HEADS, CHUNK))


def _adam_update(g, w, m, v):
    m_new = ADAM_B1 * m + (1.0 - ADAM_B1) * g
    v_new = ADAM_B2 * v + (1.0 - ADAM_B2) * (g * g)
    m_hat = m_new / (1.0 - ADAM_B1 ** ADAM_STEP)
    v_hat = v_new / (1.0 - ADAM_B2 ** ADAM_STEP)
    return -ADAM_LR * (m_hat / (jnp.sqrt(v_hat) + ADAM_EPS) + ADAM_WD * w), m_new, v_new


def _adamw(g_big, w_big, m_big, v_big, gvec, w_small, m_small, v_small):
    nb, ns = len(g_big), len(w_small)

    def body(*refs):
        it = iter(refs)
        take = lambda n: [next(it) for _ in range(n)]
        g_b, w_b, m_b, v_b = take(nb), take(nb), take(nb), take(nb)
        gv = next(it)
        w_s, m_s, v_s = take(ns), take(ns), take(ns)
        d_bo, m_bo, v_bo = take(nb), take(nb), take(nb)
        g_so, d_so, m_so, v_so = take(ns), take(ns), take(ns), take(ns)
        for n in range(nb):
            d_bo[n][...], m_bo[n][...], v_bo[n][...] = _adam_update(g_b[n][...], w_b[n][...], m_b[n][...], v_b[n][...])
        for n, (row, nrow, width) in enumerate(SMALL_ROWS):
            gs = gv[row:row + nrow, 0:width]
            g_so[n][...] = gs
            d_so[n][...], m_so[n][...], v_so[n][...] = _adam_update(gs, w_s[n][...], m_s[n][...], v_s[n][...])

    def rows(a):
        nd = a.ndim
        return pl.BlockSpec((a.shape[0] // ADAM_STEPS,) + a.shape[1:], lambda i: (i,) + (0,) * (nd - 1))

    def whole(a):
        nd = a.ndim
        return pl.BlockSpec(a.shape, lambda i: (0,) * nd)

    big = [jax.ShapeDtypeStruct(a.shape, F32) for a in w_big]
    small = [jax.ShapeDtypeStruct(a.shape, F32) for a in w_small]
    return pl.pallas_call(
        body, name="adamw", grid=(ADAM_STEPS,), out_shape=big * 3 + small * 4,
        in_specs=[rows(a) for a in g_big + w_big + m_big + v_big] + [whole(gvec)]
        + [whole(a) for a in w_small + m_small + v_small],
        out_specs=[rows(a) for a in w_big] * 3 + [whole(a) for a in w_small] * 4,
        compiler_params=pltpu.CompilerParams(dimension_semantics=("arbitrary",), vmem_limit_bytes=VMEM_LIMIT),
    )(*g_big, *w_big, *m_big, *v_big, gvec, *w_small, *m_small, *v_small)


def kernel(x, positions, w_in, q_norm_g, w_uq, kv_norm_g, w_ukv, sgu_norm_g, sgu_norm_b, w_spatial, b_spatial, w_out, ln_g, ln_b, loss_target, m_w_in, m_q_norm_g, m_w_uq, m_kv_norm_g, m_w_ukv, m_sgu_norm_g, m_sgu_norm_b, m_w_spatial, m_b_spatial, m_w_out, m_ln_g, m_ln_b, v_w_in, v_q_norm_g, v_w_uq, v_kv_norm_g, v_w_ukv, v_sgu_norm_g, v_sgu_norm_b, v_w_spatial, v_b_spatial, v_w_out, v_ln_g, v_ln_b):
    seq = x.shape[1]
    x2 = x.reshape(seq, D_MODEL)
    tgt = loss_target.reshape(seq, D_MODEL)
    pos = positions.reshape(seq, 1)

    a_in, a_uq, a_ukv, a_out = _weight_gather([w_in, w_uq, w_ukv, w_out])
    w_uq_f = jnp.swapaxes(a_uq, 0, 1).reshape(Q_LORA, HEADS * (NOPE + ROPE))
    w_ukv_f = jnp.swapaxes(a_ukv, 0, 1).reshape(KV_LORA, HEADS * (NOPE + VDIM))
    wout = a_out.reshape(D_MODEL, D_MODEL)
    zc = lambda n: jnp.zeros((D_MODEL, n), BF16)
    win = jnp.concatenate([a_in[0][:, 0:384], zc(64), a_in[0][:, 384:416], zc(32), a_in[0][:, 416:],
                           a_in[1], a_in[2], a_in[3]], axis=1)
    wuq = jnp.pad(w_uq_f.reshape(Q_LORA, HEADS, NOPE + ROPE), ((0, 0), (0, 0), (0, HP - NOPE - ROPE)))
    wuq = wuq.reshape(Q_LORA, HEADS * HP)
    ukv = w_ukv_f.reshape(KV_LORA, HEADS, NOPE + VDIM)
    wk = jnp.pad(ukv[:, :, 0:NOPE], ((0, 0), (0, 0), (0, HP - NOPE))).reshape(KV_LORA, HEADS * HP)
    wkv = jnp.concatenate([wk, ukv[:, :, NOPE:].reshape(KV_LORA, MLA_W)], axis=1)

    lane = np.arange(HP)
    half = ROPE // 2
    inv_freq = (1.0 / (ROPE_THETA ** (np.arange(half, dtype=np.float32) / half))).astype(np.float32)
    in_rope = (lane >= NOPE) & (lane < NOPE + ROPE)
    invf = jnp.asarray(np.where(in_rope, inv_freq[(lane - NOPE) % half], 0.0).astype(np.float32))
    m1 = jnp.asarray(np.where((lane >= NOPE) & (lane < NOPE + half), -1.0, 0.0).astype(np.float32))
    m2 = jnp.asarray(np.where((lane >= NOPE + half) & (lane < NOPE + ROPE), 1.0, 0.0).astype(np.float32))
    row = lambda a: jnp.pad(a.astype(F32), (0, D_MODEL - a.shape[0]))
    pvec = jnp.stack([row(q_norm_g), row(kv_norm_g), row(sgu_norm_g), row(sgu_norm_b), row(invf), row(m1),
                      row(m2), row(ln_g), row(ln_b)] + [jnp.zeros((D_MODEL,), F32)] * (PV_ROWS - 9))
    tri = jnp.tril(jnp.ones((CHUNK, CHUNK), dtype=bool))
    wt = jnp.where(tri[None], w_spatial, 0.0).astype(BF16)
    wtt = jnp.swapaxes(wt, 1, 2)
    bsp = jnp.repeat(b_spatial.T, VDIM, axis=1)

    cq, ckv, gate, q, k, v, vt, cs = _fwd_pre(x2, pos, win, wuq, wkv, pvec)
    o, lse = _attn_fwd(q, k, vt)
    dh2, do, dgate, g_wout, g_wsp, gvec = _post(x2, tgt, o, gate, wout, pvec, wt, wtt, bsp)
    dq, dk, dv = _attn_bwd(q, k, v, do, o, lse, cs, pvec)
    gx, g_win, g_wuq, g_wkv, gvec = _bwd_pre(x2, dh2, cq, ckv, cs, dq, dk, dv, dgate, win, wuq, wkv, pvec, gvec)

    first = D_INR - 3 * 616
    g_win_0 = jnp.concatenate([g_win[:, 0:384], g_win[:, 448:480], g_win[:, 512:first]], axis=1)
    g_win_b = jnp.stack([g_win_0] + [g_win[:, first + 616 * jb:first + 616 * (jb + 1)] for jb in range(3)])
    g_wuq_f = g_wuq.reshape(Q_LORA, HEADS, HP)[:, :, 0:NOPE + ROPE].reshape(Q_LORA, HEADS * (NOPE + ROPE))
    g_k = g_wkv[:, 0:HEADS * HP].reshape(KV_LORA, HEADS, HP)[:, :, 0:NOPE]
    g_v = g_wkv[:, HEADS * HP:].reshape(KV_LORA, HEADS, VDIM)
    g_wukv_f = jnp.concatenate([g_k, g_v], axis=2).reshape(KV_LORA, HEADS * (NOPE + VDIM))

    def by_chip(a):
        rows, cols = a.shape[0], a.shape[1] // 4
        return jnp.swapaxes(a.reshape(rows, 4, cols), 0, 1).reshape(4, 2, rows // 2, cols)

    gs = [g_win_b.reshape(4, 2, D_MODEL // 2, 616), by_chip(g_wuq_f), by_chip(g_wukv_f), g_wout.reshape(4, 2, 128, D_MODEL),
          g_wsp.reshape(4, 2, CHUNK, CHUNK)]
    r_in, r_uq, r_ukv, r_out, r_wsp, r_vec = _grad_reduce(gs, gvec)

    g_big = [r_in.reshape(w_in.shape), r_uq.reshape(w_uq.shape), r_ukv.reshape(w_ukv.shape),
             r_out.reshape(w_out.shape), r_wsp.reshape(w_spatial.shape)]
    small = lambda qg, kvg, sg, sb, lng, lnb, bs: [qg.reshape(1, -1), kvg.reshape(1, -1), sg.reshape(1, -1),
                                                   sb.reshape(1, -1), lng.reshape(1, -1), lnb.reshape(1, -1), bs]
    res = _adamw(g_big, [w_in, w_uq, w_ukv, w_out, w_spatial], [m_w_in, m_w_uq, m_w_ukv, m_w_out, m_w_spatial],
                 [v_w_in, v_w_uq, v_w_ukv, v_w_out, v_w_spatial], r_vec,
                 small(q_norm_g, kv_norm_g, sgu_norm_g, sgu_norm_b, ln_g, ln_b, b_spatial),
                 small(m_q_norm_g, m_kv_norm_g, m_sgu_norm_g, m_sgu_norm_b, m_ln_g, m_ln_b, m_b_spatial),
                 small(v_q_norm_g, v_kv_norm_g, v_sgu_norm_g, v_sgu_norm_b, v_ln_g, v_ln_b, v_b_spatial))

    def ordered(big, sm):
        vec = lambda n: sm[n].reshape(-1)
        return [big[0], vec(0), big[1], vec(1), big[2], vec(2), vec(3), big[4], sm[6], big[3], vec(4), vec(5)]

    loss = r_vec[GV_LOSS, 0]
    return (loss, gx.reshape(1, seq, D_MODEL), *ordered(g_big, res[15:22]), *ordered(res[0:5], res[22:29]),
            *ordered(res[5:10], res[29:36]), *ordered(res[10:15], res[36:43]))
```

```python
import math

import jax
import jax.numpy as jnp
import numpy as np
from jax import lax
from jax.experimental import pallas as pl
from jax.experimental.pallas import tpu as pltpu

F32 = jnp.float32
BF16 = jnp.bfloat16

D_MODEL = 1024
Q_LORA = 256
KV_LORA = 128
HEADS = 8
NOPE = 64
ROPE = 32
VDIM = 64
MLA_W = HEADS * VDIM
GW = 512
CHUNK = 128
HP = 128
PAIRS = HEADS // 2
D_IN = 2464
D_INR = 2560
ROPE_THETA = 10000.0
DN_ALPHA = 2.0 ** 0.25
EPS = 1e-5
SCALE = 1.0 / math.sqrt(NOPE + ROPE)
SCALE_LOG2E = SCALE * 1.4426950408889634
INV_SQRT2 = 0.7071067811865476
INV_SQRT_2PI = 0.3989422804014327

ADAM_LR = 0.001
ADAM_B1 = 0.9
ADAM_B2 = 0.999
ADAM_EPS = 1e-08
ADAM_WD = 0.01
ADAM_STEP = 10

PV_QG, PV_KVG, PV_SG, PV_SB, PV_INVF, PV_M1, PV_M2, PV_LNG, PV_LNB = range(9)
PV_ROWS = 16
GV_QG, GV_KVG, GV_SG, GV_SB, GV_LNG, GV_LNB, GV_LOSS = range(7)
GV_BSP = 8
GV_ROWS = 16

MESH = pl.DeviceIdType.MESH

FWD_TILE = 512
POST_TILE = 512
BWD_TILE = 512
ATT_BLK = 512
ADAM_STEPS = 4
VMEM_LIMIT = 56 * 1024 * 1024


def _dot(a, b):
    return jnp.dot(a, b, preferred_element_type=F32)


def _dot_nt(a, b):
    return lax.dot_general(a, b, (((1,), (1,)), ((), ())), preferred_element_type=F32)


def _dot_tn(a, b):
    return lax.dot_general(a, b, (((0,), (0,)), ((), ())), preferred_element_type=F32)


def _sigmoid(z):
    return pl.reciprocal(1.0 + jnp.exp(-z), approx=True)


def _gelu_and_grad(x):
    cdf = 0.5 * (1.0 + lax.erf(x * INV_SQRT2))
    return x * cdf, cdf + x * (INV_SQRT_2PI * jnp.exp(-0.5 * x * x))


def _rms_stats(x):
    r = lax.rsqrt(jnp.mean(x * x, axis=-1, keepdims=True) + EPS)
    return x * r, r


def _rms_bwd(dy, g, xh, r):
    dyg = dy * g
    return r * (dyg - xh * jnp.mean(dyg * xh, axis=-1, keepdims=True))


def _ln_stats(x):
    mu = jnp.mean(x, axis=-1, keepdims=True)
    xc = x - mu
    r = lax.rsqrt(jnp.mean(xc * xc, axis=-1, keepdims=True) + EPS)
    return xc * r, r


def _ln_bwd(dy, g, xh, r):
    dxh = dy * g
    return r * (dxh - jnp.mean(dxh, axis=-1, keepdims=True) - xh * jnp.mean(dxh * xh, axis=-1, keepdims=True))


def _rope_fwd(t, c, s1, s2):
    return t * c + pltpu.roll(t, HP - 16, 1) * s1 + pltpu.roll(t, 16, 1) * s2


def _rope_bwd(d, c, s1, s2):
    return d * c + pltpu.roll(d * s1, 16, 1) + pltpu.roll(d * s2, HP - 16, 1)


def _lane_lt64(shape):
    return lax.broadcasted_iota(jnp.int32, shape, len(shape) - 1) < 64


def _spatial_mix(w_ref, src, dst_ref, rows):
    for c in range(rows // CHUNK):
        for p in range(PAIRS):
            blk = src[c * CHUNK:(c + 1) * CHUNK, p * HP:(p + 1) * HP]
            a = _dot(w_ref[2 * p], blk)
            b = _dot(w_ref[2 * p + 1], blk)
            dst_ref[c * CHUNK:(c + 1) * CHUNK, p * HP:(p + 1) * HP] = jnp.where(_lane_lt64(a.shape), a, b)


def _gmlp_fwd(u_pre, v_pre, zb, sg, sb, wt_ref, bsp_ref, sv_ref, rows):
    u, du = _gelu_and_grad(u_pre)
    gv, dgv = _gelu_and_grad(v_pre)
    xh, r = _ln_stats(gv)
    vln = (xh * sg + sb).astype(BF16)
    _spatial_mix(wt_ref, vln, sv_ref, rows)
    bias = bsp_ref[...]
    svb = sv_ref[...] + jnp.concatenate([bias] * (rows // CHUNK), axis=0)
    sig = _sigmoid(zb)
    return u, du, dgv, xh, r, vln, svb, sig


def _weight_gather(shards):
    n_arr = len(shards)

    def body(*refs):
        ins, outs = refs[0:n_arr], refs[n_arr:2 * n_arr]
        send_sems, recv_sems = refs[2 * n_arr:]
        x, y, c = lax.axis_index("x"), lax.axis_index("y"), lax.axis_index("c")
        j = 2 * x + y
        sib = (x, y, 1 - c)
        chips = [(1 - x, y), (x, 1 - y), (1 - x, 1 - y)]
        for n in range(n_arr):
            outs[n][j] = ins[n][...].astype(BF16)

        def half(n, blk, core):
            r = shards[n].shape[0] // 2
            return outs[n].at[blk, pl.ds(pl.multiple_of(core * r, 16), r), :]

        def copy(k, ref, to):
            return pltpu.make_async_remote_copy(
                src_ref=ref, dst_ref=ref, send_sem=send_sems.at[k], recv_sem=recv_sems.at[k],
                device_id=to, device_id_type=MESH)

        first = [copy(6 * n + kk, half(n, j, c), (px, py, c))
                 for n in range(n_arr) for kk, (px, py) in enumerate(chips)]
        for cp in first:
            cp.start()
        passed = []
        for n in range(n_arr):
            for kk, (px, py) in enumerate(chips):
                landed = half(n, 2 * px + py, c)
                copy(6 * n + kk, landed, (px, py, c)).wait_recv()
                passed.append(copy(6 * n + 3 + kk, landed, sib))
                passed[-1].start()
        for n in range(n_arr):
            for kk, (px, py) in enumerate(chips):
                copy(6 * n + 3 + kk, half(n, 2 * px + py, 1 - c), sib).wait_recv()
        for cp in first + passed:
            cp.wait_send()

    vmem = pl.BlockSpec(memory_space=pltpu.VMEM)
    return pl.pallas_call(
        body, name="weight_gather",
        out_shape=[jax.ShapeDtypeStruct((4,) + a.shape, BF16) for a in shards],
        in_specs=[vmem] * n_arr, out_specs=[vmem] * n_arr,
        scratch_shapes=[pltpu.SemaphoreType.DMA((6 * n_arr,)), pltpu.SemaphoreType.DMA((6 * n_arr,))],
        compiler_params=pltpu.CompilerParams(vmem_limit_bytes=VMEM_LIMIT),
    )(*shards)


def _fwd_pre(x, pos, win, wuq, wkv, pvec):
    seq = x.shape[0]
    t = FWD_TILE

    def body(x_ref, pos_ref, win_ref, wuq_ref, wkv_ref, pv_ref,
             cq_o, ckv_o, gate_o, q_o, k_o, v_o, vt_o, cs_o):
        proj = _dot(x_ref[...].astype(BF16), win_ref[...])
        cq = proj[:, 0:256]
        ckv = proj[:, 256:384]
        kr = proj[:, 384:512]
        cq_o[...] = cq
        ckv_o[...] = ckv
        gate_o[...] = proj[:, 512:D_INR].astype(BF16)

        ang = pos_ref[...].astype(F32) * pv_ref[PV_INVF:PV_INVF + 1, 0:HP]
        cos = jnp.cos(ang)
        sin = jnp.sin(ang)
        cs_o[:, 0:HP] = cos
        cs_o[:, HP:2 * HP] = sin
        s1 = sin * pv_ref[PV_M1:PV_M1 + 1, 0:HP]
        s2 = sin * pv_ref[PV_M2:PV_M2 + 1, 0:HP]

        cqh, _ = _rms_stats(cq)
        q_all = _dot((cqh * pv_ref[PV_QG:PV_QG + 1, 0:Q_LORA]).astype(BF16), wuq_ref[...])
        ckvh, _ = _rms_stats(ckv)
        kv_all = _dot((ckvh * pv_ref[PV_KVG:PV_KVG + 1, 0:KV_LORA]).astype(BF16), wkv_ref[...])
        krr = _rope_fwd(kr, cos, s1, s2)
        for h in range(HEADS):
            sl = slice(h * HP, (h + 1) * HP)
            q_o[:, sl] = (_rope_fwd(q_all[:, sl], cos, s1, s2) * SCALE_LOG2E).astype(BF16)
            k_o[:, sl] = (kv_all[:, sl] + krr).astype(BF16)
        val = kv_all[:, HEADS * HP:].astype(BF16)
        v_o[...] = val
        vt_o[...] = val.T

    tile = lambda w: pl.BlockSpec((t, w), lambda i: (i, 0))
    full = lambda a: pl.BlockSpec(a.shape, lambda i: (0,) * a.ndim)
    outs = [(Q_LORA, F32), (KV_LORA, F32), (2048, BF16), (HEADS * HP, BF16), (HEADS * HP, BF16), (MLA_W, BF16)]
    per_blk = ATT_BLK // t
    out_specs = [tile(w) for w, _ in outs]
    out_specs += [pl.BlockSpec((None, MLA_W, t), lambda i: (i // per_blk, 0, i % per_blk)), tile(2 * HP)]
    out_shape = [jax.ShapeDtypeStruct((seq, w), d) for w, d in outs]
    out_shape += [jax.ShapeDtypeStruct((seq // ATT_BLK, MLA_W, ATT_BLK), BF16), jax.ShapeDtypeStruct((seq, 2 * HP), F32)]
    return pl.pallas_call(
        body, name="fwd_pre", grid=(seq // t,),
        in_specs=[tile(D_MODEL), tile(1), full(win), full(wuq), full(wkv), full(pvec)],
        out_specs=out_specs, out_shape=out_shape,
        compiler_params=pltpu.CompilerParams(dimension_semantics=("arbitrary",), vmem_limit_bytes=VMEM_LIMIT),
    )(x, pos, win, wuq, wkv, pvec)


def _attn_fwd(q, k, vt):
    seq = q.shape[0]
    b = ATT_BLK
    nq = seq // b
    assert nq % 2 == 0
    n_off = nq * (nq - 1) // 2

    def body(q_ref, k_ref, vt_ref, o_o, lse_o, m_ref, l_ref, acc_ref, s_even, s_odd):
        m_ref[...] = jnp.full(m_ref.shape, -jnp.inf, F32)
        l_ref[...] = jnp.zeros(l_ref.shape, F32)
        acc_ref[...] = jnp.zeros(acc_ref.shape, F32)

        def scores(i, j, s_ref):
            qrows = pl.ds(pl.multiple_of(i * b, b), b)
            krows = pl.ds(pl.multiple_of(j * b, b), b)
            for a in range(2):
                s_ref[a] = _dot_nt(k_ref[krows, a * HP:(a + 1) * HP], q_ref[qrows, a * HP:(a + 1) * HP])

        def consume(i, j, s_ref, masked):
            vt_blk = vt_ref[j]
            for a in range(2):
                st = s_ref[a]
                if masked:
                    ki = lax.broadcasted_iota(jnp.int32, st.shape, 0)
                    qi = lax.broadcasted_iota(jnp.int32, st.shape, 1)
                    st = jnp.where(ki <= qi, st, -jnp.inf)
                m_prev = m_ref[i, a:a + 1, :]
                m_new = jnp.maximum(m_prev, jnp.max(st, axis=0, keepdims=True))
                alpha = jnp.exp2(m_prev - m_new)
                pt = jnp.exp2(st - m_new)
                l_ref[i, a:a + 1, :] = alpha * l_ref[i, a:a + 1, :] + jnp.sum(pt, axis=0, keepdims=True)
                acc_ref[i, a] = alpha * acc_ref[i, a] + _dot(vt_blk, pt.astype(BF16))
                m_ref[i, a:a + 1, :] = m_new

        def after(i, j):
            wrap = j + 1 >= i
            return jnp.minimum(jnp.where(wrap, i + 1, i), nq - 1), jnp.where(wrap, 0, j + 1)

        if n_off > 0:
            scores(1, 0, s_even)

            def below(u, ij):
                i1, j1 = after(*ij)
                scores(i1, j1, s_odd)
                consume(ij[0], ij[1], s_even, False)
                i2, j2 = after(i1, j1)
                scores(i2, j2, s_even)
                consume(i1, j1, s_odd, False)
                return i2, j2

            last = lax.fori_loop(0, n_off // 2, below, (jnp.int32(1), jnp.int32(0)))
            if n_off % 2:
                consume(last[0], last[1], s_even, False)

        scores(0, 0, s_even)

        def diagonal(u, carry):
            i0 = 2 * u
            scores(i0 + 1, i0 + 1, s_odd)
            consume(i0, i0, s_even, True)
            i2 = jnp.minimum(i0 + 2, nq - 1)
            scores(i2, i2, s_even)
            consume(i0 + 1, i0 + 1, s_odd, True)
            return carry

        lax.fori_loop(0, nq // 2, diagonal, 0)
        top = lax.broadcasted_iota(jnp.int32, (HP, b), 0) < 64

        def finish(i, carry):
            rows = pl.ds(pl.multiple_of(i * b, b), b)
            o_o[rows, :] = jnp.where(top, acc_ref[i, 0] / l_ref[i, 0:1, :], acc_ref[i, 1] / l_ref[i, 1:2, :]).T
            lse_o[i] = m_ref[i, 0:2, :] + jnp.log2(l_ref[i, 0:2, :])
            return carry

        lax.fori_loop(0, nq, finish, 0)

    return pl.pallas_call(
        body, name="attn_fwd", grid=(PAIRS,),
        in_specs=[pl.BlockSpec((seq, 2 * HP), lambda p: (0, p)),
                  pl.BlockSpec((seq, 2 * HP), lambda p: (0, p)),
                  pl.BlockSpec((nq, HP, b), lambda p: (0, p, 0))],
        out_specs=[pl.BlockSpec((seq, HP), lambda p: (0, p)),
                   pl.BlockSpec((None, nq, 2, b), lambda p: (p, 0, 0, 0))],
        out_shape=[jax.ShapeDtypeStruct((seq, MLA_W), F32),
                   jax.ShapeDtypeStruct((PAIRS, nq, 2, b), F32)],
        scratch_shapes=[pltpu.VMEM((nq, 8, b), F32), pltpu.VMEM((nq, 8, b), F32), pltpu.VMEM((nq, 2, HP, b), F32),
                        pltpu.VMEM((2, b, b), F32), pltpu.VMEM((2, b, b), F32)],
        compiler_params=pltpu.CompilerParams(dimension_semantics=("arbitrary",), vmem_limit_bytes=VMEM_LIMIT),
    )(q, k, vt)


def _post(x, tgt, o, gate, wout, pvec, wt, wtt, bsp):
    seq = x.shape[0]
    t = POST_TILE
    nt = seq // t

    def body(x_ref, tgt_ref, o_ref, gate_ref, wout_ref, pv_ref, wt_ref, wtt_ref, bsp_ref,
             dh2_o, do_o, dgate_o, gwout_o, gwsp_o, vec_o, sv_ref, dvln_ref, bacc_ref):
        i = pl.program_id(0)

        @pl.when(i == 0)
        def _():
            gwout_o[...] = jnp.zeros_like(gwout_o)
            gwsp_o[...] = jnp.zeros_like(gwsp_o)
            vec_o[...] = jnp.zeros_like(vec_o)
            bacc_ref[...] = jnp.zeros_like(bacc_ref)

        za = gate_ref[:, 0:512].astype(F32)
        u_pre = gate_ref[:, 512:1024].astype(F32)
        v_pre = gate_ref[:, 1024:1536].astype(F32)
        zb = gate_ref[:, 1536:2048].astype(F32)
        sg = pv_ref[PV_SG:PV_SG + 1, 0:GW]
        sb = pv_ref[PV_SB:PV_SB + 1, 0:GW]
        lng = pv_ref[PV_LNG:PV_LNG + 1, :]
        lnb = pv_ref[PV_LNB:PV_LNB + 1, :]
        o = o_ref[...]

        sig_a = _sigmoid(za)
        silu_a = za * sig_a
        u, du, dgv, xh, r, vln, svb, sig_b = _gmlp_fwd(u_pre, v_pre, zb, sg, sb, wt_ref, bsp_ref, sv_ref, t)
        silu_b = zb * sig_b
        sgu = u * svb
        merged = jnp.concatenate([o * silu_a, sgu * silu_b], axis=1).astype(BF16)
        h2 = DN_ALPHA * x_ref[...] + _dot(merged, wout_ref[...])
        xh2, r2 = _ln_stats(h2)
        err = xh2 * lng + lnb - tgt_ref[...]
        d_out = err * (1.0 / D_MODEL)
        vec_o[GV_LNG:GV_LNG + 1, :] += jnp.sum(d_out * xh2, axis=0, keepdims=True)
        vec_o[GV_LNB:GV_LNB + 1, :] += jnp.sum(d_out, axis=0, keepdims=True)
        vec_o[GV_LOSS:GV_LOSS + 1, :] += jnp.sum(err * err, axis=0, keepdims=True) * (0.5 / D_MODEL)

        d_h2 = _ln_bwd(d_out, lng, xh2, r2)
        dh2_o[...] = d_h2
        dh2b = d_h2.astype(BF16)
        gwout_o[...] += _dot_tn(merged, dh2b)
        d_m = _dot_nt(dh2b, wout_ref[...])
        d_oa = d_m[:, 0:512]
        d_ob = d_m[:, 512:1024]
        do_o[...] = (d_oa * silu_a).astype(BF16)
        dgate_o[:, 0:512] = (d_oa * o * (sig_a * (1.0 + za * (1.0 - sig_a)))).astype(BF16)
        dgate_o[:, 1536:2048] = (d_ob * sgu * (sig_b * (1.0 + zb * (1.0 - sig_b)))).astype(BF16)
        d_sgu = d_ob * silu_b
        dgate_o[:, 512:1024] = (d_sgu * svb * du).astype(BF16)
        d_sv = d_sgu * u
        acc = bacc_ref[...]
        for c in range(t // CHUNK):
            acc = acc + d_sv[c * CHUNK:(c + 1) * CHUNK, :]
        bacc_ref[...] = acc
        d_svb = d_sv.astype(BF16)
        for c in range(t // CHUNK):
            for p in range(PAIRS):
                blk = d_svb[c * CHUNK:(c + 1) * CHUNK, p * HP:(p + 1) * HP]
                vblk = vln[c * CHUNK:(c + 1) * CHUNK, p * HP:(p + 1) * HP]
                first = _lane_lt64(blk.shape)
                gwsp_o[2 * p] += _dot_nt(jnp.where(first, blk, jnp.zeros_like(blk)), vblk)
                gwsp_o[2 * p + 1] += _dot_nt(jnp.where(first, jnp.zeros_like(blk), blk), vblk)
        _spatial_mix(wtt_ref, d_svb, dvln_ref, t)
        d_vln = dvln_ref[...]
        vec_o[GV_SG:GV_SG + 1, 0:GW] += jnp.sum(d_vln * xh, axis=0, keepdims=True)
        vec_o[GV_SB:GV_SB + 1, 0:GW] += jnp.sum(d_vln, axis=0, keepdims=True)
        dgate_o[:, 1024:1536] = (_ln_bwd(d_vln, sg, xh, r) * dgv).astype(BF16)

        @pl.when(i == nt - 1)
        def _():
            tri = (lax.broadcasted_iota(jnp.int32, (CHUNK, CHUNK), 1)
                   <= lax.broadcasted_iota(jnp.int32, (CHUNK, CHUNK), 0))
            for h in range(HEADS):
                gwsp_o[h] = jnp.where(tri, gwsp_o[h], 0.0)
            lane = lax.broadcasted_iota(jnp.int32, (CHUNK, HP), 1)
            res = jnp.zeros((CHUNK, HP), F32)
            for h in range(HEADS):
                p, a = divmod(h, 2)
                blk = bacc_ref[:, p * HP:(p + 1) * HP]
                part = jnp.where(_lane_lt64(blk.shape) == (a == 0), blk, 0.0)
                res = jnp.where(lane == h, jnp.sum(part, axis=-1, keepdims=True), res)
            vec_o[GV_BSP:GV_BSP + HEADS, 0:HP] = res.T[0:HEADS, :]
            lane1 = lax.broadcasted_iota(jnp.int32, (1, D_MODEL), 1)
            total = jnp.sum(vec_o[GV_LOSS:GV_LOSS + 1, :], axis=-1, keepdims=True)
            vec_o[GV_LOSS:GV_LOSS + 1, :] = jnp.where(lane1 == 0, total, 0.0)

    tile = lambda w: pl.BlockSpec((t, w), lambda i: (i, 0))
    full = lambda a: pl.BlockSpec(a.shape, lambda i: (0,) * a.ndim)
    const = lambda s: pl.BlockSpec(s, lambda i: (0,) * len(s))
    return pl.pallas_call(
        body, name="post", grid=(nt,),
        in_specs=[tile(D_MODEL), tile(D_MODEL), tile(MLA_W), tile(2048), full(wout), full(pvec),
                  full(wt), full(wtt), full(bsp)],
        out_specs=[tile(D_MODEL), tile(MLA_W), tile(2048), const((D_MODEL, D_MODEL)),
                   const((HEADS, CHUNK, CHUNK)), const((GV_ROWS, D_MODEL))],
        out_shape=[jax.ShapeDtypeStruct((seq, D_MODEL), F32), jax.ShapeDtypeStruct((seq, MLA_W), BF16),
                   jax.ShapeDtypeStruct((seq, 2048), BF16), jax.ShapeDtypeStruct((D_MODEL, D_MODEL), F32),
                   jax.ShapeDtypeStruct((HEADS, CHUNK, CHUNK), F32), jax.ShapeDtypeStruct((GV_ROWS, D_MODEL), F32)],
        scratch_shapes=[pltpu.VMEM((t, GW), F32), pltpu.VMEM((t, GW), F32), pltpu.VMEM((CHUNK, GW), F32)],
        compiler_params=pltpu.CompilerParams(dimension_semantics=("arbitrary",), vmem_limit_bytes=VMEM_LIMIT),
    )(x, tgt, o, gate, wout, pvec, wt, wtt, bsp)


def _attn_bwd(q, k, v, do, o, lse, cs, pvec):
    seq = q.shape[0]
    b = ATT_BLK
    hb = b // 2
    nq = seq // b

    def body(q_ref, k_ref, v_ref, do_ref, o_ref, lse_ref, cs_ref, pv_ref, dq_o, dk_o, dv_o, dk_acc, dv_acc):
        i = pl.program_id(1)

        @pl.when(i == 0)
        def _():
            dk_acc[...] = jnp.zeros_like(dk_acc)
            dv_acc[...] = jnp.zeros_like(dv_acc)

        first = _lane_lt64((b, HP))
        do = do_ref[...]
        zero = jnp.zeros_like(do)
        dos = [jnp.where(first, do, zero), jnp.where(first, zero, do)]
        prod_t = (do.astype(F32) * o_ref[...]).T
        deltas = [jnp.sum(prod_t[0:64, :], axis=0, keepdims=True),
                  jnp.sum(prod_t[64:128, :], axis=0, keepdims=True)]
        lses = [lse_ref[0:1, :], lse_ref[1:2, :]]
        qs = [q_ref[:, a * HP:(a + 1) * HP] for a in range(2)]

        def step(j, dqs, k0=0, nk=b, q0=0, masked=False):
            rows = pl.ds(pl.multiple_of(j * b + k0, hb), nk)
            vb = v_ref[rows, :]
            new_dq = []
            dvs = []
            for a in range(2):
                kb = k_ref[rows, a * HP:(a + 1) * HP]
                pt = jnp.exp2(_dot_nt(kb, qs[a][q0:, :]) - lses[a][:, q0:])
                if masked:
                    ki = lax.broadcasted_iota(jnp.int32, pt.shape, 0) + k0
                    qi = lax.broadcasted_iota(jnp.int32, pt.shape, 1) + q0
                    pt = jnp.where(ki <= qi, pt, 0.0)
                dvs.append(_dot(pt.astype(BF16), do[q0:, :]))
                dpt = _dot_nt(vb, dos[a][q0:, :])
                dst = (pt * (dpt - deltas[a][:, q0:])).astype(BF16)
                dk_acc[rows, a * HP:(a + 1) * HP] += _dot(dst, qs[a][q0:, :])
                part = _dot_tn(dst, kb)
                if q0:
                    part = jnp.concatenate([jnp.zeros((q0, HP), F32), part], axis=0)
                new_dq.append(dqs[a] + part)
            dv_acc[rows, :] += jnp.where(first[0:nk, :], dvs[0], dvs[1])
            return tuple(new_dq)

        init = (jnp.zeros((b, HP), F32), jnp.zeros((b, HP), F32))
        dqs = lax.fori_loop(0, i, lambda j, cr: step(j, cr), init)
        dqs = step(i, dqs, k0=0, nk=hb, q0=0, masked=True)
        dqs = step(i, dqs, k0=hb, nk=hb, q0=hb, masked=True)
        cos = cs_ref[:, 0:HP]
        sin = cs_ref[:, HP:2 * HP]
        s1 = sin * pv_ref[PV_M1:PV_M1 + 1, 0:HP]
        s2 = sin * pv_ref[PV_M2:PV_M2 + 1, 0:HP]
        for a in range(2):
            dq_o[:, a * HP:(a + 1) * HP] = _rope_bwd(dqs[a] * SCALE, cos, s1, s2).astype(BF16)

        @pl.when(i == nq - 1)
        def _():
            dk_o[...] = (dk_acc[...] * (SCALE / SCALE_LOG2E)).astype(BF16)
            dv_o[...] = dv_acc[...].astype(BF16)

    return pl.pallas_call(
        body, name="attn_bwd", grid=(PAIRS, nq),
        in_specs=[pl.BlockSpec((b, 2 * HP), lambda p, i: (i, p)),
                  pl.BlockSpec((seq, 2 * HP), lambda p, i: (0, p)),
                  pl.BlockSpec((seq, HP), lambda p, i: (0, p)),
                  pl.BlockSpec((b, HP), lambda p, i: (i, p)),
                  pl.BlockSpec((b, HP), lambda p, i: (i, p)),
                  pl.BlockSpec((None, None, 2, b), lambda p, i: (p, i, 0, 0)),
                  pl.BlockSpec((b, 2 * HP), lambda p, i: (i, 0)),
                  pl.BlockSpec(pvec.shape, lambda p, i: (0, 0))],
        out_specs=[pl.BlockSpec((b, 2 * HP), lambda p, i: (i, p)),
                   pl.BlockSpec((seq, 2 * HP), lambda p, i: (0, p)),
                   pl.BlockSpec((seq, HP), lambda p, i: (0, p))],
        out_shape=[jax.ShapeDtypeStruct((seq, HEADS * HP), BF16),
                   jax.ShapeDtypeStruct((seq, HEADS * HP), BF16),
                   jax.ShapeDtypeStruct((seq, MLA_W), BF16)],
        scratch_shapes=[pltpu.VMEM((seq, 2 * HP), F32), pltpu.VMEM((seq, HP), F32)],
        compiler_params=pltpu.CompilerParams(dimension_semantics=("arbitrary", "arbitrary"),
                                             vmem_limit_bytes=VMEM_LIMIT),
    )(q, k, v, do, o, lse, cs, pvec)


def _bwd_pre(x, dh2, cq, ckv, cs, dq, dk, dv, dgate, win, wuq, wkv, pvec, gvec):
    seq = x.shape[0]
    t = BWD_TILE

    def body(x_ref, dh2_ref, cq_ref, ckv_ref, cs_ref, dq_ref, dk_ref, dv_ref, dgate_ref,
             win_ref, wuq_ref, wkv_ref, pv_ref, gv_ref, gx_o, gwin_o, gwuq_o, gwkv_o, vec_o):
        i = pl.program_id(0)

        @pl.when(i == 0)
        def _():
            gwin_o[...] = jnp.zeros_like(gwin_o)
            gwuq_o[...] = jnp.zeros_like(gwuq_o)
            gwkv_o[...] = jnp.zeros_like(gwkv_o)
            vec_o[...] = gv_ref[...]

        qg = pv_ref[PV_QG:PV_QG + 1, 0:Q_LORA]
        kvg = pv_ref[PV_KVG:PV_KVG + 1, 0:KV_LORA]
        dq = dq_ref[...]
        cqh, rq = _rms_stats(cq_ref[...])
        d_cqn = _dot_nt(dq, wuq_ref[...])
        gwuq_o[...] += _dot_tn((cqh * qg).astype(BF16), dq)
        vec_o[GV_QG:GV_QG + 1, 0:Q_LORA] += jnp.sum(d_cqn * cqh, axis=0, keepdims=True)
        d_cq = _rms_bwd(d_cqn, qg, cqh, rq)

        dk = dk_ref[...]
        dkv = jnp.concatenate([dk, dv_ref[...]], axis=1)
        ckvh, rkv = _rms_stats(ckv_ref[...])
        d_ckvn = _dot_nt(dkv, wkv_ref[...])
        gwkv_o[...] += _dot_tn((ckvh * kvg).astype(BF16), dkv)
        vec_o[GV_KVG:GV_KVG + 1, 0:KV_LORA] += jnp.sum(d_ckvn * ckvh, axis=0, keepdims=True)
        d_ckv = _rms_bwd(d_ckvn, kvg, ckvh, rkv)

        dks = dk[:, 0:HP].astype(F32)
        for h in range(1, HEADS):
            dks = dks + dk[:, h * HP:(h + 1) * HP].astype(F32)
        cos = cs_ref[:, 0:HP]
        sin = cs_ref[:, HP:2 * HP]
        d_kr = _rope_bwd(dks, cos, sin * pv_ref[PV_M1:PV_M1 + 1, 0:HP], sin * pv_ref[PV_M2:PV_M2 + 1, 0:HP])

        d_proj = jnp.concatenate([d_cq.astype(BF16), d_ckv.astype(BF16), d_kr.astype(BF16), dgate_ref[...]], axis=1)
        gwin_o[...] += _dot_tn(x_ref[...].astype(BF16), d_proj)
        gx_o[...] = DN_ALPHA * dh2_ref[...] + _dot_nt(d_proj, win_ref[...])

    tile = lambda w: pl.BlockSpec((t, w), lambda i: (i, 0))
    full = lambda a: pl.BlockSpec(a.shape, lambda i: (0,) * a.ndim)
    const = lambda s: pl.BlockSpec(s, lambda i: (0,) * len(s))
    return pl.pallas_call(
        body, name="bwd_pre", grid=(seq // t,),
        in_specs=[tile(D_MODEL), tile(D_MODEL), tile(Q_LORA), tile(KV_LORA), tile(2 * HP), tile(HEADS * HP),
                  tile(HEADS * HP), tile(MLA_W), tile(2048), full(win), full(wuq), full(wkv), full(pvec), full(gvec)],
        out_specs=[tile(D_MODEL), const((D_MODEL, D_INR)), const((Q_LORA, HEADS * HP)),
                   const((KV_LORA, HEADS * HP + MLA_W)), const((GV_ROWS, D_MODEL))],
        out_shape=[jax.ShapeDtypeStruct((seq, D_MODEL), F32), jax.ShapeDtypeStruct((D_MODEL, D_INR), F32),
                   jax.ShapeDtypeStruct((Q_LORA, HEADS * HP), F32),
                   jax.ShapeDtypeStruct((KV_LORA, HEADS * HP + MLA_W), F32),
                   jax.ShapeDtypeStruct((GV_ROWS, D_MODEL), F32)],
        compiler_params=pltpu.CompilerParams(dimension_semantics=("arbitrary",), vmem_limit_bytes=VMEM_LIMIT),
    )(x, dh2, cq, ckv, cs, dq, dk, dv, dgate, win, wuq, wkv, pvec, gvec)


def _grad_reduce(gs, gvec):
    n_arr = len(gs)
    n_big = n_arr - 1
    k1 = lambda n, blk: 4 * n + blk
    k2 = lambda n, kk: 4 * n_arr + 3 * n + kk
    k3 = lambda n: 7 * n_arr + n
    k3w = lambda k: 7 * n_arr + n_big + k
    kv = lambda k: 7 * n_arr + n_big + 7 + k
    n_sem = 7 * n_arr + n_big + 14

    def body(*refs):
        g, gv = refs[0:n_arr], refs[n_arr]
        outs, ov = refs[n_arr + 1:2 * n_arr + 1], refs[2 * n_arr + 1]
        r1 = refs[2 * n_arr + 2:3 * n_arr + 2]
        r2 = refs[3 * n_arr + 2:4 * n_arr + 2]
        s2 = refs[4 * n_arr + 2:5 * n_arr + 2]
        vbuf, send_sems, recv_sems = refs[5 * n_arr + 2:]
        x, y, c = lax.axis_index("x"), lax.axis_index("y"), lax.axis_index("c")
        j = 2 * x + y
        me = 2 * j + c
        sib = (x, y, 1 - c)
        chips = [(1 - x, y), (x, 1 - y), (1 - x, 1 - y)]
        others = [sib] + [(px, py, pc) for (px, py) in chips for pc in (c, 1 - c)]

        def copy(k, src, dst, to):
            return pltpu.make_async_remote_copy(
                src_ref=src, dst_ref=dst, send_sem=send_sems.at[k], recv_sem=recv_sems.at[k],
                device_id=to, device_id_type=MESH)

        l1 = [copy(k1(n, blk), g[n].at[blk, 1 - c], r1[n].at[blk], sib) for n in range(n_arr) for blk in range(4)]
        lv = [copy(kv(k), gv, vbuf.at[me], to) for k, to in enumerate(others)]
        for cp in l1 + lv:
            cp.start()
        for n in range(n_arr):
            for blk in range(4):
                copy(k1(n, blk), g[n].at[blk, c], r1[n].at[blk], sib).wait_recv()
        for n in range(n_arr):
            for blk in range(4):
                r1[n][blk] = g[n][blk, c] + r1[n][blk]
                s2[n][blk] = r1[n][blk].astype(BF16)

        l2 = [copy(k2(n, kk), s2[n].at[2 * px + py], r2[n].at[kk], (px, py, c))
              for n in range(n_arr) for kk, (px, py) in enumerate(chips)]
        for cp in l2:
            cp.start()
        for n in range(n_arr):
            for kk in range(3):
                copy(k2(n, kk), s2[n].at[0], r2[n].at[kk], sib).wait_recv()
        for n in range(n_arr):
            red = ((r1[n][j] + r2[n][0].astype(F32)) + r2[n][1].astype(F32)) + r2[n][2].astype(F32)
            if n < n_big:
                outs[n][c] = red
            else:
                outs[n][j, c] = red

        l3 = [copy(k3(n), outs[n].at[c], outs[n].at[c], sib) for n in range(n_big)]
        l3 += [copy(k3w(k), outs[n_big].at[j, c], outs[n_big].at[j, c], to) for k, to in enumerate(others)]
        for cp in l3:
            cp.start()
        for n in range(n_big):
            copy(k3(n), outs[n].at[1 - c], outs[n].at[1 - c], sib).wait_recv()
        for k, (px, py, pc) in enumerate(others):
            landed = outs[n_big].at[2 * px + py, pc]
            copy(k3w(k), landed, landed, (px, py, pc)).wait_recv()
            copy(kv(k), gv, vbuf.at[4 * px + 2 * py + pc], (px, py, pc)).wait_recv()
        vbuf[me] = gv[...]
        total = vbuf[0]
        for d in range(1, 8):
            total = total + vbuf[d]
        ov[...] = total
        for cp in l1 + lv + l2 + l3:
            cp.wait_send()

    vmem = pl.BlockSpec(memory_space=pltpu.VMEM)
    half_shapes = [a.shape[2:] for a in gs]
    out_shape = [jax.ShapeDtypeStruct((2,) + s, F32) for s in half_shapes[:n_big]]
    out_shape += [jax.ShapeDtypeStruct((4, 2) + half_shapes[n_big], F32), jax.ShapeDtypeStruct(gvec.shape, F32)]
    scratch = [pltpu.VMEM((4,) + s, F32) for s in half_shapes] + [pltpu.VMEM((3,) + s, BF16) for s in half_shapes]
    scratch += [pltpu.VMEM((4,) + s, BF16) for s in half_shapes]
    scratch += [pltpu.VMEM((8,) + gvec.shape, F32), pltpu.SemaphoreType.DMA((n_sem,)), pltpu.SemaphoreType.DMA((n_sem,))]
    return pl.pallas_call(
        body, name="grad_reduce", out_shape=out_shape,
        in_specs=[vmem] * (n_arr + 1), out_specs=[vmem] * (n_arr + 1), scratch_shapes=scratch,
        compiler_params=pltpu.CompilerParams(vmem_limit_bytes=VMEM_LIMIT),
    )(*gs, gvec)


SMALL_ROWS = ((GV_QG, 1, Q_LORA), (GV_KVG, 1, KV_LORA), (GV_SG, 1, GW), (GV_SB, 1, GW),
              (GV_LNG, 1, D_MODEL), (GV_LNB, 1, D_MODEL), (GV_BSP, HEADS, CHUNK))


def _adam_update(g, w, m, v):
    m_new = ADAM_B1 * m + (1.0 - ADAM_B1) * g
    v_new = ADAM_B2 * v + (1.0 - ADAM_B2) * (g * g)
    m_hat = m_new / (1.0 - ADAM_B1 ** ADAM_STEP)
    v_hat = v_new / (1.0 - ADAM_B2 ** ADAM_STEP)
    return -ADAM_LR * (m_hat / (jnp.sqrt(v_hat) + ADAM_EPS) + ADAM_WD * w), m_new, v_new


def _adamw(g_big, w_big, m_big, v_big, gvec, w_small, m_small, v_small):
    nb, ns = len(g_big), len(w_small)

    def body(*refs):
        it = iter(refs)
        take = lambda n: [next(it) for _ in range(n)]
        g_b, w_b, m_b, v_b = take(nb), take(nb), take(nb), take(nb)
        gv = next(it)
        w_s, m_s, v_s = take(ns), take(ns), take(ns)
        d_bo, m_bo, v_bo = take(nb), take(nb), take(nb)
        g_so, d_so, m_so, v_so = take(ns), take(ns), take(ns), take(ns)
        for n in range(nb):
            d_bo[n][...], m_bo[n][...], v_bo[n][...] = _adam_update(g_b[n][...], w_b[n][...], m_b[n][...], v_b[n][...])
        for n, (row, nrow, width) in enumerate(SMALL_ROWS):
            gs = gv[row:row + nrow, 0:width]
            g_so[n][...] = gs
            d_so[n][...], m_so[n][...], v_so[n][...] = _adam_update(gs, w_s[n][...], m_s[n][...], v_s[n][...])

    def rows(a):
        nd = a.ndim
        return pl.BlockSpec((a.shape[0] // ADAM_STEPS,) + a.shape[1:], lambda i: (i,) + (0,) * (nd - 1))

    def whole(a):
        nd = a.ndim
        return pl.BlockSpec(a.shape, lambda i: (0,) * nd)

    big = [jax.ShapeDtypeStruct(a.shape, F32) for a in w_big]
    small = [jax.ShapeDtypeStruct(a.shape, F32) for a in w_small]
    return pl.pallas_call(
        body, name="adamw", grid=(ADAM_STEPS,), out_shape=big * 3 + small * 4,
        in_specs=[rows(a) for a in g_big + w_big + m_big + v_big] + [whole(gvec)]
        + [whole(a) for a in w_small + m_small + v_small],
        out_specs=[rows(a) for a in w_big] * 3 + [whole(a) for a in w_small] * 4,
        compiler_params=pltpu.CompilerParams(dimension_semantics=("arbitrary",), vmem_limit_bytes=VMEM_LIMIT),
    )(*g_big, *w_big, *m_big, *v_big, gvec, *w_small, *m_small, *v_small)


def kernel(x, positions, w_in, q_norm_g, w_uq, kv_norm_g, w_ukv, sgu_norm_g, sgu_norm_b, w_spatial, b_spatial, w_out, ln_g, ln_b, loss_target, m_w_in, m_q_norm_g, m_w_uq, m_kv_norm_g, m_w_ukv, m_sgu_norm_g, m_sgu_norm_b, m_w_spatial, m_b_spatial, m_w_out, m_ln_g, m_ln_b, v_w_in, v_q_norm_g, v_w_uq, v_kv_norm_g, v_w_ukv, v_sgu_norm_g, v_sgu_norm_b, v_w_spatial, v_b_spatial, v_w_out, v_ln_g, v_ln_b):
    seq = x.shape[1]
    x2 = x.reshape(seq, D_MODEL)
    tgt = loss_target.reshape(seq, D_MODEL)
    pos = positions.reshape(seq, 1)

    a_in, a_uq, a_ukv, a_out = _weight_gather([w_in, w_uq, w_ukv, w_out])
    w_uq_f = jnp.swapaxes(a_uq, 0, 1).reshape(Q_LORA, HEADS * (NOPE + ROPE))
    w_ukv_f = jnp.swapaxes(a_ukv, 0, 1).reshape(KV_LORA, HEADS * (NOPE + VDIM))
    wout = a_out.reshape(D_MODEL, D_MODEL)
    zc = lambda n: jnp.zeros((D_MODEL, n), BF16)
    win = jnp.concatenate([a_in[0][:, 0:384], zc(64), a_in[0][:, 384:416], zc(32), a_in[0][:, 416:],
                           a_in[1], a_in[2], a_in[3]], axis=1)
    wuq = jnp.pad(w_uq_f.reshape(Q_LORA, HEADS, NOPE + ROPE), ((0, 0), (0, 0), (0, HP - NOPE - ROPE)))
    wuq = wuq.reshape(Q_LORA, HEADS * HP)
    ukv = w_ukv_f.reshape(KV_LORA, HEADS, NOPE + VDIM)
    wk = jnp.pad(ukv[:, :, 0:NOPE], ((0, 0), (0, 0), (0, HP - NOPE))).reshape(KV_LORA, HEADS * HP)
    wkv = jnp.concatenate([wk, ukv[:, :, NOPE:].reshape(KV_LORA, MLA_W)], axis=1)

    lane = np.arange(HP)
    half = ROPE // 2
    inv_freq = (1.0 / (ROPE_THETA ** (np.arange(half, dtype=np.float32) / half))).astype(np.float32)
    in_rope = (lane >= NOPE) & (lane < NOPE + ROPE)
    invf = jnp.asarray(np.where(in_rope, inv_freq[(lane - NOPE) % half], 0.0).astype(np.float32))
    m1 = jnp.asarray(np.where((lane >= NOPE) & (lane < NOPE + half), -1.0, 0.0).astype(np.float32))
    m2 = jnp.asarray(np.where((lane >= NOPE + half) & (lane < NOPE + ROPE), 1.0, 0.0).astype(np.float32))
    row = lambda a: jnp.pad(a.astype(F32), (0, D_MODEL - a.shape[0]))
    pvec = jnp.stack([row(q_norm_g), row(kv_norm_g), row(sgu_norm_g), row(sgu_norm_b), row(invf), row(m1),
                      row(m2), row(ln_g), row(ln_b)] + [jnp.zeros((D_MODEL,), F32)] * (PV_ROWS - 9))
    tri = jnp.tril(jnp.ones((CHUNK, CHUNK), dtype=bool))
    wt = jnp.where(tri[None], w_spatial, 0.0).astype(BF16)
    wtt = jnp.swapaxes(wt, 1, 2)
    bsp = jnp.repeat(b_spatial.T, VDIM, axis=1)

    cq, ckv, gate, q, k, v, vt, cs = _fwd_pre(x2, pos, win, wuq, wkv, pvec)
    o, lse = _attn_fwd(q, k, vt)
    dh2, do, dgate, g_wout, g_wsp, gvec = _post(x2, tgt, o, gate, wout, pvec, wt, wtt, bsp)
    dq, dk, dv = _attn_bwd(q, k, v, do, o, lse, cs, pvec)
    gx, g_win, g_wuq, g_wkv, gvec = _bwd_pre(x2, dh2, cq, ckv, cs, dq, dk, dv, dgate, win, wuq, wkv, pvec, gvec)

    cw = w_in.shape[1]
    first = D_INR - 3 * cw
    g_win_0 = jnp.concatenate([g_win[:, 0:384], g_win[:, 448:480], g_win[:, 512:first]], axis=1)
    g_win_b = jnp.stack([g_win_0] + [g_win[:, first + cw * jb:first + cw * (jb + 1)] for jb in range(3)])
    g_wuq_f = g_wuq.reshape(Q_LORA, HEADS, HP)[:, :, 0:NOPE + ROPE].reshape(Q_LORA, HEADS * (NOPE + ROPE))
    g_k = g_wkv[:, 0:HEADS * HP].reshape(KV_LORA, HEADS, HP)[:, :, 0:NOPE]
    g_v = g_wkv[:, HEADS * HP:].reshape(KV_LORA, HEADS, VDIM)
    g_wukv_f = jnp.concatenate([g_k, g_v], axis=2).reshape(KV_LORA, HEADS * (NOPE + VDIM))

    def by_chip(a):
        rows, cols = a.shape[0], a.shape[1] // 4
        return jnp.swapaxes(a.reshape(rows, 4, cols), 0, 1).reshape(4, 2, rows // 2, cols)

    gs = [g_win_b.reshape(4, 2, D_MODEL // 2, cw), by_chip(g_wuq_f), by_chip(g_wukv_f), g_wout.reshape(4, 2, 128, D_MODEL),
          g_wsp.reshape(4, 2, CHUNK, CHUNK)]
    r_in, r_uq, r_ukv, r_out, r_wsp, r_vec = _grad_reduce(gs, gvec)

    g_big = [r_in.reshape(w_in.shape), r_uq.reshape(w_uq.shape), r_ukv.reshape(w_ukv.shape),
             r_out.reshape(w_out.shape), r_wsp.reshape(w_spatial.shape)]
    small = lambda qg, kvg, sg, sb, lng, lnb, bs: [qg.reshape(1, -1), kvg.reshape(1, -1), sg.reshape(1, -1),
                                                   sb.reshape(1, -1), lng.reshape(1, -1), lnb.reshape(1, -1), bs]
    res = _adamw(g_big, [w_in, w_uq, w_ukv, w_out, w_spatial], [m_w_in, m_w_uq, m_w_ukv, m_w_out, m_w_spatial],
                 [v_w_in, v_w_uq, v_w_ukv, v_w_out, v_w_spatial], r_vec,
                 small(q_norm_g, kv_norm_g, sgu_norm_g, sgu_norm_b, ln_g, ln_b, b_spatial),
                 small(m_q_norm_g, m_kv_norm_g, m_sgu_norm_g, m_sgu_norm_b, m_ln_g, m_ln_b, m_b_spatial),
                 small(v_q_norm_g, v_kv_norm_g, v_sgu_norm_g, v_sgu_norm_b, v_ln_g, v_ln_b, v_b_spatial))

    def ordered(big, sm):
        vec = lambda n: sm[n].reshape(-1)
        return [big[0], vec(0), big[1], vec(1), big[2], vec(2), vec(3), big[4], sm[6], big[3], vec(4), vec(5)]

    loss = r_vec[GV_LOSS, 0]
    return (loss, gx.reshape(1, seq, D_MODEL), *ordered(g_big, res[15:22]), *ordered(res[0:5], res[22:29]),
            *ordered(res[5:10], res[29:36]), *ordered(res[10:15], res[36:43]))
```

```python
import math

import jax
import jax.numpy as jnp
import numpy as np
from jax import lax
from jax.experimental import pallas as pl
from jax.experimental.pallas import tpu as pltpu

F32 = jnp.float32
BF16 = jnp.bfloat16

D_MODEL = 1024
Q_LORA = 256
KV_LORA = 128
HEADS = 8
NOPE = 64
ROPE = 32
VDIM = 64
MLA_W = HEADS * VDIM
GW = 512
CHUNK = 128
HP = 128
PAIRS = HEADS // 2
D_IN = 2464
D_INR = 2560
ROPE_THETA = 10000.0
DN_ALPHA = 2.0 ** 0.25
EPS = 1e-5
SCALE = 1.0 / math.sqrt(NOPE + ROPE)
SCALE_LOG2E = SCALE * 1.4426950408889634
INV_SQRT2 = 0.7071067811865476
INV_SQRT_2PI = 0.3989422804014327

ADAM_LR = 0.001
ADAM_B1 = 0.9
ADAM_B2 = 0.999
ADAM_EPS = 1e-08
ADAM_WD = 0.01
ADAM_STEP = 10

PV_QG, PV_KVG, PV_SG, PV_SB, PV_INVF, PV_M1, PV_M2, PV_LNG, PV_LNB = range(9)
PV_ROWS = 16
GV_QG, GV_KVG, GV_SG, GV_SB, GV_LNG, GV_LNB, GV_LOSS = range(7)
GV_BSP = 8
GV_ROWS = 16

MESH = pl.DeviceIdType.MESH

FWD_TILE = 512
POST_TILE = 512
BWD_TILE = 512
ATT_BLK = 512
ADAM_STEPS = 4
VMEM_LIMIT = 56 * 1024 * 1024


def _dot(a, b):
    return jnp.dot(a, b, preferred_element_type=F32)


def _dot_nt(a, b):
    return lax.dot_general(a, b, (((1,), (1,)), ((), ())), preferred_element_type=F32)


def _dot_tn(a, b):
    return lax.dot_general(a, b, (((0,), (0,)), ((), ())), preferred_element_type=F32)


def _sigmoid(z):
    return pl.reciprocal(1.0 + jnp.exp(-z), approx=True)


def _gelu_and_grad(x):
    cdf = 0.5 * (1.0 + lax.erf(x * INV_SQRT2))
    return x * cdf, cdf + x * (INV_SQRT_2PI * jnp.exp(-0.5 * x * x))


def _rms_stats(x):
    r = lax.rsqrt(jnp.mean(x * x, axis=-1, keepdims=True) + EPS)
    return x * r, r


def _rms_bwd(dy, g, xh, r):
    dyg = dy * g
    return r * (dyg - xh * jnp.mean(dyg * xh, axis=-1, keepdims=True))


def _ln_stats(x):
    mu = jnp.mean(x, axis=-1, keepdims=True)
    xc = x - mu
    r = lax.rsqrt(jnp.mean(xc * xc, axis=-1, keepdims=True) + EPS)
    return xc * r, r


def _ln_bwd(dy, g, xh, r):
    dxh = dy * g
    return r * (dxh - jnp.mean(dxh, axis=-1, keepdims=True) - xh * jnp.mean(dxh * xh, axis=-1, keepdims=True))


def _rope_fwd(t, c, s1, s2):
    return t * c + pltpu.roll(t, HP - 16, 1) * s1 + pltpu.roll(t, 16, 1) * s2


def _rope_bwd(d, c, s1, s2):
    return d * c + pltpu.roll(d * s1, 16, 1) + pltpu.roll(d * s2, HP - 16, 1)


def _lane_lt64(shape):
    return lax.broadcasted_iota(jnp.int32, shape, len(shape) - 1) < 64


def _spatial_mix(w_ref, src, dst_ref, rows):
    for c in range(rows // CHUNK):
        for p in range(PAIRS):
            blk = src[c * CHUNK:(c + 1) * CHUNK, p * HP:(p + 1) * HP]
            a = _dot(w_ref[2 * p], blk)
            b = _dot(w_ref[2 * p + 1], blk)
            dst_ref[c * CHUNK:(c + 1) * CHUNK, p * HP:(p + 1) * HP] = jnp.where(_lane_lt64(a.shape), a, b)


def _gmlp_fwd(u_pre, v_pre, zb, sg, sb, wt_ref, bsp_ref, sv_ref, rows):
    u, du = _gelu_and_grad(u_pre)
    gv, dgv = _gelu_and_grad(v_pre)
    xh, r = _ln_stats(gv)
    vln = (xh * sg + sb).astype(BF16)
    _spatial_mix(wt_ref, vln, sv_ref, rows)
    bias = bsp_ref[...]
    svb = sv_ref[...] + jnp.concatenate([bias] * (rows // CHUNK), axis=0)
    sig = _sigmoid(zb)
    return u, du, dgv, xh, r, vln, svb, sig


def _weight_gather(shards):
    n_arr = len(shards)

    def body(*refs):
        ins, outs = refs[0:n_arr], refs[n_arr:2 * n_arr]
        send_sems, recv_sems = refs[2 * n_arr:]
        x, y, c = lax.axis_index("x"), lax.axis_index("y"), lax.axis_index("c")
        j = 2 * x + y
        sib = (x, y, 1 - c)
        chips = [(1 - x, y), (x, 1 - y), (1 - x, 1 - y)]
        for n in range(n_arr):
            outs[n][j] = ins[n][...].astype(BF16)

        def half(n, blk, core):
            r = shards[n].shape[0] // 2
            return outs[n].at[blk, pl.ds(pl.multiple_of(core * r, 16), r), :]

        def copy(k, ref, to):
            return pltpu.make_async_remote_copy(
                src_ref=ref, dst_ref=ref, send_sem=send_sems.at[k], recv_sem=recv_sems.at[k],
                device_id=to, device_id_type=MESH)

        first = [copy(6 * n + kk, half(n, j, c), (px, py, c))
                 for n in range(n_arr) for kk, (px, py) in enumerate(chips)]
        for cp in first:
            cp.start()
        passed = []
        for n in range(n_arr):
            for kk, (px, py) in enumerate(chips):
                landed = half(n, 2 * px + py, c)
                copy(6 * n + kk, landed, (px, py, c)).wait_recv()
                passed.append(copy(6 * n + 3 + kk, landed, sib))
                passed[-1].start()
        for n in range(n_arr):
            for kk, (px, py) in enumerate(chips):
                copy(6 * n + 3 + kk, half(n, 2 * px + py, 1 - c), sib).wait_recv()
        for cp in first + passed:
            cp.wait_send()

    vmem = pl.BlockSpec(memory_space=pltpu.VMEM)
    return pl.pallas_call(
        body, name="weight_gather",
        out_shape=[jax.ShapeDtypeStruct((4,) + a.shape, BF16) for a in shards],
        in_specs=[vmem] * n_arr, out_specs=[vmem] * n_arr,
        scratch_shapes=[pltpu.SemaphoreType.DMA((6 * n_arr,)), pltpu.SemaphoreType.DMA((6 * n_arr,))],
        compiler_params=pltpu.CompilerParams(vmem_limit_bytes=VMEM_LIMIT),
    )(*shards)


def _fwd_pre(x, pos, win, wuq, wkv, pvec):
    seq = x.shape[0]
    t = FWD_TILE

    def body(x_ref, pos_ref, win_ref, wuq_ref, wkv_ref, pv_ref,
             cq_o, ckv_o, gate_o, q_o, k_o, v_o, vt_o, cs_o):
        proj = _dot(x_ref[...].astype(BF16), win_ref[...])
        cq = proj[:, 0:256]
        ckv = proj[:, 256:384]
        kr = proj[:, 384:512]
        cq_o[...] = cq
        ckv_o[...] = ckv
        gate_o[...] = proj[:, 512:D_INR].astype(BF16)

        ang = pos_ref[...].astype(F32) * pv_ref[PV_INVF:PV_INVF + 1, 0:HP]
        cos = jnp.cos(ang)
        sin = jnp.sin(ang)
        cs_o[:, 0:HP] = cos
        cs_o[:, HP:2 * HP] = sin
        s1 = sin * pv_ref[PV_M1:PV_M1 + 1, 0:HP]
        s2 = sin * pv_ref[PV_M2:PV_M2 + 1, 0:HP]

        cqh, _ = _rms_stats(cq)
        q_all = _dot((cqh * pv_ref[PV_QG:PV_QG + 1, 0:Q_LORA]).astype(BF16), wuq_ref[...])
        ckvh, _ = _rms_stats(ckv)
        kv_all = _dot((ckvh * pv_ref[PV_KVG:PV_KVG + 1, 0:KV_LORA]).astype(BF16), wkv_ref[...])
        krr = _rope_fwd(kr, cos, s1, s2)
        for h in range(HEADS):
            sl = slice(h * HP, (h + 1) * HP)
            q_o[:, sl] = (_rope_fwd(q_all[:, sl], cos, s1, s2) * SCALE_LOG2E).astype(BF16)
            k_o[:, sl] = (kv_all[:, sl] + krr).astype(BF16)
        val = kv_all[:, HEADS * HP:].astype(BF16)
        v_o[...] = val
        vt_o[...] = val.T

    tile = lambda w: pl.BlockSpec((t, w), lambda i: (i, 0))
    full = lambda a: pl.BlockSpec(a.shape, lambda i: (0,) * a.ndim)
    outs = [(Q_LORA, F32), (KV_LORA, F32), (2048, BF16), (HEADS * HP, BF16), (HEADS * HP, BF16), (MLA_W, BF16)]
    per_blk = ATT_BLK // t
    out_specs = [tile(w) for w, _ in outs]
    out_specs += [pl.BlockSpec((None, MLA_W, t), lambda i: (i // per_blk, 0, i % per_blk)), tile(2 * HP)]
    out_shape = [jax.ShapeDtypeStruct((seq, w), d) for w, d in outs]
    out_shape += [jax.ShapeDtypeStruct((seq // ATT_BLK, MLA_W, ATT_BLK), BF16), jax.ShapeDtypeStruct((seq, 2 * HP), F32)]
    return pl.pallas_call(
        body, name="fwd_pre", grid=(seq // t,),
        in_specs=[tile(D_MODEL), tile(1), full(win), full(wuq), full(wkv), full(pvec)],
        out_specs=out_specs, out_shape=out_shape,
        compiler_params=pltpu.CompilerParams(dimension_semantics=("arbitrary",), vmem_limit_bytes=VMEM_LIMIT),
    )(x, pos, win, wuq, wkv, pvec)


def _attn_fwd(q, k, vt):
    seq = q.shape[0]
    b = ATT_BLK
    nq = seq // b
    assert nq % 2 == 0
    n_off = nq * (nq - 1) // 2

    def body(q_ref, k_ref, vt_ref, o_o, lse_o, m_ref, l_ref, acc_ref, s_even, s_odd):
        m_ref[...] = jnp.full(m_ref.shape, -jnp.inf, F32)
        l_ref[...] = jnp.zeros(l_ref.shape, F32)
        acc_ref[...] = jnp.zeros(acc_ref.shape, F32)

        def scores(i, j, s_ref):
            qrows = pl.ds(pl.multiple_of(i * b, b), b)
            krows = pl.ds(pl.multiple_of(j * b, b), b)
            for a in range(2):
                s_ref[a] = _dot_nt(k_ref[krows, a * HP:(a + 1) * HP], q_ref[qrows, a * HP:(a + 1) * HP])

        def consume(i, j, s_ref, masked):
            vt_blk = vt_ref[j]
            for a in range(2):
                st = s_ref[a]
                if masked:
                    ki = lax.broadcasted_iota(jnp.int32, st.shape, 0)
                    qi = lax.broadcasted_iota(jnp.int32, st.shape, 1)
                    st = jnp.where(ki <= qi, st, -jnp.inf)
                m_prev = m_ref[i, a:a + 1, :]
                m_new = jnp.maximum(m_prev, jnp.max(st, axis=0, keepdims=True))
                alpha = jnp.exp2(m_prev - m_new)
                pt = jnp.exp2(st - m_new)
                l_ref[i, a:a + 1, :] = alpha * l_ref[i, a:a + 1, :] + jnp.sum(pt, axis=0, keepdims=True)
                acc_ref[i, a] = alpha * acc_ref[i, a] + _dot(vt_blk, pt.astype(BF16))
                m_ref[i, a:a + 1, :] = m_new

        def after(i, j):
            wrap = j + 1 >= i
            return jnp.minimum(jnp.where(wrap, i + 1, i), nq - 1), jnp.where(wrap, 0, j + 1)

        if n_off > 0:
            scores(1, 0, s_even)

            def below(u, ij):
                i1, j1 = after(*ij)
                scores(i1, j1, s_odd)
                consume(ij[0], ij[1], s_even, False)
                i2, j2 = after(i1, j1)
                scores(i2, j2, s_even)
                consume(i1, j1, s_odd, False)
                return i2, j2

            last = lax.fori_loop(0, n_off // 2, below, (jnp.int32(1), jnp.int32(0)))
            if n_off % 2:
                consume(last[0], last[1], s_even, False)

        scores(0, 0, s_even)

        def diagonal(u, carry):
            i0 = 2 * u
            scores(i0 + 1, i0 + 1, s_odd)
            consume(i0, i0, s_even, True)
            i2 = jnp.minimum(i0 + 2, nq - 1)
            scores(i2, i2, s_even)
            consume(i0 + 1, i0 + 1, s_odd, True)
            return carry

        lax.fori_loop(0, nq // 2, diagonal, 0)
        top = lax.broadcasted_iota(jnp.int32, (HP, b), 0) < 64

        def finish(i, carry):
            rows = pl.ds(pl.multiple_of(i * b, b), b)
            o_o[rows, :] = jnp.where(top, acc_ref[i, 0] / l_ref[i, 0:1, :], acc_ref[i, 1] / l_ref[i, 1:2, :]).T
            lse_o[i] = m_ref[i, 0:2, :] + jnp.log2(l_ref[i, 0:2, :])
            return carry

        lax.fori_loop(0, nq, finish, 0)

    return pl.pallas_call(
        body, name="attn_fwd", grid=(PAIRS,),
        in_specs=[pl.BlockSpec((seq, 2 * HP), lambda p: (0, p)),
                  pl.BlockSpec((seq, 2 * HP), lambda p: (0, p)),
                  pl.BlockSpec((nq, HP, b), lambda p: (0, p, 0))],
        out_specs=[pl.BlockSpec((seq, HP), lambda p: (0, p)),
                   pl.BlockSpec((None, nq, 2, b), lambda p: (p, 0, 0, 0))],
        out_shape=[jax.ShapeDtypeStruct((seq, MLA_W), F32),
                   jax.ShapeDtypeStruct((PAIRS, nq, 2, b), F32)],
        scratch_shapes=[pltpu.VMEM((nq, 8, b), F32), pltpu.VMEM((nq, 8, b), F32), pltpu.VMEM((nq, 2, HP, b), F32),
                        pltpu.VMEM((2, b, b), F32), pltpu.VMEM((2, b, b), F32)],
        compiler_params=pltpu.CompilerParams(dimension_semantics=("arbitrary",), vmem_limit_bytes=VMEM_LIMIT),
    )(q, k, vt)


def _post(x, tgt, o, gate, wout, pvec, wt, wtt, bsp):
    seq = x.shape[0]
    t = POST_TILE
    nt = seq // t

    def body(x_ref, tgt_ref, o_ref, gate_ref, wout_ref, pv_ref, wt_ref, wtt_ref, bsp_ref,
             dh2_o, do_o, dgate_o, gwout_o, gwsp_o, vec_o, sv_ref, dvln_ref, bacc_ref):
        i = pl.program_id(0)

        @pl.when(i == 0)
        def _():
            gwout_o[...] = jnp.zeros_like(gwout_o)
            gwsp_o[...] = jnp.zeros_like(gwsp_o)
            vec_o[...] = jnp.zeros_like(vec_o)
            bacc_ref[...] = jnp.zeros_like(bacc_ref)

        za = gate_ref[:, 0:512].astype(F32)
        u_pre = gate_ref[:, 512:1024].astype(F32)
        v_pre = gate_ref[:, 1024:1536].astype(F32)
        zb = gate_ref[:, 1536:2048].astype(F32)
        sg = pv_ref[PV_SG:PV_SG + 1, 0:GW]
        sb = pv_ref[PV_SB:PV_SB + 1, 0:GW]
        lng = pv_ref[PV_LNG:PV_LNG + 1, :]
        lnb = pv_ref[PV_LNB:PV_LNB + 1, :]
        o = o_ref[...]

        sig_a = _sigmoid(za)
        silu_a = za * sig_a
        u, du, dgv, xh, r, vln, svb, sig_b = _gmlp_fwd(u_pre, v_pre, zb, sg, sb, wt_ref, bsp_ref, sv_ref, t)
        silu_b = zb * sig_b
        sgu = u * svb
        merged = jnp.concatenate([o * silu_a, sgu * silu_b], axis=1).astype(BF16)
        h2 = DN_ALPHA * x_ref[...] + _dot(merged, wout_ref[...])
        xh2, r2 = _ln_stats(h2)
        err = xh2 * lng + lnb - tgt_ref[...]
        d_out = err * (1.0 / D_MODEL)
        vec_o[GV_LNG:GV_LNG + 1, :] += jnp.sum(d_out * xh2, axis=0, keepdims=True)
        vec_o[GV_LNB:GV_LNB + 1, :] += jnp.sum(d_out, axis=0, keepdims=True)
        vec_o[GV_LOSS:GV_LOSS + 1, :] += jnp.sum(err * err, axis=0, keepdims=True) * (0.5 / D_MODEL)

        d_h2 = _ln_bwd(d_out, lng, xh2, r2)
        dh2_o[...] = d_h2
        dh2b = d_h2.astype(BF16)
        gwout_o[...] += _dot_tn(merged, dh2b)
        d_m = _dot_nt(dh2b, wout_ref[...])
        d_oa = d_m[:, 0:512]
        d_ob = d_m[:, 512:1024]
        do_o[...] = (d_oa * silu_a).astype(BF16)
        dgate_o[:, 0:512] = (d_oa * o * (sig_a * (1.0 + za * (1.0 - sig_a)))).astype(BF16)
        dgate_o[:, 1536:2048] = (d_ob * sgu * (sig_b * (1.0 + zb * (1.0 - sig_b)))).astype(BF16)
        d_sgu = d_ob * silu_b
        dgate_o[:, 512:1024] = (d_sgu * svb * du).astype(BF16)
        d_sv = d_sgu * u
        acc = bacc_ref[...]
        for c in range(t // CHUNK):
            acc = acc + d_sv[c * CHUNK:(c + 1) * CHUNK, :]
        bacc_ref[...] = acc
        d_svb = d_sv.astype(BF16)
        for c in range(t // CHUNK):
            for p in range(PAIRS):
                blk = d_svb[c * CHUNK:(c + 1) * CHUNK, p * HP:(p + 1) * HP]
                vblk = vln[c * CHUNK:(c + 1) * CHUNK, p * HP:(p + 1) * HP]
                first = _lane_lt64(blk.shape)
                gwsp_o[2 * p] += _dot_nt(jnp.where(first, blk, jnp.zeros_like(blk)), vblk)
                gwsp_o[2 * p + 1] += _dot_nt(jnp.where(first, jnp.zeros_like(blk), blk), vblk)
        _spatial_mix(wtt_ref, d_svb, dvln_ref, t)
        d_vln = dvln_ref[...]
        vec_o[GV_SG:GV_SG + 1, 0:GW] += jnp.sum(d_vln * xh, axis=0, keepdims=True)
        vec_o[GV_SB:GV_SB + 1, 0:GW] += jnp.sum(d_vln, axis=0, keepdims=True)
        dgate_o[:, 1024:1536] = (_ln_bwd(d_vln, sg, xh, r) * dgv).astype(BF16)

        @pl.when(i == nt - 1)
        def _():
            tri = (lax.broadcasted_iota(jnp.int32, (CHUNK, CHUNK), 1)
                   <= lax.broadcasted_iota(jnp.int32, (CHUNK, CHUNK), 0))
            for h in range(HEADS):
                gwsp_o[h] = jnp.where(tri, gwsp_o[h], 0.0)
            lane = lax.broadcasted_iota(jnp.int32, (CHUNK, HP), 1)
            res = jnp.zeros((CHUNK, HP), F32)
            for h in range(HEADS):
                p, a = divmod(h, 2)
                blk = bacc_ref[:, p * HP:(p + 1) * HP]
                part = jnp.where(_lane_lt64(blk.shape) == (a == 0), blk, 0.0)
                res = jnp.where(lane == h, jnp.sum(part, axis=-1, keepdims=True), res)
            vec_o[GV_BSP:GV_BSP + HEADS, 0:HP] = res.T[0:HEADS, :]
            lane1 = lax.broadcasted_iota(jnp.int32, (1, D_MODEL), 1)
            total = jnp.sum(vec_o[GV_LOSS:GV_LOSS + 1, :], axis=-1, keepdims=True)
            vec_o[GV_LOSS:GV_LOSS + 1, :] = jnp.where(lane1 == 0, total, 0.0)

    tile = lambda w: pl.BlockSpec((t, w), lambda i: (i, 0))
    full = lambda a: pl.BlockSpec(a.shape, lambda i: (0,) * a.ndim)
    const = lambda s: pl.BlockSpec(s, lambda i: (0,) * len(s))
    return pl.pallas_call(
        body, name="post", grid=(nt,),
        in_specs=[tile(D_MODEL), tile(D_MODEL), tile(MLA_W), tile(2048), full(wout), full(pvec),
                  full(wt), full(wtt), full(bsp)],
        out_specs=[tile(D_MODEL), tile(MLA_W), tile(2048), const((D_MODEL, D_MODEL)),
                   const((HEADS, CHUNK, CHUNK)), const((GV_ROWS, D_MODEL))],
        out_shape=[jax.ShapeDtypeStruct((seq, D_MODEL), F32), jax.ShapeDtypeStruct((seq, MLA_W), BF16),
                   jax.ShapeDtypeStruct((seq, 2048), BF16), jax.ShapeDtypeStruct((D_MODEL, D_MODEL), F32),
                   jax.ShapeDtypeStruct((HEADS, CHUNK, CHUNK), F32), jax.ShapeDtypeStruct((GV_ROWS, D_MODEL), F32)],
        scratch_shapes=[pltpu.VMEM((t, GW), F32), pltpu.VMEM((t, GW), F32), pltpu.VMEM((CHUNK, GW), F32)],
        compiler_params=pltpu.CompilerParams(dimension_semantics=("arbitrary",), vmem_limit_bytes=VMEM_LIMIT),
    )(x, tgt, o, gate, wout, pvec, wt, wtt, bsp)


def _attn_bwd(q, k, v, do, o, lse, cs, pvec):
    seq = q.shape[0]
    b = ATT_BLK
    nq = seq // b

    def body(q_ref, k_ref, v_ref, do_ref, o_ref, lse_ref, cs_ref, pv_ref, dq_o, dk_o, dv_o, dk_acc, dv_acc):
        i = pl.program_id(1)

        @pl.when(i == 0)
        def _():
            dk_acc[...] = jnp.zeros_like(dk_acc)
            dv_acc[...] = jnp.zeros_like(dv_acc)

        first = _lane_lt64((b, HP))
        do = do_ref[...]
        zero = jnp.zeros_like(do)
        dos = [jnp.where(first, do, zero), jnp.where(first, zero, do)]
        prod_t = (do.astype(F32) * o_ref[...]).T
        deltas = [jnp.sum(prod_t[0:64, :], axis=0, keepdims=True),
                  jnp.sum(prod_t[64:128, :], axis=0, keepdims=True)]
        lses = [lse_ref[0:1, :], lse_ref[1:2, :]]
        qs = [q_ref[:, a * HP:(a + 1) * HP] for a in range(2)]

        def step(j, dqs, masked):
            rows = pl.ds(pl.multiple_of(j * b, b), b)
            vb = v_ref[rows, :]
            new_dq = []
            dvs = []
            for a in range(2):
                kb = k_ref[rows, a * HP:(a + 1) * HP]
                pt = jnp.exp2(_dot_nt(kb, qs[a]) - lses[a])
                if masked:
                    ki = lax.broadcasted_iota(jnp.int32, pt.shape, 0)
                    qi = lax.broadcasted_iota(jnp.int32, pt.shape, 1)
                    pt = jnp.where(ki <= qi, pt, 0.0)
                dvs.append(_dot(pt.astype(BF16), do))
                dpt = _dot_nt(vb, dos[a])
                dst = (pt * (dpt - deltas[a])).astype(BF16)
                dk_acc[rows, a * HP:(a + 1) * HP] += _dot(dst, qs[a])
                new_dq.append(dqs[a] + _dot_tn(dst, kb))
            dv_acc[rows, :] += jnp.where(first, dvs[0], dvs[1])
            return tuple(new_dq)

        init = (jnp.zeros((b, HP), F32), jnp.zeros((b, HP), F32))
        dqs = lax.fori_loop(0, i, lambda j, cr: step(j, cr, False), init)
        dqs = step(i, dqs, True)
        cos = cs_ref[:, 0:HP]
        sin = cs_ref[:, HP:2 * HP]
        s1 = sin * pv_ref[PV_M1:PV_M1 + 1, 0:HP]
        s2 = sin * pv_ref[PV_M2:PV_M2 + 1, 0:HP]
        for a in range(2):
            dq_o[:, a * HP:(a + 1) * HP] = _rope_bwd(dqs[a] * SCALE, cos, s1, s2).astype(BF16)

        @pl.when(i == nq - 1)
        def _():
            dk_o[...] = (dk_acc[...] * (SCALE / SCALE_LOG2E)).astype(BF16)
            dv_o[...] = dv_acc[...].astype(BF16)

    return pl.pallas_call(
        body, name="attn_bwd", grid=(PAIRS, nq),
        in_specs=[pl.BlockSpec((b, 2 * HP), lambda p, i: (i, p)),
                  pl.BlockSpec((seq, 2 * HP), lambda p, i: (0, p)),
                  pl.BlockSpec((seq, HP), lambda p, i: (0, p)),
                  pl.BlockSpec((b, HP), lambda p, i: (i, p)),
                  pl.BlockSpec((b, HP), lambda p, i: (i, p)),
                  pl.BlockSpec((None, None, 2, b), lambda p, i: (p, i, 0, 0)),
                  pl.BlockSpec((b, 2 * HP), lambda p, i: (i, 0)),
                  pl.BlockSpec(pvec.shape, lambda p, i: (0, 0))],
        out_specs=[pl.BlockSpec((b, 2 * HP), lambda p, i: (i, p)),
                   pl.BlockSpec((seq, 2 * HP), lambda p, i: (0, p)),
                   pl.BlockSpec((seq, HP), lambda p, i: (0, p))],
        out_shape=[jax.ShapeDtypeStruct((seq, HEADS * HP), BF16),
                   jax.ShapeDtypeStruct((seq, HEADS * HP), BF16),
                   jax.ShapeDtypeStruct((seq, MLA_W), BF16)],
        scratch_shapes=[pltpu.VMEM((seq, 2 * HP), F32), pltpu.VMEM((seq, HP), F32)],
        compiler_params=pltpu.CompilerParams(dimension_semantics=("arbitrary", "arbitrary"),
                                             vmem_limit_bytes=VMEM_LIMIT),
    )(q, k, v, do, o, lse, cs, pvec)


def _bwd_pre(x, dh2, cq, ckv, cs, dq, dk, dv, dgate, win, wuq, wkv, pvec, gvec):
    seq = x.shape[0]
    t = BWD_TILE

    def body(x_ref, dh2_ref, cq_ref, ckv_ref, cs_ref, dq_ref, dk_ref, dv_ref, dgate_ref,
             win_ref, wuq_ref, wkv_ref, pv_ref, gv_ref, gx_o, gwin_o, gwuq_o, gwkv_o, vec_o):
        i = pl.program_id(0)

        @pl.when(i == 0)
        def _():
            gwin_o[...] = jnp.zeros_like(gwin_o)
            gwuq_o[...] = jnp.zeros_like(gwuq_o)
            gwkv_o[...] = jnp.zeros_like(gwkv_o)
            vec_o[...] = gv_ref[...]

        qg = pv_ref[PV_QG:PV_QG + 1, 0:Q_LORA]
        kvg = pv_ref[PV_KVG:PV_KVG + 1, 0:KV_LORA]
        dq = dq_ref[...]
        cqh, rq = _rms_stats(cq_ref[...])
        d_cqn = _dot_nt(dq, wuq_ref[...])
        gwuq_o[...] += _dot_tn((cqh * qg).astype(BF16), dq)
        vec_o[GV_QG:GV_QG + 1, 0:Q_LORA] += jnp.sum(d_cqn * cqh, axis=0, keepdims=True)
        d_cq = _rms_bwd(d_cqn, qg, cqh, rq)

        dk = dk_ref[...]
        dkv = jnp.concatenate([dk, dv_ref[...]], axis=1)
        ckvh, rkv = _rms_stats(ckv_ref[...])
        d_ckvn = _dot_nt(dkv, wkv_ref[...])
        gwkv_o[...] += _dot_tn((ckvh * kvg).astype(BF16), dkv)
        vec_o[GV_KVG:GV_KVG + 1, 0:KV_LORA] += jnp.sum(d_ckvn * ckvh, axis=0, keepdims=True)
        d_ckv = _rms_bwd(d_ckvn, kvg, ckvh, rkv)

        dks = dk[:, 0:HP].astype(F32)
        for h in range(1, HEADS):
            dks = dks + dk[:, h * HP:(h + 1) * HP].astype(F32)
        cos = cs_ref[:, 0:HP]
        sin = cs_ref[:, HP:2 * HP]
        d_kr = _rope_bwd(dks, cos, sin * pv_ref[PV_M1:PV_M1 + 1, 0:HP], sin * pv_ref[PV_M2:PV_M2 + 1, 0:HP])

        d_proj = jnp.concatenate([d_cq.astype(BF16), d_ckv.astype(BF16), d_kr.astype(BF16), dgate_ref[...]], axis=1)
        gwin_o[...] += _dot_tn(x_ref[...].astype(BF16), d_proj)
        gx_o[...] = DN_ALPHA * dh2_ref[...] + _dot_nt(d_proj, win_ref[...])

    tile = lambda w: pl.BlockSpec((t, w), lambda i: (i, 0))
    full = lambda a: pl.BlockSpec(a.shape, lambda i: (0,) * a.ndim)
    const = lambda s: pl.BlockSpec(s, lambda i: (0,) * len(s))
    return pl.pallas_call(
        body, name="bwd_pre", grid=(seq // t,),
        in_specs=[tile(D_MODEL), tile(D_MODEL), tile(Q_LORA), tile(KV_LORA), tile(2 * HP), tile(HEADS * HP),
                  tile(HEADS * HP), tile(MLA_W), tile(2048), full(win), full(wuq), full(wkv), full(pvec), full(gvec)],
        out_specs=[tile(D_MODEL), const((D_MODEL, D_INR)), const((Q_LORA, HEADS * HP)),
                   const((KV_LORA, HEADS * HP + MLA_W)), const((GV_ROWS, D_MODEL))],
        out_shape=[jax.ShapeDtypeStruct((seq, D_MODEL), F32), jax.ShapeDtypeStruct((D_MODEL, D_INR), F32),
                   jax.ShapeDtypeStruct((Q_LORA, HEADS * HP), F32),
                   jax.ShapeDtypeStruct((KV_LORA, HEADS * HP + MLA_W), F32),
                   jax.ShapeDtypeStruct((GV_ROWS, D_MODEL), F32)],
        compiler_params=pltpu.CompilerParams(dimension_semantics=("arbitrary",), vmem_limit_bytes=VMEM_LIMIT),
    )(x, dh2, cq, ckv, cs, dq, dk, dv, dgate, win, wuq, wkv, pvec, gvec)


def _grad_reduce(gs, gvec):
    n_arr = len(gs)
    n_big = n_arr - 1
    k1 = lambda n, blk: 4 * n + blk
    k2 = lambda n, kk: 4 * n_arr + 3 * n + kk
    k3 = lambda n: 7 * n_arr + n
    k3w = lambda k: 7 * n_arr + n_big + k
    kv = lambda k: 7 * n_arr + n_big + 7 + k
    n_sem = 7 * n_arr + n_big + 14

    def body(*refs):
        g, gv = refs[0:n_arr], refs[n_arr]
        outs, ov = refs[n_arr + 1:2 * n_arr + 1], refs[2 * n_arr + 1]
        r1 = refs[2 * n_arr + 2:3 * n_arr + 2]
        r2 = refs[3 * n_arr + 2:4 * n_arr + 2]
        s2 = refs[4 * n_arr + 2:5 * n_arr + 2]
        vbuf, send_sems, recv_sems = refs[5 * n_arr + 2:]
        x, y, c = lax.axis_index("x"), lax.axis_index("y"), lax.axis_index("c")
        j = 2 * x + y
        me = 2 * j + c
        sib = (x, y, 1 - c)
        chips = [(1 - x, y), (x, 1 - y), (1 - x, 1 - y)]
        others = [sib] + [(px, py, pc) for (px, py) in chips for pc in (c, 1 - c)]

        def copy(k, src, dst, to):
            return pltpu.make_async_remote_copy(
                src_ref=src, dst_ref=dst, send_sem=send_sems.at[k], recv_sem=recv_sems.at[k],
                device_id=to, device_id_type=MESH)

        l1 = [copy(k1(n, blk), g[n].at[blk, 1 - c], r1[n].at[blk], sib) for n in range(n_arr) for blk in range(4)]
        lv = [copy(kv(k), gv, vbuf.at[me], to) for k, to in enumerate(others)]
        for cp in l1 + lv:
            cp.start()
        l2 = []
        for n in range(n_arr):
            for blk in range(4):
                copy(k1(n, blk), g[n].at[blk, c], r1[n].at[blk], sib).wait_recv()
            for blk in range(4):
                r1[n][blk] = g[n][blk, c] + r1[n][blk]
                s2[n][blk] = r1[n][blk].astype(BF16)
            for kk, (px, py) in enumerate(chips):
                l2.append(copy(k2(n, kk), s2[n].at[2 * px + py], r2[n].at[kk], (px, py, c)))
                l2[-1].start()

        l3 = []
        for n in range(n_arr):
            for kk in range(3):
                copy(k2(n, kk), s2[n].at[0], r2[n].at[kk], sib).wait_recv()
            red = ((r1[n][j] + r2[n][0].astype(F32)) + r2[n][1].astype(F32)) + r2[n][2].astype(F32)
            if n < n_big:
                outs[n][c] = red
                back = [copy(k3(n), outs[n].at[c], outs[n].at[c], sib)]
            else:
                outs[n][j, c] = red
                back = [copy(k3w(k), outs[n].at[j, c], outs[n].at[j, c], to) for k, to in enumerate(others)]
            for cp in back:
                cp.start()
            l3 += back
        for n in range(n_big):
            copy(k3(n), outs[n].at[1 - c], outs[n].at[1 - c], sib).wait_recv()
        for k, (px, py, pc) in enumerate(others):
            landed = outs[n_big].at[2 * px + py, pc]
            copy(k3w(k), landed, landed, (px, py, pc)).wait_recv()
            copy(kv(k), gv, vbuf.at[4 * px + 2 * py + pc], (px, py, pc)).wait_recv()
        vbuf[me] = gv[...]
        total = vbuf[0]
        for d in range(1, 8):
            total = total + vbuf[d]
        ov[...] = total
        for cp in l1 + lv + l2 + l3:
            cp.wait_send()

    vmem = pl.BlockSpec(memory_space=pltpu.VMEM)
    half_shapes = [a.shape[2:] for a in gs]
    out_shape = [jax.ShapeDtypeStruct((2,) + s, F32) for s in half_shapes[:n_big]]
    out_shape += [jax.ShapeDtypeStruct((4, 2) + half_shapes[n_big], F32), jax.ShapeDtypeStruct(gvec.shape, F32)]
    scratch = [pltpu.VMEM((4,) + s, F32) for s in half_shapes] + [pltpu.VMEM((3,) + s, BF16) for s in half_shapes]
    scratch += [pltpu.VMEM((4,) + s, BF16) for s in half_shapes]
    scratch += [pltpu.VMEM((8,) + gvec.shape, F32), pltpu.SemaphoreType.DMA((n_sem,)), pltpu.SemaphoreType.DMA((n_sem,))]
    return pl.pallas_call(
        body, name="grad_reduce", out_shape=out_shape,
        in_specs=[vmem] * (n_arr + 1), out_specs=[vmem] * (n_arr + 1), scratch_shapes=scratch,
        compiler_params=pltpu.CompilerParams(vmem_limit_bytes=VMEM_LIMIT),
    )(*gs, gvec)


SMALL_ROWS = ((GV_QG, 1, Q_LORA), (GV_KVG, 1, KV_LORA), (GV_SG, 1, GW), (GV_SB, 1, GW),
              (GV_LNG, 1, D_MODEL), (GV_LNB, 1, D_MODEL), (GV_BSP, HEADS, CHUNK))


def _adam_update(g, w, m, v):
    m_new = ADAM_B1 * m + (1.0 - ADAM_B1) * g
    v_new = ADAM_B2 * v + (1.0 - ADAM_B2) * (g * g)
    m_hat = m_new / (1.0 - ADAM_B1 ** ADAM_STEP)
    v_hat = v_new / (1.0 - ADAM_B2 ** ADAM_STEP)
    return -ADAM_LR * (m_hat / (jnp.sqrt(v_hat) + ADAM_EPS) + ADAM_WD * w), m_new, v_new


def _adamw(g_big, w_big, m_big, v_big, gvec, w_small, m_small, v_small):
    nb, ns = len(g_big), len(w_small)

    def body(*refs):
        it = iter(refs)
        take = lambda n: [next(it) for _ in range(n)]
        g_b, w_b, m_b, v_b = take(nb), take(nb), take(nb), take(nb)
        gv = next(it)
        w_s, m_s, v_s = take(ns), take(ns), take(ns)
        g_bo, d_bo, m_bo, v_bo = take(nb), take(nb), take(nb), take(nb)
        g_so, d_so, m_so, v_so = take(ns), take(ns), take(ns), take(ns)
        for n in range(nb):
            gb = g_b[n][...]
            g_bo[n][...] = gb
            d_bo[n][...], m_bo[n][...], v_bo[n][...] = _adam_update(gb, w_b[n][...], m_b[n][...], v_b[n][...])
        for n, (row, nrow, width) in enumerate(SMALL_ROWS):
            gs = gv[row:row + nrow, 0:width]
            g_so[n][...] = gs
            d_so[n][...], m_so[n][...], v_so[n][...] = _adam_update(gs, w_s[n][...], m_s[n][...], v_s[n][...])

    def rows(a):
        nd = a.ndim
        return pl.BlockSpec((a.shape[0] // ADAM_STEPS,) + a.shape[1:], lambda i: (i,) + (0,) * (nd - 1))

    def whole(a):
        nd = a.ndim
        return pl.BlockSpec(a.shape, lambda i: (0,) * nd)

    big = [jax.ShapeDtypeStruct(a.shape, F32) for a in w_big]
    small = [jax.ShapeDtypeStruct(a.shape, F32) for a in w_small]
    return pl.pallas_call(
        body, name="adamw", grid=(ADAM_STEPS,), out_shape=big * 4 + small * 4,
        in_specs=[rows(a) for a in g_big + w_big + m_big + v_big] + [whole(gvec)]
        + [whole(a) for a in w_small + m_small + v_small],
        out_specs=[rows(a) for a in w_big] * 4 + [whole(a) for a in w_small] * 4,
        compiler_params=pltpu.CompilerParams(dimension_semantics=("arbitrary",), vmem_limit_bytes=VMEM_LIMIT),
    )(*g_big, *w_big, *m_big, *v_big, gvec, *w_small, *m_small, *v_small)


def kernel(x, positions, w_in, q_norm_g, w_uq, kv_norm_g, w_ukv, sgu_norm_g, sgu_norm_b, w_spatial, b_spatial, w_out, ln_g, ln_b, loss_target, m_w_in, m_q_norm_g, m_w_uq, m_kv_norm_g, m_w_ukv, m_sgu_norm_g, m_sgu_norm_b, m_w_spatial, m_b_spatial, m_w_out, m_ln_g, m_ln_b, v_w_in, v_q_norm_g, v_w_uq, v_kv_norm_g, v_w_ukv, v_sgu_norm_g, v_sgu_norm_b, v_w_spatial, v_b_spatial, v_w_out, v_ln_g, v_ln_b):
    seq = x.shape[1]
    x2 = x.reshape(seq, D_MODEL)
    tgt = loss_target.reshape(seq, D_MODEL)
    pos = positions.reshape(seq, 1)

    a_in, a_uq, a_ukv, a_out = _weight_gather([w_in, w_uq, w_ukv, w_out])
    w_uq_f = jnp.swapaxes(a_uq, 0, 1).reshape(Q_LORA, HEADS * (NOPE + ROPE))
    w_ukv_f = jnp.swapaxes(a_ukv, 0, 1).reshape(KV_LORA, HEADS * (NOPE + VDIM))
    wout = a_out.reshape(D_MODEL, D_MODEL)
    zc = lambda n: jnp.zeros((D_MODEL, n), BF16)
    win = jnp.concatenate([a_in[0][:, 0:384], zc(64), a_in[0][:, 384:416], zc(32), a_in[0][:, 416:],
                           a_in[1], a_in[2], a_in[3]], axis=1)
    wuq = jnp.pad(w_uq_f.reshape(Q_LORA, HEADS, NOPE + ROPE), ((0, 0), (0, 0), (0, HP - NOPE - ROPE)))
    wuq = wuq.reshape(Q_LORA, HEADS * HP)
    ukv = w_ukv_f.reshape(KV_LORA, HEADS, NOPE + VDIM)
    wk = jnp.pad(ukv[:, :, 0:NOPE], ((0, 0), (0, 0), (0, HP - NOPE))).reshape(KV_LORA, HEADS * HP)
    wkv = jnp.concatenate([wk, ukv[:, :, NOPE:].reshape(KV_LORA, MLA_W)], axis=1)

    lane = np.arange(HP)
    half = ROPE // 2
    inv_freq = (1.0 / (ROPE_THETA ** (np.arange(half, dtype=np.float32) / half))).astype(np.float32)
    in_rope = (lane >= NOPE) & (lane < NOPE + ROPE)
    invf = jnp.asarray(np.where(in_rope, inv_freq[(lane - NOPE) % half], 0.0).astype(np.float32))
    m1 = jnp.asarray(np.where((lane >= NOPE) & (lane < NOPE + half), -1.0, 0.0).astype(np.float32))
    m2 = jnp.asarray(np.where((lane >= NOPE + half) & (lane < NOPE + ROPE), 1.0, 0.0).astype(np.float32))
    row = lambda a: jnp.pad(a.astype(F32), (0, D_MODEL - a.shape[0]))
    pvec = jnp.stack([row(q_norm_g), row(kv_norm_g), row(sgu_norm_g), row(sgu_norm_b), row(invf), row(m1),
                      row(m2), row(ln_g), row(ln_b)] + [jnp.zeros((D_MODEL,), F32)] * (PV_ROWS - 9))
    tri = jnp.tril(jnp.ones((CHUNK, CHUNK), dtype=bool))
    wt = jnp.where(tri[None], w_spatial, 0.0).astype(BF16)
    wtt = jnp.swapaxes(wt, 1, 2)
    bsp = jnp.repeat(b_spatial.T, VDIM, axis=1)

    cq, ckv, gate, q, k, v, vt, cs = _fwd_pre(x2, pos, win, wuq, wkv, pvec)
    o, lse = _attn_fwd(q, k, vt)
    dh2, do, dgate, g_wout, g_wsp, gvec = _post(x2, tgt, o, gate, wout, pvec, wt, wtt, bsp)
    dq, dk, dv = _attn_bwd(q, k, v, do, o, lse, cs, pvec)
    gx, g_win, g_wuq, g_wkv, gvec = _bwd_pre(x2, dh2, cq, ckv, cs, dq, dk, dv, dgate, win, wuq, wkv, pvec, gvec)

    cw = w_in.shape[1]
    first = D_INR - 3 * cw
    g_win_0 = jnp.concatenate([g_win[:, 0:384], g_win[:, 448:480], g_win[:, 512:first]], axis=1)
    g_win_b = jnp.stack([g_win_0] + [g_win[:, first + cw * jb:first + cw * (jb + 1)] for jb in range(3)])
    g_wuq_f = g_wuq.reshape(Q_LORA, HEADS, HP)[:, :, 0:NOPE + ROPE].reshape(Q_LORA, HEADS * (NOPE + ROPE))
    g_k = g_wkv[:, 0:HEADS * HP].reshape(KV_LORA, HEADS, HP)[:, :, 0:NOPE]
    g_v = g_wkv[:, HEADS * HP:].reshape(KV_LORA, HEADS, VDIM)
    g_wukv_f = jnp.concatenate([g_k, g_v], axis=2).reshape(KV_LORA, HEADS * (NOPE + VDIM))

    def by_chip(a):
        rows, cols = a.shape[0], a.shape[1] // 4
        return jnp.swapaxes(a.reshape(rows, 4, cols), 0, 1).reshape(4, 2, rows // 2, cols)

    gs = [g_win_b.reshape(4, 2, D_MODEL // 2, cw), by_chip(g_wuq_f), by_chip(g_wukv_f), g_wout.reshape(4, 2, 128, D_MODEL),
          g_wsp.reshape(4, 2, CHUNK, CHUNK)]
    r_in, r_uq, r_ukv, r_out, r_wsp, r_vec = _grad_reduce(gs, gvec)

    g_big = [r_in.reshape(w_in.shape), r_uq.reshape(w_uq.shape), r_ukv.reshape(w_ukv.shape),
             r_out.reshape(w_out.shape), r_wsp.reshape(w_spatial.shape)]
    small = lambda qg, kvg, sg, sb, lng, lnb, bs: [qg.reshape(1, -1), kvg.reshape(1, -1), sg.reshape(1, -1),
                                                   sb.reshape(1, -1), lng.reshape(1, -1), lnb.reshape(1, -1), bs]
    res = _adamw(g_big, [w_in, w_uq, w_ukv, w_out, w_spatial], [m_w_in, m_w_uq, m_w_ukv, m_w_out, m_w_spatial],
                 [v_w_in, v_w_uq, v_w_ukv, v_w_out, v_w_spatial], r_vec,
                 small(q_norm_g, kv_norm_g, sgu_norm_g, sgu_norm_b, ln_g, ln_b, b_spatial),
                 small(m_q_norm_g, m_kv_norm_g, m_sgu_norm_g, m_sgu_norm_b, m_ln_g, m_ln_b, m_b_spatial),
                 small(v_q_norm_g, v_kv_norm_g, v_sgu_norm_g, v_sgu_norm_b, v_ln_g, v_ln_b, v_b_spatial))

    def ordered(big, sm):
        vec = lambda n: sm[n].reshape(-1)
        return [big[0], vec(0), big[1], vec(1), big[2], vec(2), vec(3), big[4], sm[6], big[3], vec(4), vec(5)]

    loss = r_vec[GV_LOSS, 0]
    return (loss, gx.reshape(1, seq, D_MODEL), *ordered(res[0:5], res[20:27]), *ordered(res[5:10], res[27:34]),
            *ordered(res[10:15], res[34:41]), *ordered(res[15:20], res[41:48]))
```

```python
import math

import jax
import jax.numpy as jnp
import numpy as np
from jax import lax
from jax.experimental import pallas as pl
from jax.experimental.pallas import tpu as pltpu

F32 = jnp.float32
BF16 = jnp.bfloat16

D_MODEL = 1024
Q_LORA = 256
KV_LORA = 128
HEADS = 8
NOPE = 64
ROPE = 32
VDIM = 64
MLA_W = HEADS * VDIM
GW = 512
CHUNK = 128
HP = 128
PAIRS = HEADS // 2
D_IN = 2464
D_INR = 2560
C_CKV = Q_LORA
C_KR = Q_LORA + KV_LORA
C_GATE = C_KR + HP
ROPE_THETA = 10000.0
DN_ALPHA = 2.0 ** 0.25
EPS = 1e-5
SCALE = 1.0 / math.sqrt(NOPE + ROPE)
SCALE_LOG2E = SCALE * 1.4426950408889634
INV_SQRT2 = 0.7071067811865476
INV_SQRT_2PI = 0.3989422804014327

ADAM_LR = 0.001
ADAM_B1 = 0.9
ADAM_B2 = 0.999
ADAM_EPS = 1e-08
ADAM_WD = 0.01
ADAM_STEP = 10

PV_QG, PV_KVG, PV_SG, PV_SB, PV_INVF, PV_M1, PV_M2, PV_LNG, PV_LNB = range(9)
PV_ROWS = 16
GV_QG, GV_KVG, GV_SG, GV_SB, GV_LNG, GV_LNB, GV_LOSS = range(7)
GV_BSP = 8
GV_ROWS = 16

MESH = pl.DeviceIdType.MESH

FWD_TILE = 512
POST_TILE = 512
BWD_TILE = 512
ATT_BLK = 512
ADAM_STEPS = 4
VMEM_LIMIT = 56 * 1024 * 1024


def _dot(a, b):
    return jnp.dot(a, b, preferred_element_type=F32)


def _dot_nt(a, b):
    return lax.dot_general(a, b, (((1,), (1,)), ((), ())), preferred_element_type=F32)


def _dot_tn(a, b):
    return lax.dot_general(a, b, (((0,), (0,)), ((), ())), preferred_element_type=F32)


def _sigmoid(z):
    return pl.reciprocal(1.0 + jnp.exp(-z), approx=True)


def _gelu_and_grad(x):
    cdf = 0.5 * (1.0 + lax.erf(x * INV_SQRT2))
    return x * cdf, cdf + x * (INV_SQRT_2PI * jnp.exp(-0.5 * x * x))


def _rms_stats(x):
    r = lax.rsqrt(jnp.mean(x * x, axis=-1, keepdims=True) + EPS)
    return x * r, r


def _rms_bwd(dy, g, xh, r):
    dyg = dy * g
    return r * (dyg - xh * jnp.mean(dyg * xh, axis=-1, keepdims=True))


def _ln_stats(x):
    mu = jnp.mean(x, axis=-1, keepdims=True)
    xc = x - mu
    r = lax.rsqrt(jnp.mean(xc * xc, axis=-1, keepdims=True) + EPS)
    return xc * r, r


def _ln_bwd(dy, g, xh, r):
    dxh = dy * g
    return r * (dxh - jnp.mean(dxh, axis=-1, keepdims=True) - xh * jnp.mean(dxh * xh, axis=-1, keepdims=True))


def _rope_fwd(t, c, s1, s2):
    return t * c + pltpu.roll(t, HP - 16, 1) * s1 + pltpu.roll(t, 16, 1) * s2


def _rope_bwd(d, c, s1, s2):
    return d * c + pltpu.roll(d * s1, 16, 1) + pltpu.roll(d * s2, HP - 16, 1)


def _lane_lt64(shape):
    return lax.broadcasted_iota(jnp.int32, shape, len(shape) - 1) < 64


def _spatial_mix(w_ref, src, dst_ref, rows):
    for c in range(rows // CHUNK):
        for p in range(PAIRS):
            blk = src[c * CHUNK:(c + 1) * CHUNK, p * HP:(p + 1) * HP]
            a = _dot(w_ref[2 * p], blk)
            b = _dot(w_ref[2 * p + 1], blk)
            dst_ref[c * CHUNK:(c + 1) * CHUNK, p * HP:(p + 1) * HP] = jnp.where(_lane_lt64(a.shape), a, b)


def _gmlp_fwd(u_pre, v_pre, zb, sg, sb, wt_ref, bsp_ref, sv_ref, rows):
    u, du = _gelu_and_grad(u_pre)
    gv, dgv = _gelu_and_grad(v_pre)
    xh, r = _ln_stats(gv)
    vln = (xh * sg + sb).astype(BF16)
    _spatial_mix(wt_ref, vln, sv_ref, rows)
    bias = bsp_ref[...]
    svb = sv_ref[...] + jnp.concatenate([bias] * (rows // CHUNK), axis=0)
    sig = _sigmoid(zb)
    return u, du, dgv, xh, r, vln, svb, sig


def _weight_gather(shards):
    n_arr = len(shards)

    def body(*refs):
        ins, outs = refs[0:n_arr], refs[n_arr:2 * n_arr]
        send_sems, recv_sems = refs[2 * n_arr:]
        x, y, c = lax.axis_index("x"), lax.axis_index("y"), lax.axis_index("c")
        j = 2 * x + y
        sib = (x, y, 1 - c)
        chips = [(1 - x, y), (x, 1 - y), (1 - x, 1 - y)]
        for n in range(n_arr):
            outs[n][j] = ins[n][...].astype(BF16)

        def half(n, blk, core):
            r = shards[n].shape[0] // 2
            return outs[n].at[blk, pl.ds(pl.multiple_of(core * r, 16), r), :]

        def copy(k, ref, to):
            return pltpu.make_async_remote_copy(
                src_ref=ref, dst_ref=ref, send_sem=send_sems.at[k], recv_sem=recv_sems.at[k],
                device_id=to, device_id_type=MESH)

        first = [copy(6 * n + kk, half(n, j, c), (px, py, c))
                 for n in range(n_arr) for kk, (px, py) in enumerate(chips)]
        for cp in first:
            cp.start()
        passed = []
        for n in range(n_arr):
            for kk, (px, py) in enumerate(chips):
                landed = half(n, 2 * px + py, c)
                copy(6 * n + kk, landed, (px, py, c)).wait_recv()
                passed.append(copy(6 * n + 3 + kk, landed, sib))
                passed[-1].start()
        for n in range(n_arr):
            for kk, (px, py) in enumerate(chips):
                copy(6 * n + 3 + kk, half(n, 2 * px + py, 1 - c), sib).wait_recv()
        for cp in first + passed:
            cp.wait_send()

    vmem = pl.BlockSpec(memory_space=pltpu.VMEM)
    return pl.pallas_call(
        body, name="weight_gather",
        out_shape=[jax.ShapeDtypeStruct((4,) + a.shape, BF16) for a in shards],
        in_specs=[vmem] * n_arr, out_specs=[vmem] * n_arr,
        scratch_shapes=[pltpu.SemaphoreType.DMA((6 * n_arr,)), pltpu.SemaphoreType.DMA((6 * n_arr,))],
        compiler_params=pltpu.CompilerParams(vmem_limit_bytes=VMEM_LIMIT),
    )(*shards)


def _fwd_pre(x, pos, win, wuq, wkv, pvec):
    seq = x.shape[0]
    t = FWD_TILE

    def body(x_ref, pos_ref, win_ref, wuq_ref, wkv_ref, pv_ref,
             cq_o, ckv_o, gate_o, q_o, k_o, v_o, vt_o, cs_o):
        xb = x_ref[...].astype(BF16)
        proj = _dot(xb, win_ref[:, 0:C_GATE])
        cq = proj[:, 0:C_CKV]
        ckv = proj[:, C_CKV:C_KR]
        kr = proj[:, C_KR:C_GATE]
        cq_o[...] = cq
        ckv_o[...] = ckv

        ang = pos_ref[...].astype(F32) * pv_ref[PV_INVF:PV_INVF + 1, 0:HP]
        cos = jnp.cos(ang)
        sin = jnp.sin(ang)
        cs_o[:, 0:HP] = cos
        cs_o[:, HP:2 * HP] = sin
        s1 = sin * pv_ref[PV_M1:PV_M1 + 1, 0:HP]
        s2 = sin * pv_ref[PV_M2:PV_M2 + 1, 0:HP]

        cqh, _ = _rms_stats(cq)
        q_all = _dot((cqh * pv_ref[PV_QG:PV_QG + 1, 0:Q_LORA]).astype(BF16), wuq_ref[...])
        ckvh, _ = _rms_stats(ckv)
        kv_all = _dot((ckvh * pv_ref[PV_KVG:PV_KVG + 1, 0:KV_LORA]).astype(BF16), wkv_ref[...])
        krr = _rope_fwd(kr, cos, s1, s2)
        for h in range(HEADS):
            sl = slice(h * HP, (h + 1) * HP)
            q_o[:, sl] = (_rope_fwd(q_all[:, sl], cos, s1, s2) * SCALE_LOG2E).astype(BF16)
            k_o[:, sl] = (kv_all[:, sl] + krr).astype(BF16)
        val = kv_all[:, HEADS * HP:].astype(BF16)
        v_o[...] = val
        vt_o[...] = val.T
        gate_o[...] = _dot(xb, win_ref[:, C_GATE:D_INR]).astype(BF16)

    tile = lambda w: pl.BlockSpec((t, w), lambda i: (i, 0))
    full = lambda a: pl.BlockSpec(a.shape, lambda i: (0,) * a.ndim)
    outs = [(Q_LORA, F32), (KV_LORA, F32), (2048, BF16), (HEADS * HP, BF16), (HEADS * HP, BF16), (MLA_W, BF16)]
    per_blk = ATT_BLK // t
    out_specs = [tile(w) for w, _ in outs]
    out_specs += [pl.BlockSpec((None, MLA_W, t), lambda i: (i // per_blk, 0, i % per_blk)), tile(2 * HP)]
    out_shape = [jax.ShapeDtypeStruct((seq, w), d) for w, d in outs]
    out_shape += [jax.ShapeDtypeStruct((seq // ATT_BLK, MLA_W, ATT_BLK), BF16), jax.ShapeDtypeStruct((seq, 2 * HP), F32)]
    return pl.pallas_call(
        body, name="fwd_pre", grid=(seq // t,),
        in_specs=[tile(D_MODEL), tile(1), full(win), full(wuq), full(wkv), full(pvec)],
        out_specs=out_specs, out_shape=out_shape,
        compiler_params=pltpu.CompilerParams(dimension_semantics=("arbitrary",), vmem_limit_bytes=VMEM_LIMIT),
    )(x, pos, win, wuq, wkv, pvec)


def _attn_fwd(q, k, vt):
    seq = q.shape[0]
    b = ATT_BLK
    nq = seq // b
    assert nq % 2 == 0
    n_off = nq * (nq - 1) // 2

    def body(q_ref, k_ref, vt_ref, o_o, lse_o, m_ref, l_ref, acc_ref, s_even, s_odd):
        m_ref[...] = jnp.full(m_ref.shape, -jnp.inf, F32)
        l_ref[...] = jnp.zeros(l_ref.shape, F32)
        acc_ref[...] = jnp.zeros(acc_ref.shape, F32)

        def scores(i, j, s_ref):
            qrows = pl.ds(pl.multiple_of(i * b, b), b)
            krows = pl.ds(pl.multiple_of(j * b, b), b)
            for a in range(2):
                s_ref[a] = _dot_nt(k_ref[krows, a * HP:(a + 1) * HP], q_ref[qrows, a * HP:(a + 1) * HP])

        def consume(i, j, s_ref, masked):
            vt_blk = vt_ref[j]
            for a in range(2):
                st = s_ref[a]
                if masked:
                    ki = lax.broadcasted_iota(jnp.int32, st.shape, 0)
                    qi = lax.broadcasted_iota(jnp.int32, st.shape, 1)
                    st = jnp.where(ki <= qi, st, -jnp.inf)
                m_prev = m_ref[i, a:a + 1, :]
                m_new = jnp.maximum(m_prev, jnp.max(st, axis=0, keepdims=True))
                alpha = jnp.exp2(m_prev - m_new)
                pt = jnp.exp2(st - m_new)
                l_ref[i, a:a + 1, :] = alpha * l_ref[i, a:a + 1, :] + jnp.sum(pt, axis=0, keepdims=True)
                acc_ref[i, a] = alpha * acc_ref[i, a] + _dot(vt_blk, pt.astype(BF16))
                m_ref[i, a:a + 1, :] = m_new

        def after(i, j):
            wrap = j + 1 >= i
            return jnp.minimum(jnp.where(wrap, i + 1, i), nq - 1), jnp.where(wrap, 0, j + 1)

        if n_off > 0:
            scores(1, 0, s_even)

            def below(u, ij):
                i1, j1 = after(*ij)
                scores(i1, j1, s_odd)
                consume(ij[0], ij[1], s_even, False)
                i2, j2 = after(i1, j1)
                scores(i2, j2, s_even)
                consume(i1, j1, s_odd, False)
                return i2, j2

            last = lax.fori_loop(0, n_off // 2, below, (jnp.int32(1), jnp.int32(0)))
            if n_off % 2:
                consume(last[0], last[1], s_even, False)

        scores(0, 0, s_even)

        def diagonal(u, carry):
            i0 = 2 * u
            scores(i0 + 1, i0 + 1, s_odd)
            consume(i0, i0, s_even, True)
            i2 = jnp.minimum(i0 + 2, nq - 1)
            scores(i2, i2, s_even)
            consume(i0 + 1, i0 + 1, s_odd, True)
            return carry

        lax.fori_loop(0, nq // 2, diagonal, 0)
        top = lax.broadcasted_iota(jnp.int32, (HP, b), 0) < 64

        def finish(i, carry):
            rows = pl.ds(pl.multiple_of(i * b, b), b)
            o_o[rows, :] = jnp.where(top, acc_ref[i, 0] / l_ref[i, 0:1, :], acc_ref[i, 1] / l_ref[i, 1:2, :]).T
            lse_o[i] = m_ref[i, 0:2, :] + jnp.log2(l_ref[i, 0:2, :])
            return carry

        lax.fori_loop(0, nq, finish, 0)

    return pl.pallas_call(
        body, name="attn_fwd", grid=(PAIRS,),
        in_specs=[pl.BlockSpec((seq, 2 * HP), lambda p: (0, p)),
                  pl.BlockSpec((seq, 2 * HP), lambda p: (0, p)),
                  pl.BlockSpec((nq, HP, b), lambda p: (0, p, 0))],
        out_specs=[pl.BlockSpec((seq, HP), lambda p: (0, p)),
                   pl.BlockSpec((None, nq, 2, b), lambda p: (p, 0, 0, 0))],
        out_shape=[jax.ShapeDtypeStruct((seq, MLA_W), F32),
                   jax.ShapeDtypeStruct((PAIRS, nq, 2, b), F32)],
        scratch_shapes=[pltpu.VMEM((nq, 8, b), F32), pltpu.VMEM((nq, 8, b), F32), pltpu.VMEM((nq, 2, HP, b), F32),
                        pltpu.VMEM((2, b, b), F32), pltpu.VMEM((2, b, b), F32)],
        compiler_params=pltpu.CompilerParams(dimension_semantics=("arbitrary",), vmem_limit_bytes=VMEM_LIMIT),
    )(q, k, vt)


def _post(x, tgt, o, gate, wout, pvec, wt, wtt, bsp):
    seq = x.shape[0]
    t = POST_TILE
    nt = seq // t

    def body(x_ref, tgt_ref, o_ref, gate_ref, wout_ref, pv_ref, wt_ref, wtt_ref, bsp_ref,
             dh2_o, do_o, dgate_o, gwout_o, gwsp_o, vec_o, sv_ref, dvln_ref, bacc_ref):
        i = pl.program_id(0)

        @pl.when(i == 0)
        def _():
            gwout_o[...] = jnp.zeros_like(gwout_o)
            gwsp_o[...] = jnp.zeros_like(gwsp_o)
            vec_o[...] = jnp.zeros_like(vec_o)
            bacc_ref[...] = jnp.zeros_like(bacc_ref)

        za = gate_ref[:, 0:512].astype(F32)
        u_pre = gate_ref[:, 512:1024].astype(F32)
        v_pre = gate_ref[:, 1024:1536].astype(F32)
        zb = gate_ref[:, 1536:2048].astype(F32)
        sg = pv_ref[PV_SG:PV_SG + 1, 0:GW]
        sb = pv_ref[PV_SB:PV_SB + 1, 0:GW]
        lng = pv_ref[PV_LNG:PV_LNG + 1, :]
        lnb = pv_ref[PV_LNB:PV_LNB + 1, :]
        o = o_ref[...]

        sig_a = _sigmoid(za)
        silu_a = za * sig_a
        u, du, dgv, xh, r, vln, svb, sig_b = _gmlp_fwd(u_pre, v_pre, zb, sg, sb, wt_ref, bsp_ref, sv_ref, t)
        silu_b = zb * sig_b
        sgu = u * svb
        merged = jnp.concatenate([o * silu_a, sgu * silu_b], axis=1).astype(BF16)
        h2 = DN_ALPHA * x_ref[...] + _dot(merged, wout_ref[...])
        xh2, r2 = _ln_stats(h2)
        err = xh2 * lng + lnb - tgt_ref[...]
        d_out = err * (1.0 / D_MODEL)
        vec_o[GV_LNG:GV_LNG + 1, :] += jnp.sum(d_out * xh2, axis=0, keepdims=True)
        vec_o[GV_LNB:GV_LNB + 1, :] += jnp.sum(d_out, axis=0, keepdims=True)
        vec_o[GV_LOSS:GV_LOSS + 1, :] += jnp.sum(err * err, axis=0, keepdims=True) * (0.5 / D_MODEL)

        d_h2 = _ln_bwd(d_out, lng, xh2, r2)
        dh2_o[...] = d_h2
        dh2b = d_h2.astype(BF16)
        gwout_o[...] += _dot_tn(merged, dh2b)
        d_m = _dot_nt(dh2b, wout_ref[...])
        d_oa = d_m[:, 0:512]
        d_ob = d_m[:, 512:1024]
        do_o[...] = (d_oa * silu_a).astype(BF16)
        dgate_o[:, 0:512] = (d_oa * o * (sig_a * (1.0 + za * (1.0 - sig_a)))).astype(BF16)
        dgate_o[:, 1536:2048] = (d_ob * sgu * (sig_b * (1.0 + zb * (1.0 - sig_b)))).astype(BF16)
        d_sgu = d_ob * silu_b
        dgate_o[:, 512:1024] = (d_sgu * svb * du).astype(BF16)
        d_sv = d_sgu * u
        acc = bacc_ref[...]
        for c in range(t // CHUNK):
            acc = acc + d_sv[c * CHUNK:(c + 1) * CHUNK, :]
        bacc_ref[...] = acc
        d_svb = d_sv.astype(BF16)
        for c in range(t // CHUNK):
            for p in range(PAIRS):
                blk = d_svb[c * CHUNK:(c + 1) * CHUNK, p * HP:(p + 1) * HP]
                vblk = vln[c * CHUNK:(c + 1) * CHUNK, p * HP:(p + 1) * HP]
                first = _lane_lt64(blk.shape)
                gwsp_o[2 * p] += _dot_nt(jnp.where(first, blk, jnp.zeros_like(blk)), vblk)
                gwsp_o[2 * p + 1] += _dot_nt(jnp.where(first, jnp.zeros_like(blk), blk), vblk)
        _spatial_mix(wtt_ref, d_svb, dvln_ref, t)
        d_vln = dvln_ref[...]
        vec_o[GV_SG:GV_SG + 1, 0:GW] += jnp.sum(d_vln * xh, axis=0, keepdims=True)
        vec_o[GV_SB:GV_SB + 1, 0:GW] += jnp.sum(d_vln, axis=0, keepdims=True)
        dgate_o[:, 1024:1536] = (_ln_bwd(d_vln, sg, xh, r) * dgv).astype(BF16)


        @pl.when(i == nt - 1)
        def _():
            tri = (lax.broadcasted_iota(jnp.int32, (CHUNK, CHUNK), 1)
                   <= lax.broadcasted_iota(jnp.int32, (CHUNK, CHUNK), 0))
            for h in range(HEADS):
                gwsp_o[h] = jnp.where(tri, gwsp_o[h], 0.0)
            lane = lax.broadcasted_iota(jnp.int32, (CHUNK, HP), 1)
            res = jnp.zeros((CHUNK, HP), F32)
            for h in range(HEADS):
                p, a = divmod(h, 2)
                blk = bacc_ref[:, p * HP:(p + 1) * HP]
                part = jnp.where(_lane_lt64(blk.shape) == (a == 0), blk, 0.0)
                res = jnp.where(lane == h, jnp.sum(part, axis=-1, keepdims=True), res)
            vec_o[GV_BSP:GV_BSP + HEADS, 0:HP] = res.T[0:HEADS, :]
            lane1 = lax.broadcasted_iota(jnp.int32, (1, D_MODEL), 1)
            total = jnp.sum(vec_o[GV_LOSS:GV_LOSS + 1, :], axis=-1, keepdims=True)
            vec_o[GV_LOSS:GV_LOSS + 1, :] = jnp.where(lane1 == 0, total, 0.0)

    tile = lambda w: pl.BlockSpec((t, w), lambda i: (i, 0))
    full = lambda a: pl.BlockSpec(a.shape, lambda i: (0,) * a.ndim)
    const = lambda s: pl.BlockSpec(s, lambda i: (0,) * len(s))
    return pl.pallas_call(
        body, name="post", grid=(nt,),
        in_specs=[tile(D_MODEL), tile(D_MODEL), tile(MLA_W), tile(2048), full(wout), full(pvec),
                  full(wt), full(wtt), full(bsp)],
        out_specs=[tile(D_MODEL), tile(MLA_W), tile(2048), const((D_MODEL, D_MODEL)),
                   const((HEADS, CHUNK, CHUNK)), const((GV_ROWS, D_MODEL))],
        out_shape=[jax.ShapeDtypeStruct((seq, D_MODEL), F32), jax.ShapeDtypeStruct((seq, MLA_W), BF16),
                   jax.ShapeDtypeStruct((seq, 2048), BF16), jax.ShapeDtypeStruct((D_MODEL, D_MODEL), F32),
                   jax.ShapeDtypeStruct((HEADS, CHUNK, CHUNK), F32), jax.ShapeDtypeStruct((GV_ROWS, D_MODEL), F32)],
        scratch_shapes=[pltpu.VMEM((t, GW), F32), pltpu.VMEM((t, GW), F32), pltpu.VMEM((CHUNK, GW), F32)],
        compiler_params=pltpu.CompilerParams(dimension_semantics=("arbitrary",), vmem_limit_bytes=VMEM_LIMIT),
    )(x, tgt, o, gate, wout, pvec, wt, wtt, bsp)


def _attn_bwd(q, k, v, do, o, lse, cs, pvec):
    seq = q.shape[0]
    b = ATT_BLK
    nq = seq // b

    def body(q_ref, k_ref, v_ref, do_ref, o_ref, lse_ref, cs_ref, pv_ref, dq_o, dk_o, dv_o, dk_acc, dv_acc):
        i = pl.program_id(1)

        @pl.when(i == 0)
        def _():
            dk_acc[...] = jnp.zeros_like(dk_acc)
            dv_acc[...] = jnp.zeros_like(dv_acc)

        first = _lane_lt64((b, HP))
        do = do_ref[...]
        zero = jnp.zeros_like(do)
        dos = [jnp.where(first, do, zero), jnp.where(first, zero, do)]
        prod_t = (do.astype(F32) * o_ref[...]).T
        deltas = [jnp.sum(prod_t[0:64, :], axis=0, keepdims=True),
                  jnp.sum(prod_t[64:128, :], axis=0, keepdims=True)]
        lses = [lse_ref[0:1, :], lse_ref[1:2, :]]
        qs = [q_ref[:, a * HP:(a + 1) * HP] for a in range(2)]

        def step(j, dqs, masked):
            rows = pl.ds(pl.multiple_of(j * b, b), b)
            vb = v_ref[rows, :]
            new_dq = []
            dvs = []
            for a in range(2):
                kb = k_ref[rows, a * HP:(a + 1) * HP]
                pt = jnp.exp2(_dot_nt(kb, qs[a]) - lses[a])
                if masked:
                    ki = lax.broadcasted_iota(jnp.int32, pt.shape, 0)
                    qi = lax.broadcasted_iota(jnp.int32, pt.shape, 1)
                    pt = jnp.where(ki <= qi, pt, 0.0)
                dvs.append(_dot(pt.astype(BF16), do))
                dpt = _dot_nt(vb, dos[a])
                dst = (pt * (dpt - deltas[a])).astype(BF16)
                dk_acc[rows, a * HP:(a + 1) * HP] += _dot(dst, qs[a])
                new_dq.append(dqs[a] + _dot_tn(dst, kb))
            dv_acc[rows, :] += jnp.where(first, dvs[0], dvs[1])
            return tuple(new_dq)

        init = (jnp.zeros((b, HP), F32), jnp.zeros((b, HP), F32))
        dqs = lax.fori_loop(0, i, lambda j, cr: step(j, cr, False), init)
        dqs = step(i, dqs, True)
        cos = cs_ref[:, 0:HP]
        sin = cs_ref[:, HP:2 * HP]
        s1 = sin * pv_ref[PV_M1:PV_M1 + 1, 0:HP]
        s2 = sin * pv_ref[PV_M2:PV_M2 + 1, 0:HP]
        for a in range(2):
            dq_o[:, a * HP:(a + 1) * HP] = _rope_bwd(dqs[a] * SCALE, cos, s1, s2).astype(BF16)

        @pl.when(i == nq - 1)
        def _():
            dk_o[...] = (dk_acc[...] * (SCALE / SCALE_LOG2E)).astype(BF16)
            dv_o[...] = dv_acc[...].astype(BF16)

    return pl.pallas_call(
        body, name="attn_bwd", grid=(PAIRS, nq),
        in_specs=[pl.BlockSpec((b, 2 * HP), lambda p, i: (i, p)),
                  pl.BlockSpec((seq, 2 * HP), lambda p, i: (0, p)),
                  pl.BlockSpec((seq, HP), lambda p, i: (0, p)),
                  pl.BlockSpec((b, HP), lambda p, i: (i, p)),
                  pl.BlockSpec((b, HP), lambda p, i: (i, p)),
                  pl.BlockSpec((None, None, 2, b), lambda p, i: (p, i, 0, 0)),
                  pl.BlockSpec((b, 2 * HP), lambda p, i: (i, 0)),
                  pl.BlockSpec(pvec.shape, lambda p, i: (0, 0))],
        out_specs=[pl.BlockSpec((b, 2 * HP), lambda p, i: (i, p)),
                   pl.BlockSpec((seq, 2 * HP), lambda p, i: (0, p)),
                   pl.BlockSpec((seq, HP), lambda p, i: (0, p))],
        out_shape=[jax.ShapeDtypeStruct((seq, HEADS * HP), BF16),
                   jax.ShapeDtypeStruct((seq, HEADS * HP), BF16),
                   jax.ShapeDtypeStruct((seq, MLA_W), BF16)],
        scratch_shapes=[pltpu.VMEM((seq, 2 * HP), F32), pltpu.VMEM((seq, HP), F32)],
        compiler_params=pltpu.CompilerParams(dimension_semantics=("arbitrary", "arbitrary"),
                                             vmem_limit_bytes=VMEM_LIMIT),
    )(q, k, v, do, o, lse, cs, pvec)


def _bwd_pre(x, dh2, cq, ckv, cs, dq, dk, dv, dgate, win, wuq, wkv, pvec, gvec):
    seq = x.shape[0]
    t = BWD_TILE

    def body(x_ref, dh2_ref, cq_ref, ckv_ref, cs_ref, dq_ref, dk_ref, dv_ref, dgate_ref,
             win_ref, wuq_ref, wkv_ref, pv_ref, gv_ref, gx_o, gwin_o, gwuq_o, gwkv_o, vec_o):
        i = pl.program_id(0)

        @pl.when(i == 0)
        def _():
            gwin_o[...] = jnp.zeros_like(gwin_o)
            gwuq_o[...] = jnp.zeros_like(gwuq_o)
            gwkv_o[...] = jnp.zeros_like(gwkv_o)
            vec_o[...] = gv_ref[...]

        xb = x_ref[...].astype(BF16)
        dgate = dgate_ref[...]
        gwin_o[:, C_GATE:D_INR] += _dot_tn(xb, dgate)
        gx_gate = _dot_nt(dgate, win_ref[:, C_GATE:D_INR])

        qg = pv_ref[PV_QG:PV_QG + 1, 0:Q_LORA]
        kvg = pv_ref[PV_KVG:PV_KVG + 1, 0:KV_LORA]
        dq = dq_ref[...]
        cqh, rq = _rms_stats(cq_ref[...])
        d_cqn = _dot_nt(dq, wuq_ref[...])
        gwuq_o[...] += _dot_tn((cqh * qg).astype(BF16), dq)
        vec_o[GV_QG:GV_QG + 1, 0:Q_LORA] += jnp.sum(d_cqn * cqh, axis=0, keepdims=True)
        d_cq = _rms_bwd(d_cqn, qg, cqh, rq)

        dk = dk_ref[...]
        dkv = jnp.concatenate([dk, dv_ref[...]], axis=1)
        ckvh, rkv = _rms_stats(ckv_ref[...])
        d_ckvn = _dot_nt(dkv, wkv_ref[...])
        gwkv_o[...] += _dot_tn((ckvh * kvg).astype(BF16), dkv)
        vec_o[GV_KVG:GV_KVG + 1, 0:KV_LORA] += jnp.sum(d_ckvn * ckvh, axis=0, keepdims=True)
        d_ckv = _rms_bwd(d_ckvn, kvg, ckvh, rkv)

        dks = dk[:, 0:HP].astype(F32)
        for h in range(1, HEADS):
            dks = dks + dk[:, h * HP:(h + 1) * HP].astype(F32)
        cos = cs_ref[:, 0:HP]
        sin = cs_ref[:, HP:2 * HP]
        d_kr = _rope_bwd(dks, cos, sin * pv_ref[PV_M1:PV_M1 + 1, 0:HP], sin * pv_ref[PV_M2:PV_M2 + 1, 0:HP])

        d_lat = jnp.concatenate([d_cq.astype(BF16), d_ckv.astype(BF16), d_kr.astype(BF16)], axis=1)
        gwin_o[:, 0:C_GATE] += _dot_tn(xb, d_lat)
        gx_o[...] = DN_ALPHA * dh2_ref[...] + gx_gate + _dot_nt(d_lat, win_ref[:, 0:C_GATE])

    tile = lambda w: pl.BlockSpec((t, w), lambda i: (i, 0))
    full = lambda a: pl.BlockSpec(a.shape, lambda i: (0,) * a.ndim)
    const = lambda s: pl.BlockSpec(s, lambda i: (0,) * len(s))
    return pl.pallas_call(
        body, name="bwd_pre", grid=(seq // t,),
        in_specs=[tile(D_MODEL), tile(D_MODEL), tile(Q_LORA), tile(KV_LORA), tile(2 * HP), tile(HEADS * HP),
                  tile(HEADS * HP), tile(MLA_W), tile(2048), full(win), full(wuq), full(wkv), full(pvec), full(gvec)],
        out_specs=[tile(D_MODEL), const((D_MODEL, D_INR)), const((Q_LORA, HEADS * HP)),
                   const((KV_LORA, HEADS * HP + MLA_W)), const((GV_ROWS, D_MODEL))],
        out_shape=[jax.ShapeDtypeStruct((seq, D_MODEL), F32), jax.ShapeDtypeStruct((D_MODEL, D_INR), F32),
                   jax.ShapeDtypeStruct((Q_LORA, HEADS * HP), F32),
                   jax.ShapeDtypeStruct((KV_LORA, HEADS * HP + MLA_W), F32),
                   jax.ShapeDtypeStruct((GV_ROWS, D_MODEL), F32)],
        compiler_params=pltpu.CompilerParams(dimension_semantics=("arbitrary",), vmem_limit_bytes=VMEM_LIMIT),
    )(x, dh2, cq, ckv, cs, dq, dk, dv, dgate, win, wuq, wkv, pvec, gvec)


def _grad_reduce(gs, gvec):
    n_arr = len(gs)
    n_big = n_arr - 1
    k1 = lambda n, blk: 4 * n + blk
    k2 = lambda n, kk: 4 * n_arr + 3 * n + kk
    k3 = lambda n: 7 * n_arr + n
    k3w = lambda k: 7 * n_arr + n_big + k
    kv = lambda k: 7 * n_arr + n_big + 7 + k
    n_sem = 7 * n_arr + n_big + 14

    def body(*refs):
        g, gv = refs[0:n_arr], refs[n_arr]
        outs, ov = refs[n_arr + 1:2 * n_arr + 1], refs[2 * n_arr + 1]
        r1 = refs[2 * n_arr + 2:3 * n_arr + 2]
        r2 = refs[3 * n_arr + 2:4 * n_arr + 2]
        s2 = refs[4 * n_arr + 2:5 * n_arr + 2]
        vbuf, send_sems, recv_sems = refs[5 * n_arr + 2:]
        x, y, c = lax.axis_index("x"), lax.axis_index("y"), lax.axis_index("c")
        j = 2 * x + y
        me = 2 * j + c
        sib = (x, y, 1 - c)
        chips = [(1 - x, y), (x, 1 - y), (1 - x, 1 - y)]
        others = [sib] + [(px, py, pc) for (px, py) in chips for pc in (c, 1 - c)]

        def copy(k, src, dst, to):
            return pltpu.make_async_remote_copy(
                src_ref=src, dst_ref=dst, send_sem=send_sems.at[k], recv_sem=recv_sems.at[k],
                device_id=to, device_id_type=MESH)

        l1 = [copy(k1(n, blk), g[n].at[blk, 1 - c], r1[n].at[blk], sib) for n in range(n_arr) for blk in range(4)]
        lv = [copy(kv(k), gv, vbuf.at[me], to) for k, to in enumerate(others)]
        for cp in l1 + lv:
            cp.start()
        l2 = []
        for n in range(n_arr):
            for blk in range(4):
                copy(k1(n, blk), g[n].at[blk, c], r1[n].at[blk], sib).wait_recv()
            for blk in range(4):
                r1[n][blk] = g[n][blk, c] + r1[n][blk]
                s2[n][blk] = r1[n][blk].astype(BF16)
            for kk, (px, py) in enumerate(chips):
                l2.append(copy(k2(n, kk), s2[n].at[2 * px + py], r2[n].at[kk], (px, py, c)))
                l2[-1].start()

        l3 = []
        for n in range(n_arr):
            for kk in range(3):
                copy(k2(n, kk), s2[n].at[0], r2[n].at[kk], sib).wait_recv()
            red = ((r1[n][j] + r2[n][0].astype(F32)) + r2[n][1].astype(F32)) + r2[n][2].astype(F32)
            if n < n_big:
                outs[n][c] = red
                back = [copy(k3(n), outs[n].at[c], outs[n].at[c], sib)]
            else:
                outs[n][j, c] = red
                back = [copy(k3w(k), outs[n].at[j, c], outs[n].at[j, c], to) for k, to in enumerate(others)]
            for cp in back:
                cp.start()
            l3 += back
        for n in range(n_big):
            copy(k3(n), outs[n].at[1 - c], outs[n].at[1 - c], sib).wait_recv()
        for k, (px, py, pc) in enumerate(others):
            landed = outs[n_big].at[2 * px + py, pc]
            copy(k3w(k), landed, landed, (px, py, pc)).wait_recv()
            copy(kv(k), gv, vbuf.at[4 * px + 2 * py + pc], (px, py, pc)).wait_recv()
        vbuf[me] = gv[...]
        total = vbuf[0]
        for d in range(1, 8):
            total = total + vbuf[d]
        ov[...] = total
        for cp in l1 + lv + l2 + l3:
            cp.wait_send()

    vmem = pl.BlockSpec(memory_space=pltpu.VMEM)
    half_shapes = [a.shape[2:] for a in gs]
    out_shape = [jax.ShapeDtypeStruct((2,) + s, F32) for s in half_shapes[:n_big]]
    out_shape += [jax.ShapeDtypeStruct((4, 2) + half_shapes[n_big], F32), jax.ShapeDtypeStruct(gvec.shape, F32)]
    scratch = [pltpu.VMEM((4,) + s, F32) for s in half_shapes] + [pltpu.VMEM((3,) + s, BF16) for s in half_shapes]
    scratch += [pltpu.VMEM((4,) + s, BF16) for s in half_shapes]
    scratch += [pltpu.VMEM((8,) + gvec.shape, F32), pltpu.SemaphoreType.DMA((n_sem,)), pltpu.SemaphoreType.DMA((n_sem,))]
    return pl.pallas_call(
        body, name="grad_reduce", out_shape=out_shape,
        in_specs=[vmem] * (n_arr + 1), out_specs=[vmem] * (n_arr + 1), scratch_shapes=scratch,
        compiler_params=pltpu.CompilerParams(vmem_limit_bytes=VMEM_LIMIT),
    )(*gs, gvec)


SMALL_ROWS = ((GV_QG, 1, Q_LORA), (GV_KVG, 1, KV_LORA), (GV_SG, 1, GW), (GV_SB, 1, GW),
              (GV_LNG, 1, D_MODEL), (GV_LNB, 1, D_MODEL), (GV_BSP, HEADS, CHUNK))


def _adam_update(g, w, m, v):
    m_new = ADAM_B1 * m + (1.0 - ADAM_B1) * g
    v_new = ADAM_B2 * v + (1.0 - ADAM_B2) * (g * g)
    m_hat = m_new / (1.0 - ADAM_B1 ** ADAM_STEP)
    v_hat = v_new / (1.0 - ADAM_B2 ** ADAM_STEP)
    return -ADAM_LR * (m_hat / (jnp.sqrt(v_hat) + ADAM_EPS) + ADAM_WD * w), m_new, v_new


def _adamw(g_big, w_big, m_big, v_big, gvec, w_small, m_small, v_small):
    nb, ns = len(g_big), len(w_small)

    def body(*refs):
        it = iter(refs)
        take = lambda n: [next(it) for _ in range(n)]
        g_b, w_b, m_b, v_b = take(nb), take(nb), take(nb), take(nb)
        gv = next(it)
        w_s, m_s, v_s = take(ns), take(ns), take(ns)
        g_bo, d_bo, m_bo, v_bo = take(nb), take(nb), take(nb), take(nb)
        g_so, d_so, m_so, v_so = take(ns), take(ns), take(ns), take(ns)
        for n in range(nb):
            gb = g_b[n][...]
            g_bo[n][...] = gb
            d_bo[n][...], m_bo[n][...], v_bo[n][...] = _adam_update(gb, w_b[n][...], m_b[n][...], v_b[n][...])
        for n, (row, nrow, width) in enumerate(SMALL_ROWS):
            gs = gv[row:row + nrow, 0:width]
            g_so[n][...] = gs
            d_so[n][...], m_so[n][...], v_so[n][...] = _adam_update(gs, w_s[n][...], m_s[n][...], v_s[n][...])

    def rows(a):
        nd = a.ndim
        return pl.BlockSpec((a.shape[0] // ADAM_STEPS,) + a.shape[1:], lambda i: (i,) + (0,) * (nd - 1))

    def whole(a):
        nd = a.ndim
        return pl.BlockSpec(a.shape, lambda i: (0,) * nd)

    big = [jax.ShapeDtypeStruct(a.shape, F32) for a in w_big]
    small = [jax.ShapeDtypeStruct(a.shape, F32) for a in w_small]
    return pl.pallas_call(
        body, name="adamw", grid=(ADAM_STEPS,), out_shape=big * 4 + small * 4,
        in_specs=[rows(a) for a in g_big + w_big + m_big + v_big] + [whole(gvec)]
        + [whole(a) for a in w_small + m_small + v_small],
        out_specs=[rows(a) for a in w_big] * 4 + [whole(a) for a in w_small] * 4,
        compiler_params=pltpu.CompilerParams(dimension_semantics=("arbitrary",), vmem_limit_bytes=VMEM_LIMIT),
    )(*g_big, *w_big, *m_big, *v_big, gvec, *w_small, *m_small, *v_small)


def kernel(x, positions, w_in, q_norm_g, w_uq, kv_norm_g, w_ukv, sgu_norm_g, sgu_norm_b, w_spatial, b_spatial, w_out, ln_g, ln_b, loss_target, m_w_in, m_q_norm_g, m_w_uq, m_kv_norm_g, m_w_ukv, m_sgu_norm_g, m_sgu_norm_b, m_w_spatial, m_b_spatial, m_w_out, m_ln_g, m_ln_b, v_w_in, v_q_norm_g, v_w_uq, v_kv_norm_g, v_w_ukv, v_sgu_norm_g, v_sgu_norm_b, v_w_spatial, v_b_spatial, v_w_out, v_ln_g, v_ln_b):
    seq = x.shape[1]
    x2 = x.reshape(seq, D_MODEL)
    tgt = loss_target.reshape(seq, D_MODEL)
    pos = positions.reshape(seq, 1)

    a_in, a_uq, a_ukv, a_out = _weight_gather([w_in, w_uq, w_ukv, w_out])
    w_uq_f = jnp.swapaxes(a_uq, 0, 1).reshape(Q_LORA, HEADS * (NOPE + ROPE))
    w_ukv_f = jnp.swapaxes(a_ukv, 0, 1).reshape(KV_LORA, HEADS * (NOPE + VDIM))
    wout = a_out.reshape(D_MODEL, D_MODEL)
    zc = lambda n: jnp.zeros((D_MODEL, n), BF16)
    win = jnp.concatenate([a_in[0][:, 0:C_KR], zc(NOPE), a_in[0][:, C_KR:C_KR + ROPE], zc(HP - NOPE - ROPE),
                           a_in[0][:, C_KR + ROPE:],
                           a_in[1], a_in[2], a_in[3]], axis=1)
    wuq = jnp.pad(w_uq_f.reshape(Q_LORA, HEADS, NOPE + ROPE), ((0, 0), (0, 0), (0, HP - NOPE - ROPE)))
    wuq = wuq.reshape(Q_LORA, HEADS * HP)
    ukv = w_ukv_f.reshape(KV_LORA, HEADS, NOPE + VDIM)
    wk = jnp.pad(ukv[:, :, 0:NOPE], ((0, 0), (0, 0), (0, HP - NOPE))).reshape(KV_LORA, HEADS * HP)
    wkv = jnp.concatenate([wk, ukv[:, :, NOPE:].reshape(KV_LORA, MLA_W)], axis=1)

    lane = np.arange(HP)
    half = ROPE // 2
    inv_freq = (1.0 / (ROPE_THETA ** (np.arange(half, dtype=np.float32) / half))).astype(np.float32)
    in_rope = (lane >= NOPE) & (lane < NOPE + ROPE)
    invf = jnp.asarray(np.where(in_rope, inv_freq[(lane - NOPE) % half], 0.0).astype(np.float32))
    m1 = jnp.asarray(np.where((lane >= NOPE) & (lane < NOPE + half), -1.0, 0.0).astype(np.float32))
    m2 = jnp.asarray(np.where((lane >= NOPE + half) & (lane < NOPE + ROPE), 1.0, 0.0).astype(np.float32))
    row = lambda a: jnp.pad(a.astype(F32), (0, D_MODEL - a.shape[0]))
    pvec = jnp.stack([row(q_norm_g), row(kv_norm_g), row(sgu_norm_g), row(sgu_norm_b), row(invf), row(m1),
                      row(m2), row(ln_g), row(ln_b)] + [jnp.zeros((D_MODEL,), F32)] * (PV_ROWS - 9))
    tri = jnp.tril(jnp.ones((CHUNK, CHUNK), dtype=bool))
    wt = jnp.where(tri[None], w_spatial, 0.0).astype(BF16)
    wtt = jnp.swapaxes(wt, 1, 2)
    bsp = jnp.repeat(b_spatial.T, VDIM, axis=1)

    cq, ckv, gate, q, k, v, vt, cs = _fwd_pre(x2, pos, win, wuq, wkv, pvec)
    o, lse = _attn_fwd(q, k, vt)
    dh2, do, dgate, g_wout, g_wsp, gvec = _post(x2, tgt, o, gate, wout, pvec, wt, wtt, bsp)
    dq, dk, dv = _attn_bwd(q, k, v, do, o, lse, cs, pvec)
    gx, g_win, g_wuq, g_wkv, gvec = _bwd_pre(x2, dh2, cq, ckv, cs, dq, dk, dv, dgate, win, wuq, wkv, pvec, gvec)

    cw = w_in.shape[1]
    first = D_INR - 3 * cw
    g_win_0 = jnp.concatenate([g_win[:, 0:C_KR], g_win[:, C_KR + NOPE:C_KR + NOPE + ROPE], g_win[:, C_GATE:first]],
                              axis=1)
    g_win_b = jnp.stack([g_win_0] + [g_win[:, first + cw * jb:first + cw * (jb + 1)] for jb in range(3)])
    g_wuq_f = g_wuq.reshape(Q_LORA, HEADS, HP)[:, :, 0:NOPE + ROPE].reshape(Q_LORA, HEADS * (NOPE + ROPE))
    g_k = g_wkv[:, 0:HEADS * HP].reshape(KV_LORA, HEADS, HP)[:, :, 0:NOPE]
    g_v = g_wkv[:, HEADS * HP:].reshape(KV_LORA, HEADS, VDIM)
    g_wukv_f = jnp.concatenate([g_k, g_v], axis=2).reshape(KV_LORA, HEADS * (NOPE + VDIM))

    def by_chip(a):
        rows, cols = a.shape[0], a.shape[1] // 4
        return jnp.swapaxes(a.reshape(rows, 4, cols), 0, 1).reshape(4, 2, rows // 2, cols)

    gs = [g_win_b.reshape(4, 2, D_MODEL // 2, cw), by_chip(g_wuq_f), by_chip(g_wukv_f), g_wout.reshape(4, 2, 128, D_MODEL),
          g_wsp.reshape(4, 2, CHUNK, CHUNK)]
    r_in, r_uq, r_ukv, r_out, r_wsp, r_vec = _grad_reduce(gs, gvec)

    g_big = [r_in.reshape(w_in.shape), r_uq.reshape(w_uq.shape), r_ukv.reshape(w_ukv.shape),
             r_out.reshape(w_out.shape), r_wsp.reshape(w_spatial.shape)]
    small = lambda qg, kvg, sg, sb, lng, lnb, bs: [qg.reshape(1, -1), kvg.reshape(1, -1), sg.reshape(1, -1),
                                                   sb.reshape(1, -1), lng.reshape(1, -1), lnb.reshape(1, -1), bs]
    res = _adamw(g_big, [w_in, w_uq, w_ukv, w_out, w_spatial], [m_w_in, m_w_uq, m_w_ukv, m_w_out, m_w_spatial],
                 [v_w_in, v_w_uq, v_w_ukv, v_w_out, v_w_spatial], r_vec,
                 small(q_norm_g, kv_norm_g, sgu_norm_g, sgu_norm_b, ln_g, ln_b, b_spatial),
                 small(m_q_norm_g, m_kv_norm_g, m_sgu_norm_g, m_sgu_norm_b, m_ln_g, m_ln_b, m_b_spatial),
                 small(v_q_norm_g, v_kv_norm_g, v_sgu_norm_g, v_sgu_norm_b, v_ln_g, v_ln_b, v_b_spatial))

    def ordered(big, sm):
        vec = lambda n: sm[n].reshape(-1)
        return [big[0], vec(0), big[1], vec(1), big[2], vec(2), vec(3), big[4], sm[6], big[3], vec(4), vec(5)]

    loss = r_vec[GV_LOSS, 0]
    return (loss, gx.reshape(1, seq, D_MODEL), *ordered(res[0:5], res[20:27]), *ordered(res[5:10], res[27:34]),
            *ordered(res[10:15], res[34:41]), *ordered(res[15:20], res[41:48]))
```

```python
import math

import jax
import jax.numpy as jnp
import numpy as np
from jax import lax
from jax.experimental import pallas as pl
from jax.experimental.pallas import tpu as pltpu

F32 = jnp.float32
BF16 = jnp.bfloat16

D_MODEL = 1024
Q_LORA = 256
KV_LORA = 128
HEADS = 8
NOPE = 64
ROPE = 32
VDIM = 64
MLA_W = HEADS * VDIM
GW = 512
CHUNK = 128
HP = 128
PAIRS = HEADS // 2
D_IN = 2464
D_INR = 2560
C_CKV = Q_LORA
C_KR = Q_LORA + KV_LORA
C_GATE = C_KR + HP
ROPE_THETA = 10000.0
DN_ALPHA = 2.0 ** 0.25
EPS = 1e-5
SCALE = 1.0 / math.sqrt(NOPE + ROPE)
SCALE_LOG2E = SCALE * 1.4426950408889634
INV_SQRT2 = 0.7071067811865476
INV_SQRT_2PI = 0.3989422804014327

ADAM_LR = 0.001
ADAM_B1 = 0.9
ADAM_B2 = 0.999
ADAM_EPS = 1e-08
ADAM_WD = 0.01
ADAM_STEP = 10

PV_QG, PV_KVG, PV_SG, PV_SB, PV_INVF, PV_M1, PV_M2, PV_LNG, PV_LNB = range(9)
PV_ROWS = 16
GV_QG, GV_KVG, GV_SG, GV_SB, GV_LNG, GV_LNB, GV_LOSS = range(7)
GV_BSP = 8
GV_ROWS = 16

MESH = pl.DeviceIdType.MESH

FWD_TILE = 512
POST_TILE = 512
BWD_TILE = 512
ATT_BLK = 512
ADAM_STEPS = 4
VMEM_LIMIT = 56 * 1024 * 1024


def _dot(a, b):
    return jnp.dot(a, b, preferred_element_type=F32)


def _dot_nt(a, b):
    return lax.dot_general(a, b, (((1,), (1,)), ((), ())), preferred_element_type=F32)


def _dot_tn(a, b):
    return lax.dot_general(a, b, (((0,), (0,)), ((), ())), preferred_element_type=F32)


def _sigmoid(z):
    return pl.reciprocal(1.0 + jnp.exp(-z), approx=True)


def _gelu_and_grad(x):
    cdf = 0.5 * (1.0 + lax.erf(x * INV_SQRT2))
    return x * cdf, cdf + x * (INV_SQRT_2PI * jnp.exp(-0.5 * x * x))


def _rms_stats(x):
    r = lax.rsqrt(jnp.mean(x * x, axis=-1, keepdims=True) + EPS)
    return x * r, r


def _rms_bwd(dy, g, xh, r):
    dyg = dy * g
    return r * (dyg - xh * jnp.mean(dyg * xh, axis=-1, keepdims=True))


def _ln_stats(x):
    mu = jnp.mean(x, axis=-1, keepdims=True)
    xc = x - mu
    r = lax.rsqrt(jnp.mean(xc * xc, axis=-1, keepdims=True) + EPS)
    return xc * r, r


def _ln_bwd(dy, g, xh, r):
    dxh = dy * g
    return r * (dxh - jnp.mean(dxh, axis=-1, keepdims=True) - xh * jnp.mean(dxh * xh, axis=-1, keepdims=True))


def _rope_fwd(t, c, s1, s2):
    return t * c + pltpu.roll(t, HP - 16, 1) * s1 + pltpu.roll(t, 16, 1) * s2


def _rope_bwd(d, c, s1, s2):
    return d * c + pltpu.roll(d * s1, 16, 1) + pltpu.roll(d * s2, HP - 16, 1)


def _lane_lt64(shape):
    return lax.broadcasted_iota(jnp.int32, shape, len(shape) - 1) < 64


def _spatial_mix(w_ref, src, dst_ref, rows):
    for c in range(rows // CHUNK):
        for p in range(PAIRS):
            blk = src[c * CHUNK:(c + 1) * CHUNK, p * HP:(p + 1) * HP]
            a = _dot(w_ref[2 * p], blk)
            b = _dot(w_ref[2 * p + 1], blk)
            dst_ref[c * CHUNK:(c + 1) * CHUNK, p * HP:(p + 1) * HP] = jnp.where(_lane_lt64(a.shape), a, b)


def _gmlp_fwd(u_pre, v_pre, zb, sg, sb, wt_ref, bsp_ref, sv_ref, rows):
    u, du = _gelu_and_grad(u_pre)
    gv, dgv = _gelu_and_grad(v_pre)
    xh, r = _ln_stats(gv)
    vln = (xh * sg + sb).astype(BF16)
    _spatial_mix(wt_ref, vln, sv_ref, rows)
    bias = bsp_ref[...]
    svb = sv_ref[...] + jnp.concatenate([bias] * (rows // CHUNK), axis=0)
    sig = _sigmoid(zb)
    return u, du, dgv, xh, r, vln, svb, sig


def _weight_gather(shards):
    n_arr = len(shards)

    def body(*refs):
        ins, outs = refs[0:n_arr], refs[n_arr:2 * n_arr]
        send_sems, recv_sems = refs[2 * n_arr:]
        x, y, c = lax.axis_index("x"), lax.axis_index("y"), lax.axis_index("c")
        j = 2 * x + y
        sib = (x, y, 1 - c)
        chips = [(1 - x, y), (x, 1 - y), (1 - x, 1 - y)]
        for n in range(n_arr):
            outs[n][j] = ins[n][...].astype(BF16)

        def half(n, blk, core):
            r = shards[n].shape[0] // 2
            return outs[n].at[blk, pl.ds(pl.multiple_of(core * r, 16), r), :]

        def copy(k, ref, to):
            return pltpu.make_async_remote_copy(
                src_ref=ref, dst_ref=ref, send_sem=send_sems.at[k], recv_sem=recv_sems.at[k],
                device_id=to, device_id_type=MESH)

        first = [copy(6 * n + kk, half(n, j, c), (px, py, c))
                 for n in range(n_arr) for kk, (px, py) in enumerate(chips)]
        for cp in first:
            cp.start()
        passed = []
        for n in range(n_arr):
            for kk, (px, py) in enumerate(chips):
                landed = half(n, 2 * px + py, c)
                copy(6 * n + kk, landed, (px, py, c)).wait_recv()
                passed.append(copy(6 * n + 3 + kk, landed, sib))
                passed[-1].start()
        for n in range(n_arr):
            for kk, (px, py) in enumerate(chips):
                copy(6 * n + 3 + kk, half(n, 2 * px + py, 1 - c), sib).wait_recv()
        for cp in first + passed:
            cp.wait_send()

    vmem = pl.BlockSpec(memory_space=pltpu.VMEM)
    return pl.pallas_call(
        body, name="weight_gather",
        out_shape=[jax.ShapeDtypeStruct((4,) + a.shape, BF16) for a in shards],
        in_specs=[vmem] * n_arr, out_specs=[vmem] * n_arr,
        scratch_shapes=[pltpu.SemaphoreType.DMA((6 * n_arr,)), pltpu.SemaphoreType.DMA((6 * n_arr,))],
        compiler_params=pltpu.CompilerParams(vmem_limit_bytes=VMEM_LIMIT),
    )(*shards)


def _fwd_pre(x, pos, win, wuq, wkv, pvec):
    seq = x.shape[0]
    t = FWD_TILE

    def body(x_ref, pos_ref, win_ref, wuq_ref, wkv_ref, pv_ref,
             cq_o, ckv_o, gate_o, q_o, k_o, v_o, vt_o, cs_o):
        xb = x_ref[...].astype(BF16)
        proj = _dot(xb, win_ref[:, 0:C_GATE])
        cq = proj[:, 0:C_CKV]
        ckv = proj[:, C_CKV:C_KR]
        kr = proj[:, C_KR:C_GATE]
        cq_o[...] = cq
        ckv_o[...] = ckv

        ang = pos_ref[...].astype(F32) * pv_ref[PV_INVF:PV_INVF + 1, 0:HP]
        cos = jnp.cos(ang)
        sin = jnp.sin(ang)
        cs_o[:, 0:HP] = cos
        cs_o[:, HP:2 * HP] = sin
        s1 = sin * pv_ref[PV_M1:PV_M1 + 1, 0:HP]
        s2 = sin * pv_ref[PV_M2:PV_M2 + 1, 0:HP]

        cqh, _ = _rms_stats(cq)
        q_all = _dot((cqh * pv_ref[PV_QG:PV_QG + 1, 0:Q_LORA]).astype(BF16), wuq_ref[...])
        ckvh, _ = _rms_stats(ckv)
        kv_all = _dot((ckvh * pv_ref[PV_KVG:PV_KVG + 1, 0:KV_LORA]).astype(BF16), wkv_ref[...])
        krr = _rope_fwd(kr, cos, s1, s2)
        for h in range(HEADS):
            sl = slice(h * HP, (h + 1) * HP)
            q_o[:, sl] = (_rope_fwd(q_all[:, sl], cos, s1, s2) * SCALE_LOG2E).astype(BF16)
            k_o[:, sl] = (kv_all[:, sl] + krr).astype(BF16)
        val = kv_all[:, HEADS * HP:].astype(BF16)
        v_o[...] = val
        vt_o[...] = val.T
        gate_o[...] = _dot(xb, win_ref[:, C_GATE:D_INR]).astype(BF16)

    tile = lambda w: pl.BlockSpec((t, w), lambda i: (i, 0))
    full = lambda a: pl.BlockSpec(a.shape, lambda i: (0,) * a.ndim)
    outs = [(Q_LORA, F32), (KV_LORA, F32), (2048, BF16), (HEADS * HP, BF16), (HEADS * HP, BF16), (MLA_W, BF16)]
    per_blk = ATT_BLK // t
    out_specs = [tile(w) for w, _ in outs]
    out_specs += [pl.BlockSpec((None, MLA_W, t), lambda i: (i // per_blk, 0, i % per_blk)), tile(2 * HP)]
    out_shape = [jax.ShapeDtypeStruct((seq, w), d) for w, d in outs]
    out_shape += [jax.ShapeDtypeStruct((seq // ATT_BLK, MLA_W, ATT_BLK), BF16), jax.ShapeDtypeStruct((seq, 2 * HP), F32)]
    return pl.pallas_call(
        body, name="fwd_pre", grid=(seq // t,),
        in_specs=[tile(D_MODEL), tile(1), full(win), full(wuq), full(wkv), full(pvec)],
        out_specs=out_specs, out_shape=out_shape,
        compiler_params=pltpu.CompilerParams(dimension_semantics=("arbitrary",), vmem_limit_bytes=VMEM_LIMIT),
    )(x, pos, win, wuq, wkv, pvec)


def _attn_fwd(q, k, vt):
    seq = q.shape[0]
    b = ATT_BLK
    nq = seq // b
    assert nq % 2 == 0
    n_off = nq * (nq - 1) // 2
    n_blk = nq * (nq + 1) // 2

    def body(q_ref, k_ref, vt_ref, o_o, lse_o, mt_o, p_hbm, m_ref, l_ref, acc_ref, s_even, s_odd, stage, sems):
        m_ref[...] = jnp.full(m_ref.shape, -jnp.inf, F32)
        l_ref[...] = jnp.zeros(l_ref.shape, F32)
        acc_ref[...] = jnp.zeros(acc_ref.shape, F32)
        stage[...] = jnp.zeros(stage.shape, BF16)
        pair = pl.program_id(0)

        def p_copy(slot, t):
            return pltpu.make_async_copy(stage.at[slot], p_hbm.at[pair, t], sems.at[slot])

        for slot in range(4):
            p_copy(slot, n_blk + slot).start()

        def wait_group(g):
            for half in range(2):
                p_copy(2 * g + half, 0).wait()

        def scores(i, j, s_ref):
            qrows = pl.ds(pl.multiple_of(i * b, b), b)
            krows = pl.ds(pl.multiple_of(j * b, b), b)
            for a in range(2):
                s_ref[a] = _dot_nt(k_ref[krows, a * HP:(a + 1) * HP], q_ref[qrows, a * HP:(a + 1) * HP])

        def consume(i, j, s_ref, slot, masked):
            t = (i * (i + 1)) // 2 + j
            vt_blk = vt_ref[j]
            for a in range(2):
                st = s_ref[a]
                if masked:
                    ki = lax.broadcasted_iota(jnp.int32, st.shape, 0)
                    qi = lax.broadcasted_iota(jnp.int32, st.shape, 1)
                    st = jnp.where(ki <= qi, st, -jnp.inf)
                m_prev = m_ref[i, a:a + 1, :]
                m_new = jnp.maximum(m_prev, jnp.max(st, axis=0, keepdims=True))
                alpha = jnp.exp2(m_prev - m_new)
                pt = jnp.exp2(st - m_new)
                ptb = pt.astype(BF16)
                l_ref[i, a:a + 1, :] = alpha * l_ref[i, a:a + 1, :] + jnp.sum(pt, axis=0, keepdims=True)
                acc_ref[i, a] = alpha * acc_ref[i, a] + _dot(vt_blk, ptb)
                m_ref[i, a:a + 1, :] = m_new
                stage[slot, a] = ptb
                mt_o[t, a:a + 1, :] = m_new
            return t

        def after(i, j):
            wrap = j + 1 >= i
            return jnp.minimum(jnp.where(wrap, i + 1, i), nq - 1), jnp.where(wrap, 0, j + 1)

        if n_off > 0:
            scores(1, 0, s_even)

            def below(u, ij):
                g = lax.rem(u, 2)
                wait_group(g)
                i1, j1 = after(*ij)
                scores(i1, j1, s_odd)
                t0 = consume(ij[0], ij[1], s_even, 2 * g, False)
                i2, j2 = after(i1, j1)
                scores(i2, j2, s_even)
                t1 = consume(i1, j1, s_odd, 2 * g + 1, False)
                p_copy(2 * g, t0).start()
                p_copy(2 * g + 1, t1).start()
                return i2, j2

            last = lax.fori_loop(0, n_off // 2, below, (jnp.int32(1), jnp.int32(0)))
            if n_off % 2:
                g = (n_off // 2) % 2
                p_copy(2 * g, 0).wait()
                p_copy(2 * g, consume(last[0], last[1], s_even, 2 * g, False)).start()

        scores(0, 0, s_even)
        passes = (n_off + 1) // 2

        def diagonal(u, carry):
            g = lax.rem(u + passes, 2)
            wait_group(g)
            i0 = 2 * u
            scores(i0 + 1, i0 + 1, s_odd)
            t0 = consume(i0, i0, s_even, 2 * g, True)
            i2 = jnp.minimum(i0 + 2, nq - 1)
            scores(i2, i2, s_even)
            t1 = consume(i0 + 1, i0 + 1, s_odd, 2 * g + 1, True)
            p_copy(2 * g, t0).start()
            p_copy(2 * g + 1, t1).start()
            return carry

        lax.fori_loop(0, nq // 2, diagonal, 0)
        top = lax.broadcasted_iota(jnp.int32, (HP, b), 0) < 64

        def finish(i, carry):
            rows = pl.ds(pl.multiple_of(i * b, b), b)
            o_o[rows, :] = jnp.where(top, acc_ref[i, 0] / l_ref[i, 0:1, :], acc_ref[i, 1] / l_ref[i, 1:2, :]).T
            lse_o[i] = m_ref[i, 0:2, :] + jnp.log2(l_ref[i, 0:2, :])
            return carry

        lax.fori_loop(0, nq, finish, 0)
        wait_group(0)
        wait_group(1)

    return pl.pallas_call(
        body, name="attn_fwd", grid=(PAIRS,),
        in_specs=[pl.BlockSpec((seq, 2 * HP), lambda p: (0, p)),
                  pl.BlockSpec((seq, 2 * HP), lambda p: (0, p)),
                  pl.BlockSpec((nq, HP, b), lambda p: (0, p, 0))],
        out_specs=[pl.BlockSpec((seq, HP), lambda p: (0, p)),
                   pl.BlockSpec((None, nq, 2, b), lambda p: (p, 0, 0, 0)),
                   pl.BlockSpec((None, n_blk, 2, b), lambda p: (p, 0, 0, 0)),
                   pl.BlockSpec(memory_space=pl.ANY)],
        out_shape=[jax.ShapeDtypeStruct((seq, MLA_W), F32),
                   jax.ShapeDtypeStruct((PAIRS, nq, 2, b), F32),
                   jax.ShapeDtypeStruct((PAIRS, n_blk, 2, b), F32),
                   jax.ShapeDtypeStruct((PAIRS, n_blk + 4, 2, b, b), BF16)],
        scratch_shapes=[pltpu.VMEM((nq, 8, b), F32), pltpu.VMEM((nq, 8, b), F32), pltpu.VMEM((nq, 2, HP, b), F32),
                        pltpu.VMEM((2, b, b), F32), pltpu.VMEM((2, b, b), F32), pltpu.VMEM((4, 2, b, b), BF16),
                        pltpu.SemaphoreType.DMA((4,))],
        compiler_params=pltpu.CompilerParams(dimension_semantics=("arbitrary",), vmem_limit_bytes=VMEM_LIMIT),
    )(q, k, vt)


def _post(x, tgt, o, gate, wout, pvec, wt, wtt, bsp):
    seq = x.shape[0]
    t = POST_TILE
    nt = seq // t

    def body(x_ref, tgt_ref, o_ref, gate_ref, wout_ref, pv_ref, wt_ref, wtt_ref, bsp_ref,
             dh2_o, do_o, dgate_o, gwout_o, gwsp_o, vec_o, sv_ref, dvln_ref, bacc_ref):
        i = pl.program_id(0)

        @pl.when(i == 0)
        def _():
            gwout_o[...] = jnp.zeros_like(gwout_o)
            gwsp_o[...] = jnp.zeros_like(gwsp_o)
            vec_o[...] = jnp.zeros_like(vec_o)
            bacc_ref[...] = jnp.zeros_like(bacc_ref)

        za = gate_ref[:, 0:512].astype(F32)
        u_pre = gate_ref[:, 512:1024].astype(F32)
        v_pre = gate_ref[:, 1024:1536].astype(F32)
        zb = gate_ref[:, 1536:2048].astype(F32)
        sg = pv_ref[PV_SG:PV_SG + 1, 0:GW]
        sb = pv_ref[PV_SB:PV_SB + 1, 0:GW]
        lng = pv_ref[PV_LNG:PV_LNG + 1, :]
        lnb = pv_ref[PV_LNB:PV_LNB + 1, :]
        o = o_ref[...]

        sig_a = _sigmoid(za)
        silu_a = za * sig_a
        u, du, dgv, xh, r, vln, svb, sig_b = _gmlp_fwd(u_pre, v_pre, zb, sg, sb, wt_ref, bsp_ref, sv_ref, t)
        silu_b = zb * sig_b
        sgu = u * svb
        merged = jnp.concatenate([o * silu_a, sgu * silu_b], axis=1).astype(BF16)
        h2 = DN_ALPHA * x_ref[...] + _dot(merged, wout_ref[...])
        xh2, r2 = _ln_stats(h2)
        err = xh2 * lng + lnb - tgt_ref[...]
        d_out = err * (1.0 / D_MODEL)
        vec_o[GV_LNG:GV_LNG + 1, :] += jnp.sum(d_out * xh2, axis=0, keepdims=True)
        vec_o[GV_LNB:GV_LNB + 1, :] += jnp.sum(d_out, axis=0, keepdims=True)
        vec_o[GV_LOSS:GV_LOSS + 1, :] += jnp.sum(err * err, axis=0, keepdims=True) * (0.5 / D_MODEL)

        d_h2 = _ln_bwd(d_out, lng, xh2, r2)
        dh2_o[...] = d_h2
        dh2b = d_h2.astype(BF16)
        gwout_o[...] += _dot_tn(merged, dh2b)
        d_m = _dot_nt(dh2b, wout_ref[...])
        d_oa = d_m[:, 0:512]
        d_ob = d_m[:, 512:1024]
        do_o[...] = (d_oa * silu_a).astype(BF16)
        dgate_o[:, 0:512] = (d_oa * o * (sig_a * (1.0 + za * (1.0 - sig_a)))).astype(BF16)
        dgate_o[:, 1536:2048] = (d_ob * sgu * (sig_b * (1.0 + zb * (1.0 - sig_b)))).astype(BF16)
        d_sgu = d_ob * silu_b
        dgate_o[:, 512:1024] = (d_sgu * svb * du).astype(BF16)
        d_sv = d_sgu * u
        acc = bacc_ref[...]
        for c in range(t // CHUNK):
            acc = acc + d_sv[c * CHUNK:(c + 1) * CHUNK, :]
        bacc_ref[...] = acc
        d_svb = d_sv.astype(BF16)
        for c in range(t // CHUNK):
            for p in range(PAIRS):
                blk = d_svb[c * CHUNK:(c + 1) * CHUNK, p * HP:(p + 1) * HP]
                vblk = vln[c * CHUNK:(c + 1) * CHUNK, p * HP:(p + 1) * HP]
                first = _lane_lt64(blk.shape)
                gwsp_o[2 * p] += _dot_nt(jnp.where(first, blk, jnp.zeros_like(blk)), vblk)
                gwsp_o[2 * p + 1] += _dot_nt(jnp.where(first, jnp.zeros_like(blk), blk), vblk)
        _spatial_mix(wtt_ref, d_svb, dvln_ref, t)
        d_vln = dvln_ref[...]
        vec_o[GV_SG:GV_SG + 1, 0:GW] += jnp.sum(d_vln * xh, axis=0, keepdims=True)
        vec_o[GV_SB:GV_SB + 1, 0:GW] += jnp.sum(d_vln, axis=0, keepdims=True)
        dgate_o[:, 1024:1536] = (_ln_bwd(d_vln, sg, xh, r) * dgv).astype(BF16)


        @pl.when(i == nt - 1)
        def _():
            tri = (lax.broadcasted_iota(jnp.int32, (CHUNK, CHUNK), 1)
                   <= lax.broadcasted_iota(jnp.int32, (CHUNK, CHUNK), 0))
            for h in range(HEADS):
                gwsp_o[h] = jnp.where(tri, gwsp_o[h], 0.0)
            lane = lax.broadcasted_iota(jnp.int32, (CHUNK, HP), 1)
            res = jnp.zeros((CHUNK, HP), F32)
            for h in range(HEADS):
                p, a = divmod(h, 2)
                blk = bacc_ref[:, p * HP:(p + 1) * HP]
                part = jnp.where(_lane_lt64(blk.shape) == (a == 0), blk, 0.0)
                res = jnp.where(lane == h, jnp.sum(part, axis=-1, keepdims=True), res)
            vec_o[GV_BSP:GV_BSP + HEADS, 0:HP] = res.T[0:HEADS, :]
            lane1 = lax.broadcasted_iota(jnp.int32, (1, D_MODEL), 1)
            total = jnp.sum(vec_o[GV_LOSS:GV_LOSS + 1, :], axis=-1, keepdims=True)
            vec_o[GV_LOSS:GV_LOSS + 1, :] = jnp.where(lane1 == 0, total, 0.0)

    tile = lambda w: pl.BlockSpec((t, w), lambda i: (i, 0))
    full = lambda a: pl.BlockSpec(a.shape, lambda i: (0,) * a.ndim)
    const = lambda s: pl.BlockSpec(s, lambda i: (0,) * len(s))
    return pl.pallas_call(
        body, name="post", grid=(nt,),
        in_specs=[tile(D_MODEL), tile(D_MODEL), tile(MLA_W), tile(2048), full(wout), full(pvec),
                  full(wt), full(wtt), full(bsp)],
        out_specs=[tile(D_MODEL), tile(MLA_W), tile(2048), const((D_MODEL, D_MODEL)),
                   const((HEADS, CHUNK, CHUNK)), const((GV_ROWS, D_MODEL))],
        out_shape=[jax.ShapeDtypeStruct((seq, D_MODEL), F32), jax.ShapeDtypeStruct((seq, MLA_W), BF16),
                   jax.ShapeDtypeStruct((seq, 2048), BF16), jax.ShapeDtypeStruct((D_MODEL, D_MODEL), F32),
                   jax.ShapeDtypeStruct((HEADS, CHUNK, CHUNK), F32), jax.ShapeDtypeStruct((GV_ROWS, D_MODEL), F32)],
        scratch_shapes=[pltpu.VMEM((t, GW), F32), pltpu.VMEM((t, GW), F32), pltpu.VMEM((CHUNK, GW), F32)],
        compiler_params=pltpu.CompilerParams(dimension_semantics=("arbitrary",), vmem_limit_bytes=VMEM_LIMIT),
    )(x, tgt, o, gate, wout, pvec, wt, wtt, bsp)


def _attn_bwd(q, k, v, do, o, lse, cs, pvec, mt, p_all):
    seq = q.shape[0]
    b = ATT_BLK
    nq = seq // b

    def body(q_ref, k_ref, v_ref, do_ref, o_ref, lse_ref, cs_ref, pv_ref, mt_ref, p_hbm, dq_o, dk_o, dv_o, dk_acc, dv_acc,
             pbuf, sems):
        pair = pl.program_id(0)
        i = pl.program_id(1)
        base = (i * (i + 1)) // 2

        def p_copy(slot, t):
            return pltpu.make_async_copy(p_hbm.at[pair, t], pbuf.at[slot], sems.at[slot])

        p_copy(0, base).start()

        @pl.when(i == 0)
        def _():
            dk_acc[...] = jnp.zeros_like(dk_acc)
            dv_acc[...] = jnp.zeros_like(dv_acc)

        first = _lane_lt64((b, HP))
        do = do_ref[...]
        zero = jnp.zeros_like(do)
        dos = [jnp.where(first, do, zero), jnp.where(first, zero, do)]
        prod_t = (do.astype(F32) * o_ref[...]).T
        deltas = [jnp.sum(prod_t[0:64, :], axis=0, keepdims=True),
                  jnp.sum(prod_t[64:128, :], axis=0, keepdims=True)]
        lses = [lse_ref[0:1, :], lse_ref[1:2, :]]
        qs = [q_ref[:, a * HP:(a + 1) * HP] for a in range(2)]

        def step(j, dqs):
            slot = lax.rem(j, 2)
            p_copy(slot, base + j).wait()
            p_copy(1 - slot, base + jnp.minimum(j + 1, i)).start()
            rows = pl.ds(pl.multiple_of(j * b, b), b)
            vb = v_ref[rows, :]
            new_dq = []
            dvs = []
            for a in range(2):
                kb = k_ref[rows, a * HP:(a + 1) * HP]
                pt = pbuf[slot, a].astype(F32) * jnp.exp2(mt_ref[base + j, a:a + 1, :] - lses[a])
                dvs.append(_dot(pt.astype(BF16), do))
                dpt = _dot_nt(vb, dos[a])
                dst = (pt * (dpt - deltas[a])).astype(BF16)
                dk_acc[rows, a * HP:(a + 1) * HP] += _dot(dst, qs[a])
                new_dq.append(dqs[a] + _dot_tn(dst, kb))
            dv_acc[rows, :] += jnp.where(first, dvs[0], dvs[1])
            return tuple(new_dq)

        init = (jnp.zeros((b, HP), F32), jnp.zeros((b, HP), F32))
        dqs = lax.fori_loop(0, i + 1, step, init)
        p_copy(lax.rem(i + 1, 2), base + i).wait()
        cos = cs_ref[:, 0:HP]
        sin = cs_ref[:, HP:2 * HP]
        s1 = sin * pv_ref[PV_M1:PV_M1 + 1, 0:HP]
        s2 = sin * pv_ref[PV_M2:PV_M2 + 1, 0:HP]
        for a in range(2):
            dq_o[:, a * HP:(a + 1) * HP] = _rope_bwd(dqs[a] * SCALE, cos, s1, s2).astype(BF16)

        @pl.when(i == nq - 1)
        def _():
            dk_o[...] = (dk_acc[...] * (SCALE / SCALE_LOG2E)).astype(BF16)
            dv_o[...] = dv_acc[...].astype(BF16)

    return pl.pallas_call(
        body, name="attn_bwd", grid=(PAIRS, nq),
        in_specs=[pl.BlockSpec((b, 2 * HP), lambda p, i: (i, p)),
                  pl.BlockSpec((seq, 2 * HP), lambda p, i: (0, p)),
                  pl.BlockSpec((seq, HP), lambda p, i: (0, p)),
                  pl.BlockSpec((b, HP), lambda p, i: (i, p)),
                  pl.BlockSpec((b, HP), lambda p, i: (i, p)),
                  pl.BlockSpec((None, None, 2, b), lambda p, i: (p, i, 0, 0)),
                  pl.BlockSpec((b, 2 * HP), lambda p, i: (i, 0)),
                  pl.BlockSpec(pvec.shape, lambda p, i: (0, 0)),
                  pl.BlockSpec((None,) + mt.shape[1:], lambda p, i: (p, 0, 0, 0)),
                  pl.BlockSpec(memory_space=pl.ANY)],
        out_specs=[pl.BlockSpec((b, 2 * HP), lambda p, i: (i, p)),
                   pl.BlockSpec((seq, 2 * HP), lambda p, i: (0, p)),
                   pl.BlockSpec((seq, HP), lambda p, i: (0, p))],
        out_shape=[jax.ShapeDtypeStruct((seq, HEADS * HP), BF16),
                   jax.ShapeDtypeStruct((seq, HEADS * HP), BF16),
                   jax.ShapeDtypeStruct((seq, MLA_W), BF16)],
        scratch_shapes=[pltpu.VMEM((seq, 2 * HP), F32), pltpu.VMEM((seq, HP), F32), pltpu.VMEM((2, 2, b, b), BF16),
                        pltpu.SemaphoreType.DMA((2,))],
        compiler_params=pltpu.CompilerParams(dimension_semantics=("arbitrary", "arbitrary"),
                                             vmem_limit_bytes=VMEM_LIMIT),
    )(q, k, v, do, o, lse, cs, pvec, mt, p_all)


def _bwd_pre(x, dh2, cq, ckv, cs, dq, dk, dv, dgate, win, wuq, wkv, pvec, gvec):
    seq = x.shape[0]
    t = BWD_TILE

    def body(x_ref, dh2_ref, cq_ref, ckv_ref, cs_ref, dq_ref, dk_ref, dv_ref, dgate_ref,
             win_ref, wuq_ref, wkv_ref, pv_ref, gv_ref, gx_o, gwin_o, gwuq_o, gwkv_o, vec_o):
        i = pl.program_id(0)

        @pl.when(i == 0)
        def _():
            gwin_o[...] = jnp.zeros_like(gwin_o)
            gwuq_o[...] = jnp.zeros_like(gwuq_o)
            gwkv_o[...] = jnp.zeros_like(gwkv_o)
            vec_o[...] = gv_ref[...]

        xb = x_ref[...].astype(BF16)
        dgate = dgate_ref[...]
        gwin_o[:, C_GATE:D_INR] += _dot_tn(xb, dgate)
        gx_gate = _dot_nt(dgate, win_ref[:, C_GATE:D_INR])

        qg = pv_ref[PV_QG:PV_QG + 1, 0:Q_LORA]
        kvg = pv_ref[PV_KVG:PV_KVG + 1, 0:KV_LORA]
        dq = dq_ref[...]
        cqh, rq = _rms_stats(cq_ref[...])
        d_cqn = _dot_nt(dq, wuq_ref[...])
        gwuq_o[...] += _dot_tn((cqh * qg).astype(BF16), dq)
        vec_o[GV_QG:GV_QG + 1, 0:Q_LORA] += jnp.sum(d_cqn * cqh, axis=0, keepdims=True)
        d_cq = _rms_bwd(d_cqn, qg, cqh, rq)

        dk = dk_ref[...]
        dkv = jnp.concatenate([dk, dv_ref[...]], axis=1)
        ckvh, rkv = _rms_stats(ckv_ref[...])
        d_ckvn = _dot_nt(dkv, wkv_ref[...])
        gwkv_o[...] += _dot_tn((ckvh * kvg).astype(BF16), dkv)
        vec_o[GV_KVG:GV_KVG + 1, 0:KV_LORA] += jnp.sum(d_ckvn * ckvh, axis=0, keepdims=True)
        d_ckv = _rms_bwd(d_ckvn, kvg, ckvh, rkv)

        dks = dk[:, 0:HP].astype(F32)
        for h in range(1, HEADS):
            dks = dks + dk[:, h * HP:(h + 1) * HP].astype(F32)
        cos = cs_ref[:, 0:HP]
        sin = cs_ref[:, HP:2 * HP]
        d_kr = _rope_bwd(dks, cos, sin * pv_ref[PV_M1:PV_M1 + 1, 0:HP], sin * pv_ref[PV_M2:PV_M2 + 1, 0:HP])

        d_lat = jnp.concatenate([d_cq.astype(BF16), d_ckv.astype(BF16), d_kr.astype(BF16)], axis=1)
        gwin_o[:, 0:C_GATE] += _dot_tn(xb, d_lat)
        gx_o[...] = DN_ALPHA * dh2_ref[...] + gx_gate + _dot_nt(d_lat, win_ref[:, 0:C_GATE])

    tile = lambda w: pl.BlockSpec((t, w), lambda i: (i, 0))
    full = lambda a: pl.BlockSpec(a.shape, lambda i: (0,) * a.ndim)
    const = lambda s: pl.BlockSpec(s, lambda i: (0,) * len(s))
    return pl.pallas_call(
        body, name="bwd_pre", grid=(seq // t,),
        in_specs=[tile(D_MODEL), tile(D_MODEL), tile(Q_LORA), tile(KV_LORA), tile(2 * HP), tile(HEADS * HP),
                  tile(HEADS * HP), tile(MLA_W), tile(2048), full(win), full(wuq), full(wkv), full(pvec), full(gvec)],
        out_specs=[tile(D_MODEL), const((D_MODEL, D_INR)), const((Q_LORA, HEADS * HP)),
                   const((KV_LORA, HEADS * HP + MLA_W)), const((GV_ROWS, D_MODEL))],
        out_shape=[jax.ShapeDtypeStruct((seq, D_MODEL), F32), jax.ShapeDtypeStruct((D_MODEL, D_INR), F32),
                   jax.ShapeDtypeStruct((Q_LORA, HEADS * HP), F32),
                   jax.ShapeDtypeStruct((KV_LORA, HEADS * HP + MLA_W), F32),
                   jax.ShapeDtypeStruct((GV_ROWS, D_MODEL), F32)],
        compiler_params=pltpu.CompilerParams(dimension_semantics=("arbitrary",), vmem_limit_bytes=VMEM_LIMIT),
    )(x, dh2, cq, ckv, cs, dq, dk, dv, dgate, win, wuq, wkv, pvec, gvec)


def _grad_reduce(gs, gvec):
    n_arr = len(gs)
    n_big = n_arr - 1
    k1 = lambda n, blk: 4 * n + blk
    k2 = lambda n, kk: 4 * n_arr + 3 * n + kk
    k3 = lambda n: 7 * n_arr + n
    k3w = lambda k: 7 * n_arr + n_big + k
    kv = lambda k: 7 * n_arr + n_big + 7 + k
    n_sem = 7 * n_arr + n_big + 14

    def body(*refs):
        g, gv = refs[0:n_arr], refs[n_arr]
        outs, ov = refs[n_arr + 1:2 * n_arr + 1], refs[2 * n_arr + 1]
        r1 = refs[2 * n_arr + 2:3 * n_arr + 2]
        r2 = refs[3 * n_arr + 2:4 * n_arr + 2]
        s2 = refs[4 * n_arr + 2:5 * n_arr + 2]
        vbuf, send_sems, recv_sems = refs[5 * n_arr + 2:]
        x, y, c = lax.axis_index("x"), lax.axis_index("y"), lax.axis_index("c")
        j = 2 * x + y
        me = 2 * j + c
        sib = (x, y, 1 - c)
        chips = [(1 - x, y), (x, 1 - y), (1 - x, 1 - y)]
        others = [sib] + [(px, py, pc) for (px, py) in chips for pc in (c, 1 - c)]

        def copy(k, src, dst, to):
            return pltpu.make_async_remote_copy(
                src_ref=src, dst_ref=dst, send_sem=send_sems.at[k], recv_sem=recv_sems.at[k],
                device_id=to, device_id_type=MESH)

        l1 = [copy(k1(n, blk), g[n].at[blk, 1 - c], r1[n].at[blk], sib) for n in range(n_arr) for blk in range(4)]
        lv = [copy(kv(k), gv, vbuf.at[me], to) for k, to in enumerate(others)]
        for cp in l1 + lv:
            cp.start()
        l2 = []
        for n in range(n_arr):
            for blk in range(4):
                copy(k1(n, blk), g[n].at[blk, c], r1[n].at[blk], sib).wait_recv()
            for blk in range(4):
                r1[n][blk] = g[n][blk, c] + r1[n][blk]
                s2[n][blk] = r1[n][blk].astype(BF16)
            for kk, (px, py) in enumerate(chips):
                l2.append(copy(k2(n, kk), s2[n].at[2 * px + py], r2[n].at[kk], (px, py, c)))
                l2[-1].start()

        l3 = []
        for n in range(n_arr):
            for kk in range(3):
                copy(k2(n, kk), s2[n].at[0], r2[n].at[kk], sib).wait_recv()
            red = ((r1[n][j] + r2[n][0].astype(F32)) + r2[n][1].astype(F32)) + r2[n][2].astype(F32)
            if n < n_big:
                outs[n][c] = red
                back = [copy(k3(n), outs[n].at[c], outs[n].at[c], sib)]
            else:
                outs[n][j, c] = red
                back = [copy(k3w(k), outs[n].at[j, c], outs[n].at[j, c], to) for k, to in enumerate(others)]
            for cp in back:
                cp.start()
            l3 += back
        for n in range(n_big):
            copy(k3(n), outs[n].at[1 - c], outs[n].at[1 - c], sib).wait_recv()
        for k, (px, py, pc) in enumerate(others):
            landed = outs[n_big].at[2 * px + py, pc]
            copy(k3w(k), landed, landed, (px, py, pc)).wait_recv()
            copy(kv(k), gv, vbuf.at[4 * px + 2 * py + pc], (px, py, pc)).wait_recv()
        vbuf[me] = gv[...]
        total = vbuf[0]
        for d in range(1, 8):
            total = total + vbuf[d]
        ov[...] = total
        for cp in l1 + lv + l2 + l3:
            cp.wait_send()

    vmem = pl.BlockSpec(memory_space=pltpu.VMEM)
    half_shapes = [a.shape[2:] for a in gs]
    out_shape = [jax.ShapeDtypeStruct((2,) + s, F32) for s in half_shapes[:n_big]]
    out_shape += [jax.ShapeDtypeStruct((4, 2) + half_shapes[n_big], F32), jax.ShapeDtypeStruct(gvec.shape, F32)]
    scratch = [pltpu.VMEM((4,) + s, F32) for s in half_shapes] + [pltpu.VMEM((3,) + s, BF16) for s in half_shapes]
    scratch += [pltpu.VMEM((4,) + s, BF16) for s in half_shapes]
    scratch += [pltpu.VMEM((8,) + gvec.shape, F32), pltpu.SemaphoreType.DMA((n_sem,)), pltpu.SemaphoreType.DMA((n_sem,))]
    return pl.pallas_call(
        body, name="grad_reduce", out_shape=out_shape,
        in_specs=[vmem] * (n_arr + 1), out_specs=[vmem] * (n_arr + 1), scratch_shapes=scratch,
        compiler_params=pltpu.CompilerParams(vmem_limit_bytes=VMEM_LIMIT),
    )(*gs, gvec)


SMALL_ROWS = ((GV_QG, 1, Q_LORA), (GV_KVG, 1, KV_LORA), (GV_SG, 1, GW), (GV_SB, 1, GW),
              (GV_LNG, 1, D_MODEL), (GV_LNB, 1, D_MODEL), (GV_BSP, HEADS, CHUNK))


def _adam_update(g, w, m, v):
    m_new = ADAM_B1 * m + (1.0 - ADAM_B1) * g
    v_new = ADAM_B2 * v + (1.0 - ADAM_B2) * (g * g)
    m_hat = m_new / (1.0 - ADAM_B1 ** ADAM_STEP)
    v_hat = v_new / (1.0 - ADAM_B2 ** ADAM_STEP)
    return -ADAM_LR * (m_hat / (jnp.sqrt(v_hat) + ADAM_EPS) + ADAM_WD * w), m_new, v_new


def _adamw(g_big, w_big, m_big, v_big, gvec, w_small, m_small, v_small):
    nb, ns = len(g_big), len(w_small)

    def body(*refs):
        it = iter(refs)
        take = lambda n: [next(it) for _ in range(n)]
        g_b, w_b, m_b, v_b = take(nb), take(nb), take(nb), take(nb)
        gv = next(it)
        w_s, m_s, v_s = take(ns), take(ns), take(ns)
        g_bo, d_bo, m_bo, v_bo = take(nb), take(nb), take(nb), take(nb)
        g_so, d_so, m_so, v_so = take(ns), take(ns), take(ns), take(ns)
        for n in range(nb):
            gb = g_b[n][...]
            g_bo[n][...] = gb
            d_bo[n][...], m_bo[n][...], v_bo[n][...] = _adam_update(gb, w_b[n][...], m_b[n][...], v_b[n][...])
        for n, (row, nrow, width) in enumerate(SMALL_ROWS):
            gs = gv[row:row + nrow, 0:width]
            g_so[n][...] = gs
            d_so[n][...], m_so[n][...], v_so[n][...] = _adam_update(gs, w_s[n][...], m_s[n][...], v_s[n][...])

    def rows(a):
        nd = a.ndim
        return pl.BlockSpec((a.shape[0] // ADAM_STEPS,) + a.shape[1:], lambda i: (i,) + (0,) * (nd - 1))

    def whole(a):
        nd = a.ndim
        return pl.BlockSpec(a.shape, lambda i: (0,) * nd)

    big = [jax.ShapeDtypeStruct(a.shape, F32) for a in w_big]
    small = [jax.ShapeDtypeStruct(a.shape, F32) for a in w_small]
    return pl.pallas_call(
        body, name="adamw", grid=(ADAM_STEPS,), out_shape=big * 4 + small * 4,
        in_specs=[rows(a) for a in g_big + w_big + m_big + v_big] + [whole(gvec)]
        + [whole(a) for a in w_small + m_small + v_small],
        out_specs=[rows(a) for a in w_big] * 4 + [whole(a) for a in w_small] * 4,
        compiler_params=pltpu.CompilerParams(dimension_semantics=("arbitrary",), vmem_limit_bytes=VMEM_LIMIT),
    )(*g_big, *w_big, *m_big, *v_big, gvec, *w_small, *m_small, *v_small)


def kernel(x, positions, w_in, q_norm_g, w_uq, kv_norm_g, w_ukv, sgu_norm_g, sgu_norm_b, w_spatial, b_spatial, w_out, ln_g, ln_b, loss_target, m_w_in, m_q_norm_g, m_w_uq, m_kv_norm_g, m_w_ukv, m_sgu_norm_g, m_sgu_norm_b, m_w_spatial, m_b_spatial, m_w_out, m_ln_g, m_ln_b, v_w_in, v_q_norm_g, v_w_uq, v_kv_norm_g, v_w_ukv, v_sgu_norm_g, v_sgu_norm_b, v_w_spatial, v_b_spatial, v_w_out, v_ln_g, v_ln_b):
    seq = x.shape[1]
    x2 = x.reshape(seq, D_MODEL)
    tgt = loss_target.reshape(seq, D_MODEL)
    pos = positions.reshape(seq, 1)

    a_in, a_uq, a_ukv, a_out = _weight_gather([w_in, w_uq, w_ukv, w_out])
    w_uq_f = jnp.swapaxes(a_uq, 0, 1).reshape(Q_LORA, HEADS * (NOPE + ROPE))
    w_ukv_f = jnp.swapaxes(a_ukv, 0, 1).reshape(KV_LORA, HEADS * (NOPE + VDIM))
    wout = a_out.reshape(D_MODEL, D_MODEL)
    zc = lambda n: jnp.zeros((D_MODEL, n), BF16)
    win = jnp.concatenate([a_in[0][:, 0:C_KR], zc(NOPE), a_in[0][:, C_KR:C_KR + ROPE], zc(HP - NOPE - ROPE),
                           a_in[0][:, C_KR + ROPE:],
                           a_in[1], a_in[2], a_in[3]], axis=1)
    wuq = jnp.pad(w_uq_f.reshape(Q_LORA, HEADS, NOPE + ROPE), ((0, 0), (0, 0), (0, HP - NOPE - ROPE)))
    wuq = wuq.reshape(Q_LORA, HEADS * HP)
    ukv = w_ukv_f.reshape(KV_LORA, HEADS, NOPE + VDIM)
    wk = jnp.pad(ukv[:, :, 0:NOPE], ((0, 0), (0, 0), (0, HP - NOPE))).reshape(KV_LORA, HEADS * HP)
    wkv = jnp.concatenate([wk, ukv[:, :, NOPE:].reshape(KV_LORA, MLA_W)], axis=1)

    lane = np.arange(HP)
    half = ROPE // 2
    inv_freq = (1.0 / (ROPE_THETA ** (np.arange(half, dtype=np.float32) / half))).astype(np.float32)
    in_rope = (lane >= NOPE) & (lane < NOPE + ROPE)
    invf = jnp.asarray(np.where(in_rope, inv_freq[(lane - NOPE) % half], 0.0).astype(np.float32))
    m1 = jnp.asarray(np.where((lane >= NOPE) & (lane < NOPE + half), -1.0, 0.0).astype(np.float32))
    m2 = jnp.asarray(np.where((lane >= NOPE + half) & (lane < NOPE + ROPE), 1.0, 0.0).astype(np.float32))
    row = lambda a: jnp.pad(a.astype(F32), (0, D_MODEL - a.shape[0]))
    pvec = jnp.stack([row(q_norm_g), row(kv_norm_g), row(sgu_norm_g), row(sgu_norm_b), row(invf), row(m1),
                      row(m2), row(ln_g), row(ln_b)] + [jnp.zeros((D_MODEL,), F32)] * (PV_ROWS - 9))
    tri = jnp.tril(jnp.ones((CHUNK, CHUNK), dtype=bool))
    wt = jnp.where(tri[None], w_spatial, 0.0).astype(BF16)
    wtt = jnp.swapaxes(wt, 1, 2)
    bsp = jnp.repeat(b_spatial.T, VDIM, axis=1)

    cq, ckv, gate, q, k, v, vt, cs = _fwd_pre(x2, pos, win, wuq, wkv, pvec)
    o, lse, mt, p_all = _attn_fwd(q, k, vt)
    dh2, do, dgate, g_wout, g_wsp, gvec = _post(x2, tgt, o, gate, wout, pvec, wt, wtt, bsp)
    dq, dk, dv = _attn_bwd(q, k, v, do, o, lse, cs, pvec, mt, p_all)
    gx, g_win, g_wuq, g_wkv, gvec = _bwd_pre(x2, dh2, cq, ckv, cs, dq, dk, dv, dgate, win, wuq, wkv, pvec, gvec)

    cw = w_in.shape[1]
    first = D_INR - 3 * cw
    g_win_0 = jnp.concatenate([g_win[:, 0:C_KR], g_win[:, C_KR + NOPE:C_KR + NOPE + ROPE], g_win[:, C_GATE:first]],
                              axis=1)
    g_win_b = jnp.stack([g_win_0] + [g_win[:, first + cw * jb:first + cw * (jb + 1)] for jb in range(3)])
    g_wuq_f = g_wuq.reshape(Q_LORA, HEADS, HP)[:, :, 0:NOPE + ROPE].reshape(Q_LORA, HEADS * (NOPE + ROPE))
    g_k = g_wkv[:, 0:HEADS * HP].reshape(KV_LORA, HEADS, HP)[:, :, 0:NOPE]
    g_v = g_wkv[:, HEADS * HP:].reshape(KV_LORA, HEADS, VDIM)
    g_wukv_f = jnp.concatenate([g_k, g_v], axis=2).reshape(KV_LORA, HEADS * (NOPE + VDIM))

    def by_chip(a):
        rows, cols = a.shape[0], a.shape[1] // 4
        return jnp.swapaxes(a.reshape(rows, 4, cols), 0, 1).reshape(4, 2, rows // 2, cols)

    gs = [g_win_b.reshape(4, 2, D_MODEL // 2, cw), by_chip(g_wuq_f), by_chip(g_wukv_f), g_wout.reshape(4, 2, 128, D_MODEL),
          g_wsp.reshape(4, 2, CHUNK, CHUNK)]
    r_in, r_uq, r_ukv, r_out, r_wsp, r_vec = _grad_reduce(gs, gvec)

    g_big = [r_in.reshape(w_in.shape), r_uq.reshape(w_uq.shape), r_ukv.reshape(w_ukv.shape),
             r_out.reshape(w_out.shape), r_wsp.reshape(w_spatial.shape)]
    small = lambda qg, kvg, sg, sb, lng, lnb, bs: [qg.reshape(1, -1), kvg.reshape(1, -1), sg.reshape(1, -1),
                                                   sb.reshape(1, -1), lng.reshape(1, -1), lnb.reshape(1, -1), bs]
    res = _adamw(g_big, [w_in, w_uq, w_ukv, w_out, w_spatial], [m_w_in, m_w_uq, m_w_ukv, m_w_out, m_w_spatial],
                 [v_w_in, v_w_uq, v_w_ukv, v_w_out, v_w_spatial], r_vec,
                 small(q_norm_g, kv_norm_g, sgu_norm_g, sgu_norm_b, ln_g, ln_b, b_spatial),
                 small(m_q_norm_g, m_kv_norm_g, m_sgu_norm_g, m_sgu_norm_b, m_ln_g, m_ln_b, m_b_spatial),
                 small(v_q_norm_g, v_kv_norm_g, v_sgu_norm_g, v_sgu_norm_b, v_ln_g, v_ln_b, v_b_spatial))

    def ordered(big, sm):
        vec = lambda n: sm[n].reshape(-1)
        return [big[0], vec(0), big[1], vec(1), big[2], vec(2), vec(3), big[4], sm[6], big[3], vec(4), vec(5)]

    loss = r_vec[GV_LOSS, 0]
    return (loss, gx.reshape(1, seq, D_MODEL), *ordered(res[0:5], res[20:27]), *ordered(res[5:10], res[27:34]),
            *ordered(res[10:15], res[34:41]), *ordered(res[15:20], res[41:48]))
```

```python
import math

import jax
import jax.numpy as jnp
import numpy as np
from jax import lax
from jax.experimental import pallas as pl
from jax.experimental.pallas import tpu as pltpu

F32 = jnp.float32
BF16 = jnp.bfloat16

D_MODEL = 1024
Q_LORA = 256
KV_LORA = 128
HEADS = 8
NOPE = 64
ROPE = 32
VDIM = 64
MLA_W = HEADS * VDIM
GW = 512
CHUNK = 128
HP = 128
PAIRS = HEADS // 2
D_IN = 2464
D_INR = 2560
C_CKV = Q_LORA
C_KR = Q_LORA + KV_LORA
C_GATE = C_KR + HP
ROPE_THETA = 10000.0
DN_ALPHA = 2.0 ** 0.25
EPS = 1e-5
SCALE = 1.0 / math.sqrt(NOPE + ROPE)
SCALE_LOG2E = SCALE * 1.4426950408889634
INV_SQRT2 = 0.7071067811865476
INV_SQRT_2PI = 0.3989422804014327

ADAM_LR = 0.001
ADAM_B1 = 0.9
ADAM_B2 = 0.999
ADAM_EPS = 1e-08
ADAM_WD = 0.01
ADAM_STEP = 10

PV_QG, PV_KVG, PV_SG, PV_SB, PV_INVF, PV_M1, PV_M2, PV_LNG, PV_LNB = range(9)
PV_ROWS = 16
GV_QG, GV_KVG, GV_SG, GV_SB, GV_LNG, GV_LNB, GV_LOSS = range(7)
GV_BSP = 8
GV_ROWS = 16

MESH = pl.DeviceIdType.MESH

FWD_TILE = 512
POST_TILE = 512
BWD_TILE = 512
ATT_BLK = 512
ADAM_STEPS = 4
P_AHEAD = 3
P_GROUPS = 3
VMEM_LIMIT = 56 * 1024 * 1024


def _dot(a, b):
    return jnp.dot(a, b, preferred_element_type=F32)


def _dot_nt(a, b):
    return lax.dot_general(a, b, (((1,), (1,)), ((), ())), preferred_element_type=F32)


def _dot_tn(a, b):
    return lax.dot_general(a, b, (((0,), (0,)), ((), ())), preferred_element_type=F32)


def _sigmoid(z):
    return pl.reciprocal(1.0 + jnp.exp(-z), approx=True)


def _gelu_and_grad(x):
    cdf = 0.5 * (1.0 + lax.erf(x * INV_SQRT2))
    return x * cdf, cdf + x * (INV_SQRT_2PI * jnp.exp(-0.5 * x * x))


def _rms_stats(x):
    r = lax.rsqrt(jnp.mean(x * x, axis=-1, keepdims=True) + EPS)
    return x * r, r


def _rms_bwd(dy, g, xh, r):
    dyg = dy * g
    return r * (dyg - xh * jnp.mean(dyg * xh, axis=-1, keepdims=True))


def _ln_stats(x):
    mu = jnp.mean(x, axis=-1, keepdims=True)
    xc = x - mu
    r = lax.rsqrt(jnp.mean(xc * xc, axis=-1, keepdims=True) + EPS)
    return xc * r, r


def _ln_bwd(dy, g, xh, r):
    dxh = dy * g
    return r * (dxh - jnp.mean(dxh, axis=-1, keepdims=True) - xh * jnp.mean(dxh * xh, axis=-1, keepdims=True))


def _rope_fwd(t, c, s1, s2):
    return t * c + pltpu.roll(t, HP - 16, 1) * s1 + pltpu.roll(t, 16, 1) * s2


def _rope_bwd(d, c, s1, s2):
    return d * c + pltpu.roll(d * s1, 16, 1) + pltpu.roll(d * s2, HP - 16, 1)


def _lane_lt64(shape):
    return lax.broadcasted_iota(jnp.int32, shape, len(shape) - 1) < 64


def _spatial_mix(w_ref, src, dst_ref, rows):
    for c in range(rows // CHUNK):
        for p in range(PAIRS):
            blk = src[c * CHUNK:(c + 1) * CHUNK, p * HP:(p + 1) * HP]
            a = _dot(w_ref[2 * p], blk)
            b = _dot(w_ref[2 * p + 1], blk)
            dst_ref[c * CHUNK:(c + 1) * CHUNK, p * HP:(p + 1) * HP] = jnp.where(_lane_lt64(a.shape), a, b)


def _gmlp_fwd(u_pre, v_pre, zb, sg, sb, wt_ref, bsp_ref, sv_ref, rows):
    u, du = _gelu_and_grad(u_pre)
    gv, dgv = _gelu_and_grad(v_pre)
    xh, r = _ln_stats(gv)
    vln = (xh * sg + sb).astype(BF16)
    _spatial_mix(wt_ref, vln, sv_ref, rows)
    bias = bsp_ref[...]
    svb = sv_ref[...] + jnp.concatenate([bias] * (rows // CHUNK), axis=0)
    sig = _sigmoid(zb)
    return u, du, dgv, xh, r, vln, svb, sig


def _weight_gather(shards):
    n_arr = len(shards)

    def body(*refs):
        ins, outs = refs[0:n_arr], refs[n_arr:2 * n_arr]
        send_sems, recv_sems = refs[2 * n_arr:]
        x, y, c = lax.axis_index("x"), lax.axis_index("y"), lax.axis_index("c")
        j = 2 * x + y
        sib = (x, y, 1 - c)
        chips = [(1 - x, y), (x, 1 - y), (1 - x, 1 - y)]
        for n in range(n_arr):
            outs[n][j] = ins[n][...].astype(BF16)

        def half(n, blk, core):
            r = shards[n].shape[0] // 2
            return outs[n].at[blk, pl.ds(pl.multiple_of(core * r, 16), r), :]

        def copy(k, ref, to):
            return pltpu.make_async_remote_copy(
                src_ref=ref, dst_ref=ref, send_sem=send_sems.at[k], recv_sem=recv_sems.at[k],
                device_id=to, device_id_type=MESH)

        first = [copy(6 * n + kk, half(n, j, c), (px, py, c))
                 for n in range(n_arr) for kk, (px, py) in enumerate(chips)]
        for cp in first:
            cp.start()
        passed = []
        for n in range(n_arr):
            for kk, (px, py) in enumerate(chips):
                landed = half(n, 2 * px + py, c)
                copy(6 * n + kk, landed, (px, py, c)).wait_recv()
                passed.append(copy(6 * n + 3 + kk, landed, sib))
                passed[-1].start()
        for n in range(n_arr):
            for kk, (px, py) in enumerate(chips):
                copy(6 * n + 3 + kk, half(n, 2 * px + py, 1 - c), sib).wait_recv()
        for cp in first + passed:
            cp.wait_send()

    vmem = pl.BlockSpec(memory_space=pltpu.VMEM)
    return pl.pallas_call(
        body, name="weight_gather",
        out_shape=[jax.ShapeDtypeStruct((4,) + a.shape, BF16) for a in shards],
        in_specs=[vmem] * n_arr, out_specs=[vmem] * n_arr,
        scratch_shapes=[pltpu.SemaphoreType.DMA((6 * n_arr,)), pltpu.SemaphoreType.DMA((6 * n_arr,))],
        compiler_params=pltpu.CompilerParams(vmem_limit_bytes=VMEM_LIMIT),
    )(*shards)


def _fwd_pre(x, pos, win, wuq, wkv, pvec):
    seq = x.shape[0]
    t = FWD_TILE

    def body(x_ref, pos_ref, win_ref, wuq_ref, wkv_ref, pv_ref,
             cq_o, ckv_o, gate_o, q_o, k_o, v_o, vt_o, cs_o):
        xb = x_ref[...].astype(BF16)
        proj = _dot(xb, win_ref[:, 0:C_GATE])
        cq = proj[:, 0:C_CKV]
        ckv = proj[:, C_CKV:C_KR]
        kr = proj[:, C_KR:C_GATE]
        cq_o[...] = cq
        ckv_o[...] = ckv

        ang = pos_ref[...].astype(F32) * pv_ref[PV_INVF:PV_INVF + 1, 0:HP]
        cos = jnp.cos(ang)
        sin = jnp.sin(ang)
        cs_o[:, 0:HP] = cos
        cs_o[:, HP:2 * HP] = sin
        s1 = sin * pv_ref[PV_M1:PV_M1 + 1, 0:HP]
        s2 = sin * pv_ref[PV_M2:PV_M2 + 1, 0:HP]

        cqh, _ = _rms_stats(cq)
        q_all = _dot((cqh * pv_ref[PV_QG:PV_QG + 1, 0:Q_LORA]).astype(BF16), wuq_ref[...])
        ckvh, _ = _rms_stats(ckv)
        kv_all = _dot((ckvh * pv_ref[PV_KVG:PV_KVG + 1, 0:KV_LORA]).astype(BF16), wkv_ref[...])
        krr = _rope_fwd(kr, cos, s1, s2)
        for h in range(HEADS):
            sl = slice(h * HP, (h + 1) * HP)
            q_o[:, sl] = (_rope_fwd(q_all[:, sl], cos, s1, s2) * SCALE_LOG2E).astype(BF16)
            k_o[:, sl] = (kv_all[:, sl] + krr).astype(BF16)
        val = kv_all[:, HEADS * HP:].astype(BF16)
        v_o[...] = val
        vt_o[...] = val.T
        gate_o[...] = _dot(xb, win_ref[:, C_GATE:D_INR]).astype(BF16)

    tile = lambda w: pl.BlockSpec((t, w), lambda i: (i, 0))
    full = lambda a: pl.BlockSpec(a.shape, lambda i: (0,) * a.ndim)
    outs = [(Q_LORA, F32), (KV_LORA, F32), (2048, BF16), (HEADS * HP, BF16), (HEADS * HP, BF16), (MLA_W, BF16)]
    per_blk = ATT_BLK // t
    out_specs = [tile(w) for w, _ in outs]
    out_specs += [pl.BlockSpec((None, MLA_W, t), lambda i: (i // per_blk, 0, i % per_blk)), tile(2 * HP)]
    out_shape = [jax.ShapeDtypeStruct((seq, w), d) for w, d in outs]
    out_shape += [jax.ShapeDtypeStruct((seq // ATT_BLK, MLA_W, ATT_BLK), BF16), jax.ShapeDtypeStruct((seq, 2 * HP), F32)]
    return pl.pallas_call(
        body, name="fwd_pre", grid=(seq // t,),
        in_specs=[tile(D_MODEL), tile(1), full(win), full(wuq), full(wkv), full(pvec)],
        out_specs=out_specs, out_shape=out_shape,
        compiler_params=pltpu.CompilerParams(dimension_semantics=("arbitrary",), vmem_limit_bytes=VMEM_LIMIT),
    )(x, pos, win, wuq, wkv, pvec)


def _attn_fwd(q, k, vt):
    seq = q.shape[0]
    b = ATT_BLK
    nq = seq // b
    assert nq % 2 == 0
    n_off = nq * (nq - 1) // 2
    n_blk = nq * (nq + 1) // 2

    def body(q_ref, k_ref, vt_ref, o_o, lse_o, mt_o, p_hbm, m_ref, l_ref, acc_ref, s_even, s_odd, stage, sems):
        m_ref[...] = jnp.full(m_ref.shape, -jnp.inf, F32)
        l_ref[...] = jnp.zeros(l_ref.shape, F32)
        acc_ref[...] = jnp.zeros(acc_ref.shape, F32)
        stage[...] = jnp.zeros(stage.shape, BF16)
        pair = pl.program_id(0)

        def p_copy(slot, t):
            return pltpu.make_async_copy(stage.at[slot], p_hbm.at[pair, t], sems.at[slot])

        for slot in range(2 * P_GROUPS):
            p_copy(slot, n_blk + slot).start()

        def wait_group(g):
            for half in range(2):
                p_copy(2 * g + half, 0).wait()

        def scores(i, j, s_ref):
            qrows = pl.ds(pl.multiple_of(i * b, b), b)
            krows = pl.ds(pl.multiple_of(j * b, b), b)
            for a in range(2):
                s_ref[a] = _dot_nt(k_ref[krows, a * HP:(a + 1) * HP], q_ref[qrows, a * HP:(a + 1) * HP])

        def consume(i, j, s_ref, slot, masked):
            t = (i * (i + 1)) // 2 + j
            vt_blk = vt_ref[j]
            for a in range(2):
                st = s_ref[a]
                if masked:
                    ki = lax.broadcasted_iota(jnp.int32, st.shape, 0)
                    qi = lax.broadcasted_iota(jnp.int32, st.shape, 1)
                    st = jnp.where(ki <= qi, st, -jnp.inf)
                m_prev = m_ref[i, a:a + 1, :]
                m_new = jnp.maximum(m_prev, jnp.max(st, axis=0, keepdims=True))
                alpha = jnp.exp2(m_prev - m_new)
                pt = jnp.exp2(st - m_new)
                ptb = pt.astype(BF16)
                l_ref[i, a:a + 1, :] = alpha * l_ref[i, a:a + 1, :] + jnp.sum(pt, axis=0, keepdims=True)
                acc_ref[i, a] = alpha * acc_ref[i, a] + _dot(vt_blk, ptb)
                m_ref[i, a:a + 1, :] = m_new
                stage[slot, a] = ptb
                mt_o[t, a:a + 1, :] = m_new
            return t

        def after(i, j):
            wrap = j + 1 >= i
            return jnp.minimum(jnp.where(wrap, i + 1, i), nq - 1), jnp.where(wrap, 0, j + 1)

        if n_off > 0:
            scores(1, 0, s_even)

            def below(u, ij):
                g = lax.rem(u, P_GROUPS)
                wait_group(g)
                i1, j1 = after(*ij)
                scores(i1, j1, s_odd)
                t0 = consume(ij[0], ij[1], s_even, 2 * g, False)
                i2, j2 = after(i1, j1)
                scores(i2, j2, s_even)
                t1 = consume(i1, j1, s_odd, 2 * g + 1, False)
                p_copy(2 * g, t0).start()
                p_copy(2 * g + 1, t1).start()
                return i2, j2

            last = lax.fori_loop(0, n_off // 2, below, (jnp.int32(1), jnp.int32(0)))
            if n_off % 2:
                g = (n_off // 2) % P_GROUPS
                p_copy(2 * g, 0).wait()
                p_copy(2 * g, consume(last[0], last[1], s_even, 2 * g, False)).start()

        scores(0, 0, s_even)
        passes = (n_off + 1) // 2

        def diagonal(u, carry):
            g = lax.rem(u + passes, P_GROUPS)
            wait_group(g)
            i0 = 2 * u
            scores(i0 + 1, i0 + 1, s_odd)
            t0 = consume(i0, i0, s_even, 2 * g, True)
            i2 = jnp.minimum(i0 + 2, nq - 1)
            scores(i2, i2, s_even)
            t1 = consume(i0 + 1, i0 + 1, s_odd, 2 * g + 1, True)
            p_copy(2 * g, t0).start()
            p_copy(2 * g + 1, t1).start()
            return carry

        lax.fori_loop(0, nq // 2, diagonal, 0)
        top = lax.broadcasted_iota(jnp.int32, (HP, b), 0) < 64

        def finish(i, carry):
            rows = pl.ds(pl.multiple_of(i * b, b), b)
            o_o[rows, :] = jnp.where(top, acc_ref[i, 0] / l_ref[i, 0:1, :], acc_ref[i, 1] / l_ref[i, 1:2, :]).T
            lse_o[i] = m_ref[i, 0:2, :] + jnp.log2(l_ref[i, 0:2, :])
            return carry

        lax.fori_loop(0, nq, finish, 0)
        for g in range(P_GROUPS):
            wait_group(g)

    return pl.pallas_call(
        body, name="attn_fwd", grid=(PAIRS,),
        in_specs=[pl.BlockSpec((seq, 2 * HP), lambda p: (0, p)),
                  pl.BlockSpec((seq, 2 * HP), lambda p: (0, p)),
                  pl.BlockSpec((nq, HP, b), lambda p: (0, p, 0))],
        out_specs=[pl.BlockSpec((seq, HP), lambda p: (0, p)),
                   pl.BlockSpec((None, nq, 2, b), lambda p: (p, 0, 0, 0)),
                   pl.BlockSpec((None, n_blk, 2, b), lambda p: (p, 0, 0, 0)),
                   pl.BlockSpec(memory_space=pl.ANY)],
        out_shape=[jax.ShapeDtypeStruct((seq, MLA_W), F32),
                   jax.ShapeDtypeStruct((PAIRS, nq, 2, b), F32),
                   jax.ShapeDtypeStruct((PAIRS, n_blk, 2, b), F32),
                   jax.ShapeDtypeStruct((PAIRS, n_blk + 2 * P_GROUPS, 2, b, b), BF16)],
        scratch_shapes=[pltpu.VMEM((nq, 8, b), F32), pltpu.VMEM((nq, 8, b), F32), pltpu.VMEM((nq, 2, HP, b), F32),
                        pltpu.VMEM((2, b, b), F32), pltpu.VMEM((2, b, b), F32),
                        pltpu.VMEM((2 * P_GROUPS, 2, b, b), BF16), pltpu.SemaphoreType.DMA((2 * P_GROUPS,))],
        compiler_params=pltpu.CompilerParams(dimension_semantics=("arbitrary",), vmem_limit_bytes=VMEM_LIMIT),
    )(q, k, vt)


def _post(x, tgt, o, gate, wout, pvec, wt, wtt, bsp):
    seq = x.shape[0]
    t = POST_TILE
    nt = seq // t

    def body(x_ref, tgt_ref, o_ref, gate_ref, wout_ref, pv_ref, wt_ref, wtt_ref, bsp_ref,
             dh2_o, do_o, dgate_o, gwout_o, gwsp_o, vec_o, sv_ref, dvln_ref, bacc_ref):
        i = pl.program_id(0)

        @pl.when(i == 0)
        def _():
            gwout_o[...] = jnp.zeros_like(gwout_o)
            gwsp_o[...] = jnp.zeros_like(gwsp_o)
            vec_o[...] = jnp.zeros_like(vec_o)
            bacc_ref[...] = jnp.zeros_like(bacc_ref)

        za = gate_ref[:, 0:512].astype(F32)
        u_pre = gate_ref[:, 512:1024].astype(F32)
        v_pre = gate_ref[:, 1024:1536].astype(F32)
        zb = gate_ref[:, 1536:2048].astype(F32)
        sg = pv_ref[PV_SG:PV_SG + 1, 0:GW]
        sb = pv_ref[PV_SB:PV_SB + 1, 0:GW]
        lng = pv_ref[PV_LNG:PV_LNG + 1, :]
        lnb = pv_ref[PV_LNB:PV_LNB + 1, :]
        o = o_ref[...]

        sig_a = _sigmoid(za)
        silu_a = za * sig_a
        u, du, dgv, xh, r, vln, svb, sig_b = _gmlp_fwd(u_pre, v_pre, zb, sg, sb, wt_ref, bsp_ref, sv_ref, t)
        silu_b = zb * sig_b
        sgu = u * svb
        merged = jnp.concatenate([o * silu_a, sgu * silu_b], axis=1).astype(BF16)
        h2 = DN_ALPHA * x_ref[...] + _dot(merged, wout_ref[...])
        xh2, r2 = _ln_stats(h2)
        err = xh2 * lng + lnb - tgt_ref[...]
        d_out = err * (1.0 / D_MODEL)
        vec_o[GV_LNG:GV_LNG + 1, :] += jnp.sum(d_out * xh2, axis=0, keepdims=True)
        vec_o[GV_LNB:GV_LNB + 1, :] += jnp.sum(d_out, axis=0, keepdims=True)
        vec_o[GV_LOSS:GV_LOSS + 1, :] += jnp.sum(err * err, axis=0, keepdims=True) * (0.5 / D_MODEL)

        d_h2 = _ln_bwd(d_out, lng, xh2, r2)
        dh2_o[...] = d_h2
        dh2b = d_h2.astype(BF16)
        gwout_o[...] += _dot_tn(merged, dh2b)
        d_m = _dot_nt(dh2b, wout_ref[...])
        d_oa = d_m[:, 0:512]
        d_ob = d_m[:, 512:1024]
        do_o[...] = (d_oa * silu_a).astype(BF16)
        dgate_o[:, 0:512] = (d_oa * o * (sig_a * (1.0 + za * (1.0 - sig_a)))).astype(BF16)
        dgate_o[:, 1536:2048] = (d_ob * sgu * (sig_b * (1.0 + zb * (1.0 - sig_b)))).astype(BF16)
        d_sgu = d_ob * silu_b
        dgate_o[:, 512:1024] = (d_sgu * svb * du).astype(BF16)
        d_sv = d_sgu * u
        acc = bacc_ref[...]
        for c in range(t // CHUNK):
            acc = acc + d_sv[c * CHUNK:(c + 1) * CHUNK, :]
        bacc_ref[...] = acc
        d_svb = d_sv.astype(BF16)
        for c in range(t // CHUNK):
            for p in range(PAIRS):
                blk = d_svb[c * CHUNK:(c + 1) * CHUNK, p * HP:(p + 1) * HP]
                vblk = vln[c * CHUNK:(c + 1) * CHUNK, p * HP:(p + 1) * HP]
                first = _lane_lt64(blk.shape)
                gwsp_o[2 * p] += _dot_nt(jnp.where(first, blk, jnp.zeros_like(blk)), vblk)
                gwsp_o[2 * p + 1] += _dot_nt(jnp.where(first, jnp.zeros_like(blk), blk), vblk)
        _spatial_mix(wtt_ref, d_svb, dvln_ref, t)
        d_vln = dvln_ref[...]
        vec_o[GV_SG:GV_SG + 1, 0:GW] += jnp.sum(d_vln * xh, axis=0, keepdims=True)
        vec_o[GV_SB:GV_SB + 1, 0:GW] += jnp.sum(d_vln, axis=0, keepdims=True)
        dgate_o[:, 1024:1536] = (_ln_bwd(d_vln, sg, xh, r) * dgv).astype(BF16)


        @pl.when(i == nt - 1)
        def _():
            tri = (lax.broadcasted_iota(jnp.int32, (CHUNK, CHUNK), 1)
                   <= lax.broadcasted_iota(jnp.int32, (CHUNK, CHUNK), 0))
            for h in range(HEADS):
                gwsp_o[h] = jnp.where(tri, gwsp_o[h], 0.0)
            lane = lax.broadcasted_iota(jnp.int32, (CHUNK, HP), 1)
            res = jnp.zeros((CHUNK, HP), F32)
            for h in range(HEADS):
                p, a = divmod(h, 2)
                blk = bacc_ref[:, p * HP:(p + 1) * HP]
                part = jnp.where(_lane_lt64(blk.shape) == (a == 0), blk, 0.0)
                res = jnp.where(lane == h, jnp.sum(part, axis=-1, keepdims=True), res)
            vec_o[GV_BSP:GV_BSP + HEADS, 0:HP] = res.T[0:HEADS, :]
            lane1 = lax.broadcasted_iota(jnp.int32, (1, D_MODEL), 1)
            total = jnp.sum(vec_o[GV_LOSS:GV_LOSS + 1, :], axis=-1, keepdims=True)
            vec_o[GV_LOSS:GV_LOSS + 1, :] = jnp.where(lane1 == 0, total, 0.0)

    tile = lambda w: pl.BlockSpec((t, w), lambda i: (i, 0))
    full = lambda a: pl.BlockSpec(a.shape, lambda i: (0,) * a.ndim)
    const = lambda s: pl.BlockSpec(s, lambda i: (0,) * len(s))
    return pl.pallas_call(
        body, name="post", grid=(nt,),
        in_specs=[tile(D_MODEL), tile(D_MODEL), tile(MLA_W), tile(2048), full(wout), full(pvec),
                  full(wt), full(wtt), full(bsp)],
        out_specs=[tile(D_MODEL), tile(MLA_W), tile(2048), const((D_MODEL, D_MODEL)),
                   const((HEADS, CHUNK, CHUNK)), const((GV_ROWS, D_MODEL))],
        out_shape=[jax.ShapeDtypeStruct((seq, D_MODEL), F32), jax.ShapeDtypeStruct((seq, MLA_W), BF16),
                   jax.ShapeDtypeStruct((seq, 2048), BF16), jax.ShapeDtypeStruct((D_MODEL, D_MODEL), F32),
                   jax.ShapeDtypeStruct((HEADS, CHUNK, CHUNK), F32), jax.ShapeDtypeStruct((GV_ROWS, D_MODEL), F32)],
        scratch_shapes=[pltpu.VMEM((t, GW), F32), pltpu.VMEM((t, GW), F32), pltpu.VMEM((CHUNK, GW), F32)],
        compiler_params=pltpu.CompilerParams(dimension_semantics=("arbitrary",), vmem_limit_bytes=VMEM_LIMIT),
    )(x, tgt, o, gate, wout, pvec, wt, wtt, bsp)


def _attn_bwd(q, k, v, do, o, lse, cs, pvec, mt, p_all):
    seq = q.shape[0]
    b = ATT_BLK
    nq = seq // b

    def body(q_ref, k_ref, v_ref, do_ref, o_ref, lse_ref, cs_ref, pv_ref, mt_ref, p_hbm, dq_o, dk_o, dv_o, dk_acc, dv_acc,
             pbuf, sems):
        pair = pl.program_id(0)
        i = pl.program_id(1)
        base = (i * (i + 1)) // 2

        def p_copy(slot, t):
            return pltpu.make_async_copy(p_hbm.at[pair, t], pbuf.at[slot], sems.at[slot])

        for ahead in range(P_AHEAD):
            p_copy(ahead, base + jnp.minimum(ahead, i)).start()

        @pl.when(i == 0)
        def _():
            dk_acc[...] = jnp.zeros_like(dk_acc)
            dv_acc[...] = jnp.zeros_like(dv_acc)

        first = _lane_lt64((b, HP))
        do = do_ref[...]
        zero = jnp.zeros_like(do)
        dos = [jnp.where(first, do, zero), jnp.where(first, zero, do)]
        prod_t = (do.astype(F32) * o_ref[...]).T
        deltas = [jnp.sum(prod_t[0:64, :], axis=0, keepdims=True),
                  jnp.sum(prod_t[64:128, :], axis=0, keepdims=True)]
        lses = [lse_ref[0:1, :], lse_ref[1:2, :]]
        qs = [q_ref[:, a * HP:(a + 1) * HP] for a in range(2)]

        def step(j, dqs):
            slot = lax.rem(j, P_AHEAD + 1)
            p_copy(slot, base + j).wait()
            p_copy(lax.rem(j + P_AHEAD, P_AHEAD + 1), base + jnp.minimum(j + P_AHEAD, i)).start()
            rows = pl.ds(pl.multiple_of(j * b, b), b)
            vb = v_ref[rows, :]
            new_dq = []
            dvs = []
            for a in range(2):
                kb = k_ref[rows, a * HP:(a + 1) * HP]
                pt = pbuf[slot, a].astype(F32) * jnp.exp2(mt_ref[base + j, a:a + 1, :] - lses[a])
                dvs.append(_dot(pt.astype(BF16), do))
                dpt = _dot_nt(vb, dos[a])
                dst = (pt * (dpt - deltas[a])).astype(BF16)
                dk_acc[rows, a * HP:(a + 1) * HP] += _dot(dst, qs[a])
                new_dq.append(dqs[a] + _dot_tn(dst, kb))
            dv_acc[rows, :] += jnp.where(first, dvs[0], dvs[1])
            return tuple(new_dq)

        init = (jnp.zeros((b, HP), F32), jnp.zeros((b, HP), F32))
        dqs = lax.fori_loop(0, i + 1, step, init)
        for ahead in range(1, P_AHEAD + 1):
            p_copy(lax.rem(i + ahead, P_AHEAD + 1), base + i).wait()
        cos = cs_ref[:, 0:HP]
        sin = cs_ref[:, HP:2 * HP]
        s1 = sin * pv_ref[PV_M1:PV_M1 + 1, 0:HP]
        s2 = sin * pv_ref[PV_M2:PV_M2 + 1, 0:HP]
        for a in range(2):
            dq_o[:, a * HP:(a + 1) * HP] = _rope_bwd(dqs[a] * SCALE, cos, s1, s2).astype(BF16)

        @pl.when(i == nq - 1)
        def _():
            dk_o[...] = (dk_acc[...] * (SCALE / SCALE_LOG2E)).astype(BF16)
            dv_o[...] = dv_acc[...].astype(BF16)

    return pl.pallas_call(
        body, name="attn_bwd", grid=(PAIRS, nq),
        in_specs=[pl.BlockSpec((b, 2 * HP), lambda p, i: (i, p)),
                  pl.BlockSpec((seq, 2 * HP), lambda p, i: (0, p)),
                  pl.BlockSpec((seq, HP), lambda p, i: (0, p)),
                  pl.BlockSpec((b, HP), lambda p, i: (i, p)),
                  pl.BlockSpec((b, HP), lambda p, i: (i, p)),
                  pl.BlockSpec((None, None, 2, b), lambda p, i: (p, i, 0, 0)),
                  pl.BlockSpec((b, 2 * HP), lambda p, i: (i, 0)),
                  pl.BlockSpec(pvec.shape, lambda p, i: (0, 0)),
                  pl.BlockSpec((None,) + mt.shape[1:], lambda p, i: (p, 0, 0, 0)),
                  pl.BlockSpec(memory_space=pl.ANY)],
        out_specs=[pl.BlockSpec((b, 2 * HP), lambda p, i: (i, p)),
                   pl.BlockSpec((seq, 2 * HP), lambda p, i: (0, p)),
                   pl.BlockSpec((seq, HP), lambda p, i: (0, p))],
        out_shape=[jax.ShapeDtypeStruct((seq, HEADS * HP), BF16),
                   jax.ShapeDtypeStruct((seq, HEADS * HP), BF16),
                   jax.ShapeDtypeStruct((seq, MLA_W), BF16)],
        scratch_shapes=[pltpu.VMEM((seq, 2 * HP), F32), pltpu.VMEM((seq, HP), F32),
                        pltpu.VMEM((P_AHEAD + 1, 2, b, b), BF16), pltpu.SemaphoreType.DMA((P_AHEAD + 1,))],
        compiler_params=pltpu.CompilerParams(dimension_semantics=("arbitrary", "arbitrary"),
                                             vmem_limit_bytes=VMEM_LIMIT),
    )(q, k, v, do, o, lse, cs, pvec, mt, p_all)


def _bwd_pre(x, dh2, cq, ckv, cs, dq, dk, dv, dgate, win, wuq, wkv, pvec, gvec):
    seq = x.shape[0]
    t = BWD_TILE

    def body(x_ref, dh2_ref, cq_ref, ckv_ref, cs_ref, dq_ref, dk_ref, dv_ref, dgate_ref,
             win_ref, wuq_ref, wkv_ref, pv_ref, gv_ref, gx_o, gwin_o, gwuq_o, gwkv_o, vec_o):
        i = pl.program_id(0)

        @pl.when(i == 0)
        def _():
            gwin_o[...] = jnp.zeros_like(gwin_o)
            gwuq_o[...] = jnp.zeros_like(gwuq_o)
            gwkv_o[...] = jnp.zeros_like(gwkv_o)
            vec_o[...] = gv_ref[...]

        xb = x_ref[...].astype(BF16)
        dgate = dgate_ref[...]
        gwin_o[:, C_GATE:D_INR] += _dot_tn(xb, dgate)
        gx_gate = _dot_nt(dgate, win_ref[:, C_GATE:D_INR])

        qg = pv_ref[PV_QG:PV_QG + 1, 0:Q_LORA]
        kvg = pv_ref[PV_KVG:PV_KVG + 1, 0:KV_LORA]
        dq = dq_ref[...]
        cqh, rq = _rms_stats(cq_ref[...])
        d_cqn = _dot_nt(dq, wuq_ref[...])
        gwuq_o[...] += _dot_tn((cqh * qg).astype(BF16), dq)
        vec_o[GV_QG:GV_QG + 1, 0:Q_LORA] += jnp.sum(d_cqn * cqh, axis=0, keepdims=True)
        d_cq = _rms_bwd(d_cqn, qg, cqh, rq)

        dk = dk_ref[...]
        dkv = jnp.concatenate([dk, dv_ref[...]], axis=1)
        ckvh, rkv = _rms_stats(ckv_ref[...])
        d_ckvn = _dot_nt(dkv, wkv_ref[...])
        gwkv_o[...] += _dot_tn((ckvh * kvg).astype(BF16), dkv)
        vec_o[GV_KVG:GV_KVG + 1, 0:KV_LORA] += jnp.sum(d_ckvn * ckvh, axis=0, keepdims=True)
        d_ckv = _rms_bwd(d_ckvn, kvg, ckvh, rkv)

        dks = dk[:, 0:HP].astype(F32)
        for h in range(1, HEADS):
            dks = dks + dk[:, h * HP:(h + 1) * HP].astype(F32)
        cos = cs_ref[:, 0:HP]
        sin = cs_ref[:, HP:2 * HP]
        d_kr = _rope_bwd(dks, cos, sin * pv_ref[PV_M1:PV_M1 + 1, 0:HP], sin * pv_ref[PV_M2:PV_M2 + 1, 0:HP])

        d_lat = jnp.concatenate([d_cq.astype(BF16), d_ckv.astype(BF16), d_kr.astype(BF16)], axis=1)
        gwin_o[:, 0:C_GATE] += _dot_tn(xb, d_lat)
        gx_o[...] = DN_ALPHA * dh2_ref[...] + gx_gate + _dot_nt(d_lat, win_ref[:, 0:C_GATE])

    tile = lambda w: pl.BlockSpec((t, w), lambda i: (i, 0))
    full = lambda a: pl.BlockSpec(a.shape, lambda i: (0,) * a.ndim)
    const = lambda s: pl.BlockSpec(s, lambda i: (0,) * len(s))
    return pl.pallas_call(
        body, name="bwd_pre", grid=(seq // t,),
        in_specs=[tile(D_MODEL), tile(D_MODEL), tile(Q_LORA), tile(KV_LORA), tile(2 * HP), tile(HEADS * HP),
                  tile(HEADS * HP), tile(MLA_W), tile(2048), full(win), full(wuq), full(wkv), full(pvec), full(gvec)],
        out_specs=[tile(D_MODEL), const((D_MODEL, D_INR)), const((Q_LORA, HEADS * HP)),
                   const((KV_LORA, HEADS * HP + MLA_W)), const((GV_ROWS, D_MODEL))],
        out_shape=[jax.ShapeDtypeStruct((seq, D_MODEL), F32), jax.ShapeDtypeStruct((D_MODEL, D_INR), F32),
                   jax.ShapeDtypeStruct((Q_LORA, HEADS * HP), F32),
                   jax.ShapeDtypeStruct((KV_LORA, HEADS * HP + MLA_W), F32),
                   jax.ShapeDtypeStruct((GV_ROWS, D_MODEL), F32)],
        compiler_params=pltpu.CompilerParams(dimension_semantics=("arbitrary",), vmem_limit_bytes=VMEM_LIMIT),
    )(x, dh2, cq, ckv, cs, dq, dk, dv, dgate, win, wuq, wkv, pvec, gvec)


def _grad_reduce(gs, gvec):
    n_arr = len(gs)
    n_big = n_arr - 1
    k1 = lambda n, blk: 4 * n + blk
    k2 = lambda n, kk: 4 * n_arr + 3 * n + kk
    k3 = lambda n: 7 * n_arr + n
    k3w = lambda k: 7 * n_arr + n_big + k
    kv = lambda k: 7 * n_arr + n_big + 7 + k
    n_sem = 7 * n_arr + n_big + 14

    def body(*refs):
        g, gv = refs[0:n_arr], refs[n_arr]
        outs, ov = refs[n_arr + 1:2 * n_arr + 1], refs[2 * n_arr + 1]
        r1 = refs[2 * n_arr + 2:3 * n_arr + 2]
        r2 = refs[3 * n_arr + 2:4 * n_arr + 2]
        s2 = refs[4 * n_arr + 2:5 * n_arr + 2]
        vbuf, send_sems, recv_sems = refs[5 * n_arr + 2:]
        x, y, c = lax.axis_index("x"), lax.axis_index("y"), lax.axis_index("c")
        j = 2 * x + y
        me = 2 * j + c
        sib = (x, y, 1 - c)
        chips = [(1 - x, y), (x, 1 - y), (1 - x, 1 - y)]
        others = [sib] + [(px, py, pc) for (px, py) in chips for pc in (c, 1 - c)]

        def copy(k, src, dst, to):
            return pltpu.make_async_remote_copy(
                src_ref=src, dst_ref=dst, send_sem=send_sems.at[k], recv_sem=recv_sems.at[k],
                device_id=to, device_id_type=MESH)

        l1 = [copy(k1(n, blk), g[n].at[blk, 1 - c], r1[n].at[blk], sib) for n in range(n_arr) for blk in range(4)]
        lv = [copy(kv(k), gv, vbuf.at[me], to) for k, to in enumerate(others)]
        for cp in l1 + lv:
            cp.start()
        l2 = []
        for n in range(n_arr):
            for blk in range(4):
                copy(k1(n, blk), g[n].at[blk, c], r1[n].at[blk], sib).wait_recv()
            for blk in range(4):
                r1[n][blk] = g[n][blk, c] + r1[n][blk]
                s2[n][blk] = r1[n][blk].astype(BF16)
            for kk, (px, py) in enumerate(chips):
                l2.append(copy(k2(n, kk), s2[n].at[2 * px + py], r2[n].at[kk], (px, py, c)))
                l2[-1].start()

        l3 = []
        for n in range(n_arr):
            for kk in range(3):
                copy(k2(n, kk), s2[n].at[0], r2[n].at[kk], sib).wait_recv()
            red = ((r1[n][j] + r2[n][0].astype(F32)) + r2[n][1].astype(F32)) + r2[n][2].astype(F32)
            if n < n_big:
                outs[n][c] = red
                back = [copy(k3(n), outs[n].at[c], outs[n].at[c], sib)]
            else:
                outs[n][j, c] = red
                back = [copy(k3w(k), outs[n].at[j, c], outs[n].at[j, c], to) for k, to in enumerate(others)]
            for cp in back:
                cp.start()
            l3 += back
        for n in range(n_big):
            copy(k3(n), outs[n].at[1 - c], outs[n].at[1 - c], sib).wait_recv()
        for k, (px, py, pc) in enumerate(others):
            landed = outs[n_big].at[2 * px + py, pc]
            copy(k3w(k), landed, landed, (px, py, pc)).wait_recv()
            copy(kv(k), gv, vbuf.at[4 * px + 2 * py + pc], (px, py, pc)).wait_recv()
        vbuf[me] = gv[...]
        total = vbuf[0]
        for d in range(1, 8):
            total = total + vbuf[d]
        ov[...] = total
        for cp in l1 + lv + l2 + l3:
            cp.wait_send()

    vmem = pl.BlockSpec(memory_space=pltpu.VMEM)
    half_shapes = [a.shape[2:] for a in gs]
    out_shape = [jax.ShapeDtypeStruct((2,) + s, F32) for s in half_shapes[:n_big]]
    out_shape += [jax.ShapeDtypeStruct((4, 2) + half_shapes[n_big], F32), jax.ShapeDtypeStruct(gvec.shape, F32)]
    scratch = [pltpu.VMEM((4,) + s, F32) for s in half_shapes] + [pltpu.VMEM((3,) + s, BF16) for s in half_shapes]
    scratch += [pltpu.VMEM((4,) + s, BF16) for s in half_shapes]
    scratch += [pltpu.VMEM((8,) + gvec.shape, F32), pltpu.SemaphoreType.DMA((n_sem,)), pltpu.SemaphoreType.DMA((n_sem,))]
    return pl.pallas_call(
        body, name="grad_reduce", out_shape=out_shape,
        in_specs=[vmem] * (n_arr + 1), out_specs=[vmem] * (n_arr + 1), scratch_shapes=scratch,
        compiler_params=pltpu.CompilerParams(vmem_limit_bytes=VMEM_LIMIT),
    )(*gs, gvec)


SMALL_ROWS = ((GV_QG, 1, Q_LORA), (GV_KVG, 1, KV_LORA), (GV_SG, 1, GW), (GV_SB, 1, GW),
              (GV_LNG, 1, D_MODEL), (GV_LNB, 1, D_MODEL), (GV_BSP, HEADS, CHUNK))


def _adam_update(g, w, m, v):
    m_new = ADAM_B1 * m + (1.0 - ADAM_B1) * g
    v_new = ADAM_B2 * v + (1.0 - ADAM_B2) * (g * g)
    m_hat = m_new / (1.0 - ADAM_B1 ** ADAM_STEP)
    v_hat = v_new / (1.0 - ADAM_B2 ** ADAM_STEP)
    return -ADAM_LR * (m_hat / (jnp.sqrt(v_hat) + ADAM_EPS) + ADAM_WD * w), m_new, v_new


def _adamw(g_big, w_big, m_big, v_big, gvec, w_small, m_small, v_small):
    nb, ns = len(g_big), len(w_small)

    def body(*refs):
        it = iter(refs)
        take = lambda n: [next(it) for _ in range(n)]
        g_b, w_b, m_b, v_b = take(nb), take(nb), take(nb), take(nb)
        gv = next(it)
        w_s, m_s, v_s = take(ns), take(ns), take(ns)
        g_bo, d_bo, m_bo, v_bo = take(nb), take(nb), take(nb), take(nb)
        g_so, d_so, m_so, v_so = take(ns), take(ns), take(ns), take(ns)
        for n in range(nb):
            gb = g_b[n][...]
            g_bo[n][...] = gb
            d_bo[n][...], m_bo[n][...], v_bo[n][...] = _adam_update(gb, w_b[n][...], m_b[n][...], v_b[n][...])
        for n, (row, nrow, width) in enumerate(SMALL_ROWS):
            gs = gv[row:row + nrow, 0:width]
            g_so[n][...] = gs
            d_so[n][...], m_so[n][...], v_so[n][...] = _adam_update(gs, w_s[n][...], m_s[n][...], v_s[n][...])

    def rows(a):
        nd = a.ndim
        return pl.BlockSpec((a.shape[0] // ADAM_STEPS,) + a.shape[1:], lambda i: (i,) + (0,) * (nd - 1))

    def whole(a):
        nd = a.ndim
        return pl.BlockSpec(a.shape, lambda i: (0,) * nd)

    big = [jax.ShapeDtypeStruct(a.shape, F32) for a in w_big]
    small = [jax.ShapeDtypeStruct(a.shape, F32) for a in w_small]
    return pl.pallas_call(
        body, name="adamw", grid=(ADAM_STEPS,), out_shape=big * 4 + small * 4,
        in_specs=[rows(a) for a in g_big + w_big + m_big + v_big] + [whole(gvec)]
        + [whole(a) for a in w_small + m_small + v_small],
        out_specs=[rows(a) for a in w_big] * 4 + [whole(a) for a in w_small] * 4,
        compiler_params=pltpu.CompilerParams(dimension_semantics=("arbitrary",), vmem_limit_bytes=VMEM_LIMIT),
    )(*g_big, *w_big, *m_big, *v_big, gvec, *w_small, *m_small, *v_small)


def kernel(x, positions, w_in, q_norm_g, w_uq, kv_norm_g, w_ukv, sgu_norm_g, sgu_norm_b, w_spatial, b_spatial, w_out, ln_g, ln_b, loss_target, m_w_in, m_q_norm_g, m_w_uq, m_kv_norm_g, m_w_ukv, m_sgu_norm_g, m_sgu_norm_b, m_w_spatial, m_b_spatial, m_w_out, m_ln_g, m_ln_b, v_w_in, v_q_norm_g, v_w_uq, v_kv_norm_g, v_w_ukv, v_sgu_norm_g, v_sgu_norm_b, v_w_spatial, v_b_spatial, v_w_out, v_ln_g, v_ln_b):
    seq = x.shape[1]
    x2 = x.reshape(seq, D_MODEL)
    tgt = loss_target.reshape(seq, D_MODEL)
    pos = positions.reshape(seq, 1)

    a_in, a_uq, a_ukv, a_out = _weight_gather([w_in, w_uq, w_ukv, w_out])
    w_uq_f = jnp.swapaxes(a_uq, 0, 1).reshape(Q_LORA, HEADS * (NOPE + ROPE))
    w_ukv_f = jnp.swapaxes(a_ukv, 0, 1).reshape(KV_LORA, HEADS * (NOPE + VDIM))
    wout = a_out.reshape(D_MODEL, D_MODEL)
    zc = lambda n: jnp.zeros((D_MODEL, n), BF16)
    win = jnp.concatenate([a_in[0][:, 0:C_KR], zc(NOPE), a_in[0][:, C_KR:C_KR + ROPE], zc(HP - NOPE - ROPE),
                           a_in[0][:, C_KR + ROPE:],
                           a_in[1], a_in[2], a_in[3]], axis=1)
    wuq = jnp.pad(w_uq_f.reshape(Q_LORA, HEADS, NOPE + ROPE), ((0, 0), (0, 0), (0, HP - NOPE - ROPE)))
    wuq = wuq.reshape(Q_LORA, HEADS * HP)
    ukv = w_ukv_f.reshape(KV_LORA, HEADS, NOPE + VDIM)
    wk = jnp.pad(ukv[:, :, 0:NOPE], ((0, 0), (0, 0), (0, HP - NOPE))).reshape(KV_LORA, HEADS * HP)
    wkv = jnp.concatenate([wk, ukv[:, :, NOPE:].reshape(KV_LORA, MLA_W)], axis=1)

    lane = np.arange(HP)
    half = ROPE // 2
    inv_freq = (1.0 / (ROPE_THETA ** (np.arange(half, dtype=np.float32) / half))).astype(np.float32)
    in_rope = (lane >= NOPE) & (lane < NOPE + ROPE)
    invf = jnp.asarray(np.where(in_rope, inv_freq[(lane - NOPE) % half], 0.0).astype(np.float32))
    m1 = jnp.asarray(np.where((lane >= NOPE) & (lane < NOPE + half), -1.0, 0.0).astype(np.float32))
    m2 = jnp.asarray(np.where((lane >= NOPE + half) & (lane < NOPE + ROPE), 1.0, 0.0).astype(np.float32))
    row = lambda a: jnp.pad(a.astype(F32), (0, D_MODEL - a.shape[0]))
    pvec = jnp.stack([row(q_norm_g), row(kv_norm_g), row(sgu_norm_g), row(sgu_norm_b), row(invf), row(m1),
                      row(m2), row(ln_g), row(ln_b)] + [jnp.zeros((D_MODEL,), F32)] * (PV_ROWS - 9))
    tri = jnp.tril(jnp.ones((CHUNK, CHUNK), dtype=bool))
    wt = jnp.where(tri[None], w_spatial, 0.0).astype(BF16)
    wtt = jnp.swapaxes(wt, 1, 2)
    bsp = jnp.repeat(b_spatial.T, VDIM, axis=1)

    cq, ckv, gate, q, k, v, vt, cs = _fwd_pre(x2, pos, win, wuq, wkv, pvec)
    o, lse, mt, p_all = _attn_fwd(q, k, vt)
    dh2, do, dgate, g_wout, g_wsp, gvec = _post(x2, tgt, o, gate, wout, pvec, wt, wtt, bsp)
    dq, dk, dv = _attn_bwd(q, k, v, do, o, lse, cs, pvec, mt, p_all)
    gx, g_win, g_wuq, g_wkv, gvec = _bwd_pre(x2, dh2, cq, ckv, cs, dq, dk, dv, dgate, win, wuq, wkv, pvec, gvec)

    cw = w_in.shape[1]
    first = D_INR - 3 * cw
    g_win_0 = jnp.concatenate([g_win[:, 0:C_KR], g_win[:, C_KR + NOPE:C_KR + NOPE + ROPE], g_win[:, C_GATE:first]],
                              axis=1)
    g_win_b = jnp.stack([g_win_0] + [g_win[:, first + cw * jb:first + cw * (jb + 1)] for jb in range(3)])
    g_wuq_f = g_wuq.reshape(Q_LORA, HEADS, HP)[:, :, 0:NOPE + ROPE].reshape(Q_LORA, HEADS * (NOPE + ROPE))
    g_k = g_wkv[:, 0:HEADS * HP].reshape(KV_LORA, HEADS, HP)[:, :, 0:NOPE]
    g_v = g_wkv[:, HEADS * HP:].reshape(KV_LORA, HEADS, VDIM)
    g_wukv_f = jnp.concatenate([g_k, g_v], axis=2).reshape(KV_LORA, HEADS * (NOPE + VDIM))

    def by_chip(a):
        rows, cols = a.shape[0], a.shape[1] // 4
        return jnp.swapaxes(a.reshape(rows, 4, cols), 0, 1).reshape(4, 2, rows // 2, cols)

    gs = [g_win_b.reshape(4, 2, D_MODEL // 2, cw), by_chip(g_wuq_f), by_chip(g_wukv_f), g_wout.reshape(4, 2, 128, D_MODEL),
          g_wsp.reshape(4, 2, CHUNK, CHUNK)]
    r_in, r_uq, r_ukv, r_out, r_wsp, r_vec = _grad_reduce(gs, gvec)

    g_big = [r_in.reshape(w_in.shape), r_uq.reshape(w_uq.shape), r_ukv.reshape(w_ukv.shape),
             r_out.reshape(w_out.shape), r_wsp.reshape(w_spatial.shape)]
    small = lambda qg, kvg, sg, sb, lng, lnb, bs: [qg.reshape(1, -1), kvg.reshape(1, -1), sg.reshape(1, -1),
                                                   sb.reshape(1, -1), lng.reshape(1, -1), lnb.reshape(1, -1), bs]
    res = _adamw(g_big, [w_in, w_uq, w_ukv, w_out, w_spatial], [m_w_in, m_w_uq, m_w_ukv, m_w_out, m_w_spatial],
                 [v_w_in, v_w_uq, v_w_ukv, v_w_out, v_w_spatial], r_vec,
                 small(q_norm_g, kv_norm_g, sgu_norm_g, sgu_norm_b, ln_g, ln_b, b_spatial),
                 small(m_q_norm_g, m_kv_norm_g, m_sgu_norm_g, m_sgu_norm_b, m_ln_g, m_ln_b, m_b_spatial),
                 small(v_q_norm_g, v_kv_norm_g, v_sgu_norm_g, v_sgu_norm_b, v_ln_g, v_ln_b, v_b_spatial))

    def ordered(big, sm):
        vec = lambda n: sm[n].reshape(-1)
        return [big[0], vec(0), big[1], vec(1), big[2], vec(2), vec(3), big[4], sm[6], big[3], vec(4), vec(5)]

    loss = r_vec[GV_LOSS, 0]
    return (loss, gx.reshape(1, seq, D_MODEL), *ordered(res[0:5], res[20:27]), *ordered(res[5:10], res[27:34]),
            *ordered(res[10:15], res[34:41]), *ordered(res[15:20], res[41:48]))
```

```python
import math

import jax
import jax.numpy as jnp
import numpy as np
from jax import lax
from jax.experimental import pallas as pl
from jax.experimental.pallas import tpu as pltpu

F32 = jnp.float32
BF16 = jnp.bfloat16

D_MODEL = 1024
Q_LORA = 256
KV_LORA = 128
HEADS = 8
NOPE = 64
ROPE = 32
VDIM = 64
MLA_W = HEADS * VDIM
GW = 512
CHUNK = 128
HP = 128
PAIRS = HEADS // 2
D_IN = 2464
D_INR = 2560
C_CKV = Q_LORA
C_KR = Q_LORA + KV_LORA
C_GATE = C_KR + HP
ROPE_THETA = 10000.0
DN_ALPHA = 2.0 ** 0.25
EPS = 1e-5
SCALE = 1.0 / math.sqrt(NOPE + ROPE)
SCALE_LOG2E = SCALE * 1.4426950408889634
INV_SQRT2 = 0.7071067811865476
INV_SQRT_2PI = 0.3989422804014327

ADAM_LR = 0.001
ADAM_B1 = 0.9
ADAM_B2 = 0.999
ADAM_EPS = 1e-08
ADAM_WD = 0.01
ADAM_STEP = 10

PV_QG, PV_KVG, PV_SG, PV_SB, PV_INVF, PV_M1, PV_M2, PV_LNG, PV_LNB = range(9)
PV_ROWS = 16
GV_QG, GV_KVG, GV_SG, GV_SB, GV_LNG, GV_LNB, GV_LOSS = range(7)
GV_BSP = 8
GV_ROWS = 16

MESH = pl.DeviceIdType.MESH

FWD_TILE = 512
POST_TILE = 512
BWD_TILE = 512
ATT_BLK = 512
ADAM_STEPS = 4
VMEM_LIMIT = 56 * 1024 * 1024


def _dot(a, b):
    return jnp.dot(a, b, preferred_element_type=F32)


def _dot_nt(a, b):
    return lax.dot_general(a, b, (((1,), (1,)), ((), ())), preferred_element_type=F32)


def _dot_tn(a, b):
    return lax.dot_general(a, b, (((0,), (0,)), ((), ())), preferred_element_type=F32)


def _sigmoid(z):
    return pl.reciprocal(1.0 + jnp.exp(-z), approx=True)


def _gelu_and_grad(x):
    cdf = 0.5 * (1.0 + lax.erf(x * INV_SQRT2))
    return x * cdf, cdf + x * (INV_SQRT_2PI * jnp.exp(-0.5 * x * x))


def _rms_stats(x):
    r = lax.rsqrt(jnp.mean(x * x, axis=-1, keepdims=True) + EPS)
    return x * r, r


def _rms_bwd(dy, g, xh, r):
    dyg = dy * g
    return r * (dyg - xh * jnp.mean(dyg * xh, axis=-1, keepdims=True))


def _ln_stats(x):
    mu = jnp.mean(x, axis=-1, keepdims=True)
    xc = x - mu
    r = lax.rsqrt(jnp.mean(xc * xc, axis=-1, keepdims=True) + EPS)
    return xc * r, r


def _ln_bwd(dy, g, xh, r):
    dxh = dy * g
    return r * (dxh - jnp.mean(dxh, axis=-1, keepdims=True) - xh * jnp.mean(dxh * xh, axis=-1, keepdims=True))


def _rope_fwd(t, c, s1, s2):
    return t * c + pltpu.roll(t, HP - 16, 1) * s1 + pltpu.roll(t, 16, 1) * s2


def _rope_bwd(d, c, s1, s2):
    return d * c + pltpu.roll(d * s1, 16, 1) + pltpu.roll(d * s2, HP - 16, 1)


def _lane_lt64(shape):
    return lax.broadcasted_iota(jnp.int32, shape, len(shape) - 1) < 64


def _spatial_mix(w_ref, src, dst_ref, rows):
    for c in range(rows // CHUNK):
        for p in range(PAIRS):
            blk = src[c * CHUNK:(c + 1) * CHUNK, p * HP:(p + 1) * HP]
            a = _dot(w_ref[2 * p], blk)
            b = _dot(w_ref[2 * p + 1], blk)
            dst_ref[c * CHUNK:(c + 1) * CHUNK, p * HP:(p + 1) * HP] = jnp.where(_lane_lt64(a.shape), a, b)


def _gmlp_fwd(u_pre, v_pre, zb, sg, sb, wt_ref, bsp_ref, sv_ref, rows):
    u, du = _gelu_and_grad(u_pre)
    gv, dgv = _gelu_and_grad(v_pre)
    xh, r = _ln_stats(gv)
    vln = (xh * sg + sb).astype(BF16)
    _spatial_mix(wt_ref, vln, sv_ref, rows)
    bias = bsp_ref[...]
    svb = sv_ref[...] + jnp.concatenate([bias] * (rows // CHUNK), axis=0)
    sig = _sigmoid(zb)
    return u, du, dgv, xh, r, vln, svb, sig


def _weight_gather(shards):
    n_arr = len(shards)

    def body(*refs):
        ins, outs = refs[0:n_arr], refs[n_arr:2 * n_arr]
        send_sems, recv_sems = refs[2 * n_arr:]
        x, y, c = lax.axis_index("x"), lax.axis_index("y"), lax.axis_index("c")
        j = 2 * x + y
        sib = (x, y, 1 - c)
        chips = [(1 - x, y), (x, 1 - y), (1 - x, 1 - y)]
        for n in range(n_arr):
            outs[n][j] = ins[n][...].astype(BF16)

        def half(n, blk, core):
            r = shards[n].shape[0] // 2
            return outs[n].at[blk, pl.ds(pl.multiple_of(core * r, 16), r), :]

        def copy(k, ref, to):
            return pltpu.make_async_remote_copy(
                src_ref=ref, dst_ref=ref, send_sem=send_sems.at[k], recv_sem=recv_sems.at[k],
                device_id=to, device_id_type=MESH)

        first = [copy(6 * n + kk, half(n, j, c), (px, py, c))
                 for n in range(n_arr) for kk, (px, py) in enumerate(chips)]
        for cp in first:
            cp.start()
        passed = []
        for n in range(n_arr):
            for kk, (px, py) in enumerate(chips):
                landed = half(n, 2 * px + py, c)
                copy(6 * n + kk, landed, (px, py, c)).wait_recv()
                passed.append(copy(6 * n + 3 + kk, landed, sib))
                passed[-1].start()
        for n in range(n_arr):
            for kk, (px, py) in enumerate(chips):
                copy(6 * n + 3 + kk, half(n, 2 * px + py, 1 - c), sib).wait_recv()
        for cp in first + passed:
            cp.wait_send()

    vmem = pl.BlockSpec(memory_space=pltpu.VMEM)
    return pl.pallas_call(
        body, name="weight_gather",
        out_shape=[jax.ShapeDtypeStruct((4,) + a.shape, BF16) for a in shards],
        in_specs=[vmem] * n_arr, out_specs=[vmem] * n_arr,
        scratch_shapes=[pltpu.SemaphoreType.DMA((6 * n_arr,)), pltpu.SemaphoreType.DMA((6 * n_arr,))],
        compiler_params=pltpu.CompilerParams(vmem_limit_bytes=VMEM_LIMIT),
    )(*shards)


def _fwd_pre(x, pos, win, wuq, wkv, pvec):
    seq = x.shape[0]
    t = FWD_TILE

    def body(x_ref, pos_ref, win_ref, wuq_ref, wkv_ref, pv_ref,
             cq_o, ckv_o, gate_o, q_o, k_o, v_o, vt_o, cs_o):
        xb = x_ref[...].astype(BF16)
        proj = _dot(xb, win_ref[:, 0:C_GATE])
        cq = proj[:, 0:C_CKV]
        ckv = proj[:, C_CKV:C_KR]
        kr = proj[:, C_KR:C_GATE]
        cq_o[...] = cq
        ckv_o[...] = ckv

        ang = pos_ref[...].astype(F32) * pv_ref[PV_INVF:PV_INVF + 1, 0:HP]
        cos = jnp.cos(ang)
        sin = jnp.sin(ang)
        cs_o[:, 0:HP] = cos
        cs_o[:, HP:2 * HP] = sin
        s1 = sin * pv_ref[PV_M1:PV_M1 + 1, 0:HP]
        s2 = sin * pv_ref[PV_M2:PV_M2 + 1, 0:HP]

        cqh, _ = _rms_stats(cq)
        q_all = _dot((cqh * pv_ref[PV_QG:PV_QG + 1, 0:Q_LORA]).astype(BF16), wuq_ref[...])
        ckvh, _ = _rms_stats(ckv)
        kv_all = _dot((ckvh * pv_ref[PV_KVG:PV_KVG + 1, 0:KV_LORA]).astype(BF16), wkv_ref[...])
        krr = _rope_fwd(kr, cos, s1, s2)
        for h in range(HEADS):
            sl = slice(h * HP, (h + 1) * HP)
            q_o[:, sl] = (_rope_fwd(q_all[:, sl], cos, s1, s2) * SCALE_LOG2E).astype(BF16)
            k_o[:, sl] = (kv_all[:, sl] + krr).astype(BF16)
        val = kv_all[:, HEADS * HP:].astype(BF16)
        v_o[...] = val
        vt_o[...] = val.T
        gate_o[...] = _dot(xb, win_ref[:, C_GATE:D_INR]).astype(BF16)

    tile = lambda w: pl.BlockSpec((t, w), lambda i: (i, 0))
    full = lambda a: pl.BlockSpec(a.shape, lambda i: (0,) * a.ndim)
    outs = [(Q_LORA, F32), (KV_LORA, F32), (2048, BF16), (HEADS * HP, BF16), (HEADS * HP, BF16), (MLA_W, BF16)]
    per_blk = ATT_BLK // t
    out_specs = [tile(w) for w, _ in outs]
    out_specs += [pl.BlockSpec((None, MLA_W, t), lambda i: (i // per_blk, 0, i % per_blk)), tile(2 * HP)]
    out_shape = [jax.ShapeDtypeStruct((seq, w), d) for w, d in outs]
    out_shape += [jax.ShapeDtypeStruct((seq // ATT_BLK, MLA_W, ATT_BLK), BF16), jax.ShapeDtypeStruct((seq, 2 * HP), F32)]
    return pl.pallas_call(
        body, name="fwd_pre", grid=(seq // t,),
        in_specs=[tile(D_MODEL), tile(1), full(win), full(wuq), full(wkv), full(pvec)],
        out_specs=out_specs, out_shape=out_shape,
        compiler_params=pltpu.CompilerParams(dimension_semantics=("arbitrary",), vmem_limit_bytes=VMEM_LIMIT),
    )(x, pos, win, wuq, wkv, pvec)


def _attn_fwd(q, k, vt):
    seq = q.shape[0]
    b = ATT_BLK
    nq = seq // b
    assert nq % 2 == 0
    n_off = nq * (nq - 1) // 2

    def body(q_ref, k_ref, vt_ref, o_o, lse_o, m_ref, l_ref, acc_ref, s_even, s_odd):
        m_ref[...] = jnp.full(m_ref.shape, -jnp.inf, F32)
        l_ref[...] = jnp.zeros(l_ref.shape, F32)
        acc_ref[...] = jnp.zeros(acc_ref.shape, F32)

        def scores(i, j, s_ref):
            qrows = pl.ds(pl.multiple_of(i * b, b), b)
            krows = pl.ds(pl.multiple_of(j * b, b), b)
            for a in range(2):
                s_ref[a] = _dot_nt(k_ref[krows, a * HP:(a + 1) * HP], q_ref[qrows, a * HP:(a + 1) * HP])

        def consume(i, j, s_ref, masked):
            vt_blk = vt_ref[j]
            for a in range(2):
                st = s_ref[a]
                if masked:
                    ki = lax.broadcasted_iota(jnp.int32, st.shape, 0)
                    qi = lax.broadcasted_iota(jnp.int32, st.shape, 1)
                    st = jnp.where(ki <= qi, st, -jnp.inf)
                m_prev = m_ref[i, a:a + 1, :]
                m_new = jnp.maximum(m_prev, jnp.max(st, axis=0, keepdims=True))
                alpha = jnp.exp2(m_prev - m_new)
                pt = jnp.exp2(st - m_new)
                l_ref[i, a:a + 1, :] = alpha * l_ref[i, a:a + 1, :] + jnp.sum(pt, axis=0, keepdims=True)
                acc_ref[i, a] = alpha * acc_ref[i, a] + _dot(vt_blk, pt.astype(BF16))
                m_ref[i, a:a + 1, :] = m_new

        def after(i, j):
            wrap = j + 1 >= i
            return jnp.minimum(jnp.where(wrap, i + 1, i), nq - 1), jnp.where(wrap, 0, j + 1)

        if n_off > 0:
            scores(1, 0, s_even)

            def below(u, ij):
                i1, j1 = after(*ij)
                scores(i1, j1, s_odd)
                consume(ij[0], ij[1], s_even, False)
                i2, j2 = after(i1, j1)
                scores(i2, j2, s_even)
                consume(i1, j1, s_odd, False)
                return i2, j2

            last = lax.fori_loop(0, n_off // 2, below, (jnp.int32(1), jnp.int32(0)))
            if n_off % 2:
                consume(last[0], last[1], s_even, False)

        scores(0, 0, s_even)

        def diagonal(u, carry):
            i0 = 2 * u
            scores(i0 + 1, i0 + 1, s_odd)
            consume(i0, i0, s_even, True)
            i2 = jnp.minimum(i0 + 2, nq - 1)
            scores(i2, i2, s_even)
            consume(i0 + 1, i0 + 1, s_odd, True)
            return carry

        lax.fori_loop(0, nq // 2, diagonal, 0)
        top = lax.broadcasted_iota(jnp.int32, (HP, b), 0) < 64

        def finish(i, carry):
            rows = pl.ds(pl.multiple_of(i * b, b), b)
            o_o[rows, :] = jnp.where(top, acc_ref[i, 0] / l_ref[i, 0:1, :], acc_ref[i, 1] / l_ref[i, 1:2, :]).T
            lse_o[i] = m_ref[i, 0:2, :] + jnp.log2(l_ref[i, 0:2, :])
            return carry

        lax.fori_loop(0, nq, finish, 0)

    return pl.pallas_call(
        body, name="attn_fwd", grid=(PAIRS,),
        in_specs=[pl.BlockSpec((seq, 2 * HP), lambda p: (0, p)),
                  pl.BlockSpec((seq, 2 * HP), lambda p: (0, p)),
                  pl.BlockSpec((nq, HP, b), lambda p: (0, p, 0))],
        out_specs=[pl.BlockSpec((seq, HP), lambda p: (0, p)),
                   pl.BlockSpec((None, nq, 2, b), lambda p: (p, 0, 0, 0))],
        out_shape=[jax.ShapeDtypeStruct((seq, MLA_W), F32),
                   jax.ShapeDtypeStruct((PAIRS, nq, 2, b), F32)],
        scratch_shapes=[pltpu.VMEM((nq, 8, b), F32), pltpu.VMEM((nq, 8, b), F32), pltpu.VMEM((nq, 2, HP, b), F32),
                        pltpu.VMEM((2, b, b), F32), pltpu.VMEM((2, b, b), F32)],
        compiler_params=pltpu.CompilerParams(dimension_semantics=("arbitrary",), vmem_limit_bytes=VMEM_LIMIT),
    )(q, k, vt)


def _post(x, tgt, o, gate, wout, pvec, wt, wtt, bsp):
    seq = x.shape[0]
    t = POST_TILE
    nt = seq // t

    def body(x_ref, tgt_ref, o_ref, gate_ref, wout_ref, pv_ref, wt_ref, wtt_ref, bsp_ref,
             dh2_o, do_o, dgate_o, gwout_o, gwsp_o, vec_o, sv_ref, dvln_ref, bacc_ref):
        i = pl.program_id(0)

        @pl.when(i == 0)
        def _():
            gwout_o[...] = jnp.zeros_like(gwout_o)
            gwsp_o[...] = jnp.zeros_like(gwsp_o)
            vec_o[...] = jnp.zeros_like(vec_o)
            bacc_ref[...] = jnp.zeros_like(bacc_ref)

        za = gate_ref[:, 0:512].astype(F32)
        u_pre = gate_ref[:, 512:1024].astype(F32)
        v_pre = gate_ref[:, 1024:1536].astype(F32)
        zb = gate_ref[:, 1536:2048].astype(F32)
        sg = pv_ref[PV_SG:PV_SG + 1, 0:GW]
        sb = pv_ref[PV_SB:PV_SB + 1, 0:GW]
        lng = pv_ref[PV_LNG:PV_LNG + 1, :]
        lnb = pv_ref[PV_LNB:PV_LNB + 1, :]
        o = o_ref[...]

        sig_a = _sigmoid(za)
        silu_a = za * sig_a
        u, du, dgv, xh, r, vln, svb, sig_b = _gmlp_fwd(u_pre, v_pre, zb, sg, sb, wt_ref, bsp_ref, sv_ref, t)
        silu_b = zb * sig_b
        sgu = u * svb
        merged = jnp.concatenate([o * silu_a, sgu * silu_b], axis=1).astype(BF16)
        h2 = DN_ALPHA * x_ref[...] + _dot(merged, wout_ref[...])
        xh2, r2 = _ln_stats(h2)
        err = xh2 * lng + lnb - tgt_ref[...]
        d_out = err * (1.0 / D_MODEL)
        vec_o[GV_LNG:GV_LNG + 1, :] += jnp.sum(d_out * xh2, axis=0, keepdims=True)
        vec_o[GV_LNB:GV_LNB + 1, :] += jnp.sum(d_out, axis=0, keepdims=True)
        vec_o[GV_LOSS:GV_LOSS + 1, :] += jnp.sum(err * err, axis=0, keepdims=True) * (0.5 / D_MODEL)

        d_h2 = _ln_bwd(d_out, lng, xh2, r2)
        dh2_o[...] = d_h2
        dh2b = d_h2.astype(BF16)
        gwout_o[...] += _dot_tn(merged, dh2b)
        d_m = _dot_nt(dh2b, wout_ref[...])
        d_oa = d_m[:, 0:512]
        d_ob = d_m[:, 512:1024]
        do_o[...] = (d_oa * silu_a).astype(BF16)
        dgate_o[:, 0:512] = (d_oa * o * (sig_a * (1.0 + za * (1.0 - sig_a)))).astype(BF16)
        dgate_o[:, 1536:2048] = (d_ob * sgu * (sig_b * (1.0 + zb * (1.0 - sig_b)))).astype(BF16)
        d_sgu = d_ob * silu_b
        dgate_o[:, 512:1024] = (d_sgu * svb * du).astype(BF16)
        d_sv = d_sgu * u
        acc = bacc_ref[...]
        for c in range(t // CHUNK):
            acc = acc + d_sv[c * CHUNK:(c + 1) * CHUNK, :]
        bacc_ref[...] = acc
        d_svb = d_sv.astype(BF16)
        for c in range(t // CHUNK):
            for p in range(PAIRS):
                blk = d_svb[c * CHUNK:(c + 1) * CHUNK, p * HP:(p + 1) * HP]
                vblk = vln[c * CHUNK:(c + 1) * CHUNK, p * HP:(p + 1) * HP]
                first = _lane_lt64(blk.shape)
                gwsp_o[2 * p] += _dot_nt(jnp.where(first, blk, jnp.zeros_like(blk)), vblk)
                gwsp_o[2 * p + 1] += _dot_nt(jnp.where(first, jnp.zeros_like(blk), blk), vblk)
        _spatial_mix(wtt_ref, d_svb, dvln_ref, t)
        d_vln = dvln_ref[...]
        vec_o[GV_SG:GV_SG + 1, 0:GW] += jnp.sum(d_vln * xh, axis=0, keepdims=True)
        vec_o[GV_SB:GV_SB + 1, 0:GW] += jnp.sum(d_vln, axis=0, keepdims=True)
        dgate_o[:, 1024:1536] = (_ln_bwd(d_vln, sg, xh, r) * dgv).astype(BF16)


        @pl.when(i == nt - 1)
        def _():
            tri = (lax.broadcasted_iota(jnp.int32, (CHUNK, CHUNK), 1)
                   <= lax.broadcasted_iota(jnp.int32, (CHUNK, CHUNK), 0))
            for h in range(HEADS):
                gwsp_o[h] = jnp.where(tri, gwsp_o[h], 0.0)
            lane = lax.broadcasted_iota(jnp.int32, (CHUNK, HP), 1)
            res = jnp.zeros((CHUNK, HP), F32)
            for h in range(HEADS):
                p, a = divmod(h, 2)
                blk = bacc_ref[:, p * HP:(p + 1) * HP]
                part = jnp.where(_lane_lt64(blk.shape) == (a == 0), blk, 0.0)
                res = jnp.where(lane == h, jnp.sum(part, axis=-1, keepdims=True), res)
            vec_o[GV_BSP:GV_BSP + HEADS, 0:HP] = res.T[0:HEADS, :]
            lane1 = lax.broadcasted_iota(jnp.int32, (1, D_MODEL), 1)
            total = jnp.sum(vec_o[GV_LOSS:GV_LOSS + 1, :], axis=-1, keepdims=True)
            vec_o[GV_LOSS:GV_LOSS + 1, :] = jnp.where(lane1 == 0, total, 0.0)

    tile = lambda w: pl.BlockSpec((t, w), lambda i: (i, 0))
    full = lambda a: pl.BlockSpec(a.shape, lambda i: (0,) * a.ndim)
    const = lambda s: pl.BlockSpec(s, lambda i: (0,) * len(s))
    return pl.pallas_call(
        body, name="post", grid=(nt,),
        in_specs=[tile(D_MODEL), tile(D_MODEL), tile(MLA_W), tile(2048), full(wout), full(pvec),
                  full(wt), full(wtt), full(bsp)],
        out_specs=[tile(D_MODEL), tile(MLA_W), tile(2048), const((D_MODEL, D_MODEL)),
                   const((HEADS, CHUNK, CHUNK)), const((GV_ROWS, D_MODEL))],
        out_shape=[jax.ShapeDtypeStruct((seq, D_MODEL), F32), jax.ShapeDtypeStruct((seq, MLA_W), BF16),
                   jax.ShapeDtypeStruct((seq, 2048), BF16), jax.ShapeDtypeStruct((D_MODEL, D_MODEL), F32),
                   jax.ShapeDtypeStruct((HEADS, CHUNK, CHUNK), F32), jax.ShapeDtypeStruct((GV_ROWS, D_MODEL), F32)],
        scratch_shapes=[pltpu.VMEM((t, GW), F32), pltpu.VMEM((t, GW), F32), pltpu.VMEM((CHUNK, GW), F32)],
        compiler_params=pltpu.CompilerParams(dimension_semantics=("arbitrary",), vmem_limit_bytes=VMEM_LIMIT),
    )(x, tgt, o, gate, wout, pvec, wt, wtt, bsp)


def _attn_bwd(q, k, v, do, o, lse, cs, pvec):
    seq = q.shape[0]
    b = ATT_BLK
    nq = seq // b

    def body(q_ref, k_ref, v_ref, do_ref, o_ref, lse_ref, cs_ref, pv_ref, dq_o, dk_o, dv_o, dk_acc, dv_acc):
        i = pl.program_id(1)

        @pl.when(i == 0)
        def _():
            dk_acc[...] = jnp.zeros_like(dk_acc)
            dv_acc[...] = jnp.zeros_like(dv_acc)

        first = _lane_lt64((b, HP))
        do = do_ref[...]
        zero = jnp.zeros_like(do)
        dos = [jnp.where(first, do, zero), jnp.where(first, zero, do)]
        prod_t = (do.astype(F32) * o_ref[...]).T
        deltas = [jnp.sum(prod_t[0:64, :], axis=0, keepdims=True),
                  jnp.sum(prod_t[64:128, :], axis=0, keepdims=True)]
        lses = [lse_ref[0:1, :], lse_ref[1:2, :]]
        qs = [q_ref[:, a * HP:(a + 1) * HP] for a in range(2)]

        def step(j, dqs, masked, nk=b):
            rows = pl.ds(pl.multiple_of(j * b, b), nk)
            vb = v_ref[rows, :]
            new_dq = []
            dvs = []
            for a in range(2):
                kb = k_ref[rows, a * HP:(a + 1) * HP]
                pt = jnp.exp2(_dot_nt(kb, qs[a]) - lses[a])
                if masked:
                    ki = lax.broadcasted_iota(jnp.int32, pt.shape, 0)
                    qi = lax.broadcasted_iota(jnp.int32, pt.shape, 1)
                    pt = jnp.where(ki <= qi, pt, 0.0)
                dvs.append(_dot(pt.astype(BF16), do))
                dpt = _dot_nt(vb, dos[a])
                dst = (pt * (dpt - deltas[a])).astype(BF16)
                dk_acc[rows, a * HP:(a + 1) * HP] += _dot(dst, qs[a])
                new_dq.append(dqs[a] + _dot_tn(dst, kb))
            dv_acc[rows, :] += jnp.where(_lane_lt64((nk, HP)), dvs[0], dvs[1])
            return tuple(new_dq)

        init = (jnp.zeros((b, HP), F32), jnp.zeros((b, HP), F32))
        dqs = lax.fori_loop(0, i // 2, lambda jj, cr: step(2 * jj, cr, False, 2 * b), init)
        dqs = lax.fori_loop(0, i % 2, lambda _, cr: step(i - 1, cr, False), dqs)
        dqs = step(i, dqs, True)
        cos = cs_ref[:, 0:HP]
        sin = cs_ref[:, HP:2 * HP]
        s1 = sin * pv_ref[PV_M1:PV_M1 + 1, 0:HP]
        s2 = sin * pv_ref[PV_M2:PV_M2 + 1, 0:HP]
        for a in range(2):
            dq_o[:, a * HP:(a + 1) * HP] = _rope_bwd(dqs[a] * SCALE, cos, s1, s2).astype(BF16)

        @pl.when(i == nq - 1)
        def _():
            dk_o[...] = (dk_acc[...] * (SCALE / SCALE_LOG2E)).astype(BF16)
            dv_o[...] = dv_acc[...].astype(BF16)

    return pl.pallas_call(
        body, name="attn_bwd", grid=(PAIRS, nq),
        in_specs=[pl.BlockSpec((b, 2 * HP), lambda p, i: (i, p)),
                  pl.BlockSpec((seq, 2 * HP), lambda p, i: (0, p)),
                  pl.BlockSpec((seq, HP), lambda p, i: (0, p)),
                  pl.BlockSpec((b, HP), lambda p, i: (i, p)),
                  pl.BlockSpec((b, HP), lambda p, i: (i, p)),
                  pl.BlockSpec((None, None, 2, b), lambda p, i: (p, i, 0, 0)),
                  pl.BlockSpec((b, 2 * HP), lambda p, i: (i, 0)),
                  pl.BlockSpec(pvec.shape, lambda p, i: (0, 0))],
        out_specs=[pl.BlockSpec((b, 2 * HP), lambda p, i: (i, p)),
                   pl.BlockSpec((seq, 2 * HP), lambda p, i: (0, p)),
                   pl.BlockSpec((seq, HP), lambda p, i: (0, p))],
        out_shape=[jax.ShapeDtypeStruct((seq, HEADS * HP), BF16),
                   jax.ShapeDtypeStruct((seq, HEADS * HP), BF16),
                   jax.ShapeDtypeStruct((seq, MLA_W), BF16)],
        scratch_shapes=[pltpu.VMEM((seq, 2 * HP), F32), pltpu.VMEM((seq, HP), F32)],
        compiler_params=pltpu.CompilerParams(dimension_semantics=("arbitrary", "arbitrary"),
                                             vmem_limit_bytes=VMEM_LIMIT),
    )(q, k, v, do, o, lse, cs, pvec)


def _bwd_pre(x, dh2, cq, ckv, cs, dq, dk, dv, dgate, win, wuq, wkv, pvec, gvec):
    seq = x.shape[0]
    t = BWD_TILE

    def body(x_ref, dh2_ref, cq_ref, ckv_ref, cs_ref, dq_ref, dk_ref, dv_ref, dgate_ref,
             win_ref, wuq_ref, wkv_ref, pv_ref, gv_ref, gx_o, gwin_o, gwuq_o, gwkv_o, vec_o):
        i = pl.program_id(0)

        @pl.when(i == 0)
        def _():
            gwin_o[...] = jnp.zeros_like(gwin_o)
            gwuq_o[...] = jnp.zeros_like(gwuq_o)
            gwkv_o[...] = jnp.zeros_like(gwkv_o)
            vec_o[...] = gv_ref[...]

        xb = x_ref[...].astype(BF16)
        dgate = dgate_ref[...]
        gwin_o[:, C_GATE:D_INR] += _dot_tn(xb, dgate)
        gx_gate = _dot_nt(dgate, win_ref[:, C_GATE:D_INR])

        qg = pv_ref[PV_QG:PV_QG + 1, 0:Q_LORA]
        kvg = pv_ref[PV_KVG:PV_KVG + 1, 0:KV_LORA]
        dq = dq_ref[...]
        cqh, rq = _rms_stats(cq_ref[...])
        d_cqn = _dot_nt(dq, wuq_ref[...])
        gwuq_o[...] += _dot_tn((cqh * qg).astype(BF16), dq)
        vec_o[GV_QG:GV_QG + 1, 0:Q_LORA] += jnp.sum(d_cqn * cqh, axis=0, keepdims=True)
        d_cq = _rms_bwd(d_cqn, qg, cqh, rq)

        dk = dk_ref[...]
        dkv = jnp.concatenate([dk, dv_ref[...]], axis=1)
        ckvh, rkv = _rms_stats(ckv_ref[...])
        d_ckvn = _dot_nt(dkv, wkv_ref[...])
        gwkv_o[...] += _dot_tn((ckvh * kvg).astype(BF16), dkv)
        vec_o[GV_KVG:GV_KVG + 1, 0:KV_LORA] += jnp.sum(d_ckvn * ckvh, axis=0, keepdims=True)
        d_ckv = _rms_bwd(d_ckvn, kvg, ckvh, rkv)

        dks = dk[:, 0:HP].astype(F32)
        for h in range(1, HEADS):
            dks = dks + dk[:, h * HP:(h + 1) * HP].astype(F32)
        cos = cs_ref[:, 0:HP]
        sin = cs_ref[:, HP:2 * HP]
        d_kr = _rope_bwd(dks, cos, sin * pv_ref[PV_M1:PV_M1 + 1, 0:HP], sin * pv_ref[PV_M2:PV_M2 + 1, 0:HP])

        d_lat = jnp.concatenate([d_cq.astype(BF16), d_ckv.astype(BF16), d_kr.astype(BF16)], axis=1)
        gwin_o[:, 0:C_GATE] += _dot_tn(xb, d_lat)
        gx_o[...] = DN_ALPHA * dh2_ref[...] + gx_gate + _dot_nt(d_lat, win_ref[:, 0:C_GATE])

    tile = lambda w: pl.BlockSpec((t, w), lambda i: (i, 0))
    full = lambda a: pl.BlockSpec(a.shape, lambda i: (0,) * a.ndim)
    const = lambda s: pl.BlockSpec(s, lambda i: (0,) * len(s))
    return pl.pallas_call(
        body, name="bwd_pre", grid=(seq // t,),
        in_specs=[tile(D_MODEL), tile(D_MODEL), tile(Q_LORA), tile(KV_LORA), tile(2 * HP), tile(HEADS * HP),
                  tile(HEADS * HP), tile(MLA_W), tile(2048), full(win), full(wuq), full(wkv), full(pvec), full(gvec)],
        out_specs=[tile(D_MODEL), const((D_MODEL, D_INR)), const((Q_LORA, HEADS * HP)),
                   const((KV_LORA, HEADS * HP + MLA_W)), const((GV_ROWS, D_MODEL))],
        out_shape=[jax.ShapeDtypeStruct((seq, D_MODEL), F32), jax.ShapeDtypeStruct((D_MODEL, D_INR), F32),
                   jax.ShapeDtypeStruct((Q_LORA, HEADS * HP), F32),
                   jax.ShapeDtypeStruct((KV_LORA, HEADS * HP + MLA_W), F32),
                   jax.ShapeDtypeStruct((GV_ROWS, D_MODEL), F32)],
        compiler_params=pltpu.CompilerParams(dimension_semantics=("arbitrary",), vmem_limit_bytes=VMEM_LIMIT),
    )(x, dh2, cq, ckv, cs, dq, dk, dv, dgate, win, wuq, wkv, pvec, gvec)


def _grad_reduce(gs, gvec):
    n_arr = len(gs)
    n_big = n_arr - 1
    k1 = lambda n, blk: 4 * n + blk
    k2 = lambda n, kk: 4 * n_arr + 3 * n + kk
    k3 = lambda n: 7 * n_arr + n
    k3w = lambda k: 7 * n_arr + n_big + k
    kv = lambda k: 7 * n_arr + n_big + 7 + k
    n_sem = 7 * n_arr + n_big + 14

    def body(*refs):
        g, gv = refs[0:n_arr], refs[n_arr]
        outs, ov = refs[n_arr + 1:2 * n_arr + 1], refs[2 * n_arr + 1]
        r1 = refs[2 * n_arr + 2:3 * n_arr + 2]
        r2 = refs[3 * n_arr + 2:4 * n_arr + 2]
        s2 = refs[4 * n_arr + 2:5 * n_arr + 2]
        vbuf, send_sems, recv_sems = refs[5 * n_arr + 2:]
        x, y, c = lax.axis_index("x"), lax.axis_index("y"), lax.axis_index("c")
        j = 2 * x + y
        me = 2 * j + c
        sib = (x, y, 1 - c)
        chips = [(1 - x, y), (x, 1 - y), (1 - x, 1 - y)]
        others = [sib] + [(px, py, pc) for (px, py) in chips for pc in (c, 1 - c)]

        def copy(k, src, dst, to):
            return pltpu.make_async_remote_copy(
                src_ref=src, dst_ref=dst, send_sem=send_sems.at[k], recv_sem=recv_sems.at[k],
                device_id=to, device_id_type=MESH)

        l1 = [copy(k1(n, blk), g[n].at[blk, 1 - c], r1[n].at[blk], sib) for n in range(n_arr) for blk in range(4)]
        lv = [copy(kv(k), gv, vbuf.at[me], to) for k, to in enumerate(others)]
        for cp in l1 + lv:
            cp.start()
        l2 = []
        for n in range(n_arr):
            for blk in range(4):
                copy(k1(n, blk), g[n].at[blk, c], r1[n].at[blk], sib).wait_recv()
            for blk in range(4):
                r1[n][blk] = g[n][blk, c] + r1[n][blk]
                s2[n][blk] = r1[n][blk].astype(BF16)
            for kk, (px, py) in enumerate(chips):
                l2.append(copy(k2(n, kk), s2[n].at[2 * px + py], r2[n].at[kk], (px, py, c)))
                l2[-1].start()

        l3 = []
        for n in range(n_arr):
            for kk in range(3):
                copy(k2(n, kk), s2[n].at[0], r2[n].at[kk], sib).wait_recv()
            red = ((r1[n][j] + r2[n][0].astype(F32)) + r2[n][1].astype(F32)) + r2[n][2].astype(F32)
            if n < n_big:
                outs[n][c] = red
                back = [copy(k3(n), outs[n].at[c], outs[n].at[c], sib)]
            else:
                outs[n][j, c] = red
                back = [copy(k3w(k), outs[n].at[j, c], outs[n].at[j, c], to) for k, to in enumerate(others)]
            for cp in back:
                cp.start()
            l3 += back
        for n in range(n_big):
            copy(k3(n), outs[n].at[1 - c], outs[n].at[1 - c], sib).wait_recv()
        for k, (px, py, pc) in enumerate(others):
            landed = outs[n_big].at[2 * px + py, pc]
            copy(k3w(k), landed, landed, (px, py, pc)).wait_recv()
            copy(kv(k), gv, vbuf.at[4 * px + 2 * py + pc], (px, py, pc)).wait_recv()
        vbuf[me] = gv[...]
        total = vbuf[0]
        for d in range(1, 8):
            total = total + vbuf[d]
        ov[...] = total
        for cp in l1 + lv + l2 + l3:
            cp.wait_send()

    vmem = pl.BlockSpec(memory_space=pltpu.VMEM)
    half_shapes = [a.shape[2:] for a in gs]
    out_shape = [jax.ShapeDtypeStruct((2,) + s, F32) for s in half_shapes[:n_big]]
    out_shape += [jax.ShapeDtypeStruct((4, 2) + half_shapes[n_big], F32), jax.ShapeDtypeStruct(gvec.shape, F32)]
    scratch = [pltpu.VMEM((4,) + s, F32) for s in half_shapes] + [pltpu.VMEM((3,) + s, BF16) for s in half_shapes]
    scratch += [pltpu.VMEM((4,) + s, BF16) for s in half_shapes]
    scratch += [pltpu.VMEM((8,) + gvec.shape, F32), pltpu.SemaphoreType.DMA((n_sem,)), pltpu.SemaphoreType.DMA((n_sem,))]
    return pl.pallas_call(
        body, name="grad_reduce", out_shape=out_shape,
        in_specs=[vmem] * (n_arr + 1), out_specs=[vmem] * (n_arr + 1), scratch_shapes=scratch,
        compiler_params=pltpu.CompilerParams(vmem_limit_bytes=VMEM_LIMIT),
    )(*gs, gvec)


SMALL_ROWS = ((GV_QG, 1, Q_LORA), (GV_KVG, 1, KV_LORA), (GV_SG, 1, GW), (GV_SB, 1, GW),
              (GV_LNG, 1, D_MODEL), (GV_LNB, 1, D_MODEL), (GV_BSP, HEADS, CHUNK))


def _adam_update(g, w, m, v):
    m_new = ADAM_B1 * m + (1.0 - ADAM_B1) * g
    v_new = ADAM_B2 * v + (1.0 - ADAM_B2) * (g * g)
    m_hat = m_new / (1.0 - ADAM_B1 ** ADAM_STEP)
    v_hat = v_new / (1.0 - ADAM_B2 ** ADAM_STEP)
    return -ADAM_LR * (m_hat / (jnp.sqrt(v_hat) + ADAM_EPS) + ADAM_WD * w), m_new, v_new


def _adamw(g_big, w_big, m_big, v_big, gvec, w_small, m_small, v_small):
    nb, ns = len(g_big), len(w_small)

    def body(*refs):
        it = iter(refs)
        take = lambda n: [next(it) for _ in range(n)]
        g_b, w_b, m_b, v_b = take(nb), take(nb), take(nb), take(nb)
        gv = next(it)
        w_s, m_s, v_s = take(ns), take(ns), take(ns)
        g_bo, d_bo, m_bo, v_bo = take(nb), take(nb), take(nb), take(nb)
        g_so, d_so, m_so, v_so = take(ns), take(ns), take(ns), take(ns)
        for n in range(nb):
            gb = g_b[n][...]
            g_bo[n][...] = gb
            d_bo[n][...], m_bo[n][...], v_bo[n][...] = _adam_update(gb, w_b[n][...], m_b[n][...], v_b[n][...])
        for n, (row, nrow, width) in enumerate(SMALL_ROWS):
            gs = gv[row:row + nrow, 0:width]
            g_so[n][...] = gs
            d_so[n][...], m_so[n][...], v_so[n][...] = _adam_update(gs, w_s[n][...], m_s[n][...], v_s[n][...])

    def rows(a):
        nd = a.ndim
        return pl.BlockSpec((a.shape[0] // ADAM_STEPS,) + a.shape[1:], lambda i: (i,) + (0,) * (nd - 1))

    def whole(a):
        nd = a.ndim
        return pl.BlockSpec(a.shape, lambda i: (0,) * nd)

    big = [jax.ShapeDtypeStruct(a.shape, F32) for a in w_big]
    small = [jax.ShapeDtypeStruct(a.shape, F32) for a in w_small]
    return pl.pallas_call(
        body, name="adamw", grid=(ADAM_STEPS,), out_shape=big * 4 + small * 4,
        in_specs=[rows(a) for a in g_big + w_big + m_big + v_big] + [whole(gvec)]
        + [whole(a) for a in w_small + m_small + v_small],
        out_specs=[rows(a) for a in w_big] * 4 + [whole(a) for a in w_small] * 4,
        compiler_params=pltpu.CompilerParams(dimension_semantics=("arbitrary",), vmem_limit_bytes=VMEM_LIMIT),
    )(*g_big, *w_big, *m_big, *v_big, gvec, *w_small, *m_small, *v_small)


def kernel(x, positions, w_in, q_norm_g, w_uq, kv_norm_g, w_ukv, sgu_norm_g, sgu_norm_b, w_spatial, b_spatial, w_out, ln_g, ln_b, loss_target, m_w_in, m_q_norm_g, m_w_uq, m_kv_norm_g, m_w_ukv, m_sgu_norm_g, m_sgu_norm_b, m_w_spatial, m_b_spatial, m_w_out, m_ln_g, m_ln_b, v_w_in, v_q_norm_g, v_w_uq, v_kv_norm_g, v_w_ukv, v_sgu_norm_g, v_sgu_norm_b, v_w_spatial, v_b_spatial, v_w_out, v_ln_g, v_ln_b):
    seq = x.shape[1]
    x2 = x.reshape(seq, D_MODEL)
    tgt = loss_target.reshape(seq, D_MODEL)
    pos = positions.reshape(seq, 1)

    a_in, a_uq, a_ukv, a_out = _weight_gather([w_in, w_uq, w_ukv, w_out])
    w_uq_f = jnp.swapaxes(a_uq, 0, 1).reshape(Q_LORA, HEADS * (NOPE + ROPE))
    w_ukv_f = jnp.swapaxes(a_ukv, 0, 1).reshape(KV_LORA, HEADS * (NOPE + VDIM))
    wout = a_out.reshape(D_MODEL, D_MODEL)
    zc = lambda n: jnp.zeros((D_MODEL, n), BF16)
    win = jnp.concatenate([a_in[0][:, 0:C_KR], zc(NOPE), a_in[0][:, C_KR:C_KR + ROPE], zc(HP - NOPE - ROPE),
                           a_in[0][:, C_KR + ROPE:],
                           a_in[1], a_in[2], a_in[3]], axis=1)
    wuq = jnp.pad(w_uq_f.reshape(Q_LORA, HEADS, NOPE + ROPE), ((0, 0), (0, 0), (0, HP - NOPE - ROPE)))
    wuq = wuq.reshape(Q_LORA, HEADS * HP)
    ukv = w_ukv_f.reshape(KV_LORA, HEADS, NOPE + VDIM)
    wk = jnp.pad(ukv[:, :, 0:NOPE], ((0, 0), (0, 0), (0, HP - NOPE))).reshape(KV_LORA, HEADS * HP)
    wkv = jnp.concatenate([wk, ukv[:, :, NOPE:].reshape(KV_LORA, MLA_W)], axis=1)

    lane = np.arange(HP)
    half = ROPE // 2
    inv_freq = (1.0 / (ROPE_THETA ** (np.arange(half, dtype=np.float32) / half))).astype(np.float32)
    in_rope = (lane >= NOPE) & (lane < NOPE + ROPE)
    invf = jnp.asarray(np.where(in_rope, inv_freq[(lane - NOPE) % half], 0.0).astype(np.float32))
    m1 = jnp.asarray(np.where((lane >= NOPE) & (lane < NOPE + half), -1.0, 0.0).astype(np.float32))
    m2 = jnp.asarray(np.where((lane >= NOPE + half) & (lane < NOPE + ROPE), 1.0, 0.0).astype(np.float32))
    row = lambda a: jnp.pad(a.astype(F32), (0, D_MODEL - a.shape[0]))
    pvec = jnp.stack([row(q_norm_g), row(kv_norm_g), row(sgu_norm_g), row(sgu_norm_b), row(invf), row(m1),
                      row(m2), row(ln_g), row(ln_b)] + [jnp.zeros((D_MODEL,), F32)] * (PV_ROWS - 9))
    tri = jnp.tril(jnp.ones((CHUNK, CHUNK), dtype=bool))
    wt = jnp.where(tri[None], w_spatial, 0.0).astype(BF16)
    wtt = jnp.swapaxes(wt, 1, 2)
    bsp = jnp.repeat(b_spatial.T, VDIM, axis=1)

    cq, ckv, gate, q, k, v, vt, cs = _fwd_pre(x2, pos, win, wuq, wkv, pvec)
    o, lse = _attn_fwd(q, k, vt)
    dh2, do, dgate, g_wout, g_wsp, gvec = _post(x2, tgt, o, gate, wout, pvec, wt, wtt, bsp)
    dq, dk, dv = _attn_bwd(q, k, v, do, o, lse, cs, pvec)
    gx, g_win, g_wuq, g_wkv, gvec = _bwd_pre(x2, dh2, cq, ckv, cs, dq, dk, dv, dgate, win, wuq, wkv, pvec, gvec)

    cw = w_in.shape[1]
    first = D_INR - 3 * cw
    g_win_0 = jnp.concatenate([g_win[:, 0:C_KR], g_win[:, C_KR + NOPE:C_KR + NOPE + ROPE], g_win[:, C_GATE:first]],
                              axis=1)
    g_win_b = jnp.stack([g_win_0] + [g_win[:, first + cw * jb:first + cw * (jb + 1)] for jb in range(3)])
    g_wuq_f = g_wuq.reshape(Q_LORA, HEADS, HP)[:, :, 0:NOPE + ROPE].reshape(Q_LORA, HEADS * (NOPE + ROPE))
    g_k = g_wkv[:, 0:HEADS * HP].reshape(KV_LORA, HEADS, HP)[:, :, 0:NOPE]
    g_v = g_wkv[:, HEADS * HP:].reshape(KV_LORA, HEADS, VDIM)
    g_wukv_f = jnp.concatenate([g_k, g_v], axis=2).reshape(KV_LORA, HEADS * (NOPE + VDIM))

    def by_chip(a):
        rows, cols = a.shape[0], a.shape[1] // 4
        return jnp.swapaxes(a.reshape(rows, 4, cols), 0, 1).reshape(4, 2, rows // 2, cols)

    gs = [g_win_b.reshape(4, 2, D_MODEL // 2, cw), by_chip(g_wuq_f), by_chip(g_wukv_f), g_wout.reshape(4, 2, 128, D_MODEL),
          g_wsp.reshape(4, 2, CHUNK, CHUNK)]
    r_in, r_uq, r_ukv, r_out, r_wsp, r_vec = _grad_reduce(gs, gvec)

    g_big = [r_in.reshape(w_in.shape), r_uq.reshape(w_uq.shape), r_ukv.reshape(w_ukv.shape),
             r_out.reshape(w_out.shape), r_wsp.reshape(w_spatial.shape)]
    small = lambda qg, kvg, sg, sb, lng, lnb, bs: [qg.reshape(1, -1), kvg.reshape(1, -1), sg.reshape(1, -1),
                                                   sb.reshape(1, -1), lng.reshape(1, -1), lnb.reshape(1, -1), bs]
    res = _adamw(g_big, [w_in, w_uq, w_ukv, w_out, w_spatial], [m_w_in, m_w_uq, m_w_ukv, m_w_out, m_w_spatial],
                 [v_w_in, v_w_uq, v_w_ukv, v_w_out, v_w_spatial], r_vec,
                 small(q_norm_g, kv_norm_g, sgu_norm_g, sgu_norm_b, ln_g, ln_b, b_spatial),
                 small(m_q_norm_g, m_kv_norm_g, m_sgu_norm_g, m_sgu_norm_b, m_ln_g, m_ln_b, m_b_spatial),
                 small(v_q_norm_g, v_kv_norm_g, v_sgu_norm_g, v_sgu_norm_b, v_ln_g, v_ln_b, v_b_spatial))

    def ordered(big, sm):
        vec = lambda n: sm[n].reshape(-1)
        return [big[0], vec(0), big[1], vec(1), big[2], vec(2), vec(3), big[4], sm[6], big[3], vec(4), vec(5)]

    loss = r_vec[GV_LOSS, 0]
    return (loss, gx.reshape(1, seq, D_MODEL), *ordered(res[0:5], res[20:27]), *ordered(res[5:10], res[27:34]),
            *ordered(res[10:15], res[34:41]), *ordered(res[15:20], res[41:48]))
```

```python
import math

import jax
import jax.numpy as jnp
import numpy as np
from jax import lax
from jax.experimental import pallas as pl
from jax.experimental.pallas import tpu as pltpu

F32 = jnp.float32
BF16 = jnp.bfloat16

D_MODEL = 1024
Q_LORA = 256
KV_LORA = 128
HEADS = 8
NOPE = 64
ROPE = 32
VDIM = 64
MLA_W = HEADS * VDIM
GW = 512
CHUNK = 128
HP = 128
PAIRS = HEADS // 2
D_IN = 2464
D_INR = 2560
C_CKV = Q_LORA
C_KR = Q_LORA + KV_LORA
C_GATE = C_KR + HP
ROPE_THETA = 10000.0
DN_ALPHA = 2.0 ** 0.25
EPS = 1e-5
SCALE = 1.0 / math.sqrt(NOPE + ROPE)
SCALE_LOG2E = SCALE * 1.4426950408889634
INV_SQRT2 = 0.7071067811865476
INV_SQRT_2PI = 0.3989422804014327

ADAM_LR = 0.001
ADAM_B1 = 0.9
ADAM_B2 = 0.999
ADAM_EPS = 1e-08
ADAM_WD = 0.01
ADAM_STEP = 10

PV_QG, PV_KVG, PV_SG, PV_SB, PV_INVF, PV_M1, PV_M2, PV_LNG, PV_LNB = range(9)
PV_ROWS = 16
GV_QG, GV_KVG, GV_SG, GV_SB, GV_LNG, GV_LNB, GV_LOSS = range(7)
GV_BSP = 8
GV_ROWS = 16

MESH = pl.DeviceIdType.MESH

FWD_TILE = 512
POST_TILE = 512
BWD_TILE = 512
ATT_BLK = 512
ADAM_STEPS = 4
VMEM_LIMIT = 56 * 1024 * 1024


def _dot(a, b):
    return jnp.dot(a, b, preferred_element_type=F32)


def _dot_nt(a, b):
    return lax.dot_general(a, b, (((1,), (1,)), ((), ())), preferred_element_type=F32)


def _dot_tn(a, b):
    return lax.dot_general(a, b, (((0,), (0,)), ((), ())), preferred_element_type=F32)


def _sigmoid(z):
    return pl.reciprocal(1.0 + jnp.exp(-z), approx=True)


def _gelu_and_grad(x):
    cdf = 0.5 * (1.0 + lax.erf(x * INV_SQRT2))
    return x * cdf, cdf + x * (INV_SQRT_2PI * jnp.exp(-0.5 * x * x))


def _rms_stats(x):
    r = lax.rsqrt(jnp.mean(x * x, axis=-1, keepdims=True) + EPS)
    return x * r, r


def _rms_bwd(dy, g, xh, r):
    dyg = dy * g
    return r * (dyg - xh * jnp.mean(dyg * xh, axis=-1, keepdims=True))


def _ln_stats(x):
    mu = jnp.mean(x, axis=-1, keepdims=True)
    xc = x - mu
    r = lax.rsqrt(jnp.mean(xc * xc, axis=-1, keepdims=True) + EPS)
    return xc * r, r


def _ln_bwd(dy, g, xh, r):
    dxh = dy * g
    return r * (dxh - jnp.mean(dxh, axis=-1, keepdims=True) - xh * jnp.mean(dxh * xh, axis=-1, keepdims=True))


def _rope_fwd(t, c, s1, s2):
    return t * c + pltpu.roll(t, HP - 16, 1) * s1 + pltpu.roll(t, 16, 1) * s2


def _rope_bwd(d, c, s1, s2):
    return d * c + pltpu.roll(d * s1, 16, 1) + pltpu.roll(d * s2, HP - 16, 1)


def _lane_lt64(shape):
    return lax.broadcasted_iota(jnp.int32, shape, len(shape) - 1) < 64


def _spatial_mix(w_ref, src, dst_ref, rows):
    for c in range(rows // CHUNK):
        for p in range(PAIRS):
            blk = src[c * CHUNK:(c + 1) * CHUNK, p * HP:(p + 1) * HP]
            a = _dot(w_ref[2 * p], blk)
            b = _dot(w_ref[2 * p + 1], blk)
            dst_ref[c * CHUNK:(c + 1) * CHUNK, p * HP:(p + 1) * HP] = jnp.where(_lane_lt64(a.shape), a, b)


def _gmlp_fwd(u_pre, v_pre, zb, sg, sb, wt_ref, bsp_ref, sv_ref, rows):
    u, du = _gelu_and_grad(u_pre)
    gv, dgv = _gelu_and_grad(v_pre)
    xh, r = _ln_stats(gv)
    vln = (xh * sg + sb).astype(BF16)
    _spatial_mix(wt_ref, vln, sv_ref, rows)
    bias = bsp_ref[...]
    svb = sv_ref[...] + jnp.concatenate([bias] * (rows // CHUNK), axis=0)
    sig = _sigmoid(zb)
    return u, du, dgv, xh, r, vln, svb, sig


def _weight_gather(shards):
    n_arr = len(shards)

    def body(*refs):
        ins, outs = refs[0:n_arr], refs[n_arr:2 * n_arr]
        send_sems, recv_sems = refs[2 * n_arr:]
        x, y, c = lax.axis_index("x"), lax.axis_index("y"), lax.axis_index("c")
        j = 2 * x + y
        sib = (x, y, 1 - c)
        chips = [(1 - x, y), (x, 1 - y), (1 - x, 1 - y)]
        for n in range(n_arr):
            outs[n][j] = ins[n][...].astype(BF16)

        def half(n, blk, core):
            r = shards[n].shape[0] // 2
            return outs[n].at[blk, pl.ds(pl.multiple_of(core * r, 16), r), :]

        def copy(k, ref, to):
            return pltpu.make_async_remote_copy(
                src_ref=ref, dst_ref=ref, send_sem=send_sems.at[k], recv_sem=recv_sems.at[k],
                device_id=to, device_id_type=MESH)

        first = [copy(6 * n + kk, half(n, j, c), (px, py, c))
                 for n in range(n_arr) for kk, (px, py) in enumerate(chips)]
        for cp in first:
            cp.start()
        passed = []
        for n in range(n_arr):
            for kk, (px, py) in enumerate(chips):
                landed = half(n, 2 * px + py, c)
                copy(6 * n + kk, landed, (px, py, c)).wait_recv()
                passed.append(copy(6 * n + 3 + kk, landed, sib))
                passed[-1].start()
        for n in range(n_arr):
            for kk, (px, py) in enumerate(chips):
                copy(6 * n + 3 + kk, half(n, 2 * px + py, 1 - c), sib).wait_recv()
        for cp in first + passed:
            cp.wait_send()

    vmem = pl.BlockSpec(memory_space=pltpu.VMEM)
    return pl.pallas_call(
        body, name="weight_gather",
        out_shape=[jax.ShapeDtypeStruct((4,) + a.shape, BF16) for a in shards],
        in_specs=[vmem] * n_arr, out_specs=[vmem] * n_arr,
        scratch_shapes=[pltpu.SemaphoreType.DMA((6 * n_arr,)), pltpu.SemaphoreType.DMA((6 * n_arr,))],
        compiler_params=pltpu.CompilerParams(vmem_limit_bytes=VMEM_LIMIT),
    )(*shards)


def _fwd_pre(x, pos, win, wuq, wkv, pvec):
    seq = x.shape[0]
    t = FWD_TILE

    def body(x_ref, pos_ref, win_ref, wuq_ref, wkv_ref, pv_ref,
             cq_o, ckv_o, gate_o, q_o, k_o, v_o, vt_o, cs_o):
        xb = x_ref[...].astype(BF16)
        proj = _dot(xb, win_ref[:, 0:C_GATE])
        cq = proj[:, 0:C_CKV]
        ckv = proj[:, C_CKV:C_KR]
        kr = proj[:, C_KR:C_GATE]
        cq_o[...] = cq
        ckv_o[...] = ckv

        ang = pos_ref[...].astype(F32) * pv_ref[PV_INVF:PV_INVF + 1, 0:HP]
        cos = jnp.cos(ang)
        sin = jnp.sin(ang)
        cs_o[:, 0:HP] = cos
        cs_o[:, HP:2 * HP] = sin
        s1 = sin * pv_ref[PV_M1:PV_M1 + 1, 0:HP]
        s2 = sin * pv_ref[PV_M2:PV_M2 + 1, 0:HP]

        cqh, _ = _rms_stats(cq)
        q_all = _dot((cqh * pv_ref[PV_QG:PV_QG + 1, 0:Q_LORA]).astype(BF16), wuq_ref[...])
        ckvh, _ = _rms_stats(ckv)
        kv_all = _dot((ckvh * pv_ref[PV_KVG:PV_KVG + 1, 0:KV_LORA]).astype(BF16), wkv_ref[...])
        krr = _rope_fwd(kr, cos, s1, s2)
        for h in range(HEADS):
            sl = slice(h * HP, (h + 1) * HP)
            q_o[:, sl] = (_rope_fwd(q_all[:, sl], cos, s1, s2) * SCALE_LOG2E).astype(BF16)
            k_o[:, sl] = (kv_all[:, sl] + krr).astype(BF16)
        val = kv_all[:, HEADS * HP:].astype(BF16)
        v_o[...] = val
        vt_o[...] = val.T
        gate_o[...] = _dot(xb, win_ref[:, C_GATE:D_INR]).astype(BF16)

    tile = lambda w: pl.BlockSpec((t, w), lambda i: (i, 0))
    full = lambda a: pl.BlockSpec(a.shape, lambda i: (0,) * a.ndim)
    outs = [(Q_LORA, F32), (KV_LORA, F32), (2048, BF16), (HEADS * HP, BF16), (HEADS * HP, BF16), (MLA_W, BF16)]
    per_blk = ATT_BLK // t
    out_specs = [tile(w) for w, _ in outs]
    out_specs += [pl.BlockSpec((None, MLA_W, t), lambda i: (i // per_blk, 0, i % per_blk)), tile(2 * HP)]
    out_shape = [jax.ShapeDtypeStruct((seq, w), d) for w, d in outs]
    out_shape += [jax.ShapeDtypeStruct((seq // ATT_BLK, MLA_W, ATT_BLK), BF16), jax.ShapeDtypeStruct((seq, 2 * HP), F32)]
    return pl.pallas_call(
        body, name="fwd_pre", grid=(seq // t,),
        in_specs=[tile(D_MODEL), tile(1), full(win), full(wuq), full(wkv), full(pvec)],
        out_specs=out_specs, out_shape=out_shape,
        compiler_params=pltpu.CompilerParams(dimension_semantics=("arbitrary",), vmem_limit_bytes=VMEM_LIMIT),
    )(x, pos, win, wuq, wkv, pvec)


def _attn_fwd(q, k, vt):
    seq = q.shape[0]
    b = ATT_BLK
    nq = seq // b
    assert nq % 2 == 0
    n_off = nq * (nq - 1) // 2

    def body(q_ref, k_ref, vt_ref, o_o, lse_o, m_ref, l_ref, acc_ref, s_even, s_odd):
        m_ref[...] = jnp.full(m_ref.shape, -jnp.inf, F32)
        l_ref[...] = jnp.zeros(l_ref.shape, F32)
        acc_ref[...] = jnp.zeros(acc_ref.shape, F32)

        def scores(i, j, s_ref):
            qrows = pl.ds(pl.multiple_of(i * b, b), b)
            krows = pl.ds(pl.multiple_of(j * b, b), b)
            for a in range(2):
                s_ref[a] = _dot_nt(k_ref[krows, a * HP:(a + 1) * HP], q_ref[qrows, a * HP:(a + 1) * HP])

        def consume(i, j, s_ref, masked):
            vt_blk = vt_ref[j]
            for a in range(2):
                st = s_ref[a]
                if masked:
                    ki = lax.broadcasted_iota(jnp.int32, st.shape, 0)
                    qi = lax.broadcasted_iota(jnp.int32, st.shape, 1)
                    st = jnp.where(ki <= qi, st, -jnp.inf)
                m_prev = m_ref[i, a:a + 1, :]
                m_new = jnp.maximum(m_prev, jnp.max(st, axis=0, keepdims=True))
                alpha = jnp.exp2(m_prev - m_new)
                pt = jnp.exp2(st - m_new)
                l_ref[i, a:a + 1, :] = alpha * l_ref[i, a:a + 1, :] + jnp.sum(pt, axis=0, keepdims=True)
                acc_ref[i, a] = alpha * acc_ref[i, a] + _dot(vt_blk, pt.astype(BF16))
                m_ref[i, a:a + 1, :] = m_new

        def after(i, j):
            wrap = j + 1 >= i
            return jnp.minimum(jnp.where(wrap, i + 1, i), nq - 1), jnp.where(wrap, 0, j + 1)

        if n_off > 0:
            scores(1, 0, s_even)

            def below(u, ij):
                i1, j1 = after(*ij)
                scores(i1, j1, s_odd)
                consume(ij[0], ij[1], s_even, False)
                i2, j2 = after(i1, j1)
                scores(i2, j2, s_even)
                consume(i1, j1, s_odd, False)
                return i2, j2

            last = lax.fori_loop(0, n_off // 2, below, (jnp.int32(1), jnp.int32(0)))
            if n_off % 2:
                consume(last[0], last[1], s_even, False)

        scores(0, 0, s_even)

        def diagonal(u, carry):
            i0 = 2 * u
            scores(i0 + 1, i0 + 1, s_odd)
            consume(i0, i0, s_even, True)
            i2 = jnp.minimum(i0 + 2, nq - 1)
            scores(i2, i2, s_even)
            consume(i0 + 1, i0 + 1, s_odd, True)
            return carry

        lax.fori_loop(0, nq // 2, diagonal, 0)
        top = lax.broadcasted_iota(jnp.int32, (HP, b), 0) < 64

        def finish(i, carry):
            rows = pl.ds(pl.multiple_of(i * b, b), b)
            o_o[rows, :] = jnp.where(top, acc_ref[i, 0] / l_ref[i, 0:1, :], acc_ref[i, 1] / l_ref[i, 1:2, :]).T
            lse_o[i] = m_ref[i, 0:2, :] + jnp.log2(l_ref[i, 0:2, :])
            return carry

        lax.fori_loop(0, nq, finish, 0)

    return pl.pallas_call(
        body, name="attn_fwd", grid=(PAIRS,),
        in_specs=[pl.BlockSpec((seq, 2 * HP), lambda p: (0, p)),
                  pl.BlockSpec((seq, 2 * HP), lambda p: (0, p)),
                  pl.BlockSpec((nq, HP, b), lambda p: (0, p, 0))],
        out_specs=[pl.BlockSpec((seq, HP), lambda p: (0, p)),
                   pl.BlockSpec((None, nq, 2, b), lambda p: (p, 0, 0, 0))],
        out_shape=[jax.ShapeDtypeStruct((seq, MLA_W), F32),
                   jax.ShapeDtypeStruct((PAIRS, nq, 2, b), F32)],
        scratch_shapes=[pltpu.VMEM((nq, 8, b), F32), pltpu.VMEM((nq, 8, b), F32), pltpu.VMEM((nq, 2, HP, b), F32),
                        pltpu.VMEM((2, b, b), F32), pltpu.VMEM((2, b, b), F32)],
        compiler_params=pltpu.CompilerParams(dimension_semantics=("arbitrary",), vmem_limit_bytes=VMEM_LIMIT),
    )(q, k, vt)


def _post(x, tgt, o, gate, wout, pvec, wt, wtt, bsp):
    seq = x.shape[0]
    t = POST_TILE
    nt = seq // t

    def body(x_ref, tgt_ref, o_ref, gate_ref, wout_ref, pv_ref, wt_ref, wtt_ref, bsp_ref,
             dh2_o, do_o, dgate_o, gwout_o, gwsp_o, vec_o, sv_ref, dvln_ref, bacc_ref):
        i = pl.program_id(0)

        @pl.when(i == 0)
        def _():
            gwout_o[...] = jnp.zeros_like(gwout_o)
            gwsp_o[...] = jnp.zeros_like(gwsp_o)
            vec_o[...] = jnp.zeros_like(vec_o)
            bacc_ref[...] = jnp.zeros_like(bacc_ref)

        za = gate_ref[:, 0:512].astype(F32)
        u_pre = gate_ref[:, 512:1024].astype(F32)
        v_pre = gate_ref[:, 1024:1536].astype(F32)
        zb = gate_ref[:, 1536:2048].astype(F32)
        sg = pv_ref[PV_SG:PV_SG + 1, 0:GW]
        sb = pv_ref[PV_SB:PV_SB + 1, 0:GW]
        lng = pv_ref[PV_LNG:PV_LNG + 1, :]
        lnb = pv_ref[PV_LNB:PV_LNB + 1, :]
        o = o_ref[...]

        sig_a = _sigmoid(za)
        silu_a = za * sig_a
        u, du, dgv, xh, r, vln, svb, sig_b = _gmlp_fwd(u_pre, v_pre, zb, sg, sb, wt_ref, bsp_ref, sv_ref, t)
        silu_b = zb * sig_b
        sgu = u * svb
        merged = jnp.concatenate([o * silu_a, sgu * silu_b], axis=1).astype(BF16)
        h2 = DN_ALPHA * x_ref[...] + _dot(merged, wout_ref[...])
        xh2, r2 = _ln_stats(h2)
        err = xh2 * lng + lnb - tgt_ref[...]
        d_out = err * (1.0 / D_MODEL)
        vec_o[GV_LNG:GV_LNG + 1, :] += jnp.sum(d_out * xh2, axis=0, keepdims=True)
        vec_o[GV_LNB:GV_LNB + 1, :] += jnp.sum(d_out, axis=0, keepdims=True)
        vec_o[GV_LOSS:GV_LOSS + 1, :] += jnp.sum(err * err, axis=0, keepdims=True) * (0.5 / D_MODEL)

        d_h2 = _ln_bwd(d_out, lng, xh2, r2)
        dh2_o[...] = d_h2
        dh2b = d_h2.astype(BF16)
        gwout_o[...] += _dot_tn(merged, dh2b)
        d_m = _dot_nt(dh2b, wout_ref[...])
        d_oa = d_m[:, 0:512]
        d_ob = d_m[:, 512:1024]
        do_o[...] = (d_oa * silu_a).astype(BF16)
        dgate_o[:, 0:512] = (d_oa * o * (sig_a * (1.0 + za * (1.0 - sig_a)))).astype(BF16)
        dgate_o[:, 1536:2048] = (d_ob * sgu * (sig_b * (1.0 + zb * (1.0 - sig_b)))).astype(BF16)
        d_sgu = d_ob * silu_b
        dgate_o[:, 512:1024] = (d_sgu * svb * du).astype(BF16)
        d_sv = d_sgu * u
        acc = bacc_ref[...]
        for c in range(t // CHUNK):
            acc = acc + d_sv[c * CHUNK:(c + 1) * CHUNK, :]
        bacc_ref[...] = acc
        d_svb = d_sv.astype(BF16)
        for c in range(t // CHUNK):
            for p in range(PAIRS):
                blk = d_svb[c * CHUNK:(c + 1) * CHUNK, p * HP:(p + 1) * HP]
                vblk = vln[c * CHUNK:(c + 1) * CHUNK, p * HP:(p + 1) * HP]
                first = _lane_lt64(blk.shape)
                gwsp_o[2 * p] += _dot_nt(jnp.where(first, blk, jnp.zeros_like(blk)), vblk)
                gwsp_o[2 * p + 1] += _dot_nt(jnp.where(first, jnp.zeros_like(blk), blk), vblk)
        _spatial_mix(wtt_ref, d_svb, dvln_ref, t)
        d_vln = dvln_ref[...]
        vec_o[GV_SG:GV_SG + 1, 0:GW] += jnp.sum(d_vln * xh, axis=0, keepdims=True)
        vec_o[GV_SB:GV_SB + 1, 0:GW] += jnp.sum(d_vln, axis=0, keepdims=True)
        dgate_o[:, 1024:1536] = (_ln_bwd(d_vln, sg, xh, r) * dgv).astype(BF16)


        @pl.when(i == nt - 1)
        def _():
            tri = (lax.broadcasted_iota(jnp.int32, (CHUNK, CHUNK), 1)
                   <= lax.broadcasted_iota(jnp.int32, (CHUNK, CHUNK), 0))
            for h in range(HEADS):
                gwsp_o[h] = jnp.where(tri, gwsp_o[h], 0.0)
            lane = lax.broadcasted_iota(jnp.int32, (CHUNK, HP), 1)
            res = jnp.zeros((CHUNK, HP), F32)
            for h in range(HEADS):
                p, a = divmod(h, 2)
                blk = bacc_ref[:, p * HP:(p + 1) * HP]
                part = jnp.where(_lane_lt64(blk.shape) == (a == 0), blk, 0.0)
                res = jnp.where(lane == h, jnp.sum(part, axis=-1, keepdims=True), res)
            vec_o[GV_BSP:GV_BSP + HEADS, 0:HP] = res.T[0:HEADS, :]
            lane1 = lax.broadcasted_iota(jnp.int32, (1, D_MODEL), 1)
            total = jnp.sum(vec_o[GV_LOSS:GV_LOSS + 1, :], axis=-1, keepdims=True)
            vec_o[GV_LOSS:GV_LOSS + 1, :] = jnp.where(lane1 == 0, total, 0.0)

    tile = lambda w: pl.BlockSpec((t, w), lambda i: (i, 0))
    full = lambda a: pl.BlockSpec(a.shape, lambda i: (0,) * a.ndim)
    const = lambda s: pl.BlockSpec(s, lambda i: (0,) * len(s))
    return pl.pallas_call(
        body, name="post", grid=(nt,),
        in_specs=[tile(D_MODEL), tile(D_MODEL), tile(MLA_W), tile(2048), full(wout), full(pvec),
                  full(wt), full(wtt), full(bsp)],
        out_specs=[tile(D_MODEL), tile(MLA_W), tile(2048), const((D_MODEL, D_MODEL)),
                   const((HEADS, CHUNK, CHUNK)), const((GV_ROWS, D_MODEL))],
        out_shape=[jax.ShapeDtypeStruct((seq, D_MODEL), F32), jax.ShapeDtypeStruct((seq, MLA_W), BF16),
                   jax.ShapeDtypeStruct((seq, 2048), BF16), jax.ShapeDtypeStruct((D_MODEL, D_MODEL), F32),
                   jax.ShapeDtypeStruct((HEADS, CHUNK, CHUNK), F32), jax.ShapeDtypeStruct((GV_ROWS, D_MODEL), F32)],
        scratch_shapes=[pltpu.VMEM((t, GW), F32), pltpu.VMEM((t, GW), F32), pltpu.VMEM((CHUNK, GW), F32)],
        compiler_params=pltpu.CompilerParams(dimension_semantics=("arbitrary",), vmem_limit_bytes=VMEM_LIMIT),
    )(x, tgt, o, gate, wout, pvec, wt, wtt, bsp)


def _attn_bwd(q, k, v, do, o, lse, cs, pvec):
    seq = q.shape[0]
    b = ATT_BLK
    nq = seq // b

    def body(q_ref, k_ref, v_ref, do_ref, o_ref, lse_ref, cs_ref, pv_ref, dq_o, dk_o, dv_o, dk_acc, dv_acc):
        i = pl.program_id(1)

        @pl.when(i == 0)
        def _():
            dk_acc[...] = jnp.zeros_like(dk_acc)
            dv_acc[...] = jnp.zeros_like(dv_acc)

        first = _lane_lt64((b, HP))
        do = do_ref[...]
        zero = jnp.zeros_like(do)
        dos = [jnp.where(first, do, zero), jnp.where(first, zero, do)]
        prod_t = (do.astype(F32) * o_ref[...]).T
        deltas = [jnp.sum(prod_t[0:64, :], axis=0, keepdims=True),
                  jnp.sum(prod_t[64:128, :], axis=0, keepdims=True)]
        lses = [lse_ref[0:1, :], lse_ref[1:2, :]]
        qs = [q_ref[:, a * HP:(a + 1) * HP] for a in range(2)]

        def step(j, dqs, masked, nk=b):
            rows = pl.ds(pl.multiple_of(j * b, b), nk)
            vb = v_ref[rows, :]
            new_dq = []
            dvs = []
            for a in range(2):
                kb = k_ref[rows, a * HP:(a + 1) * HP]
                pt = jnp.exp2(_dot_nt(kb, qs[a]) - lses[a])
                if masked:
                    ki = lax.broadcasted_iota(jnp.int32, pt.shape, 0)
                    qi = lax.broadcasted_iota(jnp.int32, pt.shape, 1)
                    pt = jnp.where(ki <= qi, pt, 0.0)
                dvs.append(_dot(pt.astype(BF16), do))
                dpt = _dot_nt(vb, dos[a])
                dst = (pt * (dpt - deltas[a])).astype(BF16)
                dk_acc[rows, a * HP:(a + 1) * HP] += _dot(dst, qs[a])
                new_dq.append(dqs[a] + _dot_tn(dst, kb))
            dv_acc[rows, :] += jnp.where(_lane_lt64((nk, HP)), dvs[0], dvs[1])
            return tuple(new_dq)

        init = (jnp.zeros((b, HP), F32), jnp.zeros((b, HP), F32))
        dqs = lax.fori_loop(0, i // 4, lambda jj, cr: step(4 * jj, cr, False, 4 * b), init)
        dqs = lax.fori_loop(0, (i % 4) // 2, lambda _, cr: step(4 * (i // 4), cr, False, 2 * b), dqs)
        dqs = lax.fori_loop(0, i % 2, lambda _, cr: step(i - 1, cr, False), dqs)
        dqs = step(i, dqs, True)
        cos = cs_ref[:, 0:HP]
        sin = cs_ref[:, HP:2 * HP]
        s1 = sin * pv_ref[PV_M1:PV_M1 + 1, 0:HP]
        s2 = sin * pv_ref[PV_M2:PV_M2 + 1, 0:HP]
        for a in range(2):
            dq_o[:, a * HP:(a + 1) * HP] = _rope_bwd(dqs[a] * SCALE, cos, s1, s2).astype(BF16)

        @pl.when(i == nq - 1)
        def _():
            dk_o[...] = (dk_acc[...] * (SCALE / SCALE_LOG2E)).astype(BF16)
            dv_o[...] = dv_acc[...].astype(BF16)

    return pl.pallas_call(
        body, name="attn_bwd", grid=(PAIRS, nq),
        in_specs=[pl.BlockSpec((b, 2 * HP), lambda p, i: (i, p)),
                  pl.BlockSpec((seq, 2 * HP), lambda p, i: (0, p)),
                  pl.BlockSpec((seq, HP), lambda p, i: (0, p)),
                  pl.BlockSpec((b, HP), lambda p, i: (i, p)),
                  pl.BlockSpec((b, HP), lambda p, i: (i, p)),
                  pl.BlockSpec((None, None, 2, b), lambda p, i: (p, i, 0, 0)),
                  pl.BlockSpec((b, 2 * HP), lambda p, i: (i, 0)),
                  pl.BlockSpec(pvec.shape, lambda p, i: (0, 0))],
        out_specs=[pl.BlockSpec((b, 2 * HP), lambda p, i: (i, p)),
                   pl.BlockSpec((seq, 2 * HP), lambda p, i: (0, p)),
                   pl.BlockSpec((seq, HP), lambda p, i: (0, p))],
        out_shape=[jax.ShapeDtypeStruct((seq, HEADS * HP), BF16),
                   jax.ShapeDtypeStruct((seq, HEADS * HP), BF16),
                   jax.ShapeDtypeStruct((seq, MLA_W), BF16)],
        scratch_shapes=[pltpu.VMEM((seq, 2 * HP), F32), pltpu.VMEM((seq, HP), F32)],
        compiler_params=pltpu.CompilerParams(dimension_semantics=("arbitrary", "arbitrary"),
                                             vmem_limit_bytes=VMEM_LIMIT),
    )(q, k, v, do, o, lse, cs, pvec)


def _bwd_pre(x, dh2, cq, ckv, cs, dq, dk, dv, dgate, win, wuq, wkv, pvec, gvec):
    seq = x.shape[0]
    t = BWD_TILE

    def body(x_ref, dh2_ref, cq_ref, ckv_ref, cs_ref, dq_ref, dk_ref, dv_ref, dgate_ref,
             win_ref, wuq_ref, wkv_ref, pv_ref, gv_ref, gx_o, gwin_o, gwuq_o, gwkv_o, vec_o):
        i = pl.program_id(0)

        @pl.when(i == 0)
        def _():
            gwin_o[...] = jnp.zeros_like(gwin_o)
            gwuq_o[...] = jnp.zeros_like(gwuq_o)
            gwkv_o[...] = jnp.zeros_like(gwkv_o)
            vec_o[...] = gv_ref[...]

        xb = x_ref[...].astype(BF16)
        dgate = dgate_ref[...]
        gwin_o[:, C_GATE:D_INR] += _dot_tn(xb, dgate)
        gx_gate = _dot_nt(dgate, win_ref[:, C_GATE:D_INR])

        qg = pv_ref[PV_QG:PV_QG + 1, 0:Q_LORA]
        kvg = pv_ref[PV_KVG:PV_KVG + 1, 0:KV_LORA]
        dq = dq_ref[...]
        cqh, rq = _rms_stats(cq_ref[...])
        d_cqn = _dot_nt(dq, wuq_ref[...])
        gwuq_o[...] += _dot_tn((cqh * qg).astype(BF16), dq)
        vec_o[GV_QG:GV_QG + 1, 0:Q_LORA] += jnp.sum(d_cqn * cqh, axis=0, keepdims=True)
        d_cq = _rms_bwd(d_cqn, qg, cqh, rq)

        dk = dk_ref[...]
        dkv = jnp.concatenate([dk, dv_ref[...]], axis=1)
        ckvh, rkv = _rms_stats(ckv_ref[...])
        d_ckvn = _dot_nt(dkv, wkv_ref[...])
        gwkv_o[...] += _dot_tn((ckvh * kvg).astype(BF16), dkv)
        vec_o[GV_KVG:GV_KVG + 1, 0:KV_LORA] += jnp.sum(d_ckvn * ckvh, axis=0, keepdims=True)
        d_ckv = _rms_bwd(d_ckvn, kvg, ckvh, rkv)

        dks = dk[:, 0:HP].astype(F32)
        for h in range(1, HEADS):
            dks = dks + dk[:, h * HP:(h + 1) * HP].astype(F32)
        cos = cs_ref[:, 0:HP]
        sin = cs_ref[:, HP:2 * HP]
        d_kr = _rope_bwd(dks, cos, sin * pv_ref[PV_M1:PV_M1 + 1, 0:HP], sin * pv_ref[PV_M2:PV_M2 + 1, 0:HP])

        d_lat = jnp.concatenate([d_cq.astype(BF16), d_ckv.astype(BF16), d_kr.astype(BF16)], axis=1)
        gwin_o[:, 0:C_GATE] += _dot_tn(xb, d_lat)
        gx_o[...] = DN_ALPHA * dh2_ref[...] + gx_gate + _dot_nt(d_lat, win_ref[:, 0:C_GATE])

    tile = lambda w: pl.BlockSpec((t, w), lambda i: (i, 0))
    full = lambda a: pl.BlockSpec(a.shape, lambda i: (0,) * a.ndim)
    const = lambda s: pl.BlockSpec(s, lambda i: (0,) * len(s))
    return pl.pallas_call(
        body, name="bwd_pre", grid=(seq // t,),
        in_specs=[tile(D_MODEL), tile(D_MODEL), tile(Q_LORA), tile(KV_LORA), tile(2 * HP), tile(HEADS * HP),
                  tile(HEADS * HP), tile(MLA_W), tile(2048), full(win), full(wuq), full(wkv), full(pvec), full(gvec)],
        out_specs=[tile(D_MODEL), const((D_MODEL, D_INR)), const((Q_LORA, HEADS * HP)),
                   const((KV_LORA, HEADS * HP + MLA_W)), const((GV_ROWS, D_MODEL))],
        out_shape=[jax.ShapeDtypeStruct((seq, D_MODEL), F32), jax.ShapeDtypeStruct((D_MODEL, D_INR), F32),
                   jax.ShapeDtypeStruct((Q_LORA, HEADS * HP), F32),
                   jax.ShapeDtypeStruct((KV_LORA, HEADS * HP + MLA_W), F32),
                   jax.ShapeDtypeStruct((GV_ROWS, D_MODEL), F32)],
        compiler_params=pltpu.CompilerParams(dimension_semantics=("arbitrary",), vmem_limit_bytes=VMEM_LIMIT),
    )(x, dh2, cq, ckv, cs, dq, dk, dv, dgate, win, wuq, wkv, pvec, gvec)


def _grad_reduce(gs, gvec):
    n_arr = len(gs)
    n_big = n_arr - 1
    k1 = lambda n, blk: 4 * n + blk
    k2 = lambda n, kk: 4 * n_arr + 3 * n + kk
    k3 = lambda n: 7 * n_arr + n
    k3w = lambda k: 7 * n_arr + n_big + k
    kv = lambda k: 7 * n_arr + n_big + 7 + k
    n_sem = 7 * n_arr + n_big + 14

    def body(*refs):
        g, gv = refs[0:n_arr], refs[n_arr]
        outs, ov = refs[n_arr + 1:2 * n_arr + 1], refs[2 * n_arr + 1]
        r1 = refs[2 * n_arr + 2:3 * n_arr + 2]
        r2 = refs[3 * n_arr + 2:4 * n_arr + 2]
        s2 = refs[4 * n_arr + 2:5 * n_arr + 2]
        vbuf, send_sems, recv_sems = refs[5 * n_arr + 2:]
        x, y, c = lax.axis_index("x"), lax.axis_index("y"), lax.axis_index("c")
        j = 2 * x + y
        me = 2 * j + c
        sib = (x, y, 1 - c)
        chips = [(1 - x, y), (x, 1 - y), (1 - x, 1 - y)]
        others = [sib] + [(px, py, pc) for (px, py) in chips for pc in (c, 1 - c)]

        def copy(k, src, dst, to):
            return pltpu.make_async_remote_copy(
                src_ref=src, dst_ref=dst, send_sem=send_sems.at[k], recv_sem=recv_sems.at[k],
                device_id=to, device_id_type=MESH)

        l1 = [copy(k1(n, blk), g[n].at[blk, 1 - c], r1[n].at[blk], sib) for n in range(n_arr) for blk in range(4)]
        lv = [copy(kv(k), gv, vbuf.at[me], to) for k, to in enumerate(others)]
        for cp in l1 + lv:
            cp.start()
        l2 = []
        for n in range(n_arr):
            for blk in range(4):
                copy(k1(n, blk), g[n].at[blk, c], r1[n].at[blk], sib).wait_recv()
            for blk in range(4):
                r1[n][blk] = g[n][blk, c] + r1[n][blk]
                s2[n][blk] = r1[n][blk].astype(BF16)
            for kk, (px, py) in enumerate(chips):
                l2.append(copy(k2(n, kk), s2[n].at[2 * px + py], r2[n].at[kk], (px, py, c)))
                l2[-1].start()

        l3 = []
        for n in range(n_arr):
            for kk in range(3):
                copy(k2(n, kk), s2[n].at[0], r2[n].at[kk], sib).wait_recv()
            red = ((r1[n][j] + r2[n][0].astype(F32)) + r2[n][1].astype(F32)) + r2[n][2].astype(F32)
            if n < n_big:
                outs[n][c] = red
                back = [copy(k3(n), outs[n].at[c], outs[n].at[c], sib)]
            else:
                outs[n][j, c] = red
                back = [copy(k3w(k), outs[n].at[j, c], outs[n].at[j, c], to) for k, to in enumerate(others)]
            for cp in back:
                cp.start()
            l3 += back
        for n in range(n_big):
            copy(k3(n), outs[n].at[1 - c], outs[n].at[1 - c], sib).wait_recv()
        for k, (px, py, pc) in enumerate(others):
            landed = outs[n_big].at[2 * px + py, pc]
            copy(k3w(k), landed, landed, (px, py, pc)).wait_recv()
            copy(kv(k), gv, vbuf.at[4 * px + 2 * py + pc], (px, py, pc)).wait_recv()
        vbuf[me] = gv[...]
        total = vbuf[0]
        for d in range(1, 8):
            total = total + vbuf[d]
        ov[...] = total
        for cp in l1 + lv + l2 + l3:
            cp.wait_send()

    vmem = pl.BlockSpec(memory_space=pltpu.VMEM)
    half_shapes = [a.shape[2:] for a in gs]
    out_shape = [jax.ShapeDtypeStruct((2,) + s, F32) for s in half_shapes[:n_big]]
    out_shape += [jax.ShapeDtypeStruct((4, 2) + half_shapes[n_big], F32), jax.ShapeDtypeStruct(gvec.shape, F32)]
    scratch = [pltpu.VMEM((4,) + s, F32) for s in half_shapes] + [pltpu.VMEM((3,) + s, BF16) for s in half_shapes]
    scratch += [pltpu.VMEM((4,) + s, BF16) for s in half_shapes]
    scratch += [pltpu.VMEM((8,) + gvec.shape, F32), pltpu.SemaphoreType.DMA((n_sem,)), pltpu.SemaphoreType.DMA((n_sem,))]
    return pl.pallas_call(
        body, name="grad_reduce", out_shape=out_shape,
        in_specs=[vmem] * (n_arr + 1), out_specs=[vmem] * (n_arr + 1), scratch_shapes=scratch,
        compiler_params=pltpu.CompilerParams(vmem_limit_bytes=VMEM_LIMIT),
    )(*gs, gvec)


SMALL_ROWS = ((GV_QG, 1, Q_LORA), (GV_KVG, 1, KV_LORA), (GV_SG, 1, GW), (GV_SB, 1, GW),
              (GV_LNG, 1, D_MODEL), (GV_LNB, 1, D_MODEL), (GV_BSP, HEADS, CHUNK))


def _adam_update(g, w, m, v):
    m_new = ADAM_B1 * m + (1.0 - ADAM_B1) * g
    v_new = ADAM_B2 * v + (1.0 - ADAM_B2) * (g * g)
    m_hat = m_new / (1.0 - ADAM_B1 ** ADAM_STEP)
    v_hat = v_new / (1.0 - ADAM_B2 ** ADAM_STEP)
    return -ADAM_LR * (m_hat / (jnp.sqrt(v_hat) + ADAM_EPS) + ADAM_WD * w), m_new, v_new


def _adamw(g_big, w_big, m_big, v_big, gvec, w_small, m_small, v_small):
    nb, ns = len(g_big), len(w_small)

    def body(*refs):
        it = iter(refs)
        take = lambda n: [next(it) for _ in range(n)]
        g_b, w_b, m_b, v_b = take(nb), take(nb), take(nb), take(nb)
        gv = next(it)
        w_s, m_s, v_s = take(ns), take(ns), take(ns)
        g_bo, d_bo, m_bo, v_bo = take(nb), take(nb), take(nb), take(nb)
        g_so, d_so, m_so, v_so = take(ns), take(ns), take(ns), take(ns)
        for n in range(nb):
            gb = g_b[n][...]
            g_bo[n][...] = gb
            d_bo[n][...], m_bo[n][...], v_bo[n][...] = _adam_update(gb, w_b[n][...], m_b[n][...], v_b[n][...])
        for n, (row, nrow, width) in enumerate(SMALL_ROWS):
            gs = gv[row:row + nrow, 0:width]
            g_so[n][...] = gs
            d_so[n][...], m_so[n][...], v_so[n][...] = _adam_update(gs, w_s[n][...], m_s[n][...], v_s[n][...])

    def rows(a):
        nd = a.ndim
        return pl.BlockSpec((a.shape[0] // ADAM_STEPS,) + a.shape[1:], lambda i: (i,) + (0,) * (nd - 1))

    def whole(a):
        nd = a.ndim
        return pl.BlockSpec(a.shape, lambda i: (0,) * nd)

    big = [jax.ShapeDtypeStruct(a.shape, F32) for a in w_big]
    small = [jax.ShapeDtypeStruct(a.shape, F32) for a in w_small]
    return pl.pallas_call(
        body, name="adamw", grid=(ADAM_STEPS,), out_shape=big * 4 + small * 4,
        in_specs=[rows(a) for a in g_big + w_big + m_big + v_big] + [whole(gvec)]
        + [whole(a) for a in w_small + m_small + v_small],
        out_specs=[rows(a) for a in w_big] * 4 + [whole(a) for a in w_small] * 4,
        compiler_params=pltpu.CompilerParams(dimension_semantics=("arbitrary",), vmem_limit_bytes=VMEM_LIMIT),
    )(*g_big, *w_big, *m_big, *v_big, gvec, *w_small, *m_small, *v_small)


def kernel(x, positions, w_in, q_norm_g, w_uq, kv_norm_g, w_ukv, sgu_norm_g, sgu_norm_b, w_spatial, b_spatial, w_out, ln_g, ln_b, loss_target, m_w_in, m_q_norm_g, m_w_uq, m_kv_norm_g, m_w_ukv, m_sgu_norm_g, m_sgu_norm_b, m_w_spatial, m_b_spatial, m_w_out, m_ln_g, m_ln_b, v_w_in, v_q_norm_g, v_w_uq, v_kv_norm_g, v_w_ukv, v_sgu_norm_g, v_sgu_norm_b, v_w_spatial, v_b_spatial, v_w_out, v_ln_g, v_ln_b):
    seq = x.shape[1]
    x2 = x.reshape(seq, D_MODEL)
    tgt = loss_target.reshape(seq, D_MODEL)
    pos = positions.reshape(seq, 1)

    a_in, a_uq, a_ukv, a_out = _weight_gather([w_in, w_uq, w_ukv, w_out])
    w_uq_f = jnp.swapaxes(a_uq, 0, 1).reshape(Q_LORA, HEADS * (NOPE + ROPE))
    w_ukv_f = jnp.swapaxes(a_ukv, 0, 1).reshape(KV_LORA, HEADS * (NOPE + VDIM))
    wout = a_out.reshape(D_MODEL, D_MODEL)
    zc = lambda n: jnp.zeros((D_MODEL, n), BF16)
    win = jnp.concatenate([a_in[0][:, 0:C_KR], zc(NOPE), a_in[0][:, C_KR:C_KR + ROPE], zc(HP - NOPE - ROPE),
                           a_in[0][:, C_KR + ROPE:],
                           a_in[1], a_in[2], a_in[3]], axis=1)
    wuq = jnp.pad(w_uq_f.reshape(Q_LORA, HEADS, NOPE + ROPE), ((0, 0), (0, 0), (0, HP - NOPE - ROPE)))
    wuq = wuq.reshape(Q_LORA, HEADS * HP)
    ukv = w_ukv_f.reshape(KV_LORA, HEADS, NOPE + VDIM)
    wk = jnp.pad(ukv[:, :, 0:NOPE], ((0, 0), (0, 0), (0, HP - NOPE))).reshape(KV_LORA, HEADS * HP)
    wkv = jnp.concatenate([wk, ukv[:, :, NOPE:].reshape(KV_LORA, MLA_W)], axis=1)

    lane = np.arange(HP)
    half = ROPE // 2
    inv_freq = (1.0 / (ROPE_THETA ** (np.arange(half, dtype=np.float32) / half))).astype(np.float32)
    in_rope = (lane >= NOPE) & (lane < NOPE + ROPE)
    invf = jnp.asarray(np.where(in_rope, inv_freq[(lane - NOPE) % half], 0.0).astype(np.float32))
    m1 = jnp.asarray(np.where((lane >= NOPE) & (lane < NOPE + half), -1.0, 0.0).astype(np.float32))
    m2 = jnp.asarray(np.where((lane >= NOPE + half) & (lane < NOPE + ROPE), 1.0, 0.0).astype(np.float32))
    row = lambda a: jnp.pad(a.astype(F32), (0, D_MODEL - a.shape[0]))
    pvec = jnp.stack([row(q_norm_g), row(kv_norm_g), row(sgu_norm_g), row(sgu_norm_b), row(invf), row(m1),
                      row(m2), row(ln_g), row(ln_b)] + [jnp.zeros((D_MODEL,), F32)] * (PV_ROWS - 9))
    tri = jnp.tril(jnp.ones((CHUNK, CHUNK), dtype=bool))
    wt = jnp.where(tri[None], w_spatial, 0.0).astype(BF16)
    wtt = jnp.swapaxes(wt, 1, 2)
    bsp = jnp.repeat(b_spatial.T, VDIM, axis=1)

    cq, ckv, gate, q, k, v, vt, cs = _fwd_pre(x2, pos, win, wuq, wkv, pvec)
    o, lse = _attn_fwd(q, k, vt)
    dh2, do, dgate, g_wout, g_wsp, gvec = _post(x2, tgt, o, gate, wout, pvec, wt, wtt, bsp)
    dq, dk, dv = _attn_bwd(q, k, v, do, o, lse, cs, pvec)
    gx, g_win, g_wuq, g_wkv, gvec = _bwd_pre(x2, dh2, cq, ckv, cs, dq, dk, dv, dgate, win, wuq, wkv, pvec, gvec)

    cw = w_in.shape[1]
    first = D_INR - 3 * cw
    g_win_0 = jnp.concatenate([g_win[:, 0:C_KR], g_win[:, C_KR + NOPE:C_KR + NOPE + ROPE], g_win[:, C_GATE:first]],
                              axis=1)
    g_win_b = jnp.stack([g_win_0] + [g_win[:, first + cw * jb:first + cw * (jb + 1)] for jb in range(3)])
    g_wuq_f = g_wuq.reshape(Q_LORA, HEADS, HP)[:, :, 0:NOPE + ROPE].reshape(Q_LORA, HEADS * (NOPE + ROPE))
    g_k = g_wkv[:, 0:HEADS * HP].reshape(KV_LORA, HEADS, HP)[:, :, 0:NOPE]
    g_v = g_wkv[:, HEADS * HP:].reshape(KV_LORA, HEADS, VDIM)
    g_wukv_f = jnp.concatenate([g_k, g_v], axis=2).reshape(KV_LORA, HEADS * (NOPE + VDIM))

    def by_chip(a):
        rows, cols = a.shape[0], a.shape[1] // 4
        return jnp.swapaxes(a.reshape(rows, 4, cols), 0, 1).reshape(4, 2, rows // 2, cols)

    gs = [g_win_b.reshape(4, 2, D_MODEL // 2, cw), by_chip(g_wuq_f), by_chip(g_wukv_f), g_wout.reshape(4, 2, 128, D_MODEL),
          g_wsp.reshape(4, 2, CHUNK, CHUNK)]
    r_in, r_uq, r_ukv, r_out, r_wsp, r_vec = _grad_reduce(gs, gvec)

    g_big = [r_in.reshape(w_in.shape), r_uq.reshape(w_uq.shape), r_ukv.reshape(w_ukv.shape),
             r_out.reshape(w_out.shape), r_wsp.reshape(w_spatial.shape)]
    small = lambda qg, kvg, sg, sb, lng, lnb, bs: [qg.reshape(1, -1), kvg.reshape(1, -1), sg.reshape(1, -1),
                                                   sb.reshape(1, -1), lng.reshape(1, -1), lnb.reshape(1, -1), bs]
    res = _adamw(g_big, [w_in, w_uq, w_ukv, w_out, w_spatial], [m_w_in, m_w_uq, m_w_ukv, m_w_out, m_w_spatial],
                 [v_w_in, v_w_uq, v_w_ukv, v_w_out, v_w_spatial], r_vec,
                 small(q_norm_g, kv_norm_g, sgu_norm_g, sgu_norm_b, ln_g, ln_b, b_spatial),
                 small(m_q_norm_g, m_kv_norm_g, m_sgu_norm_g, m_sgu_norm_b, m_ln_g, m_ln_b, m_b_spatial),
                 small(v_q_norm_g, v_kv_norm_g, v_sgu_norm_g, v_sgu_norm_b, v_ln_g, v_ln_b, v_b_spatial))

    def ordered(big, sm):
        vec = lambda n: sm[n].reshape(-1)
        return [big[0], vec(0), big[1], vec(1), big[2], vec(2), vec(3), big[4], sm[6], big[3], vec(4), vec(5)]

    loss = r_vec[GV_LOSS, 0]
    return (loss, gx.reshape(1, seq, D_MODEL), *ordered(res[0:5], res[20:27]), *ordered(res[5:10], res[27:34]),
            *ordered(res[10:15], res[34:41]), *ordered(res[15:20], res[41:48]))
```

```python
import math

import jax
import jax.numpy as jnp
import numpy as np
from jax import lax
from jax.experimental import pallas as pl
from jax.experimental.pallas import tpu as pltpu

F32 = jnp.float32
BF16 = jnp.bfloat16

D_MODEL = 1024
Q_LORA = 256
KV_LORA = 128
HEADS = 8
NOPE = 64
ROPE = 32
VDIM = 64
MLA_W = HEADS * VDIM
GW = 512
CHUNK = 128
HP = 128
PAIRS = HEADS // 2
D_IN = 2464
D_INR = 2560
C_CKV = Q_LORA
C_KR = Q_LORA + KV_LORA
C_GATE = C_KR + HP
ROPE_THETA = 10000.0
DN_ALPHA = 2.0 ** 0.25
EPS = 1e-5
SCALE = 1.0 / math.sqrt(NOPE + ROPE)
SCALE_LOG2E = SCALE * 1.4426950408889634
INV_SQRT2 = 0.7071067811865476
INV_SQRT_2PI = 0.3989422804014327

ADAM_LR = 0.001
ADAM_B1 = 0.9
ADAM_B2 = 0.999
ADAM_EPS = 1e-08
ADAM_WD = 0.01
ADAM_STEP = 10

PV_QG, PV_KVG, PV_SG, PV_SB, PV_INVF, PV_M1, PV_M2, PV_LNG, PV_LNB = range(9)
PV_ROWS = 16
GV_QG, GV_KVG, GV_SG, GV_SB, GV_LNG, GV_LNB, GV_LOSS = range(7)
GV_BSP = 8
GV_ROWS = 16

MESH = pl.DeviceIdType.MESH

FWD_TILE = 1024
POST_TILE = 512
BWD_TILE = 512
ATT_BLK = 512
ADAM_STEPS = 4
VMEM_LIMIT = 56 * 1024 * 1024


def _dot(a, b):
    return jnp.dot(a, b, preferred_element_type=F32)


def _dot_nt(a, b):
    return lax.dot_general(a, b, (((1,), (1,)), ((), ())), preferred_element_type=F32)


def _dot_tn(a, b):
    return lax.dot_general(a, b, (((0,), (0,)), ((), ())), preferred_element_type=F32)


def _sigmoid(z):
    return pl.reciprocal(1.0 + jnp.exp(-z), approx=True)


def _gelu_and_grad(x):
    cdf = 0.5 * (1.0 + lax.erf(x * INV_SQRT2))
    return x * cdf, cdf + x * (INV_SQRT_2PI * jnp.exp(-0.5 * x * x))


def _rms_stats(x):
    r = lax.rsqrt(jnp.mean(x * x, axis=-1, keepdims=True) + EPS)
    return x * r, r


def _rms_bwd(dy, g, xh, r):
    dyg = dy * g
    return r * (dyg - xh * jnp.mean(dyg * xh, axis=-1, keepdims=True))


def _ln_stats(x):
    mu = jnp.mean(x, axis=-1, keepdims=True)
    xc = x - mu
    r = lax.rsqrt(jnp.mean(xc * xc, axis=-1, keepdims=True) + EPS)
    return xc * r, r


def _ln_bwd(dy, g, xh, r):
    dxh = dy * g
    return r * (dxh - jnp.mean(dxh, axis=-1, keepdims=True) - xh * jnp.mean(dxh * xh, axis=-1, keepdims=True))


def _rope_fwd(t, c, s1, s2):
    return t * c + pltpu.roll(t, HP - 16, 1) * s1 + pltpu.roll(t, 16, 1) * s2


def _rope_bwd(d, c, s1, s2):
    return d * c + pltpu.roll(d * s1, 16, 1) + pltpu.roll(d * s2, HP - 16, 1)


def _lane_lt64(shape):
    return lax.broadcasted_iota(jnp.int32, shape, len(shape) - 1) < 64


def _spatial_mix(w_ref, src, dst_ref, rows):
    for c in range(rows // CHUNK):
        for p in range(PAIRS):
            blk = src[c * CHUNK:(c + 1) * CHUNK, p * HP:(p + 1) * HP]
            a = _dot(w_ref[2 * p], blk)
            b = _dot(w_ref[2 * p + 1], blk)
            dst_ref[c * CHUNK:(c + 1) * CHUNK, p * HP:(p + 1) * HP] = jnp.where(_lane_lt64(a.shape), a, b)


def _gmlp_fwd(u_pre, v_pre, zb, sg, sb, wt_ref, bsp_ref, sv_ref, rows):
    u, du = _gelu_and_grad(u_pre)
    gv, dgv = _gelu_and_grad(v_pre)
    xh, r = _ln_stats(gv)
    vln = (xh * sg + sb).astype(BF16)
    _spatial_mix(wt_ref, vln, sv_ref, rows)
    bias = bsp_ref[...]
    svb = sv_ref[...] + jnp.concatenate([bias] * (rows // CHUNK), axis=0)
    sig = _sigmoid(zb)
    return u, du, dgv, xh, r, vln, svb, sig


def _weight_gather(shards):
    n_arr = len(shards)

    def body(*refs):
        ins, outs = refs[0:n_arr], refs[n_arr:2 * n_arr]
        send_sems, recv_sems = refs[2 * n_arr:]
        x, y, c = lax.axis_index("x"), lax.axis_index("y"), lax.axis_index("c")
        j = 2 * x + y
        sib = (x, y, 1 - c)
        chips = [(1 - x, y), (x, 1 - y), (1 - x, 1 - y)]
        for n in range(n_arr):
            outs[n][j] = ins[n][...].astype(BF16)

        def half(n, blk, core):
            r = shards[n].shape[0] // 2
            return outs[n].at[blk, pl.ds(pl.multiple_of(core * r, 16), r), :]

        def copy(k, ref, to):
            return pltpu.make_async_remote_copy(
                src_ref=ref, dst_ref=ref, send_sem=send_sems.at[k], recv_sem=recv_sems.at[k],
                device_id=to, device_id_type=MESH)

        first = [copy(6 * n + kk, half(n, j, c), (px, py, c))
                 for n in range(n_arr) for kk, (px, py) in enumerate(chips)]
        for cp in first:
            cp.start()
        passed = []
        for n in range(n_arr):
            for kk, (px, py) in enumerate(chips):
                landed = half(n, 2 * px + py, c)
                copy(6 * n + kk, landed, (px, py, c)).wait_recv()
                passed.append(copy(6 * n + 3 + kk, landed, sib))
                passed[-1].start()
        for n in range(n_arr):
            for kk, (px, py) in enumerate(chips):
                copy(6 * n + 3 + kk, half(n, 2 * px + py, 1 - c), sib).wait_recv()
        for cp in first + passed:
            cp.wait_send()

    vmem = pl.BlockSpec(memory_space=pltpu.VMEM)
    return pl.pallas_call(
        body, name="weight_gather",
        out_shape=[jax.ShapeDtypeStruct((4,) + a.shape, BF16) for a in shards],
        in_specs=[vmem] * n_arr, out_specs=[vmem] * n_arr,
        scratch_shapes=[pltpu.SemaphoreType.DMA((6 * n_arr,)), pltpu.SemaphoreType.DMA((6 * n_arr,))],
        compiler_params=pltpu.CompilerParams(vmem_limit_bytes=VMEM_LIMIT),
    )(*shards)


def _fwd_pre(x, pos, win, wuq, wkv, pvec):
    seq = x.shape[0]
    t = FWD_TILE

    def body(x_ref, pos_ref, win_ref, wuq_ref, wkv_ref, pv_ref,
             cq_o, ckv_o, gate_o, q_o, k_o, v_o, vt_o, cs_o):
        xb = x_ref[...].astype(BF16)
        proj = _dot(xb, win_ref[:, 0:C_GATE])
        cq = proj[:, 0:C_CKV]
        ckv = proj[:, C_CKV:C_KR]
        kr = proj[:, C_KR:C_GATE]
        cq_o[...] = cq
        ckv_o[...] = ckv

        ang = pos_ref[...].astype(F32) * pv_ref[PV_INVF:PV_INVF + 1, 0:HP]
        cos = jnp.cos(ang)
        sin = jnp.sin(ang)
        cs_o[:, 0:HP] = cos
        cs_o[:, HP:2 * HP] = sin
        s1 = sin * pv_ref[PV_M1:PV_M1 + 1, 0:HP]
        s2 = sin * pv_ref[PV_M2:PV_M2 + 1, 0:HP]

        cqh, _ = _rms_stats(cq)
        q_all = _dot((cqh * pv_ref[PV_QG:PV_QG + 1, 0:Q_LORA]).astype(BF16), wuq_ref[...])
        ckvh, _ = _rms_stats(ckv)
        kv_all = _dot((ckvh * pv_ref[PV_KVG:PV_KVG + 1, 0:KV_LORA]).astype(BF16), wkv_ref[...])
        krr = _rope_fwd(kr, cos, s1, s2)
        for h in range(HEADS):
            sl = slice(h * HP, (h + 1) * HP)
            q_o[:, sl] = (_rope_fwd(q_all[:, sl], cos, s1, s2) * SCALE_LOG2E).astype(BF16)
            k_o[:, sl] = (kv_all[:, sl] + krr).astype(BF16)
        val = kv_all[:, HEADS * HP:].astype(BF16)
        v_o[...] = val
        for blk in range(t // ATT_BLK):
            vt_o[blk] = val[blk * ATT_BLK:(blk + 1) * ATT_BLK, :].T
        gate_o[...] = _dot(xb, win_ref[:, C_GATE:D_INR]).astype(BF16)

    tile = lambda w: pl.BlockSpec((t, w), lambda i: (i, 0))
    full = lambda a: pl.BlockSpec(a.shape, lambda i: (0,) * a.ndim)
    outs = [(Q_LORA, F32), (KV_LORA, F32), (2048, BF16), (HEADS * HP, BF16), (HEADS * HP, BF16), (MLA_W, BF16)]
    assert t % ATT_BLK == 0
    out_specs = [tile(w) for w, _ in outs]
    out_specs += [pl.BlockSpec((t // ATT_BLK, MLA_W, ATT_BLK), lambda i: (i, 0, 0)), tile(2 * HP)]
    out_shape = [jax.ShapeDtypeStruct((seq, w), d) for w, d in outs]
    out_shape += [jax.ShapeDtypeStruct((seq // ATT_BLK, MLA_W, ATT_BLK), BF16), jax.ShapeDtypeStruct((seq, 2 * HP), F32)]
    return pl.pallas_call(
        body, name="fwd_pre", grid=(seq // t,),
        in_specs=[tile(D_MODEL), tile(1), full(win), full(wuq), full(wkv), full(pvec)],
        out_specs=out_specs, out_shape=out_shape,
        compiler_params=pltpu.CompilerParams(dimension_semantics=("arbitrary",), vmem_limit_bytes=VMEM_LIMIT),
    )(x, pos, win, wuq, wkv, pvec)


def _attn_fwd(q, k, vt):
    seq = q.shape[0]
    b = ATT_BLK
    nq = seq // b
    assert nq % 2 == 0
    n_off = nq * (nq - 1) // 2

    def body(q_ref, k_ref, vt_ref, o_o, lse_o, m_ref, l_ref, acc_ref, s_even, s_odd):
        m_ref[...] = jnp.full(m_ref.shape, -jnp.inf, F32)
        l_ref[...] = jnp.zeros(l_ref.shape, F32)
        acc_ref[...] = jnp.zeros(acc_ref.shape, F32)

        def scores(i, j, s_ref):
            qrows = pl.ds(pl.multiple_of(i * b, b), b)
            krows = pl.ds(pl.multiple_of(j * b, b), b)
            for a in range(2):
                s_ref[a] = _dot_nt(k_ref[krows, a * HP:(a + 1) * HP], q_ref[qrows, a * HP:(a + 1) * HP])

        def consume(i, j, s_ref, masked):
            vt_blk = vt_ref[j]
            for a in range(2):
                st = s_ref[a]
                if masked:
                    ki = lax.broadcasted_iota(jnp.int32, st.shape, 0)
                    qi = lax.broadcasted_iota(jnp.int32, st.shape, 1)
                    st = jnp.where(ki <= qi, st, -jnp.inf)
                m_prev = m_ref[i, a:a + 1, :]
                m_new = jnp.maximum(m_prev, jnp.max(st, axis=0, keepdims=True))
                alpha = jnp.exp2(m_prev - m_new)
                pt = jnp.exp2(st - m_new)
                l_ref[i, a:a + 1, :] = alpha * l_ref[i, a:a + 1, :] + jnp.sum(pt, axis=0, keepdims=True)
                acc_ref[i, a] = alpha * acc_ref[i, a] + _dot(vt_blk, pt.astype(BF16))
                m_ref[i, a:a + 1, :] = m_new

        def after(i, j):
            wrap = j + 1 >= i
            return jnp.minimum(jnp.where(wrap, i + 1, i), nq - 1), jnp.where(wrap, 0, j + 1)

        if n_off > 0:
            scores(1, 0, s_even)

            def below(u, ij):
                i1, j1 = after(*ij)
                scores(i1, j1, s_odd)
                consume(ij[0], ij[1], s_even, False)
                i2, j2 = after(i1, j1)
                scores(i2, j2, s_even)
                consume(i1, j1, s_odd, False)
                return i2, j2

            last = lax.fori_loop(0, n_off // 2, below, (jnp.int32(1), jnp.int32(0)))
            if n_off % 2:
                consume(last[0], last[1], s_even, False)

        scores(0, 0, s_even)

        def diagonal(u, carry):
            i0 = 2 * u
            scores(i0 + 1, i0 + 1, s_odd)
            consume(i0, i0, s_even, True)
            i2 = jnp.minimum(i0 + 2, nq - 1)
            scores(i2, i2, s_even)
            consume(i0 + 1, i0 + 1, s_odd, True)
            return carry

        lax.fori_loop(0, nq // 2, diagonal, 0)
        top = lax.broadcasted_iota(jnp.int32, (HP, b), 0) < 64

        def finish(i, carry):
            rows = pl.ds(pl.multiple_of(i * b, b), b)
            o_o[rows, :] = jnp.where(top, acc_ref[i, 0] / l_ref[i, 0:1, :], acc_ref[i, 1] / l_ref[i, 1:2, :]).T
            lse_o[i] = m_ref[i, 0:2, :] + jnp.log2(l_ref[i, 0:2, :])
            return carry

        lax.fori_loop(0, nq, finish, 0)

    return pl.pallas_call(
        body, name="attn_fwd", grid=(PAIRS,),
        in_specs=[pl.BlockSpec((seq, 2 * HP), lambda p: (0, p)),
                  pl.BlockSpec((seq, 2 * HP), lambda p: (0, p)),
                  pl.BlockSpec((nq, HP, b), lambda p: (0, p, 0))],
        out_specs=[pl.BlockSpec((seq, HP), lambda p: (0, p)),
                   pl.BlockSpec((None, nq, 2, b), lambda p: (p, 0, 0, 0))],
        out_shape=[jax.ShapeDtypeStruct((seq, MLA_W), F32),
                   jax.ShapeDtypeStruct((PAIRS, nq, 2, b), F32)],
        scratch_shapes=[pltpu.VMEM((nq, 8, b), F32), pltpu.VMEM((nq, 8, b), F32), pltpu.VMEM((nq, 2, HP, b), F32),
                        pltpu.VMEM((2, b, b), F32), pltpu.VMEM((2, b, b), F32)],
        compiler_params=pltpu.CompilerParams(dimension_semantics=("arbitrary",), vmem_limit_bytes=VMEM_LIMIT),
    )(q, k, vt)


def _post(x, tgt, o, gate, wout, pvec, wt, wtt, bsp):
    seq = x.shape[0]
    t = POST_TILE
    nt = seq // t

    def body(x_ref, tgt_ref, o_ref, gate_ref, wout_ref, pv_ref, wt_ref, wtt_ref, bsp_ref,
             dh2_o, do_o, dgate_o, gwout_o, gwsp_o, vec_o, sv_ref, dvln_ref, bacc_ref):
        i = pl.program_id(0)

        @pl.when(i == 0)
        def _():
            gwout_o[...] = jnp.zeros_like(gwout_o)
            gwsp_o[...] = jnp.zeros_like(gwsp_o)
            vec_o[...] = jnp.zeros_like(vec_o)
            bacc_ref[...] = jnp.zeros_like(bacc_ref)

        za = gate_ref[:, 0:512].astype(F32)
        u_pre = gate_ref[:, 512:1024].astype(F32)
        v_pre = gate_ref[:, 1024:1536].astype(F32)
        zb = gate_ref[:, 1536:2048].astype(F32)
        sg = pv_ref[PV_SG:PV_SG + 1, 0:GW]
        sb = pv_ref[PV_SB:PV_SB + 1, 0:GW]
        lng = pv_ref[PV_LNG:PV_LNG + 1, :]
        lnb = pv_ref[PV_LNB:PV_LNB + 1, :]
        o = o_ref[...]

        sig_a = _sigmoid(za)
        silu_a = za * sig_a
        u, du, dgv, xh, r, vln, svb, sig_b = _gmlp_fwd(u_pre, v_pre, zb, sg, sb, wt_ref, bsp_ref, sv_ref, t)
        silu_b = zb * sig_b
        sgu = u * svb
        merged = jnp.concatenate([o * silu_a, sgu * silu_b], axis=1).astype(BF16)
        h2 = DN_ALPHA * x_ref[...] + _dot(merged, wout_ref[...])
        xh2, r2 = _ln_stats(h2)
        err = xh2 * lng + lnb - tgt_ref[...]
        d_out = err * (1.0 / D_MODEL)
        vec_o[GV_LNG:GV_LNG + 1, :] += jnp.sum(d_out * xh2, axis=0, keepdims=True)
        vec_o[GV_LNB:GV_LNB + 1, :] += jnp.sum(d_out, axis=0, keepdims=True)
        vec_o[GV_LOSS:GV_LOSS + 1, :] += jnp.sum(err * err, axis=0, keepdims=True) * (0.5 / D_MODEL)

        d_h2 = _ln_bwd(d_out, lng, xh2, r2)
        dh2_o[...] = d_h2
        dh2b = d_h2.astype(BF16)
        gwout_o[...] += _dot_tn(merged, dh2b)
        d_m = _dot_nt(dh2b, wout_ref[...])
        d_oa = d_m[:, 0:512]
        d_ob = d_m[:, 512:1024]
        do_o[...] = (d_oa * silu_a).astype(BF16)
        dgate_o[:, 0:512] = (d_oa * o * (sig_a * (1.0 + za * (1.0 - sig_a)))).astype(BF16)
        dgate_o[:, 1536:2048] = (d_ob * sgu * (sig_b * (1.0 + zb * (1.0 - sig_b)))).astype(BF16)
        d_sgu = d_ob * silu_b
        dgate_o[:, 512:1024] = (d_sgu * svb * du).astype(BF16)
        d_sv = d_sgu * u
        acc = bacc_ref[...]
        for c in range(t // CHUNK):
            acc = acc + d_sv[c * CHUNK:(c + 1) * CHUNK, :]
        bacc_ref[...] = acc
        d_svb = d_sv.astype(BF16)
        for c in range(t // CHUNK):
            for p in range(PAIRS):
                blk = d_svb[c * CHUNK:(c + 1) * CHUNK, p * HP:(p + 1) * HP]
                vblk = vln[c * CHUNK:(c + 1) * CHUNK, p * HP:(p + 1) * HP]
                first = _lane_lt64(blk.shape)
                gwsp_o[2 * p] += _dot_nt(jnp.where(first, blk, jnp.zeros_like(blk)), vblk)
                gwsp_o[2 * p + 1] += _dot_nt(jnp.where(first, jnp.zeros_like(blk), blk), vblk)
        _spatial_mix(wtt_ref, d_svb, dvln_ref, t)
        d_vln = dvln_ref[...]
        vec_o[GV_SG:GV_SG + 1, 0:GW] += jnp.sum(d_vln * xh, axis=0, keepdims=True)
        vec_o[GV_SB:GV_SB + 1, 0:GW] += jnp.sum(d_vln, axis=0, keepdims=True)
        dgate_o[:, 1024:1536] = (_ln_bwd(d_vln, sg, xh, r) * dgv).astype(BF16)


        @pl.when(i == nt - 1)
        def _():
            tri = (lax.broadcasted_iota(jnp.int32, (CHUNK, CHUNK), 1)
                   <= lax.broadcasted_iota(jnp.int32, (CHUNK, CHUNK), 0))
            for h in range(HEADS):
                gwsp_o[h] = jnp.where(tri, gwsp_o[h], 0.0)
            lane = lax.broadcasted_iota(jnp.int32, (CHUNK, HP), 1)
            res = jnp.zeros((CHUNK, HP), F32)
            for h in range(HEADS):
                p, a = divmod(h, 2)
                blk = bacc_ref[:, p * HP:(p + 1) * HP]
                part = jnp.where(_lane_lt64(blk.shape) == (a == 0), blk, 0.0)
                res = jnp.where(lane == h, jnp.sum(part, axis=-1, keepdims=True), res)
            vec_o[GV_BSP:GV_BSP + HEADS, 0:HP] = res.T[0:HEADS, :]
            lane1 = lax.broadcasted_iota(jnp.int32, (1, D_MODEL), 1)
            total = jnp.sum(vec_o[GV_LOSS:GV_LOSS + 1, :], axis=-1, keepdims=True)
            vec_o[GV_LOSS:GV_LOSS + 1, :] = jnp.where(lane1 == 0, total, 0.0)

    tile = lambda w: pl.BlockSpec((t, w), lambda i: (i, 0))
    full = lambda a: pl.BlockSpec(a.shape, lambda i: (0,) * a.ndim)
    const = lambda s: pl.BlockSpec(s, lambda i: (0,) * len(s))
    return pl.pallas_call(
        body, name="post", grid=(nt,),
        in_specs=[tile(D_MODEL), tile(D_MODEL), tile(MLA_W), tile(2048), full(wout), full(pvec),
                  full(wt), full(wtt), full(bsp)],
        out_specs=[tile(D_MODEL), tile(MLA_W), tile(2048), const((D_MODEL, D_MODEL)),
                   const((HEADS, CHUNK, CHUNK)), const((GV_ROWS, D_MODEL))],
        out_shape=[jax.ShapeDtypeStruct((seq, D_MODEL), F32), jax.ShapeDtypeStruct((seq, MLA_W), BF16),
                   jax.ShapeDtypeStruct((seq, 2048), BF16), jax.ShapeDtypeStruct((D_MODEL, D_MODEL), F32),
                   jax.ShapeDtypeStruct((HEADS, CHUNK, CHUNK), F32), jax.ShapeDtypeStruct((GV_ROWS, D_MODEL), F32)],
        scratch_shapes=[pltpu.VMEM((t, GW), F32), pltpu.VMEM((t, GW), F32), pltpu.VMEM((CHUNK, GW), F32)],
        compiler_params=pltpu.CompilerParams(dimension_semantics=("arbitrary",), vmem_limit_bytes=VMEM_LIMIT),
    )(x, tgt, o, gate, wout, pvec, wt, wtt, bsp)


def _attn_bwd(q, k, v, do, o, lse, cs, pvec):
    seq = q.shape[0]
    b = ATT_BLK
    nq = seq // b

    def body(q_ref, k_ref, v_ref, do_ref, o_ref, lse_ref, cs_ref, pv_ref, dq_o, dk_o, dv_o, dk_acc, dv_acc):
        i = pl.program_id(1)

        @pl.when(i == 0)
        def _():
            dk_acc[...] = jnp.zeros_like(dk_acc)
            dv_acc[...] = jnp.zeros_like(dv_acc)

        first = _lane_lt64((b, HP))
        do = do_ref[...]
        zero = jnp.zeros_like(do)
        dos = [jnp.where(first, do, zero), jnp.where(first, zero, do)]
        prod_t = (do.astype(F32) * o_ref[...]).T
        deltas = [jnp.sum(prod_t[0:64, :], axis=0, keepdims=True),
                  jnp.sum(prod_t[64:128, :], axis=0, keepdims=True)]
        lses = [lse_ref[0:1, :], lse_ref[1:2, :]]
        qs = [q_ref[:, a * HP:(a + 1) * HP] for a in range(2)]

        def step(j, dqs, masked, nk=b):
            rows = pl.ds(pl.multiple_of(j * b, b), nk)
            vb = v_ref[rows, :]
            new_dq = []
            dvs = []
            for a in range(2):
                kb = k_ref[rows, a * HP:(a + 1) * HP]
                pt = jnp.exp2(_dot_nt(kb, qs[a]) - lses[a])
                if masked:
                    ki = lax.broadcasted_iota(jnp.int32, pt.shape, 0)
                    qi = lax.broadcasted_iota(jnp.int32, pt.shape, 1)
                    pt = jnp.where(ki <= qi, pt, 0.0)
                dvs.append(_dot(pt.astype(BF16), do))
                dpt = _dot_nt(vb, dos[a])
                dst = (pt * (dpt - deltas[a])).astype(BF16)
                dk_acc[rows, a * HP:(a + 1) * HP] += _dot(dst, qs[a])
                new_dq.append(dqs[a] + _dot_tn(dst, kb))
            dv_acc[rows, :] += jnp.where(_lane_lt64((nk, HP)), dvs[0], dvs[1])
            return tuple(new_dq)

        init = (jnp.zeros((b, HP), F32), jnp.zeros((b, HP), F32))
        dqs = lax.fori_loop(0, i // 4, lambda jj, cr: step(4 * jj, cr, False, 4 * b), init)
        dqs = lax.fori_loop(0, (i % 4) // 2, lambda _, cr: step(4 * (i // 4), cr, False, 2 * b), dqs)
        dqs = lax.fori_loop(0, i % 2, lambda _, cr: step(i - 1, cr, False), dqs)
        dqs = step(i, dqs, True)
        cos = cs_ref[:, 0:HP]
        sin = cs_ref[:, HP:2 * HP]
        s1 = sin * pv_ref[PV_M1:PV_M1 + 1, 0:HP]
        s2 = sin * pv_ref[PV_M2:PV_M2 + 1, 0:HP]
        for a in range(2):
            dq_o[:, a * HP:(a + 1) * HP] = _rope_bwd(dqs[a] * SCALE, cos, s1, s2).astype(BF16)

        @pl.when(i == nq - 1)
        def _():
            dk_o[...] = (dk_acc[...] * (SCALE / SCALE_LOG2E)).astype(BF16)
            dv_o[...] = dv_acc[...].astype(BF16)

    return pl.pallas_call(
        body, name="attn_bwd", grid=(PAIRS, nq),
        in_specs=[pl.BlockSpec((b, 2 * HP), lambda p, i: (i, p)),
                  pl.BlockSpec((seq, 2 * HP), lambda p, i: (0, p)),
                  pl.BlockSpec((seq, HP), lambda p, i: (0, p)),
                  pl.BlockSpec((b, HP), lambda p, i: (i, p)),
                  pl.BlockSpec((b, HP), lambda p, i: (i, p)),
                  pl.BlockSpec((None, None, 2, b), lambda p, i: (p, i, 0, 0)),
                  pl.BlockSpec((b, 2 * HP), lambda p, i: (i, 0)),
                  pl.BlockSpec(pvec.shape, lambda p, i: (0, 0))],
        out_specs=[pl.BlockSpec((b, 2 * HP), lambda p, i: (i, p)),
                   pl.BlockSpec((seq, 2 * HP), lambda p, i: (0, p)),
                   pl.BlockSpec((seq, HP), lambda p, i: (0, p))],
        out_shape=[jax.ShapeDtypeStruct((seq, HEADS * HP), BF16),
                   jax.ShapeDtypeStruct((seq, HEADS * HP), BF16),
                   jax.ShapeDtypeStruct((seq, MLA_W), BF16)],
        scratch_shapes=[pltpu.VMEM((seq, 2 * HP), F32), pltpu.VMEM((seq, HP), F32)],
        compiler_params=pltpu.CompilerParams(dimension_semantics=("arbitrary", "arbitrary"),
                                             vmem_limit_bytes=VMEM_LIMIT),
    )(q, k, v, do, o, lse, cs, pvec)


def _bwd_pre(x, dh2, cq, ckv, cs, dq, dk, dv, dgate, win, wuq, wkv, pvec, gvec):
    seq = x.shape[0]
    t = BWD_TILE

    def body(x_ref, dh2_ref, cq_ref, ckv_ref, cs_ref, dq_ref, dk_ref, dv_ref, dgate_ref,
             win_ref, wuq_ref, wkv_ref, pv_ref, gv_ref, gx_o, gwin_o, gwuq_o, gwkv_o, vec_o):
        i = pl.program_id(0)

        @pl.when(i == 0)
        def _():
            gwin_o[...] = jnp.zeros_like(gwin_o)
            gwuq_o[...] = jnp.zeros_like(gwuq_o)
            gwkv_o[...] = jnp.zeros_like(gwkv_o)
            vec_o[...] = gv_ref[...]

        xb = x_ref[...].astype(BF16)
        dgate = dgate_ref[...]
        gwin_o[:, C_GATE:D_INR] += _dot_tn(xb, dgate)
        gx_gate = _dot_nt(dgate, win_ref[:, C_GATE:D_INR])

        qg = pv_ref[PV_QG:PV_QG + 1, 0:Q_LORA]
        kvg = pv_ref[PV_KVG:PV_KVG + 1, 0:KV_LORA]
        dq = dq_ref[...]
        cqh, rq = _rms_stats(cq_ref[...])
        d_cqn = _dot_nt(dq, wuq_ref[...])
        gwuq_o[...] += _dot_tn((cqh * qg).astype(BF16), dq)
        vec_o[GV_QG:GV_QG + 1, 0:Q_LORA] += jnp.sum(d_cqn * cqh, axis=0, keepdims=True)
        d_cq = _rms_bwd(d_cqn, qg, cqh, rq)

        dk = dk_ref[...]
        dkv = jnp.concatenate([dk, dv_ref[...]], axis=1)
        ckvh, rkv = _rms_stats(ckv_ref[...])
        d_ckvn = _dot_nt(dkv, wkv_ref[...])
        gwkv_o[...] += _dot_tn((ckvh * kvg).astype(BF16), dkv)
        vec_o[GV_KVG:GV_KVG + 1, 0:KV_LORA] += jnp.sum(d_ckvn * ckvh, axis=0, keepdims=True)
        d_ckv = _rms_bwd(d_ckvn, kvg, ckvh, rkv)

        dks = dk[:, 0:HP].astype(F32)
        for h in range(1, HEADS):
            dks = dks + dk[:, h * HP:(h + 1) * HP].astype(F32)
        cos = cs_ref[:, 0:HP]
        sin = cs_ref[:, HP:2 * HP]
        d_kr = _rope_bwd(dks, cos, sin * pv_ref[PV_M1:PV_M1 + 1, 0:HP], sin * pv_ref[PV_M2:PV_M2 + 1, 0:HP])

        d_lat = jnp.concatenate([d_cq.astype(BF16), d_ckv.astype(BF16), d_kr.astype(BF16)], axis=1)
        gwin_o[:, 0:C_GATE] += _dot_tn(xb, d_lat)
        gx_o[...] = DN_ALPHA * dh2_ref[...] + gx_gate + _dot_nt(d_lat, win_ref[:, 0:C_GATE])

    tile = lambda w: pl.BlockSpec((t, w), lambda i: (i, 0))
    full = lambda a: pl.BlockSpec(a.shape, lambda i: (0,) * a.ndim)
    const = lambda s: pl.BlockSpec(s, lambda i: (0,) * len(s))
    return pl.pallas_call(
        body, name="bwd_pre", grid=(seq // t,),
        in_specs=[tile(D_MODEL), tile(D_MODEL), tile(Q_LORA), tile(KV_LORA), tile(2 * HP), tile(HEADS * HP),
                  tile(HEADS * HP), tile(MLA_W), tile(2048), full(win), full(wuq), full(wkv), full(pvec), full(gvec)],
        out_specs=[tile(D_MODEL), const((D_MODEL, D_INR)), const((Q_LORA, HEADS * HP)),
                   const((KV_LORA, HEADS * HP + MLA_W)), const((GV_ROWS, D_MODEL))],
        out_shape=[jax.ShapeDtypeStruct((seq, D_MODEL), F32), jax.ShapeDtypeStruct((D_MODEL, D_INR), F32),
                   jax.ShapeDtypeStruct((Q_LORA, HEADS * HP), F32),
                   jax.ShapeDtypeStruct((KV_LORA, HEADS * HP + MLA_W), F32),
                   jax.ShapeDtypeStruct((GV_ROWS, D_MODEL), F32)],
        compiler_params=pltpu.CompilerParams(dimension_semantics=("arbitrary",), vmem_limit_bytes=VMEM_LIMIT),
    )(x, dh2, cq, ckv, cs, dq, dk, dv, dgate, win, wuq, wkv, pvec, gvec)


def _grad_reduce(gs, gvec):
    n_arr = len(gs)
    n_big = n_arr - 1
    k1 = lambda n, blk: 4 * n + blk
    k2 = lambda n, kk: 4 * n_arr + 3 * n + kk
    k3 = lambda n: 7 * n_arr + n
    k3w = lambda k: 7 * n_arr + n_big + k
    kv = lambda k: 7 * n_arr + n_big + 7 + k
    n_sem = 7 * n_arr + n_big + 14

    def body(*refs):
        g, gv = refs[0:n_arr], refs[n_arr]
        outs, ov = refs[n_arr + 1:2 * n_arr + 1], refs[2 * n_arr + 1]
        r1 = refs[2 * n_arr + 2:3 * n_arr + 2]
        r2 = refs[3 * n_arr + 2:4 * n_arr + 2]
        s2 = refs[4 * n_arr + 2:5 * n_arr + 2]
        vbuf, send_sems, recv_sems = refs[5 * n_arr + 2:]
        x, y, c = lax.axis_index("x"), lax.axis_index("y"), lax.axis_index("c")
        j = 2 * x + y
        me = 2 * j + c
        sib = (x, y, 1 - c)
        chips = [(1 - x, y), (x, 1 - y), (1 - x, 1 - y)]
        others = [sib] + [(px, py, pc) for (px, py) in chips for pc in (c, 1 - c)]

        def copy(k, src, dst, to):
            return pltpu.make_async_remote_copy(
                src_ref=src, dst_ref=dst, send_sem=send_sems.at[k], recv_sem=recv_sems.at[k],
                device_id=to, device_id_type=MESH)

        l1 = [copy(k1(n, blk), g[n].at[blk, 1 - c], r1[n].at[blk], sib) for n in range(n_arr) for blk in range(4)]
        lv = [copy(kv(k), gv, vbuf.at[me], to) for k, to in enumerate(others)]
        for cp in l1 + lv:
            cp.start()
        l2 = []
        for n in range(n_arr):
            for blk in range(4):
                copy(k1(n, blk), g[n].at[blk, c], r1[n].at[blk], sib).wait_recv()
            for blk in range(4):
                r1[n][blk] = g[n][blk, c] + r1[n][blk]
                s2[n][blk] = r1[n][blk].astype(BF16)
            for kk, (px, py) in enumerate(chips):
                l2.append(copy(k2(n, kk), s2[n].at[2 * px + py], r2[n].at[kk], (px, py, c)))
                l2[-1].start()

        l3 = []
        for n in range(n_arr):
            for kk in range(3):
                copy(k2(n, kk), s2[n].at[0], r2[n].at[kk], sib).wait_recv()
            red = ((r1[n][j] + r2[n][0].astype(F32)) + r2[n][1].astype(F32)) + r2[n][2].astype(F32)
            if n < n_big:
                outs[n][c] = red
                back = [copy(k3(n), outs[n].at[c], outs[n].at[c], sib)]
            else:
                outs[n][j, c] = red
                back = [copy(k3w(k), outs[n].at[j, c], outs[n].at[j, c], to) for k, to in enumerate(others)]
            for cp in back:
                cp.start()
            l3 += back
        for n in range(n_big):
            copy(k3(n), outs[n].at[1 - c], outs[n].at[1 - c], sib).wait_recv()
        for k, (px, py, pc) in enumerate(others):
            landed = outs[n_big].at[2 * px + py, pc]
            copy(k3w(k), landed, landed, (px, py, pc)).wait_recv()
            copy(kv(k), gv, vbuf.at[4 * px + 2 * py + pc], (px, py, pc)).wait_recv()
        vbuf[me] = gv[...]
        total = vbuf[0]
        for d in range(1, 8):
            total = total + vbuf[d]
        ov[...] = total
        for cp in l1 + lv + l2 + l3:
            cp.wait_send()

    vmem = pl.BlockSpec(memory_space=pltpu.VMEM)
    half_shapes = [a.shape[2:] for a in gs]
    out_shape = [jax.ShapeDtypeStruct((2,) + s, F32) for s in half_shapes[:n_big]]
    out_shape += [jax.ShapeDtypeStruct((4, 2) + half_shapes[n_big], F32), jax.ShapeDtypeStruct(gvec.shape, F32)]
    scratch = [pltpu.VMEM((4,) + s, F32) for s in half_shapes] + [pltpu.VMEM((3,) + s, BF16) for s in half_shapes]
    scratch += [pltpu.VMEM((4,) + s, BF16) for s in half_shapes]
    scratch += [pltpu.VMEM((8,) + gvec.shape, F32), pltpu.SemaphoreType.DMA((n_sem,)), pltpu.SemaphoreType.DMA((n_sem,))]
    return pl.pallas_call(
        body, name="grad_reduce", out_shape=out_shape,
        in_specs=[vmem] * (n_arr + 1), out_specs=[vmem] * (n_arr + 1), scratch_shapes=scratch,
        compiler_params=pltpu.CompilerParams(vmem_limit_bytes=VMEM_LIMIT),
    )(*gs, gvec)


SMALL_ROWS = ((GV_QG, 1, Q_LORA), (GV_KVG, 1, KV_LORA), (GV_SG, 1, GW), (GV_SB, 1, GW),
              (GV_LNG, 1, D_MODEL), (GV_LNB, 1, D_MODEL), (GV_BSP, HEADS, CHUNK))


def _adam_update(g, w, m, v):
    m_new = ADAM_B1 * m + (1.0 - ADAM_B1) * g
    v_new = ADAM_B2 * v + (1.0 - ADAM_B2) * (g * g)
    m_hat = m_new / (1.0 - ADAM_B1 ** ADAM_STEP)
    v_hat = v_new / (1.0 - ADAM_B2 ** ADAM_STEP)
    return -ADAM_LR * (m_hat / (jnp.sqrt(v_hat) + ADAM_EPS) + ADAM_WD * w), m_new, v_new


def _adamw(g_big, w_big, m_big, v_big, gvec, w_small, m_small, v_small):
    nb, ns = len(g_big), len(w_small)

    def body(*refs):
        it = iter(refs)
        take = lambda n: [next(it) for _ in range(n)]
        g_b, w_b, m_b, v_b = take(nb), take(nb), take(nb), take(nb)
        gv = next(it)
        w_s, m_s, v_s = take(ns), take(ns), take(ns)
        g_bo, d_bo, m_bo, v_bo = take(nb), take(nb), take(nb), take(nb)
        g_so, d_so, m_so, v_so = take(ns), take(ns), take(ns), take(ns)
        for n in range(nb):
            gb = g_b[n][...]
            g_bo[n][...] = gb
            d_bo[n][...], m_bo[n][...], v_bo[n][...] = _adam_update(gb, w_b[n][...], m_b[n][...], v_b[n][...])
        for n, (row, nrow, width) in enumerate(SMALL_ROWS):
            gs = gv[row:row + nrow, 0:width]
            g_so[n][...] = gs
            d_so[n][...], m_so[n][...], v_so[n][...] = _adam_update(gs, w_s[n][...], m_s[n][...], v_s[n][...])

    def rows(a):
        nd = a.ndim
        return pl.BlockSpec((a.shape[0] // ADAM_STEPS,) + a.shape[1:], lambda i: (i,) + (0,) * (nd - 1))

    def whole(a):
        nd = a.ndim
        return pl.BlockSpec(a.shape, lambda i: (0,) * nd)

    big = [jax.ShapeDtypeStruct(a.shape, F32) for a in w_big]
    small = [jax.ShapeDtypeStruct(a.shape, F32) for a in w_small]
    return pl.pallas_call(
        body, name="adamw", grid=(ADAM_STEPS,), out_shape=big * 4 + small * 4,
        in_specs=[rows(a) for a in g_big + w_big + m_big + v_big] + [whole(gvec)]
        + [whole(a) for a in w_small + m_small + v_small],
        out_specs=[rows(a) for a in w_big] * 4 + [whole(a) for a in w_small] * 4,
        compiler_params=pltpu.CompilerParams(dimension_semantics=("arbitrary",), vmem_limit_bytes=VMEM_LIMIT),
    )(*g_big, *w_big, *m_big, *v_big, gvec, *w_small, *m_small, *v_small)


def kernel(x, positions, w_in, q_norm_g, w_uq, kv_norm_g, w_ukv, sgu_norm_g, sgu_norm_b, w_spatial, b_spatial, w_out, ln_g, ln_b, loss_target, m_w_in, m_q_norm_g, m_w_uq, m_kv_norm_g, m_w_ukv, m_sgu_norm_g, m_sgu_norm_b, m_w_spatial, m_b_spatial, m_w_out, m_ln_g, m_ln_b, v_w_in, v_q_norm_g, v_w_uq, v_kv_norm_g, v_w_ukv, v_sgu_norm_g, v_sgu_norm_b, v_w_spatial, v_b_spatial, v_w_out, v_ln_g, v_ln_b):
    seq = x.shape[1]
    x2 = x.reshape(seq, D_MODEL)
    tgt = loss_target.reshape(seq, D_MODEL)
    pos = positions.reshape(seq, 1)

    a_in, a_uq, a_ukv, a_out = _weight_gather([w_in, w_uq, w_ukv, w_out])
    w_uq_f = jnp.swapaxes(a_uq, 0, 1).reshape(Q_LORA, HEADS * (NOPE + ROPE))
    w_ukv_f = jnp.swapaxes(a_ukv, 0, 1).reshape(KV_LORA, HEADS * (NOPE + VDIM))
    wout = a_out.reshape(D_MODEL, D_MODEL)
    zc = lambda n: jnp.zeros((D_MODEL, n), BF16)
    win = jnp.concatenate([a_in[0][:, 0:C_KR], zc(NOPE), a_in[0][:, C_KR:C_KR + ROPE], zc(HP - NOPE - ROPE),
                           a_in[0][:, C_KR + ROPE:],
                           a_in[1], a_in[2], a_in[3]], axis=1)
    wuq = jnp.pad(w_uq_f.reshape(Q_LORA, HEADS, NOPE + ROPE), ((0, 0), (0, 0), (0, HP - NOPE - ROPE)))
    wuq = wuq.reshape(Q_LORA, HEADS * HP)
    ukv = w_ukv_f.reshape(KV_LORA, HEADS, NOPE + VDIM)
    wk = jnp.pad(ukv[:, :, 0:NOPE], ((0, 0), (0, 0), (0, HP - NOPE))).reshape(KV_LORA, HEADS * HP)
    wkv = jnp.concatenate([wk, ukv[:, :, NOPE:].reshape(KV_LORA, MLA_W)], axis=1)

    lane = np.arange(HP)
    half = ROPE // 2
    inv_freq = (1.0 / (ROPE_THETA ** (np.arange(half, dtype=np.float32) / half))).astype(np.float32)
    in_rope = (lane >= NOPE) & (lane < NOPE + ROPE)
    invf = jnp.asarray(np.where(in_rope, inv_freq[(lane - NOPE) % half], 0.0).astype(np.float32))
    m1 = jnp.asarray(np.where((lane >= NOPE) & (lane < NOPE + half), -1.0, 0.0).astype(np.float32))
    m2 = jnp.asarray(np.where((lane >= NOPE + half) & (lane < NOPE + ROPE), 1.0, 0.0).astype(np.float32))
    row = lambda a: jnp.pad(a.astype(F32), (0, D_MODEL - a.shape[0]))
    pvec = jnp.stack([row(q_norm_g), row(kv_norm_g), row(sgu_norm_g), row(sgu_norm_b), row(invf), row(m1),
                      row(m2), row(ln_g), row(ln_b)] + [jnp.zeros((D_MODEL,), F32)] * (PV_ROWS - 9))
    tri = jnp.tril(jnp.ones((CHUNK, CHUNK), dtype=bool))
    wt = jnp.where(tri[None], w_spatial, 0.0).astype(BF16)
    wtt = jnp.swapaxes(wt, 1, 2)
    bsp = jnp.repeat(b_spatial.T, VDIM, axis=1)

    cq, ckv, gate, q, k, v, vt, cs = _fwd_pre(x2, pos, win, wuq, wkv, pvec)
    o, lse = _attn_fwd(q, k, vt)
    dh2, do, dgate, g_wout, g_wsp, gvec = _post(x2, tgt, o, gate, wout, pvec, wt, wtt, bsp)
    dq, dk, dv = _attn_bwd(q, k, v, do, o, lse, cs, pvec)
    gx, g_win, g_wuq, g_wkv, gvec = _bwd_pre(x2, dh2, cq, ckv, cs, dq, dk, dv, dgate, win, wuq, wkv, pvec, gvec)

    cw = w_in.shape[1]
    first = D_INR - 3 * cw
    g_win_0 = jnp.concatenate([g_win[:, 0:C_KR], g_win[:, C_KR + NOPE:C_KR + NOPE + ROPE], g_win[:, C_GATE:first]],
                              axis=1)
    g_win_b = jnp.stack([g_win_0] + [g_win[:, first + cw * jb:first + cw * (jb + 1)] for jb in range(3)])
    g_wuq_f = g_wuq.reshape(Q_LORA, HEADS, HP)[:, :, 0:NOPE + ROPE].reshape(Q_LORA, HEADS * (NOPE + ROPE))
    g_k = g_wkv[:, 0:HEADS * HP].reshape(KV_LORA, HEADS, HP)[:, :, 0:NOPE]
    g_v = g_wkv[:, HEADS * HP:].reshape(KV_LORA, HEADS, VDIM)
    g_wukv_f = jnp.concatenate([g_k, g_v], axis=2).reshape(KV_LORA, HEADS * (NOPE + VDIM))

    def by_chip(a):
        rows, cols = a.shape[0], a.shape[1] // 4
        return jnp.swapaxes(a.reshape(rows, 4, cols), 0, 1).reshape(4, 2, rows // 2, cols)

    gs = [g_win_b.reshape(4, 2, D_MODEL // 2, cw), by_chip(g_wuq_f), by_chip(g_wukv_f), g_wout.reshape(4, 2, 128, D_MODEL),
          g_wsp.reshape(4, 2, CHUNK, CHUNK)]
    r_in, r_uq, r_ukv, r_out, r_wsp, r_vec = _grad_reduce(gs, gvec)

    g_big = [r_in.reshape(w_in.shape), r_uq.reshape(w_uq.shape), r_ukv.reshape(w_ukv.shape),
             r_out.reshape(w_out.shape), r_wsp.reshape(w_spatial.shape)]
    small = lambda qg, kvg, sg, sb, lng, lnb, bs: [qg.reshape(1, -1), kvg.reshape(1, -1), sg.reshape(1, -1),
                                                   sb.reshape(1, -1), lng.reshape(1, -1), lnb.reshape(1, -1), bs]
    res = _adamw(g_big, [w_in, w_uq, w_ukv, w_out, w_spatial], [m_w_in, m_w_uq, m_w_ukv, m_w_out, m_w_spatial],
                 [v_w_in, v_w_uq, v_w_ukv, v_w_out, v_w_spatial], r_vec,
                 small(q_norm_g, kv_norm_g, sgu_norm_g, sgu_norm_b, ln_g, ln_b, b_spatial),
                 small(m_q_norm_g, m_kv_norm_g, m_sgu_norm_g, m_sgu_norm_b, m_ln_g, m_ln_b, m_b_spatial),
                 small(v_q_norm_g, v_kv_norm_g, v_sgu_norm_g, v_sgu_norm_b, v_ln_g, v_ln_b, v_b_spatial))

    def ordered(big, sm):
        vec = lambda n: sm[n].reshape(-1)
        return [big[0], vec(0), big[1], vec(1), big[2], vec(2), vec(3), big[4], sm[6], big[3], vec(4), vec(5)]

    loss = r_vec[GV_LOSS, 0]
    return (loss, gx.reshape(1, seq, D_MODEL), *ordered(res[0:5], res[20:27]), *ordered(res[5:10], res[27:34]),
            *ordered(res[10:15], res[34:41]), *ordered(res[15:20], res[41:48]))
```

```python
import math

import jax
import jax.numpy as jnp
import numpy as np
from jax import lax
from jax.experimental import pallas as pl
from jax.experimental.pallas import tpu as pltpu

F32 = jnp.float32
BF16 = jnp.bfloat16

D_MODEL = 1024
Q_LORA = 256
KV_LORA = 128
HEADS = 8
NOPE = 64
ROPE = 32
VDIM = 64
MLA_W = HEADS * VDIM
GW = 512
CHUNK = 128
HP = 128
PAIRS = HEADS // 2
D_IN = 2464
D_INR = 2560
C_CKV = Q_LORA
C_KR = Q_LORA + KV_LORA
C_GATE = C_KR + HP
ROPE_THETA = 10000.0
DN_ALPHA = 2.0 ** 0.25
EPS = 1e-5
SCALE = 1.0 / math.sqrt(NOPE + ROPE)
SCALE_LOG2E = SCALE * 1.4426950408889634
INV_SQRT2 = 0.7071067811865476
INV_SQRT_2PI = 0.3989422804014327

ADAM_LR = 0.001
ADAM_B1 = 0.9
ADAM_B2 = 0.999
ADAM_EPS = 1e-08
ADAM_WD = 0.01
ADAM_STEP = 10

PV_QG, PV_KVG, PV_SG, PV_SB, PV_INVF, PV_M1, PV_M2, PV_LNG, PV_LNB = range(9)
PV_ROWS = 16
GV_QG, GV_KVG, GV_SG, GV_SB, GV_LNG, GV_LNB, GV_LOSS = range(7)
GV_BSP = 8
GV_ROWS = 16

MESH = pl.DeviceIdType.MESH

FWD_TILE = 1024
POST_TILE = 512
BWD_TILE = 512
ATT_BLK = 512
ADAM_STEPS = 4
VMEM_LIMIT = 56 * 1024 * 1024


def _dot(a, b):
    return jnp.dot(a, b, preferred_element_type=F32)


def _dot_nt(a, b):
    return lax.dot_general(a, b, (((1,), (1,)), ((), ())), preferred_element_type=F32)


def _dot_tn(a, b):
    return lax.dot_general(a, b, (((0,), (0,)), ((), ())), preferred_element_type=F32)


def _sigmoid(z):
    return pl.reciprocal(1.0 + jnp.exp(-z), approx=True)


def _gelu_and_grad(x):
    cdf = 0.5 * (1.0 + lax.erf(x * INV_SQRT2))
    return x * cdf, cdf + x * (INV_SQRT_2PI * jnp.exp(-0.5 * x * x))


def _rms_stats(x):
    r = lax.rsqrt(jnp.mean(x * x, axis=-1, keepdims=True) + EPS)
    return x * r, r


def _rms_bwd(dy, g, xh, r):
    dyg = dy * g
    return r * (dyg - xh * jnp.mean(dyg * xh, axis=-1, keepdims=True))


def _ln_stats(x):
    mu = jnp.mean(x, axis=-1, keepdims=True)
    xc = x - mu
    r = lax.rsqrt(jnp.mean(xc * xc, axis=-1, keepdims=True) + EPS)
    return xc * r, r


def _ln_bwd(dy, g, xh, r):
    dxh = dy * g
    return r * (dxh - jnp.mean(dxh, axis=-1, keepdims=True) - xh * jnp.mean(dxh * xh, axis=-1, keepdims=True))


def _rope_fwd(t, c, s1, s2):
    return t * c + pltpu.roll(t, HP - 16, 1) * s1 + pltpu.roll(t, 16, 1) * s2


def _rope_bwd(d, c, s1, s2):
    return d * c + pltpu.roll(d * s1, 16, 1) + pltpu.roll(d * s2, HP - 16, 1)


def _lane_lt64(shape):
    return lax.broadcasted_iota(jnp.int32, shape, len(shape) - 1) < 64


def _spatial_mix(w_ref, src, dst_ref, rows):
    for c in range(rows // CHUNK):
        for p in range(PAIRS):
            blk = src[c * CHUNK:(c + 1) * CHUNK, p * HP:(p + 1) * HP]
            a = _dot(w_ref[2 * p], blk)
            b = _dot(w_ref[2 * p + 1], blk)
            dst_ref[c * CHUNK:(c + 1) * CHUNK, p * HP:(p + 1) * HP] = jnp.where(_lane_lt64(a.shape), a, b)


def _gmlp_fwd(u_pre, v_pre, zb, sg, sb, wt_ref, bsp_ref, sv_ref, rows):
    u, du = _gelu_and_grad(u_pre)
    gv, dgv = _gelu_and_grad(v_pre)
    xh, r = _ln_stats(gv)
    vln = (xh * sg + sb).astype(BF16)
    _spatial_mix(wt_ref, vln, sv_ref, rows)
    bias = bsp_ref[...]
    svb = sv_ref[...] + jnp.concatenate([bias] * (rows // CHUNK), axis=0)
    sig = _sigmoid(zb)
    return u, du, dgv, xh, r, vln, svb, sig


def _weight_gather(shards):
    n_arr = len(shards)

    def body(*refs):
        ins, outs = refs[0:n_arr], refs[n_arr:2 * n_arr]
        send_sems, recv_sems = refs[2 * n_arr:]
        x, y, c = lax.axis_index("x"), lax.axis_index("y"), lax.axis_index("c")
        j = 2 * x + y
        sib = (x, y, 1 - c)
        chips = [(1 - x, y), (x, 1 - y), (1 - x, 1 - y)]
        for n in range(n_arr):
            outs[n][j] = ins[n][...].astype(BF16)

        def half(n, blk, core):
            r = shards[n].shape[0] // 2
            return outs[n].at[blk, pl.ds(pl.multiple_of(core * r, 16), r), :]

        def copy(k, ref, to):
            return pltpu.make_async_remote_copy(
                src_ref=ref, dst_ref=ref, send_sem=send_sems.at[k], recv_sem=recv_sems.at[k],
                device_id=to, device_id_type=MESH)

        first = [copy(6 * n + kk, half(n, j, c), (px, py, c))
                 for n in range(n_arr) for kk, (px, py) in enumerate(chips)]
        for cp in first:
            cp.start()
        passed = []
        for n in range(n_arr):
            for kk, (px, py) in enumerate(chips):
                landed = half(n, 2 * px + py, c)
                copy(6 * n + kk, landed, (px, py, c)).wait_recv()
                passed.append(copy(6 * n + 3 + kk, landed, sib))
                passed[-1].start()
        for n in range(n_arr):
            for kk, (px, py) in enumerate(chips):
                copy(6 * n + 3 + kk, half(n, 2 * px + py, 1 - c), sib).wait_recv()
        for cp in first + passed:
            cp.wait_send()

    vmem = pl.BlockSpec(memory_space=pltpu.VMEM)
    return pl.pallas_call(
        body, name="weight_gather",
        out_shape=[jax.ShapeDtypeStruct((4,) + a.shape, BF16) for a in shards],
        in_specs=[vmem] * n_arr, out_specs=[vmem] * n_arr,
        scratch_shapes=[pltpu.SemaphoreType.DMA((6 * n_arr,)), pltpu.SemaphoreType.DMA((6 * n_arr,))],
        compiler_params=pltpu.CompilerParams(vmem_limit_bytes=VMEM_LIMIT),
    )(*shards)


def _fwd_pre(x, pos, win, wuq, wkv, pvec):
    seq = x.shape[0]
    t = FWD_TILE

    def body(x_ref, pos_ref, win_ref, wuq_ref, wkv_ref, pv_ref,
             cq_o, ckv_o, gate_o, q_o, k_o, v_o, vt_o, cs_o):
        xb = x_ref[...].astype(BF16)
        proj = _dot(xb, win_ref[:, 0:C_GATE])
        cq = proj[:, 0:C_CKV]
        ckv = proj[:, C_CKV:C_KR]
        kr = proj[:, C_KR:C_GATE]
        cq_o[...] = cq
        ckv_o[...] = ckv

        ang = pos_ref[...].astype(F32) * pv_ref[PV_INVF:PV_INVF + 1, 0:HP]
        cos = jnp.cos(ang)
        sin = jnp.sin(ang)
        cs_o[:, 0:HP] = cos
        cs_o[:, HP:2 * HP] = sin
        s1 = sin * pv_ref[PV_M1:PV_M1 + 1, 0:HP]
        s2 = sin * pv_ref[PV_M2:PV_M2 + 1, 0:HP]

        cqh, _ = _rms_stats(cq)
        q_all = _dot((cqh * pv_ref[PV_QG:PV_QG + 1, 0:Q_LORA]).astype(BF16), wuq_ref[...])
        ckvh, _ = _rms_stats(ckv)
        kv_all = _dot((ckvh * pv_ref[PV_KVG:PV_KVG + 1, 0:KV_LORA]).astype(BF16), wkv_ref[...])
        krr = _rope_fwd(kr, cos, s1, s2)
        for h in range(HEADS):
            sl = slice(h * HP, (h + 1) * HP)
            q_o[:, sl] = (_rope_fwd(q_all[:, sl], cos, s1, s2) * SCALE_LOG2E).astype(BF16)
            k_o[:, sl] = (kv_all[:, sl] + krr).astype(BF16)
        val = kv_all[:, HEADS * HP:].astype(BF16)
        v_o[...] = val
        for blk in range(t // ATT_BLK):
            vt_o[blk] = val[blk * ATT_BLK:(blk + 1) * ATT_BLK, :].T
        gate_o[...] = _dot(xb, win_ref[:, C_GATE:D_INR]).astype(BF16)

    tile = lambda w: pl.BlockSpec((t, w), lambda i: (i, 0))
    full = lambda a: pl.BlockSpec(a.shape, lambda i: (0,) * a.ndim)
    outs = [(Q_LORA, F32), (KV_LORA, F32), (2048, BF16), (HEADS * HP, BF16), (HEADS * HP, BF16), (MLA_W, BF16)]
    assert t % ATT_BLK == 0
    out_specs = [tile(w) for w, _ in outs]
    out_specs += [pl.BlockSpec((t // ATT_BLK, MLA_W, ATT_BLK), lambda i: (i, 0, 0)), tile(2 * HP)]
    out_shape = [jax.ShapeDtypeStruct((seq, w), d) for w, d in outs]
    out_shape += [jax.ShapeDtypeStruct((seq // ATT_BLK, MLA_W, ATT_BLK), BF16), jax.ShapeDtypeStruct((seq, 2 * HP), F32)]
    return pl.pallas_call(
        body, name="fwd_pre", grid=(seq // t,),
        in_specs=[tile(D_MODEL), tile(1), full(win), full(wuq), full(wkv), full(pvec)],
        out_specs=out_specs, out_shape=out_shape,
        compiler_params=pltpu.CompilerParams(dimension_semantics=("arbitrary",), vmem_limit_bytes=VMEM_LIMIT),
    )(x, pos, win, wuq, wkv, pvec)


def _attn_fwd(q, k, vt):
    seq = q.shape[0]
    b = ATT_BLK
    nq = seq // b
    assert nq % 2 == 0
    n_wide = sum(i // 2 for i in range(nq))

    def body(q_ref, k_ref, vt_ref, o_o, lse_o, m_ref, l_ref, acc_ref, s_even, s_odd):
        m_ref[...] = jnp.full(m_ref.shape, -jnp.inf, F32)
        l_ref[...] = jnp.zeros(l_ref.shape, F32)
        acc_ref[...] = jnp.zeros(acc_ref.shape, F32)

        def scores(i, j, s_ref, nkb):
            qrows = pl.ds(pl.multiple_of(i * b, b), b)
            krows = pl.ds(pl.multiple_of(j * b, b), nkb * b)
            for a in range(2):
                s_ref[a, 0:nkb * b, :] = _dot_nt(k_ref[krows, a * HP:(a + 1) * HP], q_ref[qrows, a * HP:(a + 1) * HP])

        def consume(i, j, s_ref, nkb, masked):
            for a in range(2):
                st = s_ref[a, 0:nkb * b, :]
                if masked:
                    ki = lax.broadcasted_iota(jnp.int32, st.shape, 0)
                    qi = lax.broadcasted_iota(jnp.int32, st.shape, 1)
                    st = jnp.where(ki <= qi, st, -jnp.inf)
                m_prev = m_ref[i, a:a + 1, :]
                m_new = jnp.maximum(m_prev, jnp.max(st, axis=0, keepdims=True))
                alpha = jnp.exp2(m_prev - m_new)
                pt = jnp.exp2(st - m_new)
                ptb = pt.astype(BF16)
                l_ref[i, a:a + 1, :] = alpha * l_ref[i, a:a + 1, :] + jnp.sum(pt, axis=0, keepdims=True)
                pv = _dot(vt_ref[j], ptb[0:b, :])
                for kb in range(1, nkb):
                    pv = pv + _dot(vt_ref[j + kb], ptb[kb * b:(kb + 1) * b, :])
                acc_ref[i, a] = alpha * acc_ref[i, a] + pv
                m_ref[i, a:a + 1, :] = m_new

        def run(count, first, following, nkb, masked):
            if count == 0:
                return
            scores(*first, s_even, nkb)

            def two(u, ij):
                nxt = following(*ij)
                scores(*nxt, s_odd, nkb)
                consume(*ij, s_even, nkb, masked)
                nxt2 = following(*nxt)
                scores(*nxt2, s_even, nkb)
                consume(*nxt, s_odd, nkb, masked)
                return nxt2

            last = lax.fori_loop(0, count // 2, two, tuple(jnp.int32(c) for c in first))
            if count % 2:
                consume(*last, s_even, nkb, masked)

        def clamp(i):
            return jnp.minimum(i, nq - 1)

        def next_wide(i, j):
            wrap = j + 4 > i
            return clamp(jnp.where(wrap, i + 1, i)), jnp.where(wrap, 0, j + 2)

        run(n_wide, (2, 0), next_wide, 2, False)
        run(nq // 2, (1, 0), lambda i, j: (clamp(i + 2), jnp.minimum(j + 2, nq - 2)), 1, False)
        run(nq, (0, 0), lambda i, j: (clamp(i + 1), clamp(j + 1)), 1, True)
        top = lax.broadcasted_iota(jnp.int32, (HP, b), 0) < 64

        def finish(i, carry):
            rows = pl.ds(pl.multiple_of(i * b, b), b)
            o_o[rows, :] = jnp.where(top, acc_ref[i, 0] / l_ref[i, 0:1, :], acc_ref[i, 1] / l_ref[i, 1:2, :]).T
            lse_o[i] = m_ref[i, 0:2, :] + jnp.log2(l_ref[i, 0:2, :])
            return carry

        lax.fori_loop(0, nq, finish, 0)

    return pl.pallas_call(
        body, name="attn_fwd", grid=(PAIRS,),
        in_specs=[pl.BlockSpec((seq, 2 * HP), lambda p: (0, p)),
                  pl.BlockSpec((seq, 2 * HP), lambda p: (0, p)),
                  pl.BlockSpec((nq, HP, b), lambda p: (0, p, 0))],
        out_specs=[pl.BlockSpec((seq, HP), lambda p: (0, p)),
                   pl.BlockSpec((None, nq, 2, b), lambda p: (p, 0, 0, 0))],
        out_shape=[jax.ShapeDtypeStruct((seq, MLA_W), F32),
                   jax.ShapeDtypeStruct((PAIRS, nq, 2, b), F32)],
        scratch_shapes=[pltpu.VMEM((nq, 8, b), F32), pltpu.VMEM((nq, 8, b), F32), pltpu.VMEM((nq, 2, HP, b), F32),
                        pltpu.VMEM((2, 2 * b, b), F32), pltpu.VMEM((2, 2 * b, b), F32)],
        compiler_params=pltpu.CompilerParams(dimension_semantics=("arbitrary",), vmem_limit_bytes=VMEM_LIMIT),
    )(q, k, vt)


def _post(x, tgt, o, gate, wout, pvec, wt, wtt, bsp):
    seq = x.shape[0]
    t = POST_TILE
    nt = seq // t

    def body(x_ref, tgt_ref, o_ref, gate_ref, wout_ref, pv_ref, wt_ref, wtt_ref, bsp_ref,
             dh2_o, do_o, dgate_o, gwout_o, gwsp_o, vec_o, sv_ref, dvln_ref, bacc_ref):
        i = pl.program_id(0)

        @pl.when(i == 0)
        def _():
            gwout_o[...] = jnp.zeros_like(gwout_o)
            gwsp_o[...] = jnp.zeros_like(gwsp_o)
            vec_o[...] = jnp.zeros_like(vec_o)
            bacc_ref[...] = jnp.zeros_like(bacc_ref)

        za = gate_ref[:, 0:512].astype(F32)
        u_pre = gate_ref[:, 512:1024].astype(F32)
        v_pre = gate_ref[:, 1024:1536].astype(F32)
        zb = gate_ref[:, 1536:2048].astype(F32)
        sg = pv_ref[PV_SG:PV_SG + 1, 0:GW]
        sb = pv_ref[PV_SB:PV_SB + 1, 0:GW]
        lng = pv_ref[PV_LNG:PV_LNG + 1, :]
        lnb = pv_ref[PV_LNB:PV_LNB + 1, :]
        o = o_ref[...]

        sig_a = _sigmoid(za)
        silu_a = za * sig_a
        u, du, dgv, xh, r, vln, svb, sig_b = _gmlp_fwd(u_pre, v_pre, zb, sg, sb, wt_ref, bsp_ref, sv_ref, t)
        silu_b = zb * sig_b
        sgu = u * svb
        merged = jnp.concatenate([o * silu_a, sgu * silu_b], axis=1).astype(BF16)
        h2 = DN_ALPHA * x_ref[...] + _dot(merged, wout_ref[...])
        xh2, r2 = _ln_stats(h2)
        err = xh2 * lng + lnb - tgt_ref[...]
        d_out = err * (1.0 / D_MODEL)
        vec_o[GV_LNG:GV_LNG + 1, :] += jnp.sum(d_out * xh2, axis=0, keepdims=True)
        vec_o[GV_LNB:GV_LNB + 1, :] += jnp.sum(d_out, axis=0, keepdims=True)
        vec_o[GV_LOSS:GV_LOSS + 1, :] += jnp.sum(err * err, axis=0, keepdims=True) * (0.5 / D_MODEL)

        d_h2 = _ln_bwd(d_out, lng, xh2, r2)
        dh2_o[...] = d_h2
        dh2b = d_h2.astype(BF16)
        gwout_o[...] += _dot_tn(merged, dh2b)
        d_m = _dot_nt(dh2b, wout_ref[...])
        d_oa = d_m[:, 0:512]
        d_ob = d_m[:, 512:1024]
        do_o[...] = (d_oa * silu_a).astype(BF16)
        dgate_o[:, 0:512] = (d_oa * o * (sig_a * (1.0 + za * (1.0 - sig_a)))).astype(BF16)
        dgate_o[:, 1536:2048] = (d_ob * sgu * (sig_b * (1.0 + zb * (1.0 - sig_b)))).astype(BF16)
        d_sgu = d_ob * silu_b
        dgate_o[:, 512:1024] = (d_sgu * svb * du).astype(BF16)
        d_sv = d_sgu * u
        acc = bacc_ref[...]
        for c in range(t // CHUNK):
            acc = acc + d_sv[c * CHUNK:(c + 1) * CHUNK, :]
        bacc_ref[...] = acc
        d_svb = d_sv.astype(BF16)
        for c in range(t // CHUNK):
            for p in range(PAIRS):
                blk = d_svb[c * CHUNK:(c + 1) * CHUNK, p * HP:(p + 1) * HP]
                vblk = vln[c * CHUNK:(c + 1) * CHUNK, p * HP:(p + 1) * HP]
                first = _lane_lt64(blk.shape)
                gwsp_o[2 * p] += _dot_nt(jnp.where(first, blk, jnp.zeros_like(blk)), vblk)
                gwsp_o[2 * p + 1] += _dot_nt(jnp.where(first, jnp.zeros_like(blk), blk), vblk)
        _spatial_mix(wtt_ref, d_svb, dvln_ref, t)
        d_vln = dvln_ref[...]
        vec_o[GV_SG:GV_SG + 1, 0:GW] += jnp.sum(d_vln * xh, axis=0, keepdims=True)
        vec_o[GV_SB:GV_SB + 1, 0:GW] += jnp.sum(d_vln, axis=0, keepdims=True)
        dgate_o[:, 1024:1536] = (_ln_bwd(d_vln, sg, xh, r) * dgv).astype(BF16)


        @pl.when(i == nt - 1)
        def _():
            tri = (lax.broadcasted_iota(jnp.int32, (CHUNK, CHUNK), 1)
                   <= lax.broadcasted_iota(jnp.int32, (CHUNK, CHUNK), 0))
            for h in range(HEADS):
                gwsp_o[h] = jnp.where(tri, gwsp_o[h], 0.0)
            lane = lax.broadcasted_iota(jnp.int32, (CHUNK, HP), 1)
            res = jnp.zeros((CHUNK, HP), F32)
            for h in range(HEADS):
                p, a = divmod(h, 2)
                blk = bacc_ref[:, p * HP:(p + 1) * HP]
                part = jnp.where(_lane_lt64(blk.shape) == (a == 0), blk, 0.0)
                res = jnp.where(lane == h, jnp.sum(part, axis=-1, keepdims=True), res)
            vec_o[GV_BSP:GV_BSP + HEADS, 0:HP] = res.T[0:HEADS, :]
            lane1 = lax.broadcasted_iota(jnp.int32, (1, D_MODEL), 1)
            total = jnp.sum(vec_o[GV_LOSS:GV_LOSS + 1, :], axis=-1, keepdims=True)
            vec_o[GV_LOSS:GV_LOSS + 1, :] = jnp.where(lane1 == 0, total, 0.0)

    tile = lambda w: pl.BlockSpec((t, w), lambda i: (i, 0))
    full = lambda a: pl.BlockSpec(a.shape, lambda i: (0,) * a.ndim)
    const = lambda s: pl.BlockSpec(s, lambda i: (0,) * len(s))
    return pl.pallas_call(
        body, name="post", grid=(nt,),
        in_specs=[tile(D_MODEL), tile(D_MODEL), tile(MLA_W), tile(2048), full(wout), full(pvec),
                  full(wt), full(wtt), full(bsp)],
        out_specs=[tile(D_MODEL), tile(MLA_W), tile(2048), const((D_MODEL, D_MODEL)),
                   const((HEADS, CHUNK, CHUNK)), const((GV_ROWS, D_MODEL))],
        out_shape=[jax.ShapeDtypeStruct((seq, D_MODEL), F32), jax.ShapeDtypeStruct((seq, MLA_W), BF16),
                   jax.ShapeDtypeStruct((seq, 2048), BF16), jax.ShapeDtypeStruct((D_MODEL, D_MODEL), F32),
                   jax.ShapeDtypeStruct((HEADS, CHUNK, CHUNK), F32), jax.ShapeDtypeStruct((GV_ROWS, D_MODEL), F32)],
        scratch_shapes=[pltpu.VMEM((t, GW), F32), pltpu.VMEM((t, GW), F32), pltpu.VMEM((CHUNK, GW), F32)],
        compiler_params=pltpu.CompilerParams(dimension_semantics=("arbitrary",), vmem_limit_bytes=VMEM_LIMIT),
    )(x, tgt, o, gate, wout, pvec, wt, wtt, bsp)


def _attn_bwd(q, k, v, do, o, lse, cs, pvec):
    seq = q.shape[0]
    b = ATT_BLK
    nq = seq // b

    def body(q_ref, k_ref, v_ref, do_ref, o_ref, lse_ref, cs_ref, pv_ref, dq_o, dk_o, dv_o, dk_acc, dv_acc):
        i = pl.program_id(1)

        @pl.when(i == 0)
        def _():
            dk_acc[...] = jnp.zeros_like(dk_acc)
            dv_acc[...] = jnp.zeros_like(dv_acc)

        first = _lane_lt64((b, HP))
        do = do_ref[...]
        zero = jnp.zeros_like(do)
        dos = [jnp.where(first, do, zero), jnp.where(first, zero, do)]
        prod_t = (do.astype(F32) * o_ref[...]).T
        deltas = [jnp.sum(prod_t[0:64, :], axis=0, keepdims=True),
                  jnp.sum(prod_t[64:128, :], axis=0, keepdims=True)]
        lses = [lse_ref[0:1, :], lse_ref[1:2, :]]
        qs = [q_ref[:, a * HP:(a + 1) * HP] for a in range(2)]

        def step(j, dqs, masked, nk=b):
            rows = pl.ds(pl.multiple_of(j * b, b), nk)
            vb = v_ref[rows, :]
            new_dq = []
            dvs = []
            for a in range(2):
                kb = k_ref[rows, a * HP:(a + 1) * HP]
                pt = jnp.exp2(_dot_nt(kb, qs[a]) - lses[a])
                if masked:
                    ki = lax.broadcasted_iota(jnp.int32, pt.shape, 0)
                    qi = lax.broadcasted_iota(jnp.int32, pt.shape, 1)
                    pt = jnp.where(ki <= qi, pt, 0.0)
                dvs.append(_dot(pt.astype(BF16), do))
                dpt = _dot_nt(vb, dos[a])
                dst = (pt * (dpt - deltas[a])).astype(BF16)
                dk_acc[rows, a * HP:(a + 1) * HP] += _dot(dst, qs[a])
                new_dq.append(dqs[a] + _dot_tn(dst, kb))
            dv_acc[rows, :] += jnp.where(_lane_lt64((nk, HP)), dvs[0], dvs[1])
            return tuple(new_dq)

        init = (jnp.zeros((b, HP), F32), jnp.zeros((b, HP), F32))
        dqs = lax.fori_loop(0, i // 4, lambda jj, cr: step(4 * jj, cr, False, 4 * b), init)
        dqs = lax.fori_loop(0, (i % 4) // 2, lambda _, cr: step(4 * (i // 4), cr, False, 2 * b), dqs)
        dqs = lax.fori_loop(0, i % 2, lambda _, cr: step(i - 1, cr, False), dqs)
        dqs = step(i, dqs, True)
        cos = cs_ref[:, 0:HP]
        sin = cs_ref[:, HP:2 * HP]
        s1 = sin * pv_ref[PV_M1:PV_M1 + 1, 0:HP]
        s2 = sin * pv_ref[PV_M2:PV_M2 + 1, 0:HP]
        for a in range(2):
            dq_o[:, a * HP:(a + 1) * HP] = _rope_bwd(dqs[a] * SCALE, cos, s1, s2).astype(BF16)

        @pl.when(i == nq - 1)
        def _():
            dk_o[...] = (dk_acc[...] * (SCALE / SCALE_LOG2E)).astype(BF16)
            dv_o[...] = dv_acc[...].astype(BF16)

    return pl.pallas_call(
        body, name="attn_bwd", grid=(PAIRS, nq),
        in_specs=[pl.BlockSpec((b, 2 * HP), lambda p, i: (i, p)),
                  pl.BlockSpec((seq, 2 * HP), lambda p, i: (0, p)),
                  pl.BlockSpec((seq, HP), lambda p, i: (0, p)),
                  pl.BlockSpec((b, HP), lambda p, i: (i, p)),
                  pl.BlockSpec((b, HP), lambda p, i: (i, p)),
                  pl.BlockSpec((None, None, 2, b), lambda p, i: (p, i, 0, 0)),
                  pl.BlockSpec((b, 2 * HP), lambda p, i: (i, 0)),
                  pl.BlockSpec(pvec.shape, lambda p, i: (0, 0))],
        out_specs=[pl.BlockSpec((b, 2 * HP), lambda p, i: (i, p)),
                   pl.BlockSpec((seq, 2 * HP), lambda p, i: (0, p)),
                   pl.BlockSpec((seq, HP), lambda p, i: (0, p))],
        out_shape=[jax.ShapeDtypeStruct((seq, HEADS * HP), BF16),
                   jax.ShapeDtypeStruct((seq, HEADS * HP), BF16),
                   jax.ShapeDtypeStruct((seq, MLA_W), BF16)],
        scratch_shapes=[pltpu.VMEM((seq, 2 * HP), F32), pltpu.VMEM((seq, HP), F32)],
        compiler_params=pltpu.CompilerParams(dimension_semantics=("arbitrary", "arbitrary"),
                                             vmem_limit_bytes=VMEM_LIMIT),
    )(q, k, v, do, o, lse, cs, pvec)


def _bwd_pre(x, dh2, cq, ckv, cs, dq, dk, dv, dgate, win, wuq, wkv, pvec, gvec):
    seq = x.shape[0]
    t = BWD_TILE

    def body(x_ref, dh2_ref, cq_ref, ckv_ref, cs_ref, dq_ref, dk_ref, dv_ref, dgate_ref,
             win_ref, wuq_ref, wkv_ref, pv_ref, gv_ref, gx_o, gwin_o, gwuq_o, gwkv_o, vec_o):
        i = pl.program_id(0)

        @pl.when(i == 0)
        def _():
            gwin_o[...] = jnp.zeros_like(gwin_o)
            gwuq_o[...] = jnp.zeros_like(gwuq_o)
            gwkv_o[...] = jnp.zeros_like(gwkv_o)
            vec_o[...] = gv_ref[...]

        xb = x_ref[...].astype(BF16)
        dgate = dgate_ref[...]
        gwin_o[:, C_GATE:D_INR] += _dot_tn(xb, dgate)
        gx_gate = _dot_nt(dgate, win_ref[:, C_GATE:D_INR])

        qg = pv_ref[PV_QG:PV_QG + 1, 0:Q_LORA]
        kvg = pv_ref[PV_KVG:PV_KVG + 1, 0:KV_LORA]
        dq = dq_ref[...]
        cqh, rq = _rms_stats(cq_ref[...])
        d_cqn = _dot_nt(dq, wuq_ref[...])
        gwuq_o[...] += _dot_tn((cqh * qg).astype(BF16), dq)
        vec_o[GV_QG:GV_QG + 1, 0:Q_LORA] += jnp.sum(d_cqn * cqh, axis=0, keepdims=True)
        d_cq = _rms_bwd(d_cqn, qg, cqh, rq)

        dk = dk_ref[...]
        dkv = jnp.concatenate([dk, dv_ref[...]], axis=1)
        ckvh, rkv = _rms_stats(ckv_ref[...])
        d_ckvn = _dot_nt(dkv, wkv_ref[...])
        gwkv_o[...] += _dot_tn((ckvh * kvg).astype(BF16), dkv)
        vec_o[GV_KVG:GV_KVG + 1, 0:KV_LORA] += jnp.sum(d_ckvn * ckvh, axis=0, keepdims=True)
        d_ckv = _rms_bwd(d_ckvn, kvg, ckvh, rkv)

        dks = dk[:, 0:HP].astype(F32)
        for h in range(1, HEADS):
            dks = dks + dk[:, h * HP:(h + 1) * HP].astype(F32)
        cos = cs_ref[:, 0:HP]
        sin = cs_ref[:, HP:2 * HP]
        d_kr = _rope_bwd(dks, cos, sin * pv_ref[PV_M1:PV_M1 + 1, 0:HP], sin * pv_ref[PV_M2:PV_M2 + 1, 0:HP])

        d_lat = jnp.concatenate([d_cq.astype(BF16), d_ckv.astype(BF16), d_kr.astype(BF16)], axis=1)
        gwin_o[:, 0:C_GATE] += _dot_tn(xb, d_lat)
        gx_o[...] = DN_ALPHA * dh2_ref[...] + gx_gate + _dot_nt(d_lat, win_ref[:, 0:C_GATE])

    tile = lambda w: pl.BlockSpec((t, w), lambda i: (i, 0))
    full = lambda a: pl.BlockSpec(a.shape, lambda i: (0,) * a.ndim)
    const = lambda s: pl.BlockSpec(s, lambda i: (0,) * len(s))
    return pl.pallas_call(
        body, name="bwd_pre", grid=(seq // t,),
        in_specs=[tile(D_MODEL), tile(D_MODEL), tile(Q_LORA), tile(KV_LORA), tile(2 * HP), tile(HEADS * HP),
                  tile(HEADS * HP), tile(MLA_W), tile(2048), full(win), full(wuq), full(wkv), full(pvec), full(gvec)],
        out_specs=[tile(D_MODEL), const((D_MODEL, D_INR)), const((Q_LORA, HEADS * HP)),
                   const((KV_LORA, HEADS * HP + MLA_W)), const((GV_ROWS, D_MODEL))],
        out_shape=[jax.ShapeDtypeStruct((seq, D_MODEL), F32), jax.ShapeDtypeStruct((D_MODEL, D_INR), F32),
                   jax.ShapeDtypeStruct((Q_LORA, HEADS * HP), F32),
                   jax.ShapeDtypeStruct((KV_LORA, HEADS * HP + MLA_W), F32),
                   jax.ShapeDtypeStruct((GV_ROWS, D_MODEL), F32)],
        compiler_params=pltpu.CompilerParams(dimension_semantics=("arbitrary",), vmem_limit_bytes=VMEM_LIMIT),
    )(x, dh2, cq, ckv, cs, dq, dk, dv, dgate, win, wuq, wkv, pvec, gvec)


def _grad_reduce(gs, gvec):
    n_arr = len(gs)
    n_big = n_arr - 1
    k1 = lambda n, blk: 4 * n + blk
    k2 = lambda n, kk: 4 * n_arr + 3 * n + kk
    k3 = lambda n: 7 * n_arr + n
    k3w = lambda k: 7 * n_arr + n_big + k
    kv = lambda k: 7 * n_arr + n_big + 7 + k
    n_sem = 7 * n_arr + n_big + 14

    def body(*refs):
        g, gv = refs[0:n_arr], refs[n_arr]
        outs, ov = refs[n_arr + 1:2 * n_arr + 1], refs[2 * n_arr + 1]
        r1 = refs[2 * n_arr + 2:3 * n_arr + 2]
        r2 = refs[3 * n_arr + 2:4 * n_arr + 2]
        s2 = refs[4 * n_arr + 2:5 * n_arr + 2]
        vbuf, send_sems, recv_sems = refs[5 * n_arr + 2:]
        x, y, c = lax.axis_index("x"), lax.axis_index("y"), lax.axis_index("c")
        j = 2 * x + y
        me = 2 * j + c
        sib = (x, y, 1 - c)
        chips = [(1 - x, y), (x, 1 - y), (1 - x, 1 - y)]
        others = [sib] + [(px, py, pc) for (px, py) in chips for pc in (c, 1 - c)]

        def copy(k, src, dst, to):
            return pltpu.make_async_remote_copy(
                src_ref=src, dst_ref=dst, send_sem=send_sems.at[k], recv_sem=recv_sems.at[k],
                device_id=to, device_id_type=MESH)

        l1 = [copy(k1(n, blk), g[n].at[blk, 1 - c], r1[n].at[blk], sib) for n in range(n_arr) for blk in range(4)]
        lv = [copy(kv(k), gv, vbuf.at[me], to) for k, to in enumerate(others)]
        for cp in l1 + lv:
            cp.start()
        l2 = []
        for n in range(n_arr):
            for blk in range(4):
                copy(k1(n, blk), g[n].at[blk, c], r1[n].at[blk], sib).wait_recv()
            for blk in range(4):
                r1[n][blk] = g[n][blk, c] + r1[n][blk]
                s2[n][blk] = r1[n][blk].astype(BF16)
            for kk, (px, py) in enumerate(chips):
                l2.append(copy(k2(n, kk), s2[n].at[2 * px + py], r2[n].at[kk], (px, py, c)))
                l2[-1].start()

        l3 = []
        for n in range(n_arr):
            for kk in range(3):
                copy(k2(n, kk), s2[n].at[0], r2[n].at[kk], sib).wait_recv()
            red = ((r1[n][j] + r2[n][0].astype(F32)) + r2[n][1].astype(F32)) + r2[n][2].astype(F32)
            if n < n_big:
                outs[n][c] = red
                back = [copy(k3(n), outs[n].at[c], outs[n].at[c], sib)]
            else:
                outs[n][j, c] = red
                back = [copy(k3w(k), outs[n].at[j, c], outs[n].at[j, c], to) for k, to in enumerate(others)]
            for cp in back:
                cp.start()
            l3 += back
        for n in range(n_big):
            copy(k3(n), outs[n].at[1 - c], outs[n].at[1 - c], sib).wait_recv()
        for k, (px, py, pc) in enumerate(others):
            landed = outs[n_big].at[2 * px + py, pc]
            copy(k3w(k), landed, landed, (px, py, pc)).wait_recv()
            copy(kv(k), gv, vbuf.at[4 * px + 2 * py + pc], (px, py, pc)).wait_recv()
        vbuf[me] = gv[...]
        total = vbuf[0]
        for d in range(1, 8):
            total = total + vbuf[d]
        ov[...] = total
        for cp in l1 + lv + l2 + l3:
            cp.wait_send()

    vmem = pl.BlockSpec(memory_space=pltpu.VMEM)
    half_shapes = [a.shape[2:] for a in gs]
    out_shape = [jax.ShapeDtypeStruct((2,) + s, F32) for s in half_shapes[:n_big]]
    out_shape += [jax.ShapeDtypeStruct((4, 2) + half_shapes[n_big], F32), jax.ShapeDtypeStruct(gvec.shape, F32)]
    scratch = [pltpu.VMEM((4,) + s, F32) for s in half_shapes] + [pltpu.VMEM((3,) + s, BF16) for s in half_shapes]
    scratch += [pltpu.VMEM((4,) + s, BF16) for s in half_shapes]
    scratch += [pltpu.VMEM((8,) + gvec.shape, F32), pltpu.SemaphoreType.DMA((n_sem,)), pltpu.SemaphoreType.DMA((n_sem,))]
    return pl.pallas_call(
        body, name="grad_reduce", out_shape=out_shape,
        in_specs=[vmem] * (n_arr + 1), out_specs=[vmem] * (n_arr + 1), scratch_shapes=scratch,
        compiler_params=pltpu.CompilerParams(vmem_limit_bytes=VMEM_LIMIT),
    )(*gs, gvec)


SMALL_ROWS = ((GV_QG, 1, Q_LORA), (GV_KVG, 1, KV_LORA), (GV_SG, 1, GW), (GV_SB, 1, GW),
              (GV_LNG, 1, D_MODEL), (GV_LNB, 1, D_MODEL), (GV_BSP, HEADS, CHUNK))


def _adam_update(g, w, m, v):
    m_new = ADAM_B1 * m + (1.0 - ADAM_B1) * g
    v_new = ADAM_B2 * v + (1.0 - ADAM_B2) * (g * g)
    m_hat = m_new / (1.0 - ADAM_B1 ** ADAM_STEP)
    v_hat = v_new / (1.0 - ADAM_B2 ** ADAM_STEP)
    return -ADAM_LR * (m_hat / (jnp.sqrt(v_hat) + ADAM_EPS) + ADAM_WD * w), m_new, v_new


def _adamw(g_big, w_big, m_big, v_big, gvec, w_small, m_small, v_small):
    nb, ns = len(g_big), len(w_small)

    def body(*refs):
        it = iter(refs)
        take = lambda n: [next(it) for _ in range(n)]
        g_b, w_b, m_b, v_b = take(nb), take(nb), take(nb), take(nb)
        gv = next(it)
        w_s, m_s, v_s = take(ns), take(ns), take(ns)
        g_bo, d_bo, m_bo, v_bo = take(nb), take(nb), take(nb), take(nb)
        g_so, d_so, m_so, v_so = take(ns), take(ns), take(ns), take(ns)
        for n in range(nb):
            gb = g_b[n][...]
            g_bo[n][...] = gb
            d_bo[n][...], m_bo[n][...], v_bo[n][...] = _adam_update(gb, w_b[n][...], m_b[n][...], v_b[n][...])
        for n, (row, nrow, width) in enumerate(SMALL_ROWS):
            gs = gv[row:row + nrow, 0:width]
            g_so[n][...] = gs
            d_so[n][...], m_so[n][...], v_so[n][...] = _adam_update(gs, w_s[n][...], m_s[n][...], v_s[n][...])

    def rows(a):
        nd = a.ndim
        return pl.BlockSpec((a.shape[0] // ADAM_STEPS,) + a.shape[1:], lambda i: (i,) + (0,) * (nd - 1))

    def whole(a):
        nd = a.ndim
        return pl.BlockSpec(a.shape, lambda i: (0,) * nd)

    big = [jax.ShapeDtypeStruct(a.shape, F32) for a in w_big]
    small = [jax.ShapeDtypeStruct(a.shape, F32) for a in w_small]
    return pl.pallas_call(
        body, name="adamw", grid=(ADAM_STEPS,), out_shape=big * 4 + small * 4,
        in_specs=[rows(a) for a in g_big + w_big + m_big + v_big] + [whole(gvec)]
        + [whole(a) for a in w_small + m_small + v_small],
        out_specs=[rows(a) for a in w_big] * 4 + [whole(a) for a in w_small] * 4,
        compiler_params=pltpu.CompilerParams(dimension_semantics=("arbitrary",), vmem_limit_bytes=VMEM_LIMIT),
    )(*g_big, *w_big, *m_big, *v_big, gvec, *w_small, *m_small, *v_small)


def kernel(x, positions, w_in, q_norm_g, w_uq, kv_norm_g, w_ukv, sgu_norm_g, sgu_norm_b, w_spatial, b_spatial, w_out, ln_g, ln_b, loss_target, m_w_in, m_q_norm_g, m_w_uq, m_kv_norm_g, m_w_ukv, m_sgu_norm_g, m_sgu_norm_b, m_w_spatial, m_b_spatial, m_w_out, m_ln_g, m_ln_b, v_w_in, v_q_norm_g, v_w_uq, v_kv_norm_g, v_w_ukv, v_sgu_norm_g, v_sgu_norm_b, v_w_spatial, v_b_spatial, v_w_out, v_ln_g, v_ln_b):
    seq = x.shape[1]
    x2 = x.reshape(seq, D_MODEL)
    tgt = loss_target.reshape(seq, D_MODEL)
    pos = positions.reshape(seq, 1)

    a_in, a_uq, a_ukv, a_out = _weight_gather([w_in, w_uq, w_ukv, w_out])
    w_uq_f = jnp.swapaxes(a_uq, 0, 1).reshape(Q_LORA, HEADS * (NOPE + ROPE))
    w_ukv_f = jnp.swapaxes(a_ukv, 0, 1).reshape(KV_LORA, HEADS * (NOPE + VDIM))
    wout = a_out.reshape(D_MODEL, D_MODEL)
    zc = lambda n: jnp.zeros((D_MODEL, n), BF16)
    win = jnp.concatenate([a_in[0][:, 0:C_KR], zc(NOPE), a_in[0][:, C_KR:C_KR + ROPE], zc(HP - NOPE - ROPE),
                           a_in[0][:, C_KR + ROPE:],
                           a_in[1], a_in[2], a_in[3]], axis=1)
    wuq = jnp.pad(w_uq_f.reshape(Q_LORA, HEADS, NOPE + ROPE), ((0, 0), (0, 0), (0, HP - NOPE - ROPE)))
    wuq = wuq.reshape(Q_LORA, HEADS * HP)
    ukv = w_ukv_f.reshape(KV_LORA, HEADS, NOPE + VDIM)
    wk = jnp.pad(ukv[:, :, 0:NOPE], ((0, 0), (0, 0), (0, HP - NOPE))).reshape(KV_LORA, HEADS * HP)
    wkv = jnp.concatenate([wk, ukv[:, :, NOPE:].reshape(KV_LORA, MLA_W)], axis=1)

    lane = np.arange(HP)
    half = ROPE // 2
    inv_freq = (1.0 / (ROPE_THETA ** (np.arange(half, dtype=np.float32) / half))).astype(np.float32)
    in_rope = (lane >= NOPE) & (lane < NOPE + ROPE)
    invf = jnp.asarray(np.where(in_rope, inv_freq[(lane - NOPE) % half], 0.0).astype(np.float32))
    m1 = jnp.asarray(np.where((lane >= NOPE) & (lane < NOPE + half), -1.0, 0.0).astype(np.float32))
    m2 = jnp.asarray(np.where((lane >= NOPE + half) & (lane < NOPE + ROPE), 1.0, 0.0).astype(np.float32))
    row = lambda a: jnp.pad(a.astype(F32), (0, D_MODEL - a.shape[0]))
    pvec = jnp.stack([row(q_norm_g), row(kv_norm_g), row(sgu_norm_g), row(sgu_norm_b), row(invf), row(m1),
                      row(m2), row(ln_g), row(ln_b)] + [jnp.zeros((D_MODEL,), F32)] * (PV_ROWS - 9))
    tri = jnp.tril(jnp.ones((CHUNK, CHUNK), dtype=bool))
    wt = jnp.where(tri[None], w_spatial, 0.0).astype(BF16)
    wtt = jnp.swapaxes(wt, 1, 2)
    bsp = jnp.repeat(b_spatial.T, VDIM, axis=1)

    cq, ckv, gate, q, k, v, vt, cs = _fwd_pre(x2, pos, win, wuq, wkv, pvec)
    o, lse = _attn_fwd(q, k, vt)
    dh2, do, dgate, g_wout, g_wsp, gvec = _post(x2, tgt, o, gate, wout, pvec, wt, wtt, bsp)
    dq, dk, dv = _attn_bwd(q, k, v, do, o, lse, cs, pvec)
    gx, g_win, g_wuq, g_wkv, gvec = _bwd_pre(x2, dh2, cq, ckv, cs, dq, dk, dv, dgate, win, wuq, wkv, pvec, gvec)

    cw = w_in.shape[1]
    first = D_INR - 3 * cw
    g_win_0 = jnp.concatenate([g_win[:, 0:C_KR], g_win[:, C_KR + NOPE:C_KR + NOPE + ROPE], g_win[:, C_GATE:first]],
                              axis=1)
    g_win_b = jnp.stack([g_win_0] + [g_win[:, first + cw * jb:first + cw * (jb + 1)] for jb in range(3)])
    g_wuq_f = g_wuq.reshape(Q_LORA, HEADS, HP)[:, :, 0:NOPE + ROPE].reshape(Q_LORA, HEADS * (NOPE + ROPE))
    g_k = g_wkv[:, 0:HEADS * HP].reshape(KV_LORA, HEADS, HP)[:, :, 0:NOPE]
    g_v = g_wkv[:, HEADS * HP:].reshape(KV_LORA, HEADS, VDIM)
    g_wukv_f = jnp.concatenate([g_k, g_v], axis=2).reshape(KV_LORA, HEADS * (NOPE + VDIM))

    def by_chip(a):
        rows, cols = a.shape[0], a.shape[1] // 4
        return jnp.swapaxes(a.reshape(rows, 4, cols), 0, 1).reshape(4, 2, rows // 2, cols)

    gs = [g_win_b.reshape(4, 2, D_MODEL // 2, cw), by_chip(g_wuq_f), by_chip(g_wukv_f), g_wout.reshape(4, 2, 128, D_MODEL),
          g_wsp.reshape(4, 2, CHUNK, CHUNK)]
    r_in, r_uq, r_ukv, r_out, r_wsp, r_vec = _grad_reduce(gs, gvec)

    g_big = [r_in.reshape(w_in.shape), r_uq.reshape(w_uq.shape), r_ukv.reshape(w_ukv.shape),
             r_out.reshape(w_out.shape), r_wsp.reshape(w_spatial.shape)]
    small = lambda qg, kvg, sg, sb, lng, lnb, bs: [qg.reshape(1, -1), kvg.reshape(1, -1), sg.reshape(1, -1),
                                                   sb.reshape(1, -1), lng.reshape(1, -1), lnb.reshape(1, -1), bs]
    res = _adamw(g_big, [w_in, w_uq, w_ukv, w_out, w_spatial], [m_w_in, m_w_uq, m_w_ukv, m_w_out, m_w_spatial],
                 [v_w_in, v_w_uq, v_w_ukv, v_w_out, v_w_spatial], r_vec,
                 small(q_norm_g, kv_norm_g, sgu_norm_g, sgu_norm_b, ln_g, ln_b, b_spatial),
                 small(m_q_norm_g, m_kv_norm_g, m_sgu_norm_g, m_sgu_norm_b, m_ln_g, m_ln_b, m_b_spatial),
                 small(v_q_norm_g, v_kv_norm_g, v_sgu_norm_g, v_sgu_norm_b, v_ln_g, v_ln_b, v_b_spatial))

    def ordered(big, sm):
        vec = lambda n: sm[n].reshape(-1)
        return [big[0], vec(0), big[1], vec(1), big[2], vec(2), vec(3), big[4], sm[6], big[3], vec(4), vec(5)]

    loss = r_vec[GV_LOSS, 0]
    return (loss, gx.reshape(1, seq, D_MODEL), *ordered(res[0:5], res[20:27]), *ordered(res[5:10], res[27:34]),
            *ordered(res[10:15], res[34:41]), *ordered(res[15:20], res[41:48]))
```

```python
import math

import jax
import jax.numpy as jnp
import numpy as np
from jax import lax
from jax.experimental import pallas as pl
from jax.experimental.pallas import tpu as pltpu

F32 = jnp.float32
BF16 = jnp.bfloat16

D_MODEL = 1024
Q_LORA = 256
KV_LORA = 128
HEADS = 8
NOPE = 64
ROPE = 32
VDIM = 64
MLA_W = HEADS * VDIM
GW = 512
CHUNK = 128
HP = 128
PAIRS = HEADS // 2
D_IN = 2464
D_INR = 2560
C_CKV = Q_LORA
C_KR = Q_LORA + KV_LORA
C_GATE = C_KR + HP
ROPE_THETA = 10000.0
DN_ALPHA = 2.0 ** 0.25
EPS = 1e-5
SCALE = 1.0 / math.sqrt(NOPE + ROPE)
SCALE_LOG2E = SCALE * 1.4426950408889634
INV_SQRT2 = 0.7071067811865476
INV_SQRT_2PI = 0.3989422804014327

ADAM_LR = 0.001
ADAM_B1 = 0.9
ADAM_B2 = 0.999
ADAM_EPS = 1e-08
ADAM_WD = 0.01
ADAM_STEP = 10

PV_QG, PV_KVG, PV_SG, PV_SB, PV_INVF, PV_M1, PV_M2, PV_LNG, PV_LNB = range(9)
PV_ROWS = 16
GV_QG, GV_KVG, GV_SG, GV_SB, GV_LNG, GV_LNB, GV_LOSS = range(7)
GV_BSP = 8
GV_ROWS = 16

MESH = pl.DeviceIdType.MESH

FWD_TILE = 1024
POST_TILE = 512
BWD_TILE = 512
ATT_BLK = 512
ADAM_STEPS = 4
VMEM_LIMIT = 56 * 1024 * 1024


def _dot(a, b):
    return jnp.dot(a, b, preferred_element_type=F32)


def _dot_nt(a, b):
    return lax.dot_general(a, b, (((1,), (1,)), ((), ())), preferred_element_type=F32)


def _dot_tn(a, b):
    return lax.dot_general(a, b, (((0,), (0,)), ((), ())), preferred_element_type=F32)


def _sigmoid(z):
    return pl.reciprocal(1.0 + jnp.exp(-z), approx=True)


def _gelu_and_grad(x):
    cdf = 0.5 * (1.0 + lax.erf(x * INV_SQRT2))
    return x * cdf, cdf + x * (INV_SQRT_2PI * jnp.exp(-0.5 * x * x))


def _rms_stats(x):
    r = lax.rsqrt(jnp.mean(x * x, axis=-1, keepdims=True) + EPS)
    return x * r, r


def _rms_bwd(dy, g, xh, r):
    dyg = dy * g
    return r * (dyg - xh * jnp.mean(dyg * xh, axis=-1, keepdims=True))


def _ln_stats(x):
    mu = jnp.mean(x, axis=-1, keepdims=True)
    xc = x - mu
    r = lax.rsqrt(jnp.mean(xc * xc, axis=-1, keepdims=True) + EPS)
    return xc * r, r


def _ln_bwd(dy, g, xh, r):
    dxh = dy * g
    return r * (dxh - jnp.mean(dxh, axis=-1, keepdims=True) - xh * jnp.mean(dxh * xh, axis=-1, keepdims=True))


def _rope_fwd(t, c, s1, s2):
    return t * c + pltpu.roll(t, HP - 16, 1) * s1 + pltpu.roll(t, 16, 1) * s2


def _rope_bwd(d, c, s1, s2):
    return d * c + pltpu.roll(d * s1, 16, 1) + pltpu.roll(d * s2, HP - 16, 1)


def _lane_lt64(shape):
    return lax.broadcasted_iota(jnp.int32, shape, len(shape) - 1) < 64


def _spatial_mix(w_ref, src, dst_ref, rows):
    for c in range(rows // CHUNK):
        for p in range(PAIRS):
            blk = src[c * CHUNK:(c + 1) * CHUNK, p * HP:(p + 1) * HP]
            a = _dot(w_ref[2 * p], blk)
            b = _dot(w_ref[2 * p + 1], blk)
            dst_ref[c * CHUNK:(c + 1) * CHUNK, p * HP:(p + 1) * HP] = jnp.where(_lane_lt64(a.shape), a, b)


def _gmlp_fwd(u_pre, v_pre, zb, sg, sb, wt_ref, bsp_ref, sv_ref, rows):
    u, du = _gelu_and_grad(u_pre)
    gv, dgv = _gelu_and_grad(v_pre)
    xh, r = _ln_stats(gv)
    vln = (xh * sg + sb).astype(BF16)
    _spatial_mix(wt_ref, vln, sv_ref, rows)
    bias = bsp_ref[...]
    svb = sv_ref[...] + jnp.concatenate([bias] * (rows // CHUNK), axis=0)
    sig = _sigmoid(zb)
    return u, du, dgv, xh, r, vln, svb, sig


def _weight_gather(shards):
    n_arr = len(shards)

    def body(*refs):
        ins, outs = refs[0:n_arr], refs[n_arr:2 * n_arr]
        send_sems, recv_sems = refs[2 * n_arr:]
        x, y, c = lax.axis_index("x"), lax.axis_index("y"), lax.axis_index("c")
        j = 2 * x + y
        sib = (x, y, 1 - c)
        chips = [(1 - x, y), (x, 1 - y), (1 - x, 1 - y)]
        for n in range(n_arr):
            outs[n][j] = ins[n][...].astype(BF16)

        def half(n, blk, core):
            r = shards[n].shape[0] // 2
            return outs[n].at[blk, pl.ds(pl.multiple_of(core * r, 16), r), :]

        def copy(k, ref, to):
            return pltpu.make_async_remote_copy(
                src_ref=ref, dst_ref=ref, send_sem=send_sems.at[k], recv_sem=recv_sems.at[k],
                device_id=to, device_id_type=MESH)

        first = [copy(6 * n + kk, half(n, j, c), (px, py, c))
                 for n in range(n_arr) for kk, (px, py) in enumerate(chips)]
        for cp in first:
            cp.start()
        passed = []
        for n in range(n_arr):
            for kk, (px, py) in enumerate(chips):
                landed = half(n, 2 * px + py, c)
                copy(6 * n + kk, landed, (px, py, c)).wait_recv()
                passed.append(copy(6 * n + 3 + kk, landed, sib))
                passed[-1].start()
        for n in range(n_arr):
            for kk, (px, py) in enumerate(chips):
                copy(6 * n + 3 + kk, half(n, 2 * px + py, 1 - c), sib).wait_recv()
        for cp in first + passed:
            cp.wait_send()

    vmem = pl.BlockSpec(memory_space=pltpu.VMEM)
    return pl.pallas_call(
        body, name="weight_gather",
        out_shape=[jax.ShapeDtypeStruct((4,) + a.shape, BF16) for a in shards],
        in_specs=[vmem] * n_arr, out_specs=[vmem] * n_arr,
        scratch_shapes=[pltpu.SemaphoreType.DMA((6 * n_arr,)), pltpu.SemaphoreType.DMA((6 * n_arr,))],
        compiler_params=pltpu.CompilerParams(vmem_limit_bytes=VMEM_LIMIT),
    )(*shards)


def _fwd_pre(x, pos, win, wuq, wkv, pvec):
    seq = x.shape[0]
    t = FWD_TILE

    def body(x_ref, pos_ref, win_ref, wuq_ref, wkv_ref, pv_ref,
             cq_o, ckv_o, gate_o, q_o, k_o, v_o, vt_o, cs_o):
        xb = x_ref[...].astype(BF16)
        proj = _dot(xb, win_ref[:, 0:C_GATE])
        cq = proj[:, 0:C_CKV]
        ckv = proj[:, C_CKV:C_KR]
        kr = proj[:, C_KR:C_GATE]
        cq_o[...] = cq
        ckv_o[...] = ckv

        ang = pos_ref[...].astype(F32) * pv_ref[PV_INVF:PV_INVF + 1, 0:HP]
        cos = jnp.cos(ang)
        sin = jnp.sin(ang)
        cs_o[:, 0:HP] = cos
        cs_o[:, HP:2 * HP] = sin
        s1 = sin * pv_ref[PV_M1:PV_M1 + 1, 0:HP]
        s2 = sin * pv_ref[PV_M2:PV_M2 + 1, 0:HP]

        cqh, _ = _rms_stats(cq)
        q_all = _dot((cqh * pv_ref[PV_QG:PV_QG + 1, 0:Q_LORA]).astype(BF16), wuq_ref[...])
        ckvh, _ = _rms_stats(ckv)
        kv_all = _dot((ckvh * pv_ref[PV_KVG:PV_KVG + 1, 0:KV_LORA]).astype(BF16), wkv_ref[...])
        krr = _rope_fwd(kr, cos, s1, s2)
        for h in range(HEADS):
            sl = slice(h * HP, (h + 1) * HP)
            q_o[:, sl] = (_rope_fwd(q_all[:, sl], cos, s1, s2) * SCALE_LOG2E).astype(BF16)
            k_o[:, sl] = (kv_all[:, sl] + krr).astype(BF16)
        val = kv_all[:, HEADS * HP:].astype(BF16)
        v_o[...] = val
        for blk in range(t // ATT_BLK):
            vt_o[blk] = val[blk * ATT_BLK:(blk + 1) * ATT_BLK, :].T
        gate_o[...] = _dot(xb, win_ref[:, C_GATE:D_INR]).astype(BF16)

    tile = lambda w: pl.BlockSpec((t, w), lambda i: (i, 0))
    full = lambda a: pl.BlockSpec(a.shape, lambda i: (0,) * a.ndim)
    outs = [(Q_LORA, F32), (KV_LORA, F32), (2048, BF16), (HEADS * HP, BF16), (HEADS * HP, BF16), (MLA_W, BF16)]
    assert t % ATT_BLK == 0
    out_specs = [tile(w) for w, _ in outs]
    out_specs += [pl.BlockSpec((t // ATT_BLK, MLA_W, ATT_BLK), lambda i: (i, 0, 0)), tile(2 * HP)]
    out_shape = [jax.ShapeDtypeStruct((seq, w), d) for w, d in outs]
    out_shape += [jax.ShapeDtypeStruct((seq // ATT_BLK, MLA_W, ATT_BLK), BF16), jax.ShapeDtypeStruct((seq, 2 * HP), F32)]
    return pl.pallas_call(
        body, name="fwd_pre", grid=(seq // t,),
        in_specs=[tile(D_MODEL), tile(1), full(win), full(wuq), full(wkv), full(pvec)],
        out_specs=out_specs, out_shape=out_shape,
        compiler_params=pltpu.CompilerParams(dimension_semantics=("arbitrary",), vmem_limit_bytes=VMEM_LIMIT),
    )(x, pos, win, wuq, wkv, pvec)


def _attn_fwd(q, k, vt):
    seq = q.shape[0]
    b = ATT_BLK
    nq = seq // b
    assert nq % 2 == 0
    n_wide = sum(i // 2 for i in range(nq))

    def body(q_ref, k_ref, vt_ref, o_o, lse_o, m_ref, l_ref, acc_ref, s_even, s_odd):
        m_ref[...] = jnp.full(m_ref.shape, -jnp.inf, F32)
        l_ref[...] = jnp.zeros(l_ref.shape, F32)
        acc_ref[...] = jnp.zeros(acc_ref.shape, F32)

        def scores(i, j, s_ref, nkb):
            qrows = pl.ds(pl.multiple_of(i * b, b), b)
            krows = pl.ds(pl.multiple_of(j * b, b), nkb * b)
            for a in range(2):
                s_ref[a, 0:nkb * b, :] = _dot_nt(k_ref[krows, a * HP:(a + 1) * HP], q_ref[qrows, a * HP:(a + 1) * HP])

        def consume(i, j, s_ref, nkb, masked):
            for a in range(2):
                st = s_ref[a, 0:nkb * b, :]
                if masked:
                    ki = lax.broadcasted_iota(jnp.int32, st.shape, 0)
                    qi = lax.broadcasted_iota(jnp.int32, st.shape, 1)
                    st = jnp.where(ki <= qi, st, -jnp.inf)
                m_prev = m_ref[i, a:a + 1, :]
                m_new = jnp.maximum(m_prev, jnp.max(st, axis=0, keepdims=True))
                alpha = jnp.exp2(m_prev - m_new)
                pt = jnp.exp2(st - m_new)
                ptb = pt.astype(BF16)
                l_ref[i, a:a + 1, :] = alpha * l_ref[i, a:a + 1, :] + jnp.sum(pt, axis=0, keepdims=True)
                pv = _dot(vt_ref[j], ptb[0:b, :])
                for kb in range(1, nkb):
                    pv = pv + _dot(vt_ref[j + kb], ptb[kb * b:(kb + 1) * b, :])
                acc_ref[i, a] = alpha * acc_ref[i, a] + pv
                m_ref[i, a:a + 1, :] = m_new

        def run(count, first, following, nkb, masked):
            if count == 0:
                return
            scores(*first, s_even, nkb)

            def two(u, ij):
                nxt = following(*ij)
                scores(*nxt, s_odd, nkb)
                consume(*ij, s_even, nkb, masked)
                nxt2 = following(*nxt)
                scores(*nxt2, s_even, nkb)
                consume(*nxt, s_odd, nkb, masked)
                return nxt2

            last = lax.fori_loop(0, count // 2, two, tuple(jnp.int32(c) for c in first))
            if count % 2:
                consume(*last, s_even, nkb, masked)

        def clamp(i):
            return jnp.minimum(i, nq - 1)

        def next_wide(i, j):
            wrap = j + 4 > i
            return clamp(jnp.where(wrap, i + 1, i)), jnp.where(wrap, 0, j + 2)

        run(n_wide, (2, 0), next_wide, 2, False)
        run(nq // 2, (1, 0), lambda i, j: (clamp(i + 2), jnp.minimum(j + 2, nq - 2)), 1, False)
        run(nq, (0, 0), lambda i, j: (clamp(i + 1), clamp(j + 1)), 1, True)
        top = lax.broadcasted_iota(jnp.int32, (HP, b), 0) < 64

        def finish(i, carry):
            rows = pl.ds(pl.multiple_of(i * b, b), b)
            o_o[rows, :] = jnp.where(top, acc_ref[i, 0] / l_ref[i, 0:1, :], acc_ref[i, 1] / l_ref[i, 1:2, :]).T
            lse_o[i] = m_ref[i, 0:2, :] + jnp.log2(l_ref[i, 0:2, :])
            return carry

        lax.fori_loop(0, nq, finish, 0)

    return pl.pallas_call(
        body, name="attn_fwd", grid=(PAIRS,),
        in_specs=[pl.BlockSpec((seq, 2 * HP), lambda p: (0, p)),
                  pl.BlockSpec((seq, 2 * HP), lambda p: (0, p)),
                  pl.BlockSpec((nq, HP, b), lambda p: (0, p, 0))],
        out_specs=[pl.BlockSpec((seq, HP), lambda p: (0, p)),
                   pl.BlockSpec((None, nq, 2, b), lambda p: (p, 0, 0, 0))],
        out_shape=[jax.ShapeDtypeStruct((seq, MLA_W), F32),
                   jax.ShapeDtypeStruct((PAIRS, nq, 2, b), F32)],
        scratch_shapes=[pltpu.VMEM((nq, 8, b), F32), pltpu.VMEM((nq, 8, b), F32), pltpu.VMEM((nq, 2, HP, b), F32),
                        pltpu.VMEM((2, 2 * b, b), F32), pltpu.VMEM((2, 2 * b, b), F32)],
        compiler_params=pltpu.CompilerParams(dimension_semantics=("arbitrary",), vmem_limit_bytes=VMEM_LIMIT),
    )(q, k, vt)


def _post(x, tgt, o, gate, wout, pvec, wt, wtt, bsp):
    seq = x.shape[0]
    t = POST_TILE
    nt = seq // t

    def body(x_ref, tgt_ref, o_ref, gate_ref, wout_ref, pv_ref, wt_ref, wtt_ref, bsp_ref,
             dh2_o, do_o, dgate_o, gwout_o, gwsp_o, vec_o, sv_ref, dvln_ref, bacc_ref):
        i = pl.program_id(0)

        @pl.when(i == 0)
        def _():
            gwout_o[...] = jnp.zeros_like(gwout_o)
            gwsp_o[...] = jnp.zeros_like(gwsp_o)
            vec_o[...] = jnp.zeros_like(vec_o)
            bacc_ref[...] = jnp.zeros_like(bacc_ref)

        za = gate_ref[:, 0:512].astype(F32)
        u_pre = gate_ref[:, 512:1024].astype(F32)
        v_pre = gate_ref[:, 1024:1536].astype(F32)
        zb = gate_ref[:, 1536:2048].astype(F32)
        sg = pv_ref[PV_SG:PV_SG + 1, 0:GW]
        sb = pv_ref[PV_SB:PV_SB + 1, 0:GW]
        lng = pv_ref[PV_LNG:PV_LNG + 1, :]
        lnb = pv_ref[PV_LNB:PV_LNB + 1, :]
        o = o_ref[...]

        sig_a = _sigmoid(za)
        silu_a = za * sig_a
        u, du, dgv, xh, r, vln, svb, sig_b = _gmlp_fwd(u_pre, v_pre, zb, sg, sb, wt_ref, bsp_ref, sv_ref, t)
        silu_b = zb * sig_b
        sgu = u * svb
        merged = jnp.concatenate([o * silu_a, sgu * silu_b], axis=1).astype(BF16)
        h2 = DN_ALPHA * x_ref[...] + _dot(merged, wout_ref[...])
        xh2, r2 = _ln_stats(h2)
        err = xh2 * lng + lnb - tgt_ref[...]
        d_out = err * (1.0 / D_MODEL)
        vec_o[GV_LNG:GV_LNG + 1, :] += jnp.sum(d_out * xh2, axis=0, keepdims=True)
        vec_o[GV_LNB:GV_LNB + 1, :] += jnp.sum(d_out, axis=0, keepdims=True)
        vec_o[GV_LOSS:GV_LOSS + 1, :] += jnp.sum(err * err, axis=0, keepdims=True) * (0.5 / D_MODEL)

        d_h2 = _ln_bwd(d_out, lng, xh2, r2)
        dh2_o[...] = d_h2
        dh2b = d_h2.astype(BF16)
        gwout_o[...] += _dot_tn(merged, dh2b)
        d_m = _dot_nt(dh2b, wout_ref[...])
        d_oa = d_m[:, 0:512]
        d_ob = d_m[:, 512:1024]
        do_o[...] = (d_oa * silu_a).astype(BF16)
        dgate_o[:, 0:512] = (d_oa * o * (sig_a * (1.0 + za * (1.0 - sig_a)))).astype(BF16)
        dgate_o[:, 1536:2048] = (d_ob * sgu * (sig_b * (1.0 + zb * (1.0 - sig_b)))).astype(BF16)
        d_sgu = d_ob * silu_b
        dgate_o[:, 512:1024] = (d_sgu * svb * du).astype(BF16)
        d_sv = d_sgu * u
        acc = bacc_ref[...]
        for c in range(t // CHUNK):
            acc = acc + d_sv[c * CHUNK:(c + 1) * CHUNK, :]
        bacc_ref[...] = acc
        d_svb = d_sv.astype(BF16)
        for c in range(t // CHUNK):
            for p in range(PAIRS):
                blk = d_svb[c * CHUNK:(c + 1) * CHUNK, p * HP:(p + 1) * HP]
                vblk = vln[c * CHUNK:(c + 1) * CHUNK, p * HP:(p + 1) * HP]
                first = _lane_lt64(blk.shape)
                gwsp_o[2 * p] += _dot_nt(jnp.where(first, blk, jnp.zeros_like(blk)), vblk)
                gwsp_o[2 * p + 1] += _dot_nt(jnp.where(first, jnp.zeros_like(blk), blk), vblk)
        _spatial_mix(wtt_ref, d_svb, dvln_ref, t)
        d_vln = dvln_ref[...]
        vec_o[GV_SG:GV_SG + 1, 0:GW] += jnp.sum(d_vln * xh, axis=0, keepdims=True)
        vec_o[GV_SB:GV_SB + 1, 0:GW] += jnp.sum(d_vln, axis=0, keepdims=True)
        dgate_o[:, 1024:1536] = (_ln_bwd(d_vln, sg, xh, r) * dgv).astype(BF16)


        @pl.when(i == nt - 1)
        def _():
            tri = (lax.broadcasted_iota(jnp.int32, (CHUNK, CHUNK), 1)
                   <= lax.broadcasted_iota(jnp.int32, (CHUNK, CHUNK), 0))
            for h in range(HEADS):
                gwsp_o[h] = jnp.where(tri, gwsp_o[h], 0.0)
            lane = lax.broadcasted_iota(jnp.int32, (CHUNK, HP), 1)
            res = jnp.zeros((CHUNK, HP), F32)
            for h in range(HEADS):
                p, a = divmod(h, 2)
                blk = bacc_ref[:, p * HP:(p + 1) * HP]
                part = jnp.where(_lane_lt64(blk.shape) == (a == 0), blk, 0.0)
                res = jnp.where(lane == h, jnp.sum(part, axis=-1, keepdims=True), res)
            vec_o[GV_BSP:GV_BSP + HEADS, 0:HP] = res.T[0:HEADS, :]
            lane1 = lax.broadcasted_iota(jnp.int32, (1, D_MODEL), 1)
            total = jnp.sum(vec_o[GV_LOSS:GV_LOSS + 1, :], axis=-1, keepdims=True)
            vec_o[GV_LOSS:GV_LOSS + 1, :] = jnp.where(lane1 == 0, total, 0.0)

    tile = lambda w: pl.BlockSpec((t, w), lambda i: (i, 0))
    full = lambda a: pl.BlockSpec(a.shape, lambda i: (0,) * a.ndim)
    const = lambda s: pl.BlockSpec(s, lambda i: (0,) * len(s))
    return pl.pallas_call(
        body, name="post", grid=(nt,),
        in_specs=[tile(D_MODEL), tile(D_MODEL), tile(MLA_W), tile(2048), full(wout), full(pvec),
                  full(wt), full(wtt), full(bsp)],
        out_specs=[tile(D_MODEL), tile(MLA_W), tile(2048), const((D_MODEL, D_MODEL)),
                   const((HEADS, CHUNK, CHUNK)), const((GV_ROWS, D_MODEL))],
        out_shape=[jax.ShapeDtypeStruct((seq, D_MODEL), F32), jax.ShapeDtypeStruct((seq, MLA_W), BF16),
                   jax.ShapeDtypeStruct((seq, 2048), BF16), jax.ShapeDtypeStruct((D_MODEL, D_MODEL), F32),
                   jax.ShapeDtypeStruct((HEADS, CHUNK, CHUNK), F32), jax.ShapeDtypeStruct((GV_ROWS, D_MODEL), F32)],
        scratch_shapes=[pltpu.VMEM((t, GW), F32), pltpu.VMEM((t, GW), F32), pltpu.VMEM((CHUNK, GW), F32)],
        compiler_params=pltpu.CompilerParams(dimension_semantics=("arbitrary",), vmem_limit_bytes=VMEM_LIMIT),
    )(x, tgt, o, gate, wout, pvec, wt, wtt, bsp)


def _attn_bwd(q, k, v, do, o, lse, cs, pvec):
    seq = q.shape[0]
    b = ATT_BLK
    nq = seq // b

    def body(q_ref, k_ref, v_ref, do_ref, o_ref, lse_ref, cs_ref, pv_ref, dq_o, dk_o, dv_o, dk_acc, dv_acc):
        i = pl.program_id(1)

        @pl.when(i == 0)
        def _():
            dk_acc[...] = jnp.zeros_like(dk_acc)
            dv_acc[...] = jnp.zeros_like(dv_acc)

        first = _lane_lt64((b, HP))
        do = do_ref[...]
        zero = jnp.zeros_like(do)
        dos = [jnp.where(first, do, zero), jnp.where(first, zero, do)]
        prod_t = (do.astype(F32) * o_ref[...]).T
        deltas = [jnp.sum(prod_t[0:64, :], axis=0, keepdims=True),
                  jnp.sum(prod_t[64:128, :], axis=0, keepdims=True)]
        lses = [lse_ref[0:1, :], lse_ref[1:2, :]]
        qs = [q_ref[:, a * HP:(a + 1) * HP] for a in range(2)]

        def step(j, dqs, masked, nk=b):
            rows = pl.ds(pl.multiple_of(j * b, b), nk)
            vb = v_ref[rows, :]
            new_dq = []
            dvs = []
            for a in range(2):
                kb = k_ref[rows, a * HP:(a + 1) * HP]
                pt = jnp.exp2(_dot_nt(kb, qs[a]) - lses[a])
                if masked:
                    ki = lax.broadcasted_iota(jnp.int32, pt.shape, 0)
                    qi = lax.broadcasted_iota(jnp.int32, pt.shape, 1) + (nk - b)
                    pt = jnp.where(ki <= qi, pt, 0.0)
                dvs.append(_dot(pt.astype(BF16), do))
                dpt = _dot_nt(vb, dos[a])
                dst = (pt * (dpt - deltas[a])).astype(BF16)
                dk_acc[rows, a * HP:(a + 1) * HP] += _dot(dst, qs[a])
                new_dq.append(dqs[a] + _dot_tn(dst, kb))
            dv_acc[rows, :] += jnp.where(_lane_lt64((nk, HP)), dvs[0], dvs[1])
            return tuple(new_dq)

        init = (jnp.zeros((b, HP), F32), jnp.zeros((b, HP), F32))
        dqs = lax.fori_loop(0, i // 4, lambda jj, cr: step(4 * jj, cr, False, 4 * b), init)
        last = [lambda cr, w=w: step(4 * (i // 4), cr, True, w * b) for w in (1, 2, 3, 4)]
        dqs = lax.switch(i % 4, last, dqs)
        cos = cs_ref[:, 0:HP]
        sin = cs_ref[:, HP:2 * HP]
        s1 = sin * pv_ref[PV_M1:PV_M1 + 1, 0:HP]
        s2 = sin * pv_ref[PV_M2:PV_M2 + 1, 0:HP]
        for a in range(2):
            dq_o[:, a * HP:(a + 1) * HP] = _rope_bwd(dqs[a] * SCALE, cos, s1, s2).astype(BF16)

        @pl.when(i == nq - 1)
        def _():
            dk_o[...] = (dk_acc[...] * (SCALE / SCALE_LOG2E)).astype(BF16)
            dv_o[...] = dv_acc[...].astype(BF16)

    return pl.pallas_call(
        body, name="attn_bwd", grid=(PAIRS, nq),
        in_specs=[pl.BlockSpec((b, 2 * HP), lambda p, i: (i, p)),
                  pl.BlockSpec((seq, 2 * HP), lambda p, i: (0, p)),
                  pl.BlockSpec((seq, HP), lambda p, i: (0, p)),
                  pl.BlockSpec((b, HP), lambda p, i: (i, p)),
                  pl.BlockSpec((b, HP), lambda p, i: (i, p)),
                  pl.BlockSpec((None, None, 2, b), lambda p, i: (p, i, 0, 0)),
                  pl.BlockSpec((b, 2 * HP), lambda p, i: (i, 0)),
                  pl.BlockSpec(pvec.shape, lambda p, i: (0, 0))],
        out_specs=[pl.BlockSpec((b, 2 * HP), lambda p, i: (i, p)),
                   pl.BlockSpec((seq, 2 * HP), lambda p, i: (0, p)),
                   pl.BlockSpec((seq, HP), lambda p, i: (0, p))],
        out_shape=[jax.ShapeDtypeStruct((seq, HEADS * HP), BF16),
                   jax.ShapeDtypeStruct((seq, HEADS * HP), BF16),
                   jax.ShapeDtypeStruct((seq, MLA_W), BF16)],
        scratch_shapes=[pltpu.VMEM((seq, 2 * HP), F32), pltpu.VMEM((seq, HP), F32)],
        compiler_params=pltpu.CompilerParams(dimension_semantics=("arbitrary", "arbitrary"),
                                             vmem_limit_bytes=VMEM_LIMIT),
    )(q, k, v, do, o, lse, cs, pvec)


def _bwd_pre(x, dh2, cq, ckv, cs, dq, dk, dv, dgate, win, wuq, wkv, pvec, gvec):
    seq = x.shape[0]
    t = BWD_TILE

    def body(x_ref, dh2_ref, cq_ref, ckv_ref, cs_ref, dq_ref, dk_ref, dv_ref, dgate_ref,
             win_ref, wuq_ref, wkv_ref, pv_ref, gv_ref, gx_o, gwin_o, gwuq_o, gwkv_o, vec_o):
        i = pl.program_id(0)

        @pl.when(i == 0)
        def _():
            gwin_o[...] = jnp.zeros_like(gwin_o)
            gwuq_o[...] = jnp.zeros_like(gwuq_o)
            gwkv_o[...] = jnp.zeros_like(gwkv_o)
            vec_o[...] = gv_ref[...]

        xb = x_ref[...].astype(BF16)
        dgate = dgate_ref[...]
        gwin_o[:, C_GATE:D_INR] += _dot_tn(xb, dgate)
        gx_gate = _dot_nt(dgate, win_ref[:, C_GATE:D_INR])

        qg = pv_ref[PV_QG:PV_QG + 1, 0:Q_LORA]
        kvg = pv_ref[PV_KVG:PV_KVG + 1, 0:KV_LORA]
        dq = dq_ref[...]
        cqh, rq = _rms_stats(cq_ref[...])
        d_cqn = _dot_nt(dq, wuq_ref[...])
        gwuq_o[...] += _dot_tn((cqh * qg).astype(BF16), dq)
        vec_o[GV_QG:GV_QG + 1, 0:Q_LORA] += jnp.sum(d_cqn * cqh, axis=0, keepdims=True)
        d_cq = _rms_bwd(d_cqn, qg, cqh, rq)

        dk = dk_ref[...]
        dkv = jnp.concatenate([dk, dv_ref[...]], axis=1)
        ckvh, rkv = _rms_stats(ckv_ref[...])
        d_ckvn = _dot_nt(dkv, wkv_ref[...])
        gwkv_o[...] += _dot_tn((ckvh * kvg).astype(BF16), dkv)
        vec_o[GV_KVG:GV_KVG + 1, 0:KV_LORA] += jnp.sum(d_ckvn * ckvh, axis=0, keepdims=True)
        d_ckv = _rms_bwd(d_ckvn, kvg, ckvh, rkv)

        dks = dk[:, 0:HP].astype(F32)
        for h in range(1, HEADS):
            dks = dks + dk[:, h * HP:(h + 1) * HP].astype(F32)
        cos = cs_ref[:, 0:HP]
        sin = cs_ref[:, HP:2 * HP]
        d_kr = _rope_bwd(dks, cos, sin * pv_ref[PV_M1:PV_M1 + 1, 0:HP], sin * pv_ref[PV_M2:PV_M2 + 1, 0:HP])

        d_lat = jnp.concatenate([d_cq.astype(BF16), d_ckv.astype(BF16), d_kr.astype(BF16)], axis=1)
        gwin_o[:, 0:C_GATE] += _dot_tn(xb, d_lat)
        gx_o[...] = DN_ALPHA * dh2_ref[...] + gx_gate + _dot_nt(d_lat, win_ref[:, 0:C_GATE])

    tile = lambda w: pl.BlockSpec((t, w), lambda i: (i, 0))
    full = lambda a: pl.BlockSpec(a.shape, lambda i: (0,) * a.ndim)
    const = lambda s: pl.BlockSpec(s, lambda i: (0,) * len(s))
    return pl.pallas_call(
        body, name="bwd_pre", grid=(seq // t,),
        in_specs=[tile(D_MODEL), tile(D_MODEL), tile(Q_LORA), tile(KV_LORA), tile(2 * HP), tile(HEADS * HP),
                  tile(HEADS * HP), tile(MLA_W), tile(2048), full(win), full(wuq), full(wkv), full(pvec), full(gvec)],
        out_specs=[tile(D_MODEL), const((D_MODEL, D_INR)), const((Q_LORA, HEADS * HP)),
                   const((KV_LORA, HEADS * HP + MLA_W)), const((GV_ROWS, D_MODEL))],
        out_shape=[jax.ShapeDtypeStruct((seq, D_MODEL), F32), jax.ShapeDtypeStruct((D_MODEL, D_INR), F32),
                   jax.ShapeDtypeStruct((Q_LORA, HEADS * HP), F32),
                   jax.ShapeDtypeStruct((KV_LORA, HEADS * HP + MLA_W), F32),
                   jax.ShapeDtypeStruct((GV_ROWS, D_MODEL), F32)],
        compiler_params=pltpu.CompilerParams(dimension_semantics=("arbitrary",), vmem_limit_bytes=VMEM_LIMIT),
    )(x, dh2, cq, ckv, cs, dq, dk, dv, dgate, win, wuq, wkv, pvec, gvec)


def _grad_reduce(gs, gvec):
    n_arr = len(gs)
    n_big = n_arr - 1
    k1 = lambda n, blk: 4 * n + blk
    k2 = lambda n, kk: 4 * n_arr + 3 * n + kk
    k3 = lambda n: 7 * n_arr + n
    k3w = lambda k: 7 * n_arr + n_big + k
    kv = lambda k: 7 * n_arr + n_big + 7 + k
    n_sem = 7 * n_arr + n_big + 14

    def body(*refs):
        g, gv = refs[0:n_arr], refs[n_arr]
        outs, ov = refs[n_arr + 1:2 * n_arr + 1], refs[2 * n_arr + 1]
        r1 = refs[2 * n_arr + 2:3 * n_arr + 2]
        r2 = refs[3 * n_arr + 2:4 * n_arr + 2]
        s2 = refs[4 * n_arr + 2:5 * n_arr + 2]
        vbuf, send_sems, recv_sems = refs[5 * n_arr + 2:]
        x, y, c = lax.axis_index("x"), lax.axis_index("y"), lax.axis_index("c")
        j = 2 * x + y
        me = 2 * j + c
        sib = (x, y, 1 - c)
        chips = [(1 - x, y), (x, 1 - y), (1 - x, 1 - y)]
        others = [sib] + [(px, py, pc) for (px, py) in chips for pc in (c, 1 - c)]

        def copy(k, src, dst, to):
            return pltpu.make_async_remote_copy(
                src_ref=src, dst_ref=dst, send_sem=send_sems.at[k], recv_sem=recv_sems.at[k],
                device_id=to, device_id_type=MESH)

        l1 = [copy(k1(n, blk), g[n].at[blk, 1 - c], r1[n].at[blk], sib) for n in range(n_arr) for blk in range(4)]
        lv = [copy(kv(k), gv, vbuf.at[me], to) for k, to in enumerate(others)]
        for cp in l1 + lv:
            cp.start()
        l2 = []
        for n in range(n_arr):
            for blk in range(4):
                copy(k1(n, blk), g[n].at[blk, c], r1[n].at[blk], sib).wait_recv()
            for blk in range(4):
                r1[n][blk] = g[n][blk, c] + r1[n][blk]
                s2[n][blk] = r1[n][blk].astype(BF16)
            for kk, (px, py) in enumerate(chips):
                l2.append(copy(k2(n, kk), s2[n].at[2 * px + py], r2[n].at[kk], (px, py, c)))
                l2[-1].start()

        l3 = []
        for n in range(n_arr):
            for kk in range(3):
                copy(k2(n, kk), s2[n].at[0], r2[n].at[kk], sib).wait_recv()
            red = ((r1[n][j] + r2[n][0].astype(F32)) + r2[n][1].astype(F32)) + r2[n][2].astype(F32)
            if n < n_big:
                outs[n][c] = red
                back = [copy(k3(n), outs[n].at[c], outs[n].at[c], sib)]
            else:
                outs[n][j, c] = red
                back = [copy(k3w(k), outs[n].at[j, c], outs[n].at[j, c], to) for k, to in enumerate(others)]
            for cp in back:
                cp.start()
            l3 += back
        for n in range(n_big):
            copy(k3(n), outs[n].at[1 - c], outs[n].at[1 - c], sib).wait_recv()
        for k, (px, py, pc) in enumerate(others):
            landed = outs[n_big].at[2 * px + py, pc]
            copy(k3w(k), landed, landed, (px, py, pc)).wait_recv()
            copy(kv(k), gv, vbuf.at[4 * px + 2 * py + pc], (px, py, pc)).wait_recv()
        vbuf[me] = gv[...]
        total = vbuf[0]
        for d in range(1, 8):
            total = total + vbuf[d]
        ov[...] = total
        for cp in l1 + lv + l2 + l3:
            cp.wait_send()

    vmem = pl.BlockSpec(memory_space=pltpu.VMEM)
    half_shapes = [a.shape[2:] for a in gs]
    out_shape = [jax.ShapeDtypeStruct((2,) + s, F32) for s in half_shapes[:n_big]]
    out_shape += [jax.ShapeDtypeStruct((4, 2) + half_shapes[n_big], F32), jax.ShapeDtypeStruct(gvec.shape, F32)]
    scratch = [pltpu.VMEM((4,) + s, F32) for s in half_shapes] + [pltpu.VMEM((3,) + s, BF16) for s in half_shapes]
    scratch += [pltpu.VMEM((4,) + s, BF16) for s in half_shapes]
    scratch += [pltpu.VMEM((8,) + gvec.shape, F32), pltpu.SemaphoreType.DMA((n_sem,)), pltpu.SemaphoreType.DMA((n_sem,))]
    return pl.pallas_call(
        body, name="grad_reduce", out_shape=out_shape,
        in_specs=[vmem] * (n_arr + 1), out_specs=[vmem] * (n_arr + 1), scratch_shapes=scratch,
        compiler_params=pltpu.CompilerParams(vmem_limit_bytes=VMEM_LIMIT),
    )(*gs, gvec)


SMALL_ROWS = ((GV_QG, 1, Q_LORA), (GV_KVG, 1, KV_LORA), (GV_SG, 1, GW), (GV_SB, 1, GW),
              (GV_LNG, 1, D_MODEL), (GV_LNB, 1, D_MODEL), (GV_BSP, HEADS, CHUNK))


def _adam_update(g, w, m, v):
    m_new = ADAM_B1 * m + (1.0 - ADAM_B1) * g
    v_new = ADAM_B2 * v + (1.0 - ADAM_B2) * (g * g)
    m_hat = m_new / (1.0 - ADAM_B1 ** ADAM_STEP)
    v_hat = v_new / (1.0 - ADAM_B2 ** ADAM_STEP)
    return -ADAM_LR * (m_hat / (jnp.sqrt(v_hat) + ADAM_EPS) + ADAM_WD * w), m_new, v_new


def _adamw(g_big, w_big, m_big, v_big, gvec, w_small, m_small, v_small):
    nb, ns = len(g_big), len(w_small)

    def body(*refs):
        it = iter(refs)
        take = lambda n: [next(it) for _ in range(n)]
        g_b, w_b, m_b, v_b = take(nb), take(nb), take(nb), take(nb)
        gv = next(it)
        w_s, m_s, v_s = take(ns), take(ns), take(ns)
        g_bo, d_bo, m_bo, v_bo = take(nb), take(nb), take(nb), take(nb)
        g_so, d_so, m_so, v_so = take(ns), take(ns), take(ns), take(ns)
        for n in range(nb):
            gb = g_b[n][...]
            g_bo[n][...] = gb
            d_bo[n][...], m_bo[n][...], v_bo[n][...] = _adam_update(gb, w_b[n][...], m_b[n][...], v_b[n][...])
        for n, (row, nrow, width) in enumerate(SMALL_ROWS):
            gs = gv[row:row + nrow, 0:width]
            g_so[n][...] = gs
            d_so[n][...], m_so[n][...], v_so[n][...] = _adam_update(gs, w_s[n][...], m_s[n][...], v_s[n][...])

    def rows(a):
        nd = a.ndim
        return pl.BlockSpec((a.shape[0] // ADAM_STEPS,) + a.shape[1:], lambda i: (i,) + (0,) * (nd - 1))

    def whole(a):
        nd = a.ndim
        return pl.BlockSpec(a.shape, lambda i: (0,) * nd)

    big = [jax.ShapeDtypeStruct(a.shape, F32) for a in w_big]
    small = [jax.ShapeDtypeStruct(a.shape, F32) for a in w_small]
    return pl.pallas_call(
        body, name="adamw", grid=(ADAM_STEPS,), out_shape=big * 4 + small * 4,
        in_specs=[rows(a) for a in g_big + w_big + m_big + v_big] + [whole(gvec)]
        + [whole(a) for a in w_small + m_small + v_small],
        out_specs=[rows(a) for a in w_big] * 4 + [whole(a) for a in w_small] * 4,
        compiler_params=pltpu.CompilerParams(dimension_semantics=("arbitrary",), vmem_limit_bytes=VMEM_LIMIT),
    )(*g_big, *w_big, *m_big, *v_big, gvec, *w_small, *m_small, *v_small)


def kernel(x, positions, w_in, q_norm_g, w_uq, kv_norm_g, w_ukv, sgu_norm_g, sgu_norm_b, w_spatial, b_spatial, w_out, ln_g, ln_b, loss_target, m_w_in, m_q_norm_g, m_w_uq, m_kv_norm_g, m_w_ukv, m_sgu_norm_g, m_sgu_norm_b, m_w_spatial, m_b_spatial, m_w_out, m_ln_g, m_ln_b, v_w_in, v_q_norm_g, v_w_uq, v_kv_norm_g, v_w_ukv, v_sgu_norm_g, v_sgu_norm_b, v_w_spatial, v_b_spatial, v_w_out, v_ln_g, v_ln_b):
    seq = x.shape[1]
    x2 = x.reshape(seq, D_MODEL)
    tgt = loss_target.reshape(seq, D_MODEL)
    pos = positions.reshape(seq, 1)

    a_in, a_uq, a_ukv, a_out = _weight_gather([w_in, w_uq, w_ukv, w_out])
    w_uq_f = jnp.swapaxes(a_uq, 0, 1).reshape(Q_LORA, HEADS * (NOPE + ROPE))
    w_ukv_f = jnp.swapaxes(a_ukv, 0, 1).reshape(KV_LORA, HEADS * (NOPE + VDIM))
    wout = a_out.reshape(D_MODEL, D_MODEL)
    zc = lambda n: jnp.zeros((D_MODEL, n), BF16)
    win = jnp.concatenate([a_in[0][:, 0:C_KR], zc(NOPE), a_in[0][:, C_KR:C_KR + ROPE], zc(HP - NOPE - ROPE),
                           a_in[0][:, C_KR + ROPE:],
                           a_in[1], a_in[2], a_in[3]], axis=1)
    wuq = jnp.pad(w_uq_f.reshape(Q_LORA, HEADS, NOPE + ROPE), ((0, 0), (0, 0), (0, HP - NOPE - ROPE)))
    wuq = wuq.reshape(Q_LORA, HEADS * HP)
    ukv = w_ukv_f.reshape(KV_LORA, HEADS, NOPE + VDIM)
    wk = jnp.pad(ukv[:, :, 0:NOPE], ((0, 0), (0, 0), (0, HP - NOPE))).reshape(KV_LORA, HEADS * HP)
    wkv = jnp.concatenate([wk, ukv[:, :, NOPE:].reshape(KV_LORA, MLA_W)], axis=1)

    lane = np.arange(HP)
    half = ROPE // 2
    inv_freq = (1.0 / (ROPE_THETA ** (np.arange(half, dtype=np.float32) / half))).astype(np.float32)
    in_rope = (lane >= NOPE) & (lane < NOPE + ROPE)
    invf = jnp.asarray(np.where(in_rope, inv_freq[(lane - NOPE) % half], 0.0).astype(np.float32))
    m1 = jnp.asarray(np.where((lane >= NOPE) & (lane < NOPE + half), -1.0, 0.0).astype(np.float32))
    m2 = jnp.asarray(np.where((lane >= NOPE + half) & (lane < NOPE + ROPE), 1.0, 0.0).astype(np.float32))
    row = lambda a: jnp.pad(a.astype(F32), (0, D_MODEL - a.shape[0]))
    pvec = jnp.stack([row(q_norm_g), row(kv_norm_g), row(sgu_norm_g), row(sgu_norm_b), row(invf), row(m1),
                      row(m2), row(ln_g), row(ln_b)] + [jnp.zeros((D_MODEL,), F32)] * (PV_ROWS - 9))
    tri = jnp.tril(jnp.ones((CHUNK, CHUNK), dtype=bool))
    wt = jnp.where(tri[None], w_spatial, 0.0).astype(BF16)
    wtt = jnp.swapaxes(wt, 1, 2)
    bsp = jnp.repeat(b_spatial.T, VDIM, axis=1)

    cq, ckv, gate, q, k, v, vt, cs = _fwd_pre(x2, pos, win, wuq, wkv, pvec)
    o, lse = _attn_fwd(q, k, vt)
    dh2, do, dgate, g_wout, g_wsp, gvec = _post(x2, tgt, o, gate, wout, pvec, wt, wtt, bsp)
    dq, dk, dv = _attn_bwd(q, k, v, do, o, lse, cs, pvec)
    gx, g_win, g_wuq, g_wkv, gvec = _bwd_pre(x2, dh2, cq, ckv, cs, dq, dk, dv, dgate, win, wuq, wkv, pvec, gvec)

    cw = w_in.shape[1]
    first = D_INR - 3 * cw
    g_win_0 = jnp.concatenate([g_win[:, 0:C_KR], g_win[:, C_KR + NOPE:C_KR + NOPE + ROPE], g_win[:, C_GATE:first]],
                              axis=1)
    g_win_b = jnp.stack([g_win_0] + [g_win[:, first + cw * jb:first + cw * (jb + 1)] for jb in range(3)])
    g_wuq_f = g_wuq.reshape(Q_LORA, HEADS, HP)[:, :, 0:NOPE + ROPE].reshape(Q_LORA, HEADS * (NOPE + ROPE))
    g_k = g_wkv[:, 0:HEADS * HP].reshape(KV_LORA, HEADS, HP)[:, :, 0:NOPE]
    g_v = g_wkv[:, HEADS * HP:].reshape(KV_LORA, HEADS, VDIM)
    g_wukv_f = jnp.concatenate([g_k, g_v], axis=2).reshape(KV_LORA, HEADS * (NOPE + VDIM))

    def by_chip(a):
        rows, cols = a.shape[0], a.shape[1] // 4
        return jnp.swapaxes(a.reshape(rows, 4, cols), 0, 1).reshape(4, 2, rows // 2, cols)

    gs = [g_win_b.reshape(4, 2, D_MODEL // 2, cw), by_chip(g_wuq_f), by_chip(g_wukv_f), g_wout.reshape(4, 2, 128, D_MODEL),
          g_wsp.reshape(4, 2, CHUNK, CHUNK)]
    r_in, r_uq, r_ukv, r_out, r_wsp, r_vec = _grad_reduce(gs, gvec)

    g_big = [r_in.reshape(w_in.shape), r_uq.reshape(w_uq.shape), r_ukv.reshape(w_ukv.shape),
             r_out.reshape(w_out.shape), r_wsp.reshape(w_spatial.shape)]
    small = lambda qg, kvg, sg, sb, lng, lnb, bs: [qg.reshape(1, -1), kvg.reshape(1, -1), sg.reshape(1, -1),
                                                   sb.reshape(1, -1), lng.reshape(1, -1), lnb.reshape(1, -1), bs]
    res = _adamw(g_big, [w_in, w_uq, w_ukv, w_out, w_spatial], [m_w_in, m_w_uq, m_w_ukv, m_w_out, m_w_spatial],
                 [v_w_in, v_w_uq, v_w_ukv, v_w_out, v_w_spatial], r_vec,
                 small(q_norm_g, kv_norm_g, sgu_norm_g, sgu_norm_b, ln_g, ln_b, b_spatial),
                 small(m_q_norm_g, m_kv_norm_g, m_sgu_norm_g, m_sgu_norm_b, m_ln_g, m_ln_b, m_b_spatial),
                 small(v_q_norm_g, v_kv_norm_g, v_sgu_norm_g, v_sgu_norm_b, v_ln_g, v_ln_b, v_b_spatial))

    def ordered(big, sm):
        vec = lambda n: sm[n].reshape(-1)
        return [big[0], vec(0), big[1], vec(1), big[2], vec(2), vec(3), big[4], sm[6], big[3], vec(4), vec(5)]

    loss = r_vec[GV_LOSS, 0]
    return (loss, gx.reshape(1, seq, D_MODEL), *ordered(res[0:5], res[20:27]), *ordered(res[5:10], res[27:34]),
            *ordered(res[10:15], res[34:41]), *ordered(res[15:20], res[41:48]))
```

```python
import math

import jax
import jax.numpy as jnp
import numpy as np
from jax import lax
from jax.experimental import pallas as pl
from jax.experimental.pallas import tpu as pltpu

F32 = jnp.float32
BF16 = jnp.bfloat16

D_MODEL = 1024
Q_LORA = 256
KV_LORA = 128
HEADS = 8
NOPE = 64
ROPE = 32
VDIM = 64
MLA_W = HEADS * VDIM
GW = 512
CHUNK = 128
HP = 128
PAIRS = HEADS // 2
D_IN = 2464
D_INR = 2560
C_CKV = Q_LORA
C_KR = Q_LORA + KV_LORA
C_GATE = C_KR + HP
ROPE_THETA = 10000.0
DN_ALPHA = 2.0 ** 0.25
EPS = 1e-5
SCALE = 1.0 / math.sqrt(NOPE + ROPE)
SCALE_LOG2E = SCALE * 1.4426950408889634
INV_SQRT2 = 0.7071067811865476
INV_SQRT_2PI = 0.3989422804014327

ADAM_LR = 0.001
ADAM_B1 = 0.9
ADAM_B2 = 0.999
ADAM_EPS = 1e-08
ADAM_WD = 0.01
ADAM_STEP = 10

PV_QG, PV_KVG, PV_SG, PV_SB, PV_INVF, PV_M1, PV_M2, PV_LNG, PV_LNB = range(9)
PV_ROWS = 16
GV_QG, GV_KVG, GV_SG, GV_SB, GV_LNG, GV_LNB, GV_LOSS = range(7)
GV_BSP = 8
GV_ROWS = 16

MESH = pl.DeviceIdType.MESH

FWD_TILE = 1024
POST_TILE = 512
BWD_TILE = 512
ATT_BLK = 512
ADAM_STEPS = 4
VMEM_LIMIT = 60 * 1024 * 1024


def _dot(a, b):
    return jnp.dot(a, b, preferred_element_type=F32)


def _dot_nt(a, b):
    return lax.dot_general(a, b, (((1,), (1,)), ((), ())), preferred_element_type=F32)


def _dot_tn(a, b):
    return lax.dot_general(a, b, (((0,), (0,)), ((), ())), preferred_element_type=F32)


def _sigmoid(z):
    return pl.reciprocal(1.0 + jnp.exp(-z), approx=True)


def _gelu_and_grad(x):
    cdf = 0.5 * (1.0 + lax.erf(x * INV_SQRT2))
    return x * cdf, cdf + x * (INV_SQRT_2PI * jnp.exp(-0.5 * x * x))


def _rms_stats(x):
    r = lax.rsqrt(jnp.mean(x * x, axis=-1, keepdims=True) + EPS)
    return x * r, r


def _rms_bwd(dy, g, xh, r):
    dyg = dy * g
    return r * (dyg - xh * jnp.mean(dyg * xh, axis=-1, keepdims=True))


def _ln_stats(x):
    mu = jnp.mean(x, axis=-1, keepdims=True)
    xc = x - mu
    r = lax.rsqrt(jnp.mean(xc * xc, axis=-1, keepdims=True) + EPS)
    return xc * r, r


def _ln_bwd(dy, g, xh, r):
    dxh = dy * g
    return r * (dxh - jnp.mean(dxh, axis=-1, keepdims=True) - xh * jnp.mean(dxh * xh, axis=-1, keepdims=True))


def _rope_fwd(t, c, s1, s2):
    return t * c + pltpu.roll(t, HP - 16, 1) * s1 + pltpu.roll(t, 16, 1) * s2


def _rope_bwd(d, c, s1, s2):
    return d * c + pltpu.roll(d * s1, 16, 1) + pltpu.roll(d * s2, HP - 16, 1)


def _lane_lt64(shape):
    return lax.broadcasted_iota(jnp.int32, shape, len(shape) - 1) < 64


def _spatial_mix(w_ref, src, dst_ref, rows):
    for c in range(rows // CHUNK):
        for p in range(PAIRS):
            blk = src[c * CHUNK:(c + 1) * CHUNK, p * HP:(p + 1) * HP]
            a = _dot(w_ref[2 * p], blk)
            b = _dot(w_ref[2 * p + 1], blk)
            dst_ref[c * CHUNK:(c + 1) * CHUNK, p * HP:(p + 1) * HP] = jnp.where(_lane_lt64(a.shape), a, b)


def _gmlp_fwd(u_pre, v_pre, zb, sg, sb, wt_ref, bsp_ref, sv_ref, rows):
    u, du = _gelu_and_grad(u_pre)
    gv, dgv = _gelu_and_grad(v_pre)
    xh, r = _ln_stats(gv)
    vln = (xh * sg + sb).astype(BF16)
    _spatial_mix(wt_ref, vln, sv_ref, rows)
    bias = bsp_ref[...]
    svb = sv_ref[...] + jnp.concatenate([bias] * (rows // CHUNK), axis=0)
    sig = _sigmoid(zb)
    return u, du, dgv, xh, r, vln, svb, sig


def _weight_gather(shards):
    n_arr = len(shards)

    def body(*refs):
        ins, outs = refs[0:n_arr], refs[n_arr:2 * n_arr]
        send_sems, recv_sems = refs[2 * n_arr:]
        x, y, c = lax.axis_index("x"), lax.axis_index("y"), lax.axis_index("c")
        j = 2 * x + y
        sib = (x, y, 1 - c)
        chips = [(1 - x, y), (x, 1 - y), (1 - x, 1 - y)]
        for n in range(n_arr):
            outs[n][j] = ins[n][...].astype(BF16)

        def half(n, blk, core):
            r = shards[n].shape[0] // 2
            return outs[n].at[blk, pl.ds(pl.multiple_of(core * r, 16), r), :]

        def copy(k, ref, to):
            return pltpu.make_async_remote_copy(
                src_ref=ref, dst_ref=ref, send_sem=send_sems.at[k], recv_sem=recv_sems.at[k],
                device_id=to, device_id_type=MESH)

        first = [copy(6 * n + kk, half(n, j, c), (px, py, c))
                 for n in range(n_arr) for kk, (px, py) in enumerate(chips)]
        for cp in first:
            cp.start()
        passed = []
        for n in range(n_arr):
            for kk, (px, py) in enumerate(chips):
                landed = half(n, 2 * px + py, c)
                copy(6 * n + kk, landed, (px, py, c)).wait_recv()
                passed.append(copy(6 * n + 3 + kk, landed, sib))
                passed[-1].start()
        for n in range(n_arr):
            for kk, (px, py) in enumerate(chips):
                copy(6 * n + 3 + kk, half(n, 2 * px + py, 1 - c), sib).wait_recv()
        for cp in first + passed:
            cp.wait_send()

    vmem = pl.BlockSpec(memory_space=pltpu.VMEM)
    return pl.pallas_call(
        body, name="weight_gather",
        out_shape=[jax.ShapeDtypeStruct((4,) + a.shape, BF16) for a in shards],
        in_specs=[vmem] * n_arr, out_specs=[vmem] * n_arr,
        scratch_shapes=[pltpu.SemaphoreType.DMA((6 * n_arr,)), pltpu.SemaphoreType.DMA((6 * n_arr,))],
        compiler_params=pltpu.CompilerParams(vmem_limit_bytes=VMEM_LIMIT),
    )(*shards)


def _fwd_pre(x, pos, win, wuq, wkv, pvec):
    seq = x.shape[0]
    t = FWD_TILE

    def body(x_ref, pos_ref, win_ref, wuq_ref, wkv_ref, pv_ref,
             cq_o, ckv_o, gate_o, q_o, k_o, v_o, vt_o, cs_o):
        xb = x_ref[...].astype(BF16)
        proj = _dot(xb, win_ref[:, 0:C_GATE])
        cq = proj[:, 0:C_CKV]
        ckv = proj[:, C_CKV:C_KR]
        kr = proj[:, C_KR:C_GATE]
        cq_o[...] = cq
        ckv_o[...] = ckv

        ang = pos_ref[...].astype(F32) * pv_ref[PV_INVF:PV_INVF + 1, 0:HP]
        cos = jnp.cos(ang)
        sin = jnp.sin(ang)
        cs_o[:, 0:HP] = cos
        cs_o[:, HP:2 * HP] = sin
        s1 = sin * pv_ref[PV_M1:PV_M1 + 1, 0:HP]
        s2 = sin * pv_ref[PV_M2:PV_M2 + 1, 0:HP]

        cqh, _ = _rms_stats(cq)
        q_all = _dot((cqh * pv_ref[PV_QG:PV_QG + 1, 0:Q_LORA]).astype(BF16), wuq_ref[...])
        ckvh, _ = _rms_stats(ckv)
        kv_all = _dot((ckvh * pv_ref[PV_KVG:PV_KVG + 1, 0:KV_LORA]).astype(BF16), wkv_ref[...])
        krr = _rope_fwd(kr, cos, s1, s2)
        for h in range(HEADS):
            sl = slice(h * HP, (h + 1) * HP)
            q_o[:, sl] = (_rope_fwd(q_all[:, sl], cos, s1, s2) * SCALE_LOG2E).astype(BF16)
            k_o[:, sl] = (kv_all[:, sl] + krr).astype(BF16)
        val = kv_all[:, HEADS * HP:].astype(BF16)
        v_o[...] = val
        for blk in range(t // ATT_BLK):
            vt_o[blk] = val[blk * ATT_BLK:(blk + 1) * ATT_BLK, :].T
        gate_o[...] = _dot(xb, win_ref[:, C_GATE:D_INR]).astype(BF16)

    tile = lambda w: pl.BlockSpec((t, w), lambda i: (i, 0))
    full = lambda a: pl.BlockSpec(a.shape, lambda i: (0,) * a.ndim)
    outs = [(Q_LORA, F32), (KV_LORA, F32), (2048, BF16), (HEADS * HP, BF16), (HEADS * HP, BF16), (MLA_W, BF16)]
    assert t % ATT_BLK == 0
    out_specs = [tile(w) for w, _ in outs]
    out_specs += [pl.BlockSpec((t // ATT_BLK, MLA_W, ATT_BLK), lambda i: (i, 0, 0)), tile(2 * HP)]
    out_shape = [jax.ShapeDtypeStruct((seq, w), d) for w, d in outs]
    out_shape += [jax.ShapeDtypeStruct((seq // ATT_BLK, MLA_W, ATT_BLK), BF16), jax.ShapeDtypeStruct((seq, 2 * HP), F32)]
    return pl.pallas_call(
        body, name="fwd_pre", grid=(seq // t,),
        in_specs=[tile(D_MODEL), tile(1), full(win), full(wuq), full(wkv), full(pvec)],
        out_specs=out_specs, out_shape=out_shape,
        compiler_params=pltpu.CompilerParams(dimension_semantics=("arbitrary",), vmem_limit_bytes=VMEM_LIMIT),
    )(x, pos, win, wuq, wkv, pvec)


def _attn_fwd(q, k, vt):
    seq = q.shape[0]
    b = ATT_BLK
    nq = seq // b
    assert nq % 2 == 0
    assert nq % 4 == 0
    n_wide = sum(i // 4 for i in range(nq))

    def body(q_ref, k_ref, vt_ref, o_o, lse_o, m_ref, l_ref, acc_ref, s_even, s_odd):
        m_ref[...] = jnp.full(m_ref.shape, -jnp.inf, F32)
        l_ref[...] = jnp.zeros(l_ref.shape, F32)
        acc_ref[...] = jnp.zeros(acc_ref.shape, F32)

        def scores(i, j, s_ref, nkb):
            qrows = pl.ds(pl.multiple_of(i * b, b), b)
            krows = pl.ds(pl.multiple_of(j * b, b), nkb * b)
            for a in range(2):
                s_ref[a, 0:nkb * b, :] = _dot_nt(k_ref[krows, a * HP:(a + 1) * HP], q_ref[qrows, a * HP:(a + 1) * HP])

        def consume(i, j, s_ref, nkb, masked):
            for a in range(2):
                st = s_ref[a, 0:nkb * b, :]
                if masked:
                    ki = lax.broadcasted_iota(jnp.int32, st.shape, 0)
                    qi = lax.broadcasted_iota(jnp.int32, st.shape, 1)
                    st = jnp.where(ki <= qi, st, -jnp.inf)
                m_prev = m_ref[i, a:a + 1, :]
                m_new = jnp.maximum(m_prev, jnp.max(st, axis=0, keepdims=True))
                alpha = jnp.exp2(m_prev - m_new)
                pt = jnp.exp2(st - m_new)
                ptb = pt.astype(BF16)
                l_ref[i, a:a + 1, :] = alpha * l_ref[i, a:a + 1, :] + jnp.sum(pt, axis=0, keepdims=True)
                pv = _dot(vt_ref[j], ptb[0:b, :])
                for kb in range(1, nkb):
                    pv = pv + _dot(vt_ref[j + kb], ptb[kb * b:(kb + 1) * b, :])
                acc_ref[i, a] = alpha * acc_ref[i, a] + pv
                m_ref[i, a:a + 1, :] = m_new

        def run(count, first, following, nkb, masked):
            if count == 0:
                return
            scores(*first, s_even, nkb)

            def two(u, ij):
                nxt = following(*ij)
                scores(*nxt, s_odd, nkb)
                consume(*ij, s_even, nkb, masked)
                nxt2 = following(*nxt)
                scores(*nxt2, s_even, nkb)
                consume(*nxt, s_odd, nkb, masked)
                return nxt2

            last = lax.fori_loop(0, count // 2, two, tuple(jnp.int32(c) for c in first))
            if count % 2:
                consume(*last, s_even, nkb, masked)

        def clamp(i):
            return jnp.minimum(i, nq - 1)

        def next_wide(i, j):
            wrap = j + 8 > i
            return clamp(jnp.where(wrap, i + 1, i)), jnp.where(wrap, 0, j + 4)

        def next_pair(i, j):
            low = lax.rem(i, 4) == 2
            return clamp(jnp.where(low, i + 1, i + 3)), jnp.minimum(jnp.where(low, j, j + 4), nq - 4)

        run(n_wide, (4, 0), next_wide, 4, False)
        run(nq // 2, (2, 0), next_pair, 2, False)
        run(nq // 2, (1, 0), lambda i, j: (clamp(i + 2), jnp.minimum(j + 2, nq - 2)), 1, False)
        run(nq, (0, 0), lambda i, j: (clamp(i + 1), clamp(j + 1)), 1, True)
        top = lax.broadcasted_iota(jnp.int32, (HP, b), 0) < 64

        def finish(i, carry):
            rows = pl.ds(pl.multiple_of(i * b, b), b)
            o_o[rows, :] = jnp.where(top, acc_ref[i, 0] / l_ref[i, 0:1, :], acc_ref[i, 1] / l_ref[i, 1:2, :]).T
            lse_o[i] = m_ref[i, 0:2, :] + jnp.log2(l_ref[i, 0:2, :])
            return carry

        lax.fori_loop(0, nq, finish, 0)

    return pl.pallas_call(
        body, name="attn_fwd", grid=(PAIRS,),
        in_specs=[pl.BlockSpec((seq, 2 * HP), lambda p: (0, p)),
                  pl.BlockSpec((seq, 2 * HP), lambda p: (0, p)),
                  pl.BlockSpec((nq, HP, b), lambda p: (0, p, 0))],
        out_specs=[pl.BlockSpec((seq, HP), lambda p: (0, p)),
                   pl.BlockSpec((None, nq, 2, b), lambda p: (p, 0, 0, 0))],
        out_shape=[jax.ShapeDtypeStruct((seq, MLA_W), F32),
                   jax.ShapeDtypeStruct((PAIRS, nq, 2, b), F32)],
        scratch_shapes=[pltpu.VMEM((nq, 8, b), F32), pltpu.VMEM((nq, 8, b), F32), pltpu.VMEM((nq, 2, HP, b), F32),
                        pltpu.VMEM((2, 4 * b, b), F32), pltpu.VMEM((2, 4 * b, b), F32)],
        compiler_params=pltpu.CompilerParams(dimension_semantics=("arbitrary",), vmem_limit_bytes=VMEM_LIMIT),
    )(q, k, vt)


def _post(x, tgt, o, gate, wout, pvec, wt, wtt, bsp):
    seq = x.shape[0]
    t = POST_TILE
    nt = seq // t

    def body(x_ref, tgt_ref, o_ref, gate_ref, wout_ref, pv_ref, wt_ref, wtt_ref, bsp_ref,
             dh2_o, do_o, dgate_o, gwout_o, gwsp_o, vec_o, sv_ref, dvln_ref, bacc_ref):
        i = pl.program_id(0)

        @pl.when(i == 0)
        def _():
            gwout_o[...] = jnp.zeros_like(gwout_o)
            gwsp_o[...] = jnp.zeros_like(gwsp_o)
            vec_o[...] = jnp.zeros_like(vec_o)
            bacc_ref[...] = jnp.zeros_like(bacc_ref)

        za = gate_ref[:, 0:512].astype(F32)
        u_pre = gate_ref[:, 512:1024].astype(F32)
        v_pre = gate_ref[:, 1024:1536].astype(F32)
        zb = gate_ref[:, 1536:2048].astype(F32)
        sg = pv_ref[PV_SG:PV_SG + 1, 0:GW]
        sb = pv_ref[PV_SB:PV_SB + 1, 0:GW]
        lng = pv_ref[PV_LNG:PV_LNG + 1, :]
        lnb = pv_ref[PV_LNB:PV_LNB + 1, :]
        o = o_ref[...]

        sig_a = _sigmoid(za)
        silu_a = za * sig_a
        u, du, dgv, xh, r, vln, svb, sig_b = _gmlp_fwd(u_pre, v_pre, zb, sg, sb, wt_ref, bsp_ref, sv_ref, t)
        silu_b = zb * sig_b
        sgu = u * svb
        merged = jnp.concatenate([o * silu_a, sgu * silu_b], axis=1).astype(BF16)
        h2 = DN_ALPHA * x_ref[...] + _dot(merged, wout_ref[...])
        xh2, r2 = _ln_stats(h2)
        err = xh2 * lng + lnb - tgt_ref[...]
        d_out = err * (1.0 / D_MODEL)
        vec_o[GV_LNG:GV_LNG + 1, :] += jnp.sum(d_out * xh2, axis=0, keepdims=True)
        vec_o[GV_LNB:GV_LNB + 1, :] += jnp.sum(d_out, axis=0, keepdims=True)
        vec_o[GV_LOSS:GV_LOSS + 1, :] += jnp.sum(err * err, axis=0, keepdims=True) * (0.5 / D_MODEL)

        d_h2 = _ln_bwd(d_out, lng, xh2, r2)
        dh2_o[...] = d_h2
        dh2b = d_h2.astype(BF16)
        gwout_o[...] += _dot_tn(merged, dh2b)
        d_m = _dot_nt(dh2b, wout_ref[...])
        d_oa = d_m[:, 0:512]
        d_ob = d_m[:, 512:1024]
        do_o[...] = (d_oa * silu_a).astype(BF16)
        dgate_o[:, 0:512] = (d_oa * o * (sig_a * (1.0 + za * (1.0 - sig_a)))).astype(BF16)
        dgate_o[:, 1536:2048] = (d_ob * sgu * (sig_b * (1.0 + zb * (1.0 - sig_b)))).astype(BF16)
        d_sgu = d_ob * silu_b
        dgate_o[:, 512:1024] = (d_sgu * svb * du).astype(BF16)
        d_sv = d_sgu * u
        acc = bacc_ref[...]
        for c in range(t // CHUNK):
            acc = acc + d_sv[c * CHUNK:(c + 1) * CHUNK, :]
        bacc_ref[...] = acc
        d_svb = d_sv.astype(BF16)
        for c in range(t // CHUNK):
            for p in range(PAIRS):
                blk = d_svb[c * CHUNK:(c + 1) * CHUNK, p * HP:(p + 1) * HP]
                vblk = vln[c * CHUNK:(c + 1) * CHUNK, p * HP:(p + 1) * HP]
                first = _lane_lt64(blk.shape)
                gwsp_o[2 * p] += _dot_nt(jnp.where(first, blk, jnp.zeros_like(blk)), vblk)
                gwsp_o[2 * p + 1] += _dot_nt(jnp.where(first, jnp.zeros_like(blk), blk), vblk)
        _spatial_mix(wtt_ref, d_svb, dvln_ref, t)
        d_vln = dvln_ref[...]
        vec_o[GV_SG:GV_SG + 1, 0:GW] += jnp.sum(d_vln * xh, axis=0, keepdims=True)
        vec_o[GV_SB:GV_SB + 1, 0:GW] += jnp.sum(d_vln, axis=0, keepdims=True)
        dgate_o[:, 1024:1536] = (_ln_bwd(d_vln, sg, xh, r) * dgv).astype(BF16)


        @pl.when(i == nt - 1)
        def _():
            tri = (lax.broadcasted_iota(jnp.int32, (CHUNK, CHUNK), 1)
                   <= lax.broadcasted_iota(jnp.int32, (CHUNK, CHUNK), 0))
            for h in range(HEADS):
                gwsp_o[h] = jnp.where(tri, gwsp_o[h], 0.0)
            lane = lax.broadcasted_iota(jnp.int32, (CHUNK, HP), 1)
            res = jnp.zeros((CHUNK, HP), F32)
            for h in range(HEADS):
                p, a = divmod(h, 2)
                blk = bacc_ref[:, p * HP:(p + 1) * HP]
                part = jnp.where(_lane_lt64(blk.shape) == (a == 0), blk, 0.0)
                res = jnp.where(lane == h, jnp.sum(part, axis=-1, keepdims=True), res)
            vec_o[GV_BSP:GV_BSP + HEADS, 0:HP] = res.T[0:HEADS, :]
            lane1 = lax.broadcasted_iota(jnp.int32, (1, D_MODEL), 1)
            total = jnp.sum(vec_o[GV_LOSS:GV_LOSS + 1, :], axis=-1, keepdims=True)
            vec_o[GV_LOSS:GV_LOSS + 1, :] = jnp.where(lane1 == 0, total, 0.0)

    tile = lambda w: pl.BlockSpec((t, w), lambda i: (i, 0))
    full = lambda a: pl.BlockSpec(a.shape, lambda i: (0,) * a.ndim)
    const = lambda s: pl.BlockSpec(s, lambda i: (0,) * len(s))
    return pl.pallas_call(
        body, name="post", grid=(nt,),
        in_specs=[tile(D_MODEL), tile(D_MODEL), tile(MLA_W), tile(2048), full(wout), full(pvec),
                  full(wt), full(wtt), full(bsp)],
        out_specs=[tile(D_MODEL), tile(MLA_W), tile(2048), const((D_MODEL, D_MODEL)),
                   const((HEADS, CHUNK, CHUNK)), const((GV_ROWS, D_MODEL))],
        out_shape=[jax.ShapeDtypeStruct((seq, D_MODEL), F32), jax.ShapeDtypeStruct((seq, MLA_W), BF16),
                   jax.ShapeDtypeStruct((seq, 2048), BF16), jax.ShapeDtypeStruct((D_MODEL, D_MODEL), F32),
                   jax.ShapeDtypeStruct((HEADS, CHUNK, CHUNK), F32), jax.ShapeDtypeStruct((GV_ROWS, D_MODEL), F32)],
        scratch_shapes=[pltpu.VMEM((t, GW), F32), pltpu.VMEM((t, GW), F32), pltpu.VMEM((CHUNK, GW), F32)],
        compiler_params=pltpu.CompilerParams(dimension_semantics=("arbitrary",), vmem_limit_bytes=VMEM_LIMIT),
    )(x, tgt, o, gate, wout, pvec, wt, wtt, bsp)


def _attn_bwd(q, k, v, do, o, lse, cs, pvec):
    seq = q.shape[0]
    b = ATT_BLK
    nq = seq // b

    def body(q_ref, k_ref, v_ref, do_ref, o_ref, lse_ref, cs_ref, pv_ref, dq_o, dk_o, dv_o, dk_acc, dv_acc):
        i = pl.program_id(1)

        @pl.when(i == 0)
        def _():
            dk_acc[...] = jnp.zeros_like(dk_acc)
            dv_acc[...] = jnp.zeros_like(dv_acc)

        first = _lane_lt64((b, HP))
        do = do_ref[...]
        zero = jnp.zeros_like(do)
        dos = [jnp.where(first, do, zero), jnp.where(first, zero, do)]
        prod_t = (do.astype(F32) * o_ref[...]).T
        deltas = [jnp.sum(prod_t[0:64, :], axis=0, keepdims=True),
                  jnp.sum(prod_t[64:128, :], axis=0, keepdims=True)]
        lses = [lse_ref[0:1, :], lse_ref[1:2, :]]
        qs = [q_ref[:, a * HP:(a + 1) * HP] for a in range(2)]

        def step(j, dqs, masked, nk=b):
            rows = pl.ds(pl.multiple_of(j * b, b), nk)
            vb = v_ref[rows, :]
            new_dq = []
            dvs = []
            for a in range(2):
                kb = k_ref[rows, a * HP:(a + 1) * HP]
                pt = jnp.exp2(_dot_nt(kb, qs[a]) - lses[a])
                if masked:
                    ki = lax.broadcasted_iota(jnp.int32, pt.shape, 0)
                    qi = lax.broadcasted_iota(jnp.int32, pt.shape, 1) + (nk - b)
                    pt = jnp.where(ki <= qi, pt, 0.0)
                dvs.append(_dot(pt.astype(BF16), do))
                dpt = _dot_nt(vb, dos[a])
                dst = (pt * (dpt - deltas[a])).astype(BF16)
                dk_acc[rows, a * HP:(a + 1) * HP] += _dot(dst, qs[a])
                new_dq.append(dqs[a] + _dot_tn(dst, kb))
            dv_acc[rows, :] += jnp.where(_lane_lt64((nk, HP)), dvs[0], dvs[1])
            return tuple(new_dq)

        init = (jnp.zeros((b, HP), F32), jnp.zeros((b, HP), F32))
        dqs = lax.fori_loop(0, i // 4, lambda jj, cr: step(4 * jj, cr, False, 4 * b), init)
        last = [lambda cr, w=w: step(4 * (i // 4), cr, True, w * b) for w in (1, 2, 3, 4)]
        dqs = lax.switch(i % 4, last, dqs)
        cos = cs_ref[:, 0:HP]
        sin = cs_ref[:, HP:2 * HP]
        s1 = sin * pv_ref[PV_M1:PV_M1 + 1, 0:HP]
        s2 = sin * pv_ref[PV_M2:PV_M2 + 1, 0:HP]
        for a in range(2):
            dq_o[:, a * HP:(a + 1) * HP] = _rope_bwd(dqs[a] * SCALE, cos, s1, s2).astype(BF16)

        @pl.when(i == nq - 1)
        def _():
            dk_o[...] = (dk_acc[...] * (SCALE / SCALE_LOG2E)).astype(BF16)
            dv_o[...] = dv_acc[...].astype(BF16)

    return pl.pallas_call(
        body, name="attn_bwd", grid=(PAIRS, nq),
        in_specs=[pl.BlockSpec((b, 2 * HP), lambda p, i: (i, p)),
                  pl.BlockSpec((seq, 2 * HP), lambda p, i: (0, p)),
                  pl.BlockSpec((seq, HP), lambda p, i: (0, p)),
                  pl.BlockSpec((b, HP), lambda p, i: (i, p)),
                  pl.BlockSpec((b, HP), lambda p, i: (i, p)),
                  pl.BlockSpec((None, None, 2, b), lambda p, i: (p, i, 0, 0)),
                  pl.BlockSpec((b, 2 * HP), lambda p, i: (i, 0)),
                  pl.BlockSpec(pvec.shape, lambda p, i: (0, 0))],
        out_specs=[pl.BlockSpec((b, 2 * HP), lambda p, i: (i, p)),
                   pl.BlockSpec((seq, 2 * HP), lambda p, i: (0, p)),
                   pl.BlockSpec((seq, HP), lambda p, i: (0, p))],
        out_shape=[jax.ShapeDtypeStruct((seq, HEADS * HP), BF16),
                   jax.ShapeDtypeStruct((seq, HEADS * HP), BF16),
                   jax.ShapeDtypeStruct((seq, MLA_W), BF16)],
        scratch_shapes=[pltpu.VMEM((seq, 2 * HP), F32), pltpu.VMEM((seq, HP), F32)],
        compiler_params=pltpu.CompilerParams(dimension_semantics=("arbitrary", "arbitrary"),
                                             vmem_limit_bytes=VMEM_LIMIT),
    )(q, k, v, do, o, lse, cs, pvec)


def _bwd_pre(x, dh2, cq, ckv, cs, dq, dk, dv, dgate, win, wuq, wkv, pvec, gvec):
    seq = x.shape[0]
    t = BWD_TILE

    def body(x_ref, dh2_ref, cq_ref, ckv_ref, cs_ref, dq_ref, dk_ref, dv_ref, dgate_ref,
             win_ref, wuq_ref, wkv_ref, pv_ref, gv_ref, gx_o, gwin_o, gwuq_o, gwkv_o, vec_o):
        i = pl.program_id(0)

        @pl.when(i == 0)
        def _():
            gwin_o[...] = jnp.zeros_like(gwin_o)
            gwuq_o[...] = jnp.zeros_like(gwuq_o)
            gwkv_o[...] = jnp.zeros_like(gwkv_o)
            vec_o[...] = gv_ref[...]

        xb = x_ref[...].astype(BF16)
        dgate = dgate_ref[...]
        gwin_o[:, C_GATE:D_INR] += _dot_tn(xb, dgate)
        gx_gate = _dot_nt(dgate, win_ref[:, C_GATE:D_INR])

        qg = pv_ref[PV_QG:PV_QG + 1, 0:Q_LORA]
        kvg = pv_ref[PV_KVG:PV_KVG + 1, 0:KV_LORA]
        dq = dq_ref[...]
        cqh, rq = _rms_stats(cq_ref[...])
        d_cqn = _dot_nt(dq, wuq_ref[...])
        gwuq_o[...] += _dot_tn((cqh * qg).astype(BF16), dq)
        vec_o[GV_QG:GV_QG + 1, 0:Q_LORA] += jnp.sum(d_cqn * cqh, axis=0, keepdims=True)
        d_cq = _rms_bwd(d_cqn, qg, cqh, rq)

        dk = dk_ref[...]
        dkv = jnp.concatenate([dk, dv_ref[...]], axis=1)
        ckvh, rkv = _rms_stats(ckv_ref[...])
        d_ckvn = _dot_nt(dkv, wkv_ref[...])
        gwkv_o[...] += _dot_tn((ckvh * kvg).astype(BF16), dkv)
        vec_o[GV_KVG:GV_KVG + 1, 0:KV_LORA] += jnp.sum(d_ckvn * ckvh, axis=0, keepdims=True)
        d_ckv = _rms_bwd(d_ckvn, kvg, ckvh, rkv)

        dks = dk[:, 0:HP].astype(F32)
        for h in range(1, HEADS):
            dks = dks + dk[:, h * HP:(h + 1) * HP].astype(F32)
        cos = cs_ref[:, 0:HP]
        sin = cs_ref[:, HP:2 * HP]
        d_kr = _rope_bwd(dks, cos, sin * pv_ref[PV_M1:PV_M1 + 1, 0:HP], sin * pv_ref[PV_M2:PV_M2 + 1, 0:HP])

        d_lat = jnp.concatenate([d_cq.astype(BF16), d_ckv.astype(BF16), d_kr.astype(BF16)], axis=1)
        gwin_o[:, 0:C_GATE] += _dot_tn(xb, d_lat)
        gx_o[...] = DN_ALPHA * dh2_ref[...] + gx_gate + _dot_nt(d_lat, win_ref[:, 0:C_GATE])

    tile = lambda w: pl.BlockSpec((t, w), lambda i: (i, 0))
    full = lambda a: pl.BlockSpec(a.shape, lambda i: (0,) * a.ndim)
    const = lambda s: pl.BlockSpec(s, lambda i: (0,) * len(s))
    return pl.pallas_call(
        body, name="bwd_pre", grid=(seq // t,),
        in_specs=[tile(D_MODEL), tile(D_MODEL), tile(Q_LORA), tile(KV_LORA), tile(2 * HP), tile(HEADS * HP),
                  tile(HEADS * HP), tile(MLA_W), tile(2048), full(win), full(wuq), full(wkv), full(pvec), full(gvec)],
        out_specs=[tile(D_MODEL), const((D_MODEL, D_INR)), const((Q_LORA, HEADS * HP)),
                   const((KV_LORA, HEADS * HP + MLA_W)), const((GV_ROWS, D_MODEL))],
        out_shape=[jax.ShapeDtypeStruct((seq, D_MODEL), F32), jax.ShapeDtypeStruct((D_MODEL, D_INR), F32),
                   jax.ShapeDtypeStruct((Q_LORA, HEADS * HP), F32),
                   jax.ShapeDtypeStruct((KV_LORA, HEADS * HP + MLA_W), F32),
                   jax.ShapeDtypeStruct((GV_ROWS, D_MODEL), F32)],
        compiler_params=pltpu.CompilerParams(dimension_semantics=("arbitrary",), vmem_limit_bytes=VMEM_LIMIT),
    )(x, dh2, cq, ckv, cs, dq, dk, dv, dgate, win, wuq, wkv, pvec, gvec)


def _grad_reduce(gs, gvec):
    n_arr = len(gs)
    n_big = n_arr - 1
    k1 = lambda n, blk: 4 * n + blk
    k2 = lambda n, kk: 4 * n_arr + 3 * n + kk
    k3 = lambda n: 7 * n_arr + n
    k3w = lambda k: 7 * n_arr + n_big + k
    kv = lambda k: 7 * n_arr + n_big + 7 + k
    n_sem = 7 * n_arr + n_big + 14

    def body(*refs):
        g, gv = refs[0:n_arr], refs[n_arr]
        outs, ov = refs[n_arr + 1:2 * n_arr + 1], refs[2 * n_arr + 1]
        r1 = refs[2 * n_arr + 2:3 * n_arr + 2]
        r2 = refs[3 * n_arr + 2:4 * n_arr + 2]
        s2 = refs[4 * n_arr + 2:5 * n_arr + 2]
        vbuf, send_sems, recv_sems = refs[5 * n_arr + 2:]
        x, y, c = lax.axis_index("x"), lax.axis_index("y"), lax.axis_index("c")
        j = 2 * x + y
        me = 2 * j + c
        sib = (x, y, 1 - c)
        chips = [(1 - x, y), (x, 1 - y), (1 - x, 1 - y)]
        others = [sib] + [(px, py, pc) for (px, py) in chips for pc in (c, 1 - c)]

        def copy(k, src, dst, to):
            return pltpu.make_async_remote_copy(
                src_ref=src, dst_ref=dst, send_sem=send_sems.at[k], recv_sem=recv_sems.at[k],
                device_id=to, device_id_type=MESH)

        l1 = [copy(k1(n, blk), g[n].at[blk, 1 - c], r1[n].at[blk], sib) for n in range(n_arr) for blk in range(4)]
        lv = [copy(kv(k), gv, vbuf.at[me], to) for k, to in enumerate(others)]
        for cp in l1 + lv:
            cp.start()
        l2 = []
        for n in range(n_arr):
            for blk in range(4):
                copy(k1(n, blk), g[n].at[blk, c], r1[n].at[blk], sib).wait_recv()
            for blk in range(4):
                r1[n][blk] = g[n][blk, c] + r1[n][blk]
                s2[n][blk] = r1[n][blk].astype(BF16)
            for kk, (px, py) in enumerate(chips):
                l2.append(copy(k2(n, kk), s2[n].at[2 * px + py], r2[n].at[kk], (px, py, c)))
                l2[-1].start()

        l3 = []
        for n in range(n_arr):
            for kk in range(3):
                copy(k2(n, kk), s2[n].at[0], r2[n].at[kk], sib).wait_recv()
            red = ((r1[n][j] + r2[n][0].astype(F32)) + r2[n][1].astype(F32)) + r2[n][2].astype(F32)
            if n < n_big:
                outs[n][c] = red
                back = [copy(k3(n), outs[n].at[c], outs[n].at[c], sib)]
            else:
                outs[n][j, c] = red
                back = [copy(k3w(k), outs[n].at[j, c], outs[n].at[j, c], to) for k, to in enumerate(others)]
            for cp in back:
                cp.start()
            l3 += back
        for n in range(n_big):
            copy(k3(n), outs[n].at[1 - c], outs[n].at[1 - c], sib).wait_recv()
        for k, (px, py, pc) in enumerate(others):
            landed = outs[n_big].at[2 * px + py, pc]
            copy(k3w(k), landed, landed, (px, py, pc)).wait_recv()
            copy(kv(k), gv, vbuf.at[4 * px + 2 * py + pc], (px, py, pc)).wait_recv()
        vbuf[me] = gv[...]
        total = vbuf[0]
        for d in range(1, 8):
            total = total + vbuf[d]
        ov[...] = total
        for cp in l1 + lv + l2 + l3:
            cp.wait_send()

    vmem = pl.BlockSpec(memory_space=pltpu.VMEM)
    half_shapes = [a.shape[2:] for a in gs]
    out_shape = [jax.ShapeDtypeStruct((2,) + s, F32) for s in half_shapes[:n_big]]
    out_shape += [jax.ShapeDtypeStruct((4, 2) + half_shapes[n_big], F32), jax.ShapeDtypeStruct(gvec.shape, F32)]
    scratch = [pltpu.VMEM((4,) + s, F32) for s in half_shapes] + [pltpu.VMEM((3,) + s, BF16) for s in half_shapes]
    scratch += [pltpu.VMEM((4,) + s, BF16) for s in half_shapes]
    scratch += [pltpu.VMEM((8,) + gvec.shape, F32), pltpu.SemaphoreType.DMA((n_sem,)), pltpu.SemaphoreType.DMA((n_sem,))]
    return pl.pallas_call(
        body, name="grad_reduce", out_shape=out_shape,
        in_specs=[vmem] * (n_arr + 1), out_specs=[vmem] * (n_arr + 1), scratch_shapes=scratch,
        compiler_params=pltpu.CompilerParams(vmem_limit_bytes=VMEM_LIMIT),
    )(*gs, gvec)


SMALL_ROWS = ((GV_QG, 1, Q_LORA), (GV_KVG, 1, KV_LORA), (GV_SG, 1, GW), (GV_SB, 1, GW),
              (GV_LNG, 1, D_MODEL), (GV_LNB, 1, D_MODEL), (GV_BSP, HEADS, CHUNK))


def _adam_update(g, w, m, v):
    m_new = ADAM_B1 * m + (1.0 - ADAM_B1) * g
    v_new = ADAM_B2 * v + (1.0 - ADAM_B2) * (g * g)
    m_hat = m_new / (1.0 - ADAM_B1 ** ADAM_STEP)
    v_hat = v_new / (1.0 - ADAM_B2 ** ADAM_STEP)
    return -ADAM_LR * (m_hat / (jnp.sqrt(v_hat) + ADAM_EPS) + ADAM_WD * w), m_new, v_new


def _adamw(g_big, w_big, m_big, v_big, gvec, w_small, m_small, v_small):
    nb, ns = len(g_big), len(w_small)

    def body(*refs):
        it = iter(refs)
        take = lambda n: [next(it) for _ in range(n)]
        g_b, w_b, m_b, v_b = take(nb), take(nb), take(nb), take(nb)
        gv = next(it)
        w_s, m_s, v_s = take(ns), take(ns), take(ns)
        g_bo, d_bo, m_bo, v_bo = take(nb), take(nb), take(nb), take(nb)
        g_so, d_so, m_so, v_so = take(ns), take(ns), take(ns), take(ns)
        for n in range(nb):
            gb = g_b[n][...]
            g_bo[n][...] = gb
            d_bo[n][...], m_bo[n][...], v_bo[n][...] = _adam_update(gb, w_b[n][...], m_b[n][...], v_b[n][...])
        for n, (row, nrow, width) in enumerate(SMALL_ROWS):
            gs = gv[row:row + nrow, 0:width]
            g_so[n][...] = gs
            d_so[n][...], m_so[n][...], v_so[n][...] = _adam_update(gs, w_s[n][...], m_s[n][...], v_s[n][...])

    def rows(a):
        nd = a.ndim
        return pl.BlockSpec((a.shape[0] // ADAM_STEPS,) + a.shape[1:], lambda i: (i,) + (0,) * (nd - 1))

    def whole(a):
        nd = a.ndim
        return pl.BlockSpec(a.shape, lambda i: (0,) * nd)

    big = [jax.ShapeDtypeStruct(a.shape, F32) for a in w_big]
    small = [jax.ShapeDtypeStruct(a.shape, F32) for a in w_small]
    return pl.pallas_call(
        body, name="adamw", grid=(ADAM_STEPS,), out_shape=big * 4 + small * 4,
        in_specs=[rows(a) for a in g_big + w_big + m_big + v_big] + [whole(gvec)]
        + [whole(a) for a in w_small + m_small + v_small],
        out_specs=[rows(a) for a in w_big] * 4 + [whole(a) for a in w_small] * 4,
        compiler_params=pltpu.CompilerParams(dimension_semantics=("arbitrary",), vmem_limit_bytes=VMEM_LIMIT),
    )(*g_big, *w_big, *m_big, *v_big, gvec, *w_small, *m_small, *v_small)


def kernel(x, positions, w_in, q_norm_g, w_uq, kv_norm_g, w_ukv, sgu_norm_g, sgu_norm_b, w_spatial, b_spatial, w_out, ln_g, ln_b, loss_target, m_w_in, m_q_norm_g, m_w_uq, m_kv_norm_g, m_w_ukv, m_sgu_norm_g, m_sgu_norm_b, m_w_spatial, m_b_spatial, m_w_out, m_ln_g, m_ln_b, v_w_in, v_q_norm_g, v_w_uq, v_kv_norm_g, v_w_ukv, v_sgu_norm_g, v_sgu_norm_b, v_w_spatial, v_b_spatial, v_w_out, v_ln_g, v_ln_b):
    seq = x.shape[1]
    x2 = x.reshape(seq, D_MODEL)
    tgt = loss_target.reshape(seq, D_MODEL)
    pos = positions.reshape(seq, 1)

    a_in, a_uq, a_ukv, a_out = _weight_gather([w_in, w_uq, w_ukv, w_out])
    w_uq_f = jnp.swapaxes(a_uq, 0, 1).reshape(Q_LORA, HEADS * (NOPE + ROPE))
    w_ukv_f = jnp.swapaxes(a_ukv, 0, 1).reshape(KV_LORA, HEADS * (NOPE + VDIM))
    wout = a_out.reshape(D_MODEL, D_MODEL)
    zc = lambda n: jnp.zeros((D_MODEL, n), BF16)
    win = jnp.concatenate([a_in[0][:, 0:C_KR], zc(NOPE), a_in[0][:, C_KR:C_KR + ROPE], zc(HP - NOPE - ROPE),
                           a_in[0][:, C_KR + ROPE:],
                           a_in[1], a_in[2], a_in[3]], axis=1)
    wuq = jnp.pad(w_uq_f.reshape(Q_LORA, HEADS, NOPE + ROPE), ((0, 0), (0, 0), (0, HP - NOPE - ROPE)))
    wuq = wuq.reshape(Q_LORA, HEADS * HP)
    ukv = w_ukv_f.reshape(KV_LORA, HEADS, NOPE + VDIM)
    wk = jnp.pad(ukv[:, :, 0:NOPE], ((0, 0), (0, 0), (0, HP - NOPE))).reshape(KV_LORA, HEADS * HP)
    wkv = jnp.concatenate([wk, ukv[:, :, NOPE:].reshape(KV_LORA, MLA_W)], axis=1)

    lane = np.arange(HP)
    half = ROPE // 2
    inv_freq = (1.0 / (ROPE_THETA ** (np.arange(half, dtype=np.float32) / half))).astype(np.float32)
    in_rope = (lane >= NOPE) & (lane < NOPE + ROPE)
    invf = jnp.asarray(np.where(in_rope, inv_freq[(lane - NOPE) % half], 0.0).astype(np.float32))
    m1 = jnp.asarray(np.where((lane >= NOPE) & (lane < NOPE + half), -1.0, 0.0).astype(np.float32))
    m2 = jnp.asarray(np.where((lane >= NOPE + half) & (lane < NOPE + ROPE), 1.0, 0.0).astype(np.float32))
    row = lambda a: jnp.pad(a.astype(F32), (0, D_MODEL - a.shape[0]))
    pvec = jnp.stack([row(q_norm_g), row(kv_norm_g), row(sgu_norm_g), row(sgu_norm_b), row(invf), row(m1),
                      row(m2), row(ln_g), row(ln_b)] + [jnp.zeros((D_MODEL,), F32)] * (PV_ROWS - 9))
    tri = jnp.tril(jnp.ones((CHUNK, CHUNK), dtype=bool))
    wt = jnp.where(tri[None], w_spatial, 0.0).astype(BF16)
    wtt = jnp.swapaxes(wt, 1, 2)
    bsp = jnp.repeat(b_spatial.T, VDIM, axis=1)

    cq, ckv, gate, q, k, v, vt, cs = _fwd_pre(x2, pos, win, wuq, wkv, pvec)
    o, lse = _attn_fwd(q, k, vt)
    dh2, do, dgate, g_wout, g_wsp, gvec = _post(x2, tgt, o, gate, wout, pvec, wt, wtt, bsp)
    dq, dk, dv = _attn_bwd(q, k, v, do, o, lse, cs, pvec)
    gx, g_win, g_wuq, g_wkv, gvec = _bwd_pre(x2, dh2, cq, ckv, cs, dq, dk, dv, dgate, win, wuq, wkv, pvec, gvec)

    cw = w_in.shape[1]
    first = D_INR - 3 * cw
    g_win_0 = jnp.concatenate([g_win[:, 0:C_KR], g_win[:, C_KR + NOPE:C_KR + NOPE + ROPE], g_win[:, C_GATE:first]],
                              axis=1)
    g_win_b = jnp.stack([g_win_0] + [g_win[:, first + cw * jb:first + cw * (jb + 1)] for jb in range(3)])
    g_wuq_f = g_wuq.reshape(Q_LORA, HEADS, HP)[:, :, 0:NOPE + ROPE].reshape(Q_LORA, HEADS * (NOPE + ROPE))
    g_k = g_wkv[:, 0:HEADS * HP].reshape(KV_LORA, HEADS, HP)[:, :, 0:NOPE]
    g_v = g_wkv[:, HEADS * HP:].reshape(KV_LORA, HEADS, VDIM)
    g_wukv_f = jnp.concatenate([g_k, g_v], axis=2).reshape(KV_LORA, HEADS * (NOPE + VDIM))

    def by_chip(a):
        rows, cols = a.shape[0], a.shape[1] // 4
        return jnp.swapaxes(a.reshape(rows, 4, cols), 0, 1).reshape(4, 2, rows // 2, cols)

    gs = [g_win_b.reshape(4, 2, D_MODEL // 2, cw), by_chip(g_wuq_f), by_chip(g_wukv_f), g_wout.reshape(4, 2, 128, D_MODEL),
          g_wsp.reshape(4, 2, CHUNK, CHUNK)]
    r_in, r_uq, r_ukv, r_out, r_wsp, r_vec = _grad_reduce(gs, gvec)

    g_big = [r_in.reshape(w_in.shape), r_uq.reshape(w_uq.shape), r_ukv.reshape(w_ukv.shape),
             r_out.reshape(w_out.shape), r_wsp.reshape(w_spatial.shape)]
    small = lambda qg, kvg, sg, sb, lng, lnb, bs: [qg.reshape(1, -1), kvg.reshape(1, -1), sg.reshape(1, -1),
                                                   sb.reshape(1, -1), lng.reshape(1, -1), lnb.reshape(1, -1), bs]
    res = _adamw(g_big, [w_in, w_uq, w_ukv, w_out, w_spatial], [m_w_in, m_w_uq, m_w_ukv, m_w_out, m_w_spatial],
                 [v_w_in, v_w_uq, v_w_ukv, v_w_out, v_w_spatial], r_vec,
                 small(q_norm_g, kv_norm_g, sgu_norm_g, sgu_norm_b, ln_g, ln_b, b_spatial),
                 small(m_q_norm_g, m_kv_norm_g, m_sgu_norm_g, m_sgu_norm_b, m_ln_g, m_ln_b, m_b_spatial),
                 small(v_q_norm_g, v_kv_norm_g, v_sgu_norm_g, v_sgu_norm_b, v_ln_g, v_ln_b, v_b_spatial))

    def ordered(big, sm):
        vec = lambda n: sm[n].reshape(-1)
        return [big[0], vec(0), big[1], vec(1), big[2], vec(2), vec(3), big[4], sm[6], big[3], vec(4), vec(5)]

    loss = r_vec[GV_LOSS, 0]
    return (loss, gx.reshape(1, seq, D_MODEL), *ordered(res[0:5], res[20:27]), *ordered(res[5:10], res[27:34]),
            *ordered(res[10:15], res[34:41]), *ordered(res[15:20], res[41:48]))
```

```python
import math

import jax
import jax.numpy as jnp
import numpy as np
from jax import lax
from jax.experimental import pallas as pl
from jax.experimental.pallas import tpu as pltpu

F32 = jnp.float32
BF16 = jnp.bfloat16

D_MODEL = 1024
Q_LORA = 256
KV_LORA = 128
HEADS = 8
NOPE = 64
ROPE = 32
VDIM = 64
MLA_W = HEADS * VDIM
GW = 512
CHUNK = 128
HP = 128
PAIRS = HEADS // 2
D_IN = 2464
D_INR = 2560
C_CKV = Q_LORA
C_KR = Q_LORA + KV_LORA
C_GATE = C_KR + HP
ROPE_THETA = 10000.0
DN_ALPHA = 2.0 ** 0.25
EPS = 1e-5
SCALE = 1.0 / math.sqrt(NOPE + ROPE)
SCALE_LOG2E = SCALE * 1.4426950408889634
INV_SQRT2 = 0.7071067811865476
INV_SQRT_2PI = 0.3989422804014327

ADAM_LR = 0.001
ADAM_B1 = 0.9
ADAM_B2 = 0.999
ADAM_EPS = 1e-08
ADAM_WD = 0.01
ADAM_STEP = 10

PV_QG, PV_KVG, PV_SG, PV_SB, PV_INVF, PV_M1, PV_M2, PV_LNG, PV_LNB = range(9)
PV_ROWS = 16
GV_QG, GV_KVG, GV_SG, GV_SB, GV_LNG, GV_LNB, GV_LOSS = range(7)
GV_BSP = 8
GV_ROWS = 16

MESH = pl.DeviceIdType.MESH

FWD_TILE = 1024
POST_TILE = 512
BWD_TILE = 512
ATT_BLK = 512
ADAM_STEPS = 4
P_SPARE = 8
VMEM_LIMIT = 60 * 1024 * 1024


def _dot(a, b):
    return jnp.dot(a, b, preferred_element_type=F32)


def _dot_nt(a, b):
    return lax.dot_general(a, b, (((1,), (1,)), ((), ())), preferred_element_type=F32)


def _dot_tn(a, b):
    return lax.dot_general(a, b, (((0,), (0,)), ((), ())), preferred_element_type=F32)


def _sigmoid(z):
    return pl.reciprocal(1.0 + jnp.exp(-z), approx=True)


def _gelu_and_grad(x):
    cdf = 0.5 * (1.0 + lax.erf(x * INV_SQRT2))
    return x * cdf, cdf + x * (INV_SQRT_2PI * jnp.exp(-0.5 * x * x))


def _rms_stats(x):
    r = lax.rsqrt(jnp.mean(x * x, axis=-1, keepdims=True) + EPS)
    return x * r, r


def _rms_bwd(dy, g, xh, r):
    dyg = dy * g
    return r * (dyg - xh * jnp.mean(dyg * xh, axis=-1, keepdims=True))


def _ln_stats(x):
    mu = jnp.mean(x, axis=-1, keepdims=True)
    xc = x - mu
    r = lax.rsqrt(jnp.mean(xc * xc, axis=-1, keepdims=True) + EPS)
    return xc * r, r


def _ln_bwd(dy, g, xh, r):
    dxh = dy * g
    return r * (dxh - jnp.mean(dxh, axis=-1, keepdims=True) - xh * jnp.mean(dxh * xh, axis=-1, keepdims=True))


def _rope_fwd(t, c, s1, s2):
    return t * c + pltpu.roll(t, HP - 16, 1) * s1 + pltpu.roll(t, 16, 1) * s2


def _rope_bwd(d, c, s1, s2):
    return d * c + pltpu.roll(d * s1, 16, 1) + pltpu.roll(d * s2, HP - 16, 1)


def _lane_lt64(shape):
    return lax.broadcasted_iota(jnp.int32, shape, len(shape) - 1) < 64


def _spatial_mix(w_ref, src, dst_ref, rows):
    for c in range(rows // CHUNK):
        for p in range(PAIRS):
            blk = src[c * CHUNK:(c + 1) * CHUNK, p * HP:(p + 1) * HP]
            a = _dot(w_ref[2 * p], blk)
            b = _dot(w_ref[2 * p + 1], blk)
            dst_ref[c * CHUNK:(c + 1) * CHUNK, p * HP:(p + 1) * HP] = jnp.where(_lane_lt64(a.shape), a, b)


def _gmlp_fwd(u_pre, v_pre, zb, sg, sb, wt_ref, bsp_ref, sv_ref, rows):
    u, du = _gelu_and_grad(u_pre)
    gv, dgv = _gelu_and_grad(v_pre)
    xh, r = _ln_stats(gv)
    vln = (xh * sg + sb).astype(BF16)
    _spatial_mix(wt_ref, vln, sv_ref, rows)
    bias = bsp_ref[...]
    svb = sv_ref[...] + jnp.concatenate([bias] * (rows // CHUNK), axis=0)
    sig = _sigmoid(zb)
    return u, du, dgv, xh, r, vln, svb, sig


def _weight_gather(shards):
    n_arr = len(shards)

    def body(*refs):
        ins, outs = refs[0:n_arr], refs[n_arr:2 * n_arr]
        send_sems, recv_sems = refs[2 * n_arr:]
        x, y, c = lax.axis_index("x"), lax.axis_index("y"), lax.axis_index("c")
        j = 2 * x + y
        sib = (x, y, 1 - c)
        chips = [(1 - x, y), (x, 1 - y), (1 - x, 1 - y)]
        for n in range(n_arr):
            outs[n][j] = ins[n][...].astype(BF16)

        def half(n, blk, core):
            r = shards[n].shape[0] // 2
            return outs[n].at[blk, pl.ds(pl.multiple_of(core * r, 16), r), :]

        def copy(k, ref, to):
            return pltpu.make_async_remote_copy(
                src_ref=ref, dst_ref=ref, send_sem=send_sems.at[k], recv_sem=recv_sems.at[k],
                device_id=to, device_id_type=MESH)

        first = [copy(6 * n + kk, half(n, j, c), (px, py, c))
                 for n in range(n_arr) for kk, (px, py) in enumerate(chips)]
        for cp in first:
            cp.start()
        passed = []
        for n in range(n_arr):
            for kk, (px, py) in enumerate(chips):
                landed = half(n, 2 * px + py, c)
                copy(6 * n + kk, landed, (px, py, c)).wait_recv()
                passed.append(copy(6 * n + 3 + kk, landed, sib))
                passed[-1].start()
        for n in range(n_arr):
            for kk, (px, py) in enumerate(chips):
                copy(6 * n + 3 + kk, half(n, 2 * px + py, 1 - c), sib).wait_recv()
        for cp in first + passed:
            cp.wait_send()

    vmem = pl.BlockSpec(memory_space=pltpu.VMEM)
    return pl.pallas_call(
        body, name="weight_gather",
        out_shape=[jax.ShapeDtypeStruct((4,) + a.shape, BF16) for a in shards],
        in_specs=[vmem] * n_arr, out_specs=[vmem] * n_arr,
        scratch_shapes=[pltpu.SemaphoreType.DMA((6 * n_arr,)), pltpu.SemaphoreType.DMA((6 * n_arr,))],
        compiler_params=pltpu.CompilerParams(vmem_limit_bytes=VMEM_LIMIT),
    )(*shards)


def _fwd_pre(x, pos, win, wuq, wkv, pvec):
    seq = x.shape[0]
    t = FWD_TILE

    def body(x_ref, pos_ref, win_ref, wuq_ref, wkv_ref, pv_ref,
             cq_o, ckv_o, gate_o, q_o, k_o, v_o, vt_o, cs_o):
        xb = x_ref[...].astype(BF16)
        proj = _dot(xb, win_ref[:, 0:C_GATE])
        cq = proj[:, 0:C_CKV]
        ckv = proj[:, C_CKV:C_KR]
        kr = proj[:, C_KR:C_GATE]
        cq_o[...] = cq
        ckv_o[...] = ckv

        ang = pos_ref[...].astype(F32) * pv_ref[PV_INVF:PV_INVF + 1, 0:HP]
        cos = jnp.cos(ang)
        sin = jnp.sin(ang)
        cs_o[:, 0:HP] = cos
        cs_o[:, HP:2 * HP] = sin
        s1 = sin * pv_ref[PV_M1:PV_M1 + 1, 0:HP]
        s2 = sin * pv_ref[PV_M2:PV_M2 + 1, 0:HP]

        cqh, _ = _rms_stats(cq)
        q_all = _dot((cqh * pv_ref[PV_QG:PV_QG + 1, 0:Q_LORA]).astype(BF16), wuq_ref[...])
        ckvh, _ = _rms_stats(ckv)
        kv_all = _dot((ckvh * pv_ref[PV_KVG:PV_KVG + 1, 0:KV_LORA]).astype(BF16), wkv_ref[...])
        krr = _rope_fwd(kr, cos, s1, s2)
        for h in range(HEADS):
            sl = slice(h * HP, (h + 1) * HP)
            q_o[:, sl] = (_rope_fwd(q_all[:, sl], cos, s1, s2) * SCALE_LOG2E).astype(BF16)
            k_o[:, sl] = (kv_all[:, sl] + krr).astype(BF16)
        val = kv_all[:, HEADS * HP:].astype(BF16)
        v_o[...] = val
        for blk in range(t // ATT_BLK):
            vt_o[blk] = val[blk * ATT_BLK:(blk + 1) * ATT_BLK, :].T
        gate_o[...] = _dot(xb, win_ref[:, C_GATE:D_INR]).astype(BF16)

    tile = lambda w: pl.BlockSpec((t, w), lambda i: (i, 0))
    full = lambda a: pl.BlockSpec(a.shape, lambda i: (0,) * a.ndim)
    outs = [(Q_LORA, F32), (KV_LORA, F32), (2048, BF16), (HEADS * HP, BF16), (HEADS * HP, BF16), (MLA_W, BF16)]
    assert t % ATT_BLK == 0
    out_specs = [tile(w) for w, _ in outs]
    out_specs += [pl.BlockSpec((t // ATT_BLK, MLA_W, ATT_BLK), lambda i: (i, 0, 0)), tile(2 * HP)]
    out_shape = [jax.ShapeDtypeStruct((seq, w), d) for w, d in outs]
    out_shape += [jax.ShapeDtypeStruct((seq // ATT_BLK, MLA_W, ATT_BLK), BF16), jax.ShapeDtypeStruct((seq, 2 * HP), F32)]
    return pl.pallas_call(
        body, name="fwd_pre", grid=(seq // t,),
        in_specs=[tile(D_MODEL), tile(1), full(win), full(wuq), full(wkv), full(pvec)],
        out_specs=out_specs, out_shape=out_shape,
        compiler_params=pltpu.CompilerParams(dimension_semantics=("arbitrary",), vmem_limit_bytes=VMEM_LIMIT),
    )(x, pos, win, wuq, wkv, pvec)


def _attn_fwd(q, k, vt):
    seq = q.shape[0]
    b = ATT_BLK
    nq = seq // b
    assert nq % 2 == 0
    n_wide = sum(i // 2 for i in range(nq))
    n_blk = nq * (nq + 1) // 2

    def body(q_ref, k_ref, vt_ref, o_o, lse_o, mt_o, p_hbm, m_ref, l_ref, acc_ref, s_even, s_odd, stage, sems):
        m_ref[...] = jnp.full(m_ref.shape, -jnp.inf, F32)
        l_ref[...] = jnp.zeros(l_ref.shape, F32)
        acc_ref[...] = jnp.zeros(acc_ref.shape, F32)
        stage[...] = jnp.zeros(stage.shape, BF16)
        pair = pl.program_id(0)

        def p_copy(slot, t, nkb):
            return pltpu.make_async_copy(stage.at[slot, pl.ds(0, nkb)], p_hbm.at[pair, pl.ds(t, nkb)], sems.at[slot])

        def scores(i, j, s_ref, nkb):
            qrows = pl.ds(pl.multiple_of(i * b, b), b)
            krows = pl.ds(pl.multiple_of(j * b, b), nkb * b)
            for a in range(2):
                s_ref[a, 0:nkb * b, :] = _dot_nt(k_ref[krows, a * HP:(a + 1) * HP], q_ref[qrows, a * HP:(a + 1) * HP])

        def consume(i, j, s_ref, nkb, masked, slot):
            t = (i * (i + 1)) // 2 + j
            for a in range(2):
                st = s_ref[a, 0:nkb * b, :]
                if masked:
                    ki = lax.broadcasted_iota(jnp.int32, st.shape, 0)
                    qi = lax.broadcasted_iota(jnp.int32, st.shape, 1)
                    st = jnp.where(ki <= qi, st, -jnp.inf)
                m_prev = m_ref[i, a:a + 1, :]
                m_new = jnp.maximum(m_prev, jnp.max(st, axis=0, keepdims=True))
                alpha = jnp.exp2(m_prev - m_new)
                pt = jnp.exp2(st - m_new)
                ptb = pt.astype(BF16)
                l_ref[i, a:a + 1, :] = alpha * l_ref[i, a:a + 1, :] + jnp.sum(pt, axis=0, keepdims=True)
                pv = _dot(vt_ref[j], ptb[0:b, :])
                for kb in range(1, nkb):
                    pv = pv + _dot(vt_ref[j + kb], ptb[kb * b:(kb + 1) * b, :])
                acc_ref[i, a] = alpha * acc_ref[i, a] + pv
                m_ref[i, a:a + 1, :] = m_new
                for kb in range(nkb):
                    stage[slot, kb, a] = ptb[kb * b:(kb + 1) * b, :]
                    mt_o[t + kb, a:a + 1, :] = m_new
            return t

        def run(count, first, following, nkb, masked):
            if count == 0:
                return
            for slot in range(4):
                p_copy(slot, n_blk + slot * nkb, nkb).start()
            scores(*first, s_even, nkb)

            def two(u, ij):
                g = lax.rem(u, 2)
                p_copy(2 * g, 0, nkb).wait()
                p_copy(2 * g + 1, 0, nkb).wait()
                nxt = following(*ij)
                scores(*nxt, s_odd, nkb)
                t0 = consume(*ij, s_even, nkb, masked, 2 * g)
                nxt2 = following(*nxt)
                scores(*nxt2, s_even, nkb)
                t1 = consume(*nxt, s_odd, nkb, masked, 2 * g + 1)
                p_copy(2 * g, t0, nkb).start()
                p_copy(2 * g + 1, t1, nkb).start()
                return nxt2

            last = lax.fori_loop(0, count // 2, two, tuple(jnp.int32(c) for c in first))
            if count % 2:
                slot = 2 * ((count // 2) % 2)
                p_copy(slot, 0, nkb).wait()
                p_copy(slot, consume(*last, s_even, nkb, masked, slot), nkb).start()
            for slot in range(4):
                p_copy(slot, 0, nkb).wait()

        def clamp(i):
            return jnp.minimum(i, nq - 1)

        def next_wide(i, j):
            wrap = j + 4 > i
            return clamp(jnp.where(wrap, i + 1, i)), jnp.where(wrap, 0, j + 2)

        run(n_wide, (2, 0), next_wide, 2, False)
        run(nq // 2, (1, 0), lambda i, j: (clamp(i + 2), jnp.minimum(j + 2, nq - 2)), 1, False)
        run(nq, (0, 0), lambda i, j: (clamp(i + 1), clamp(j + 1)), 1, True)
        top = lax.broadcasted_iota(jnp.int32, (HP, b), 0) < 64

        def finish(i, carry):
            rows = pl.ds(pl.multiple_of(i * b, b), b)
            o_o[rows, :] = jnp.where(top, acc_ref[i, 0] / l_ref[i, 0:1, :], acc_ref[i, 1] / l_ref[i, 1:2, :]).T
            lse_o[i] = m_ref[i, 0:2, :] + jnp.log2(l_ref[i, 0:2, :])
            return carry

        lax.fori_loop(0, nq, finish, 0)

    return pl.pallas_call(
        body, name="attn_fwd", grid=(PAIRS,),
        in_specs=[pl.BlockSpec((seq, 2 * HP), lambda p: (0, p)),
                  pl.BlockSpec((seq, 2 * HP), lambda p: (0, p)),
                  pl.BlockSpec((nq, HP, b), lambda p: (0, p, 0))],
        out_specs=[pl.BlockSpec((seq, HP), lambda p: (0, p)),
                   pl.BlockSpec((None, nq, 2, b), lambda p: (p, 0, 0, 0)),
                   pl.BlockSpec((None, n_blk, 2, b), lambda p: (p, 0, 0, 0)),
                   pl.BlockSpec(memory_space=pl.ANY)],
        out_shape=[jax.ShapeDtypeStruct((seq, MLA_W), F32),
                   jax.ShapeDtypeStruct((PAIRS, nq, 2, b), F32),
                   jax.ShapeDtypeStruct((PAIRS, n_blk, 2, b), F32),
                   jax.ShapeDtypeStruct((PAIRS, n_blk + P_SPARE, 2, b, b), BF16)],
        scratch_shapes=[pltpu.VMEM((nq, 8, b), F32), pltpu.VMEM((nq, 8, b), F32), pltpu.VMEM((nq, 2, HP, b), F32),
                        pltpu.VMEM((2, 2 * b, b), F32), pltpu.VMEM((2, 2 * b, b), F32),
                        pltpu.VMEM((4, 2, 2, b, b), BF16), pltpu.SemaphoreType.DMA((4,))],
        compiler_params=pltpu.CompilerParams(dimension_semantics=("arbitrary",), vmem_limit_bytes=VMEM_LIMIT),
    )(q, k, vt)


def _post(x, tgt, o, gate, wout, pvec, wt, wtt, bsp):
    seq = x.shape[0]
    t = POST_TILE
    nt = seq // t

    def body(x_ref, tgt_ref, o_ref, gate_ref, wout_ref, pv_ref, wt_ref, wtt_ref, bsp_ref,
             dh2_o, do_o, dgate_o, gwout_o, gwsp_o, vec_o, sv_ref, dvln_ref, bacc_ref):
        i = pl.program_id(0)

        @pl.when(i == 0)
        def _():
            gwout_o[...] = jnp.zeros_like(gwout_o)
            gwsp_o[...] = jnp.zeros_like(gwsp_o)
            vec_o[...] = jnp.zeros_like(vec_o)
            bacc_ref[...] = jnp.zeros_like(bacc_ref)

        za = gate_ref[:, 0:512].astype(F32)
        u_pre = gate_ref[:, 512:1024].astype(F32)
        v_pre = gate_ref[:, 1024:1536].astype(F32)
        zb = gate_ref[:, 1536:2048].astype(F32)
        sg = pv_ref[PV_SG:PV_SG + 1, 0:GW]
        sb = pv_ref[PV_SB:PV_SB + 1, 0:GW]
        lng = pv_ref[PV_LNG:PV_LNG + 1, :]
        lnb = pv_ref[PV_LNB:PV_LNB + 1, :]
        o = o_ref[...]

        sig_a = _sigmoid(za)
        silu_a = za * sig_a
        u, du, dgv, xh, r, vln, svb, sig_b = _gmlp_fwd(u_pre, v_pre, zb, sg, sb, wt_ref, bsp_ref, sv_ref, t)
        silu_b = zb * sig_b
        sgu = u * svb
        merged = jnp.concatenate([o * silu_a, sgu * silu_b], axis=1).astype(BF16)
        h2 = DN_ALPHA * x_ref[...] + _dot(merged, wout_ref[...])
        xh2, r2 = _ln_stats(h2)
        err = xh2 * lng + lnb - tgt_ref[...]
        d_out = err * (1.0 / D_MODEL)
        vec_o[GV_LNG:GV_LNG + 1, :] += jnp.sum(d_out * xh2, axis=0, keepdims=True)
        vec_o[GV_LNB:GV_LNB + 1, :] += jnp.sum(d_out, axis=0, keepdims=True)
        vec_o[GV_LOSS:GV_LOSS + 1, :] += jnp.sum(err * err, axis=0, keepdims=True) * (0.5 / D_MODEL)

        d_h2 = _ln_bwd(d_out, lng, xh2, r2)
        dh2_o[...] = d_h2
        dh2b = d_h2.astype(BF16)
        gwout_o[...] += _dot_tn(merged, dh2b)
        d_m = _dot_nt(dh2b, wout_ref[...])
        d_oa = d_m[:, 0:512]
        d_ob = d_m[:, 512:1024]
        do_o[...] = (d_oa * silu_a).astype(BF16)
        dgate_o[:, 0:512] = (d_oa * o * (sig_a * (1.0 + za * (1.0 - sig_a)))).astype(BF16)
        dgate_o[:, 1536:2048] = (d_ob * sgu * (sig_b * (1.0 + zb * (1.0 - sig_b)))).astype(BF16)
        d_sgu = d_ob * silu_b
        dgate_o[:, 512:1024] = (d_sgu * svb * du).astype(BF16)
        d_sv = d_sgu * u
        acc = bacc_ref[...]
        for c in range(t // CHUNK):
            acc = acc + d_sv[c * CHUNK:(c + 1) * CHUNK, :]
        bacc_ref[...] = acc
        d_svb = d_sv.astype(BF16)
        for c in range(t // CHUNK):
            for p in range(PAIRS):
                blk = d_svb[c * CHUNK:(c + 1) * CHUNK, p * HP:(p + 1) * HP]
                vblk = vln[c * CHUNK:(c + 1) * CHUNK, p * HP:(p + 1) * HP]
                first = _lane_lt64(blk.shape)
                gwsp_o[2 * p] += _dot_nt(jnp.where(first, blk, jnp.zeros_like(blk)), vblk)
                gwsp_o[2 * p + 1] += _dot_nt(jnp.where(first, jnp.zeros_like(blk), blk), vblk)
        _spatial_mix(wtt_ref, d_svb, dvln_ref, t)
        d_vln = dvln_ref[...]
        vec_o[GV_SG:GV_SG + 1, 0:GW] += jnp.sum(d_vln * xh, axis=0, keepdims=True)
        vec_o[GV_SB:GV_SB + 1, 0:GW] += jnp.sum(d_vln, axis=0, keepdims=True)
        dgate_o[:, 1024:1536] = (_ln_bwd(d_vln, sg, xh, r) * dgv).astype(BF16)


        @pl.when(i == nt - 1)
        def _():
            tri = (lax.broadcasted_iota(jnp.int32, (CHUNK, CHUNK), 1)
                   <= lax.broadcasted_iota(jnp.int32, (CHUNK, CHUNK), 0))
            for h in range(HEADS):
                gwsp_o[h] = jnp.where(tri, gwsp_o[h], 0.0)
            lane = lax.broadcasted_iota(jnp.int32, (CHUNK, HP), 1)
            res = jnp.zeros((CHUNK, HP), F32)
            for h in range(HEADS):
                p, a = divmod(h, 2)
                blk = bacc_ref[:, p * HP:(p + 1) * HP]
                part = jnp.where(_lane_lt64(blk.shape) == (a == 0), blk, 0.0)
                res = jnp.where(lane == h, jnp.sum(part, axis=-1, keepdims=True), res)
            vec_o[GV_BSP:GV_BSP + HEADS, 0:HP] = res.T[0:HEADS, :]
            lane1 = lax.broadcasted_iota(jnp.int32, (1, D_MODEL), 1)
            total = jnp.sum(vec_o[GV_LOSS:GV_LOSS + 1, :], axis=-1, keepdims=True)
            vec_o[GV_LOSS:GV_LOSS + 1, :] = jnp.where(lane1 == 0, total, 0.0)

    tile = lambda w: pl.BlockSpec((t, w), lambda i: (i, 0))
    full = lambda a: pl.BlockSpec(a.shape, lambda i: (0,) * a.ndim)
    const = lambda s: pl.BlockSpec(s, lambda i: (0,) * len(s))
    return pl.pallas_call(
        body, name="post", grid=(nt,),
        in_specs=[tile(D_MODEL), tile(D_MODEL), tile(MLA_W), tile(2048), full(wout), full(pvec),
                  full(wt), full(wtt), full(bsp)],
        out_specs=[tile(D_MODEL), tile(MLA_W), tile(2048), const((D_MODEL, D_MODEL)),
                   const((HEADS, CHUNK, CHUNK)), const((GV_ROWS, D_MODEL))],
        out_shape=[jax.ShapeDtypeStruct((seq, D_MODEL), F32), jax.ShapeDtypeStruct((seq, MLA_W), BF16),
                   jax.ShapeDtypeStruct((seq, 2048), BF16), jax.ShapeDtypeStruct((D_MODEL, D_MODEL), F32),
                   jax.ShapeDtypeStruct((HEADS, CHUNK, CHUNK), F32), jax.ShapeDtypeStruct((GV_ROWS, D_MODEL), F32)],
        scratch_shapes=[pltpu.VMEM((t, GW), F32), pltpu.VMEM((t, GW), F32), pltpu.VMEM((CHUNK, GW), F32)],
        compiler_params=pltpu.CompilerParams(dimension_semantics=("arbitrary",), vmem_limit_bytes=VMEM_LIMIT),
    )(x, tgt, o, gate, wout, pvec, wt, wtt, bsp)


def _attn_bwd(q, k, v, do, o, lse, cs, pvec, mt, p_all):
    seq = q.shape[0]
    b = ATT_BLK
    nq = seq // b

    def body(q_ref, k_ref, v_ref, do_ref, o_ref, lse_ref, cs_ref, pv_ref, mt_ref, p_hbm, dq_o, dk_o, dv_o, dk_acc, dv_acc,
             pbuf, sems):
        pair = pl.program_id(0)
        i = pl.program_id(1)
        base = (i * (i + 1)) // 2

        def p_copy(c):
            slot = lax.rem(c, 2)
            return pltpu.make_async_copy(p_hbm.at[pair, pl.ds(base + 4 * c, 4)], pbuf.at[slot], sems.at[slot])

        p_copy(0).start()

        @pl.when(i == 0)
        def _():
            dk_acc[...] = jnp.zeros_like(dk_acc)
            dv_acc[...] = jnp.zeros_like(dv_acc)

        first = _lane_lt64((b, HP))
        do = do_ref[...]
        zero = jnp.zeros_like(do)
        dos = [jnp.where(first, do, zero), jnp.where(first, zero, do)]
        prod_t = (do.astype(F32) * o_ref[...]).T
        deltas = [jnp.sum(prod_t[0:64, :], axis=0, keepdims=True),
                  jnp.sum(prod_t[64:128, :], axis=0, keepdims=True)]
        lses = [lse_ref[0:1, :], lse_ref[1:2, :]]
        qs = [q_ref[:, a * HP:(a + 1) * HP] for a in range(2)]

        def step(c, dqs, nkb):
            slot = lax.rem(c, 2)
            j = 4 * c
            rows = pl.ds(pl.multiple_of(j * b, b), nkb * b)
            vb = v_ref[rows, :]
            new_dq = []
            dvs = []
            for a in range(2):
                kb = k_ref[rows, a * HP:(a + 1) * HP]
                pt = jnp.concatenate(
                    [pbuf[slot, n, a].astype(F32) * jnp.exp2(mt_ref[base + j + n, a:a + 1, :] - lses[a]) for n in range(nkb)],
                    axis=0)
                dvs.append(_dot(pt.astype(BF16), do))
                dpt = _dot_nt(vb, dos[a])
                dst = (pt * (dpt - deltas[a])).astype(BF16)
                dk_acc[rows, a * HP:(a + 1) * HP] += _dot(dst, qs[a])
                new_dq.append(dqs[a] + _dot_tn(dst, kb))
            dv_acc[rows, :] += jnp.where(_lane_lt64((nkb * b, HP)), dvs[0], dvs[1])
            return tuple(new_dq)

        init = (jnp.zeros((b, HP), F32), jnp.zeros((b, HP), F32))
        n_wide = i // 4

        def wide(c, cr):
            p_copy(c).wait()
            p_copy(c + 1).start()
            return step(c, cr, 4)

        dqs = lax.fori_loop(0, n_wide, wide, init)
        p_copy(n_wide).wait()
        last = [lambda cr, w=w: step(n_wide, cr, w) for w in (1, 2, 3, 4)]
        dqs = lax.switch(i % 4, last, dqs)
        cos = cs_ref[:, 0:HP]
        sin = cs_ref[:, HP:2 * HP]
        s1 = sin * pv_ref[PV_M1:PV_M1 + 1, 0:HP]
        s2 = sin * pv_ref[PV_M2:PV_M2 + 1, 0:HP]
        for a in range(2):
            dq_o[:, a * HP:(a + 1) * HP] = _rope_bwd(dqs[a] * SCALE, cos, s1, s2).astype(BF16)

        @pl.when(i == nq - 1)
        def _():
            dk_o[...] = (dk_acc[...] * (SCALE / SCALE_LOG2E)).astype(BF16)
            dv_o[...] = dv_acc[...].astype(BF16)

    return pl.pallas_call(
        body, name="attn_bwd", grid=(PAIRS, nq),
        in_specs=[pl.BlockSpec((b, 2 * HP), lambda p, i: (i, p)),
                  pl.BlockSpec((seq, 2 * HP), lambda p, i: (0, p)),
                  pl.BlockSpec((seq, HP), lambda p, i: (0, p)),
                  pl.BlockSpec((b, HP), lambda p, i: (i, p)),
                  pl.BlockSpec((b, HP), lambda p, i: (i, p)),
                  pl.BlockSpec((None, None, 2, b), lambda p, i: (p, i, 0, 0)),
                  pl.BlockSpec((b, 2 * HP), lambda p, i: (i, 0)),
                  pl.BlockSpec(pvec.shape, lambda p, i: (0, 0)),
                  pl.BlockSpec((None,) + mt.shape[1:], lambda p, i: (p, 0, 0, 0)),
                  pl.BlockSpec(memory_space=pl.ANY)],
        out_specs=[pl.BlockSpec((b, 2 * HP), lambda p, i: (i, p)),
                   pl.BlockSpec((seq, 2 * HP), lambda p, i: (0, p)),
                   pl.BlockSpec((seq, HP), lambda p, i: (0, p))],
        out_shape=[jax.ShapeDtypeStruct((seq, HEADS * HP), BF16),
                   jax.ShapeDtypeStruct((seq, HEADS * HP), BF16),
                   jax.ShapeDtypeStruct((seq, MLA_W), BF16)],
        scratch_shapes=[pltpu.VMEM((seq, 2 * HP), F32), pltpu.VMEM((seq, HP), F32), pltpu.VMEM((2, 4, 2, b, b), BF16),
                        pltpu.SemaphoreType.DMA((2,))],
        compiler_params=pltpu.CompilerParams(dimension_semantics=("arbitrary", "arbitrary"),
                                             vmem_limit_bytes=VMEM_LIMIT),
    )(q, k, v, do, o, lse, cs, pvec, mt, p_all)


def _bwd_pre(x, dh2, cq, ckv, cs, dq, dk, dv, dgate, win, wuq, wkv, pvec, gvec):
    seq = x.shape[0]
    t = BWD_TILE

    def body(x_ref, dh2_ref, cq_ref, ckv_ref, cs_ref, dq_ref, dk_ref, dv_ref, dgate_ref,
             win_ref, wuq_ref, wkv_ref, pv_ref, gv_ref, gx_o, gwin_o, gwuq_o, gwkv_o, vec_o):
        i = pl.program_id(0)

        @pl.when(i == 0)
        def _():
            gwin_o[...] = jnp.zeros_like(gwin_o)
            gwuq_o[...] = jnp.zeros_like(gwuq_o)
            gwkv_o[...] = jnp.zeros_like(gwkv_o)
            vec_o[...] = gv_ref[...]

        xb = x_ref[...].astype(BF16)
        dgate = dgate_ref[...]
        gwin_o[:, C_GATE:D_INR] += _dot_tn(xb, dgate)
        gx_gate = _dot_nt(dgate, win_ref[:, C_GATE:D_INR])

        qg = pv_ref[PV_QG:PV_QG + 1, 0:Q_LORA]
        kvg = pv_ref[PV_KVG:PV_KVG + 1, 0:KV_LORA]
        dq = dq_ref[...]
        cqh, rq = _rms_stats(cq_ref[...])
        d_cqn = _dot_nt(dq, wuq_ref[...])
        gwuq_o[...] += _dot_tn((cqh * qg).astype(BF16), dq)
        vec_o[GV_QG:GV_QG + 1, 0:Q_LORA] += jnp.sum(d_cqn * cqh, axis=0, keepdims=True)
        d_cq = _rms_bwd(d_cqn, qg, cqh, rq)

        dk = dk_ref[...]
        dkv = jnp.concatenate([dk, dv_ref[...]], axis=1)
        ckvh, rkv = _rms_stats(ckv_ref[...])
        d_ckvn = _dot_nt(dkv, wkv_ref[...])
        gwkv_o[...] += _dot_tn((ckvh * kvg).astype(BF16), dkv)
        vec_o[GV_KVG:GV_KVG + 1, 0:KV_LORA] += jnp.sum(d_ckvn * ckvh, axis=0, keepdims=True)
        d_ckv = _rms_bwd(d_ckvn, kvg, ckvh, rkv)

        dks = dk[:, 0:HP].astype(F32)
        for h in range(1, HEADS):
            dks = dks + dk[:, h * HP:(h + 1) * HP].astype(F32)
        cos = cs_ref[:, 0:HP]
        sin = cs_ref[:, HP:2 * HP]
        d_kr = _rope_bwd(dks, cos, sin * pv_ref[PV_M1:PV_M1 + 1, 0:HP], sin * pv_ref[PV_M2:PV_M2 + 1, 0:HP])

        d_lat = jnp.concatenate([d_cq.astype(BF16), d_ckv.astype(BF16), d_kr.astype(BF16)], axis=1)
        gwin_o[:, 0:C_GATE] += _dot_tn(xb, d_lat)
        gx_o[...] = DN_ALPHA * dh2_ref[...] + gx_gate + _dot_nt(d_lat, win_ref[:, 0:C_GATE])

    tile = lambda w: pl.BlockSpec((t, w), lambda i: (i, 0))
    full = lambda a: pl.BlockSpec(a.shape, lambda i: (0,) * a.ndim)
    const = lambda s: pl.BlockSpec(s, lambda i: (0,) * len(s))
    return pl.pallas_call(
        body, name="bwd_pre", grid=(seq // t,),
        in_specs=[tile(D_MODEL), tile(D_MODEL), tile(Q_LORA), tile(KV_LORA), tile(2 * HP), tile(HEADS * HP),
                  tile(HEADS * HP), tile(MLA_W), tile(2048), full(win), full(wuq), full(wkv), full(pvec), full(gvec)],
        out_specs=[tile(D_MODEL), const((D_MODEL, D_INR)), const((Q_LORA, HEADS * HP)),
                   const((KV_LORA, HEADS * HP + MLA_W)), const((GV_ROWS, D_MODEL))],
        out_shape=[jax.ShapeDtypeStruct((seq, D_MODEL), F32), jax.ShapeDtypeStruct((D_MODEL, D_INR), F32),
                   jax.ShapeDtypeStruct((Q_LORA, HEADS * HP), F32),
                   jax.ShapeDtypeStruct((KV_LORA, HEADS * HP + MLA_W), F32),
                   jax.ShapeDtypeStruct((GV_ROWS, D_MODEL), F32)],
        compiler_params=pltpu.CompilerParams(dimension_semantics=("arbitrary",), vmem_limit_bytes=VMEM_LIMIT),
    )(x, dh2, cq, ckv, cs, dq, dk, dv, dgate, win, wuq, wkv, pvec, gvec)


def _grad_reduce(gs, gvec):
    n_arr = len(gs)
    n_big = n_arr - 1
    k1 = lambda n, blk: 4 * n + blk
    k2 = lambda n, kk: 4 * n_arr + 3 * n + kk
    k3 = lambda n: 7 * n_arr + n
    k3w = lambda k: 7 * n_arr + n_big + k
    kv = lambda k: 7 * n_arr + n_big + 7 + k
    n_sem = 7 * n_arr + n_big + 14

    def body(*refs):
        g, gv = refs[0:n_arr], refs[n_arr]
        outs, ov = refs[n_arr + 1:2 * n_arr + 1], refs[2 * n_arr + 1]
        r1 = refs[2 * n_arr + 2:3 * n_arr + 2]
        r2 = refs[3 * n_arr + 2:4 * n_arr + 2]
        s2 = refs[4 * n_arr + 2:5 * n_arr + 2]
        vbuf, send_sems, recv_sems = refs[5 * n_arr + 2:]
        x, y, c = lax.axis_index("x"), lax.axis_index("y"), lax.axis_index("c")
        j = 2 * x + y
        me = 2 * j + c
        sib = (x, y, 1 - c)
        chips = [(1 - x, y), (x, 1 - y), (1 - x, 1 - y)]
        others = [sib] + [(px, py, pc) for (px, py) in chips for pc in (c, 1 - c)]

        def copy(k, src, dst, to):
            return pltpu.make_async_remote_copy(
                src_ref=src, dst_ref=dst, send_sem=send_sems.at[k], recv_sem=recv_sems.at[k],
                device_id=to, device_id_type=MESH)

        l1 = [copy(k1(n, blk), g[n].at[blk, 1 - c], r1[n].at[blk], sib) for n in range(n_arr) for blk in range(4)]
        lv = [copy(kv(k), gv, vbuf.at[me], to) for k, to in enumerate(others)]
        for cp in l1 + lv:
            cp.start()
        l2 = []
        for n in range(n_arr):
            for blk in range(4):
                copy(k1(n, blk), g[n].at[blk, c], r1[n].at[blk], sib).wait_recv()
            for blk in range(4):
                r1[n][blk] = g[n][blk, c] + r1[n][blk]
                s2[n][blk] = r1[n][blk].astype(BF16)
            for kk, (px, py) in enumerate(chips):
                l2.append(copy(k2(n, kk), s2[n].at[2 * px + py], r2[n].at[kk], (px, py, c)))
                l2[-1].start()

        l3 = []
        for n in range(n_arr):
            for kk in range(3):
                copy(k2(n, kk), s2[n].at[0], r2[n].at[kk], sib).wait_recv()
            red = ((r1[n][j] + r2[n][0].astype(F32)) + r2[n][1].astype(F32)) + r2[n][2].astype(F32)
            if n < n_big:
                outs[n][c] = red
                back = [copy(k3(n), outs[n].at[c], outs[n].at[c], sib)]
            else:
                outs[n][j, c] = red
                back = [copy(k3w(k), outs[n].at[j, c], outs[n].at[j, c], to) for k, to in enumerate(others)]
            for cp in back:
                cp.start()
            l3 += back
        for n in range(n_big):
            copy(k3(n), outs[n].at[1 - c], outs[n].at[1 - c], sib).wait_recv()
        for k, (px, py, pc) in enumerate(others):
            landed = outs[n_big].at[2 * px + py, pc]
            copy(k3w(k), landed, landed, (px, py, pc)).wait_recv()
            copy(kv(k), gv, vbuf.at[4 * px + 2 * py + pc], (px, py, pc)).wait_recv()
        vbuf[me] = gv[...]
        total = vbuf[0]
        for d in range(1, 8):
            total = total + vbuf[d]
        ov[...] = total
        for cp in l1 + lv + l2 + l3:
            cp.wait_send()

    vmem = pl.BlockSpec(memory_space=pltpu.VMEM)
    half_shapes = [a.shape[2:] for a in gs]
    out_shape = [jax.ShapeDtypeStruct((2,) + s, F32) for s in half_shapes[:n_big]]
    out_shape += [jax.ShapeDtypeStruct((4, 2) + half_shapes[n_big], F32), jax.ShapeDtypeStruct(gvec.shape, F32)]
    scratch = [pltpu.VMEM((4,) + s, F32) for s in half_shapes] + [pltpu.VMEM((3,) + s, BF16) for s in half_shapes]
    scratch += [pltpu.VMEM((4,) + s, BF16) for s in half_shapes]
    scratch += [pltpu.VMEM((8,) + gvec.shape, F32), pltpu.SemaphoreType.DMA((n_sem,)), pltpu.SemaphoreType.DMA((n_sem,))]
    return pl.pallas_call(
        body, name="grad_reduce", out_shape=out_shape,
        in_specs=[vmem] * (n_arr + 1), out_specs=[vmem] * (n_arr + 1), scratch_shapes=scratch,
        compiler_params=pltpu.CompilerParams(vmem_limit_bytes=VMEM_LIMIT),
    )(*gs, gvec)


SMALL_ROWS = ((GV_QG, 1, Q_LORA), (GV_KVG, 1, KV_LORA), (GV_SG, 1, GW), (GV_SB, 1, GW),
              (GV_LNG, 1, D_MODEL), (GV_LNB, 1, D_MODEL), (GV_BSP, HEADS, CHUNK))


def _adam_update(g, w, m, v):
    m_new = ADAM_B1 * m + (1.0 - ADAM_B1) * g
    v_new = ADAM_B2 * v + (1.0 - ADAM_B2) * (g * g)
    m_hat = m_new / (1.0 - ADAM_B1 ** ADAM_STEP)
    v_hat = v_new / (1.0 - ADAM_B2 ** ADAM_STEP)
    return -ADAM_LR * (m_hat / (jnp.sqrt(v_hat) + ADAM_EPS) + ADAM_WD * w), m_new, v_new


def _adamw(g_big, w_big, m_big, v_big, gvec, w_small, m_small, v_small):
    nb, ns = len(g_big), len(w_small)

    def body(*refs):
        it = iter(refs)
        take = lambda n: [next(it) for _ in range(n)]
        g_b, w_b, m_b, v_b = take(nb), take(nb), take(nb), take(nb)
        gv = next(it)
        w_s, m_s, v_s = take(ns), take(ns), take(ns)
        g_bo, d_bo, m_bo, v_bo = take(nb), take(nb), take(nb), take(nb)
        g_so, d_so, m_so, v_so = take(ns), take(ns), take(ns), take(ns)
        for n in range(nb):
            gb = g_b[n][...]
            g_bo[n][...] = gb
            d_bo[n][...], m_bo[n][...], v_bo[n][...] = _adam_update(gb, w_b[n][...], m_b[n][...], v_b[n][...])
        for n, (row, nrow, width) in enumerate(SMALL_ROWS):
            gs = gv[row:row + nrow, 0:width]
            g_so[n][...] = gs
            d_so[n][...], m_so[n][...], v_so[n][...] = _adam_update(gs, w_s[n][...], m_s[n][...], v_s[n][...])

    def rows(a):
        nd = a.ndim
        return pl.BlockSpec((a.shape[0] // ADAM_STEPS,) + a.shape[1:], lambda i: (i,) + (0,) * (nd - 1))

    def whole(a):
        nd = a.ndim
        return pl.BlockSpec(a.shape, lambda i: (0,) * nd)

    big = [jax.ShapeDtypeStruct(a.shape, F32) for a in w_big]
    small = [jax.ShapeDtypeStruct(a.shape, F32) for a in w_small]
    return pl.pallas_call(
        body, name="adamw", grid=(ADAM_STEPS,), out_shape=big * 4 + small * 4,
        in_specs=[rows(a) for a in g_big + w_big + m_big + v_big] + [whole(gvec)]
        + [whole(a) for a in w_small + m_small + v_small],
        out_specs=[rows(a) for a in w_big] * 4 + [whole(a) for a in w_small] * 4,
        compiler_params=pltpu.CompilerParams(dimension_semantics=("arbitrary",), vmem_limit_bytes=VMEM_LIMIT),
    )(*g_big, *w_big, *m_big, *v_big, gvec, *w_small, *m_small, *v_small)


def kernel(x, positions, w_in, q_norm_g, w_uq, kv_norm_g, w_ukv, sgu_norm_g, sgu_norm_b, w_spatial, b_spatial, w_out, ln_g, ln_b, loss_target, m_w_in, m_q_norm_g, m_w_uq, m_kv_norm_g, m_w_ukv, m_sgu_norm_g, m_sgu_norm_b, m_w_spatial, m_b_spatial, m_w_out, m_ln_g, m_ln_b, v_w_in, v_q_norm_g, v_w_uq, v_kv_norm_g, v_w_ukv, v_sgu_norm_g, v_sgu_norm_b, v_w_spatial, v_b_spatial, v_w_out, v_ln_g, v_ln_b):
    seq = x.shape[1]
    x2 = x.reshape(seq, D_MODEL)
    tgt = loss_target.reshape(seq, D_MODEL)
    pos = positions.reshape(seq, 1)

    a_in, a_uq, a_ukv, a_out = _weight_gather([w_in, w_uq, w_ukv, w_out])
    w_uq_f = jnp.swapaxes(a_uq, 0, 1).reshape(Q_LORA, HEADS * (NOPE + ROPE))
    w_ukv_f = jnp.swapaxes(a_ukv, 0, 1).reshape(KV_LORA, HEADS * (NOPE + VDIM))
    wout = a_out.reshape(D_MODEL, D_MODEL)
    zc = lambda n: jnp.zeros((D_MODEL, n), BF16)
    win = jnp.concatenate([a_in[0][:, 0:C_KR], zc(NOPE), a_in[0][:, C_KR:C_KR + ROPE], zc(HP - NOPE - ROPE),
                           a_in[0][:, C_KR + ROPE:],
                           a_in[1], a_in[2], a_in[3]], axis=1)
    wuq = jnp.pad(w_uq_f.reshape(Q_LORA, HEADS, NOPE + ROPE), ((0, 0), (0, 0), (0, HP - NOPE - ROPE)))
    wuq = wuq.reshape(Q_LORA, HEADS * HP)
    ukv = w_ukv_f.reshape(KV_LORA, HEADS, NOPE + VDIM)
    wk = jnp.pad(ukv[:, :, 0:NOPE], ((0, 0), (0, 0), (0, HP - NOPE))).reshape(KV_LORA, HEADS * HP)
    wkv = jnp.concatenate([wk, ukv[:, :, NOPE:].reshape(KV_LORA, MLA_W)], axis=1)

    lane = np.arange(HP)
    half = ROPE // 2
    inv_freq = (1.0 / (ROPE_THETA ** (np.arange(half, dtype=np.float32) / half))).astype(np.float32)
    in_rope = (lane >= NOPE) & (lane < NOPE + ROPE)
    invf = jnp.asarray(np.where(in_rope, inv_freq[(lane - NOPE) % half], 0.0).astype(np.float32))
    m1 = jnp.asarray(np.where((lane >= NOPE) & (lane < NOPE + half), -1.0, 0.0).astype(np.float32))
    m2 = jnp.asarray(np.where((lane >= NOPE + half) & (lane < NOPE + ROPE), 1.0, 0.0).astype(np.float32))
    row = lambda a: jnp.pad(a.astype(F32), (0, D_MODEL - a.shape[0]))
    pvec = jnp.stack([row(q_norm_g), row(kv_norm_g), row(sgu_norm_g), row(sgu_norm_b), row(invf), row(m1),
                      row(m2), row(ln_g), row(ln_b)] + [jnp.zeros((D_MODEL,), F32)] * (PV_ROWS - 9))
    tri = jnp.tril(jnp.ones((CHUNK, CHUNK), dtype=bool))
    wt = jnp.where(tri[None], w_spatial, 0.0).astype(BF16)
    wtt = jnp.swapaxes(wt, 1, 2)
    bsp = jnp.repeat(b_spatial.T, VDIM, axis=1)

    cq, ckv, gate, q, k, v, vt, cs = _fwd_pre(x2, pos, win, wuq, wkv, pvec)
    o, lse, mt, p_all = _attn_fwd(q, k, vt)
    dh2, do, dgate, g_wout, g_wsp, gvec = _post(x2, tgt, o, gate, wout, pvec, wt, wtt, bsp)
    dq, dk, dv = _attn_bwd(q, k, v, do, o, lse, cs, pvec, mt, p_all)
    gx, g_win, g_wuq, g_wkv, gvec = _bwd_pre(x2, dh2, cq, ckv, cs, dq, dk, dv, dgate, win, wuq, wkv, pvec, gvec)

    cw = w_in.shape[1]
    first = D_INR - 3 * cw
    g_win_0 = jnp.concatenate([g_win[:, 0:C_KR], g_win[:, C_KR + NOPE:C_KR + NOPE + ROPE], g_win[:, C_GATE:first]],
                              axis=1)
    g_win_b = jnp.stack([g_win_0] + [g_win[:, first + cw * jb:first + cw * (jb + 1)] for jb in range(3)])
    g_wuq_f = g_wuq.reshape(Q_LORA, HEADS, HP)[:, :, 0:NOPE + ROPE].reshape(Q_LORA, HEADS * (NOPE + ROPE))
    g_k = g_wkv[:, 0:HEADS * HP].reshape(KV_LORA, HEADS, HP)[:, :, 0:NOPE]
    g_v = g_wkv[:, HEADS * HP:].reshape(KV_LORA, HEADS, VDIM)
    g_wukv_f = jnp.concatenate([g_k, g_v], axis=2).reshape(KV_LORA, HEADS * (NOPE + VDIM))

    def by_chip(a):
        rows, cols = a.shape[0], a.shape[1] // 4
        return jnp.swapaxes(a.reshape(rows, 4, cols), 0, 1).reshape(4, 2, rows // 2, cols)

    gs = [g_win_b.reshape(4, 2, D_MODEL // 2, cw), by_chip(g_wuq_f), by_chip(g_wukv_f), g_wout.reshape(4, 2, 128, D_MODEL),
          g_wsp.reshape(4, 2, CHUNK, CHUNK)]
    r_in, r_uq, r_ukv, r_out, r_wsp, r_vec = _grad_reduce(gs, gvec)

    g_big = [r_in.reshape(w_in.shape), r_uq.reshape(w_uq.shape), r_ukv.reshape(w_ukv.shape),
             r_out.reshape(w_out.shape), r_wsp.reshape(w_spatial.shape)]
    small = lambda qg, kvg, sg, sb, lng, lnb, bs: [qg.reshape(1, -1), kvg.reshape(1, -1), sg.reshape(1, -1),
                                                   sb.reshape(1, -1), lng.reshape(1, -1), lnb.reshape(1, -1), bs]
    res = _adamw(g_big, [w_in, w_uq, w_ukv, w_out, w_spatial], [m_w_in, m_w_uq, m_w_ukv, m_w_out, m_w_spatial],
                 [v_w_in, v_w_uq, v_w_ukv, v_w_out, v_w_spatial], r_vec,
                 small(q_norm_g, kv_norm_g, sgu_norm_g, sgu_norm_b, ln_g, ln_b, b_spatial),
                 small(m_q_norm_g, m_kv_norm_g, m_sgu_norm_g, m_sgu_norm_b, m_ln_g, m_ln_b, m_b_spatial),
                 small(v_q_norm_g, v_kv_norm_g, v_sgu_norm_g, v_sgu_norm_b, v_ln_g, v_ln_b, v_b_spatial))

    def ordered(big, sm):
        vec = lambda n: sm[n].reshape(-1)
        return [big[0], vec(0), big[1], vec(1), big[2], vec(2), vec(3), big[4], sm[6], big[3], vec(4), vec(5)]

    loss = r_vec[GV_LOSS, 0]
    return (loss, gx.reshape(1, seq, D_MODEL), *ordered(res[0:5], res[20:27]), *ordered(res[5:10], res[27:34]),
            *ordered(res[10:15], res[34:41]), *ordered(res[15:20], res[41:48]))
```

```python
import math

import jax
import jax.numpy as jnp
import numpy as np
from jax import lax
from jax.experimental import pallas as pl
from jax.experimental.pallas import tpu as pltpu

F32 = jnp.float32
BF16 = jnp.bfloat16

D_MODEL = 1024
Q_LORA = 256
KV_LORA = 128
HEADS = 8
NOPE = 64
ROPE = 32
VDIM = 64
MLA_W = HEADS * VDIM
GW = 512
CHUNK = 128
HP = 128
PAIRS = HEADS // 2
D_IN = 2464
D_INR = 2560
C_CKV = Q_LORA
C_KR = Q_LORA + KV_LORA
C_GATE = C_KR + HP
ROPE_THETA = 10000.0
DN_ALPHA = 2.0 ** 0.25
EPS = 1e-5
SCALE = 1.0 / math.sqrt(NOPE + ROPE)
SCALE_LOG2E = SCALE * 1.4426950408889634
INV_SQRT2 = 0.7071067811865476
INV_SQRT_2PI = 0.3989422804014327

ADAM_LR = 0.001
ADAM_B1 = 0.9
ADAM_B2 = 0.999
ADAM_EPS = 1e-08
ADAM_WD = 0.01
ADAM_STEP = 10

PV_QG, PV_KVG, PV_SG, PV_SB, PV_INVF, PV_M1, PV_M2, PV_LNG, PV_LNB = range(9)
PV_ROWS = 16
GV_QG, GV_KVG, GV_SG, GV_SB, GV_LNG, GV_LNB, GV_LOSS = range(7)
GV_BSP = 8
GV_ROWS = 16

MESH = pl.DeviceIdType.MESH

FWD_TILE = 1024
POST_TILE = 512
BWD_TILE = 512
ATT_BLK = 512
ADAM_STEPS = 4
SUBLANES, LANES = 8, 128
VMEM_LIMIT = 60 * 1024 * 1024


def _dot(a, b):
    return jnp.dot(a, b, preferred_element_type=F32)


def _dot_nt(a, b):
    return lax.dot_general(a, b, (((1,), (1,)), ((), ())), preferred_element_type=F32)


def _dot_tn(a, b):
    return lax.dot_general(a, b, (((0,), (0,)), ((), ())), preferred_element_type=F32)


def _sigmoid(z):
    return pl.reciprocal(1.0 + jnp.exp(-z), approx=True)


def _gelu_and_grad(x):
    cdf = 0.5 * (1.0 + lax.erf(x * INV_SQRT2))
    return x * cdf, cdf + x * (INV_SQRT_2PI * jnp.exp(-0.5 * x * x))


def _rms_stats(x):
    r = lax.rsqrt(jnp.mean(x * x, axis=-1, keepdims=True) + EPS)
    return x * r, r


def _rms_bwd(dy, g, xh, r):
    dyg = dy * g
    return r * (dyg - xh * jnp.mean(dyg * xh, axis=-1, keepdims=True))


def _ln_stats(x):
    mu = jnp.mean(x, axis=-1, keepdims=True)
    xc = x - mu
    r = lax.rsqrt(jnp.mean(xc * xc, axis=-1, keepdims=True) + EPS)
    return xc * r, r


def _ln_bwd(dy, g, xh, r):
    dxh = dy * g
    return r * (dxh - jnp.mean(dxh, axis=-1, keepdims=True) - xh * jnp.mean(dxh * xh, axis=-1, keepdims=True))


def _rope_fwd(t, c, s1, s2):
    return t * c + pltpu.roll(t, HP - 16, 1) * s1 + pltpu.roll(t, 16, 1) * s2


def _rope_bwd(d, c, s1, s2):
    return d * c + pltpu.roll(d * s1, 16, 1) + pltpu.roll(d * s2, HP - 16, 1)


def _lane_lt64(shape):
    return lax.broadcasted_iota(jnp.int32, shape, len(shape) - 1) < 64


def _spatial_mix(w_ref, src, dst_ref, rows):
    for c in range(rows // CHUNK):
        for p in range(PAIRS):
            blk = src[c * CHUNK:(c + 1) * CHUNK, p * HP:(p + 1) * HP]
            a = _dot(w_ref[2 * p], blk)
            b = _dot(w_ref[2 * p + 1], blk)
            dst_ref[c * CHUNK:(c + 1) * CHUNK, p * HP:(p + 1) * HP] = jnp.where(_lane_lt64(a.shape), a, b)


def _gmlp_fwd(u_pre, v_pre, zb, sg, sb, wt_ref, bsp_ref, sv_ref, rows):
    u, du = _gelu_and_grad(u_pre)
    gv, dgv = _gelu_and_grad(v_pre)
    xh, r = _ln_stats(gv)
    vln = (xh * sg + sb).astype(BF16)
    _spatial_mix(wt_ref, vln, sv_ref, rows)
    bias = bsp_ref[...]
    svb = sv_ref[...] + jnp.concatenate([bias] * (rows // CHUNK), axis=0)
    sig = _sigmoid(zb)
    return u, du, dgv, xh, r, vln, svb, sig


def _weight_gather(shards):
    n_arr = len(shards)

    def body(*refs):
        ins, outs = refs[0:n_arr], refs[n_arr:2 * n_arr]
        send_sems, recv_sems = refs[2 * n_arr:]
        x, y, c = lax.axis_index("x"), lax.axis_index("y"), lax.axis_index("c")
        j = 2 * x + y
        sib = (x, y, 1 - c)
        chips = [(1 - x, y), (x, 1 - y), (1 - x, 1 - y)]
        for n in range(n_arr):
            outs[n][j] = ins[n][...].astype(BF16)

        def half(n, blk, core):
            r = shards[n].shape[0] // 2
            return outs[n].at[blk, pl.ds(pl.multiple_of(core * r, 16), r), :]

        def copy(k, ref, to):
            return pltpu.make_async_remote_copy(
                src_ref=ref, dst_ref=ref, send_sem=send_sems.at[k], recv_sem=recv_sems.at[k],
                device_id=to, device_id_type=MESH)

        first = [copy(6 * n + kk, half(n, j, c), (px, py, c))
                 for n in range(n_arr) for kk, (px, py) in enumerate(chips)]
        for cp in first:
            cp.start()
        passed = []
        for n in range(n_arr):
            for kk, (px, py) in enumerate(chips):
                landed = half(n, 2 * px + py, c)
                copy(6 * n + kk, landed, (px, py, c)).wait_recv()
                passed.append(copy(6 * n + 3 + kk, landed, sib))
                passed[-1].start()
        for n in range(n_arr):
            for kk, (px, py) in enumerate(chips):
                copy(6 * n + 3 + kk, half(n, 2 * px + py, 1 - c), sib).wait_recv()
        for cp in first + passed:
            cp.wait_send()

    vmem = pl.BlockSpec(memory_space=pltpu.VMEM)
    return pl.pallas_call(
        body, name="weight_gather",
        out_shape=[jax.ShapeDtypeStruct((4,) + a.shape, BF16) for a in shards],
        in_specs=[vmem] * n_arr, out_specs=[vmem] * n_arr,
        scratch_shapes=[pltpu.SemaphoreType.DMA((6 * n_arr,)), pltpu.SemaphoreType.DMA((6 * n_arr,))],
        compiler_params=pltpu.CompilerParams(vmem_limit_bytes=VMEM_LIMIT),
    )(*shards)


def _fwd_pre(x, pos, win, wuq, wkv, pvec):
    seq = x.shape[0]
    t = FWD_TILE

    def body(x_ref, pos_ref, win_ref, wuq_ref, wkv_ref, pv_ref,
             cq_o, ckv_o, gate_o, q_o, k_o, v_o, vt_o, cs_o):
        xb = x_ref[...].astype(BF16)
        proj = _dot(xb, win_ref[:, 0:C_GATE])
        cq = proj[:, 0:C_CKV]
        ckv = proj[:, C_CKV:C_KR]
        kr = proj[:, C_KR:C_GATE]
        cq_o[...] = cq
        ckv_o[...] = ckv

        ang = pos_ref[...].astype(F32) * pv_ref[PV_INVF:PV_INVF + 1, 0:HP]
        cos = jnp.cos(ang)
        sin = jnp.sin(ang)
        cs_o[:, 0:HP] = cos
        cs_o[:, HP:2 * HP] = sin
        s1 = sin * pv_ref[PV_M1:PV_M1 + 1, 0:HP]
        s2 = sin * pv_ref[PV_M2:PV_M2 + 1, 0:HP]

        cqh, _ = _rms_stats(cq)
        q_all = _dot((cqh * pv_ref[PV_QG:PV_QG + 1, 0:Q_LORA]).astype(BF16), wuq_ref[...])
        ckvh, _ = _rms_stats(ckv)
        kv_all = _dot((ckvh * pv_ref[PV_KVG:PV_KVG + 1, 0:KV_LORA]).astype(BF16), wkv_ref[...])
        krr = _rope_fwd(kr, cos, s1, s2)
        for h in range(HEADS):
            sl = slice(h * HP, (h + 1) * HP)
            q_o[:, sl] = (_rope_fwd(q_all[:, sl], cos, s1, s2) * SCALE_LOG2E).astype(BF16)
            k_o[:, sl] = (kv_all[:, sl] + krr).astype(BF16)
        val = kv_all[:, HEADS * HP:].astype(BF16)
        v_o[...] = val
        for blk in range(t // ATT_BLK):
            vt_o[blk] = val[blk * ATT_BLK:(blk + 1) * ATT_BLK, :].T
        gate_o[...] = _dot(xb, win_ref[:, C_GATE:D_INR]).astype(BF16)

    tile = lambda w: pl.BlockSpec((t, w), lambda i: (i, 0))
    full = lambda a: pl.BlockSpec(a.shape, lambda i: (0,) * a.ndim)
    outs = [(Q_LORA, F32), (KV_LORA, F32), (2048, BF16), (HEADS * HP, BF16), (HEADS * HP, BF16), (MLA_W, BF16)]
    assert t % ATT_BLK == 0
    out_specs = [tile(w) for w, _ in outs]
    out_specs += [pl.BlockSpec((t // ATT_BLK, MLA_W, ATT_BLK), lambda i: (i, 0, 0)), tile(2 * HP)]
    out_shape = [jax.ShapeDtypeStruct((seq, w), d) for w, d in outs]
    out_shape += [jax.ShapeDtypeStruct((seq // ATT_BLK, MLA_W, ATT_BLK), BF16), jax.ShapeDtypeStruct((seq, 2 * HP), F32)]
    return pl.pallas_call(
        body, name="fwd_pre", grid=(seq // t,),
        in_specs=[tile(D_MODEL), tile(1), full(win), full(wuq), full(wkv), full(pvec)],
        out_specs=out_specs, out_shape=out_shape,
        compiler_params=pltpu.CompilerParams(dimension_semantics=("arbitrary",), vmem_limit_bytes=VMEM_LIMIT),
    )(x, pos, win, wuq, wkv, pvec)


def _attn_fwd(q, k, vt):
    seq = q.shape[0]
    b = ATT_BLK
    nq = seq // b
    assert nq % 2 == 0
    assert nq % 4 == 0
    n_wide = sum(i // 4 for i in range(nq))

    def body(q_ref, k_ref, vt_ref, o_o, lse_o, m_ref, l_ref, acc_ref, s_even, s_odd):
        m_ref[...] = jnp.full(m_ref.shape, -jnp.inf, F32)
        l_ref[...] = jnp.zeros(l_ref.shape, F32)
        acc_ref[...] = jnp.zeros(acc_ref.shape, F32)

        def scores(i, j, s_ref, nkb):
            qrows = pl.ds(pl.multiple_of(i * b, b), b)
            krows = pl.ds(pl.multiple_of(j * b, b), nkb * b)
            for a in range(2):
                s_ref[a, 0:nkb * b, :] = _dot_nt(k_ref[krows, a * HP:(a + 1) * HP], q_ref[qrows, a * HP:(a + 1) * HP])

        def consume(i, j, s_ref, nkb, masked):
            for a in range(2):
                st = s_ref[a, 0:nkb * b, :]
                if masked:
                    ki = lax.broadcasted_iota(jnp.int32, st.shape, 0)
                    qi = lax.broadcasted_iota(jnp.int32, st.shape, 1)
                    st = jnp.where(ki <= qi, st, -jnp.inf)
                m_prev = m_ref[i, a:a + 1, :]
                m_new = jnp.maximum(m_prev, jnp.max(st, axis=0, keepdims=True))
                alpha = jnp.exp2(m_prev - m_new)
                pt = jnp.exp2(st - m_new)
                ptb = pt.astype(BF16)
                l_ref[i, a:a + 1, :] = alpha * l_ref[i, a:a + 1, :] + jnp.sum(pt, axis=0, keepdims=True)
                pv = _dot(vt_ref[j], ptb[0:b, :])
                for kb in range(1, nkb):
                    pv = pv + _dot(vt_ref[j + kb], ptb[kb * b:(kb + 1) * b, :])
                acc_ref[i, a] = alpha * acc_ref[i, a] + pv
                m_ref[i, a:a + 1, :] = m_new

        def run(count, first, following, nkb, masked):
            if count == 0:
                return
            scores(*first, s_even, nkb)

            def two(u, ij):
                nxt = following(*ij)
                scores(*nxt, s_odd, nkb)
                consume(*ij, s_even, nkb, masked)
                nxt2 = following(*nxt)
                scores(*nxt2, s_even, nkb)
                consume(*nxt, s_odd, nkb, masked)
                return nxt2

            last = lax.fori_loop(0, count // 2, two, tuple(jnp.int32(c) for c in first))
            if count % 2:
                consume(*last, s_even, nkb, masked)

        def clamp(i):
            return jnp.minimum(i, nq - 1)

        def next_wide(i, j):
            wrap = j + 8 > i
            return clamp(jnp.where(wrap, i + 1, i)), jnp.where(wrap, 0, j + 4)

        def next_pair(i, j):
            low = lax.rem(i, 4) == 2
            return clamp(jnp.where(low, i + 1, i + 3)), jnp.minimum(jnp.where(low, j, j + 4), nq - 4)

        run(n_wide, (4, 0), next_wide, 4, False)
        run(nq // 2, (2, 0), next_pair, 2, False)
        run(nq // 2, (1, 0), lambda i, j: (clamp(i + 2), jnp.minimum(j + 2, nq - 2)), 1, False)
        run(nq, (0, 0), lambda i, j: (clamp(i + 1), clamp(j + 1)), 1, True)
        top = lax.broadcasted_iota(jnp.int32, (HP, b), 0) < 64

        def finish(i, carry):
            rows = pl.ds(pl.multiple_of(i * b, b), b)
            o_o[rows, :] = jnp.where(top, acc_ref[i, 0] / l_ref[i, 0:1, :], acc_ref[i, 1] / l_ref[i, 1:2, :]).T
            lse_o[i] = m_ref[i, 0:2, :] + jnp.log2(l_ref[i, 0:2, :])
            return carry

        lax.fori_loop(0, nq, finish, 0)

    return pl.pallas_call(
        body, name="attn_fwd", grid=(PAIRS,),
        in_specs=[pl.BlockSpec((seq, 2 * HP), lambda p: (0, p)),
                  pl.BlockSpec((seq, 2 * HP), lambda p: (0, p)),
                  pl.BlockSpec((nq, HP, b), lambda p: (0, p, 0))],
        out_specs=[pl.BlockSpec((seq, HP), lambda p: (0, p)),
                   pl.BlockSpec((None, nq, 2, b), lambda p: (p, 0, 0, 0))],
        out_shape=[jax.ShapeDtypeStruct((seq, MLA_W), F32),
                   jax.ShapeDtypeStruct((PAIRS, nq, 2, b), F32)],
        scratch_shapes=[pltpu.VMEM((nq, 8, b), F32), pltpu.VMEM((nq, 8, b), F32), pltpu.VMEM((nq, 2, HP, b), F32),
                        pltpu.VMEM((2, 4 * b, b), F32), pltpu.VMEM((2, 4 * b, b), F32)],
        compiler_params=pltpu.CompilerParams(dimension_semantics=("arbitrary",), vmem_limit_bytes=VMEM_LIMIT),
    )(q, k, vt)


def _post(x, tgt, o, gate, wout, pvec, wt, wtt, bsp):
    seq = x.shape[0]
    t = POST_TILE
    nt = seq // t

    def body(x_ref, tgt_ref, o_ref, gate_ref, wout_ref, pv_ref, wt_ref, wtt_ref, bsp_ref,
             dh2_o, do_o, dgate_o, gwout_o, gwsp_o, vec_o, sv_ref, dvln_ref, bacc_ref):
        i = pl.program_id(0)

        @pl.when(i == 0)
        def _():
            gwout_o[...] = jnp.zeros_like(gwout_o)
            gwsp_o[...] = jnp.zeros_like(gwsp_o)
            vec_o[...] = jnp.zeros_like(vec_o)
            bacc_ref[...] = jnp.zeros_like(bacc_ref)

        za = gate_ref[:, 0:512].astype(F32)
        u_pre = gate_ref[:, 512:1024].astype(F32)
        v_pre = gate_ref[:, 1024:1536].astype(F32)
        zb = gate_ref[:, 1536:2048].astype(F32)
        sg = pv_ref[PV_SG:PV_SG + 1, 0:GW]
        sb = pv_ref[PV_SB:PV_SB + 1, 0:GW]
        lng = pv_ref[PV_LNG:PV_LNG + 1, :]
        lnb = pv_ref[PV_LNB:PV_LNB + 1, :]
        o = o_ref[...]

        sig_a = _sigmoid(za)
        silu_a = za * sig_a
        u, du, dgv, xh, r, vln, svb, sig_b = _gmlp_fwd(u_pre, v_pre, zb, sg, sb, wt_ref, bsp_ref, sv_ref, t)
        silu_b = zb * sig_b
        sgu = u * svb
        merged = jnp.concatenate([o * silu_a, sgu * silu_b], axis=1).astype(BF16)
        h2 = DN_ALPHA * x_ref[...] + _dot(merged, wout_ref[...])
        xh2, r2 = _ln_stats(h2)
        err = xh2 * lng + lnb - tgt_ref[...]
        d_out = err * (1.0 / D_MODEL)
        vec_o[GV_LNG:GV_LNG + 1, :] += jnp.sum(d_out * xh2, axis=0, keepdims=True)
        vec_o[GV_LNB:GV_LNB + 1, :] += jnp.sum(d_out, axis=0, keepdims=True)
        vec_o[GV_LOSS:GV_LOSS + 1, :] += jnp.sum(err * err, axis=0, keepdims=True) * (0.5 / D_MODEL)

        d_h2 = _ln_bwd(d_out, lng, xh2, r2)
        dh2_o[...] = d_h2
        dh2b = d_h2.astype(BF16)
        gwout_o[...] += _dot_tn(merged, dh2b)
        d_m = _dot_nt(dh2b, wout_ref[...])
        d_oa = d_m[:, 0:512]
        d_ob = d_m[:, 512:1024]
        do_o[...] = (d_oa * silu_a).astype(BF16)
        dgate_o[:, 0:512] = (d_oa * o * (sig_a * (1.0 + za * (1.0 - sig_a)))).astype(BF16)
        dgate_o[:, 1536:2048] = (d_ob * sgu * (sig_b * (1.0 + zb * (1.0 - sig_b)))).astype(BF16)
        d_sgu = d_ob * silu_b
        dgate_o[:, 512:1024] = (d_sgu * svb * du).astype(BF16)
        d_sv = d_sgu * u
        acc = bacc_ref[...]
        for c in range(t // CHUNK):
            acc = acc + d_sv[c * CHUNK:(c + 1) * CHUNK, :]
        bacc_ref[...] = acc
        d_svb = d_sv.astype(BF16)
        for c in range(t // CHUNK):
            for p in range(PAIRS):
                blk = d_svb[c * CHUNK:(c + 1) * CHUNK, p * HP:(p + 1) * HP]
                vblk = vln[c * CHUNK:(c + 1) * CHUNK, p * HP:(p + 1) * HP]
                first = _lane_lt64(blk.shape)
                gwsp_o[2 * p] += _dot_nt(jnp.where(first, blk, jnp.zeros_like(blk)), vblk)
                gwsp_o[2 * p + 1] += _dot_nt(jnp.where(first, jnp.zeros_like(blk), blk), vblk)
        _spatial_mix(wtt_ref, d_svb, dvln_ref, t)
        d_vln = dvln_ref[...]
        vec_o[GV_SG:GV_SG + 1, 0:GW] += jnp.sum(d_vln * xh, axis=0, keepdims=True)
        vec_o[GV_SB:GV_SB + 1, 0:GW] += jnp.sum(d_vln, axis=0, keepdims=True)
        dgate_o[:, 1024:1536] = (_ln_bwd(d_vln, sg, xh, r) * dgv).astype(BF16)


        @pl.when(i == nt - 1)
        def _():
            tri = (lax.broadcasted_iota(jnp.int32, (CHUNK, CHUNK), 1)
                   <= lax.broadcasted_iota(jnp.int32, (CHUNK, CHUNK), 0))
            for h in range(HEADS):
                gwsp_o[h] = jnp.where(tri, gwsp_o[h], 0.0)
            lane = lax.broadcasted_iota(jnp.int32, (CHUNK, HP), 1)
            res = jnp.zeros((CHUNK, HP), F32)
            for h in range(HEADS):
                p, a = divmod(h, 2)
                blk = bacc_ref[:, p * HP:(p + 1) * HP]
                part = jnp.where(_lane_lt64(blk.shape) == (a == 0), blk, 0.0)
                res = jnp.where(lane == h, jnp.sum(part, axis=-1, keepdims=True), res)
            vec_o[GV_BSP:GV_BSP + HEADS, 0:HP] = res.T[0:HEADS, :]
            lane1 = lax.broadcasted_iota(jnp.int32, (1, D_MODEL), 1)
            total = jnp.sum(vec_o[GV_LOSS:GV_LOSS + 1, :], axis=-1, keepdims=True)
            vec_o[GV_LOSS:GV_LOSS + 1, :] = jnp.where(lane1 == 0, total, 0.0)

    tile = lambda w: pl.BlockSpec((t, w), lambda i: (i, 0))
    full = lambda a: pl.BlockSpec(a.shape, lambda i: (0,) * a.ndim)
    const = lambda s: pl.BlockSpec(s, lambda i: (0,) * len(s))
    return pl.pallas_call(
        body, name="post", grid=(nt,),
        in_specs=[tile(D_MODEL), tile(D_MODEL), tile(MLA_W), tile(2048), full(wout), full(pvec),
                  full(wt), full(wtt), full(bsp)],
        out_specs=[tile(D_MODEL), tile(MLA_W), tile(2048), const((D_MODEL, D_MODEL)),
                   const((HEADS, CHUNK, CHUNK)), const((GV_ROWS, D_MODEL))],
        out_shape=[jax.ShapeDtypeStruct((seq, D_MODEL), F32), jax.ShapeDtypeStruct((seq, MLA_W), BF16),
                   jax.ShapeDtypeStruct((seq, 2048), BF16), jax.ShapeDtypeStruct((D_MODEL, D_MODEL), F32),
                   jax.ShapeDtypeStruct((HEADS, CHUNK, CHUNK), F32), jax.ShapeDtypeStruct((GV_ROWS, D_MODEL), F32)],
        scratch_shapes=[pltpu.VMEM((t, GW), F32), pltpu.VMEM((t, GW), F32), pltpu.VMEM((CHUNK, GW), F32)],
        compiler_params=pltpu.CompilerParams(dimension_semantics=("arbitrary",), vmem_limit_bytes=VMEM_LIMIT),
    )(x, tgt, o, gate, wout, pvec, wt, wtt, bsp)


def _attn_bwd(q, k, v, do, o, lse, cs, pvec):
    seq = q.shape[0]
    b = ATT_BLK
    nq = seq // b

    def body(q_ref, k_ref, v_ref, do_ref, o_ref, lse_ref, cs_ref, pv_ref, dq_o, dk_o, dv_o, dk_acc, dv_acc):
        i = pl.program_id(1)

        @pl.when(i == 0)
        def _():
            dk_acc[...] = jnp.zeros_like(dk_acc)
            dv_acc[...] = jnp.zeros_like(dv_acc)

        first = _lane_lt64((b, HP))
        do = do_ref[...]
        zero = jnp.zeros_like(do)
        dos = [jnp.where(first, do, zero), jnp.where(first, zero, do)]
        prod_t = (do.astype(F32) * o_ref[...]).T
        deltas = [jnp.sum(prod_t[0:64, :], axis=0, keepdims=True),
                  jnp.sum(prod_t[64:128, :], axis=0, keepdims=True)]
        lses = [lse_ref[0:1, :], lse_ref[1:2, :]]
        qs = [q_ref[:, a * HP:(a + 1) * HP] for a in range(2)]

        def step(j, dqs, masked, nk=b):
            rows = pl.ds(pl.multiple_of(j * b, b), nk)
            vb = v_ref[rows, :]
            new_dq = []
            dvs = []
            for a in range(2):
                kb = k_ref[rows, a * HP:(a + 1) * HP]
                pt = jnp.exp2(_dot_nt(kb, qs[a]) - lses[a])
                if masked:
                    ki = lax.broadcasted_iota(jnp.int32, pt.shape, 0)
                    qi = lax.broadcasted_iota(jnp.int32, pt.shape, 1) + (nk - b)
                    pt = jnp.where(ki <= qi, pt, 0.0)
                dvs.append(_dot(pt.astype(BF16), do))
                dpt = _dot_nt(vb, dos[a])
                dst = (pt * (dpt - deltas[a])).astype(BF16)
                dk_acc[rows, a * HP:(a + 1) * HP] += _dot(dst, qs[a])
                new_dq.append(dqs[a] + _dot_tn(dst, kb))
            dv_acc[rows, :] += jnp.where(_lane_lt64((nk, HP)), dvs[0], dvs[1])
            return tuple(new_dq)

        init = (jnp.zeros((b, HP), F32), jnp.zeros((b, HP), F32))
        dqs = lax.fori_loop(0, i // 4, lambda jj, cr: step(4 * jj, cr, False, 4 * b), init)
        last = [lambda cr, w=w: step(4 * (i // 4), cr, True, w * b) for w in (1, 2, 3, 4)]
        dqs = lax.switch(i % 4, last, dqs)
        cos = cs_ref[:, 0:HP]
        sin = cs_ref[:, HP:2 * HP]
        s1 = sin * pv_ref[PV_M1:PV_M1 + 1, 0:HP]
        s2 = sin * pv_ref[PV_M2:PV_M2 + 1, 0:HP]
        for a in range(2):
            dq_o[:, a * HP:(a + 1) * HP] = _rope_bwd(dqs[a] * SCALE, cos, s1, s2).astype(BF16)

        @pl.when(i == nq - 1)
        def _():
            dk_o[...] = (dk_acc[...] * (SCALE / SCALE_LOG2E)).astype(BF16)
            dv_o[...] = dv_acc[...].astype(BF16)

    return pl.pallas_call(
        body, name="attn_bwd", grid=(PAIRS, nq),
        in_specs=[pl.BlockSpec((b, 2 * HP), lambda p, i: (i, p)),
                  pl.BlockSpec((seq, 2 * HP), lambda p, i: (0, p)),
                  pl.BlockSpec((seq, HP), lambda p, i: (0, p)),
                  pl.BlockSpec((b, HP), lambda p, i: (i, p)),
                  pl.BlockSpec((b, HP), lambda p, i: (i, p)),
                  pl.BlockSpec((None, None, 2, b), lambda p, i: (p, i, 0, 0)),
                  pl.BlockSpec((b, 2 * HP), lambda p, i: (i, 0)),
                  pl.BlockSpec(pvec.shape, lambda p, i: (0, 0))],
        out_specs=[pl.BlockSpec((b, 2 * HP), lambda p, i: (i, p)),
                   pl.BlockSpec((seq, 2 * HP), lambda p, i: (0, p)),
                   pl.BlockSpec((seq, HP), lambda p, i: (0, p))],
        out_shape=[jax.ShapeDtypeStruct((seq, HEADS * HP), BF16),
                   jax.ShapeDtypeStruct((seq, HEADS * HP), BF16),
                   jax.ShapeDtypeStruct((seq, MLA_W), BF16)],
        scratch_shapes=[pltpu.VMEM((seq, 2 * HP), F32), pltpu.VMEM((seq, HP), F32)],
        compiler_params=pltpu.CompilerParams(dimension_semantics=("arbitrary", "arbitrary"),
                                             vmem_limit_bytes=VMEM_LIMIT),
    )(q, k, v, do, o, lse, cs, pvec)


def _bwd_pre(x, dh2, cq, ckv, cs, dq, dk, dv, dgate, win, wuq, wkv, pvec, gvec):
    seq = x.shape[0]
    t = BWD_TILE

    def body(x_ref, dh2_ref, cq_ref, ckv_ref, cs_ref, dq_ref, dk_ref, dv_ref, dgate_ref,
             win_ref, wuq_ref, wkv_ref, pv_ref, gv_ref, gx_o, gwin_o, gwuq_o, gwkv_o, vec_o):
        i = pl.program_id(0)

        @pl.when(i == 0)
        def _():
            gwin_o[...] = jnp.zeros_like(gwin_o)
            gwuq_o[...] = jnp.zeros_like(gwuq_o)
            gwkv_o[...] = jnp.zeros_like(gwkv_o)
            vec_o[...] = gv_ref[...]

        xb = x_ref[...].astype(BF16)
        dgate = dgate_ref[...]
        gwin_o[:, C_GATE:D_INR] += _dot_tn(xb, dgate)
        gx_gate = _dot_nt(dgate, win_ref[:, C_GATE:D_INR])

        qg = pv_ref[PV_QG:PV_QG + 1, 0:Q_LORA]
        kvg = pv_ref[PV_KVG:PV_KVG + 1, 0:KV_LORA]
        dq = dq_ref[...]
        cqh, rq = _rms_stats(cq_ref[...])
        d_cqn = _dot_nt(dq, wuq_ref[...])
        gwuq_o[...] += _dot_tn((cqh * qg).astype(BF16), dq)
        vec_o[GV_QG:GV_QG + 1, 0:Q_LORA] += jnp.sum(d_cqn * cqh, axis=0, keepdims=True)
        d_cq = _rms_bwd(d_cqn, qg, cqh, rq)

        dk = dk_ref[...]
        dkv = jnp.concatenate([dk, dv_ref[...]], axis=1)
        ckvh, rkv = _rms_stats(ckv_ref[...])
        d_ckvn = _dot_nt(dkv, wkv_ref[...])
        gwkv_o[...] += _dot_tn((ckvh * kvg).astype(BF16), dkv)
        vec_o[GV_KVG:GV_KVG + 1, 0:KV_LORA] += jnp.sum(d_ckvn * ckvh, axis=0, keepdims=True)
        d_ckv = _rms_bwd(d_ckvn, kvg, ckvh, rkv)

        dks = dk[:, 0:HP].astype(F32)
        for h in range(1, HEADS):
            dks = dks + dk[:, h * HP:(h + 1) * HP].astype(F32)
        cos = cs_ref[:, 0:HP]
        sin = cs_ref[:, HP:2 * HP]
        d_kr = _rope_bwd(dks, cos, sin * pv_ref[PV_M1:PV_M1 + 1, 0:HP], sin * pv_ref[PV_M2:PV_M2 + 1, 0:HP])

        d_lat = jnp.concatenate([d_cq.astype(BF16), d_ckv.astype(BF16), d_kr.astype(BF16)], axis=1)
        gwin_o[:, 0:C_GATE] += _dot_tn(xb, d_lat)
        gx_o[...] = DN_ALPHA * dh2_ref[...] + gx_gate + _dot_nt(d_lat, win_ref[:, 0:C_GATE])

    tile = lambda w: pl.BlockSpec((t, w), lambda i: (i, 0))
    full = lambda a: pl.BlockSpec(a.shape, lambda i: (0,) * a.ndim)
    const = lambda s: pl.BlockSpec(s, lambda i: (0,) * len(s))
    return pl.pallas_call(
        body, name="bwd_pre", grid=(seq // t,),
        in_specs=[tile(D_MODEL), tile(D_MODEL), tile(Q_LORA), tile(KV_LORA), tile(2 * HP), tile(HEADS * HP),
                  tile(HEADS * HP), tile(MLA_W), tile(2048), full(win), full(wuq), full(wkv), full(pvec), full(gvec)],
        out_specs=[tile(D_MODEL), const((D_MODEL, D_INR)), const((Q_LORA, HEADS * HP)),
                   const((KV_LORA, HEADS * HP + MLA_W)), const((GV_ROWS, D_MODEL))],
        out_shape=[jax.ShapeDtypeStruct((seq, D_MODEL), F32), jax.ShapeDtypeStruct((D_MODEL, D_INR), F32),
                   jax.ShapeDtypeStruct((Q_LORA, HEADS * HP), F32),
                   jax.ShapeDtypeStruct((KV_LORA, HEADS * HP + MLA_W), F32),
                   jax.ShapeDtypeStruct((GV_ROWS, D_MODEL), F32)],
        compiler_params=pltpu.CompilerParams(dimension_semantics=("arbitrary",), vmem_limit_bytes=VMEM_LIMIT),
    )(x, dh2, cq, ckv, cs, dq, dk, dv, dgate, win, wuq, wkv, pvec, gvec)


def _grad_reduce(gs, gvec):
    n_arr = len(gs)
    n_big = n_arr - 1
    k1 = lambda n, blk: 4 * n + blk
    k2 = lambda n, kk: 4 * n_arr + 3 * n + kk
    k3 = lambda n: 7 * n_arr + n
    k3w = lambda k: 7 * n_arr + n_big + k
    kv = lambda k: 7 * n_arr + n_big + 7 + k
    n_sem = 7 * n_arr + n_big + 14

    def body(*refs):
        g, gv = refs[0:n_arr], refs[n_arr]
        outs, ov = refs[n_arr + 1:2 * n_arr + 1], refs[2 * n_arr + 1]
        r1 = refs[2 * n_arr + 2:3 * n_arr + 2]
        r2 = refs[3 * n_arr + 2:4 * n_arr + 2]
        s2 = refs[4 * n_arr + 2:5 * n_arr + 2]
        vbuf, send_sems, recv_sems = refs[5 * n_arr + 2:]
        x, y, c = lax.axis_index("x"), lax.axis_index("y"), lax.axis_index("c")
        j = 2 * x + y
        me = 2 * j + c
        sib = (x, y, 1 - c)
        chips = [(1 - x, y), (x, 1 - y), (1 - x, 1 - y)]
        others = [sib] + [(px, py, pc) for (px, py) in chips for pc in (c, 1 - c)]

        def copy(k, src, dst, to):
            return pltpu.make_async_remote_copy(
                src_ref=src, dst_ref=dst, send_sem=send_sems.at[k], recv_sem=recv_sems.at[k],
                device_id=to, device_id_type=MESH)

        l1 = [copy(k1(n, blk), g[n].at[blk, 1 - c], r1[n].at[blk], sib) for n in range(n_arr) for blk in range(4)]
        lv = [copy(kv(k), gv, vbuf.at[me], to) for k, to in enumerate(others)]
        for cp in l1 + lv:
            cp.start()
        l2 = []
        for n in range(n_arr):
            for blk in range(4):
                copy(k1(n, blk), g[n].at[blk, c], r1[n].at[blk], sib).wait_recv()
            for blk in range(4):
                r1[n][blk] = g[n][blk, c] + r1[n][blk]
                s2[n][blk] = r1[n][blk].astype(BF16)
            for kk, (px, py) in enumerate(chips):
                l2.append(copy(k2(n, kk), s2[n].at[2 * px + py], r2[n].at[kk], (px, py, c)))
                l2[-1].start()

        l3 = []
        for n in range(n_arr):
            for kk in range(3):
                copy(k2(n, kk), s2[n].at[0], r2[n].at[kk], sib).wait_recv()
            red = ((r1[n][j] + r2[n][0].astype(F32)) + r2[n][1].astype(F32)) + r2[n][2].astype(F32)
            if n < n_big:
                outs[n][c] = red
                back = [copy(k3(n), outs[n].at[c], outs[n].at[c], sib)]
            else:
                outs[n][j, c] = red
                back = [copy(k3w(k), outs[n].at[j, c], outs[n].at[j, c], to) for k, to in enumerate(others)]
            for cp in back:
                cp.start()
            l3 += back
        for n in range(n_big):
            copy(k3(n), outs[n].at[1 - c], outs[n].at[1 - c], sib).wait_recv()
        for k, (px, py, pc) in enumerate(others):
            landed = outs[n_big].at[2 * px + py, pc]
            copy(k3w(k), landed, landed, (px, py, pc)).wait_recv()
            copy(kv(k), gv, vbuf.at[4 * px + 2 * py + pc], (px, py, pc)).wait_recv()
        vbuf[me] = gv[...]
        total = vbuf[0]
        for d in range(1, 8):
            total = total + vbuf[d]
        ov[...] = total
        for cp in l1 + lv + l2 + l3:
            cp.wait_send()

    vmem = pl.BlockSpec(memory_space=pltpu.VMEM)
    half_shapes = [a.shape[2:] for a in gs]
    out_shape = [jax.ShapeDtypeStruct((2,) + s, F32) for s in half_shapes[:n_big]]
    out_shape += [jax.ShapeDtypeStruct((4, 2) + half_shapes[n_big], F32), jax.ShapeDtypeStruct(gvec.shape, F32)]
    scratch = [pltpu.VMEM((4,) + s, F32) for s in half_shapes] + [pltpu.VMEM((3,) + s, BF16) for s in half_shapes]
    scratch += [pltpu.VMEM((4,) + s, BF16) for s in half_shapes]
    scratch += [pltpu.VMEM((8,) + gvec.shape, F32), pltpu.SemaphoreType.DMA((n_sem,)), pltpu.SemaphoreType.DMA((n_sem,))]
    return pl.pallas_call(
        body, name="grad_reduce", out_shape=out_shape,
        in_specs=[vmem] * (n_arr + 1), out_specs=[vmem] * (n_arr + 1), scratch_shapes=scratch,
        compiler_params=pltpu.CompilerParams(vmem_limit_bytes=VMEM_LIMIT),
    )(*gs, gvec)


SMALL_ROWS = ((GV_QG, 1, Q_LORA), (GV_KVG, 1, KV_LORA), (GV_SG, 1, GW), (GV_SB, 1, GW),
              (GV_LNG, 1, D_MODEL), (GV_LNB, 1, D_MODEL), (GV_BSP, HEADS, CHUNK))


def _adam_update(g, w, m, v):
    m_new = ADAM_B1 * m + (1.0 - ADAM_B1) * g
    v_new = ADAM_B2 * v + (1.0 - ADAM_B2) * (g * g)
    m_hat = m_new / (1.0 - ADAM_B1 ** ADAM_STEP)
    v_hat = v_new / (1.0 - ADAM_B2 ** ADAM_STEP)
    return -ADAM_LR * (m_hat / (jnp.sqrt(v_hat) + ADAM_EPS) + ADAM_WD * w), m_new, v_new


def _adamw(g_big, w_big, m_big, v_big, gvec, w_small, m_small, v_small):
    nb, ns = len(g_big), len(w_small)

    def body(*refs):
        it = iter(refs)
        take = lambda n: [next(it) for _ in range(n)]
        g_b, w_b, m_b, v_b = take(nb), take(nb), take(nb), take(nb)
        gv = next(it)
        w_s, m_s, v_s = take(ns), take(ns), take(ns)
        g_bo, d_bo, m_bo, v_bo = take(nb), take(nb), take(nb), take(nb)
        g_so, d_so, m_so, v_so = take(ns), take(ns), take(ns), take(ns)
        for n in range(nb):
            gb = g_b[n][...]
            g_bo[n][...] = gb
            d_bo[n][...], m_bo[n][...], v_bo[n][...] = _adam_update(gb, w_b[n][...], m_b[n][...], v_b[n][...])
        for n, (row, nrow, width) in enumerate(SMALL_ROWS):
            gs = gv[row:row + nrow, 0:width]
            g_so[n][...] = gs
            d_so[n][...], m_so[n][...], v_so[n][...] = _adam_update(gs, w_s[n][...], m_s[n][...], v_s[n][...])

    def part(a):
        nd = a.ndim
        if nd > 2 or a.shape[0] % (SUBLANES * ADAM_STEPS) == 0:
            return pl.BlockSpec((a.shape[0] // ADAM_STEPS,) + a.shape[1:], lambda i: (i,) + (0,) * (nd - 1))
        assert a.shape[1] % (LANES * ADAM_STEPS) == 0
        return pl.BlockSpec((a.shape[0], a.shape[1] // ADAM_STEPS), lambda i: (0, i))

    def whole(a):
        nd = a.ndim
        return pl.BlockSpec(a.shape, lambda i: (0,) * nd)

    big = [jax.ShapeDtypeStruct(a.shape, F32) for a in w_big]
    small = [jax.ShapeDtypeStruct(a.shape, F32) for a in w_small]
    return pl.pallas_call(
        body, name="adamw", grid=(ADAM_STEPS,), out_shape=big * 4 + small * 4,
        in_specs=[part(a) for a in g_big + w_big + m_big + v_big] + [whole(gvec)]
        + [whole(a) for a in w_small + m_small + v_small],
        out_specs=[part(a) for a in w_big] * 4 + [whole(a) for a in w_small] * 4,
        compiler_params=pltpu.CompilerParams(dimension_semantics=("arbitrary",), vmem_limit_bytes=VMEM_LIMIT),
    )(*g_big, *w_big, *m_big, *v_big, gvec, *w_small, *m_small, *v_small)


def kernel(x, positions, w_in, q_norm_g, w_uq, kv_norm_g, w_ukv, sgu_norm_g, sgu_norm_b, w_spatial, b_spatial, w_out, ln_g, ln_b, loss_target, m_w_in, m_q_norm_g, m_w_uq, m_kv_norm_g, m_w_ukv, m_sgu_norm_g, m_sgu_norm_b, m_w_spatial, m_b_spatial, m_w_out, m_ln_g, m_ln_b, v_w_in, v_q_norm_g, v_w_uq, v_kv_norm_g, v_w_ukv, v_sgu_norm_g, v_sgu_norm_b, v_w_spatial, v_b_spatial, v_w_out, v_ln_g, v_ln_b):
    seq = x.shape[1]
    x2 = x.reshape(seq, D_MODEL)
    tgt = loss_target.reshape(seq, D_MODEL)
    pos = positions.reshape(seq, 1)

    a_in, a_uq, a_ukv, a_out = _weight_gather([w_in, w_uq, w_ukv, w_out])
    w_uq_f = jnp.swapaxes(a_uq, 0, 1).reshape(Q_LORA, HEADS * (NOPE + ROPE))
    w_ukv_f = jnp.swapaxes(a_ukv, 0, 1).reshape(KV_LORA, HEADS * (NOPE + VDIM))
    wout = a_out.reshape(D_MODEL, D_MODEL)
    zc = lambda n: jnp.zeros((D_MODEL, n), BF16)
    win = jnp.concatenate([a_in[0][:, 0:C_KR], zc(NOPE), a_in[0][:, C_KR:C_KR + ROPE], zc(HP - NOPE - ROPE),
                           a_in[0][:, C_KR + ROPE:],
                           a_in[1], a_in[2], a_in[3]], axis=1)
    wuq = jnp.pad(w_uq_f.reshape(Q_LORA, HEADS, NOPE + ROPE), ((0, 0), (0, 0), (0, HP - NOPE - ROPE)))
    wuq = wuq.reshape(Q_LORA, HEADS * HP)
    ukv = w_ukv_f.reshape(KV_LORA, HEADS, NOPE + VDIM)
    wk = jnp.pad(ukv[:, :, 0:NOPE], ((0, 0), (0, 0), (0, HP - NOPE))).reshape(KV_LORA, HEADS * HP)
    wkv = jnp.concatenate([wk, ukv[:, :, NOPE:].reshape(KV_LORA, MLA_W)], axis=1)

    lane = np.arange(HP)
    half = ROPE // 2
    inv_freq = (1.0 / (ROPE_THETA ** (np.arange(half, dtype=np.float32) / half))).astype(np.float32)
    in_rope = (lane >= NOPE) & (lane < NOPE + ROPE)
    invf = jnp.asarray(np.where(in_rope, inv_freq[(lane - NOPE) % half], 0.0).astype(np.float32))
    m1 = jnp.asarray(np.where((lane >= NOPE) & (lane < NOPE + half), -1.0, 0.0).astype(np.float32))
    m2 = jnp.asarray(np.where((lane >= NOPE + half) & (lane < NOPE + ROPE), 1.0, 0.0).astype(np.float32))
    row = lambda a: jnp.pad(a.astype(F32), (0, D_MODEL - a.shape[0]))
    pvec = jnp.stack([row(q_norm_g), row(kv_norm_g), row(sgu_norm_g), row(sgu_norm_b), row(invf), row(m1),
                      row(m2), row(ln_g), row(ln_b)] + [jnp.zeros((D_MODEL,), F32)] * (PV_ROWS - 9))
    tri = jnp.tril(jnp.ones((CHUNK, CHUNK), dtype=bool))
    wt = jnp.where(tri[None], w_spatial, 0.0).astype(BF16)
    wtt = jnp.swapaxes(wt, 1, 2)
    bsp = jnp.repeat(b_spatial.T, VDIM, axis=1)

    cq, ckv, gate, q, k, v, vt, cs = _fwd_pre(x2, pos, win, wuq, wkv, pvec)
    o, lse = _attn_fwd(q, k, vt)
    dh2, do, dgate, g_wout, g_wsp, gvec = _post(x2, tgt, o, gate, wout, pvec, wt, wtt, bsp)
    dq, dk, dv = _attn_bwd(q, k, v, do, o, lse, cs, pvec)
    gx, g_win, g_wuq, g_wkv, gvec = _bwd_pre(x2, dh2, cq, ckv, cs, dq, dk, dv, dgate, win, wuq, wkv, pvec, gvec)

    cw = w_in.shape[1]
    first = D_INR - 3 * cw
    g_win_0 = jnp.concatenate([g_win[:, 0:C_KR], g_win[:, C_KR + NOPE:C_KR + NOPE + ROPE], g_win[:, C_GATE:first]],
                              axis=1)
    g_win_b = jnp.stack([g_win_0] + [g_win[:, first + cw * jb:first + cw * (jb + 1)] for jb in range(3)])
    g_wuq_f = g_wuq.reshape(Q_LORA, HEADS, HP)[:, :, 0:NOPE + ROPE].reshape(Q_LORA, HEADS * (NOPE + ROPE))
    g_k = g_wkv[:, 0:HEADS * HP].reshape(KV_LORA, HEADS, HP)[:, :, 0:NOPE]
    g_v = g_wkv[:, HEADS * HP:].reshape(KV_LORA, HEADS, VDIM)
    g_wukv_f = jnp.concatenate([g_k, g_v], axis=2).reshape(KV_LORA, HEADS * (NOPE + VDIM))

    def by_chip(a):
        rows, cols = a.shape[0], a.shape[1] // 4
        return jnp.swapaxes(a.reshape(rows, 4, cols), 0, 1).reshape(4, 2, rows // 2, cols)

    gs = [g_win_b.reshape(4, 2, D_MODEL // 2, cw), by_chip(g_wuq_f), by_chip(g_wukv_f), g_wout.reshape(4, 2, 128, D_MODEL),
          g_wsp.reshape(4, 2, CHUNK, CHUNK)]
    r_in, r_uq, r_ukv, r_out, r_wsp, r_vec = _grad_reduce(gs, gvec)

    big = [w_in, w_uq, w_ukv, w_out, w_spatial]
    flip = lambda a: a.T if a.ndim == 2 and a.shape[1] % LANES else a
    flips = lambda arrs: [flip(a) for a in arrs]
    g_big = flips([r_in.reshape(w_in.shape), r_uq.reshape(w_uq.shape), r_ukv.reshape(w_ukv.shape),
                   r_out.reshape(w_out.shape), r_wsp.reshape(w_spatial.shape)])
    small = lambda qg, kvg, sg, sb, lng, lnb, bs: [qg.reshape(1, -1), kvg.reshape(1, -1), sg.reshape(1, -1),
                                                   sb.reshape(1, -1), lng.reshape(1, -1), lnb.reshape(1, -1), bs]
    res = _adamw(g_big, flips(big), flips([m_w_in, m_w_uq, m_w_ukv, m_w_out, m_w_spatial]),
                 flips([v_w_in, v_w_uq, v_w_ukv, v_w_out, v_w_spatial]), r_vec,
                 small(q_norm_g, kv_norm_g, sgu_norm_g, sgu_norm_b, ln_g, ln_b, b_spatial),
                 small(m_q_norm_g, m_kv_norm_g, m_sgu_norm_g, m_sgu_norm_b, m_ln_g, m_ln_b, m_b_spatial),
                 small(v_q_norm_g, v_kv_norm_g, v_sgu_norm_g, v_sgu_norm_b, v_ln_g, v_ln_b, v_b_spatial))
    res = [r.T if n < 4 * len(big) and r.shape != big[n % len(big)].shape else r for n, r in enumerate(res)]

    def ordered(big, sm):
        vec = lambda n: sm[n].reshape(-1)
        return [big[0], vec(0), big[1], vec(1), big[2], vec(2), vec(3), big[4], sm[6], big[3], vec(4), vec(5)]

    loss = r_vec[GV_LOSS, 0]
    return (loss, gx.reshape(1, seq, D_MODEL), *ordered(res[0:5], res[20:27]), *ordered(res[5:10], res[27:34]),
            *ordered(res[10:15], res[34:41]), *ordered(res[15:20], res[41:48]))
```

```python
import math

import jax
import jax.numpy as jnp
import numpy as np
from jax import lax
from jax.experimental import pallas as pl
from jax.experimental.pallas import tpu as pltpu

F32 = jnp.float32
BF16 = jnp.bfloat16

D_MODEL = 1024
Q_LORA = 256
KV_LORA = 128
HEADS = 8
NOPE = 64
ROPE = 32
VDIM = 64
MLA_W = HEADS * VDIM
GW = 512
CHUNK = 128
HP = 128
PAIRS = HEADS // 2
D_IN = 2464
D_INR = 2560
C_CKV = Q_LORA
C_KR = Q_LORA + KV_LORA
C_GATE = C_KR + HP
ROPE_THETA = 10000.0
DN_ALPHA = 2.0 ** 0.25
EPS = 1e-5
SCALE = 1.0 / math.sqrt(NOPE + ROPE)
SCALE_LOG2E = SCALE * 1.4426950408889634
INV_SQRT2 = 0.7071067811865476
INV_SQRT_2PI = 0.3989422804014327

ADAM_LR = 0.001
ADAM_B1 = 0.9
ADAM_B2 = 0.999
ADAM_EPS = 1e-08
ADAM_WD = 0.01
ADAM_STEP = 10

PV_QG, PV_KVG, PV_SG, PV_SB, PV_INVF, PV_M1, PV_M2, PV_LNG, PV_LNB = range(9)
PV_ROWS = 16
GV_QG, GV_KVG, GV_SG, GV_SB, GV_LNG, GV_LNB, GV_LOSS = range(7)
GV_BSP = 8
GV_ROWS = 16

MESH = pl.DeviceIdType.MESH

FWD_TILE = 1024
POST_TILE = 512
BWD_TILE = 512
ATT_BLK = 512
ADAM_STEPS = 4
SUBLANES, LANES = 8, 128
VMEM_LIMIT = 60 * 1024 * 1024


def _dot(a, b):
    return jnp.dot(a, b, preferred_element_type=F32)


def _dot_nt(a, b):
    return lax.dot_general(a, b, (((1,), (1,)), ((), ())), preferred_element_type=F32)


def _dot_tn(a, b):
    return lax.dot_general(a, b, (((0,), (0,)), ((), ())), preferred_element_type=F32)


def _sigmoid(z):
    return pl.reciprocal(1.0 + jnp.exp(-z), approx=True)


def _gelu_and_grad(x):
    cdf = 0.5 * (1.0 + lax.erf(x * INV_SQRT2))
    return x * cdf, cdf + x * (INV_SQRT_2PI * jnp.exp(-0.5 * x * x))


def _rms_stats(x):
    r = lax.rsqrt(jnp.mean(x * x, axis=-1, keepdims=True) + EPS)
    return x * r, r


def _rms_bwd(dy, g, xh, r):
    dyg = dy * g
    return r * (dyg - xh * jnp.mean(dyg * xh, axis=-1, keepdims=True))


def _ln_stats(x):
    mu = jnp.mean(x, axis=-1, keepdims=True)
    xc = x - mu
    r = lax.rsqrt(jnp.mean(xc * xc, axis=-1, keepdims=True) + EPS)
    return xc * r, r


def _ln_bwd(dy, g, xh, r):
    dxh = dy * g
    return r * (dxh - jnp.mean(dxh, axis=-1, keepdims=True) - xh * jnp.mean(dxh * xh, axis=-1, keepdims=True))


def _rope_fwd(t, c, s1, s2):
    return t * c + pltpu.roll(t, HP - 16, 1) * s1 + pltpu.roll(t, 16, 1) * s2


def _rope_bwd(d, c, s1, s2):
    return d * c + pltpu.roll(d * s1, 16, 1) + pltpu.roll(d * s2, HP - 16, 1)


def _lane_lt64(shape):
    return lax.broadcasted_iota(jnp.int32, shape, len(shape) - 1) < 64


def _spatial_mix(w_ref, src, dst_ref, rows):
    for c in range(rows // CHUNK):
        for p in range(PAIRS):
            blk = src[c * CHUNK:(c + 1) * CHUNK, p * HP:(p + 1) * HP]
            a = _dot(w_ref[2 * p], blk)
            b = _dot(w_ref[2 * p + 1], blk)
            dst_ref[c * CHUNK:(c + 1) * CHUNK, p * HP:(p + 1) * HP] = jnp.where(_lane_lt64(a.shape), a, b)


def _gmlp_fwd(u_pre, v_pre, zb, sg, sb, wt_ref, bsp_ref, sv_ref, rows):
    u, du = _gelu_and_grad(u_pre)
    gv, dgv = _gelu_and_grad(v_pre)
    xh, r = _ln_stats(gv)
    vln = (xh * sg + sb).astype(BF16)
    _spatial_mix(wt_ref, vln, sv_ref, rows)
    bias = bsp_ref[...]
    svb = sv_ref[...] + jnp.concatenate([bias] * (rows // CHUNK), axis=0)
    sig = _sigmoid(zb)
    return u, du, dgv, xh, r, vln, svb, sig


def _weight_gather(shards):
    n_arr = len(shards)
    w_in, w_uq, w_ukv, _ = shards
    cw = w_in.shape[1]
    gate0 = C_KR + ROPE
    rc = 128

    def body(*refs):
        ins = refs[0:n_arr]
        win_o, wuq_o, wkv_o, out_o = refs[n_arr:n_arr + 4]
        a_in, a_uq, a_ukv, send_sems, recv_sems = refs[n_arr + 4:]
        outs = [a_in, a_uq, a_ukv, out_o]
        x, y, c = lax.axis_index("x"), lax.axis_index("y"), lax.axis_index("c")
        j = 2 * x + y
        sib = (x, y, 1 - c)
        chips = [(1 - x, y), (x, 1 - y), (1 - x, 1 - y)]
        for n in range(n_arr):
            outs[n][j] = ins[n][...].astype(BF16)

        def half(n, blk, core):
            r = shards[n].shape[0] // 2
            return outs[n].at[blk, pl.ds(pl.multiple_of(core * r, 16), r), :]

        def copy(k, ref, to):
            return pltpu.make_async_remote_copy(
                src_ref=ref, dst_ref=ref, send_sem=send_sems.at[k], recv_sem=recv_sems.at[k],
                device_id=to, device_id_type=MESH)

        first = [copy(6 * n + kk, half(n, j, c), (px, py, c))
                 for n in range(n_arr) for kk, (px, py) in enumerate(chips)]
        for cp in first:
            cp.start()
        passed = []
        for n in range(n_arr):
            for kk, (px, py) in enumerate(chips):
                landed = half(n, 2 * px + py, c)
                copy(6 * n + kk, landed, (px, py, c)).wait_recv()
                passed.append(copy(6 * n + 3 + kk, landed, sib))
                passed[-1].start()
        for n in range(n_arr):
            for kk, (px, py) in enumerate(chips):
                copy(6 * n + 3 + kk, half(n, 2 * px + py, 1 - c), sib).wait_recv()

        def lay_in(r, carry):
            rows = pl.ds(pl.multiple_of(r * rc, rc), rc)
            win_o[rows, 0:C_KR] = a_in[0, rows, 0:C_KR]
            win_o[rows, C_KR:C_KR + NOPE] = jnp.zeros((rc, NOPE), BF16)
            win_o[rows, C_KR + NOPE:C_KR + NOPE + ROPE] = a_in[0, rows, C_KR:gate0]
            win_o[rows, C_KR + NOPE + ROPE:C_GATE] = jnp.zeros((rc, HP - NOPE - ROPE), BF16)
            win_o[rows, C_GATE:C_GATE + cw - gate0] = a_in[0, rows, gate0:cw]
            for blk in range(1, 4):
                win_o[rows, D_INR - (4 - blk) * cw:D_INR - (3 - blk) * cw] = a_in[blk, rows, :]
            return carry

        lax.fori_loop(0, D_MODEL // rc, lay_in, 0)
        wuq_o[...] = jnp.zeros(wuq_o.shape, BF16)
        wkv_o[...] = jnp.zeros(wkv_o.shape, BF16)
        qw = NOPE + ROPE
        for h in range(HEADS):
            blk, e = h // 2, h % 2
            wuq_o[:, h * HP:h * HP + qw] = a_uq[blk, :, e * qw:(e + 1) * qw]
            wkv_o[:, h * HP:h * HP + NOPE] = a_ukv[blk, :, e * HP:e * HP + NOPE]
            wkv_o[:, HEADS * HP + h * VDIM:HEADS * HP + (h + 1) * VDIM] = a_ukv[blk, :, e * HP + NOPE:(e + 1) * HP]
        for cp in first + passed:
            cp.wait_send()

    assert cw > gate0 and C_GATE + 4 * cw - gate0 == D_INR and NOPE + VDIM == HP and HEADS == 8
    vmem = pl.BlockSpec(memory_space=pltpu.VMEM)
    gathered = lambda a: (4,) + a.shape
    return pl.pallas_call(
        body, name="weight_gather",
        out_shape=[jax.ShapeDtypeStruct((D_MODEL, D_INR), BF16), jax.ShapeDtypeStruct((Q_LORA, HEADS * HP), BF16),
                   jax.ShapeDtypeStruct((KV_LORA, HEADS * HP + MLA_W), BF16),
                   jax.ShapeDtypeStruct(gathered(shards[3]), BF16)],
        in_specs=[vmem] * n_arr, out_specs=[vmem] * 4,
        scratch_shapes=[pltpu.VMEM(gathered(a), BF16) for a in shards[0:3]]
        + [pltpu.SemaphoreType.DMA((6 * n_arr,)), pltpu.SemaphoreType.DMA((6 * n_arr,))],
        compiler_params=pltpu.CompilerParams(vmem_limit_bytes=VMEM_LIMIT),
    )(*shards)


def _fwd_pre(x, pos, win, wuq, wkv, pvec):
    seq = x.shape[0]
    t = FWD_TILE

    def body(x_ref, pos_ref, win_ref, wuq_ref, wkv_ref, pv_ref,
             cq_o, ckv_o, gate_o, q_o, k_o, v_o, vt_o, cs_o):
        xb = x_ref[...].astype(BF16)
        proj = _dot(xb, win_ref[:, 0:C_GATE])
        cq = proj[:, 0:C_CKV]
        ckv = proj[:, C_CKV:C_KR]
        kr = proj[:, C_KR:C_GATE]
        cq_o[...] = cq
        ckv_o[...] = ckv

        ang = pos_ref[...].astype(F32) * pv_ref[PV_INVF:PV_INVF + 1, 0:HP]
        cos = jnp.cos(ang)
        sin = jnp.sin(ang)
        cs_o[:, 0:HP] = cos
        cs_o[:, HP:2 * HP] = sin
        s1 = sin * pv_ref[PV_M1:PV_M1 + 1, 0:HP]
        s2 = sin * pv_ref[PV_M2:PV_M2 + 1, 0:HP]

        cqh, _ = _rms_stats(cq)
        q_all = _dot((cqh * pv_ref[PV_QG:PV_QG + 1, 0:Q_LORA]).astype(BF16), wuq_ref[...])
        ckvh, _ = _rms_stats(ckv)
        kv_all = _dot((ckvh * pv_ref[PV_KVG:PV_KVG + 1, 0:KV_LORA]).astype(BF16), wkv_ref[...])
        krr = _rope_fwd(kr, cos, s1, s2)
        for h in range(HEADS):
            sl = slice(h * HP, (h + 1) * HP)
            q_o[:, sl] = (_rope_fwd(q_all[:, sl], cos, s1, s2) * SCALE_LOG2E).astype(BF16)
            k_o[:, sl] = (kv_all[:, sl] + krr).astype(BF16)
        val = kv_all[:, HEADS * HP:].astype(BF16)
        v_o[...] = val
        for blk in range(t // ATT_BLK):
            vt_o[blk] = val[blk * ATT_BLK:(blk + 1) * ATT_BLK, :].T
        gate_o[...] = _dot(xb, win_ref[:, C_GATE:D_INR]).astype(BF16)

    tile = lambda w: pl.BlockSpec((t, w), lambda i: (i, 0))
    full = lambda a: pl.BlockSpec(a.shape, lambda i: (0,) * a.ndim)
    outs = [(Q_LORA, F32), (KV_LORA, F32), (2048, BF16), (HEADS * HP, BF16), (HEADS * HP, BF16), (MLA_W, BF16)]
    assert t % ATT_BLK == 0
    out_specs = [tile(w) for w, _ in outs]
    out_specs += [pl.BlockSpec((t // ATT_BLK, MLA_W, ATT_BLK), lambda i: (i, 0, 0)), tile(2 * HP)]
    out_shape = [jax.ShapeDtypeStruct((seq, w), d) for w, d in outs]
    out_shape += [jax.ShapeDtypeStruct((seq // ATT_BLK, MLA_W, ATT_BLK), BF16), jax.ShapeDtypeStruct((seq, 2 * HP), F32)]
    return pl.pallas_call(
        body, name="fwd_pre", grid=(seq // t,),
        in_specs=[tile(D_MODEL), tile(1), full(win), full(wuq), full(wkv), full(pvec)],
        out_specs=out_specs, out_shape=out_shape,
        compiler_params=pltpu.CompilerParams(dimension_semantics=("arbitrary",), vmem_limit_bytes=VMEM_LIMIT),
    )(x, pos, win, wuq, wkv, pvec)


def _attn_fwd(q, k, vt):
    seq = q.shape[0]
    b = ATT_BLK
    nq = seq // b
    assert nq % 2 == 0
    assert nq % 4 == 0
    n_wide = sum(i // 4 for i in range(nq))

    def body(q_ref, k_ref, vt_ref, o_o, lse_o, m_ref, l_ref, acc_ref, s_even, s_odd):
        m_ref[...] = jnp.full(m_ref.shape, -jnp.inf, F32)
        l_ref[...] = jnp.zeros(l_ref.shape, F32)
        acc_ref[...] = jnp.zeros(acc_ref.shape, F32)

        def scores(i, j, s_ref, nkb):
            qrows = pl.ds(pl.multiple_of(i * b, b), b)
            krows = pl.ds(pl.multiple_of(j * b, b), nkb * b)
            for a in range(2):
                s_ref[a, 0:nkb * b, :] = _dot_nt(k_ref[krows, a * HP:(a + 1) * HP], q_ref[qrows, a * HP:(a + 1) * HP])

        def consume(i, j, s_ref, nkb, masked):
            for a in range(2):
                st = s_ref[a, 0:nkb * b, :]
                if masked:
                    ki = lax.broadcasted_iota(jnp.int32, st.shape, 0)
                    qi = lax.broadcasted_iota(jnp.int32, st.shape, 1)
                    st = jnp.where(ki <= qi, st, -jnp.inf)
                m_prev = m_ref[i, a:a + 1, :]
                m_new = jnp.maximum(m_prev, jnp.max(st, axis=0, keepdims=True))
                alpha = jnp.exp2(m_prev - m_new)
                pt = jnp.exp2(st - m_new)
                ptb = pt.astype(BF16)
                l_ref[i, a:a + 1, :] = alpha * l_ref[i, a:a + 1, :] + jnp.sum(pt, axis=0, keepdims=True)
                pv = _dot(vt_ref[j], ptb[0:b, :])
                for kb in range(1, nkb):
                    pv = pv + _dot(vt_ref[j + kb], ptb[kb * b:(kb + 1) * b, :])
                acc_ref[i, a] = alpha * acc_ref[i, a] + pv
                m_ref[i, a:a + 1, :] = m_new

        def run(count, first, following, nkb, masked):
            if count == 0:
                return
            scores(*first, s_even, nkb)

            def two(u, ij):
                nxt = following(*ij)
                scores(*nxt, s_odd, nkb)
                consume(*ij, s_even, nkb, masked)
                nxt2 = following(*nxt)
                scores(*nxt2, s_even, nkb)
                consume(*nxt, s_odd, nkb, masked)
                return nxt2

            last = lax.fori_loop(0, count // 2, two, tuple(jnp.int32(c) for c in first))
            if count % 2:
                consume(*last, s_even, nkb, masked)

        def clamp(i):
            return jnp.minimum(i, nq - 1)

        def next_wide(i, j):
            wrap = j + 8 > i
            return clamp(jnp.where(wrap, i + 1, i)), jnp.where(wrap, 0, j + 4)

        def next_pair(i, j):
            low = lax.rem(i, 4) == 2
            return clamp(jnp.where(low, i + 1, i + 3)), jnp.minimum(jnp.where(low, j, j + 4), nq - 4)

        run(n_wide, (4, 0), next_wide, 4, False)
        run(nq // 2, (2, 0), next_pair, 2, False)
        run(nq // 2, (1, 0), lambda i, j: (clamp(i + 2), jnp.minimum(j + 2, nq - 2)), 1, False)
        run(nq, (0, 0), lambda i, j: (clamp(i + 1), clamp(j + 1)), 1, True)
        top = lax.broadcasted_iota(jnp.int32, (HP, b), 0) < 64

        def finish(i, carry):
            rows = pl.ds(pl.multiple_of(i * b, b), b)
            o_o[rows, :] = jnp.where(top, acc_ref[i, 0] / l_ref[i, 0:1, :], acc_ref[i, 1] / l_ref[i, 1:2, :]).T
            lse_o[i] = m_ref[i, 0:2, :] + jnp.log2(l_ref[i, 0:2, :])
            return carry

        lax.fori_loop(0, nq, finish, 0)

    return pl.pallas_call(
        body, name="attn_fwd", grid=(PAIRS,),
        in_specs=[pl.BlockSpec((seq, 2 * HP), lambda p: (0, p)),
                  pl.BlockSpec((seq, 2 * HP), lambda p: (0, p)),
                  pl.BlockSpec((nq, HP, b), lambda p: (0, p, 0))],
        out_specs=[pl.BlockSpec((seq, HP), lambda p: (0, p)),
                   pl.BlockSpec((None, nq, 2, b), lambda p: (p, 0, 0, 0))],
        out_shape=[jax.ShapeDtypeStruct((seq, MLA_W), F32),
                   jax.ShapeDtypeStruct((PAIRS, nq, 2, b), F32)],
        scratch_shapes=[pltpu.VMEM((nq, 8, b), F32), pltpu.VMEM((nq, 8, b), F32), pltpu.VMEM((nq, 2, HP, b), F32),
                        pltpu.VMEM((2, 4 * b, b), F32), pltpu.VMEM((2, 4 * b, b), F32)],
        compiler_params=pltpu.CompilerParams(dimension_semantics=("arbitrary",), vmem_limit_bytes=VMEM_LIMIT),
    )(q, k, vt)


def _post(x, tgt, o, gate, wout, pvec, wt, wtt, bsp):
    seq = x.shape[0]
    t = POST_TILE
    nt = seq // t

    def body(x_ref, tgt_ref, o_ref, gate_ref, wout_ref, pv_ref, wt_ref, wtt_ref, bsp_ref,
             dh2_o, do_o, dgate_o, gwout_o, gwsp_o, vec_o, sv_ref, dvln_ref, bacc_ref):
        i = pl.program_id(0)

        @pl.when(i == 0)
        def _():
            gwout_o[...] = jnp.zeros_like(gwout_o)
            gwsp_o[...] = jnp.zeros_like(gwsp_o)
            vec_o[...] = jnp.zeros_like(vec_o)
            bacc_ref[...] = jnp.zeros_like(bacc_ref)

        za = gate_ref[:, 0:512].astype(F32)
        u_pre = gate_ref[:, 512:1024].astype(F32)
        v_pre = gate_ref[:, 1024:1536].astype(F32)
        zb = gate_ref[:, 1536:2048].astype(F32)
        sg = pv_ref[PV_SG:PV_SG + 1, 0:GW]
        sb = pv_ref[PV_SB:PV_SB + 1, 0:GW]
        lng = pv_ref[PV_LNG:PV_LNG + 1, :]
        lnb = pv_ref[PV_LNB:PV_LNB + 1, :]
        o = o_ref[...]

        sig_a = _sigmoid(za)
        silu_a = za * sig_a
        u, du, dgv, xh, r, vln, svb, sig_b = _gmlp_fwd(u_pre, v_pre, zb, sg, sb, wt_ref, bsp_ref, sv_ref, t)
        silu_b = zb * sig_b
        sgu = u * svb
        merged = jnp.concatenate([o * silu_a, sgu * silu_b], axis=1).astype(BF16)
        h2 = DN_ALPHA * x_ref[...] + _dot(merged, wout_ref[...])
        xh2, r2 = _ln_stats(h2)
        err = xh2 * lng + lnb - tgt_ref[...]
        d_out = err * (1.0 / D_MODEL)
        vec_o[GV_LNG:GV_LNG + 1, :] += jnp.sum(d_out * xh2, axis=0, keepdims=True)
        vec_o[GV_LNB:GV_LNB + 1, :] += jnp.sum(d_out, axis=0, keepdims=True)
        vec_o[GV_LOSS:GV_LOSS + 1, :] += jnp.sum(err * err, axis=0, keepdims=True) * (0.5 / D_MODEL)

        d_h2 = _ln_bwd(d_out, lng, xh2, r2)
        dh2_o[...] = d_h2
        dh2b = d_h2.astype(BF16)
        gwout_o[...] += _dot_tn(merged, dh2b)
        d_m = _dot_nt(dh2b, wout_ref[...])
        d_oa = d_m[:, 0:512]
        d_ob = d_m[:, 512:1024]
        do_o[...] = (d_oa * silu_a).astype(BF16)
        dgate_o[:, 0:512] = (d_oa * o * (sig_a * (1.0 + za * (1.0 - sig_a)))).astype(BF16)
        dgate_o[:, 1536:2048] = (d_ob * sgu * (sig_b * (1.0 + zb * (1.0 - sig_b)))).astype(BF16)
        d_sgu = d_ob * silu_b
        dgate_o[:, 512:1024] = (d_sgu * svb * du).astype(BF16)
        d_sv = d_sgu * u
        acc = bacc_ref[...]
        for c in range(t // CHUNK):
            acc = acc + d_sv[c * CHUNK:(c + 1) * CHUNK, :]
        bacc_ref[...] = acc
        d_svb = d_sv.astype(BF16)
        for c in range(t // CHUNK):
            for p in range(PAIRS):
                blk = d_svb[c * CHUNK:(c + 1) * CHUNK, p * HP:(p + 1) * HP]
                vblk = vln[c * CHUNK:(c + 1) * CHUNK, p * HP:(p + 1) * HP]
                first = _lane_lt64(blk.shape)
                gwsp_o[2 * p] += _dot_nt(jnp.where(first, blk, jnp.zeros_like(blk)), vblk)
                gwsp_o[2 * p + 1] += _dot_nt(jnp.where(first, jnp.zeros_like(blk), blk), vblk)
        _spatial_mix(wtt_ref, d_svb, dvln_ref, t)
        d_vln = dvln_ref[...]
        vec_o[GV_SG:GV_SG + 1, 0:GW] += jnp.sum(d_vln * xh, axis=0, keepdims=True)
        vec_o[GV_SB:GV_SB + 1, 0:GW] += jnp.sum(d_vln, axis=0, keepdims=True)
        dgate_o[:, 1024:1536] = (_ln_bwd(d_vln, sg, xh, r) * dgv).astype(BF16)


        @pl.when(i == nt - 1)
        def _():
            tri = (lax.broadcasted_iota(jnp.int32, (CHUNK, CHUNK), 1)
                   <= lax.broadcasted_iota(jnp.int32, (CHUNK, CHUNK), 0))
            for h in range(HEADS):
                gwsp_o[h] = jnp.where(tri, gwsp_o[h], 0.0)
            lane = lax.broadcasted_iota(jnp.int32, (CHUNK, HP), 1)
            res = jnp.zeros((CHUNK, HP), F32)
            for h in range(HEADS):
                p, a = divmod(h, 2)
                blk = bacc_ref[:, p * HP:(p + 1) * HP]
                part = jnp.where(_lane_lt64(blk.shape) == (a == 0), blk, 0.0)
                res = jnp.where(lane == h, jnp.sum(part, axis=-1, keepdims=True), res)
            vec_o[GV_BSP:GV_BSP + HEADS, 0:HP] = res.T[0:HEADS, :]
            lane1 = lax.broadcasted_iota(jnp.int32, (1, D_MODEL), 1)
            total = jnp.sum(vec_o[GV_LOSS:GV_LOSS + 1, :], axis=-1, keepdims=True)
            vec_o[GV_LOSS:GV_LOSS + 1, :] = jnp.where(lane1 == 0, total, 0.0)

    tile = lambda w: pl.BlockSpec((t, w), lambda i: (i, 0))
    full = lambda a: pl.BlockSpec(a.shape, lambda i: (0,) * a.ndim)
    const = lambda s: pl.BlockSpec(s, lambda i: (0,) * len(s))
    return pl.pallas_call(
        body, name="post", grid=(nt,),
        in_specs=[tile(D_MODEL), tile(D_MODEL), tile(MLA_W), tile(2048), full(wout), full(pvec),
                  full(wt), full(wtt), full(bsp)],
        out_specs=[tile(D_MODEL), tile(MLA_W), tile(2048), const((D_MODEL, D_MODEL)),
                   const((HEADS, CHUNK, CHUNK)), const((GV_ROWS, D_MODEL))],
        out_shape=[jax.ShapeDtypeStruct((seq, D_MODEL), F32), jax.ShapeDtypeStruct((seq, MLA_W), BF16),
                   jax.ShapeDtypeStruct((seq, 2048), BF16), jax.ShapeDtypeStruct((D_MODEL, D_MODEL), F32),
                   jax.ShapeDtypeStruct((HEADS, CHUNK, CHUNK), F32), jax.ShapeDtypeStruct((GV_ROWS, D_MODEL), F32)],
        scratch_shapes=[pltpu.VMEM((t, GW), F32), pltpu.VMEM((t, GW), F32), pltpu.VMEM((CHUNK, GW), F32)],
        compiler_params=pltpu.CompilerParams(dimension_semantics=("arbitrary",), vmem_limit_bytes=VMEM_LIMIT),
    )(x, tgt, o, gate, wout, pvec, wt, wtt, bsp)


def _attn_bwd(q, k, v, do, o, lse, cs, pvec):
    seq = q.shape[0]
    b = ATT_BLK
    nq = seq // b

    def body(q_ref, k_ref, v_ref, do_ref, o_ref, lse_ref, cs_ref, pv_ref, dq_o, dk_o, dv_o, dk_acc, dv_acc):
        i = pl.program_id(1)

        @pl.when(i == 0)
        def _():
            dk_acc[...] = jnp.zeros_like(dk_acc)
            dv_acc[...] = jnp.zeros_like(dv_acc)

        first = _lane_lt64((b, HP))
        do = do_ref[...]
        zero = jnp.zeros_like(do)
        dos = [jnp.where(first, do, zero), jnp.where(first, zero, do)]
        prod_t = (do.astype(F32) * o_ref[...]).T
        deltas = [jnp.sum(prod_t[0:64, :], axis=0, keepdims=True),
                  jnp.sum(prod_t[64:128, :], axis=0, keepdims=True)]
        lses = [lse_ref[0:1, :], lse_ref[1:2, :]]
        qs = [q_ref[:, a * HP:(a + 1) * HP] for a in range(2)]

        def step(j, dqs, masked, nk=b):
            rows = pl.ds(pl.multiple_of(j * b, b), nk)
            vb = v_ref[rows, :]
            new_dq = []
            dvs = []
            for a in range(2):
                kb = k_ref[rows, a * HP:(a + 1) * HP]
                pt = jnp.exp2(_dot_nt(kb, qs[a]) - lses[a])
                if masked:
                    ki = lax.broadcasted_iota(jnp.int32, pt.shape, 0)
                    qi = lax.broadcasted_iota(jnp.int32, pt.shape, 1) + (nk - b)
                    pt = jnp.where(ki <= qi, pt, 0.0)
                dvs.append(_dot(pt.astype(BF16), do))
                dpt = _dot_nt(vb, dos[a])
                dst = (pt * (dpt - deltas[a])).astype(BF16)
                dk_acc[rows, a * HP:(a + 1) * HP] += _dot(dst, qs[a])
                new_dq.append(dqs[a] + _dot_tn(dst, kb))
            dv_acc[rows, :] += jnp.where(_lane_lt64((nk, HP)), dvs[0], dvs[1])
            return tuple(new_dq)

        init = (jnp.zeros((b, HP), F32), jnp.zeros((b, HP), F32))
        dqs = lax.fori_loop(0, i // 4, lambda jj, cr: step(4 * jj, cr, False, 4 * b), init)
        last = [lambda cr, w=w: step(4 * (i // 4), cr, True, w * b) for w in (1, 2, 3, 4)]
        dqs = lax.switch(i % 4, last, dqs)
        cos = cs_ref[:, 0:HP]
        sin = cs_ref[:, HP:2 * HP]
        s1 = sin * pv_ref[PV_M1:PV_M1 + 1, 0:HP]
        s2 = sin * pv_ref[PV_M2:PV_M2 + 1, 0:HP]
        for a in range(2):
            dq_o[:, a * HP:(a + 1) * HP] = _rope_bwd(dqs[a] * SCALE, cos, s1, s2).astype(BF16)

        @pl.when(i == nq - 1)
        def _():
            dk_o[...] = (dk_acc[...] * (SCALE / SCALE_LOG2E)).astype(BF16)
            dv_o[...] = dv_acc[...].astype(BF16)

    return pl.pallas_call(
        body, name="attn_bwd", grid=(PAIRS, nq),
        in_specs=[pl.BlockSpec((b, 2 * HP), lambda p, i: (i, p)),
                  pl.BlockSpec((seq, 2 * HP), lambda p, i: (0, p)),
                  pl.BlockSpec((seq, HP), lambda p, i: (0, p)),
                  pl.BlockSpec((b, HP), lambda p, i: (i, p)),
                  pl.BlockSpec((b, HP), lambda p, i: (i, p)),
                  pl.BlockSpec((None, None, 2, b), lambda p, i: (p, i, 0, 0)),
                  pl.BlockSpec((b, 2 * HP), lambda p, i: (i, 0)),
                  pl.BlockSpec(pvec.shape, lambda p, i: (0, 0))],
        out_specs=[pl.BlockSpec((b, 2 * HP), lambda p, i: (i, p)),
                   pl.BlockSpec((seq, 2 * HP), lambda p, i: (0, p)),
                   pl.BlockSpec((seq, HP), lambda p, i: (0, p))],
        out_shape=[jax.ShapeDtypeStruct((seq, HEADS * HP), BF16),
                   jax.ShapeDtypeStruct((seq, HEADS * HP), BF16),
                   jax.ShapeDtypeStruct((seq, MLA_W), BF16)],
        scratch_shapes=[pltpu.VMEM((seq, 2 * HP), F32), pltpu.VMEM((seq, HP), F32)],
        compiler_params=pltpu.CompilerParams(dimension_semantics=("arbitrary", "arbitrary"),
                                             vmem_limit_bytes=VMEM_LIMIT),
    )(q, k, v, do, o, lse, cs, pvec)


def _bwd_pre(x, dh2, cq, ckv, cs, dq, dk, dv, dgate, win, wuq, wkv, pvec, gvec):
    seq = x.shape[0]
    t = BWD_TILE

    def body(x_ref, dh2_ref, cq_ref, ckv_ref, cs_ref, dq_ref, dk_ref, dv_ref, dgate_ref,
             win_ref, wuq_ref, wkv_ref, pv_ref, gv_ref, gx_o, gwin_o, gwuq_o, gwkv_o, vec_o):
        i = pl.program_id(0)

        @pl.when(i == 0)
        def _():
            gwin_o[...] = jnp.zeros_like(gwin_o)
            gwuq_o[...] = jnp.zeros_like(gwuq_o)
            gwkv_o[...] = jnp.zeros_like(gwkv_o)
            vec_o[...] = gv_ref[...]

        xb = x_ref[...].astype(BF16)
        dgate = dgate_ref[...]
        gwin_o[:, C_GATE:D_INR] += _dot_tn(xb, dgate)
        gx_gate = _dot_nt(dgate, win_ref[:, C_GATE:D_INR])

        qg = pv_ref[PV_QG:PV_QG + 1, 0:Q_LORA]
        kvg = pv_ref[PV_KVG:PV_KVG + 1, 0:KV_LORA]
        dq = dq_ref[...]
        cqh, rq = _rms_stats(cq_ref[...])
        d_cqn = _dot_nt(dq, wuq_ref[...])
        gwuq_o[...] += _dot_tn((cqh * qg).astype(BF16), dq)
        vec_o[GV_QG:GV_QG + 1, 0:Q_LORA] += jnp.sum(d_cqn * cqh, axis=0, keepdims=True)
        d_cq = _rms_bwd(d_cqn, qg, cqh, rq)

        dk = dk_ref[...]
        dkv = jnp.concatenate([dk, dv_ref[...]], axis=1)
        ckvh, rkv = _rms_stats(ckv_ref[...])
        d_ckvn = _dot_nt(dkv, wkv_ref[...])
        gwkv_o[...] += _dot_tn((ckvh * kvg).astype(BF16), dkv)
        vec_o[GV_KVG:GV_KVG + 1, 0:KV_LORA] += jnp.sum(d_ckvn * ckvh, axis=0, keepdims=True)
        d_ckv = _rms_bwd(d_ckvn, kvg, ckvh, rkv)

        dks = dk[:, 0:HP].astype(F32)
        for h in range(1, HEADS):
            dks = dks + dk[:, h * HP:(h + 1) * HP].astype(F32)
        cos = cs_ref[:, 0:HP]
        sin = cs_ref[:, HP:2 * HP]
        d_kr = _rope_bwd(dks, cos, sin * pv_ref[PV_M1:PV_M1 + 1, 0:HP], sin * pv_ref[PV_M2:PV_M2 + 1, 0:HP])

        d_lat = jnp.concatenate([d_cq.astype(BF16), d_ckv.astype(BF16), d_kr.astype(BF16)], axis=1)
        gwin_o[:, 0:C_GATE] += _dot_tn(xb, d_lat)
        gx_o[...] = DN_ALPHA * dh2_ref[...] + gx_gate + _dot_nt(d_lat, win_ref[:, 0:C_GATE])

    tile = lambda w: pl.BlockSpec((t, w), lambda i: (i, 0))
    full = lambda a: pl.BlockSpec(a.shape, lambda i: (0,) * a.ndim)
    const = lambda s: pl.BlockSpec(s, lambda i: (0,) * len(s))
    return pl.pallas_call(
        body, name="bwd_pre", grid=(seq // t,),
        in_specs=[tile(D_MODEL), tile(D_MODEL), tile(Q_LORA), tile(KV_LORA), tile(2 * HP), tile(HEADS * HP),
                  tile(HEADS * HP), tile(MLA_W), tile(2048), full(win), full(wuq), full(wkv), full(pvec), full(gvec)],
        out_specs=[tile(D_MODEL), const((D_MODEL, D_INR)), const((Q_LORA, HEADS * HP)),
                   const((KV_LORA, HEADS * HP + MLA_W)), const((GV_ROWS, D_MODEL))],
        out_shape=[jax.ShapeDtypeStruct((seq, D_MODEL), F32), jax.ShapeDtypeStruct((D_MODEL, D_INR), F32),
                   jax.ShapeDtypeStruct((Q_LORA, HEADS * HP), F32),
                   jax.ShapeDtypeStruct((KV_LORA, HEADS * HP + MLA_W), F32),
                   jax.ShapeDtypeStruct((GV_ROWS, D_MODEL), F32)],
        compiler_params=pltpu.CompilerParams(dimension_semantics=("arbitrary",), vmem_limit_bytes=VMEM_LIMIT),
    )(x, dh2, cq, ckv, cs, dq, dk, dv, dgate, win, wuq, wkv, pvec, gvec)


def _grad_reduce(gs, gvec):
    n_arr = len(gs)
    n_big = n_arr - 1
    k1 = lambda n, blk: 4 * n + blk
    k2 = lambda n, kk: 4 * n_arr + 3 * n + kk
    k3 = lambda n: 7 * n_arr + n
    k3w = lambda k: 7 * n_arr + n_big + k
    kv = lambda k: 7 * n_arr + n_big + 7 + k
    n_sem = 7 * n_arr + n_big + 14

    def body(*refs):
        g, gv = refs[0:n_arr], refs[n_arr]
        outs, ov = refs[n_arr + 1:2 * n_arr + 1], refs[2 * n_arr + 1]
        r1 = refs[2 * n_arr + 2:3 * n_arr + 2]
        r2 = refs[3 * n_arr + 2:4 * n_arr + 2]
        s2 = refs[4 * n_arr + 2:5 * n_arr + 2]
        vbuf, send_sems, recv_sems = refs[5 * n_arr + 2:]
        x, y, c = lax.axis_index("x"), lax.axis_index("y"), lax.axis_index("c")
        j = 2 * x + y
        me = 2 * j + c
        sib = (x, y, 1 - c)
        chips = [(1 - x, y), (x, 1 - y), (1 - x, 1 - y)]
        others = [sib] + [(px, py, pc) for (px, py) in chips for pc in (c, 1 - c)]

        def copy(k, src, dst, to):
            return pltpu.make_async_remote_copy(
                src_ref=src, dst_ref=dst, send_sem=send_sems.at[k], recv_sem=recv_sems.at[k],
                device_id=to, device_id_type=MESH)

        l1 = [copy(k1(n, blk), g[n].at[blk, 1 - c], r1[n].at[blk], sib) for n in range(n_arr) for blk in range(4)]
        lv = [copy(kv(k), gv, vbuf.at[me], to) for k, to in enumerate(others)]
        for cp in l1 + lv:
            cp.start()
        l2 = []
        for n in range(n_arr):
            for blk in range(4):
                copy(k1(n, blk), g[n].at[blk, c], r1[n].at[blk], sib).wait_recv()
            for blk in range(4):
                r1[n][blk] = g[n][blk, c] + r1[n][blk]
                s2[n][blk] = r1[n][blk].astype(BF16)
            for kk, (px, py) in enumerate(chips):
                l2.append(copy(k2(n, kk), s2[n].at[2 * px + py], r2[n].at[kk], (px, py, c)))
                l2[-1].start()

        l3 = []
        for n in range(n_arr):
            for kk in range(3):
                copy(k2(n, kk), s2[n].at[0], r2[n].at[kk], sib).wait_recv()
            red = ((r1[n][j] + r2[n][0].astype(F32)) + r2[n][1].astype(F32)) + r2[n][2].astype(F32)
            if n < n_big:
                outs[n][c] = red
                back = [copy(k3(n), outs[n].at[c], outs[n].at[c], sib)]
            else:
                outs[n][j, c] = red
                back = [copy(k3w(k), outs[n].at[j, c], outs[n].at[j, c], to) for k, to in enumerate(others)]
            for cp in back:
                cp.start()
            l3 += back
        for n in range(n_big):
            copy(k3(n), outs[n].at[1 - c], outs[n].at[1 - c], sib).wait_recv()
        for k, (px, py, pc) in enumerate(others):
            landed = outs[n_big].at[2 * px + py, pc]
            copy(k3w(k), landed, landed, (px, py, pc)).wait_recv()
            copy(kv(k), gv, vbuf.at[4 * px + 2 * py + pc], (px, py, pc)).wait_recv()
        vbuf[me] = gv[...]
        total = vbuf[0]
        for d in range(1, 8):
            total = total + vbuf[d]
        ov[...] = total
        for cp in l1 + lv + l2 + l3:
            cp.wait_send()

    vmem = pl.BlockSpec(memory_space=pltpu.VMEM)
    half_shapes = [a.shape[2:] for a in gs]
    out_shape = [jax.ShapeDtypeStruct((2,) + s, F32) for s in half_shapes[:n_big]]
    out_shape += [jax.ShapeDtypeStruct((4, 2) + half_shapes[n_big], F32), jax.ShapeDtypeStruct(gvec.shape, F32)]
    scratch = [pltpu.VMEM((4,) + s, F32) for s in half_shapes] + [pltpu.VMEM((3,) + s, BF16) for s in half_shapes]
    scratch += [pltpu.VMEM((4,) + s, BF16) for s in half_shapes]
    scratch += [pltpu.VMEM((8,) + gvec.shape, F32), pltpu.SemaphoreType.DMA((n_sem,)), pltpu.SemaphoreType.DMA((n_sem,))]
    return pl.pallas_call(
        body, name="grad_reduce", out_shape=out_shape,
        in_specs=[vmem] * (n_arr + 1), out_specs=[vmem] * (n_arr + 1), scratch_shapes=scratch,
        compiler_params=pltpu.CompilerParams(vmem_limit_bytes=VMEM_LIMIT),
    )(*gs, gvec)


SMALL_ROWS = ((GV_QG, 1, Q_LORA), (GV_KVG, 1, KV_LORA), (GV_SG, 1, GW), (GV_SB, 1, GW),
              (GV_LNG, 1, D_MODEL), (GV_LNB, 1, D_MODEL), (GV_BSP, HEADS, CHUNK))


def _adam_update(g, w, m, v):
    m_new = ADAM_B1 * m + (1.0 - ADAM_B1) * g
    v_new = ADAM_B2 * v + (1.0 - ADAM_B2) * (g * g)
    m_hat = m_new / (1.0 - ADAM_B1 ** ADAM_STEP)
    v_hat = v_new / (1.0 - ADAM_B2 ** ADAM_STEP)
    return -ADAM_LR * (m_hat / (jnp.sqrt(v_hat) + ADAM_EPS) + ADAM_WD * w), m_new, v_new


def _adamw(g_big, w_big, m_big, v_big, gvec, w_small, m_small, v_small):
    nb, ns = len(g_big), len(w_small)

    def body(*refs):
        it = iter(refs)
        take = lambda n: [next(it) for _ in range(n)]
        g_b, w_b, m_b, v_b = take(nb), take(nb), take(nb), take(nb)
        gv = next(it)
        w_s, m_s, v_s = take(ns), take(ns), take(ns)
        g_bo, d_bo, m_bo, v_bo = take(nb), take(nb), take(nb), take(nb)
        g_so, d_so, m_so, v_so = take(ns), take(ns), take(ns), take(ns)
        for n in range(nb):
            gb = g_b[n][...]
            g_bo[n][...] = gb
            d_bo[n][...], m_bo[n][...], v_bo[n][...] = _adam_update(gb, w_b[n][...], m_b[n][...], v_b[n][...])
        for n, (row, nrow, width) in enumerate(SMALL_ROWS):
            gs = gv[row:row + nrow, 0:width]
            g_so[n][...] = gs
            d_so[n][...], m_so[n][...], v_so[n][...] = _adam_update(gs, w_s[n][...], m_s[n][...], v_s[n][...])

    def part(a):
        nd = a.ndim
        if nd > 2 or a.shape[0] % (SUBLANES * ADAM_STEPS) == 0:
            return pl.BlockSpec((a.shape[0] // ADAM_STEPS,) + a.shape[1:], lambda i: (i,) + (0,) * (nd - 1))
        assert a.shape[1] % (LANES * ADAM_STEPS) == 0
        return pl.BlockSpec((a.shape[0], a.shape[1] // ADAM_STEPS), lambda i: (0, i))

    def whole(a):
        nd = a.ndim
        return pl.BlockSpec(a.shape, lambda i: (0,) * nd)

    big = [jax.ShapeDtypeStruct(a.shape, F32) for a in w_big]
    small = [jax.ShapeDtypeStruct(a.shape, F32) for a in w_small]
    return pl.pallas_call(
        body, name="adamw", grid=(ADAM_STEPS,), out_shape=big * 4 + small * 4,
        in_specs=[part(a) for a in g_big + w_big + m_big + v_big] + [whole(gvec)]
        + [whole(a) for a in w_small + m_small + v_small],
        out_specs=[part(a) for a in w_big] * 4 + [whole(a) for a in w_small] * 4,
        compiler_params=pltpu.CompilerParams(dimension_semantics=("arbitrary",), vmem_limit_bytes=VMEM_LIMIT),
    )(*g_big, *w_big, *m_big, *v_big, gvec, *w_small, *m_small, *v_small)


def kernel(x, positions, w_in, q_norm_g, w_uq, kv_norm_g, w_ukv, sgu_norm_g, sgu_norm_b, w_spatial, b_spatial, w_out, ln_g, ln_b, loss_target, m_w_in, m_q_norm_g, m_w_uq, m_kv_norm_g, m_w_ukv, m_sgu_norm_g, m_sgu_norm_b, m_w_spatial, m_b_spatial, m_w_out, m_ln_g, m_ln_b, v_w_in, v_q_norm_g, v_w_uq, v_kv_norm_g, v_w_ukv, v_sgu_norm_g, v_sgu_norm_b, v_w_spatial, v_b_spatial, v_w_out, v_ln_g, v_ln_b):
    seq = x.shape[1]
    x2 = x.reshape(seq, D_MODEL)
    tgt = loss_target.reshape(seq, D_MODEL)
    pos = positions.reshape(seq, 1)

    win, wuq, wkv, a_out = _weight_gather([w_in, w_uq, w_ukv, w_out])
    wout = a_out.reshape(D_MODEL, D_MODEL)

    lane = np.arange(HP)
    half = ROPE // 2
    inv_freq = (1.0 / (ROPE_THETA ** (np.arange(half, dtype=np.float32) / half))).astype(np.float32)
    in_rope = (lane >= NOPE) & (lane < NOPE + ROPE)
    invf = jnp.asarray(np.where(in_rope, inv_freq[(lane - NOPE) % half], 0.0).astype(np.float32))
    m1 = jnp.asarray(np.where((lane >= NOPE) & (lane < NOPE + half), -1.0, 0.0).astype(np.float32))
    m2 = jnp.asarray(np.where((lane >= NOPE + half) & (lane < NOPE + ROPE), 1.0, 0.0).astype(np.float32))
    row = lambda a: jnp.pad(a.astype(F32), (0, D_MODEL - a.shape[0]))
    pvec = jnp.stack([row(q_norm_g), row(kv_norm_g), row(sgu_norm_g), row(sgu_norm_b), row(invf), row(m1),
                      row(m2), row(ln_g), row(ln_b)] + [jnp.zeros((D_MODEL,), F32)] * (PV_ROWS - 9))
    tri = jnp.tril(jnp.ones((CHUNK, CHUNK), dtype=bool))
    wt = jnp.where(tri[None], w_spatial, 0.0).astype(BF16)
    wtt = jnp.swapaxes(wt, 1, 2)
    bsp = jnp.repeat(b_spatial.T, VDIM, axis=1)

    cq, ckv, gate, q, k, v, vt, cs = _fwd_pre(x2, pos, win, wuq, wkv, pvec)
    o, lse = _attn_fwd(q, k, vt)
    dh2, do, dgate, g_wout, g_wsp, gvec = _post(x2, tgt, o, gate, wout, pvec, wt, wtt, bsp)
    dq, dk, dv = _attn_bwd(q, k, v, do, o, lse, cs, pvec)
    gx, g_win, g_wuq, g_wkv, gvec = _bwd_pre(x2, dh2, cq, ckv, cs, dq, dk, dv, dgate, win, wuq, wkv, pvec, gvec)

    cw = w_in.shape[1]
    first = D_INR - 3 * cw
    g_win_0 = jnp.concatenate([g_win[:, 0:C_KR], g_win[:, C_KR + NOPE:C_KR + NOPE + ROPE], g_win[:, C_GATE:first]],
                              axis=1)
    g_win_b = jnp.stack([g_win_0] + [g_win[:, first + cw * jb:first + cw * (jb + 1)] for jb in range(3)])
    g_wuq_f = g_wuq.reshape(Q_LORA, HEADS, HP)[:, :, 0:NOPE + ROPE].reshape(Q_LORA, HEADS * (NOPE + ROPE))
    g_k = g_wkv[:, 0:HEADS * HP].reshape(KV_LORA, HEADS, HP)[:, :, 0:NOPE]
    g_v = g_wkv[:, HEADS * HP:].reshape(KV_LORA, HEADS, VDIM)
    g_wukv_f = jnp.concatenate([g_k, g_v], axis=2).reshape(KV_LORA, HEADS * (NOPE + VDIM))

    def by_chip(a):
        rows, cols = a.shape[0], a.shape[1] // 4
        return jnp.swapaxes(a.reshape(rows, 4, cols), 0, 1).reshape(4, 2, rows // 2, cols)

    gs = [g_win_b.reshape(4, 2, D_MODEL // 2, cw), by_chip(g_wuq_f), by_chip(g_wukv_f), g_wout.reshape(4, 2, 128, D_MODEL),
          g_wsp.reshape(4, 2, CHUNK, CHUNK)]
    r_in, r_uq, r_ukv, r_out, r_wsp, r_vec = _grad_reduce(gs, gvec)

    big = [w_in, w_uq, w_ukv, w_out, w_spatial]
    flip = lambda a: a.T if a.ndim == 2 and a.shape[1] % LANES else a
    flips = lambda arrs: [flip(a) for a in arrs]
    g_big = flips([r_in.reshape(w_in.shape), r_uq.reshape(w_uq.shape), r_ukv.reshape(w_ukv.shape),
                   r_out.reshape(w_out.shape), r_wsp.reshape(w_spatial.shape)])
    small = lambda qg, kvg, sg, sb, lng, lnb, bs: [qg.reshape(1, -1), kvg.reshape(1, -1), sg.reshape(1, -1),
                                                   sb.reshape(1, -1), lng.reshape(1, -1), lnb.reshape(1, -1), bs]
    res = _adamw(g_big, flips(big), flips([m_w_in, m_w_uq, m_w_ukv, m_w_out, m_w_spatial]),
                 flips([v_w_in, v_w_uq, v_w_ukv, v_w_out, v_w_spatial]), r_vec,
                 small(q_norm_g, kv_norm_g, sgu_norm_g, sgu_norm_b, ln_g, ln_b, b_spatial),
                 small(m_q_norm_g, m_kv_norm_g, m_sgu_norm_g, m_sgu_norm_b, m_ln_g, m_ln_b, m_b_spatial),
                 small(v_q_norm_g, v_kv_norm_g, v_sgu_norm_g, v_sgu_norm_b, v_ln_g, v_ln_b, v_b_spatial))
    res = [r.T if n < 4 * len(big) and r.shape != big[n % len(big)].shape else r for n, r in enumerate(res)]

    def ordered(big, sm):
        vec = lambda n: sm[n].reshape(-1)
        return [big[0], vec(0), big[1], vec(1), big[2], vec(2), vec(3), big[4], sm[6], big[3], vec(4), vec(5)]

    loss = r_vec[GV_LOSS, 0]
    return (loss, gx.reshape(1, seq, D_MODEL), *ordered(res[0:5], res[20:27]), *ordered(res[5:10], res[27:34]),
            *ordered(res[10:15], res[34:41]), *ordered(res[15:20], res[41:48]))
```

```python
import math

import jax
import jax.numpy as jnp
import numpy as np
from jax import lax
from jax.experimental import pallas as pl
from jax.experimental.pallas import tpu as pltpu

F32 = jnp.float32
BF16 = jnp.bfloat16

D_MODEL = 1024
Q_LORA = 256
KV_LORA = 128
HEADS = 8
NOPE = 64
ROPE = 32
VDIM = 64
MLA_W = HEADS * VDIM
GW = 512
CHUNK = 128
HP = 128
PAIRS = HEADS // 2
D_IN = 2464
D_INR = 2560
C_CKV = Q_LORA
C_KR = Q_LORA + KV_LORA
C_GATE = C_KR + HP
ROPE_THETA = 10000.0
DN_ALPHA = 2.0 ** 0.25
EPS = 1e-5
SCALE = 1.0 / math.sqrt(NOPE + ROPE)
SCALE_LOG2E = SCALE * 1.4426950408889634
INV_SQRT2 = 0.7071067811865476
INV_SQRT_2PI = 0.3989422804014327

ADAM_LR = 0.001
ADAM_B1 = 0.9
ADAM_B2 = 0.999
ADAM_EPS = 1e-08
ADAM_WD = 0.01
ADAM_STEP = 10

PV_QG, PV_KVG, PV_SG, PV_SB, PV_INVF, PV_M1, PV_M2, PV_LNG, PV_LNB = range(9)
PV_ROWS = 16
GV_QG, GV_KVG, GV_SG, GV_SB, GV_LNG, GV_LNB, GV_LOSS = range(7)
GV_BSP = 8
GV_ROWS = 16

MESH = pl.DeviceIdType.MESH

FWD_TILE = 1024
POST_TILE = 512
BWD_TILE = 512
ATT_BLK = 512
ADAM_STEPS = 4
SUBLANES, LANES = 8, 128
VMEM_LIMIT = 60 * 1024 * 1024


def _dot(a, b):
    return jnp.dot(a, b, preferred_element_type=F32)


def _dot_nt(a, b):
    return lax.dot_general(a, b, (((1,), (1,)), ((), ())), preferred_element_type=F32)


def _dot_tn(a, b):
    return lax.dot_general(a, b, (((0,), (0,)), ((), ())), preferred_element_type=F32)


def _sigmoid(z):
    return pl.reciprocal(1.0 + jnp.exp(-z), approx=True)


def _gelu_and_grad(x):
    cdf = 0.5 * (1.0 + lax.erf(x * INV_SQRT2))
    return x * cdf, cdf + x * (INV_SQRT_2PI * jnp.exp(-0.5 * x * x))


def _rms_stats(x):
    r = lax.rsqrt(jnp.mean(x * x, axis=-1, keepdims=True) + EPS)
    return x * r, r


def _rms_bwd(dy, g, xh, r):
    dyg = dy * g
    return r * (dyg - xh * jnp.mean(dyg * xh, axis=-1, keepdims=True))


def _ln_stats(x):
    mu = jnp.mean(x, axis=-1, keepdims=True)
    xc = x - mu
    r = lax.rsqrt(jnp.mean(xc * xc, axis=-1, keepdims=True) + EPS)
    return xc * r, r


def _ln_bwd(dy, g, xh, r):
    dxh = dy * g
    return r * (dxh - jnp.mean(dxh, axis=-1, keepdims=True) - xh * jnp.mean(dxh * xh, axis=-1, keepdims=True))


def _rope_fwd(t, c, s1, s2):
    return t * c + pltpu.roll(t, HP - 16, 1) * s1 + pltpu.roll(t, 16, 1) * s2


def _rope_bwd(d, c, s1, s2):
    return d * c + pltpu.roll(d * s1, 16, 1) + pltpu.roll(d * s2, HP - 16, 1)


def _lane_lt64(shape):
    return lax.broadcasted_iota(jnp.int32, shape, len(shape) - 1) < 64


def _spatial_mix(w_ref, src, dst_ref, rows):
    for c in range(rows // CHUNK):
        for p in range(PAIRS):
            blk = src[c * CHUNK:(c + 1) * CHUNK, p * HP:(p + 1) * HP]
            a = _dot(w_ref[2 * p], blk)
            b = _dot(w_ref[2 * p + 1], blk)
            dst_ref[c * CHUNK:(c + 1) * CHUNK, p * HP:(p + 1) * HP] = jnp.where(_lane_lt64(a.shape), a, b)


def _gmlp_fwd(u_pre, v_pre, zb, sg, sb, wt_ref, bsp_ref, sv_ref, rows):
    u, du = _gelu_and_grad(u_pre)
    gv, dgv = _gelu_and_grad(v_pre)
    xh, r = _ln_stats(gv)
    vln = (xh * sg + sb).astype(BF16)
    _spatial_mix(wt_ref, vln, sv_ref, rows)
    bias = bsp_ref[...]
    svb = sv_ref[...] + jnp.concatenate([bias] * (rows // CHUNK), axis=0)
    sig = _sigmoid(zb)
    return u, du, dgv, xh, r, vln, svb, sig


def _weight_gather(shards):
    n_arr = len(shards)
    w_in, w_uq, w_ukv, _ = shards
    cw = w_in.shape[1]
    gate0 = C_KR + ROPE
    rc = 128

    def body(*refs):
        ins = refs[0:n_arr]
        win_o, wuq_o, wkv_o, out_o = refs[n_arr:n_arr + 4]
        a_in, a_uq, a_ukv, send_sems, recv_sems = refs[n_arr + 4:]
        outs = [a_in, a_uq, a_ukv, out_o]
        x, y, c = lax.axis_index("x"), lax.axis_index("y"), lax.axis_index("c")
        j = 2 * x + y
        sib = (x, y, 1 - c)
        chips = [(1 - x, y), (x, 1 - y), (1 - x, 1 - y)]
        for n in range(n_arr):
            outs[n][j] = ins[n][...].astype(BF16)

        def half(n, blk, core):
            r = shards[n].shape[0] // 2
            return outs[n].at[blk, pl.ds(pl.multiple_of(core * r, 16), r), :]

        def copy(k, ref, to):
            return pltpu.make_async_remote_copy(
                src_ref=ref, dst_ref=ref, send_sem=send_sems.at[k], recv_sem=recv_sems.at[k],
                device_id=to, device_id_type=MESH)

        first = [copy(6 * n + kk, half(n, j, c), (px, py, c))
                 for n in range(n_arr) for kk, (px, py) in enumerate(chips)]
        for cp in first:
            cp.start()
        passed = []
        for n in range(n_arr):
            for kk, (px, py) in enumerate(chips):
                landed = half(n, 2 * px + py, c)
                copy(6 * n + kk, landed, (px, py, c)).wait_recv()
                passed.append(copy(6 * n + 3 + kk, landed, sib))
                passed[-1].start()
        for n in range(n_arr):
            for kk, (px, py) in enumerate(chips):
                copy(6 * n + 3 + kk, half(n, 2 * px + py, 1 - c), sib).wait_recv()

        def lay_in(r, carry):
            rows = pl.ds(pl.multiple_of(r * rc, rc), rc)
            win_o[rows, 0:C_KR] = a_in[0, rows, 0:C_KR]
            win_o[rows, C_KR:C_KR + NOPE] = jnp.zeros((rc, NOPE), BF16)
            win_o[rows, C_KR + NOPE:C_KR + NOPE + ROPE] = a_in[0, rows, C_KR:gate0]
            win_o[rows, C_KR + NOPE + ROPE:C_GATE] = jnp.zeros((rc, HP - NOPE - ROPE), BF16)
            win_o[rows, C_GATE:C_GATE + cw - gate0] = a_in[0, rows, gate0:cw]
            for blk in range(1, 4):
                win_o[rows, D_INR - (4 - blk) * cw:D_INR - (3 - blk) * cw] = a_in[blk, rows, :]
            return carry

        lax.fori_loop(0, D_MODEL // rc, lay_in, 0)
        wuq_o[...] = jnp.zeros(wuq_o.shape, BF16)
        wkv_o[...] = jnp.zeros(wkv_o.shape, BF16)
        qw = NOPE + ROPE
        for h in range(HEADS):
            blk, e = h // 2, h % 2
            wuq_o[:, h * HP:h * HP + qw] = a_uq[blk, :, e * qw:(e + 1) * qw]
            wkv_o[:, h * HP:h * HP + NOPE] = a_ukv[blk, :, e * HP:e * HP + NOPE]
            wkv_o[:, HEADS * HP + h * VDIM:HEADS * HP + (h + 1) * VDIM] = a_ukv[blk, :, e * HP + NOPE:(e + 1) * HP]
        for cp in first + passed:
            cp.wait_send()

    assert cw > gate0 and C_GATE + 4 * cw - gate0 == D_INR and NOPE + VDIM == HP and HEADS == 8
    vmem = pl.BlockSpec(memory_space=pltpu.VMEM)
    gathered = lambda a: (4,) + a.shape
    return pl.pallas_call(
        body, name="weight_gather",
        out_shape=[jax.ShapeDtypeStruct((D_MODEL, D_INR), BF16), jax.ShapeDtypeStruct((Q_LORA, HEADS * HP), BF16),
                   jax.ShapeDtypeStruct((KV_LORA, HEADS * HP + MLA_W), BF16),
                   jax.ShapeDtypeStruct(gathered(shards[3]), BF16)],
        in_specs=[vmem] * n_arr, out_specs=[vmem] * 4,
        scratch_shapes=[pltpu.VMEM(gathered(a), BF16) for a in shards[0:3]]
        + [pltpu.SemaphoreType.DMA((6 * n_arr,)), pltpu.SemaphoreType.DMA((6 * n_arr,))],
        compiler_params=pltpu.CompilerParams(vmem_limit_bytes=VMEM_LIMIT),
    )(*shards)


def _fwd_pre(x, pos, win, wuq, wkv, pvec):
    seq = x.shape[0]
    t = FWD_TILE

    def body(x_ref, pos_ref, win_ref, wuq_ref, wkv_ref, pv_ref,
             cq_o, ckv_o, gate_o, q_o, k_o, v_o, vt_o, cs_o):
        xb = x_ref[...].astype(BF16)
        proj = _dot(xb, win_ref[:, 0:C_GATE])
        cq = proj[:, 0:C_CKV]
        ckv = proj[:, C_CKV:C_KR]
        kr = proj[:, C_KR:C_GATE]
        cq_o[...] = cq
        ckv_o[...] = ckv

        ang = pos_ref[...].astype(F32) * pv_ref[PV_INVF:PV_INVF + 1, 0:HP]
        cos = jnp.cos(ang)
        sin = jnp.sin(ang)
        cs_o[:, 0:HP] = cos
        cs_o[:, HP:2 * HP] = sin
        s1 = sin * pv_ref[PV_M1:PV_M1 + 1, 0:HP]
        s2 = sin * pv_ref[PV_M2:PV_M2 + 1, 0:HP]

        cqh, _ = _rms_stats(cq)
        q_all = _dot((cqh * pv_ref[PV_QG:PV_QG + 1, 0:Q_LORA]).astype(BF16), wuq_ref[...])
        ckvh, _ = _rms_stats(ckv)
        kv_all = _dot((ckvh * pv_ref[PV_KVG:PV_KVG + 1, 0:KV_LORA]).astype(BF16), wkv_ref[...])
        krr = _rope_fwd(kr, cos, s1, s2)
        for h in range(HEADS):
            sl = slice(h * HP, (h + 1) * HP)
            q_o[:, sl] = (_rope_fwd(q_all[:, sl], cos, s1, s2) * SCALE_LOG2E).astype(BF16)
            k_o[:, sl] = (kv_all[:, sl] + krr).astype(BF16)
        val = kv_all[:, HEADS * HP:].astype(BF16)
        v_o[...] = val
        for blk in range(t // ATT_BLK):
            vt_o[blk] = val[blk * ATT_BLK:(blk + 1) * ATT_BLK, :].T
        gate_o[...] = _dot(xb, win_ref[:, C_GATE:D_INR]).astype(BF16)

    tile = lambda w: pl.BlockSpec((t, w), lambda i: (i, 0))
    full = lambda a: pl.BlockSpec(a.shape, lambda i: (0,) * a.ndim)
    outs = [(Q_LORA, F32), (KV_LORA, F32), (2048, BF16), (HEADS * HP, BF16), (HEADS * HP, BF16), (MLA_W, BF16)]
    assert t % ATT_BLK == 0
    out_specs = [tile(w) for w, _ in outs]
    out_specs += [pl.BlockSpec((t // ATT_BLK, MLA_W, ATT_BLK), lambda i: (i, 0, 0)), tile(2 * HP)]
    out_shape = [jax.ShapeDtypeStruct((seq, w), d) for w, d in outs]
    out_shape += [jax.ShapeDtypeStruct((seq // ATT_BLK, MLA_W, ATT_BLK), BF16), jax.ShapeDtypeStruct((seq, 2 * HP), F32)]
    return pl.pallas_call(
        body, name="fwd_pre", grid=(seq // t,),
        in_specs=[tile(D_MODEL), tile(1), full(win), full(wuq), full(wkv), full(pvec)],
        out_specs=out_specs, out_shape=out_shape,
        compiler_params=pltpu.CompilerParams(dimension_semantics=("arbitrary",), vmem_limit_bytes=VMEM_LIMIT),
    )(x, pos, win, wuq, wkv, pvec)


def _attn_fwd(q, k, vt):
    seq = q.shape[0]
    b = ATT_BLK
    nq = seq // b
    assert nq % 2 == 0
    assert nq % 4 == 0
    n_wide = sum(i // 4 for i in range(nq))

    def body(q_ref, k_ref, vt_ref, o_o, lse_o, m_ref, l_ref, acc_ref, s_even, s_odd):
        m_ref[...] = jnp.full(m_ref.shape, -jnp.inf, F32)
        l_ref[...] = jnp.zeros(l_ref.shape, F32)
        acc_ref[...] = jnp.zeros(acc_ref.shape, F32)

        def scores(i, j, s_ref, nkb):
            qrows = pl.ds(pl.multiple_of(i * b, b), b)
            krows = pl.ds(pl.multiple_of(j * b, b), nkb * b)
            for a in range(2):
                s_ref[a, 0:nkb * b, :] = _dot_nt(k_ref[krows, a * HP:(a + 1) * HP], q_ref[qrows, a * HP:(a + 1) * HP])

        def consume(i, j, s_ref, nkb, masked):
            for a in range(2):
                st = s_ref[a, 0:nkb * b, :]
                if masked:
                    ki = lax.broadcasted_iota(jnp.int32, st.shape, 0)
                    qi = lax.broadcasted_iota(jnp.int32, st.shape, 1)
                    st = jnp.where(ki <= qi, st, -jnp.inf)
                m_prev = m_ref[i, a:a + 1, :]
                m_new = jnp.maximum(m_prev, jnp.max(st, axis=0, keepdims=True))
                alpha = jnp.exp2(m_prev - m_new)
                pt = jnp.exp2(st - m_new)
                ptb = pt.astype(BF16)
                l_ref[i, a:a + 1, :] = alpha * l_ref[i, a:a + 1, :] + jnp.sum(pt, axis=0, keepdims=True)
                pv = _dot(vt_ref[j], ptb[0:b, :])
                for kb in range(1, nkb):
                    pv = pv + _dot(vt_ref[j + kb], ptb[kb * b:(kb + 1) * b, :])
                acc_ref[i, a] = alpha * acc_ref[i, a] + pv
                m_ref[i, a:a + 1, :] = m_new

        def run(count, first, following, nkb, masked):
            if count == 0:
                return
            scores(*first, s_even, nkb)

            def two(u, ij):
                nxt = following(*ij)
                scores(*nxt, s_odd, nkb)
                consume(*ij, s_even, nkb, masked)
                nxt2 = following(*nxt)
                scores(*nxt2, s_even, nkb)
                consume(*nxt, s_odd, nkb, masked)
                return nxt2

            last = lax.fori_loop(0, count // 2, two, tuple(jnp.int32(c) for c in first))
            if count % 2:
                consume(*last, s_even, nkb, masked)

        def clamp(i):
            return jnp.minimum(i, nq - 1)

        def next_wide(i, j):
            wrap = j + 8 > i
            return clamp(jnp.where(wrap, i + 1, i)), jnp.where(wrap, 0, j + 4)

        def next_pair(i, j):
            low = lax.rem(i, 4) == 2
            return clamp(jnp.where(low, i + 1, i + 3)), jnp.minimum(jnp.where(low, j, j + 4), nq - 4)

        run(n_wide, (4, 0), next_wide, 4, False)
        run(nq // 2, (2, 0), next_pair, 2, False)
        run(nq // 2, (1, 0), lambda i, j: (clamp(i + 2), jnp.minimum(j + 2, nq - 2)), 1, False)
        run(nq, (0, 0), lambda i, j: (clamp(i + 1), clamp(j + 1)), 1, True)
        top = lax.broadcasted_iota(jnp.int32, (HP, b), 0) < 64

        def finish(i, carry):
            rows = pl.ds(pl.multiple_of(i * b, b), b)
            o_o[rows, :] = jnp.where(top, acc_ref[i, 0] / l_ref[i, 0:1, :], acc_ref[i, 1] / l_ref[i, 1:2, :]).T
            lse_o[i] = m_ref[i, 0:2, :] + jnp.log2(l_ref[i, 0:2, :])
            return carry

        lax.fori_loop(0, nq, finish, 0)

    return pl.pallas_call(
        body, name="attn_fwd", grid=(PAIRS,),
        in_specs=[pl.BlockSpec((seq, 2 * HP), lambda p: (0, p)),
                  pl.BlockSpec((seq, 2 * HP), lambda p: (0, p)),
                  pl.BlockSpec((nq, HP, b), lambda p: (0, p, 0))],
        out_specs=[pl.BlockSpec((seq, HP), lambda p: (0, p)),
                   pl.BlockSpec((None, nq, 2, b), lambda p: (p, 0, 0, 0))],
        out_shape=[jax.ShapeDtypeStruct((seq, MLA_W), F32),
                   jax.ShapeDtypeStruct((PAIRS, nq, 2, b), F32)],
        scratch_shapes=[pltpu.VMEM((nq, 8, b), F32), pltpu.VMEM((nq, 8, b), F32), pltpu.VMEM((nq, 2, HP, b), F32),
                        pltpu.VMEM((2, 4 * b, b), F32), pltpu.VMEM((2, 4 * b, b), F32)],
        compiler_params=pltpu.CompilerParams(dimension_semantics=("arbitrary",), vmem_limit_bytes=VMEM_LIMIT),
    )(q, k, vt)


def _post(x, tgt, o, gate, wout, pvec, wt, wtt, bsp):
    seq = x.shape[0]
    t = POST_TILE
    nt = seq // t

    def body(x_ref, tgt_ref, o_ref, gate_ref, wout_ref, pv_ref, wt_ref, wtt_ref, bsp_ref,
             dh2_o, do_o, dgate_o, gwout_o, gwsp_o, vec_o, sv_ref, dvln_ref, bacc_ref):
        i = pl.program_id(0)

        @pl.when(i == 0)
        def _():
            gwout_o[...] = jnp.zeros_like(gwout_o)
            gwsp_o[...] = jnp.zeros_like(gwsp_o)
            vec_o[...] = jnp.zeros_like(vec_o)
            bacc_ref[...] = jnp.zeros_like(bacc_ref)

        za = gate_ref[:, 0:512].astype(F32)
        u_pre = gate_ref[:, 512:1024].astype(F32)
        v_pre = gate_ref[:, 1024:1536].astype(F32)
        zb = gate_ref[:, 1536:2048].astype(F32)
        sg = pv_ref[PV_SG:PV_SG + 1, 0:GW]
        sb = pv_ref[PV_SB:PV_SB + 1, 0:GW]
        lng = pv_ref[PV_LNG:PV_LNG + 1, :]
        lnb = pv_ref[PV_LNB:PV_LNB + 1, :]
        o = o_ref[...]

        sig_a = _sigmoid(za)
        silu_a = za * sig_a
        u, du, dgv, xh, r, vln, svb, sig_b = _gmlp_fwd(u_pre, v_pre, zb, sg, sb, wt_ref, bsp_ref, sv_ref, t)
        silu_b = zb * sig_b
        sgu = u * svb
        merged = jnp.concatenate([o * silu_a, sgu * silu_b], axis=1).astype(BF16)
        h2 = DN_ALPHA * x_ref[...] + _dot(merged, wout_ref[...])
        xh2, r2 = _ln_stats(h2)
        err = xh2 * lng + lnb - tgt_ref[...]
        d_out = err * (1.0 / D_MODEL)
        vec_o[GV_LNG:GV_LNG + 1, :] += jnp.sum(d_out * xh2, axis=0, keepdims=True)
        vec_o[GV_LNB:GV_LNB + 1, :] += jnp.sum(d_out, axis=0, keepdims=True)
        vec_o[GV_LOSS:GV_LOSS + 1, :] += jnp.sum(err * err, axis=0, keepdims=True) * (0.5 / D_MODEL)

        d_h2 = _ln_bwd(d_out, lng, xh2, r2)
        dh2_o[...] = d_h2
        dh2b = d_h2.astype(BF16)
        gwout_o[...] += _dot_tn(merged, dh2b)
        d_m = _dot_nt(dh2b, wout_ref[...])
        d_oa = d_m[:, 0:512]
        d_ob = d_m[:, 512:1024]
        do_o[...] = (d_oa * silu_a).astype(BF16)
        dgate_o[:, 0:512] = (d_oa * o * (sig_a * (1.0 + za * (1.0 - sig_a)))).astype(BF16)
        dgate_o[:, 1536:2048] = (d_ob * sgu * (sig_b * (1.0 + zb * (1.0 - sig_b)))).astype(BF16)
        d_sgu = d_ob * silu_b
        dgate_o[:, 512:1024] = (d_sgu * svb * du).astype(BF16)
        d_sv = d_sgu * u
        acc = bacc_ref[...]
        for c in range(t // CHUNK):
            acc = acc + d_sv[c * CHUNK:(c + 1) * CHUNK, :]
        bacc_ref[...] = acc
        d_svb = d_sv.astype(BF16)
        for c in range(t // CHUNK):
            for p in range(PAIRS):
                blk = d_svb[c * CHUNK:(c + 1) * CHUNK, p * HP:(p + 1) * HP]
                vblk = vln[c * CHUNK:(c + 1) * CHUNK, p * HP:(p + 1) * HP]
                first = _lane_lt64(blk.shape)
                gwsp_o[2 * p] += _dot_nt(jnp.where(first, blk, jnp.zeros_like(blk)), vblk)
                gwsp_o[2 * p + 1] += _dot_nt(jnp.where(first, jnp.zeros_like(blk), blk), vblk)
        _spatial_mix(wtt_ref, d_svb, dvln_ref, t)
        d_vln = dvln_ref[...]
        vec_o[GV_SG:GV_SG + 1, 0:GW] += jnp.sum(d_vln * xh, axis=0, keepdims=True)
        vec_o[GV_SB:GV_SB + 1, 0:GW] += jnp.sum(d_vln, axis=0, keepdims=True)
        dgate_o[:, 1024:1536] = (_ln_bwd(d_vln, sg, xh, r) * dgv).astype(BF16)


        @pl.when(i == nt - 1)
        def _():
            tri = (lax.broadcasted_iota(jnp.int32, (CHUNK, CHUNK), 1)
                   <= lax.broadcasted_iota(jnp.int32, (CHUNK, CHUNK), 0))
            for h in range(HEADS):
                gwsp_o[h] = jnp.where(tri, gwsp_o[h], 0.0)
            lane = lax.broadcasted_iota(jnp.int32, (CHUNK, HP), 1)
            res = jnp.zeros((CHUNK, HP), F32)
            for h in range(HEADS):
                p, a = divmod(h, 2)
                blk = bacc_ref[:, p * HP:(p + 1) * HP]
                part = jnp.where(_lane_lt64(blk.shape) == (a == 0), blk, 0.0)
                res = jnp.where(lane == h, jnp.sum(part, axis=-1, keepdims=True), res)
            vec_o[GV_BSP:GV_BSP + HEADS, 0:HP] = res.T[0:HEADS, :]
            lane1 = lax.broadcasted_iota(jnp.int32, (1, D_MODEL), 1)
            total = jnp.sum(vec_o[GV_LOSS:GV_LOSS + 1, :], axis=-1, keepdims=True)
            vec_o[GV_LOSS:GV_LOSS + 1, :] = jnp.where(lane1 == 0, total, 0.0)

    tile = lambda w: pl.BlockSpec((t, w), lambda i: (i, 0))
    full = lambda a: pl.BlockSpec(a.shape, lambda i: (0,) * a.ndim)
    const = lambda s: pl.BlockSpec(s, lambda i: (0,) * len(s))
    return pl.pallas_call(
        body, name="post", grid=(nt,),
        in_specs=[tile(D_MODEL), tile(D_MODEL), tile(MLA_W), tile(2048), full(wout), full(pvec),
                  full(wt), full(wtt), full(bsp)],
        out_specs=[tile(D_MODEL), tile(MLA_W), tile(2048), const((D_MODEL, D_MODEL)),
                   const((HEADS, CHUNK, CHUNK)), const((GV_ROWS, D_MODEL))],
        out_shape=[jax.ShapeDtypeStruct((seq, D_MODEL), F32), jax.ShapeDtypeStruct((seq, MLA_W), BF16),
                   jax.ShapeDtypeStruct((seq, 2048), BF16), jax.ShapeDtypeStruct((D_MODEL, D_MODEL), F32),
                   jax.ShapeDtypeStruct((HEADS, CHUNK, CHUNK), F32), jax.ShapeDtypeStruct((GV_ROWS, D_MODEL), F32)],
        scratch_shapes=[pltpu.VMEM((t, GW), F32), pltpu.VMEM((t, GW), F32), pltpu.VMEM((CHUNK, GW), F32)],
        compiler_params=pltpu.CompilerParams(dimension_semantics=("arbitrary",), vmem_limit_bytes=VMEM_LIMIT),
    )(x, tgt, o, gate, wout, pvec, wt, wtt, bsp)


def _attn_bwd(q, k, v, do, o, lse, cs, pvec):
    seq = q.shape[0]
    b = ATT_BLK
    nq = seq // b

    def body(q_ref, k_ref, v_ref, do_ref, o_ref, lse_ref, cs_ref, pv_ref, dq_o, dk_o, dv_o, dk_acc, dv_acc):
        i = pl.program_id(1)

        @pl.when(i == 0)
        def _():
            dk_acc[...] = jnp.zeros_like(dk_acc)
            dv_acc[...] = jnp.zeros_like(dv_acc)

        first = _lane_lt64((b, HP))
        do = do_ref[...]
        zero = jnp.zeros_like(do)
        dos = [jnp.where(first, do, zero), jnp.where(first, zero, do)]
        prod_t = (do.astype(F32) * o_ref[...]).T
        deltas = [jnp.sum(prod_t[0:64, :], axis=0, keepdims=True),
                  jnp.sum(prod_t[64:128, :], axis=0, keepdims=True)]
        lses = [lse_ref[0:1, :], lse_ref[1:2, :]]
        qs = [q_ref[:, a * HP:(a + 1) * HP] for a in range(2)]

        def step(j, dqs, masked, nk=b):
            rows = pl.ds(pl.multiple_of(j * b, b), nk)
            vb = v_ref[rows, :]
            new_dq = []
            dvs = []
            for a in range(2):
                kb = k_ref[rows, a * HP:(a + 1) * HP]
                pt = jnp.exp2(_dot_nt(kb, qs[a]) - lses[a])
                if masked:
                    ki = lax.broadcasted_iota(jnp.int32, pt.shape, 0)
                    qi = lax.broadcasted_iota(jnp.int32, pt.shape, 1) + (nk - b)
                    pt = jnp.where(ki <= qi, pt, 0.0)
                dvs.append(_dot(pt.astype(BF16), do))
                dpt = _dot_nt(vb, dos[a])
                dst = (pt * (dpt - deltas[a])).astype(BF16)
                dk_acc[rows, a * HP:(a + 1) * HP] += _dot(dst, qs[a])
                new_dq.append(dqs[a] + _dot_tn(dst, kb))
            dv_acc[rows, :] += jnp.where(_lane_lt64((nk, HP)), dvs[0], dvs[1])
            return tuple(new_dq)

        init = (jnp.zeros((b, HP), F32), jnp.zeros((b, HP), F32))
        dqs = lax.fori_loop(0, i // 4, lambda jj, cr: step(4 * jj, cr, False, 4 * b), init)
        last = [lambda cr, w=w: step(4 * (i // 4), cr, True, w * b) for w in (1, 2, 3, 4)]
        dqs = lax.switch(i % 4, last, dqs)
        cos = cs_ref[:, 0:HP]
        sin = cs_ref[:, HP:2 * HP]
        s1 = sin * pv_ref[PV_M1:PV_M1 + 1, 0:HP]
        s2 = sin * pv_ref[PV_M2:PV_M2 + 1, 0:HP]
        for a in range(2):
            dq_o[:, a * HP:(a + 1) * HP] = _rope_bwd(dqs[a] * SCALE, cos, s1, s2).astype(BF16)

        @pl.when(i == nq - 1)
        def _():
            dk_o[...] = (dk_acc[...] * (SCALE / SCALE_LOG2E)).astype(BF16)
            dv_o[...] = dv_acc[...].astype(BF16)

    return pl.pallas_call(
        body, name="attn_bwd", grid=(PAIRS, nq),
        in_specs=[pl.BlockSpec((b, 2 * HP), lambda p, i: (i, p)),
                  pl.BlockSpec((seq, 2 * HP), lambda p, i: (0, p)),
                  pl.BlockSpec((seq, HP), lambda p, i: (0, p)),
                  pl.BlockSpec((b, HP), lambda p, i: (i, p)),
                  pl.BlockSpec((b, HP), lambda p, i: (i, p)),
                  pl.BlockSpec((None, None, 2, b), lambda p, i: (p, i, 0, 0)),
                  pl.BlockSpec((b, 2 * HP), lambda p, i: (i, 0)),
                  pl.BlockSpec(pvec.shape, lambda p, i: (0, 0))],
        out_specs=[pl.BlockSpec((b, 2 * HP), lambda p, i: (i, p)),
                   pl.BlockSpec((seq, 2 * HP), lambda p, i: (0, p)),
                   pl.BlockSpec((seq, HP), lambda p, i: (0, p))],
        out_shape=[jax.ShapeDtypeStruct((seq, HEADS * HP), BF16),
                   jax.ShapeDtypeStruct((seq, HEADS * HP), BF16),
                   jax.ShapeDtypeStruct((seq, MLA_W), BF16)],
        scratch_shapes=[pltpu.VMEM((seq, 2 * HP), F32), pltpu.VMEM((seq, HP), F32)],
        compiler_params=pltpu.CompilerParams(dimension_semantics=("arbitrary", "arbitrary"),
                                             vmem_limit_bytes=VMEM_LIMIT),
    )(q, k, v, do, o, lse, cs, pvec)


def _bwd_pre(x, dh2, cq, ckv, cs, dq, dk, dv, dgate, win, wuq, wkv, pvec, gvec):
    seq = x.shape[0]
    t = BWD_TILE

    def body(x_ref, dh2_ref, cq_ref, ckv_ref, cs_ref, dq_ref, dk_ref, dv_ref, dgate_ref,
             win_ref, wuq_ref, wkv_ref, pv_ref, gv_ref, gx_o, gwin_o, gwuq_o, gwkv_o, vec_o):
        i = pl.program_id(0)

        @pl.when(i == 0)
        def _():
            gwin_o[...] = jnp.zeros_like(gwin_o)
            gwuq_o[...] = jnp.zeros_like(gwuq_o)
            gwkv_o[...] = jnp.zeros_like(gwkv_o)
            vec_o[...] = gv_ref[...]

        xb = x_ref[...].astype(BF16)
        dgate = dgate_ref[...]
        gwin_o[:, C_GATE:D_INR] += _dot_tn(xb, dgate)
        gx_gate = _dot_nt(dgate, win_ref[:, C_GATE:D_INR])

        qg = pv_ref[PV_QG:PV_QG + 1, 0:Q_LORA]
        kvg = pv_ref[PV_KVG:PV_KVG + 1, 0:KV_LORA]
        dq = dq_ref[...]
        cqh, rq = _rms_stats(cq_ref[...])
        d_cqn = _dot_nt(dq, wuq_ref[...])
        gwuq_o[...] += _dot_tn((cqh * qg).astype(BF16), dq)
        vec_o[GV_QG:GV_QG + 1, 0:Q_LORA] += jnp.sum(d_cqn * cqh, axis=0, keepdims=True)
        d_cq = _rms_bwd(d_cqn, qg, cqh, rq)

        dk = dk_ref[...]
        dkv = jnp.concatenate([dk, dv_ref[...]], axis=1)
        ckvh, rkv = _rms_stats(ckv_ref[...])
        d_ckvn = _dot_nt(dkv, wkv_ref[...])
        gwkv_o[...] += _dot_tn((ckvh * kvg).astype(BF16), dkv)
        vec_o[GV_KVG:GV_KVG + 1, 0:KV_LORA] += jnp.sum(d_ckvn * ckvh, axis=0, keepdims=True)
        d_ckv = _rms_bwd(d_ckvn, kvg, ckvh, rkv)

        dks = dk[:, 0:HP].astype(F32)
        for h in range(1, HEADS):
            dks = dks + dk[:, h * HP:(h + 1) * HP].astype(F32)
        cos = cs_ref[:, 0:HP]
        sin = cs_ref[:, HP:2 * HP]
        d_kr = _rope_bwd(dks, cos, sin * pv_ref[PV_M1:PV_M1 + 1, 0:HP], sin * pv_ref[PV_M2:PV_M2 + 1, 0:HP])

        d_lat = jnp.concatenate([d_cq.astype(BF16), d_ckv.astype(BF16), d_kr.astype(BF16)], axis=1)
        gwin_o[:, 0:C_GATE] += _dot_tn(xb, d_lat)
        gx_o[...] = DN_ALPHA * dh2_ref[...] + gx_gate + _dot_nt(d_lat, win_ref[:, 0:C_GATE])

    tile = lambda w: pl.BlockSpec((t, w), lambda i: (i, 0))
    full = lambda a: pl.BlockSpec(a.shape, lambda i: (0,) * a.ndim)
    const = lambda s: pl.BlockSpec(s, lambda i: (0,) * len(s))
    return pl.pallas_call(
        body, name="bwd_pre", grid=(seq // t,),
        in_specs=[tile(D_MODEL), tile(D_MODEL), tile(Q_LORA), tile(KV_LORA), tile(2 * HP), tile(HEADS * HP),
                  tile(HEADS * HP), tile(MLA_W), tile(2048), full(win), full(wuq), full(wkv), full(pvec), full(gvec)],
        out_specs=[tile(D_MODEL), const((D_MODEL, D_INR)), const((Q_LORA, HEADS * HP)),
                   const((KV_LORA, HEADS * HP + MLA_W)), const((GV_ROWS, D_MODEL))],
        out_shape=[jax.ShapeDtypeStruct((seq, D_MODEL), F32), jax.ShapeDtypeStruct((D_MODEL, D_INR), F32),
                   jax.ShapeDtypeStruct((Q_LORA, HEADS * HP), F32),
                   jax.ShapeDtypeStruct((KV_LORA, HEADS * HP + MLA_W), F32),
                   jax.ShapeDtypeStruct((GV_ROWS, D_MODEL), F32)],
        compiler_params=pltpu.CompilerParams(dimension_semantics=("arbitrary",), vmem_limit_bytes=VMEM_LIMIT),
    )(x, dh2, cq, ckv, cs, dq, dk, dv, dgate, win, wuq, wkv, pvec, gvec)


def _grad_reduce(g_win, g_wuq, g_wkv, g_wout, g_wsp, gvec):
    gate0 = C_KR + ROPE
    cw = (D_INR - C_GATE + gate0) // 4
    qw = NOPE + ROPE
    rc = 128
    half_shapes = [(D_MODEL // 2, cw), (Q_LORA // 2, 2 * qw), (KV_LORA // 2, 2 * HP), g_wout.shape[2:], g_wsp.shape[2:]]
    n_arr = len(half_shapes)
    n_lay = 3
    order = [3, 0, 1, 2, 4]
    n_big = n_arr - 1
    k1 = lambda n, blk: 4 * n + blk
    k2 = lambda n, kk: 4 * n_arr + 3 * n + kk
    k3 = lambda n: 7 * n_arr + n
    k3w = lambda k: 7 * n_arr + n_big + k
    kv = lambda k: 7 * n_arr + n_big + 7 + k
    n_sem = 7 * n_arr + n_big + 14

    def body(*refs):
        gwin, gwuq, gwkv = refs[0:n_lay]
        gv = refs[n_arr]
        outs, ov = refs[n_arr + 1:2 * n_arr + 1], refs[2 * n_arr + 1]
        r1 = refs[2 * n_arr + 2:3 * n_arr + 2]
        r2 = refs[3 * n_arr + 2:4 * n_arr + 2]
        s2 = refs[4 * n_arr + 2:5 * n_arr + 2]
        g = refs[5 * n_arr + 2:5 * n_arr + 2 + n_lay] + refs[n_lay:n_arr]
        vbuf, send_sems, recv_sems = refs[5 * n_arr + 2 + n_lay:]
        x, y, c = lax.axis_index("x"), lax.axis_index("y"), lax.axis_index("c")
        j = 2 * x + y
        me = 2 * j + c
        sib = (x, y, 1 - c)
        chips = [(1 - x, y), (x, 1 - y), (1 - x, 1 - y)]
        others = [sib] + [(px, py, pc) for (px, py) in chips for pc in (c, 1 - c)]

        def copy(k, src, dst, to):
            return pltpu.make_async_remote_copy(
                src_ref=src, dst_ref=dst, send_sem=send_sems.at[k], recv_sem=recv_sems.at[k],
                device_id=to, device_id_type=MESH)

        def lay_in(r, carry):
            src = pl.ds(pl.multiple_of(r * rc, rc), rc)
            hf = r // (D_MODEL // 2 // rc)
            dst = pl.ds(pl.multiple_of((r % (D_MODEL // 2 // rc)) * rc, rc), rc)
            g[0][0, hf, dst, 0:C_KR] = gwin[src, 0:C_KR]
            g[0][0, hf, dst, C_KR:gate0] = gwin[src, C_KR + NOPE:C_KR + NOPE + ROPE]
            g[0][0, hf, dst, gate0:cw] = gwin[src, C_GATE:C_GATE + cw - gate0]
            for blk in range(1, 4):
                g[0][blk, hf, dst, :] = gwin[src, D_INR - (4 - blk) * cw:D_INR - (3 - blk) * cw]
            return carry

        def lay_heads():
            for h in range(HEADS):
                blk, e = h // 2, h % 2
                for hf in range(2):
                    rq = slice(hf * Q_LORA // 2, (hf + 1) * Q_LORA // 2)
                    rk = slice(hf * KV_LORA // 2, (hf + 1) * KV_LORA // 2)
                    g[1][blk, hf, :, e * qw:(e + 1) * qw] = gwuq[rq, h * HP:h * HP + qw]
                    g[2][blk, hf, :, e * HP:e * HP + NOPE] = gwkv[rk, h * HP:h * HP + NOPE]
                    g[2][blk, hf, :, e * HP + NOPE:(e + 1) * HP] = gwkv[rk, HEADS * HP + h * VDIM:HEADS * HP + (h + 1) * VDIM]

        l1, l2 = [], []

        def level1(n):
            for blk in range(4):
                l1.append(copy(k1(n, blk), g[n].at[blk, 1 - c], r1[n].at[blk], sib))
                l1[-1].start()

        def level2(n):
            for blk in range(4):
                copy(k1(n, blk), g[n].at[blk, c], r1[n].at[blk], sib).wait_recv()
            for blk in range(4):
                r1[n][blk] = g[n][blk, c] + r1[n][blk]
                s2[n][blk] = r1[n][blk].astype(BF16)
            for kk, (px, py) in enumerate(chips):
                l2.append(copy(k2(n, kk), s2[n].at[2 * px + py], r2[n].at[kk], (px, py, c)))
                l2[-1].start()

        lv = [copy(kv(k), gv, vbuf.at[me], to) for k, to in enumerate(others)]
        for cp in lv:
            cp.start()
        for n in range(n_lay, n_arr):
            level1(n)
        lay_heads()
        level1(1)
        level1(2)
        level2(order[0])
        lax.fori_loop(0, D_MODEL // rc, lay_in, 0)
        level1(0)
        for n in order[1:]:
            level2(n)

        l3 = []
        for n in order:
            for kk in range(3):
                copy(k2(n, kk), s2[n].at[0], r2[n].at[kk], sib).wait_recv()
            red = ((r1[n][j] + r2[n][0].astype(F32)) + r2[n][1].astype(F32)) + r2[n][2].astype(F32)
            if n < n_big:
                outs[n][c] = red
                back = [copy(k3(n), outs[n].at[c], outs[n].at[c], sib)]
            else:
                outs[n][j, c] = red
                back = [copy(k3w(k), outs[n].at[j, c], outs[n].at[j, c], to) for k, to in enumerate(others)]
            for cp in back:
                cp.start()
            l3 += back
        for n in range(n_big):
            copy(k3(n), outs[n].at[1 - c], outs[n].at[1 - c], sib).wait_recv()
        for k, (px, py, pc) in enumerate(others):
            landed = outs[n_big].at[2 * px + py, pc]
            copy(k3w(k), landed, landed, (px, py, pc)).wait_recv()
            copy(kv(k), gv, vbuf.at[4 * px + 2 * py + pc], (px, py, pc)).wait_recv()
        vbuf[me] = gv[...]
        total = vbuf[0]
        for d in range(1, 8):
            total = total + vbuf[d]
        ov[...] = total
        for cp in l1 + lv + l2 + l3:
            cp.wait_send()

    vmem = pl.BlockSpec(memory_space=pltpu.VMEM)
    assert g_win.shape == (D_MODEL, D_INR) and 4 * cw == D_INR - C_GATE + gate0 and NOPE + VDIM == HP and HEADS == 8
    out_shape =[jax.ShapeDtypeStruct((2,) + s, F32) for s in half_shapes[:n_big]]
    out_shape += [jax.ShapeDtypeStruct((4, 2) + half_shapes[n_big], F32), jax.ShapeDtypeStruct(gvec.shape, F32)]
    scratch = [pltpu.VMEM((4,) + s, F32) for s in half_shapes] + [pltpu.VMEM((3,) + s, BF16) for s in half_shapes]
    scratch += [pltpu.VMEM((4,) + s, BF16) for s in half_shapes]
    scratch += [pltpu.VMEM((4, 2) + s, F32) for s in half_shapes[:n_lay]]
    scratch += [pltpu.VMEM((8,) + gvec.shape, F32), pltpu.SemaphoreType.DMA((n_sem,)), pltpu.SemaphoreType.DMA((n_sem,))]
    return pl.pallas_call(
        body, name="grad_reduce", out_shape=out_shape,
        in_specs=[vmem] * (n_arr + 1), out_specs=[vmem] * (n_arr + 1), scratch_shapes=scratch,
        compiler_params=pltpu.CompilerParams(vmem_limit_bytes=VMEM_LIMIT),
    )(g_win, g_wuq, g_wkv, g_wout, g_wsp, gvec)


SMALL_ROWS = ((GV_QG, 1, Q_LORA), (GV_KVG, 1, KV_LORA), (GV_SG, 1, GW), (GV_SB, 1, GW),
              (GV_LNG, 1, D_MODEL), (GV_LNB, 1, D_MODEL), (GV_BSP, HEADS, CHUNK))


def _adam_update(g, w, m, v):
    m_new = ADAM_B1 * m + (1.0 - ADAM_B1) * g
    v_new = ADAM_B2 * v + (1.0 - ADAM_B2) * (g * g)
    m_hat = m_new / (1.0 - ADAM_B1 ** ADAM_STEP)
    v_hat = v_new / (1.0 - ADAM_B2 ** ADAM_STEP)
    return -ADAM_LR * (m_hat / (jnp.sqrt(v_hat) + ADAM_EPS) + ADAM_WD * w), m_new, v_new


def _adamw(g_big, w_big, m_big, v_big, gvec, w_small, m_small, v_small):
    nb, ns = len(g_big), len(w_small)

    def body(*refs):
        it = iter(refs)
        take = lambda n: [next(it) for _ in range(n)]
        g_b, w_b, m_b, v_b = take(nb), take(nb), take(nb), take(nb)
        gv = next(it)
        w_s, m_s, v_s = take(ns), take(ns), take(ns)
        g_bo, d_bo, m_bo, v_bo = take(nb), take(nb), take(nb), take(nb)
        g_so, d_so, m_so, v_so = take(ns), take(ns), take(ns), take(ns)
        for n in range(nb):
            gb = g_b[n][...]
            g_bo[n][...] = gb
            d_bo[n][...], m_bo[n][...], v_bo[n][...] = _adam_update(gb, w_b[n][...], m_b[n][...], v_b[n][...])
        for n, (row, nrow, width) in enumerate(SMALL_ROWS):
            gs = gv[row:row + nrow, 0:width]
            g_so[n][...] = gs
            d_so[n][...], m_so[n][...], v_so[n][...] = _adam_update(gs, w_s[n][...], m_s[n][...], v_s[n][...])

    def part(a):
        nd = a.ndim
        if nd > 2 or a.shape[0] % (SUBLANES * ADAM_STEPS) == 0:
            return pl.BlockSpec((a.shape[0] // ADAM_STEPS,) + a.shape[1:], lambda i: (i,) + (0,) * (nd - 1))
        assert a.shape[1] % (LANES * ADAM_STEPS) == 0
        return pl.BlockSpec((a.shape[0], a.shape[1] // ADAM_STEPS), lambda i: (0, i))

    def whole(a):
        nd = a.ndim
        return pl.BlockSpec(a.shape, lambda i: (0,) * nd)

    big = [jax.ShapeDtypeStruct(a.shape, F32) for a in w_big]
    small = [jax.ShapeDtypeStruct(a.shape, F32) for a in w_small]
    return pl.pallas_call(
        body, name="adamw", grid=(ADAM_STEPS,), out_shape=big * 4 + small * 4,
        in_specs=[part(a) for a in g_big + w_big + m_big + v_big] + [whole(gvec)]
        + [whole(a) for a in w_small + m_small + v_small],
        out_specs=[part(a) for a in w_big] * 4 + [whole(a) for a in w_small] * 4,
        compiler_params=pltpu.CompilerParams(dimension_semantics=("arbitrary",), vmem_limit_bytes=VMEM_LIMIT),
    )(*g_big, *w_big, *m_big, *v_big, gvec, *w_small, *m_small, *v_small)


def kernel(x, positions, w_in, q_norm_g, w_uq, kv_norm_g, w_ukv, sgu_norm_g, sgu_norm_b, w_spatial, b_spatial, w_out, ln_g, ln_b, loss_target, m_w_in, m_q_norm_g, m_w_uq, m_kv_norm_g, m_w_ukv, m_sgu_norm_g, m_sgu_norm_b, m_w_spatial, m_b_spatial, m_w_out, m_ln_g, m_ln_b, v_w_in, v_q_norm_g, v_w_uq, v_kv_norm_g, v_w_ukv, v_sgu_norm_g, v_sgu_norm_b, v_w_spatial, v_b_spatial, v_w_out, v_ln_g, v_ln_b):
    seq = x.shape[1]
    x2 = x.reshape(seq, D_MODEL)
    tgt = loss_target.reshape(seq, D_MODEL)
    pos = positions.reshape(seq, 1)

    win, wuq, wkv, a_out = _weight_gather([w_in, w_uq, w_ukv, w_out])
    wout = a_out.reshape(D_MODEL, D_MODEL)

    lane = np.arange(HP)
    half = ROPE // 2
    inv_freq = (1.0 / (ROPE_THETA ** (np.arange(half, dtype=np.float32) / half))).astype(np.float32)
    in_rope = (lane >= NOPE) & (lane < NOPE + ROPE)
    invf = jnp.asarray(np.where(in_rope, inv_freq[(lane - NOPE) % half], 0.0).astype(np.float32))
    m1 = jnp.asarray(np.where((lane >= NOPE) & (lane < NOPE + half), -1.0, 0.0).astype(np.float32))
    m2 = jnp.asarray(np.where((lane >= NOPE + half) & (lane < NOPE + ROPE), 1.0, 0.0).astype(np.float32))
    row = lambda a: jnp.pad(a.astype(F32), (0, D_MODEL - a.shape[0]))
    pvec = jnp.stack([row(q_norm_g), row(kv_norm_g), row(sgu_norm_g), row(sgu_norm_b), row(invf), row(m1),
                      row(m2), row(ln_g), row(ln_b)] + [jnp.zeros((D_MODEL,), F32)] * (PV_ROWS - 9))
    tri = jnp.tril(jnp.ones((CHUNK, CHUNK), dtype=bool))
    wt = jnp.where(tri[None], w_spatial, 0.0).astype(BF16)
    wtt = jnp.swapaxes(wt, 1, 2)
    bsp = jnp.repeat(b_spatial.T, VDIM, axis=1)

    cq, ckv, gate, q, k, v, vt, cs = _fwd_pre(x2, pos, win, wuq, wkv, pvec)
    o, lse = _attn_fwd(q, k, vt)
    dh2, do, dgate, g_wout, g_wsp, gvec = _post(x2, tgt, o, gate, wout, pvec, wt, wtt, bsp)
    dq, dk, dv = _attn_bwd(q, k, v, do, o, lse, cs, pvec)
    gx, g_win, g_wuq, g_wkv, gvec = _bwd_pre(x2, dh2, cq, ckv, cs, dq, dk, dv, dgate, win, wuq, wkv, pvec, gvec)

    r_in, r_uq, r_ukv, r_out, r_wsp, r_vec = _grad_reduce(
        g_win, g_wuq, g_wkv, g_wout.reshape(4, 2, D_MODEL // 8, D_MODEL), g_wsp.reshape(4, 2, CHUNK, CHUNK), gvec)

    big = [w_in, w_uq, w_ukv, w_out, w_spatial]
    flip = lambda a: a.T if a.ndim == 2 and a.shape[1] % LANES else a
    flips = lambda arrs: [flip(a) for a in arrs]
    g_big = flips([r_in.reshape(w_in.shape), r_uq.reshape(w_uq.shape), r_ukv.reshape(w_ukv.shape),
                   r_out.reshape(w_out.shape), r_wsp.reshape(w_spatial.shape)])
    small = lambda qg, kvg, sg, sb, lng, lnb, bs: [qg.reshape(1, -1), kvg.reshape(1, -1), sg.reshape(1, -1),
                                                   sb.reshape(1, -1), lng.reshape(1, -1), lnb.reshape(1, -1), bs]
    res = _adamw(g_big, flips(big), flips([m_w_in, m_w_uq, m_w_ukv, m_w_out, m_w_spatial]),
                 flips([v_w_in, v_w_uq, v_w_ukv, v_w_out, v_w_spatial]), r_vec,
                 small(q_norm_g, kv_norm_g, sgu_norm_g, sgu_norm_b, ln_g, ln_b, b_spatial),
                 small(m_q_norm_g, m_kv_norm_g, m_sgu_norm_g, m_sgu_norm_b, m_ln_g, m_ln_b, m_b_spatial),
                 small(v_q_norm_g, v_kv_norm_g, v_sgu_norm_g, v_sgu_norm_b, v_ln_g, v_ln_b, v_b_spatial))
    res = [r.T if n < 4 * len(big) and r.shape != big[n % len(big)].shape else r for n, r in enumerate(res)]

    def ordered(big, sm):
        vec = lambda n: sm[n].reshape(-1)
        return [big[0], vec(0), big[1], vec(1), big[2], vec(2), vec(3), big[4], sm[6], big[3], vec(4), vec(5)]

    loss = r_vec[GV_LOSS, 0]
    return (loss, gx.reshape(1, seq, D_MODEL), *ordered(res[0:5], res[20:27]), *ordered(res[5:10], res[27:34]),
            *ordered(res[10:15], res[34:41]), *ordered(res[15:20], res[41:48]))
```

```python
import math

import jax
import jax.numpy as jnp
import numpy as np
from jax import lax
from jax.experimental import pallas as pl
from jax.experimental.pallas import tpu as pltpu

F32 = jnp.float32
BF16 = jnp.bfloat16

D_MODEL = 1024
Q_LORA = 256
KV_LORA = 128
HEADS = 8
NOPE = 64
ROPE = 32
VDIM = 64
MLA_W = HEADS * VDIM
GW = 512
CHUNK = 128
HP = 128
PAIRS = HEADS // 2
D_IN = 2464
D_INR = 2560
C_CKV = Q_LORA
C_KR = Q_LORA + KV_LORA
C_GATE = C_KR + HP
ROPE_THETA = 10000.0
DN_ALPHA = 2.0 ** 0.25
EPS = 1e-5
SCALE = 1.0 / math.sqrt(NOPE + ROPE)
SCALE_LOG2E = SCALE * 1.4426950408889634
INV_SQRT2 = 0.7071067811865476
INV_SQRT_2PI = 0.3989422804014327

ADAM_LR = 0.001
ADAM_B1 = 0.9
ADAM_B2 = 0.999
ADAM_EPS = 1e-08
ADAM_WD = 0.01
ADAM_STEP = 10

PV_QG, PV_KVG, PV_SG, PV_SB, PV_INVF, PV_M1, PV_M2, PV_LNG, PV_LNB = range(9)
PV_ROWS = 16
GV_QG, GV_KVG, GV_SG, GV_SB, GV_LNG, GV_LNB, GV_LOSS = range(7)
GV_BSP = 8
GV_ROWS = 16

MESH = pl.DeviceIdType.MESH

FWD_TILE = 1024
POST_TILE = 512
BWD_TILE = 512
ATT_BLK = 512
ADAM_STEPS = 4
SUBLANES, LANES = 8, 128
VMEM_LIMIT = 60 * 1024 * 1024


def _dot(a, b):
    return jnp.dot(a, b, preferred_element_type=F32)


def _dot_nt(a, b):
    return lax.dot_general(a, b, (((1,), (1,)), ((), ())), preferred_element_type=F32)


def _dot_tn(a, b):
    return lax.dot_general(a, b, (((0,), (0,)), ((), ())), preferred_element_type=F32)


def _sigmoid(z):
    return pl.reciprocal(1.0 + jnp.exp(-z), approx=True)


def _gelu_and_grad(x):
    cdf = 0.5 * (1.0 + lax.erf(x * INV_SQRT2))
    return x * cdf, cdf + x * (INV_SQRT_2PI * jnp.exp(-0.5 * x * x))


def _rms_stats(x):
    r = lax.rsqrt(jnp.mean(x * x, axis=-1, keepdims=True) + EPS)
    return x * r, r


def _rms_bwd(dy, g, xh, r):
    dyg = dy * g
    return r * (dyg - xh * jnp.mean(dyg * xh, axis=-1, keepdims=True))


def _ln_stats(x):
    mu = jnp.mean(x, axis=-1, keepdims=True)
    xc = x - mu
    r = lax.rsqrt(jnp.mean(xc * xc, axis=-1, keepdims=True) + EPS)
    return xc * r, r


def _ln_bwd(dy, g, xh, r):
    dxh = dy * g
    return r * (dxh - jnp.mean(dxh, axis=-1, keepdims=True) - xh * jnp.mean(dxh * xh, axis=-1, keepdims=True))


def _rope_fwd(t, c, s1, s2):
    return t * c + pltpu.roll(t, HP - 16, 1) * s1 + pltpu.roll(t, 16, 1) * s2


def _rope_bwd(d, c, s1, s2):
    return d * c + pltpu.roll(d * s1, 16, 1) + pltpu.roll(d * s2, HP - 16, 1)


def _lane_lt64(shape):
    return lax.broadcasted_iota(jnp.int32, shape, len(shape) - 1) < 64


def _spatial_mix(w_ref, src, dst_ref, rows):
    for c in range(rows // CHUNK):
        for p in range(PAIRS):
            blk = src[c * CHUNK:(c + 1) * CHUNK, p * HP:(p + 1) * HP]
            a = _dot(w_ref[2 * p], blk)
            b = _dot(w_ref[2 * p + 1], blk)
            dst_ref[c * CHUNK:(c + 1) * CHUNK, p * HP:(p + 1) * HP] = jnp.where(_lane_lt64(a.shape), a, b)


def _gmlp_fwd(u_pre, v_pre, zb, sg, sb, wt_ref, bsp_ref, sv_ref, rows):
    u, du = _gelu_and_grad(u_pre)
    gv, dgv = _gelu_and_grad(v_pre)
    xh, r = _ln_stats(gv)
    vln = (xh * sg + sb).astype(BF16)
    _spatial_mix(wt_ref, vln, sv_ref, rows)
    bias = bsp_ref[...]
    svb = sv_ref[...] + jnp.concatenate([bias] * (rows // CHUNK), axis=0)
    sig = _sigmoid(zb)
    return u, du, dgv, xh, r, vln, svb, sig


N_GATHER_SEMS = 8


def _weight_gather(shards):
    n_arr = len(shards)
    w_in, w_uq, w_ukv, _ = shards
    cw = w_in.shape[1]
    gate0 = C_KR + ROPE
    rc = 128

    def body(*refs):
        ins = refs[0:n_arr]
        win_o, wuq_o, wkv_o, out_o = refs[n_arr:n_arr + 4]
        a_in, a_uq, a_ukv, send_sems, recv_sems = refs[n_arr + 4:]
        outs = [a_in, a_uq, a_ukv, out_o]
        x, y, c = lax.axis_index("x"), lax.axis_index("y"), lax.axis_index("c")
        j = 2 * x + y
        sib = (x, y, 1 - c)
        near = [(1 - x, y, c), (x, 1 - y, c)]
        b_near = [2 * (1 - x) + y, 2 * x + 1 - y]
        b_far = 2 * (1 - x) + 1 - y
        for n in range(n_arr):
            outs[n][j] = ins[n][...].astype(BF16)

        def half(n, blk, core):
            r = shards[n].shape[0] // 2
            return outs[n].at[blk, pl.ds(pl.multiple_of(core * r, 16), r), :]

        def quarter(n, blk, core, part):
            q = shards[n].shape[0] // 4
            return outs[n].at[blk, pl.ds(pl.multiple_of((2 * core + part) * q, 16), q), :]

        def copy(k, ref, to):
            return pltpu.make_async_remote_copy(
                src_ref=ref, dst_ref=ref, send_sem=send_sems.at[k], recv_sem=recv_sems.at[k],
                device_id=to, device_id_type=MESH)

        sent = [copy(N_GATHER_SEMS * n + kk, half(n, j, c), near[kk]) for n in range(n_arr) for kk in range(2)]
        for cp in sent:
            cp.start()

        def send(k, ref, to):
            sent.append(copy(k, ref, to))
            sent[-1].start()

        for n in range(n_arr):
            k0 = N_GATHER_SEMS * n
            for kk in range(2):
                copy(k0 + kk, half(n, b_near[kk], c), near[kk]).wait_recv()
                send(k0 + 2 + kk, quarter(n, b_near[kk], c, kk), near[1 - kk])
                send(k0 + 4 + kk, half(n, b_near[kk], c), sib)
        for n in range(n_arr):
            k0 = N_GATHER_SEMS * n
            for kk in range(2):
                copy(k0 + 2 + kk, quarter(n, b_far, c, kk), near[1 - kk]).wait_recv()
                send(k0 + 6 + kk, quarter(n, b_far, c, kk), sib)
        for n in range(n_arr):
            k0 = N_GATHER_SEMS * n
            for kk in range(2):
                copy(k0 + 4 + kk, half(n, b_near[kk], 1 - c), sib).wait_recv()
                copy(k0 + 6 + kk, quarter(n, b_far, 1 - c, kk), sib).wait_recv()

        def lay_in(r, carry):
            rows = pl.ds(pl.multiple_of(r * rc, rc), rc)
            win_o[rows, 0:C_KR] = a_in[0, rows, 0:C_KR]
            win_o[rows, C_KR:C_KR + NOPE] = jnp.zeros((rc, NOPE), BF16)
            win_o[rows, C_KR + NOPE:C_KR + NOPE + ROPE] = a_in[0, rows, C_KR:gate0]
            win_o[rows, C_KR + NOPE + ROPE:C_GATE] = jnp.zeros((rc, HP - NOPE - ROPE), BF16)
            win_o[rows, C_GATE:C_GATE + cw - gate0] = a_in[0, rows, gate0:cw]
            for blk in range(1, 4):
                win_o[rows, D_INR - (4 - blk) * cw:D_INR - (3 - blk) * cw] = a_in[blk, rows, :]
            return carry

        lax.fori_loop(0, D_MODEL // rc, lay_in, 0)
        wuq_o[...] = jnp.zeros(wuq_o.shape, BF16)
        wkv_o[...] = jnp.zeros(wkv_o.shape, BF16)
        qw = NOPE + ROPE
        for h in range(HEADS):
            blk, e = h // 2, h % 2
            wuq_o[:, h * HP:h * HP + qw] = a_uq[blk, :, e * qw:(e + 1) * qw]
            wkv_o[:, h * HP:h * HP + NOPE] = a_ukv[blk, :, e * HP:e * HP + NOPE]
            wkv_o[:, HEADS * HP + h * VDIM:HEADS * HP + (h + 1) * VDIM] = a_ukv[blk, :, e * HP + NOPE:(e + 1) * HP]
        for cp in sent:
            cp.wait_send()

    assert cw > gate0 and C_GATE + 4 * cw - gate0 == D_INR and NOPE + VDIM == HP and HEADS == 8
    vmem = pl.BlockSpec(memory_space=pltpu.VMEM)
    gathered = lambda a: (4,) + a.shape
    return pl.pallas_call(
        body, name="weight_gather",
        out_shape=[jax.ShapeDtypeStruct((D_MODEL, D_INR), BF16), jax.ShapeDtypeStruct((Q_LORA, HEADS * HP), BF16),
                   jax.ShapeDtypeStruct((KV_LORA, HEADS * HP + MLA_W), BF16),
                   jax.ShapeDtypeStruct(gathered(shards[3]), BF16)],
        in_specs=[vmem] * n_arr, out_specs=[vmem] * 4,
        scratch_shapes=[pltpu.VMEM(gathered(a), BF16) for a in shards[0:3]]
        + [pltpu.SemaphoreType.DMA((N_GATHER_SEMS * n_arr,)), pltpu.SemaphoreType.DMA((N_GATHER_SEMS * n_arr,))],
        compiler_params=pltpu.CompilerParams(vmem_limit_bytes=VMEM_LIMIT),
    )(*shards)


def _fwd_pre(x, pos, win, wuq, wkv, pvec):
    seq = x.shape[0]
    t = FWD_TILE

    def body(x_ref, pos_ref, win_ref, wuq_ref, wkv_ref, pv_ref,
             cq_o, ckv_o, gate_o, q_o, k_o, v_o, vt_o, cs_o):
        xb = x_ref[...].astype(BF16)
        proj = _dot(xb, win_ref[:, 0:C_GATE])
        cq = proj[:, 0:C_CKV]
        ckv = proj[:, C_CKV:C_KR]
        kr = proj[:, C_KR:C_GATE]
        cq_o[...] = cq
        ckv_o[...] = ckv

        ang = pos_ref[...].astype(F32) * pv_ref[PV_INVF:PV_INVF + 1, 0:HP]
        cos = jnp.cos(ang)
        sin = jnp.sin(ang)
        cs_o[:, 0:HP] = cos
        cs_o[:, HP:2 * HP] = sin
        s1 = sin * pv_ref[PV_M1:PV_M1 + 1, 0:HP]
        s2 = sin * pv_ref[PV_M2:PV_M2 + 1, 0:HP]

        cqh, _ = _rms_stats(cq)
        q_all = _dot((cqh * pv_ref[PV_QG:PV_QG + 1, 0:Q_LORA]).astype(BF16), wuq_ref[...])
        ckvh, _ = _rms_stats(ckv)
        kv_all = _dot((ckvh * pv_ref[PV_KVG:PV_KVG + 1, 0:KV_LORA]).astype(BF16), wkv_ref[...])
        krr = _rope_fwd(kr, cos, s1, s2)
        for h in range(HEADS):
            sl = slice(h * HP, (h + 1) * HP)
            q_o[:, sl] = (_rope_fwd(q_all[:, sl], cos, s1, s2) * SCALE_LOG2E).astype(BF16)
            k_o[:, sl] = (kv_all[:, sl] + krr).astype(BF16)
        val = kv_all[:, HEADS * HP:].astype(BF16)
        v_o[...] = val
        for blk in range(t // ATT_BLK):
            vt_o[blk] = val[blk * ATT_BLK:(blk + 1) * ATT_BLK, :].T
        gate_o[...] = _dot(xb, win_ref[:, C_GATE:D_INR]).astype(BF16)

    tile = lambda w: pl.BlockSpec((t, w), lambda i: (i, 0))
    full = lambda a: pl.BlockSpec(a.shape, lambda i: (0,) * a.ndim)
    outs = [(Q_LORA, F32), (KV_LORA, F32), (2048, BF16), (HEADS * HP, BF16), (HEADS * HP, BF16), (MLA_W, BF16)]
    assert t % ATT_BLK == 0
    out_specs = [tile(w) for w, _ in outs]
    out_specs += [pl.BlockSpec((t // ATT_BLK, MLA_W, ATT_BLK), lambda i: (i, 0, 0)), tile(2 * HP)]
    out_shape = [jax.ShapeDtypeStruct((seq, w), d) for w, d in outs]
    out_shape += [jax.ShapeDtypeStruct((seq // ATT_BLK, MLA_W, ATT_BLK), BF16), jax.ShapeDtypeStruct((seq, 2 * HP), F32)]
    return pl.pallas_call(
        body, name="fwd_pre", grid=(seq // t,),
        in_specs=[tile(D_MODEL), tile(1), full(win), full(wuq), full(wkv), full(pvec)],
        out_specs=out_specs, out_shape=out_shape,
        compiler_params=pltpu.CompilerParams(dimension_semantics=("arbitrary",), vmem_limit_bytes=VMEM_LIMIT),
    )(x, pos, win, wuq, wkv, pvec)


def _attn_fwd(q, k, vt):
    seq = q.shape[0]
    b = ATT_BLK
    nq = seq // b
    assert nq % 2 == 0
    assert nq % 4 == 0
    n_wide = sum(i // 4 for i in range(nq))

    def body(q_ref, k_ref, vt_ref, o_o, lse_o, m_ref, l_ref, acc_ref, s_even, s_odd):
        m_ref[...] = jnp.full(m_ref.shape, -jnp.inf, F32)
        l_ref[...] = jnp.zeros(l_ref.shape, F32)
        acc_ref[...] = jnp.zeros(acc_ref.shape, F32)

        def scores(i, j, s_ref, nkb):
            qrows = pl.ds(pl.multiple_of(i * b, b), b)
            krows = pl.ds(pl.multiple_of(j * b, b), nkb * b)
            for a in range(2):
                s_ref[a, 0:nkb * b, :] = _dot_nt(k_ref[krows, a * HP:(a + 1) * HP], q_ref[qrows, a * HP:(a + 1) * HP])

        def consume(i, j, s_ref, nkb, masked):
            for a in range(2):
                st = s_ref[a, 0:nkb * b, :]
                if masked:
                    ki = lax.broadcasted_iota(jnp.int32, st.shape, 0)
                    qi = lax.broadcasted_iota(jnp.int32, st.shape, 1)
                    st = jnp.where(ki <= qi, st, -jnp.inf)
                m_prev = m_ref[i, a:a + 1, :]
                m_new = jnp.maximum(m_prev, jnp.max(st, axis=0, keepdims=True))
                alpha = jnp.exp2(m_prev - m_new)
                pt = jnp.exp2(st - m_new)
                ptb = pt.astype(BF16)
                l_ref[i, a:a + 1, :] = alpha * l_ref[i, a:a + 1, :] + jnp.sum(pt, axis=0, keepdims=True)
                pv = _dot(vt_ref[j], ptb[0:b, :])
                for kb in range(1, nkb):
                    pv = pv + _dot(vt_ref[j + kb], ptb[kb * b:(kb + 1) * b, :])
                acc_ref[i, a] = alpha * acc_ref[i, a] + pv
                m_ref[i, a:a + 1, :] = m_new

        def run(count, first, following, nkb, masked):
            if count == 0:
                return
            scores(*first, s_even, nkb)

            def two(u, ij):
                nxt = following(*ij)
                scores(*nxt, s_odd, nkb)
                consume(*ij, s_even, nkb, masked)
                nxt2 = following(*nxt)
                scores(*nxt2, s_even, nkb)
                consume(*nxt, s_odd, nkb, masked)
                return nxt2

            last = lax.fori_loop(0, count // 2, two, tuple(jnp.int32(c) for c in first))
            if count % 2:
                consume(*last, s_even, nkb, masked)

        def clamp(i):
            return jnp.minimum(i, nq - 1)

        def next_wide(i, j):
            wrap = j + 8 > i
            return clamp(jnp.where(wrap, i + 1, i)), jnp.where(wrap, 0, j + 4)

        def next_pair(i, j):
            low = lax.rem(i, 4) == 2
            return clamp(jnp.where(low, i + 1, i + 3)), jnp.minimum(jnp.where(low, j, j + 4), nq - 4)

        run(n_wide, (4, 0), next_wide, 4, False)
        run(nq // 2, (2, 0), next_pair, 2, False)
        run(nq // 2, (1, 0), lambda i, j: (clamp(i + 2), jnp.minimum(j + 2, nq - 2)), 1, False)
        run(nq, (0, 0), lambda i, j: (clamp(i + 1), clamp(j + 1)), 1, True)
        top = lax.broadcasted_iota(jnp.int32, (HP, b), 0) < 64

        def finish(i, carry):
            rows = pl.ds(pl.multiple_of(i * b, b), b)
            o_o[rows, :] = jnp.where(top, acc_ref[i, 0] / l_ref[i, 0:1, :], acc_ref[i, 1] / l_ref[i, 1:2, :]).T
            lse_o[i] = m_ref[i, 0:2, :] + jnp.log2(l_ref[i, 0:2, :])
            return carry

        lax.fori_loop(0, nq, finish, 0)

    return pl.pallas_call(
        body, name="attn_fwd", grid=(PAIRS,),
        in_specs=[pl.BlockSpec((seq, 2 * HP), lambda p: (0, p)),
                  pl.BlockSpec((seq, 2 * HP), lambda p: (0, p)),
                  pl.BlockSpec((nq, HP, b), lambda p: (0, p, 0))],
        out_specs=[pl.BlockSpec((seq, HP), lambda p: (0, p)),
                   pl.BlockSpec((None, nq, 2, b), lambda p: (p, 0, 0, 0))],
        out_shape=[jax.ShapeDtypeStruct((seq, MLA_W), F32),
                   jax.ShapeDtypeStruct((PAIRS, nq, 2, b), F32)],
        scratch_shapes=[pltpu.VMEM((nq, 8, b), F32), pltpu.VMEM((nq, 8, b), F32), pltpu.VMEM((nq, 2, HP, b), F32),
                        pltpu.VMEM((2, 4 * b, b), F32), pltpu.VMEM((2, 4 * b, b), F32)],
        compiler_params=pltpu.CompilerParams(dimension_semantics=("arbitrary",), vmem_limit_bytes=VMEM_LIMIT),
    )(q, k, vt)


def _post(x, tgt, o, gate, wout, pvec, wt, wtt, bsp):
    seq = x.shape[0]
    t = POST_TILE
    nt = seq // t

    def body(x_ref, tgt_ref, o_ref, gate_ref, wout_ref, pv_ref, wt_ref, wtt_ref, bsp_ref,
             dh2_o, do_o, dgate_o, gwout_o, gwsp_o, vec_o, sv_ref, dvln_ref, bacc_ref):
        i = pl.program_id(0)

        @pl.when(i == 0)
        def _():
            gwout_o[...] = jnp.zeros_like(gwout_o)
            gwsp_o[...] = jnp.zeros_like(gwsp_o)
            vec_o[...] = jnp.zeros_like(vec_o)
            bacc_ref[...] = jnp.zeros_like(bacc_ref)

        za = gate_ref[:, 0:512].astype(F32)
        u_pre = gate_ref[:, 512:1024].astype(F32)
        v_pre = gate_ref[:, 1024:1536].astype(F32)
        zb = gate_ref[:, 1536:2048].astype(F32)
        sg = pv_ref[PV_SG:PV_SG + 1, 0:GW]
        sb = pv_ref[PV_SB:PV_SB + 1, 0:GW]
        lng = pv_ref[PV_LNG:PV_LNG + 1, :]
        lnb = pv_ref[PV_LNB:PV_LNB + 1, :]
        o = o_ref[...]

        sig_a = _sigmoid(za)
        silu_a = za * sig_a
        u, du, dgv, xh, r, vln, svb, sig_b = _gmlp_fwd(u_pre, v_pre, zb, sg, sb, wt_ref, bsp_ref, sv_ref, t)
        silu_b = zb * sig_b
        sgu = u * svb
        merged = jnp.concatenate([o * silu_a, sgu * silu_b], axis=1).astype(BF16)
        h2 = DN_ALPHA * x_ref[...] + _dot(merged, wout_ref[...])
        xh2, r2 = _ln_stats(h2)
        err = xh2 * lng + lnb - tgt_ref[...]
        d_out = err * (1.0 / D_MODEL)
        vec_o[GV_LNG:GV_LNG + 1, :] += jnp.sum(d_out * xh2, axis=0, keepdims=True)
        vec_o[GV_LNB:GV_LNB + 1, :] += jnp.sum(d_out, axis=0, keepdims=True)
        vec_o[GV_LOSS:GV_LOSS + 1, :] += jnp.sum(err * err, axis=0, keepdims=True) * (0.5 / D_MODEL)

        d_h2 = _ln_bwd(d_out, lng, xh2, r2)
        dh2_o[...] = d_h2
        dh2b = d_h2.astype(BF16)
        gwout_o[...] += _dot_tn(merged, dh2b)
        d_m = _dot_nt(dh2b, wout_ref[...])
        d_oa = d_m[:, 0:512]
        d_ob = d_m[:, 512:1024]
        do_o[...] = (d_oa * silu_a).astype(BF16)
        dgate_o[:, 0:512] = (d_oa * o * (sig_a * (1.0 + za * (1.0 - sig_a)))).astype(BF16)
        dgate_o[:, 1536:2048] = (d_ob * sgu * (sig_b * (1.0 + zb * (1.0 - sig_b)))).astype(BF16)
        d_sgu = d_ob * silu_b
        dgate_o[:, 512:1024] = (d_sgu * svb * du).astype(BF16)
        d_sv = d_sgu * u
        acc = bacc_ref[...]
        for c in range(t // CHUNK):
            acc = acc + d_sv[c * CHUNK:(c + 1) * CHUNK, :]
        bacc_ref[...] = acc
        d_svb = d_sv.astype(BF16)
        for c in range(t // CHUNK):
            for p in range(PAIRS):
                blk = d_svb[c * CHUNK:(c + 1) * CHUNK, p * HP:(p + 1) * HP]
                vblk = vln[c * CHUNK:(c + 1) * CHUNK, p * HP:(p + 1) * HP]
                first = _lane_lt64(blk.shape)
                gwsp_o[2 * p] += _dot_nt(jnp.where(first, blk, jnp.zeros_like(blk)), vblk)
                gwsp_o[2 * p + 1] += _dot_nt(jnp.where(first, jnp.zeros_like(blk), blk), vblk)
        _spatial_mix(wtt_ref, d_svb, dvln_ref, t)
        d_vln = dvln_ref[...]
        vec_o[GV_SG:GV_SG + 1, 0:GW] += jnp.sum(d_vln * xh, axis=0, keepdims=True)
        vec_o[GV_SB:GV_SB + 1, 0:GW] += jnp.sum(d_vln, axis=0, keepdims=True)
        dgate_o[:, 1024:1536] = (_ln_bwd(d_vln, sg, xh, r) * dgv).astype(BF16)


        @pl.when(i == nt - 1)
        def _():
            tri = (lax.broadcasted_iota(jnp.int32, (CHUNK, CHUNK), 1)
                   <= lax.broadcasted_iota(jnp.int32, (CHUNK, CHUNK), 0))
            for h in range(HEADS):
                gwsp_o[h] = jnp.where(tri, gwsp_o[h], 0.0)
            lane = lax.broadcasted_iota(jnp.int32, (CHUNK, HP), 1)
            res = jnp.zeros((CHUNK, HP), F32)
            for h in range(HEADS):
                p, a = divmod(h, 2)
                blk = bacc_ref[:, p * HP:(p + 1) * HP]
                part = jnp.where(_lane_lt64(blk.shape) == (a == 0), blk, 0.0)
                res = jnp.where(lane == h, jnp.sum(part, axis=-1, keepdims=True), res)
            vec_o[GV_BSP:GV_BSP + HEADS, 0:HP] = res.T[0:HEADS, :]
            lane1 = lax.broadcasted_iota(jnp.int32, (1, D_MODEL), 1)
            total = jnp.sum(vec_o[GV_LOSS:GV_LOSS + 1, :], axis=-1, keepdims=True)
            vec_o[GV_LOSS:GV_LOSS + 1, :] = jnp.where(lane1 == 0, total, 0.0)

    tile = lambda w: pl.BlockSpec((t, w), lambda i: (i, 0))
    full = lambda a: pl.BlockSpec(a.shape, lambda i: (0,) * a.ndim)
    const = lambda s: pl.BlockSpec(s, lambda i: (0,) * len(s))
    return pl.pallas_call(
        body, name="post", grid=(nt,),
        in_specs=[tile(D_MODEL), tile(D_MODEL), tile(MLA_W), tile(2048), full(wout), full(pvec),
                  full(wt), full(wtt), full(bsp)],
        out_specs=[tile(D_MODEL), tile(MLA_W), tile(2048), const((D_MODEL, D_MODEL)),
                   const((HEADS, CHUNK, CHUNK)), const((GV_ROWS, D_MODEL))],
        out_shape=[jax.ShapeDtypeStruct((seq, D_MODEL), F32), jax.ShapeDtypeStruct((seq, MLA_W), BF16),
                   jax.ShapeDtypeStruct((seq, 2048), BF16), jax.ShapeDtypeStruct((D_MODEL, D_MODEL), F32),
                   jax.ShapeDtypeStruct((HEADS, CHUNK, CHUNK), F32), jax.ShapeDtypeStruct((GV_ROWS, D_MODEL), F32)],
        scratch_shapes=[pltpu.VMEM((t, GW), F32), pltpu.VMEM((t, GW), F32), pltpu.VMEM((CHUNK, GW), F32)],
        compiler_params=pltpu.CompilerParams(dimension_semantics=("arbitrary",), vmem_limit_bytes=VMEM_LIMIT),
    )(x, tgt, o, gate, wout, pvec, wt, wtt, bsp)


def _attn_bwd(q, k, v, do, o, lse, cs, pvec):
    seq = q.shape[0]
    b = ATT_BLK
    nq = seq // b

    def body(q_ref, k_ref, v_ref, do_ref, o_ref, lse_ref, cs_ref, pv_ref, dq_o, dk_o, dv_o, dk_acc, dv_acc):
        i = pl.program_id(1)

        @pl.when(i == 0)
        def _():
            dk_acc[...] = jnp.zeros_like(dk_acc)
            dv_acc[...] = jnp.zeros_like(dv_acc)

        first = _lane_lt64((b, HP))
        do = do_ref[...]
        zero = jnp.zeros_like(do)
        dos = [jnp.where(first, do, zero), jnp.where(first, zero, do)]
        prod_t = (do.astype(F32) * o_ref[...]).T
        deltas = [jnp.sum(prod_t[0:64, :], axis=0, keepdims=True),
                  jnp.sum(prod_t[64:128, :], axis=0, keepdims=True)]
        lses = [lse_ref[0:1, :], lse_ref[1:2, :]]
        qs = [q_ref[:, a * HP:(a + 1) * HP] for a in range(2)]

        def step(j, dqs, masked, nk=b):
            rows = pl.ds(pl.multiple_of(j * b, b), nk)
            vb = v_ref[rows, :]
            new_dq = []
            dvs = []
            for a in range(2):
                kb = k_ref[rows, a * HP:(a + 1) * HP]
                pt = jnp.exp2(_dot_nt(kb, qs[a]) - lses[a])
                if masked:
                    ki = lax.broadcasted_iota(jnp.int32, pt.shape, 0)
                    qi = lax.broadcasted_iota(jnp.int32, pt.shape, 1) + (nk - b)
                    pt = jnp.where(ki <= qi, pt, 0.0)
                dvs.append(_dot(pt.astype(BF16), do))
                dpt = _dot_nt(vb, dos[a])
                dst = (pt * (dpt - deltas[a])).astype(BF16)
                dk_acc[rows, a * HP:(a + 1) * HP] += _dot(dst, qs[a])
                new_dq.append(dqs[a] + _dot_tn(dst, kb))
            dv_acc[rows, :] += jnp.where(_lane_lt64((nk, HP)), dvs[0], dvs[1])
            return tuple(new_dq)

        init = (jnp.zeros((b, HP), F32), jnp.zeros((b, HP), F32))
        dqs = lax.fori_loop(0, i // 4, lambda jj, cr: step(4 * jj, cr, False, 4 * b), init)
        last = [lambda cr, w=w: step(4 * (i // 4), cr, True, w * b) for w in (1, 2, 3, 4)]
        dqs = lax.switch(i % 4, last, dqs)
        cos = cs_ref[:, 0:HP]
        sin = cs_ref[:, HP:2 * HP]
        s1 = sin * pv_ref[PV_M1:PV_M1 + 1, 0:HP]
        s2 = sin * pv_ref[PV_M2:PV_M2 + 1, 0:HP]
        for a in range(2):
            dq_o[:, a * HP:(a + 1) * HP] = _rope_bwd(dqs[a] * SCALE, cos, s1, s2).astype(BF16)

        @pl.when(i == nq - 1)
        def _():
            dk_o[...] = (dk_acc[...] * (SCALE / SCALE_LOG2E)).astype(BF16)
            dv_o[...] = dv_acc[...].astype(BF16)

    return pl.pallas_call(
        body, name="attn_bwd", grid=(PAIRS, nq),
        in_specs=[pl.BlockSpec((b, 2 * HP), lambda p, i: (i, p)),
                  pl.BlockSpec((seq, 2 * HP), lambda p, i: (0, p)),
                  pl.BlockSpec((seq, HP), lambda p, i: (0, p)),
                  pl.BlockSpec((b, HP), lambda p, i: (i, p)),
                  pl.BlockSpec((b, HP), lambda p, i: (i, p)),
                  pl.BlockSpec((None, None, 2, b), lambda p, i: (p, i, 0, 0)),
                  pl.BlockSpec((b, 2 * HP), lambda p, i: (i, 0)),
                  pl.BlockSpec(pvec.shape, lambda p, i: (0, 0))],
        out_specs=[pl.BlockSpec((b, 2 * HP), lambda p, i: (i, p)),
                   pl.BlockSpec((seq, 2 * HP), lambda p, i: (0, p)),
                   pl.BlockSpec((seq, HP), lambda p, i: (0, p))],
        out_shape=[jax.ShapeDtypeStruct((seq, HEADS * HP), BF16),
                   jax.ShapeDtypeStruct((seq, HEADS * HP), BF16),
                   jax.ShapeDtypeStruct((seq, MLA_W), BF16)],
        scratch_shapes=[pltpu.VMEM((seq, 2 * HP), F32), pltpu.VMEM((seq, HP), F32)],
        compiler_params=pltpu.CompilerParams(dimension_semantics=("arbitrary", "arbitrary"),
                                             vmem_limit_bytes=VMEM_LIMIT),
    )(q, k, v, do, o, lse, cs, pvec)


def _bwd_pre(x, dh2, cq, ckv, cs, dq, dk, dv, dgate, win, wuq, wkv, pvec, gvec):
    seq = x.shape[0]
    t = BWD_TILE

    def body(x_ref, dh2_ref, cq_ref, ckv_ref, cs_ref, dq_ref, dk_ref, dv_ref, dgate_ref,
             win_ref, wuq_ref, wkv_ref, pv_ref, gv_ref, gx_o, gwin_o, gwuq_o, gwkv_o, vec_o):
        i = pl.program_id(0)

        @pl.when(i == 0)
        def _():
            gwin_o[...] = jnp.zeros_like(gwin_o)
            gwuq_o[...] = jnp.zeros_like(gwuq_o)
            gwkv_o[...] = jnp.zeros_like(gwkv_o)
            vec_o[...] = gv_ref[...]

        xb = x_ref[...].astype(BF16)
        dgate = dgate_ref[...]
        gwin_o[:, C_GATE:D_INR] += _dot_tn(xb, dgate)
        gx_gate = _dot_nt(dgate, win_ref[:, C_GATE:D_INR])

        qg = pv_ref[PV_QG:PV_QG + 1, 0:Q_LORA]
        kvg = pv_ref[PV_KVG:PV_KVG + 1, 0:KV_LORA]
        dq = dq_ref[...]
        cqh, rq = _rms_stats(cq_ref[...])
        d_cqn = _dot_nt(dq, wuq_ref[...])
        gwuq_o[...] += _dot_tn((cqh * qg).astype(BF16), dq)
        vec_o[GV_QG:GV_QG + 1, 0:Q_LORA] += jnp.sum(d_cqn * cqh, axis=0, keepdims=True)
        d_cq = _rms_bwd(d_cqn, qg, cqh, rq)

        dk = dk_ref[...]
        dkv = jnp.concatenate([dk, dv_ref[...]], axis=1)
        ckvh, rkv = _rms_stats(ckv_ref[...])
        d_ckvn = _dot_nt(dkv, wkv_ref[...])
        gwkv_o[...] += _dot_tn((ckvh * kvg).astype(BF16), dkv)
        vec_o[GV_KVG:GV_KVG + 1, 0:KV_LORA] += jnp.sum(d_ckvn * ckvh, axis=0, keepdims=True)
        d_ckv = _rms_bwd(d_ckvn, kvg, ckvh, rkv)

        dks = dk[:, 0:HP].astype(F32)
        for h in range(1, HEADS):
            dks = dks + dk[:, h * HP:(h + 1) * HP].astype(F32)
        cos = cs_ref[:, 0:HP]
        sin = cs_ref[:, HP:2 * HP]
        d_kr = _rope_bwd(dks, cos, sin * pv_ref[PV_M1:PV_M1 + 1, 0:HP], sin * pv_ref[PV_M2:PV_M2 + 1, 0:HP])

        d_lat = jnp.concatenate([d_cq.astype(BF16), d_ckv.astype(BF16), d_kr.astype(BF16)], axis=1)
        gwin_o[:, 0:C_GATE] += _dot_tn(xb, d_lat)
        gx_o[...] = DN_ALPHA * dh2_ref[...] + gx_gate + _dot_nt(d_lat, win_ref[:, 0:C_GATE])

    tile = lambda w: pl.BlockSpec((t, w), lambda i: (i, 0))
    full = lambda a: pl.BlockSpec(a.shape, lambda i: (0,) * a.ndim)
    const = lambda s: pl.BlockSpec(s, lambda i: (0,) * len(s))
    return pl.pallas_call(
        body, name="bwd_pre", grid=(seq // t,),
        in_specs=[tile(D_MODEL), tile(D_MODEL), tile(Q_LORA), tile(KV_LORA), tile(2 * HP), tile(HEADS * HP),
                  tile(HEADS * HP), tile(MLA_W), tile(2048), full(win), full(wuq), full(wkv), full(pvec), full(gvec)],
        out_specs=[tile(D_MODEL), const((D_MODEL, D_INR)), const((Q_LORA, HEADS * HP)),
                   const((KV_LORA, HEADS * HP + MLA_W)), const((GV_ROWS, D_MODEL))],
        out_shape=[jax.ShapeDtypeStruct((seq, D_MODEL), F32), jax.ShapeDtypeStruct((D_MODEL, D_INR), F32),
                   jax.ShapeDtypeStruct((Q_LORA, HEADS * HP), F32),
                   jax.ShapeDtypeStruct((KV_LORA, HEADS * HP + MLA_W), F32),
                   jax.ShapeDtypeStruct((GV_ROWS, D_MODEL), F32)],
        compiler_params=pltpu.CompilerParams(dimension_semantics=("arbitrary",), vmem_limit_bytes=VMEM_LIMIT),
    )(x, dh2, cq, ckv, cs, dq, dk, dv, dgate, win, wuq, wkv, pvec, gvec)


def _grad_reduce(g_win, g_wuq, g_wkv, g_wout, g_wsp, gvec):
    gate0 = C_KR + ROPE
    cw = (D_INR - C_GATE + gate0) // 4
    qw = NOPE + ROPE
    rc = 128
    half_shapes = [(D_MODEL // 2, cw), (Q_LORA // 2, 2 * qw), (KV_LORA // 2, 2 * HP), g_wout.shape[2:], g_wsp.shape[2:]]
    n_arr = len(half_shapes)
    n_lay = 3
    order = [3, 0, 1, 2, 4]
    n_big = n_arr - 1
    k1 = lambda n, blk: 4 * n + blk
    k2 = lambda n, kk: 4 * n_arr + 3 * n + kk
    k3 = lambda n: 7 * n_arr + n
    k3w = lambda k: 7 * n_arr + n_big + k
    kv = lambda k: 7 * n_arr + n_big + 7 + k
    n_sem = 7 * n_arr + n_big + 14

    def body(*refs):
        gwin, gwuq, gwkv = refs[0:n_lay]
        gv = refs[n_arr]
        outs, ov = refs[n_arr + 1:2 * n_arr + 1], refs[2 * n_arr + 1]
        r1 = refs[2 * n_arr + 2:3 * n_arr + 2]
        r2 = refs[3 * n_arr + 2:4 * n_arr + 2]
        s2 = refs[4 * n_arr + 2:5 * n_arr + 2]
        g = refs[5 * n_arr + 2:5 * n_arr + 2 + n_lay] + refs[n_lay:n_arr]
        vbuf, send_sems, recv_sems = refs[5 * n_arr + 2 + n_lay:]
        x, y, c = lax.axis_index("x"), lax.axis_index("y"), lax.axis_index("c")
        j = 2 * x + y
        me = 2 * j + c
        sib = (x, y, 1 - c)
        chips = [(1 - x, y), (x, 1 - y), (1 - x, 1 - y)]
        others = [sib] + [(px, py, pc) for (px, py) in chips for pc in (c, 1 - c)]

        def copy(k, src, dst, to):
            return pltpu.make_async_remote_copy(
                src_ref=src, dst_ref=dst, send_sem=send_sems.at[k], recv_sem=recv_sems.at[k],
                device_id=to, device_id_type=MESH)

        def lay_in(r, carry):
            src = pl.ds(pl.multiple_of(r * rc, rc), rc)
            hf = r // (D_MODEL // 2 // rc)
            dst = pl.ds(pl.multiple_of((r % (D_MODEL // 2 // rc)) * rc, rc), rc)
            g[0][0, hf, dst, 0:C_KR] = gwin[src, 0:C_KR]
            g[0][0, hf, dst, C_KR:gate0] = gwin[src, C_KR + NOPE:C_KR + NOPE + ROPE]
            g[0][0, hf, dst, gate0:cw] = gwin[src, C_GATE:C_GATE + cw - gate0]
            for blk in range(1, 4):
                g[0][blk, hf, dst, :] = gwin[src, D_INR - (4 - blk) * cw:D_INR - (3 - blk) * cw]
            return carry

        def lay_heads():
            for h in range(HEADS):
                blk, e = h // 2, h % 2
                for hf in range(2):
                    rq = slice(hf * Q_LORA // 2, (hf + 1) * Q_LORA // 2)
                    rk = slice(hf * KV_LORA // 2, (hf + 1) * KV_LORA // 2)
                    g[1][blk, hf, :, e * qw:(e + 1) * qw] = gwuq[rq, h * HP:h * HP + qw]
                    g[2][blk, hf, :, e * HP:e * HP + NOPE] = gwkv[rk, h * HP:h * HP + NOPE]
                    g[2][blk, hf, :, e * HP + NOPE:(e + 1) * HP] = gwkv[rk, HEADS * HP + h * VDIM:HEADS * HP + (h + 1) * VDIM]

        l1, l2 = [], []

        def level1(n):
            for blk in range(4):
                l1.append(copy(k1(n, blk), g[n].at[blk, 1 - c], r1[n].at[blk], sib))
                l1[-1].start()

        def level2(n):
            for blk in range(4):
                copy(k1(n, blk), g[n].at[blk, c], r1[n].at[blk], sib).wait_recv()
            for blk in range(4):
                r1[n][blk] = g[n][blk, c] + r1[n][blk]
                s2[n][blk] = r1[n][blk].astype(BF16)
            for kk, (px, py) in enumerate(chips):
                l2.append(copy(k2(n, kk), s2[n].at[2 * px + py], r2[n].at[kk], (px, py, c)))
                l2[-1].start()

        lv = [copy(kv(k), gv, vbuf.at[me], to) for k, to in enumerate(others)]
        for cp in lv:
            cp.start()
        for n in range(n_lay, n_arr):
            level1(n)
        lay_heads()
        level1(1)
        level1(2)
        level2(order[0])
        lax.fori_loop(0, D_MODEL // rc, lay_in, 0)
        level1(0)
        for n in order[1:]:
            level2(n)

        l3 = []
        for n in order:
            for kk in range(3):
                copy(k2(n, kk), s2[n].at[0], r2[n].at[kk], sib).wait_recv()
            red = ((r1[n][j] + r2[n][0].astype(F32)) + r2[n][1].astype(F32)) + r2[n][2].astype(F32)
            if n < n_big:
                outs[n][c] = red
                back = [copy(k3(n), outs[n].at[c], outs[n].at[c], sib)]
            else:
                outs[n][j, c] = red
                back = [copy(k3w(k), outs[n].at[j, c], outs[n].at[j, c], to) for k, to in enumerate(others)]
            for cp in back:
                cp.start()
            l3 += back
        for n in range(n_big):
            copy(k3(n), outs[n].at[1 - c], outs[n].at[1 - c], sib).wait_recv()
        for k, (px, py, pc) in enumerate(others):
            landed = outs[n_big].at[2 * px + py, pc]
            copy(k3w(k), landed, landed, (px, py, pc)).wait_recv()
            copy(kv(k), gv, vbuf.at[4 * px + 2 * py + pc], (px, py, pc)).wait_recv()
        vbuf[me] = gv[...]
        total = vbuf[0]
        for d in range(1, 8):
            total = total + vbuf[d]
        ov[...] = total
        for cp in l1 + lv + l2 + l3:
            cp.wait_send()

    vmem = pl.BlockSpec(memory_space=pltpu.VMEM)
    assert g_win.shape == (D_MODEL, D_INR) and 4 * cw == D_INR - C_GATE + gate0 and NOPE + VDIM == HP and HEADS == 8
    out_shape =[jax.ShapeDtypeStruct((2,) + s, F32) for s in half_shapes[:n_big]]
    out_shape += [jax.ShapeDtypeStruct((4, 2) + half_shapes[n_big], F32), jax.ShapeDtypeStruct(gvec.shape, F32)]
    scratch = [pltpu.VMEM((4,) + s, F32) for s in half_shapes] + [pltpu.VMEM((3,) + s, BF16) for s in half_shapes]
    scratch += [pltpu.VMEM((4,) + s, BF16) for s in half_shapes]
    scratch += [pltpu.VMEM((4, 2) + s, F32) for s in half_shapes[:n_lay]]
    scratch += [pltpu.VMEM((8,) + gvec.shape, F32), pltpu.SemaphoreType.DMA((n_sem,)), pltpu.SemaphoreType.DMA((n_sem,))]
    return pl.pallas_call(
        body, name="grad_reduce", out_shape=out_shape,
        in_specs=[vmem] * (n_arr + 1), out_specs=[vmem] * (n_arr + 1), scratch_shapes=scratch,
        compiler_params=pltpu.CompilerParams(vmem_limit_bytes=VMEM_LIMIT),
    )(g_win, g_wuq, g_wkv, g_wout, g_wsp, gvec)


SMALL_ROWS = ((GV_QG, 1, Q_LORA), (GV_KVG, 1, KV_LORA), (GV_SG, 1, GW), (GV_SB, 1, GW),
              (GV_LNG, 1, D_MODEL), (GV_LNB, 1, D_MODEL), (GV_BSP, HEADS, CHUNK))


def _adam_update(g, w, m, v):
    m_new = ADAM_B1 * m + (1.0 - ADAM_B1) * g
    v_new = ADAM_B2 * v + (1.0 - ADAM_B2) * (g * g)
    m_hat = m_new / (1.0 - ADAM_B1 ** ADAM_STEP)
    v_hat = v_new / (1.0 - ADAM_B2 ** ADAM_STEP)
    return -ADAM_LR * (m_hat / (jnp.sqrt(v_hat) + ADAM_EPS) + ADAM_WD * w), m_new, v_new


def _adamw(g_big, w_big, m_big, v_big, gvec, w_small, m_small, v_small):
    nb, ns = len(g_big), len(w_small)

    def body(*refs):
        it = iter(refs)
        take = lambda n: [next(it) for _ in range(n)]
        g_b, w_b, m_b, v_b = take(nb), take(nb), take(nb), take(nb)
        gv = next(it)
        w_s, m_s, v_s = take(ns), take(ns), take(ns)
        g_bo, d_bo, m_bo, v_bo = take(nb), take(nb), take(nb), take(nb)
        g_so, d_so, m_so, v_so = take(ns), take(ns), take(ns), take(ns)
        for n in range(nb):
            gb = g_b[n][...]
            g_bo[n][...] = gb
            d_bo[n][...], m_bo[n][...], v_bo[n][...] = _adam_update(gb, w_b[n][...], m_b[n][...], v_b[n][...])
        for n, (row, nrow, width) in enumerate(SMALL_ROWS):
            gs = gv[row:row + nrow, 0:width]
            g_so[n][...] = gs
            d_so[n][...], m_so[n][...], v_so[n][...] = _adam_update(gs, w_s[n][...], m_s[n][...], v_s[n][...])

    def part(a):
        nd = a.ndim
        if nd > 2 or a.shape[0] % (SUBLANES * ADAM_STEPS) == 0:
            return pl.BlockSpec((a.shape[0] // ADAM_STEPS,) + a.shape[1:], lambda i: (i,) + (0,) * (nd - 1))
        assert a.shape[1] % (LANES * ADAM_STEPS) == 0
        return pl.BlockSpec((a.shape[0], a.shape[1] // ADAM_STEPS), lambda i: (0, i))

    def whole(a):
        nd = a.ndim
        return pl.BlockSpec(a.shape, lambda i: (0,) * nd)

    big = [jax.ShapeDtypeStruct(a.shape, F32) for a in w_big]
    small = [jax.ShapeDtypeStruct(a.shape, F32) for a in w_small]
    return pl.pallas_call(
        body, name="adamw", grid=(ADAM_STEPS,), out_shape=big * 4 + small * 4,
        in_specs=[part(a) for a in g_big + w_big + m_big + v_big] + [whole(gvec)]
        + [whole(a) for a in w_small + m_small + v_small],
        out_specs=[part(a) for a in w_big] * 4 + [whole(a) for a in w_small] * 4,
        compiler_params=pltpu.CompilerParams(dimension_semantics=("arbitrary",), vmem_limit_bytes=VMEM_LIMIT),
    )(*g_big, *w_big, *m_big, *v_big, gvec, *w_small, *m_small, *v_small)


def kernel(x, positions, w_in, q_norm_g, w_uq, kv_norm_g, w_ukv, sgu_norm_g, sgu_norm_b, w_spatial, b_spatial, w_out, ln_g, ln_b, loss_target, m_w_in, m_q_norm_g, m_w_uq, m_kv_norm_g, m_w_ukv, m_sgu_norm_g, m_sgu_norm_b, m_w_spatial, m_b_spatial, m_w_out, m_ln_g, m_ln_b, v_w_in, v_q_norm_g, v_w_uq, v_kv_norm_g, v_w_ukv, v_sgu_norm_g, v_sgu_norm_b, v_w_spatial, v_b_spatial, v_w_out, v_ln_g, v_ln_b):
    seq = x.shape[1]
    x2 = x.reshape(seq, D_MODEL)
    tgt = loss_target.reshape(seq, D_MODEL)
    pos = positions.reshape(seq, 1)

    win, wuq, wkv, a_out = _weight_gather([w_in, w_uq, w_ukv, w_out])
    wout = a_out.reshape(D_MODEL, D_MODEL)

    lane = np.arange(HP)
    half = ROPE // 2
    inv_freq = (1.0 / (ROPE_THETA ** (np.arange(half, dtype=np.float32) / half))).astype(np.float32)
    in_rope = (lane >= NOPE) & (lane < NOPE + ROPE)
    invf = jnp.asarray(np.where(in_rope, inv_freq[(lane - NOPE) % half], 0.0).astype(np.float32))
    m1 = jnp.asarray(np.where((lane >= NOPE) & (lane < NOPE + half), -1.0, 0.0).astype(np.float32))
    m2 = jnp.asarray(np.where((lane >= NOPE + half) & (lane < NOPE + ROPE), 1.0, 0.0).astype(np.float32))
    row = lambda a: jnp.pad(a.astype(F32), (0, D_MODEL - a.shape[0]))
    pvec = jnp.stack([row(q_norm_g), row(kv_norm_g), row(sgu_norm_g), row(sgu_norm_b), row(invf), row(m1),
                      row(m2), row(ln_g), row(ln_b)] + [jnp.zeros((D_MODEL,), F32)] * (PV_ROWS - 9))
    tri = jnp.tril(jnp.ones((CHUNK, CHUNK), dtype=bool))
    wt = jnp.where(tri[None], w_spatial, 0.0).astype(BF16)
    wtt = jnp.swapaxes(wt, 1, 2)
    bsp = jnp.repeat(b_spatial.T, VDIM, axis=1)

    cq, ckv, gate, q, k, v, vt, cs = _fwd_pre(x2, pos, win, wuq, wkv, pvec)
    o, lse = _attn_fwd(q, k, vt)
    dh2, do, dgate, g_wout, g_wsp, gvec = _post(x2, tgt, o, gate, wout, pvec, wt, wtt, bsp)
    dq, dk, dv = _attn_bwd(q, k, v, do, o, lse, cs, pvec)
    gx, g_win, g_wuq, g_wkv, gvec = _bwd_pre(x2, dh2, cq, ckv, cs, dq, dk, dv, dgate, win, wuq, wkv, pvec, gvec)

    r_in, r_uq, r_ukv, r_out, r_wsp, r_vec = _grad_reduce(
        g_win, g_wuq, g_wkv, g_wout.reshape(4, 2, D_MODEL // 8, D_MODEL), g_wsp.reshape(4, 2, CHUNK, CHUNK), gvec)

    big = [w_in, w_uq, w_ukv, w_out, w_spatial]
    flip = lambda a: a.T if a.ndim == 2 and a.shape[1] % LANES else a
    flips = lambda arrs: [flip(a) for a in arrs]
    g_big = flips([r_in.reshape(w_in.shape), r_uq.reshape(w_uq.shape), r_ukv.reshape(w_ukv.shape),
                   r_out.reshape(w_out.shape), r_wsp.reshape(w_spatial.shape)])
    small = lambda qg, kvg, sg, sb, lng, lnb, bs: [qg.reshape(1, -1), kvg.reshape(1, -1), sg.reshape(1, -1),
                                                   sb.reshape(1, -1), lng.reshape(1, -1), lnb.reshape(1, -1), bs]
    res = _adamw(g_big, flips(big), flips([m_w_in, m_w_uq, m_w_ukv, m_w_out, m_w_spatial]),
                 flips([v_w_in, v_w_uq, v_w_ukv, v_w_out, v_w_spatial]), r_vec,
                 small(q_norm_g, kv_norm_g, sgu_norm_g, sgu_norm_b, ln_g, ln_b, b_spatial),
                 small(m_q_norm_g, m_kv_norm_g, m_sgu_norm_g, m_sgu_norm_b, m_ln_g, m_ln_b, m_b_spatial),
                 small(v_q_norm_g, v_kv_norm_g, v_sgu_norm_g, v_sgu_norm_b, v_ln_g, v_ln_b, v_b_spatial))
    res = [r.T if n < 4 * len(big) and r.shape != big[n % len(big)].shape else r for n, r in enumerate(res)]

    def ordered(big, sm):
        vec = lambda n: sm[n].reshape(-1)
        return [big[0], vec(0), big[1], vec(1), big[2], vec(2), vec(3), big[4], sm[6], big[3], vec(4), vec(5)]

    loss = r_vec[GV_LOSS, 0]
    return (loss, gx.reshape(1, seq, D_MODEL), *ordered(res[0:5], res[20:27]), *ordered(res[5:10], res[27:34]),
            *ordered(res[10:15], res[34:41]), *ordered(res[15:20], res[41:48]))
```

```python
import math

import jax
import jax.numpy as jnp
import numpy as np
from jax import lax
from jax.experimental import pallas as pl
from jax.experimental.pallas import tpu as pltpu

F32 = jnp.float32
BF16 = jnp.bfloat16

D_MODEL = 1024
Q_LORA = 256
KV_LORA = 128
HEADS = 8
NOPE = 64
ROPE = 32
VDIM = 64
MLA_W = HEADS * VDIM
GW = 512
CHUNK = 128
HP = 128
PAIRS = HEADS // 2
D_IN = 2464
D_INR = 2560
C_CKV = Q_LORA
C_KR = Q_LORA + KV_LORA
C_GATE = C_KR + HP
ROPE_THETA = 10000.0
DN_ALPHA = 2.0 ** 0.25
EPS = 1e-5
SCALE = 1.0 / math.sqrt(NOPE + ROPE)
SCALE_LOG2E = SCALE * 1.4426950408889634
INV_SQRT2 = 0.7071067811865476
INV_SQRT_2PI = 0.3989422804014327

ADAM_LR = 0.001
ADAM_B1 = 0.9
ADAM_B2 = 0.999
ADAM_EPS = 1e-08
ADAM_WD = 0.01
ADAM_STEP = 10

PV_QG, PV_KVG, PV_SG, PV_SB, PV_INVF, PV_M1, PV_M2, PV_LNG, PV_LNB = range(9)
PV_ROWS = 16
GV_QG, GV_KVG, GV_SG, GV_SB, GV_LNG, GV_LNB, GV_LOSS = range(7)
GV_BSP = 8
GV_ROWS = 16

MESH = pl.DeviceIdType.MESH

FWD_TILE = 1024
POST_TILE = 512
BWD_TILE = 512
ATT_BLK = 512
ADAM_STEPS = 4
SUBLANES, LANES = 8, 128
VMEM_LIMIT = 60 * 1024 * 1024


def _dot(a, b):
    return jnp.dot(a, b, preferred_element_type=F32)


def _dot_nt(a, b):
    return lax.dot_general(a, b, (((1,), (1,)), ((), ())), preferred_element_type=F32)


def _dot_tn(a, b):
    return lax.dot_general(a, b, (((0,), (0,)), ((), ())), preferred_element_type=F32)


def _sigmoid(z):
    return pl.reciprocal(1.0 + jnp.exp(-z), approx=True)


def _gelu_and_grad(x):
    cdf = 0.5 * (1.0 + lax.erf(x * INV_SQRT2))
    return x * cdf, cdf + x * (INV_SQRT_2PI * jnp.exp(-0.5 * x * x))


def _rms_stats(x):
    r = lax.rsqrt(jnp.mean(x * x, axis=-1, keepdims=True) + EPS)
    return x * r, r


def _rms_bwd(dy, g, xh, r):
    dyg = dy * g
    return r * (dyg - xh * jnp.mean(dyg * xh, axis=-1, keepdims=True))


def _ln_stats(x):
    mu = jnp.mean(x, axis=-1, keepdims=True)
    xc = x - mu
    r = lax.rsqrt(jnp.mean(xc * xc, axis=-1, keepdims=True) + EPS)
    return xc * r, r


def _ln_bwd(dy, g, xh, r):
    dxh = dy * g
    return r * (dxh - jnp.mean(dxh, axis=-1, keepdims=True) - xh * jnp.mean(dxh * xh, axis=-1, keepdims=True))


def _rope_fwd(t, c, s1, s2):
    return t * c + pltpu.roll(t, HP - 16, 1) * s1 + pltpu.roll(t, 16, 1) * s2


def _rope_bwd(d, c, s1, s2):
    return d * c + pltpu.roll(d * s1, 16, 1) + pltpu.roll(d * s2, HP - 16, 1)


def _lane_lt64(shape):
    return lax.broadcasted_iota(jnp.int32, shape, len(shape) - 1) < 64


def _spatial_mix(w_ref, src, dst_ref, rows):
    for c in range(rows // CHUNK):
        for p in range(PAIRS):
            blk = src[c * CHUNK:(c + 1) * CHUNK, p * HP:(p + 1) * HP]
            a = _dot(w_ref[2 * p], blk)
            b = _dot(w_ref[2 * p + 1], blk)
            dst_ref[c * CHUNK:(c + 1) * CHUNK, p * HP:(p + 1) * HP] = jnp.where(_lane_lt64(a.shape), a, b)


def _gmlp_fwd(u_pre, v_pre, zb, sg, sb, wt_ref, bsp_ref, sv_ref, rows):
    u, du = _gelu_and_grad(u_pre)
    gv, dgv = _gelu_and_grad(v_pre)
    xh, r = _ln_stats(gv)
    vln = (xh * sg + sb).astype(BF16)
    _spatial_mix(wt_ref, vln, sv_ref, rows)
    bias = bsp_ref[...]
    svb = sv_ref[...] + jnp.concatenate([bias] * (rows // CHUNK), axis=0)
    sig = _sigmoid(zb)
    return u, du, dgv, xh, r, vln, svb, sig


N_GATHER_SEMS = 8


def _weight_gather(shards, pv_const, pv_rows):
    n_arr = len(shards)
    n_in = n_arr + 1 + len(pv_rows)
    w_in, w_uq, w_ukv, _ = shards
    cw = w_in.shape[1]
    gate0 = C_KR + ROPE
    rc = 128

    def body(*refs):
        ins = refs[0:n_arr]
        win_o, wuq_o, wkv_o, out_o, pv_o = refs[n_in:n_in + 5]
        a_in, a_uq, a_ukv, send_sems, recv_sems = refs[n_in + 5:]
        outs = [a_in, a_uq, a_ukv, out_o]
        pv_o[...] = refs[n_arr][...]
        for (prow, vec), ref in zip(pv_rows, refs[n_arr + 1:n_in]):
            pv_o[prow:prow + 1, 0:vec.shape[1]] = ref[...].astype(F32)
        x, y, c = lax.axis_index("x"), lax.axis_index("y"), lax.axis_index("c")
        j = 2 * x + y
        sib = (x, y, 1 - c)
        near = [(1 - x, y, c), (x, 1 - y, c)]
        b_near = [2 * (1 - x) + y, 2 * x + 1 - y]
        b_far = 2 * (1 - x) + 1 - y
        for n in range(n_arr):
            outs[n][j] = ins[n][...].astype(BF16)

        def half(n, blk, core):
            r = shards[n].shape[0] // 2
            return outs[n].at[blk, pl.ds(pl.multiple_of(core * r, 16), r), :]

        def quarter(n, blk, core, part):
            q = shards[n].shape[0] // 4
            return outs[n].at[blk, pl.ds(pl.multiple_of((2 * core + part) * q, 16), q), :]

        def copy(k, ref, to):
            return pltpu.make_async_remote_copy(
                src_ref=ref, dst_ref=ref, send_sem=send_sems.at[k], recv_sem=recv_sems.at[k],
                device_id=to, device_id_type=MESH)

        sent = [copy(N_GATHER_SEMS * n + kk, half(n, j, c), near[kk]) for n in range(n_arr) for kk in range(2)]
        for cp in sent:
            cp.start()

        def send(k, ref, to):
            sent.append(copy(k, ref, to))
            sent[-1].start()

        for n in range(n_arr):
            k0 = N_GATHER_SEMS * n
            for kk in range(2):
                copy(k0 + kk, half(n, b_near[kk], c), near[kk]).wait_recv()
                send(k0 + 2 + kk, quarter(n, b_near[kk], c, kk), near[1 - kk])
                send(k0 + 4 + kk, half(n, b_near[kk], c), sib)
        for n in range(n_arr):
            k0 = N_GATHER_SEMS * n
            for kk in range(2):
                copy(k0 + 2 + kk, quarter(n, b_far, c, kk), near[1 - kk]).wait_recv()
                send(k0 + 6 + kk, quarter(n, b_far, c, kk), sib)
        for n in range(n_arr):
            k0 = N_GATHER_SEMS * n
            for kk in range(2):
                copy(k0 + 4 + kk, half(n, b_near[kk], 1 - c), sib).wait_recv()
                copy(k0 + 6 + kk, quarter(n, b_far, 1 - c, kk), sib).wait_recv()

        def lay_in(r, carry):
            rows = pl.ds(pl.multiple_of(r * rc, rc), rc)
            win_o[rows, 0:C_KR] = a_in[0, rows, 0:C_KR]
            win_o[rows, C_KR:C_KR + NOPE] = jnp.zeros((rc, NOPE), BF16)
            win_o[rows, C_KR + NOPE:C_KR + NOPE + ROPE] = a_in[0, rows, C_KR:gate0]
            win_o[rows, C_KR + NOPE + ROPE:C_GATE] = jnp.zeros((rc, HP - NOPE - ROPE), BF16)
            win_o[rows, C_GATE:C_GATE + cw - gate0] = a_in[0, rows, gate0:cw]
            for blk in range(1, 4):
                win_o[rows, D_INR - (4 - blk) * cw:D_INR - (3 - blk) * cw] = a_in[blk, rows, :]
            return carry

        lax.fori_loop(0, D_MODEL // rc, lay_in, 0)
        wuq_o[...] = jnp.zeros(wuq_o.shape, BF16)
        wkv_o[...] = jnp.zeros(wkv_o.shape, BF16)
        qw = NOPE + ROPE
        for h in range(HEADS):
            blk, e = h // 2, h % 2
            wuq_o[:, h * HP:h * HP + qw] = a_uq[blk, :, e * qw:(e + 1) * qw]
            wkv_o[:, h * HP:h * HP + NOPE] = a_ukv[blk, :, e * HP:e * HP + NOPE]
            wkv_o[:, HEADS * HP + h * VDIM:HEADS * HP + (h + 1) * VDIM] = a_ukv[blk, :, e * HP + NOPE:(e + 1) * HP]
        for cp in sent:
            cp.wait_send()

    assert cw > gate0 and C_GATE + 4 * cw - gate0 == D_INR and NOPE + VDIM == HP and HEADS == 8
    vmem = pl.BlockSpec(memory_space=pltpu.VMEM)
    gathered = lambda a: (4,) + a.shape
    return pl.pallas_call(
        body, name="weight_gather",
        out_shape=[jax.ShapeDtypeStruct((D_MODEL, D_INR), BF16), jax.ShapeDtypeStruct((Q_LORA, HEADS * HP), BF16),
                   jax.ShapeDtypeStruct((KV_LORA, HEADS * HP + MLA_W), BF16),
                   jax.ShapeDtypeStruct(gathered(shards[3]), BF16), jax.ShapeDtypeStruct(pv_const.shape, F32)],
        in_specs=[vmem] * n_in, out_specs=[vmem] * 5,
        scratch_shapes=[pltpu.VMEM(gathered(a), BF16) for a in shards[0:3]]
        + [pltpu.SemaphoreType.DMA((N_GATHER_SEMS * n_arr,)), pltpu.SemaphoreType.DMA((N_GATHER_SEMS * n_arr,))],
        compiler_params=pltpu.CompilerParams(vmem_limit_bytes=VMEM_LIMIT),
    )(*shards, pv_const, *[vec for _, vec in pv_rows])


def _fwd_pre(x, pos, win, wuq, wkv, pvec):
    seq = x.shape[0]
    t = FWD_TILE

    def body(x_ref, pos_ref, win_ref, wuq_ref, wkv_ref, pv_ref,
             cq_o, ckv_o, gate_o, q_o, k_o, v_o, vt_o, cs_o):
        xb = x_ref[...].astype(BF16)
        proj = _dot(xb, win_ref[:, 0:C_GATE])
        cq = proj[:, 0:C_CKV]
        ckv = proj[:, C_CKV:C_KR]
        kr = proj[:, C_KR:C_GATE]
        cq_o[...] = cq
        ckv_o[...] = ckv

        ang = pos_ref[...].astype(F32) * pv_ref[PV_INVF:PV_INVF + 1, 0:HP]
        cos = jnp.cos(ang)
        sin = jnp.sin(ang)
        cs_o[:, 0:HP] = cos
        cs_o[:, HP:2 * HP] = sin
        s1 = sin * pv_ref[PV_M1:PV_M1 + 1, 0:HP]
        s2 = sin * pv_ref[PV_M2:PV_M2 + 1, 0:HP]

        cqh, _ = _rms_stats(cq)
        q_all = _dot((cqh * pv_ref[PV_QG:PV_QG + 1, 0:Q_LORA]).astype(BF16), wuq_ref[...])
        ckvh, _ = _rms_stats(ckv)
        kv_all = _dot((ckvh * pv_ref[PV_KVG:PV_KVG + 1, 0:KV_LORA]).astype(BF16), wkv_ref[...])
        krr = _rope_fwd(kr, cos, s1, s2)
        for h in range(HEADS):
            sl = slice(h * HP, (h + 1) * HP)
            q_o[:, sl] = (_rope_fwd(q_all[:, sl], cos, s1, s2) * SCALE_LOG2E).astype(BF16)
            k_o[:, sl] = (kv_all[:, sl] + krr).astype(BF16)
        val = kv_all[:, HEADS * HP:].astype(BF16)
        v_o[...] = val
        for blk in range(t // ATT_BLK):
            vt_o[blk] = val[blk * ATT_BLK:(blk + 1) * ATT_BLK, :].T
        gate_o[...] = _dot(xb, win_ref[:, C_GATE:D_INR]).astype(BF16)

    tile = lambda w: pl.BlockSpec((t, w), lambda i: (i, 0))
    full = lambda a: pl.BlockSpec(a.shape, lambda i: (0,) * a.ndim)
    outs = [(Q_LORA, F32), (KV_LORA, F32), (2048, BF16), (HEADS * HP, BF16), (HEADS * HP, BF16), (MLA_W, BF16)]
    assert t % ATT_BLK == 0
    out_specs = [tile(w) for w, _ in outs]
    out_specs += [pl.BlockSpec((t // ATT_BLK, MLA_W, ATT_BLK), lambda i: (i, 0, 0)), tile(2 * HP)]
    out_shape = [jax.ShapeDtypeStruct((seq, w), d) for w, d in outs]
    out_shape += [jax.ShapeDtypeStruct((seq // ATT_BLK, MLA_W, ATT_BLK), BF16), jax.ShapeDtypeStruct((seq, 2 * HP), F32)]
    return pl.pallas_call(
        body, name="fwd_pre", grid=(seq // t,),
        in_specs=[tile(D_MODEL), tile(1), full(win), full(wuq), full(wkv), full(pvec)],
        out_specs=out_specs, out_shape=out_shape,
        compiler_params=pltpu.CompilerParams(dimension_semantics=("arbitrary",), vmem_limit_bytes=VMEM_LIMIT),
    )(x, pos, win, wuq, wkv, pvec)


def _attn_fwd(q, k, vt):
    seq = q.shape[0]
    b = ATT_BLK
    nq = seq // b
    assert nq % 2 == 0
    assert nq % 4 == 0
    n_wide = sum(i // 4 for i in range(nq))

    def body(q_ref, k_ref, vt_ref, o_o, lse_o, m_ref, l_ref, acc_ref, s_even, s_odd):
        m_ref[...] = jnp.full(m_ref.shape, -jnp.inf, F32)
        l_ref[...] = jnp.zeros(l_ref.shape, F32)
        acc_ref[...] = jnp.zeros(acc_ref.shape, F32)

        def scores(i, j, s_ref, nkb):
            qrows = pl.ds(pl.multiple_of(i * b, b), b)
            krows = pl.ds(pl.multiple_of(j * b, b), nkb * b)
            for a in range(2):
                s_ref[a, 0:nkb * b, :] = _dot_nt(k_ref[krows, a * HP:(a + 1) * HP], q_ref[qrows, a * HP:(a + 1) * HP])

        def consume(i, j, s_ref, nkb, masked):
            for a in range(2):
                st = s_ref[a, 0:nkb * b, :]
                if masked:
                    ki = lax.broadcasted_iota(jnp.int32, st.shape, 0)
                    qi = lax.broadcasted_iota(jnp.int32, st.shape, 1)
                    st = jnp.where(ki <= qi, st, -jnp.inf)
                m_prev = m_ref[i, a:a + 1, :]
                m_new = jnp.maximum(m_prev, jnp.max(st, axis=0, keepdims=True))
                alpha = jnp.exp2(m_prev - m_new)
                pt = jnp.exp2(st - m_new)
                ptb = pt.astype(BF16)
                l_ref[i, a:a + 1, :] = alpha * l_ref[i, a:a + 1, :] + jnp.sum(pt, axis=0, keepdims=True)
                pv = _dot(vt_ref[j], ptb[0:b, :])
                for kb in range(1, nkb):
                    pv = pv + _dot(vt_ref[j + kb], ptb[kb * b:(kb + 1) * b, :])
                acc_ref[i, a] = alpha * acc_ref[i, a] + pv
                m_ref[i, a:a + 1, :] = m_new

        def run(count, first, following, nkb, masked):
            if count == 0:
                return
            scores(*first, s_even, nkb)

            def two(u, ij):
                nxt = following(*ij)
                scores(*nxt, s_odd, nkb)
                consume(*ij, s_even, nkb, masked)
                nxt2 = following(*nxt)
                scores(*nxt2, s_even, nkb)
                consume(*nxt, s_odd, nkb, masked)
                return nxt2

            last = lax.fori_loop(0, count // 2, two, tuple(jnp.int32(c) for c in first))
            if count % 2:
                consume(*last, s_even, nkb, masked)

        def clamp(i):
            return jnp.minimum(i, nq - 1)

        def next_wide(i, j):
            wrap = j + 8 > i
            return clamp(jnp.where(wrap, i + 1, i)), jnp.where(wrap, 0, j + 4)

        def next_pair(i, j):
            low = lax.rem(i, 4) == 2
            return clamp(jnp.where(low, i + 1, i + 3)), jnp.minimum(jnp.where(low, j, j + 4), nq - 4)

        run(n_wide, (4, 0), next_wide, 4, False)
        run(nq // 2, (2, 0), next_pair, 2, False)
        run(nq // 2, (1, 0), lambda i, j: (clamp(i + 2), jnp.minimum(j + 2, nq - 2)), 1, False)
        run(nq, (0, 0), lambda i, j: (clamp(i + 1), clamp(j + 1)), 1, True)
        top = lax.broadcasted_iota(jnp.int32, (HP, b), 0) < 64

        def finish(i, carry):
            rows = pl.ds(pl.multiple_of(i * b, b), b)
            o_o[rows, :] = jnp.where(top, acc_ref[i, 0] / l_ref[i, 0:1, :], acc_ref[i, 1] / l_ref[i, 1:2, :]).T
            lse_o[i] = m_ref[i, 0:2, :] + jnp.log2(l_ref[i, 0:2, :])
            return carry

        lax.fori_loop(0, nq, finish, 0)

    return pl.pallas_call(
        body, name="attn_fwd", grid=(PAIRS,),
        in_specs=[pl.BlockSpec((seq, 2 * HP), lambda p: (0, p)),
                  pl.BlockSpec((seq, 2 * HP), lambda p: (0, p)),
                  pl.BlockSpec((nq, HP, b), lambda p: (0, p, 0))],
        out_specs=[pl.BlockSpec((seq, HP), lambda p: (0, p)),
                   pl.BlockSpec((None, nq, 2, b), lambda p: (p, 0, 0, 0))],
        out_shape=[jax.ShapeDtypeStruct((seq, MLA_W), F32),
                   jax.ShapeDtypeStruct((PAIRS, nq, 2, b), F32)],
        scratch_shapes=[pltpu.VMEM((nq, 8, b), F32), pltpu.VMEM((nq, 8, b), F32), pltpu.VMEM((nq, 2, HP, b), F32),
                        pltpu.VMEM((2, 4 * b, b), F32), pltpu.VMEM((2, 4 * b, b), F32)],
        compiler_params=pltpu.CompilerParams(dimension_semantics=("arbitrary",), vmem_limit_bytes=VMEM_LIMIT),
    )(q, k, vt)


def _post(x, tgt, o, gate, wout, pvec, wt, wtt, bsp):
    seq = x.shape[0]
    t = POST_TILE
    nt = seq // t

    def body(x_ref, tgt_ref, o_ref, gate_ref, wout_ref, pv_ref, wt_ref, wtt_ref, bsp_ref,
             dh2_o, do_o, dgate_o, gwout_o, gwsp_o, vec_o, sv_ref, dvln_ref, bacc_ref):
        i = pl.program_id(0)

        @pl.when(i == 0)
        def _():
            gwout_o[...] = jnp.zeros_like(gwout_o)
            gwsp_o[...] = jnp.zeros_like(gwsp_o)
            vec_o[...] = jnp.zeros_like(vec_o)
            bacc_ref[...] = jnp.zeros_like(bacc_ref)

        za = gate_ref[:, 0:512].astype(F32)
        u_pre = gate_ref[:, 512:1024].astype(F32)
        v_pre = gate_ref[:, 1024:1536].astype(F32)
        zb = gate_ref[:, 1536:2048].astype(F32)
        sg = pv_ref[PV_SG:PV_SG + 1, 0:GW]
        sb = pv_ref[PV_SB:PV_SB + 1, 0:GW]
        lng = pv_ref[PV_LNG:PV_LNG + 1, :]
        lnb = pv_ref[PV_LNB:PV_LNB + 1, :]
        o = o_ref[...]

        sig_a = _sigmoid(za)
        silu_a = za * sig_a
        u, du, dgv, xh, r, vln, svb, sig_b = _gmlp_fwd(u_pre, v_pre, zb, sg, sb, wt_ref, bsp_ref, sv_ref, t)
        silu_b = zb * sig_b
        sgu = u * svb
        merged = jnp.concatenate([o * silu_a, sgu * silu_b], axis=1).astype(BF16)
        h2 = DN_ALPHA * x_ref[...] + _dot(merged, wout_ref[...])
        xh2, r2 = _ln_stats(h2)
        err = xh2 * lng + lnb - tgt_ref[...]
        d_out = err * (1.0 / D_MODEL)
        vec_o[GV_LNG:GV_LNG + 1, :] += jnp.sum(d_out * xh2, axis=0, keepdims=True)
        vec_o[GV_LNB:GV_LNB + 1, :] += jnp.sum(d_out, axis=0, keepdims=True)
        vec_o[GV_LOSS:GV_LOSS + 1, :] += jnp.sum(err * err, axis=0, keepdims=True) * (0.5 / D_MODEL)

        d_h2 = _ln_bwd(d_out, lng, xh2, r2)
        dh2_o[...] = d_h2
        dh2b = d_h2.astype(BF16)
        gwout_o[...] += _dot_tn(merged, dh2b)
        d_m = _dot_nt(dh2b, wout_ref[...])
        d_oa = d_m[:, 0:512]
        d_ob = d_m[:, 512:1024]
        do_o[...] = (d_oa * silu_a).astype(BF16)
        dgate_o[:, 0:512] = (d_oa * o * (sig_a * (1.0 + za * (1.0 - sig_a)))).astype(BF16)
        dgate_o[:, 1536:2048] = (d_ob * sgu * (sig_b * (1.0 + zb * (1.0 - sig_b)))).astype(BF16)
        d_sgu = d_ob * silu_b
        dgate_o[:, 512:1024] = (d_sgu * svb * du).astype(BF16)
        d_sv = d_sgu * u
        acc = bacc_ref[...]
        for c in range(t // CHUNK):
            acc = acc + d_sv[c * CHUNK:(c + 1) * CHUNK, :]
        bacc_ref[...] = acc
        d_svb = d_sv.astype(BF16)
        for c in range(t // CHUNK):
            for p in range(PAIRS):
                blk = d_svb[c * CHUNK:(c + 1) * CHUNK, p * HP:(p + 1) * HP]
                vblk = vln[c * CHUNK:(c + 1) * CHUNK, p * HP:(p + 1) * HP]
                first = _lane_lt64(blk.shape)
                gwsp_o[2 * p] += _dot_nt(jnp.where(first, blk, jnp.zeros_like(blk)), vblk)
                gwsp_o[2 * p + 1] += _dot_nt(jnp.where(first, jnp.zeros_like(blk), blk), vblk)
        _spatial_mix(wtt_ref, d_svb, dvln_ref, t)
        d_vln = dvln_ref[...]
        vec_o[GV_SG:GV_SG + 1, 0:GW] += jnp.sum(d_vln * xh, axis=0, keepdims=True)
        vec_o[GV_SB:GV_SB + 1, 0:GW] += jnp.sum(d_vln, axis=0, keepdims=True)
        dgate_o[:, 1024:1536] = (_ln_bwd(d_vln, sg, xh, r) * dgv).astype(BF16)


        @pl.when(i == nt - 1)
        def _():
            tri = (lax.broadcasted_iota(jnp.int32, (CHUNK, CHUNK), 1)
                   <= lax.broadcasted_iota(jnp.int32, (CHUNK, CHUNK), 0))
            for h in range(HEADS):
                gwsp_o[h] = jnp.where(tri, gwsp_o[h], 0.0)
            lane = lax.broadcasted_iota(jnp.int32, (CHUNK, HP), 1)
            res = jnp.zeros((CHUNK, HP), F32)
            for h in range(HEADS):
                p, a = divmod(h, 2)
                blk = bacc_ref[:, p * HP:(p + 1) * HP]
                part = jnp.where(_lane_lt64(blk.shape) == (a == 0), blk, 0.0)
                res = jnp.where(lane == h, jnp.sum(part, axis=-1, keepdims=True), res)
            vec_o[GV_BSP:GV_BSP + HEADS, 0:HP] = res.T[0:HEADS, :]
            lane1 = lax.broadcasted_iota(jnp.int32, (1, D_MODEL), 1)
            total = jnp.sum(vec_o[GV_LOSS:GV_LOSS + 1, :], axis=-1, keepdims=True)
            vec_o[GV_LOSS:GV_LOSS + 1, :] = jnp.where(lane1 == 0, total, 0.0)

    tile = lambda w: pl.BlockSpec((t, w), lambda i: (i, 0))
    full = lambda a: pl.BlockSpec(a.shape, lambda i: (0,) * a.ndim)
    const = lambda s: pl.BlockSpec(s, lambda i: (0,) * len(s))
    return pl.pallas_call(
        body, name="post", grid=(nt,),
        in_specs=[tile(D_MODEL), tile(D_MODEL), tile(MLA_W), tile(2048), full(wout), full(pvec),
                  full(wt), full(wtt), full(bsp)],
        out_specs=[tile(D_MODEL), tile(MLA_W), tile(2048), const((D_MODEL, D_MODEL)),
                   const((HEADS, CHUNK, CHUNK)), const((GV_ROWS, D_MODEL))],
        out_shape=[jax.ShapeDtypeStruct((seq, D_MODEL), F32), jax.ShapeDtypeStruct((seq, MLA_W), BF16),
                   jax.ShapeDtypeStruct((seq, 2048), BF16), jax.ShapeDtypeStruct((D_MODEL, D_MODEL), F32),
                   jax.ShapeDtypeStruct((HEADS, CHUNK, CHUNK), F32), jax.ShapeDtypeStruct((GV_ROWS, D_MODEL), F32)],
        scratch_shapes=[pltpu.VMEM((t, GW), F32), pltpu.VMEM((t, GW), F32), pltpu.VMEM((CHUNK, GW), F32)],
        compiler_params=pltpu.CompilerParams(dimension_semantics=("arbitrary",), vmem_limit_bytes=VMEM_LIMIT),
    )(x, tgt, o, gate, wout, pvec, wt, wtt, bsp)


def _attn_bwd(q, k, v, do, o, lse, cs, pvec):
    seq = q.shape[0]
    b = ATT_BLK
    nq = seq // b

    def body(q_ref, k_ref, v_ref, do_ref, o_ref, lse_ref, cs_ref, pv_ref, dq_o, dk_o, dv_o, dk_acc, dv_acc):
        i = pl.program_id(1)

        @pl.when(i == 0)
        def _():
            dk_acc[...] = jnp.zeros_like(dk_acc)
            dv_acc[...] = jnp.zeros_like(dv_acc)

        first = _lane_lt64((b, HP))
        do = do_ref[...]
        zero = jnp.zeros_like(do)
        dos = [jnp.where(first, do, zero), jnp.where(first, zero, do)]
        prod_t = (do.astype(F32) * o_ref[...]).T
        deltas = [jnp.sum(prod_t[0:64, :], axis=0, keepdims=True),
                  jnp.sum(prod_t[64:128, :], axis=0, keepdims=True)]
        lses = [lse_ref[0:1, :], lse_ref[1:2, :]]
        qs = [q_ref[:, a * HP:(a + 1) * HP] for a in range(2)]

        def step(j, dqs, masked, nk=b):
            rows = pl.ds(pl.multiple_of(j * b, b), nk)
            vb = v_ref[rows, :]
            new_dq = []
            dvs = []
            for a in range(2):
                kb = k_ref[rows, a * HP:(a + 1) * HP]
                pt = jnp.exp2(_dot_nt(kb, qs[a]) - lses[a])
                if masked:
                    ki = lax.broadcasted_iota(jnp.int32, pt.shape, 0)
                    qi = lax.broadcasted_iota(jnp.int32, pt.shape, 1) + (nk - b)
                    pt = jnp.where(ki <= qi, pt, 0.0)
                dvs.append(_dot(pt.astype(BF16), do))
                dpt = _dot_nt(vb, dos[a])
                dst = (pt * (dpt - deltas[a])).astype(BF16)
                dk_acc[rows, a * HP:(a + 1) * HP] += _dot(dst, qs[a])
                new_dq.append(dqs[a] + _dot_tn(dst, kb))
            dv_acc[rows, :] += jnp.where(_lane_lt64((nk, HP)), dvs[0], dvs[1])
            return tuple(new_dq)

        init = (jnp.zeros((b, HP), F32), jnp.zeros((b, HP), F32))
        dqs = lax.fori_loop(0, i // 4, lambda jj, cr: step(4 * jj, cr, False, 4 * b), init)
        last = [lambda cr, w=w: step(4 * (i // 4), cr, True, w * b) for w in (1, 2, 3, 4)]
        dqs = lax.switch(i % 4, last, dqs)
        cos = cs_ref[:, 0:HP]
        sin = cs_ref[:, HP:2 * HP]
        s1 = sin * pv_ref[PV_M1:PV_M1 + 1, 0:HP]
        s2 = sin * pv_ref[PV_M2:PV_M2 + 1, 0:HP]
        for a in range(2):
            dq_o[:, a * HP:(a + 1) * HP] = _rope_bwd(dqs[a] * SCALE, cos, s1, s2).astype(BF16)

        @pl.when(i == nq - 1)
        def _():
            dk_o[...] = (dk_acc[...] * (SCALE / SCALE_LOG2E)).astype(BF16)
            dv_o[...] = dv_acc[...].astype(BF16)

    return pl.pallas_call(
        body, name="attn_bwd", grid=(PAIRS, nq),
        in_specs=[pl.BlockSpec((b, 2 * HP), lambda p, i: (i, p)),
                  pl.BlockSpec((seq, 2 * HP), lambda p, i: (0, p)),
                  pl.BlockSpec((seq, HP), lambda p, i: (0, p)),
                  pl.BlockSpec((b, HP), lambda p, i: (i, p)),
                  pl.BlockSpec((b, HP), lambda p, i: (i, p)),
                  pl.BlockSpec((None, None, 2, b), lambda p, i: (p, i, 0, 0)),
                  pl.BlockSpec((b, 2 * HP), lambda p, i: (i, 0)),
                  pl.BlockSpec(pvec.shape, lambda p, i: (0, 0))],
        out_specs=[pl.BlockSpec((b, 2 * HP), lambda p, i: (i, p)),
                   pl.BlockSpec((seq, 2 * HP), lambda p, i: (0, p)),
                   pl.BlockSpec((seq, HP), lambda p, i: (0, p))],
        out_shape=[jax.ShapeDtypeStruct((seq, HEADS * HP), BF16),
                   jax.ShapeDtypeStruct((seq, HEADS * HP), BF16),
                   jax.ShapeDtypeStruct((seq, MLA_W), BF16)],
        scratch_shapes=[pltpu.VMEM((seq, 2 * HP), F32), pltpu.VMEM((seq, HP), F32)],
        compiler_params=pltpu.CompilerParams(dimension_semantics=("arbitrary", "arbitrary"),
                                             vmem_limit_bytes=VMEM_LIMIT),
    )(q, k, v, do, o, lse, cs, pvec)


def _bwd_pre(x, dh2, cq, ckv, cs, dq, dk, dv, dgate, win, wuq, wkv, pvec, gvec):
    seq = x.shape[0]
    t = BWD_TILE

    def body(x_ref, dh2_ref, cq_ref, ckv_ref, cs_ref, dq_ref, dk_ref, dv_ref, dgate_ref,
             win_ref, wuq_ref, wkv_ref, pv_ref, gv_ref, gx_o, gwin_o, gwuq_o, gwkv_o, vec_o):
        i = pl.program_id(0)

        @pl.when(i == 0)
        def _():
            gwin_o[...] = jnp.zeros_like(gwin_o)
            gwuq_o[...] = jnp.zeros_like(gwuq_o)
            gwkv_o[...] = jnp.zeros_like(gwkv_o)
            vec_o[...] = gv_ref[...]

        xb = x_ref[...].astype(BF16)
        dgate = dgate_ref[...]
        gwin_o[:, C_GATE:D_INR] += _dot_tn(xb, dgate)
        gx_gate = _dot_nt(dgate, win_ref[:, C_GATE:D_INR])

        qg = pv_ref[PV_QG:PV_QG + 1, 0:Q_LORA]
        kvg = pv_ref[PV_KVG:PV_KVG + 1, 0:KV_LORA]
        dq = dq_ref[...]
        cqh, rq = _rms_stats(cq_ref[...])
        d_cqn = _dot_nt(dq, wuq_ref[...])
        gwuq_o[...] += _dot_tn((cqh * qg).astype(BF16), dq)
        vec_o[GV_QG:GV_QG + 1, 0:Q_LORA] += jnp.sum(d_cqn * cqh, axis=0, keepdims=True)
        d_cq = _rms_bwd(d_cqn, qg, cqh, rq)

        dk = dk_ref[...]
        dkv = jnp.concatenate([dk, dv_ref[...]], axis=1)
        ckvh, rkv = _rms_stats(ckv_ref[...])
        d_ckvn = _dot_nt(dkv, wkv_ref[...])
        gwkv_o[...] += _dot_tn((ckvh * kvg).astype(BF16), dkv)
        vec_o[GV_KVG:GV_KVG + 1, 0:KV_LORA] += jnp.sum(d_ckvn * ckvh, axis=0, keepdims=True)
        d_ckv = _rms_bwd(d_ckvn, kvg, ckvh, rkv)

        dks = dk[:, 0:HP].astype(F32)
        for h in range(1, HEADS):
            dks = dks + dk[:, h * HP:(h + 1) * HP].astype(F32)
        cos = cs_ref[:, 0:HP]
        sin = cs_ref[:, HP:2 * HP]
        d_kr = _rope_bwd(dks, cos, sin * pv_ref[PV_M1:PV_M1 + 1, 0:HP], sin * pv_ref[PV_M2:PV_M2 + 1, 0:HP])

        d_lat = jnp.concatenate([d_cq.astype(BF16), d_ckv.astype(BF16), d_kr.astype(BF16)], axis=1)
        gwin_o[:, 0:C_GATE] += _dot_tn(xb, d_lat)
        gx_o[...] = DN_ALPHA * dh2_ref[...] + gx_gate + _dot_nt(d_lat, win_ref[:, 0:C_GATE])

    tile = lambda w: pl.BlockSpec((t, w), lambda i: (i, 0))
    full = lambda a: pl.BlockSpec(a.shape, lambda i: (0,) * a.ndim)
    const = lambda s: pl.BlockSpec(s, lambda i: (0,) * len(s))
    return pl.pallas_call(
        body, name="bwd_pre", grid=(seq // t,),
        in_specs=[tile(D_MODEL), tile(D_MODEL), tile(Q_LORA), tile(KV_LORA), tile(2 * HP), tile(HEADS * HP),
                  tile(HEADS * HP), tile(MLA_W), tile(2048), full(win), full(wuq), full(wkv), full(pvec), full(gvec)],
        out_specs=[tile(D_MODEL), const((D_MODEL, D_INR)), const((Q_LORA, HEADS * HP)),
                   const((KV_LORA, HEADS * HP + MLA_W)), const((GV_ROWS, D_MODEL))],
        out_shape=[jax.ShapeDtypeStruct((seq, D_MODEL), F32), jax.ShapeDtypeStruct((D_MODEL, D_INR), F32),
                   jax.ShapeDtypeStruct((Q_LORA, HEADS * HP), F32),
                   jax.ShapeDtypeStruct((KV_LORA, HEADS * HP + MLA_W), F32),
                   jax.ShapeDtypeStruct((GV_ROWS, D_MODEL), F32)],
        compiler_params=pltpu.CompilerParams(dimension_semantics=("arbitrary",), vmem_limit_bytes=VMEM_LIMIT),
    )(x, dh2, cq, ckv, cs, dq, dk, dv, dgate, win, wuq, wkv, pvec, gvec)


def _grad_reduce(g_win, g_wuq, g_wkv, g_wout, g_wsp, gvec):
    gate0 = C_KR + ROPE
    cw = (D_INR - C_GATE + gate0) // 4
    qw = NOPE + ROPE
    rc = 128
    half_shapes = [(D_MODEL // 2, cw), (Q_LORA // 2, 2 * qw), (KV_LORA // 2, 2 * HP), g_wout.shape[2:], g_wsp.shape[2:]]
    n_arr = len(half_shapes)
    n_lay = 3
    order = [3, 0, 1, 2, 4]
    n_big = n_arr - 1
    k1 = lambda n, blk: 4 * n + blk
    k2 = lambda n, kk: 4 * n_arr + 3 * n + kk
    k3 = lambda n: 7 * n_arr + n
    k3w = lambda k: 7 * n_arr + n_big + k
    kv = lambda k: 7 * n_arr + n_big + 7 + k
    n_sem = 7 * n_arr + n_big + 14

    def body(*refs):
        gwin, gwuq, gwkv = refs[0:n_lay]
        gv = refs[n_arr]
        outs, ov = refs[n_arr + 1:2 * n_arr + 1], refs[2 * n_arr + 1]
        r1 = refs[2 * n_arr + 2:3 * n_arr + 2]
        r2 = refs[3 * n_arr + 2:4 * n_arr + 2]
        s2 = refs[4 * n_arr + 2:5 * n_arr + 2]
        g = refs[5 * n_arr + 2:5 * n_arr + 2 + n_lay] + refs[n_lay:n_arr]
        vbuf, send_sems, recv_sems = refs[5 * n_arr + 2 + n_lay:]
        x, y, c = lax.axis_index("x"), lax.axis_index("y"), lax.axis_index("c")
        j = 2 * x + y
        me = 2 * j + c
        sib = (x, y, 1 - c)
        chips = [(1 - x, y), (x, 1 - y), (1 - x, 1 - y)]
        others = [sib] + [(px, py, pc) for (px, py) in chips for pc in (c, 1 - c)]

        def copy(k, src, dst, to):
            return pltpu.make_async_remote_copy(
                src_ref=src, dst_ref=dst, send_sem=send_sems.at[k], recv_sem=recv_sems.at[k],
                device_id=to, device_id_type=MESH)

        def lay_in(r, carry):
            src = pl.ds(pl.multiple_of(r * rc, rc), rc)
            hf = r // (D_MODEL // 2 // rc)
            dst = pl.ds(pl.multiple_of((r % (D_MODEL // 2 // rc)) * rc, rc), rc)
            g[0][0, hf, dst, 0:C_KR] = gwin[src, 0:C_KR]
            g[0][0, hf, dst, C_KR:gate0] = gwin[src, C_KR + NOPE:C_KR + NOPE + ROPE]
            g[0][0, hf, dst, gate0:cw] = gwin[src, C_GATE:C_GATE + cw - gate0]
            for blk in range(1, 4):
                g[0][blk, hf, dst, :] = gwin[src, D_INR - (4 - blk) * cw:D_INR - (3 - blk) * cw]
            return carry

        def lay_heads():
            for h in range(HEADS):
                blk, e = h // 2, h % 2
                for hf in range(2):
                    rq = slice(hf * Q_LORA // 2, (hf + 1) * Q_LORA // 2)
                    rk = slice(hf * KV_LORA // 2, (hf + 1) * KV_LORA // 2)
                    g[1][blk, hf, :, e * qw:(e + 1) * qw] = gwuq[rq, h * HP:h * HP + qw]
                    g[2][blk, hf, :, e * HP:e * HP + NOPE] = gwkv[rk, h * HP:h * HP + NOPE]
                    g[2][blk, hf, :, e * HP + NOPE:(e + 1) * HP] = gwkv[rk, HEADS * HP + h * VDIM:HEADS * HP + (h + 1) * VDIM]

        l1, l2 = [], []

        def level1(n):
            for blk in range(4):
                l1.append(copy(k1(n, blk), g[n].at[blk, 1 - c], r1[n].at[blk], sib))
                l1[-1].start()

        def level2(n):
            for blk in range(4):
                copy(k1(n, blk), g[n].at[blk, c], r1[n].at[blk], sib).wait_recv()
            for blk in range(4):
                r1[n][blk] = g[n][blk, c] + r1[n][blk]
                s2[n][blk] = r1[n][blk].astype(BF16)
            for kk, (px, py) in enumerate(chips):
                l2.append(copy(k2(n, kk), s2[n].at[2 * px + py], r2[n].at[kk], (px, py, c)))
                l2[-1].start()

        lv = [copy(kv(k), gv, vbuf.at[me], to) for k, to in enumerate(others)]
        for cp in lv:
            cp.start()
        for n in range(n_lay, n_arr):
            level1(n)
        lay_heads()
        level1(1)
        level1(2)
        level2(order[0])
        lax.fori_loop(0, D_MODEL // rc, lay_in, 0)
        level1(0)
        for n in order[1:]:
            level2(n)

        l3 = []
        for n in order:
            for kk in range(3):
                copy(k2(n, kk), s2[n].at[0], r2[n].at[kk], sib).wait_recv()
            red = ((r1[n][j] + r2[n][0].astype(F32)) + r2[n][1].astype(F32)) + r2[n][2].astype(F32)
            if n < n_big:
                outs[n][c] = red
                back = [copy(k3(n), outs[n].at[c], outs[n].at[c], sib)]
            else:
                outs[n][j, c] = red
                back = [copy(k3w(k), outs[n].at[j, c], outs[n].at[j, c], to) for k, to in enumerate(others)]
            for cp in back:
                cp.start()
            l3 += back
        for n in range(n_big):
            copy(k3(n), outs[n].at[1 - c], outs[n].at[1 - c], sib).wait_recv()
        for k, (px, py, pc) in enumerate(others):
            landed = outs[n_big].at[2 * px + py, pc]
            copy(k3w(k), landed, landed, (px, py, pc)).wait_recv()
            copy(kv(k), gv, vbuf.at[4 * px + 2 * py + pc], (px, py, pc)).wait_recv()
        vbuf[me] = gv[...]
        total = vbuf[0]
        for d in range(1, 8):
            total = total + vbuf[d]
        ov[...] = total
        for cp in l1 + lv + l2 + l3:
            cp.wait_send()

    vmem = pl.BlockSpec(memory_space=pltpu.VMEM)
    assert g_win.shape == (D_MODEL, D_INR) and 4 * cw == D_INR - C_GATE + gate0 and NOPE + VDIM == HP and HEADS == 8
    out_shape =[jax.ShapeDtypeStruct((2,) + s, F32) for s in half_shapes[:n_big]]
    out_shape += [jax.ShapeDtypeStruct((4, 2) + half_shapes[n_big], F32), jax.ShapeDtypeStruct(gvec.shape, F32)]
    scratch = [pltpu.VMEM((4,) + s, F32) for s in half_shapes] + [pltpu.VMEM((3,) + s, BF16) for s in half_shapes]
    scratch += [pltpu.VMEM((4,) + s, BF16) for s in half_shapes]
    scratch += [pltpu.VMEM((4, 2) + s, F32) for s in half_shapes[:n_lay]]
    scratch += [pltpu.VMEM((8,) + gvec.shape, F32), pltpu.SemaphoreType.DMA((n_sem,)), pltpu.SemaphoreType.DMA((n_sem,))]
    return pl.pallas_call(
        body, name="grad_reduce", out_shape=out_shape,
        in_specs=[vmem] * (n_arr + 1), out_specs=[vmem] * (n_arr + 1), scratch_shapes=scratch,
        compiler_params=pltpu.CompilerParams(vmem_limit_bytes=VMEM_LIMIT),
    )(g_win, g_wuq, g_wkv, g_wout, g_wsp, gvec)


SMALL_ROWS = ((GV_QG, 1, Q_LORA), (GV_KVG, 1, KV_LORA), (GV_SG, 1, GW), (GV_SB, 1, GW),
              (GV_LNG, 1, D_MODEL), (GV_LNB, 1, D_MODEL), (GV_BSP, HEADS, CHUNK))


def _adam_update(g, w, m, v):
    m_new = ADAM_B1 * m + (1.0 - ADAM_B1) * g
    v_new = ADAM_B2 * v + (1.0 - ADAM_B2) * (g * g)
    m_hat = m_new / (1.0 - ADAM_B1 ** ADAM_STEP)
    v_hat = v_new / (1.0 - ADAM_B2 ** ADAM_STEP)
    return -ADAM_LR * (m_hat / (jnp.sqrt(v_hat) + ADAM_EPS) + ADAM_WD * w), m_new, v_new


def _adamw(g_big, w_big, m_big, v_big, gvec, w_small, m_small, v_small):
    nb, ns = len(g_big), len(w_small)

    def body(*refs):
        it = iter(refs)
        take = lambda n: [next(it) for _ in range(n)]
        g_b, w_b, m_b, v_b = take(nb), take(nb), take(nb), take(nb)
        gv = next(it)
        w_s, m_s, v_s = take(ns), take(ns), take(ns)
        g_bo, d_bo, m_bo, v_bo = take(nb), take(nb), take(nb), take(nb)
        g_so, d_so, m_so, v_so = take(ns), take(ns), take(ns), take(ns)
        for n in range(nb):
            gb = g_b[n][...]
            g_bo[n][...] = gb
            d_bo[n][...], m_bo[n][...], v_bo[n][...] = _adam_update(gb, w_b[n][...], m_b[n][...], v_b[n][...])
        for n, (row, nrow, width) in enumerate(SMALL_ROWS):
            gs = gv[row:row + nrow, 0:width]
            g_so[n][...] = gs
            d_so[n][...], m_so[n][...], v_so[n][...] = _adam_update(gs, w_s[n][...], m_s[n][...], v_s[n][...])

    def part(a):
        nd = a.ndim
        if nd > 2 or a.shape[0] % (SUBLANES * ADAM_STEPS) == 0:
            return pl.BlockSpec((a.shape[0] // ADAM_STEPS,) + a.shape[1:], lambda i: (i,) + (0,) * (nd - 1))
        assert a.shape[1] % (LANES * ADAM_STEPS) == 0
        return pl.BlockSpec((a.shape[0], a.shape[1] // ADAM_STEPS), lambda i: (0, i))

    def whole(a):
        nd = a.ndim
        return pl.BlockSpec(a.shape, lambda i: (0,) * nd)

    big = [jax.ShapeDtypeStruct(a.shape, F32) for a in w_big]
    small = [jax.ShapeDtypeStruct(a.shape, F32) for a in w_small]
    return pl.pallas_call(
        body, name="adamw", grid=(ADAM_STEPS,), out_shape=big * 4 + small * 4,
        in_specs=[part(a) for a in g_big + w_big + m_big + v_big] + [whole(gvec)]
        + [whole(a) for a in w_small + m_small + v_small],
        out_specs=[part(a) for a in w_big] * 4 + [whole(a) for a in w_small] * 4,
        compiler_params=pltpu.CompilerParams(dimension_semantics=("arbitrary",), vmem_limit_bytes=VMEM_LIMIT),
    )(*g_big, *w_big, *m_big, *v_big, gvec, *w_small, *m_small, *v_small)


def kernel(x, positions, w_in, q_norm_g, w_uq, kv_norm_g, w_ukv, sgu_norm_g, sgu_norm_b, w_spatial, b_spatial, w_out, ln_g, ln_b, loss_target, m_w_in, m_q_norm_g, m_w_uq, m_kv_norm_g, m_w_ukv, m_sgu_norm_g, m_sgu_norm_b, m_w_spatial, m_b_spatial, m_w_out, m_ln_g, m_ln_b, v_w_in, v_q_norm_g, v_w_uq, v_kv_norm_g, v_w_ukv, v_sgu_norm_g, v_sgu_norm_b, v_w_spatial, v_b_spatial, v_w_out, v_ln_g, v_ln_b):
    seq = x.shape[1]
    x2 = x.reshape(seq, D_MODEL)
    tgt = loss_target.reshape(seq, D_MODEL)
    pos = positions.reshape(seq, 1)

    lane = np.arange(HP)
    half = ROPE // 2
    inv_freq = (1.0 / (ROPE_THETA ** (np.arange(half, dtype=np.float32) / half))).astype(np.float32)
    in_rope = (lane >= NOPE) & (lane < NOPE + ROPE)
    pv_const = np.zeros((PV_ROWS, D_MODEL), np.float32)
    pv_const[PV_INVF, 0:HP] = np.where(in_rope, inv_freq[(lane - NOPE) % half], 0.0)
    pv_const[PV_M1, 0:HP] = np.where((lane >= NOPE) & (lane < NOPE + half), -1.0, 0.0)
    pv_const[PV_M2, 0:HP] = np.where((lane >= NOPE + half) & (lane < NOPE + ROPE), 1.0, 0.0)
    pv_rows = [(PV_QG, q_norm_g), (PV_KVG, kv_norm_g), (PV_SG, sgu_norm_g), (PV_SB, sgu_norm_b), (PV_LNG, ln_g),
               (PV_LNB, ln_b)]
    win, wuq, wkv, a_out, pvec = _weight_gather([w_in, w_uq, w_ukv, w_out], jnp.asarray(pv_const),
                                                [(prow, vec.reshape(1, -1)) for prow, vec in pv_rows])
    wout = a_out.reshape(D_MODEL, D_MODEL)
    tri = jnp.tril(jnp.ones((CHUNK, CHUNK), dtype=bool))
    wt = jnp.where(tri[None], w_spatial, 0.0).astype(BF16)
    wtt = jnp.swapaxes(wt, 1, 2)
    bsp = jnp.repeat(b_spatial.T, VDIM, axis=1)

    cq, ckv, gate, q, k, v, vt, cs = _fwd_pre(x2, pos, win, wuq, wkv, pvec)
    o, lse = _attn_fwd(q, k, vt)
    dh2, do, dgate, g_wout, g_wsp, gvec = _post(x2, tgt, o, gate, wout, pvec, wt, wtt, bsp)
    dq, dk, dv = _attn_bwd(q, k, v, do, o, lse, cs, pvec)
    gx, g_win, g_wuq, g_wkv, gvec = _bwd_pre(x2, dh2, cq, ckv, cs, dq, dk, dv, dgate, win, wuq, wkv, pvec, gvec)

    r_in, r_uq, r_ukv, r_out, r_wsp, r_vec = _grad_reduce(
        g_win, g_wuq, g_wkv, g_wout.reshape(4, 2, D_MODEL // 8, D_MODEL), g_wsp.reshape(4, 2, CHUNK, CHUNK), gvec)

    big = [w_in, w_uq, w_ukv, w_out, w_spatial]
    flip = lambda a: a.T if a.ndim == 2 and a.shape[1] % LANES else a
    flips = lambda arrs: [flip(a) for a in arrs]
    g_big = flips([r_in.reshape(w_in.shape), r_uq.reshape(w_uq.shape), r_ukv.reshape(w_ukv.shape),
                   r_out.reshape(w_out.shape), r_wsp.reshape(w_spatial.shape)])
    small = lambda qg, kvg, sg, sb, lng, lnb, bs: [qg.reshape(1, -1), kvg.reshape(1, -1), sg.reshape(1, -1),
                                                   sb.reshape(1, -1), lng.reshape(1, -1), lnb.reshape(1, -1), bs]
    res = _adamw(g_big, flips(big), flips([m_w_in, m_w_uq, m_w_ukv, m_w_out, m_w_spatial]),
                 flips([v_w_in, v_w_uq, v_w_ukv, v_w_out, v_w_spatial]), r_vec,
                 small(q_norm_g, kv_norm_g, sgu_norm_g, sgu_norm_b, ln_g, ln_b, b_spatial),
                 small(m_q_norm_g, m_kv_norm_g, m_sgu_norm_g, m_sgu_norm_b, m_ln_g, m_ln_b, m_b_spatial),
                 small(v_q_norm_g, v_kv_norm_g, v_sgu_norm_g, v_sgu_norm_b, v_ln_g, v_ln_b, v_b_spatial))
    res = [r.T if n < 4 * len(big) and r.shape != big[n % len(big)].shape else r for n, r in enumerate(res)]

    def ordered(big, sm):
        vec = lambda n: sm[n].reshape(-1)
        return [big[0], vec(0), big[1], vec(1), big[2], vec(2), vec(3), big[4], sm[6], big[3], vec(4), vec(5)]

    loss = r_vec[GV_LOSS, 0]
    return (loss, gx.reshape(1, seq, D_MODEL), *ordered(res[0:5], res[20:27]), *ordered(res[5:10], res[27:34]),
            *ordered(res[10:15], res[34:41]), *ordered(res[15:20], res[41:48]))
```

```python
import math

import jax
import jax.numpy as jnp
import numpy as np
from jax import lax
from jax.experimental import pallas as pl
from jax.experimental.pallas import tpu as pltpu

F32 = jnp.float32
BF16 = jnp.bfloat16

D_MODEL = 1024
Q_LORA = 256
KV_LORA = 128
HEADS = 8
NOPE = 64
ROPE = 32
VDIM = 64
MLA_W = HEADS * VDIM
GW = 512
CHUNK = 128
HP = 128
PAIRS = HEADS // 2
D_IN = 2464
D_INR = 2560
C_CKV = Q_LORA
C_KR = Q_LORA + KV_LORA
C_GATE = C_KR + HP
ROPE_THETA = 10000.0
DN_ALPHA = 2.0 ** 0.25
EPS = 1e-5
SCALE = 1.0 / math.sqrt(NOPE + ROPE)
SCALE_LOG2E = SCALE * 1.4426950408889634
INV_SQRT2 = 0.7071067811865476
INV_SQRT_2PI = 0.3989422804014327

ADAM_LR = 0.001
ADAM_B1 = 0.9
ADAM_B2 = 0.999
ADAM_EPS = 1e-08
ADAM_WD = 0.01
ADAM_STEP = 10

PV_QG, PV_KVG, PV_SG, PV_SB, PV_INVF, PV_M1, PV_M2, PV_LNG, PV_LNB = range(9)
PV_ROWS = 16
GV_QG, GV_KVG, GV_SG, GV_SB, GV_LNG, GV_LNB, GV_LOSS = range(7)
GV_BSP = 8
GV_ROWS = 16

MESH = pl.DeviceIdType.MESH

FWD_TILE = 1024
POST_TILE = 512
BWD_TILE = 512
ATT_BLK = 512
ADAM_STEPS = 4
SUBLANES, LANES = 8, 128
VMEM_LIMIT = 60 * 1024 * 1024


def _dot(a, b):
    return jnp.dot(a, b, preferred_element_type=F32)


def _dot_nt(a, b):
    return lax.dot_general(a, b, (((1,), (1,)), ((), ())), preferred_element_type=F32)


def _dot_tn(a, b):
    return lax.dot_general(a, b, (((0,), (0,)), ((), ())), preferred_element_type=F32)


def _sigmoid(z):
    return pl.reciprocal(1.0 + jnp.exp(-z), approx=True)


def _gelu_and_grad(x):
    cdf = 0.5 * (1.0 + lax.erf(x * INV_SQRT2))
    return x * cdf, cdf + x * (INV_SQRT_2PI * jnp.exp(-0.5 * x * x))


def _rms_stats(x):
    r = lax.rsqrt(jnp.mean(x * x, axis=-1, keepdims=True) + EPS)
    return x * r, r


def _rms_bwd(dy, g, xh, r):
    dyg = dy * g
    return r * (dyg - xh * jnp.mean(dyg * xh, axis=-1, keepdims=True))


def _ln_stats(x):
    mu = jnp.mean(x, axis=-1, keepdims=True)
    xc = x - mu
    r = lax.rsqrt(jnp.mean(xc * xc, axis=-1, keepdims=True) + EPS)
    return xc * r, r


def _ln_bwd(dy, g, xh, r):
    dxh = dy * g
    return r * (dxh - jnp.mean(dxh, axis=-1, keepdims=True) - xh * jnp.mean(dxh * xh, axis=-1, keepdims=True))


def _rope_fwd(t, c, s1, s2):
    return t * c + pltpu.roll(t, HP - 16, 1) * s1 + pltpu.roll(t, 16, 1) * s2


def _rope_bwd(d, c, s1, s2):
    return d * c + pltpu.roll(d * s1, 16, 1) + pltpu.roll(d * s2, HP - 16, 1)


def _lane_lt64(shape):
    return lax.broadcasted_iota(jnp.int32, shape, len(shape) - 1) < 64


def _spatial_mix(w_ref, src, dst_ref, rows):
    for c in range(rows // CHUNK):
        for p in range(PAIRS):
            blk = src[c * CHUNK:(c + 1) * CHUNK, p * HP:(p + 1) * HP]
            a = _dot(w_ref[2 * p], blk)
            b = _dot(w_ref[2 * p + 1], blk)
            dst_ref[c * CHUNK:(c + 1) * CHUNK, p * HP:(p + 1) * HP] = jnp.where(_lane_lt64(a.shape), a, b)


def _gmlp_fwd(u_pre, v_pre, zb, sg, sb, wt_ref, bsp_ref, sv_ref, rows):
    u, du = _gelu_and_grad(u_pre)
    gv, dgv = _gelu_and_grad(v_pre)
    xh, r = _ln_stats(gv)
    vln = (xh * sg + sb).astype(BF16)
    _spatial_mix(wt_ref, vln, sv_ref, rows)
    bias = bsp_ref[...]
    svb = sv_ref[...] + jnp.concatenate([bias] * (rows // CHUNK), axis=0)
    sig = _sigmoid(zb)
    return u, du, dgv, xh, r, vln, svb, sig


N_GATHER_SEMS = 8


def _weight_gather(shards, pv_const, pv_rows):
    n_arr = len(shards)
    n_in = n_arr + 1 + len(pv_rows)
    w_in, w_uq, w_ukv, _ = shards
    cw = w_in.shape[1]
    gate0 = C_KR + ROPE
    rc = 128

    def body(*refs):
        ins = refs[0:n_arr]
        win_o, wuq_o, wkv_o, out_o, pv_o = refs[n_in:n_in + 5]
        a_in, a_uq, a_ukv, send_sems, recv_sems = refs[n_in + 5:]
        outs = [a_in, a_uq, a_ukv, out_o]
        pv_o[...] = refs[n_arr][...]
        for (prow, vec), ref in zip(pv_rows, refs[n_arr + 1:n_in]):
            pv_o[prow:prow + 1, 0:vec.shape[1]] = ref[...].astype(F32)
        x, y, c = lax.axis_index("x"), lax.axis_index("y"), lax.axis_index("c")
        j = 2 * x + y
        sib = (x, y, 1 - c)
        near = [(1 - x, y, c), (x, 1 - y, c)]
        b_near = [2 * (1 - x) + y, 2 * x + 1 - y]
        b_far = 2 * (1 - x) + 1 - y
        for n in range(n_arr):
            outs[n][j] = ins[n][...].astype(BF16)

        def half(n, blk, core):
            r = shards[n].shape[0] // 2
            return outs[n].at[blk, pl.ds(pl.multiple_of(core * r, 16), r), :]

        def quarter(n, blk, core, part):
            q = shards[n].shape[0] // 4
            return outs[n].at[blk, pl.ds(pl.multiple_of((2 * core + part) * q, 16), q), :]

        def copy(k, ref, to):
            return pltpu.make_async_remote_copy(
                src_ref=ref, dst_ref=ref, send_sem=send_sems.at[k], recv_sem=recv_sems.at[k],
                device_id=to, device_id_type=MESH)

        sent = [copy(N_GATHER_SEMS * n + kk, half(n, j, c), near[kk]) for n in range(n_arr) for kk in range(2)]
        for cp in sent:
            cp.start()

        def send(k, ref, to):
            sent.append(copy(k, ref, to))
            sent[-1].start()

        for n in range(n_arr):
            k0 = N_GATHER_SEMS * n
            for kk in range(2):
                copy(k0 + kk, half(n, b_near[kk], c), near[kk]).wait_recv()
                send(k0 + 2 + kk, quarter(n, b_near[kk], c, kk), near[1 - kk])
                send(k0 + 4 + kk, half(n, b_near[kk], c), sib)
        for n in range(n_arr):
            k0 = N_GATHER_SEMS * n
            for kk in range(2):
                copy(k0 + 2 + kk, quarter(n, b_far, c, kk), near[1 - kk]).wait_recv()
                send(k0 + 6 + kk, quarter(n, b_far, c, kk), sib)
        for n in range(n_arr):
            k0 = N_GATHER_SEMS * n
            for kk in range(2):
                copy(k0 + 4 + kk, half(n, b_near[kk], 1 - c), sib).wait_recv()
                copy(k0 + 6 + kk, quarter(n, b_far, 1 - c, kk), sib).wait_recv()

        def lay_in(r, carry):
            rows = pl.ds(pl.multiple_of(r * rc, rc), rc)
            win_o[rows, 0:C_KR] = a_in[0, rows, 0:C_KR]
            win_o[rows, C_KR:C_KR + NOPE] = jnp.zeros((rc, NOPE), BF16)
            win_o[rows, C_KR + NOPE:C_KR + NOPE + ROPE] = a_in[0, rows, C_KR:gate0]
            win_o[rows, C_KR + NOPE + ROPE:C_GATE] = jnp.zeros((rc, HP - NOPE - ROPE), BF16)
            win_o[rows, C_GATE:C_GATE + cw - gate0] = a_in[0, rows, gate0:cw]
            for blk in range(1, 4):
                win_o[rows, D_INR - (4 - blk) * cw:D_INR - (3 - blk) * cw] = a_in[blk, rows, :]
            return carry

        lax.fori_loop(0, D_MODEL // rc, lay_in, 0)
        wuq_o[...] = jnp.zeros(wuq_o.shape, BF16)
        wkv_o[...] = jnp.zeros(wkv_o.shape, BF16)
        qw = NOPE + ROPE
        for h in range(HEADS):
            blk, e = h // 2, h % 2
            wuq_o[:, h * HP:h * HP + qw] = a_uq[blk, :, e * qw:(e + 1) * qw]
            wkv_o[:, h * HP:h * HP + NOPE] = a_ukv[blk, :, e * HP:e * HP + NOPE]
            wkv_o[:, HEADS * HP + h * VDIM:HEADS * HP + (h + 1) * VDIM] = a_ukv[blk, :, e * HP + NOPE:(e + 1) * HP]
        for cp in sent:
            cp.wait_send()

    assert cw > gate0 and C_GATE + 4 * cw - gate0 == D_INR and NOPE + VDIM == HP and HEADS == 8
    vmem = pl.BlockSpec(memory_space=pltpu.VMEM)
    gathered = lambda a: (4,) + a.shape
    return pl.pallas_call(
        body, name="weight_gather",
        out_shape=[jax.ShapeDtypeStruct((D_MODEL, D_INR), BF16), jax.ShapeDtypeStruct((Q_LORA, HEADS * HP), BF16),
                   jax.ShapeDtypeStruct((KV_LORA, HEADS * HP + MLA_W), BF16),
                   jax.ShapeDtypeStruct(gathered(shards[3]), BF16), jax.ShapeDtypeStruct(pv_const.shape, F32)],
        in_specs=[vmem] * n_in, out_specs=[vmem] * 5,
        scratch_shapes=[pltpu.VMEM(gathered(a), BF16) for a in shards[0:3]]
        + [pltpu.SemaphoreType.DMA((N_GATHER_SEMS * n_arr,)), pltpu.SemaphoreType.DMA((N_GATHER_SEMS * n_arr,))],
        compiler_params=pltpu.CompilerParams(vmem_limit_bytes=VMEM_LIMIT),
    )(*shards, pv_const, *[vec for _, vec in pv_rows])


def _fwd_pre(x, pos, win, wuq, wkv, pvec):
    seq = x.shape[0]
    t = FWD_TILE

    def body(x_ref, pos_ref, win_ref, wuq_ref, wkv_ref, pv_ref,
             cq_o, ckv_o, gate_o, q_o, k_o, v_o, vt_o, cs_o):
        xb = x_ref[...].astype(BF16)
        proj = _dot(xb, win_ref[:, 0:C_GATE])
        cq = proj[:, 0:C_CKV]
        ckv = proj[:, C_CKV:C_KR]
        kr = proj[:, C_KR:C_GATE]
        cq_o[...] = cq
        ckv_o[...] = ckv

        invf = pv_ref[PV_INVF:PV_INVF + 1, 0:HP]
        ang = jnp.concatenate([jnp.broadcast_to(pos_ref[g:g + 1, :].astype(F32), (HP, HP)).T * invf
                               for g in range(t // HP)], axis=0)
        cos = jnp.cos(ang)
        sin = jnp.sin(ang)
        cs_o[:, 0:HP] = cos
        cs_o[:, HP:2 * HP] = sin
        s1 = sin * pv_ref[PV_M1:PV_M1 + 1, 0:HP]
        s2 = sin * pv_ref[PV_M2:PV_M2 + 1, 0:HP]

        cqh, _ = _rms_stats(cq)
        q_all = _dot((cqh * pv_ref[PV_QG:PV_QG + 1, 0:Q_LORA]).astype(BF16), wuq_ref[...])
        ckvh, _ = _rms_stats(ckv)
        kv_all = _dot((ckvh * pv_ref[PV_KVG:PV_KVG + 1, 0:KV_LORA]).astype(BF16), wkv_ref[...])
        krr = _rope_fwd(kr, cos, s1, s2)
        for h in range(HEADS):
            sl = slice(h * HP, (h + 1) * HP)
            q_o[:, sl] = (_rope_fwd(q_all[:, sl], cos, s1, s2) * SCALE_LOG2E).astype(BF16)
            k_o[:, sl] = (kv_all[:, sl] + krr).astype(BF16)
        val = kv_all[:, HEADS * HP:].astype(BF16)
        v_o[...] = val
        for blk in range(t // ATT_BLK):
            vt_o[blk] = val[blk * ATT_BLK:(blk + 1) * ATT_BLK, :].T
        gate_o[...] = _dot(xb, win_ref[:, C_GATE:D_INR]).astype(BF16)

    tile = lambda w: pl.BlockSpec((t, w), lambda i: (i, 0))
    full = lambda a: pl.BlockSpec(a.shape, lambda i: (0,) * a.ndim)
    outs = [(Q_LORA, F32), (KV_LORA, F32), (2048, BF16), (HEADS * HP, BF16), (HEADS * HP, BF16), (MLA_W, BF16)]
    assert t % ATT_BLK == 0
    out_specs = [tile(w) for w, _ in outs]
    out_specs += [pl.BlockSpec((t // ATT_BLK, MLA_W, ATT_BLK), lambda i: (i, 0, 0)), tile(2 * HP)]
    out_shape = [jax.ShapeDtypeStruct((seq, w), d) for w, d in outs]
    out_shape += [jax.ShapeDtypeStruct((seq // ATT_BLK, MLA_W, ATT_BLK), BF16), jax.ShapeDtypeStruct((seq, 2 * HP), F32)]
    return pl.pallas_call(
        body, name="fwd_pre", grid=(seq // t,),
        in_specs=[tile(D_MODEL), pl.BlockSpec((t // HP, HP), lambda i: (i, 0)), full(win), full(wuq), full(wkv), full(pvec)],
        out_specs=out_specs, out_shape=out_shape,
        compiler_params=pltpu.CompilerParams(dimension_semantics=("arbitrary",), vmem_limit_bytes=VMEM_LIMIT),
    )(x, pos, win, wuq, wkv, pvec)


def _attn_fwd(q, k, vt):
    seq = q.shape[0]
    b = ATT_BLK
    nq = seq // b
    assert nq % 2 == 0
    assert nq % 4 == 0
    n_wide = sum(i // 4 for i in range(nq))

    def body(q_ref, k_ref, vt_ref, o_o, lse_o, m_ref, l_ref, acc_ref, s_even, s_odd):
        m_ref[...] = jnp.full(m_ref.shape, -jnp.inf, F32)
        l_ref[...] = jnp.zeros(l_ref.shape, F32)
        acc_ref[...] = jnp.zeros(acc_ref.shape, F32)

        def scores(i, j, s_ref, nkb):
            qrows = pl.ds(pl.multiple_of(i * b, b), b)
            krows = pl.ds(pl.multiple_of(j * b, b), nkb * b)
            for a in range(2):
                s_ref[a, 0:nkb * b, :] = _dot_nt(k_ref[krows, a * HP:(a + 1) * HP], q_ref[qrows, a * HP:(a + 1) * HP])

        def consume(i, j, s_ref, nkb, masked):
            for a in range(2):
                st = s_ref[a, 0:nkb * b, :]
                if masked:
                    ki = lax.broadcasted_iota(jnp.int32, st.shape, 0)
                    qi = lax.broadcasted_iota(jnp.int32, st.shape, 1)
                    st = jnp.where(ki <= qi, st, -jnp.inf)
                m_prev = m_ref[i, a:a + 1, :]
                m_new = jnp.maximum(m_prev, jnp.max(st, axis=0, keepdims=True))
                alpha = jnp.exp2(m_prev - m_new)
                pt = jnp.exp2(st - m_new)
                ptb = pt.astype(BF16)
                l_ref[i, a:a + 1, :] = alpha * l_ref[i, a:a + 1, :] + jnp.sum(pt, axis=0, keepdims=True)
                pv = _dot(vt_ref[j], ptb[0:b, :])
                for kb in range(1, nkb):
                    pv = pv + _dot(vt_ref[j + kb], ptb[kb * b:(kb + 1) * b, :])
                acc_ref[i, a] = alpha * acc_ref[i, a] + pv
                m_ref[i, a:a + 1, :] = m_new

        def run(count, first, following, nkb, masked):
            if count == 0:
                return
            scores(*first, s_even, nkb)

            def two(u, ij):
                nxt = following(*ij)
                scores(*nxt, s_odd, nkb)
                consume(*ij, s_even, nkb, masked)
                nxt2 = following(*nxt)
                scores(*nxt2, s_even, nkb)
                consume(*nxt, s_odd, nkb, masked)
                return nxt2

            last = lax.fori_loop(0, count // 2, two, tuple(jnp.int32(c) for c in first))
            if count % 2:
                consume(*last, s_even, nkb, masked)

        def clamp(i):
            return jnp.minimum(i, nq - 1)

        def next_wide(i, j):
            wrap = j + 8 > i
            return clamp(jnp.where(wrap, i + 1, i)), jnp.where(wrap, 0, j + 4)

        def next_pair(i, j):
            low = lax.rem(i, 4) == 2
            return clamp(jnp.where(low, i + 1, i + 3)), jnp.minimum(jnp.where(low, j, j + 4), nq - 4)

        run(n_wide, (4, 0), next_wide, 4, False)
        run(nq // 2, (2, 0), next_pair, 2, False)
        run(nq // 2, (1, 0), lambda i, j: (clamp(i + 2), jnp.minimum(j + 2, nq - 2)), 1, False)
        run(nq, (0, 0), lambda i, j: (clamp(i + 1), clamp(j + 1)), 1, True)
        top = lax.broadcasted_iota(jnp.int32, (HP, b), 0) < 64

        def finish(i, carry):
            rows = pl.ds(pl.multiple_of(i * b, b), b)
            o_o[rows, :] = jnp.where(top, acc_ref[i, 0] / l_ref[i, 0:1, :], acc_ref[i, 1] / l_ref[i, 1:2, :]).T
            lse_o[i] = m_ref[i, 0:2, :] + jnp.log2(l_ref[i, 0:2, :])
            return carry

        lax.fori_loop(0, nq, finish, 0)

    return pl.pallas_call(
        body, name="attn_fwd", grid=(PAIRS,),
        in_specs=[pl.BlockSpec((seq, 2 * HP), lambda p: (0, p)),
                  pl.BlockSpec((seq, 2 * HP), lambda p: (0, p)),
                  pl.BlockSpec((nq, HP, b), lambda p: (0, p, 0))],
        out_specs=[pl.BlockSpec((seq, HP), lambda p: (0, p)),
                   pl.BlockSpec((None, nq, 2, b), lambda p: (p, 0, 0, 0))],
        out_shape=[jax.ShapeDtypeStruct((seq, MLA_W), F32),
                   jax.ShapeDtypeStruct((PAIRS, nq, 2, b), F32)],
        scratch_shapes=[pltpu.VMEM((nq, 8, b), F32), pltpu.VMEM((nq, 8, b), F32), pltpu.VMEM((nq, 2, HP, b), F32),
                        pltpu.VMEM((2, 4 * b, b), F32), pltpu.VMEM((2, 4 * b, b), F32)],
        compiler_params=pltpu.CompilerParams(dimension_semantics=("arbitrary",), vmem_limit_bytes=VMEM_LIMIT),
    )(q, k, vt)


def _post(x, tgt, o, gate, wout, pvec, wt, wtt, bsp):
    seq = x.shape[0]
    t = POST_TILE
    nt = seq // t

    def body(x_ref, tgt_ref, o_ref, gate_ref, wout_ref, pv_ref, wt_ref, wtt_ref, bsp_ref,
             dh2_o, do_o, dgate_o, gwout_o, gwsp_o, vec_o, sv_ref, dvln_ref, bacc_ref):
        i = pl.program_id(0)

        @pl.when(i == 0)
        def _():
            gwout_o[...] = jnp.zeros_like(gwout_o)
            gwsp_o[...] = jnp.zeros_like(gwsp_o)
            vec_o[...] = jnp.zeros_like(vec_o)
            bacc_ref[...] = jnp.zeros_like(bacc_ref)

        za = gate_ref[:, 0:512].astype(F32)
        u_pre = gate_ref[:, 512:1024].astype(F32)
        v_pre = gate_ref[:, 1024:1536].astype(F32)
        zb = gate_ref[:, 1536:2048].astype(F32)
        sg = pv_ref[PV_SG:PV_SG + 1, 0:GW]
        sb = pv_ref[PV_SB:PV_SB + 1, 0:GW]
        lng = pv_ref[PV_LNG:PV_LNG + 1, :]
        lnb = pv_ref[PV_LNB:PV_LNB + 1, :]
        o = o_ref[...]

        sig_a = _sigmoid(za)
        silu_a = za * sig_a
        u, du, dgv, xh, r, vln, svb, sig_b = _gmlp_fwd(u_pre, v_pre, zb, sg, sb, wt_ref, bsp_ref, sv_ref, t)
        silu_b = zb * sig_b
        sgu = u * svb
        merged = jnp.concatenate([o * silu_a, sgu * silu_b], axis=1).astype(BF16)
        h2 = DN_ALPHA * x_ref[...] + _dot(merged, wout_ref[...])
        xh2, r2 = _ln_stats(h2)
        err = xh2 * lng + lnb - tgt_ref[...]
        d_out = err * (1.0 / D_MODEL)
        vec_o[GV_LNG:GV_LNG + 1, :] += jnp.sum(d_out * xh2, axis=0, keepdims=True)
        vec_o[GV_LNB:GV_LNB + 1, :] += jnp.sum(d_out, axis=0, keepdims=True)
        vec_o[GV_LOSS:GV_LOSS + 1, :] += jnp.sum(err * err, axis=0, keepdims=True) * (0.5 / D_MODEL)

        d_h2 = _ln_bwd(d_out, lng, xh2, r2)
        dh2_o[...] = d_h2
        dh2b = d_h2.astype(BF16)
        gwout_o[...] += _dot_tn(merged, dh2b)
        d_m = _dot_nt(dh2b, wout_ref[...])
        d_oa = d_m[:, 0:512]
        d_ob = d_m[:, 512:1024]
        do_o[...] = (d_oa * silu_a).astype(BF16)
        dgate_o[:, 0:512] = (d_oa * o * (sig_a * (1.0 + za * (1.0 - sig_a)))).astype(BF16)
        dgate_o[:, 1536:2048] = (d_ob * sgu * (sig_b * (1.0 + zb * (1.0 - sig_b)))).astype(BF16)
        d_sgu = d_ob * silu_b
        dgate_o[:, 512:1024] = (d_sgu * svb * du).astype(BF16)
        d_sv = d_sgu * u
        acc = bacc_ref[...]
        for c in range(t // CHUNK):
            acc = acc + d_sv[c * CHUNK:(c + 1) * CHUNK, :]
        bacc_ref[...] = acc
        d_svb = d_sv.astype(BF16)
        for c in range(t // CHUNK):
            for p in range(PAIRS):
                blk = d_svb[c * CHUNK:(c + 1) * CHUNK, p * HP:(p + 1) * HP]
                vblk = vln[c * CHUNK:(c + 1) * CHUNK, p * HP:(p + 1) * HP]
                first = _lane_lt64(blk.shape)
                gwsp_o[2 * p] += _dot_nt(jnp.where(first, blk, jnp.zeros_like(blk)), vblk)
                gwsp_o[2 * p + 1] += _dot_nt(jnp.where(first, jnp.zeros_like(blk), blk), vblk)
        _spatial_mix(wtt_ref, d_svb, dvln_ref, t)
        d_vln = dvln_ref[...]
        vec_o[GV_SG:GV_SG + 1, 0:GW] += jnp.sum(d_vln * xh, axis=0, keepdims=True)
        vec_o[GV_SB:GV_SB + 1, 0:GW] += jnp.sum(d_vln, axis=0, keepdims=True)
        dgate_o[:, 1024:1536] = (_ln_bwd(d_vln, sg, xh, r) * dgv).astype(BF16)


        @pl.when(i == nt - 1)
        def _():
            tri = (lax.broadcasted_iota(jnp.int32, (CHUNK, CHUNK), 1)
                   <= lax.broadcasted_iota(jnp.int32, (CHUNK, CHUNK), 0))
            for h in range(HEADS):
                gwsp_o[h] = jnp.where(tri, gwsp_o[h], 0.0)
            lane = lax.broadcasted_iota(jnp.int32, (CHUNK, HP), 1)
            res = jnp.zeros((CHUNK, HP), F32)
            for h in range(HEADS):
                p, a = divmod(h, 2)
                blk = bacc_ref[:, p * HP:(p + 1) * HP]
                part = jnp.where(_lane_lt64(blk.shape) == (a == 0), blk, 0.0)
                res = jnp.where(lane == h, jnp.sum(part, axis=-1, keepdims=True), res)
            vec_o[GV_BSP:GV_BSP + HEADS, 0:HP] = res.T[0:HEADS, :]
            lane1 = lax.broadcasted_iota(jnp.int32, (1, D_MODEL), 1)
            total = jnp.sum(vec_o[GV_LOSS:GV_LOSS + 1, :], axis=-1, keepdims=True)
            vec_o[GV_LOSS:GV_LOSS + 1, :] = jnp.where(lane1 == 0, total, 0.0)

    tile = lambda w: pl.BlockSpec((t, w), lambda i: (i, 0))
    full = lambda a: pl.BlockSpec(a.shape, lambda i: (0,) * a.ndim)
    const = lambda s: pl.BlockSpec(s, lambda i: (0,) * len(s))
    return pl.pallas_call(
        body, name="post", grid=(nt,),
        in_specs=[tile(D_MODEL), tile(D_MODEL), tile(MLA_W), tile(2048), full(wout), full(pvec),
                  full(wt), full(wtt), full(bsp)],
        out_specs=[tile(D_MODEL), tile(MLA_W), tile(2048), const((D_MODEL, D_MODEL)),
                   const((HEADS, CHUNK, CHUNK)), const((GV_ROWS, D_MODEL))],
        out_shape=[jax.ShapeDtypeStruct((seq, D_MODEL), F32), jax.ShapeDtypeStruct((seq, MLA_W), BF16),
                   jax.ShapeDtypeStruct((seq, 2048), BF16), jax.ShapeDtypeStruct((D_MODEL, D_MODEL), F32),
                   jax.ShapeDtypeStruct((HEADS, CHUNK, CHUNK), F32), jax.ShapeDtypeStruct((GV_ROWS, D_MODEL), F32)],
        scratch_shapes=[pltpu.VMEM((t, GW), F32), pltpu.VMEM((t, GW), F32), pltpu.VMEM((CHUNK, GW), F32)],
        compiler_params=pltpu.CompilerParams(dimension_semantics=("arbitrary",), vmem_limit_bytes=VMEM_LIMIT),
    )(x, tgt, o, gate, wout, pvec, wt, wtt, bsp)


def _attn_bwd(q, k, v, do, o, lse, cs, pvec):
    seq = q.shape[0]
    b = ATT_BLK
    nq = seq // b

    def body(q_ref, k_ref, v_ref, do_ref, o_ref, lse_ref, cs_ref, pv_ref, dq_o, dk_o, dv_o, dk_acc, dv_acc):
        i = pl.program_id(1)

        @pl.when(i == 0)
        def _():
            dk_acc[...] = jnp.zeros_like(dk_acc)
            dv_acc[...] = jnp.zeros_like(dv_acc)

        first = _lane_lt64((b, HP))
        do = do_ref[...]
        zero = jnp.zeros_like(do)
        dos = [jnp.where(first, do, zero), jnp.where(first, zero, do)]
        prod_t = (do.astype(F32) * o_ref[...]).T
        deltas = [jnp.sum(prod_t[0:64, :], axis=0, keepdims=True),
                  jnp.sum(prod_t[64:128, :], axis=0, keepdims=True)]
        lses = [lse_ref[0:1, :], lse_ref[1:2, :]]
        qs = [q_ref[:, a * HP:(a + 1) * HP] for a in range(2)]

        def step(j, dqs, masked, nk=b):
            rows = pl.ds(pl.multiple_of(j * b, b), nk)
            vb = v_ref[rows, :]
            new_dq = []
            dvs = []
            for a in range(2):
                kb = k_ref[rows, a * HP:(a + 1) * HP]
                pt = jnp.exp2(_dot_nt(kb, qs[a]) - lses[a])
                if masked:
                    ki = lax.broadcasted_iota(jnp.int32, pt.shape, 0)
                    qi = lax.broadcasted_iota(jnp.int32, pt.shape, 1) + (nk - b)
                    pt = jnp.where(ki <= qi, pt, 0.0)
                dvs.append(_dot(pt.astype(BF16), do))
                dpt = _dot_nt(vb, dos[a])
                dst = (pt * (dpt - deltas[a])).astype(BF16)
                dk_acc[rows, a * HP:(a + 1) * HP] += _dot(dst, qs[a])
                new_dq.append(dqs[a] + _dot_tn(dst, kb))
            dv_acc[rows, :] += jnp.where(_lane_lt64((nk, HP)), dvs[0], dvs[1])
            return tuple(new_dq)

        init = (jnp.zeros((b, HP), F32), jnp.zeros((b, HP), F32))
        dqs = lax.fori_loop(0, i // 4, lambda jj, cr: step(4 * jj, cr, False, 4 * b), init)
        last = [lambda cr, w=w: step(4 * (i // 4), cr, True, w * b) for w in (1, 2, 3, 4)]
        dqs = lax.switch(i % 4, last, dqs)
        cos = cs_ref[:, 0:HP]
        sin = cs_ref[:, HP:2 * HP]
        s1 = sin * pv_ref[PV_M1:PV_M1 + 1, 0:HP]
        s2 = sin * pv_ref[PV_M2:PV_M2 + 1, 0:HP]
        for a in range(2):
            dq_o[:, a * HP:(a + 1) * HP] = _rope_bwd(dqs[a] * SCALE, cos, s1, s2).astype(BF16)

        @pl.when(i == nq - 1)
        def _():
            dk_o[...] = (dk_acc[...] * (SCALE / SCALE_LOG2E)).astype(BF16)
            dv_o[...] = dv_acc[...].astype(BF16)

    return pl.pallas_call(
        body, name="attn_bwd", grid=(PAIRS, nq),
        in_specs=[pl.BlockSpec((b, 2 * HP), lambda p, i: (i, p)),
                  pl.BlockSpec((seq, 2 * HP), lambda p, i: (0, p)),
                  pl.BlockSpec((seq, HP), lambda p, i: (0, p)),
                  pl.BlockSpec((b, HP), lambda p, i: (i, p)),
                  pl.BlockSpec((b, HP), lambda p, i: (i, p)),
                  pl.BlockSpec((None, None, 2, b), lambda p, i: (p, i, 0, 0)),
                  pl.BlockSpec((b, 2 * HP), lambda p, i: (i, 0)),
                  pl.BlockSpec(pvec.shape, lambda p, i: (0, 0))],
        out_specs=[pl.BlockSpec((b, 2 * HP), lambda p, i: (i, p)),
                   pl.BlockSpec((seq, 2 * HP), lambda p, i: (0, p)),
                   pl.BlockSpec((seq, HP), lambda p, i: (0, p))],
        out_shape=[jax.ShapeDtypeStruct((seq, HEADS * HP), BF16),
                   jax.ShapeDtypeStruct((seq, HEADS * HP), BF16),
                   jax.ShapeDtypeStruct((seq, MLA_W), BF16)],
        scratch_shapes=[pltpu.VMEM((seq, 2 * HP), F32), pltpu.VMEM((seq, HP), F32)],
        compiler_params=pltpu.CompilerParams(dimension_semantics=("arbitrary", "arbitrary"),
                                             vmem_limit_bytes=VMEM_LIMIT),
    )(q, k, v, do, o, lse, cs, pvec)


def _bwd_pre(x, dh2, cq, ckv, cs, dq, dk, dv, dgate, win, wuq, wkv, pvec, gvec):
    seq = x.shape[0]
    t = BWD_TILE

    def body(x_ref, dh2_ref, cq_ref, ckv_ref, cs_ref, dq_ref, dk_ref, dv_ref, dgate_ref,
             win_ref, wuq_ref, wkv_ref, pv_ref, gv_ref, gx_o, gwin_o, gwuq_o, gwkv_o, vec_o):
        i = pl.program_id(0)

        @pl.when(i == 0)
        def _():
            gwin_o[...] = jnp.zeros_like(gwin_o)
            gwuq_o[...] = jnp.zeros_like(gwuq_o)
            gwkv_o[...] = jnp.zeros_like(gwkv_o)
            vec_o[...] = gv_ref[...]

        xb = x_ref[...].astype(BF16)
        dgate = dgate_ref[...]
        gwin_o[:, C_GATE:D_INR] += _dot_tn(xb, dgate)
        gx_gate = _dot_nt(dgate, win_ref[:, C_GATE:D_INR])

        qg = pv_ref[PV_QG:PV_QG + 1, 0:Q_LORA]
        kvg = pv_ref[PV_KVG:PV_KVG + 1, 0:KV_LORA]
        dq = dq_ref[...]
        cqh, rq = _rms_stats(cq_ref[...])
        d_cqn = _dot_nt(dq, wuq_ref[...])
        gwuq_o[...] += _dot_tn((cqh * qg).astype(BF16), dq)
        vec_o[GV_QG:GV_QG + 1, 0:Q_LORA] += jnp.sum(d_cqn * cqh, axis=0, keepdims=True)
        d_cq = _rms_bwd(d_cqn, qg, cqh, rq)

        dk = dk_ref[...]
        dkv = jnp.concatenate([dk, dv_ref[...]], axis=1)
        ckvh, rkv = _rms_stats(ckv_ref[...])
        d_ckvn = _dot_nt(dkv, wkv_ref[...])
        gwkv_o[...] += _dot_tn((ckvh * kvg).astype(BF16), dkv)
        vec_o[GV_KVG:GV_KVG + 1, 0:KV_LORA] += jnp.sum(d_ckvn * ckvh, axis=0, keepdims=True)
        d_ckv = _rms_bwd(d_ckvn, kvg, ckvh, rkv)

        dks = dk[:, 0:HP].astype(F32)
        for h in range(1, HEADS):
            dks = dks + dk[:, h * HP:(h + 1) * HP].astype(F32)
        cos = cs_ref[:, 0:HP]
        sin = cs_ref[:, HP:2 * HP]
        d_kr = _rope_bwd(dks, cos, sin * pv_ref[PV_M1:PV_M1 + 1, 0:HP], sin * pv_ref[PV_M2:PV_M2 + 1, 0:HP])

        d_lat = jnp.concatenate([d_cq.astype(BF16), d_ckv.astype(BF16), d_kr.astype(BF16)], axis=1)
        gwin_o[:, 0:C_GATE] += _dot_tn(xb, d_lat)
        gx_o[...] = DN_ALPHA * dh2_ref[...] + gx_gate + _dot_nt(d_lat, win_ref[:, 0:C_GATE])

    tile = lambda w: pl.BlockSpec((t, w), lambda i: (i, 0))
    full = lambda a: pl.BlockSpec(a.shape, lambda i: (0,) * a.ndim)
    const = lambda s: pl.BlockSpec(s, lambda i: (0,) * len(s))
    return pl.pallas_call(
        body, name="bwd_pre", grid=(seq // t,),
        in_specs=[tile(D_MODEL), tile(D_MODEL), tile(Q_LORA), tile(KV_LORA), tile(2 * HP), tile(HEADS * HP),
                  tile(HEADS * HP), tile(MLA_W), tile(2048), full(win), full(wuq), full(wkv), full(pvec), full(gvec)],
        out_specs=[tile(D_MODEL), const((D_MODEL, D_INR)), const((Q_LORA, HEADS * HP)),
                   const((KV_LORA, HEADS * HP + MLA_W)), const((GV_ROWS, D_MODEL))],
        out_shape=[jax.ShapeDtypeStruct((seq, D_MODEL), F32), jax.ShapeDtypeStruct((D_MODEL, D_INR), F32),
                   jax.ShapeDtypeStruct((Q_LORA, HEADS * HP), F32),
                   jax.ShapeDtypeStruct((KV_LORA, HEADS * HP + MLA_W), F32),
                   jax.ShapeDtypeStruct((GV_ROWS, D_MODEL), F32)],
        compiler_params=pltpu.CompilerParams(dimension_semantics=("arbitrary",), vmem_limit_bytes=VMEM_LIMIT),
    )(x, dh2, cq, ckv, cs, dq, dk, dv, dgate, win, wuq, wkv, pvec, gvec)


def _grad_reduce(g_win, g_wuq, g_wkv, g_wout, g_wsp, gvec):
    gate0 = C_KR + ROPE
    cw = (D_INR - C_GATE + gate0) // 4
    qw = NOPE + ROPE
    rc = 128
    half_shapes = [(D_MODEL // 2, cw), (Q_LORA // 2, 2 * qw), (KV_LORA // 2, 2 * HP), g_wout.shape[2:], g_wsp.shape[2:]]
    n_arr = len(half_shapes)
    n_lay = 3
    order = [3, 0, 1, 2, 4]
    n_big = n_arr - 1
    k1 = lambda n, blk: 4 * n + blk
    k2 = lambda n, kk: 4 * n_arr + 3 * n + kk
    k3 = lambda n: 7 * n_arr + n
    k3w = lambda k: 7 * n_arr + n_big + k
    kv = lambda k: 7 * n_arr + n_big + 7 + k
    n_sem = 7 * n_arr + n_big + 14

    def body(*refs):
        gwin, gwuq, gwkv = refs[0:n_lay]
        gv = refs[n_arr]
        outs, ov = refs[n_arr + 1:2 * n_arr + 1], refs[2 * n_arr + 1]
        r1 = refs[2 * n_arr + 2:3 * n_arr + 2]
        r2 = refs[3 * n_arr + 2:4 * n_arr + 2]
        s2 = refs[4 * n_arr + 2:5 * n_arr + 2]
        g = refs[5 * n_arr + 2:5 * n_arr + 2 + n_lay] + refs[n_lay:n_arr]
        vbuf, send_sems, recv_sems = refs[5 * n_arr + 2 + n_lay:]
        x, y, c = lax.axis_index("x"), lax.axis_index("y"), lax.axis_index("c")
        j = 2 * x + y
        me = 2 * j + c
        sib = (x, y, 1 - c)
        chips = [(1 - x, y), (x, 1 - y), (1 - x, 1 - y)]
        others = [sib] + [(px, py, pc) for (px, py) in chips for pc in (c, 1 - c)]

        def copy(k, src, dst, to):
            return pltpu.make_async_remote_copy(
                src_ref=src, dst_ref=dst, send_sem=send_sems.at[k], recv_sem=recv_sems.at[k],
                device_id=to, device_id_type=MESH)

        def lay_in(r, carry):
            src = pl.ds(pl.multiple_of(r * rc, rc), rc)
            hf = r // (D_MODEL // 2 // rc)
            dst = pl.ds(pl.multiple_of((r % (D_MODEL // 2 // rc)) * rc, rc), rc)
            g[0][0, hf, dst, 0:C_KR] = gwin[src, 0:C_KR]
            g[0][0, hf, dst, C_KR:gate0] = gwin[src, C_KR + NOPE:C_KR + NOPE + ROPE]
            g[0][0, hf, dst, gate0:cw] = gwin[src, C_GATE:C_GATE + cw - gate0]
            for blk in range(1, 4):
                g[0][blk, hf, dst, :] = gwin[src, D_INR - (4 - blk) * cw:D_INR - (3 - blk) * cw]
            return carry

        def lay_heads():
            for h in range(HEADS):
                blk, e = h // 2, h % 2
                for hf in range(2):
                    rq = slice(hf * Q_LORA // 2, (hf + 1) * Q_LORA // 2)
                    rk = slice(hf * KV_LORA // 2, (hf + 1) * KV_LORA // 2)
                    g[1][blk, hf, :, e * qw:(e + 1) * qw] = gwuq[rq, h * HP:h * HP + qw]
                    g[2][blk, hf, :, e * HP:e * HP + NOPE] = gwkv[rk, h * HP:h * HP + NOPE]
                    g[2][blk, hf, :, e * HP + NOPE:(e + 1) * HP] = gwkv[rk, HEADS * HP + h * VDIM:HEADS * HP + (h + 1) * VDIM]

        l1, l2 = [], []

        def level1(n):
            for blk in range(4):
                l1.append(copy(k1(n, blk), g[n].at[blk, 1 - c], r1[n].at[blk], sib))
                l1[-1].start()

        def level2(n):
            for blk in range(4):
                copy(k1(n, blk), g[n].at[blk, c], r1[n].at[blk], sib).wait_recv()
            for blk in range(4):
                r1[n][blk] = g[n][blk, c] + r1[n][blk]
                s2[n][blk] = r1[n][blk].astype(BF16)
            for kk, (px, py) in enumerate(chips):
                l2.append(copy(k2(n, kk), s2[n].at[2 * px + py], r2[n].at[kk], (px, py, c)))
                l2[-1].start()

        lv = [copy(kv(k), gv, vbuf.at[me], to) for k, to in enumerate(others)]
        for cp in lv:
            cp.start()
        for n in range(n_lay, n_arr):
            level1(n)
        lay_heads()
        level1(1)
        level1(2)
        level2(order[0])
        lax.fori_loop(0, D_MODEL // rc, lay_in, 0)
        level1(0)
        for n in order[1:]:
            level2(n)

        l3 = []
        for n in order:
            for kk in range(3):
                copy(k2(n, kk), s2[n].at[0], r2[n].at[kk], sib).wait_recv()
            red = ((r1[n][j] + r2[n][0].astype(F32)) + r2[n][1].astype(F32)) + r2[n][2].astype(F32)
            if n < n_big:
                outs[n][c] = red
                back = [copy(k3(n), outs[n].at[c], outs[n].at[c], sib)]
            else:
                outs[n][j, c] = red
                back = [copy(k3w(k), outs[n].at[j, c], outs[n].at[j, c], to) for k, to in enumerate(others)]
            for cp in back:
                cp.start()
            l3 += back
        for n in range(n_big):
            copy(k3(n), outs[n].at[1 - c], outs[n].at[1 - c], sib).wait_recv()
        for k, (px, py, pc) in enumerate(others):
            landed = outs[n_big].at[2 * px + py, pc]
            copy(k3w(k), landed, landed, (px, py, pc)).wait_recv()
            copy(kv(k), gv, vbuf.at[4 * px + 2 * py + pc], (px, py, pc)).wait_recv()
        vbuf[me] = gv[...]
        total = vbuf[0]
        for d in range(1, 8):
            total = total + vbuf[d]
        ov[...] = total
        for cp in l1 + lv + l2 + l3:
            cp.wait_send()

    vmem = pl.BlockSpec(memory_space=pltpu.VMEM)
    assert g_win.shape == (D_MODEL, D_INR) and 4 * cw == D_INR - C_GATE + gate0 and NOPE + VDIM == HP and HEADS == 8
    out_shape =[jax.ShapeDtypeStruct((2,) + s, F32) for s in half_shapes[:n_big]]
    out_shape += [jax.ShapeDtypeStruct((4, 2) + half_shapes[n_big], F32), jax.ShapeDtypeStruct(gvec.shape, F32)]
    scratch = [pltpu.VMEM((4,) + s, F32) for s in half_shapes] + [pltpu.VMEM((3,) + s, BF16) for s in half_shapes]
    scratch += [pltpu.VMEM((4,) + s, BF16) for s in half_shapes]
    scratch += [pltpu.VMEM((4, 2) + s, F32) for s in half_shapes[:n_lay]]
    scratch += [pltpu.VMEM((8,) + gvec.shape, F32), pltpu.SemaphoreType.DMA((n_sem,)), pltpu.SemaphoreType.DMA((n_sem,))]
    return pl.pallas_call(
        body, name="grad_reduce", out_shape=out_shape,
        in_specs=[vmem] * (n_arr + 1), out_specs=[vmem] * (n_arr + 1), scratch_shapes=scratch,
        compiler_params=pltpu.CompilerParams(vmem_limit_bytes=VMEM_LIMIT),
    )(g_win, g_wuq, g_wkv, g_wout, g_wsp, gvec)


SMALL_ROWS = ((GV_QG, 1, Q_LORA), (GV_KVG, 1, KV_LORA), (GV_SG, 1, GW), (GV_SB, 1, GW),
              (GV_LNG, 1, D_MODEL), (GV_LNB, 1, D_MODEL), (GV_BSP, HEADS, CHUNK))


def _adam_update(g, w, m, v):
    m_new = ADAM_B1 * m + (1.0 - ADAM_B1) * g
    v_new = ADAM_B2 * v + (1.0 - ADAM_B2) * (g * g)
    m_hat = m_new / (1.0 - ADAM_B1 ** ADAM_STEP)
    v_hat = v_new / (1.0 - ADAM_B2 ** ADAM_STEP)
    return -ADAM_LR * (m_hat / (jnp.sqrt(v_hat) + ADAM_EPS) + ADAM_WD * w), m_new, v_new


def _adamw(g_big, w_big, m_big, v_big, gvec, w_small, m_small, v_small):
    nb, ns = len(g_big), len(w_small)

    def body(*refs):
        it = iter(refs)
        take = lambda n: [next(it) for _ in range(n)]
        g_b, w_b, m_b, v_b = take(nb), take(nb), take(nb), take(nb)
        gv = next(it)
        w_s, m_s, v_s = take(ns), take(ns), take(ns)
        g_bo, d_bo, m_bo, v_bo = take(nb), take(nb), take(nb), take(nb)
        g_so, d_so, m_so, v_so = take(ns), take(ns), take(ns), take(ns)
        for n in range(nb):
            gb = g_b[n][...]
            g_bo[n][...] = gb
            d_bo[n][...], m_bo[n][...], v_bo[n][...] = _adam_update(gb, w_b[n][...], m_b[n][...], v_b[n][...])
        for n, (row, nrow, width) in enumerate(SMALL_ROWS):
            gs = gv[row:row + nrow, 0:width]
            g_so[n][...] = gs
            d_so[n][...], m_so[n][...], v_so[n][...] = _adam_update(gs, w_s[n][...], m_s[n][...], v_s[n][...])

    def part(a):
        nd = a.ndim
        if nd > 2 or a.shape[0] % (SUBLANES * ADAM_STEPS) == 0:
            return pl.BlockSpec((a.shape[0] // ADAM_STEPS,) + a.shape[1:], lambda i: (i,) + (0,) * (nd - 1))
        assert a.shape[1] % (LANES * ADAM_STEPS) == 0
        return pl.BlockSpec((a.shape[0], a.shape[1] // ADAM_STEPS), lambda i: (0, i))

    def whole(a):
        nd = a.ndim
        return pl.BlockSpec(a.shape, lambda i: (0,) * nd)

    big = [jax.ShapeDtypeStruct(a.shape, F32) for a in w_big]
    small = [jax.ShapeDtypeStruct(a.shape, F32) for a in w_small]
    return pl.pallas_call(
        body, name="adamw", grid=(ADAM_STEPS,), out_shape=big * 4 + small * 4,
        in_specs=[part(a) for a in g_big + w_big + m_big + v_big] + [whole(gvec)]
        + [whole(a) for a in w_small + m_small + v_small],
        out_specs=[part(a) for a in w_big] * 4 + [whole(a) for a in w_small] * 4,
        compiler_params=pltpu.CompilerParams(dimension_semantics=("arbitrary",), vmem_limit_bytes=VMEM_LIMIT),
    )(*g_big, *w_big, *m_big, *v_big, gvec, *w_small, *m_small, *v_small)


def kernel(x, positions, w_in, q_norm_g, w_uq, kv_norm_g, w_ukv, sgu_norm_g, sgu_norm_b, w_spatial, b_spatial, w_out, ln_g, ln_b, loss_target, m_w_in, m_q_norm_g, m_w_uq, m_kv_norm_g, m_w_ukv, m_sgu_norm_g, m_sgu_norm_b, m_w_spatial, m_b_spatial, m_w_out, m_ln_g, m_ln_b, v_w_in, v_q_norm_g, v_w_uq, v_kv_norm_g, v_w_ukv, v_sgu_norm_g, v_sgu_norm_b, v_w_spatial, v_b_spatial, v_w_out, v_ln_g, v_ln_b):
    seq = x.shape[1]
    x2 = x.reshape(seq, D_MODEL)
    tgt = loss_target.reshape(seq, D_MODEL)
    pos = positions.reshape(seq // HP, HP)

    lane = np.arange(HP)
    half = ROPE // 2
    inv_freq = (1.0 / (ROPE_THETA ** (np.arange(half, dtype=np.float32) / half))).astype(np.float32)
    in_rope = (lane >= NOPE) & (lane < NOPE + ROPE)
    pv_const = np.zeros((PV_ROWS, D_MODEL), np.float32)
    pv_const[PV_INVF, 0:HP] = np.where(in_rope, inv_freq[(lane - NOPE) % half], 0.0)
    pv_const[PV_M1, 0:HP] = np.where((lane >= NOPE) & (lane < NOPE + half), -1.0, 0.0)
    pv_const[PV_M2, 0:HP] = np.where((lane >= NOPE + half) & (lane < NOPE + ROPE), 1.0, 0.0)
    pv_rows = [(PV_QG, q_norm_g), (PV_KVG, kv_norm_g), (PV_SG, sgu_norm_g), (PV_SB, sgu_norm_b), (PV_LNG, ln_g),
               (PV_LNB, ln_b)]
    win, wuq, wkv, a_out, pvec = _weight_gather([w_in, w_uq, w_ukv, w_out], jnp.asarray(pv_const),
                                                [(prow, vec.reshape(1, -1)) for prow, vec in pv_rows])
    wout = a_out.reshape(D_MODEL, D_MODEL)
    tri = jnp.tril(jnp.ones((CHUNK, CHUNK), dtype=bool))
    wt = jnp.where(tri[None], w_spatial, 0.0).astype(BF16)
    wtt = jnp.swapaxes(wt, 1, 2)
    bsp = jnp.repeat(b_spatial.T, VDIM, axis=1)

    cq, ckv, gate, q, k, v, vt, cs = _fwd_pre(x2, pos, win, wuq, wkv, pvec)
    o, lse = _attn_fwd(q, k, vt)
    dh2, do, dgate, g_wout, g_wsp, gvec = _post(x2, tgt, o, gate, wout, pvec, wt, wtt, bsp)
    dq, dk, dv = _attn_bwd(q, k, v, do, o, lse, cs, pvec)
    gx, g_win, g_wuq, g_wkv, gvec = _bwd_pre(x2, dh2, cq, ckv, cs, dq, dk, dv, dgate, win, wuq, wkv, pvec, gvec)

    r_in, r_uq, r_ukv, r_out, r_wsp, r_vec = _grad_reduce(
        g_win, g_wuq, g_wkv, g_wout.reshape(4, 2, D_MODEL // 8, D_MODEL), g_wsp.reshape(4, 2, CHUNK, CHUNK), gvec)

    big = [w_in, w_uq, w_ukv, w_out, w_spatial]
    flip = lambda a: a.T if a.ndim == 2 and a.shape[1] % LANES else a
    flips = lambda arrs: [flip(a) for a in arrs]
    g_big = flips([r_in.reshape(w_in.shape), r_uq.reshape(w_uq.shape), r_ukv.reshape(w_ukv.shape),
                   r_out.reshape(w_out.shape), r_wsp.reshape(w_spatial.shape)])
    small = lambda qg, kvg, sg, sb, lng, lnb, bs: [qg.reshape(1, -1), kvg.reshape(1, -1), sg.reshape(1, -1),
                                                   sb.reshape(1, -1), lng.reshape(1, -1), lnb.reshape(1, -1), bs]
    res = _adamw(g_big, flips(big), flips([m_w_in, m_w_uq, m_w_ukv, m_w_out, m_w_spatial]),
                 flips([v_w_in, v_w_uq, v_w_ukv, v_w_out, v_w_spatial]), r_vec,
                 small(q_norm_g, kv_norm_g, sgu_norm_g, sgu_norm_b, ln_g, ln_b, b_spatial),
                 small(m_q_norm_g, m_kv_norm_g, m_sgu_norm_g, m_sgu_norm_b, m_ln_g, m_ln_b, m_b_spatial),
                 small(v_q_norm_g, v_kv_norm_g, v_sgu_norm_g, v_sgu_norm_b, v_ln_g, v_ln_b, v_b_spatial))
    res = [r.T if n < 4 * len(big) and r.shape != big[n % len(big)].shape else r for n, r in enumerate(res)]

    def ordered(big, sm):
        vec = lambda n: sm[n].reshape(-1)
        return [big[0], vec(0), big[1], vec(1), big[2], vec(2), vec(3), big[4], sm[6], big[3], vec(4), vec(5)]

    loss = r_vec[GV_LOSS, 0]
    return (loss, gx.reshape(1, seq, D_MODEL), *ordered(res[0:5], res[20:27]), *ordered(res[5:10], res[27:34]),
            *ordered(res[10:15], res[34:41]), *ordered(res[15:20], res[41:48]))
```

```python
import math

import jax
import jax.numpy as jnp
import numpy as np
from jax import lax
from jax.experimental import pallas as pl
from jax.experimental.pallas import tpu as pltpu

F32 = jnp.float32
BF16 = jnp.bfloat16

D_MODEL = 1024
Q_LORA = 256
KV_LORA = 128
HEADS = 8
NOPE = 64
ROPE = 32
VDIM = 64
MLA_W = HEADS * VDIM
GW = 512
CHUNK = 128
HP = 128
PAIRS = HEADS // 2
D_IN = 2464
D_INR = 2560
C_CKV = Q_LORA
C_KR = Q_LORA + KV_LORA
C_GATE = C_KR + HP
ROPE_THETA = 10000.0
DN_ALPHA = 2.0 ** 0.25
EPS = 1e-5
SCALE = 1.0 / math.sqrt(NOPE + ROPE)
SCALE_LOG2E = SCALE * 1.4426950408889634
INV_SQRT2 = 0.7071067811865476
INV_SQRT_2PI = 0.3989422804014327

ADAM_LR = 0.001
ADAM_B1 = 0.9
ADAM_B2 = 0.999
ADAM_EPS = 1e-08
ADAM_WD = 0.01
ADAM_STEP = 10

PV_QG, PV_KVG, PV_SG, PV_SB, PV_INVF, PV_M1, PV_M2, PV_LNG, PV_LNB = range(9)
PV_ROWS = 16
GV_QG, GV_KVG, GV_SG, GV_SB, GV_LNG, GV_LNB, GV_LOSS = range(7)
GV_BSP = 8
GV_ROWS = 16

MESH = pl.DeviceIdType.MESH

FWD_TILE = 1024
POST_TILE = 512
BWD_TILE = 512
ATT_BLK = 512
ADAM_STEPS = 4
SUBLANES, LANES = 8, 128
VMEM_LIMIT = 60 * 1024 * 1024


def _dot(a, b):
    return jnp.dot(a, b, preferred_element_type=F32)


def _dot_nt(a, b):
    return lax.dot_general(a, b, (((1,), (1,)), ((), ())), preferred_element_type=F32)


def _dot_tn(a, b):
    return lax.dot_general(a, b, (((0,), (0,)), ((), ())), preferred_element_type=F32)


def _sigmoid(z):
    return pl.reciprocal(1.0 + jnp.exp(-z), approx=True)


def _gelu_and_grad(x):
    cdf = 0.5 * (1.0 + lax.erf(x * INV_SQRT2))
    return x * cdf, cdf + x * (INV_SQRT_2PI * jnp.exp(-0.5 * x * x))


def _rms_stats(x):
    r = lax.rsqrt(jnp.mean(x * x, axis=-1, keepdims=True) + EPS)
    return x * r, r


def _rms_bwd(dy, g, xh, r):
    dyg = dy * g
    return r * (dyg - xh * jnp.mean(dyg * xh, axis=-1, keepdims=True))


def _ln_stats(x):
    mu = jnp.mean(x, axis=-1, keepdims=True)
    xc = x - mu
    r = lax.rsqrt(jnp.mean(xc * xc, axis=-1, keepdims=True) + EPS)
    return xc * r, r


def _ln_bwd(dy, g, xh, r):
    dxh = dy * g
    return r * (dxh - jnp.mean(dxh, axis=-1, keepdims=True) - xh * jnp.mean(dxh * xh, axis=-1, keepdims=True))


def _rope_fwd(t, c, s1, s2):
    return t * c + pltpu.roll(t, HP - 16, 1) * s1 + pltpu.roll(t, 16, 1) * s2


def _rope_bwd(d, c, s1, s2):
    return d * c + pltpu.roll(d * s1, 16, 1) + pltpu.roll(d * s2, HP - 16, 1)


def _lane_lt64(shape):
    return lax.broadcasted_iota(jnp.int32, shape, len(shape) - 1) < 64


def _spatial_mix(w_ref, src, dst_ref, rows):
    for c in range(rows // CHUNK):
        for p in range(PAIRS):
            blk = src[c * CHUNK:(c + 1) * CHUNK, p * HP:(p + 1) * HP]
            a = _dot(w_ref[2 * p], blk)
            b = _dot(w_ref[2 * p + 1], blk)
            dst_ref[c * CHUNK:(c + 1) * CHUNK, p * HP:(p + 1) * HP] = jnp.where(_lane_lt64(a.shape), a, b)


def _gmlp_fwd(u_pre, v_pre, zb, sg, sb, wt_ref, bsp_ref, sv_ref, rows):
    u, du = _gelu_and_grad(u_pre)
    gv, dgv = _gelu_and_grad(v_pre)
    xh, r = _ln_stats(gv)
    vln = (xh * sg + sb).astype(BF16)
    _spatial_mix(wt_ref, vln, sv_ref, rows)
    bias = bsp_ref[...]
    svb = sv_ref[...] + jnp.concatenate([bias] * (rows // CHUNK), axis=0)
    sig = _sigmoid(zb)
    return u, du, dgv, xh, r, vln, svb, sig


N_GATHER_SEMS = 8


def _weight_gather(shards, pv_const, pv_rows):
    n_arr = len(shards)
    n_in = n_arr + 1 + len(pv_rows)
    w_in, w_uq, w_ukv, _ = shards
    cw = w_in.shape[1]
    gate0 = C_KR + ROPE
    rc = 128

    def body(*refs):
        ins = refs[0:n_arr]
        win_o, wuq_o, wkv_o, out_o, pv_o = refs[n_in:n_in + 5]
        a_in, a_uq, a_ukv, send_sems, recv_sems = refs[n_in + 5:]
        outs = [a_in, a_uq, a_ukv, out_o]
        pv_o[...] = refs[n_arr][...]
        for (prow, vec), ref in zip(pv_rows, refs[n_arr + 1:n_in]):
            pv_o[prow:prow + 1, 0:vec.shape[1]] = ref[...].astype(F32)
        x, y, c = lax.axis_index("x"), lax.axis_index("y"), lax.axis_index("c")
        j = 2 * x + y
        sib = (x, y, 1 - c)
        near = [(1 - x, y, c), (x, 1 - y, c)]
        b_near = [2 * (1 - x) + y, 2 * x + 1 - y]
        b_far = 2 * (1 - x) + 1 - y
        for n in range(n_arr):
            outs[n][j] = ins[n][...].astype(BF16)

        def half(n, blk, core):
            r = shards[n].shape[0] // 2
            return outs[n].at[blk, pl.ds(pl.multiple_of(core * r, 16), r), :]

        def quarter(n, blk, core, part):
            q = shards[n].shape[0] // 4
            return outs[n].at[blk, pl.ds(pl.multiple_of((2 * core + part) * q, 16), q), :]

        def copy(k, ref, to):
            return pltpu.make_async_remote_copy(
                src_ref=ref, dst_ref=ref, send_sem=send_sems.at[k], recv_sem=recv_sems.at[k],
                device_id=to, device_id_type=MESH)

        sent = [copy(N_GATHER_SEMS * n + kk, half(n, j, c), near[kk]) for n in range(n_arr) for kk in range(2)]
        for cp in sent:
            cp.start()

        def send(k, ref, to):
            sent.append(copy(k, ref, to))
            sent[-1].start()

        for n in range(n_arr):
            k0 = N_GATHER_SEMS * n
            for kk in range(2):
                copy(k0 + kk, half(n, b_near[kk], c), near[kk]).wait_recv()
                send(k0 + 2 + kk, quarter(n, b_near[kk], c, kk), near[1 - kk])
                send(k0 + 4 + kk, half(n, b_near[kk], c), sib)
        for n in range(n_arr):
            k0 = N_GATHER_SEMS * n
            for kk in range(2):
                copy(k0 + 2 + kk, quarter(n, b_far, c, kk), near[1 - kk]).wait_recv()
                send(k0 + 6 + kk, quarter(n, b_far, c, kk), sib)
        for n in range(n_arr):
            k0 = N_GATHER_SEMS * n
            for kk in range(2):
                copy(k0 + 4 + kk, half(n, b_near[kk], 1 - c), sib).wait_recv()
                copy(k0 + 6 + kk, quarter(n, b_far, 1 - c, kk), sib).wait_recv()

        def lay_in(r, carry):
            rows = pl.ds(pl.multiple_of(r * rc, rc), rc)
            win_o[rows, 0:C_KR] = a_in[0, rows, 0:C_KR]
            win_o[rows, C_KR:C_KR + NOPE] = jnp.zeros((rc, NOPE), BF16)
            win_o[rows, C_KR + NOPE:C_KR + NOPE + ROPE] = a_in[0, rows, C_KR:gate0]
            win_o[rows, C_KR + NOPE + ROPE:C_GATE] = jnp.zeros((rc, HP - NOPE - ROPE), BF16)
            win_o[rows, C_GATE:C_GATE + cw - gate0] = a_in[0, rows, gate0:cw]
            for blk in range(1, 4):
                win_o[rows, D_INR - (4 - blk) * cw:D_INR - (3 - blk) * cw] = a_in[blk, rows, :]
            return carry

        lax.fori_loop(0, D_MODEL // rc, lay_in, 0)
        wuq_o[...] = jnp.zeros(wuq_o.shape, BF16)
        wkv_o[...] = jnp.zeros(wkv_o.shape, BF16)
        qw = NOPE + ROPE
        for h in range(HEADS):
            blk, e = h // 2, h % 2
            wuq_o[:, h * HP:h * HP + qw] = a_uq[blk, :, e * qw:(e + 1) * qw]
            wkv_o[:, h * HP:h * HP + NOPE] = a_ukv[blk, :, e * HP:e * HP + NOPE]
            wkv_o[:, HEADS * HP + h * VDIM:HEADS * HP + (h + 1) * VDIM] = a_ukv[blk, :, e * HP + NOPE:(e + 1) * HP]
        for cp in sent:
            cp.wait_send()

    assert cw > gate0 and C_GATE + 4 * cw - gate0 == D_INR and NOPE + VDIM == HP and HEADS == 8
    vmem = pl.BlockSpec(memory_space=pltpu.VMEM)
    gathered = lambda a: (4,) + a.shape
    return pl.pallas_call(
        body, name="weight_gather",
        out_shape=[jax.ShapeDtypeStruct((D_MODEL, D_INR), BF16), jax.ShapeDtypeStruct((Q_LORA, HEADS * HP), BF16),
                   jax.ShapeDtypeStruct((KV_LORA, HEADS * HP + MLA_W), BF16),
                   jax.ShapeDtypeStruct(gathered(shards[3]), BF16), jax.ShapeDtypeStruct(pv_const.shape, F32)],
        in_specs=[vmem] * n_in, out_specs=[vmem] * 5,
        scratch_shapes=[pltpu.VMEM(gathered(a), BF16) for a in shards[0:3]]
        + [pltpu.SemaphoreType.DMA((N_GATHER_SEMS * n_arr,)), pltpu.SemaphoreType.DMA((N_GATHER_SEMS * n_arr,))],
        compiler_params=pltpu.CompilerParams(vmem_limit_bytes=VMEM_LIMIT),
    )(*shards, pv_const, *[vec for _, vec in pv_rows])


def _fwd_pre(x, pos, win, wuq, wkv, pvec):
    seq = x.shape[0]
    t = FWD_TILE

    def body(x_ref, pos_ref, win_ref, wuq_ref, wkv_ref, pv_ref,
             cq_o, ckv_o, gate_o, q_o, k_o, v_o, vt_o, cs_o):
        xb = x_ref[...].astype(BF16)
        proj = _dot(xb, win_ref[:, 0:C_GATE])
        cq = proj[:, 0:C_CKV]
        ckv = proj[:, C_CKV:C_KR]
        kr = proj[:, C_KR:C_GATE]
        cq_o[...] = cq
        ckv_o[...] = ckv

        ang = pos_ref[...].astype(F32) * pv_ref[PV_INVF:PV_INVF + 1, 0:HP]
        cos = jnp.cos(ang)
        sin = jnp.sin(ang)
        cs_o[:, 0:HP] = cos
        cs_o[:, HP:2 * HP] = sin
        s1 = sin * pv_ref[PV_M1:PV_M1 + 1, 0:HP]
        s2 = sin * pv_ref[PV_M2:PV_M2 + 1, 0:HP]

        cqh, _ = _rms_stats(cq)
        q_all = _dot((cqh * pv_ref[PV_QG:PV_QG + 1, 0:Q_LORA]).astype(BF16), wuq_ref[...])
        ckvh, _ = _rms_stats(ckv)
        kv_all = _dot((ckvh * pv_ref[PV_KVG:PV_KVG + 1, 0:KV_LORA]).astype(BF16), wkv_ref[...])
        krr = _rope_fwd(kr, cos, s1, s2)
        for h in range(HEADS):
            sl = slice(h * HP, (h + 1) * HP)
            q_o[:, sl] = (_rope_fwd(q_all[:, sl], cos, s1, s2) * SCALE_LOG2E).astype(BF16)
            k_o[:, sl] = (kv_all[:, sl] + krr).astype(BF16)
        val = kv_all[:, HEADS * HP:].astype(BF16)
        v_o[...] = val
        for blk in range(t // ATT_BLK):
            vt_o[blk] = val[blk * ATT_BLK:(blk + 1) * ATT_BLK, :].T
        gate_o[...] = _dot(xb, win_ref[:, C_GATE:D_INR]).astype(BF16)

    tile = lambda w: pl.BlockSpec((t, w), lambda i: (i, 0))
    full = lambda a: pl.BlockSpec(a.shape, lambda i: (0,) * a.ndim)
    outs = [(Q_LORA, F32), (KV_LORA, F32), (2048, BF16), (HEADS * HP, BF16), (HEADS * HP, BF16), (MLA_W, BF16)]
    assert t % ATT_BLK == 0
    out_specs = [tile(w) for w, _ in outs]
    out_specs += [pl.BlockSpec((t // ATT_BLK, MLA_W, ATT_BLK), lambda i: (i, 0, 0)), tile(2 * HP)]
    out_shape = [jax.ShapeDtypeStruct((seq, w), d) for w, d in outs]
    out_shape += [jax.ShapeDtypeStruct((seq // ATT_BLK, MLA_W, ATT_BLK), BF16), jax.ShapeDtypeStruct((seq, 2 * HP), F32)]
    return pl.pallas_call(
        body, name="fwd_pre", grid=(seq // t,),
        in_specs=[tile(D_MODEL), tile(1), full(win), full(wuq), full(wkv), full(pvec)],
        out_specs=out_specs, out_shape=out_shape,
        compiler_params=pltpu.CompilerParams(dimension_semantics=("arbitrary",), vmem_limit_bytes=VMEM_LIMIT),
    )(x, pos, win, wuq, wkv, pvec)


def _attn_fwd(q, k, vt):
    seq = q.shape[0]
    b = ATT_BLK
    nq = seq // b
    assert nq % 2 == 0
    assert nq % 4 == 0
    n_wide = sum(i // 4 for i in range(nq))

    def body(q_ref, k_ref, vt_ref, o_o, lse_o, m_ref, l_ref, acc_ref, s_even, s_odd):
        m_ref[...] = jnp.full(m_ref.shape, -jnp.inf, F32)
        l_ref[...] = jnp.zeros(l_ref.shape, F32)
        acc_ref[...] = jnp.zeros(acc_ref.shape, F32)

        def scores(i, j, s_ref, nkb):
            qrows = pl.ds(pl.multiple_of(i * b, b), b)
            krows = pl.ds(pl.multiple_of(j * b, b), nkb * b)
            for a in range(2):
                s_ref[a, 0:nkb * b, :] = _dot_nt(k_ref[krows, a * HP:(a + 1) * HP], q_ref[qrows, a * HP:(a + 1) * HP])

        def consume(i, j, s_ref, nkb, masked):
            for a in range(2):
                st = s_ref[a, 0:nkb * b, :]
                if masked:
                    ki = lax.broadcasted_iota(jnp.int32, st.shape, 0)
                    qi = lax.broadcasted_iota(jnp.int32, st.shape, 1)
                    st = jnp.where(ki <= qi, st, -jnp.inf)
                m_prev = m_ref[i, a:a + 1, :]
                m_new = jnp.maximum(m_prev, jnp.max(st, axis=0, keepdims=True))
                alpha = jnp.exp2(m_prev - m_new)
                pt = jnp.exp2(st - m_new)
                ptb = pt.astype(BF16)
                l_ref[i, a:a + 1, :] = alpha * l_ref[i, a:a + 1, :] + jnp.sum(pt, axis=0, keepdims=True)
                pv = _dot(vt_ref[j], ptb[0:b, :])
                for kb in range(1, nkb):
                    pv = pv + _dot(vt_ref[j + kb], ptb[kb * b:(kb + 1) * b, :])
                acc_ref[i, a] = alpha * acc_ref[i, a] + pv
                m_ref[i, a:a + 1, :] = m_new

        def run(count, first, following, nkb, masked):
            if count == 0:
                return
            scores(*first, s_even, nkb)

            def two(u, ij):
                nxt = following(*ij)
                scores(*nxt, s_odd, nkb)
                consume(*ij, s_even, nkb, masked)
                nxt2 = following(*nxt)
                scores(*nxt2, s_even, nkb)
                consume(*nxt, s_odd, nkb, masked)
                return nxt2

            last = lax.fori_loop(0, count // 2, two, tuple(jnp.int32(c) for c in first))
            if count % 2:
                consume(*last, s_even, nkb, masked)

        def clamp(i):
            return jnp.minimum(i, nq - 1)

        def next_wide(i, j):
            wrap = j + 8 > i
            return clamp(jnp.where(wrap, i + 1, i)), jnp.where(wrap, 0, j + 4)

        def next_pair(i, j):
            low = lax.rem(i, 4) == 2
            return clamp(jnp.where(low, i + 1, i + 3)), jnp.minimum(jnp.where(low, j, j + 4), nq - 4)

        run(n_wide, (4, 0), next_wide, 4, False)
        run(nq // 2, (2, 0), next_pair, 2, False)
        run(nq // 2, (1, 0), lambda i, j: (clamp(i + 2), jnp.minimum(j + 2, nq - 2)), 1, False)
        run(nq, (0, 0), lambda i, j: (clamp(i + 1), clamp(j + 1)), 1, True)
        top = lax.broadcasted_iota(jnp.int32, (HP, b), 0) < 64

        def finish(i, carry):
            rows = pl.ds(pl.multiple_of(i * b, b), b)
            o_o[rows, :] = jnp.where(top, acc_ref[i, 0] / l_ref[i, 0:1, :], acc_ref[i, 1] / l_ref[i, 1:2, :]).T
            lse_o[i] = m_ref[i, 0:2, :] + jnp.log2(l_ref[i, 0:2, :])
            return carry

        lax.fori_loop(0, nq, finish, 0)

    return pl.pallas_call(
        body, name="attn_fwd", grid=(PAIRS,),
        in_specs=[pl.BlockSpec((seq, 2 * HP), lambda p: (0, p)),
                  pl.BlockSpec((seq, 2 * HP), lambda p: (0, p)),
                  pl.BlockSpec((nq, HP, b), lambda p: (0, p, 0))],
        out_specs=[pl.BlockSpec((seq, HP), lambda p: (0, p)),
                   pl.BlockSpec((None, nq, 2, b), lambda p: (p, 0, 0, 0))],
        out_shape=[jax.ShapeDtypeStruct((seq, MLA_W), F32),
                   jax.ShapeDtypeStruct((PAIRS, nq, 2, b), F32)],
        scratch_shapes=[pltpu.VMEM((nq, 8, b), F32), pltpu.VMEM((nq, 8, b), F32), pltpu.VMEM((nq, 2, HP, b), F32),
                        pltpu.VMEM((2, 4 * b, b), F32), pltpu.VMEM((2, 4 * b, b), F32)],
        compiler_params=pltpu.CompilerParams(dimension_semantics=("arbitrary",), vmem_limit_bytes=VMEM_LIMIT),
    )(q, k, vt)


def _post(x, tgt, o, gate, wout, pvec, wt, wtt, bsp):
    seq = x.shape[0]
    t = POST_TILE
    nt = seq // t

    def body(x_ref, tgt_ref, o_ref, gate_ref, wout_ref, pv_ref, wt_ref, wtt_ref, bsp_ref,
             dh2_o, do_o, dgate_o, gwout_o, gwsp_o, vec_o, sv_ref, dvln_ref, bacc_ref):
        i = pl.program_id(0)

        @pl.when(i == 0)
        def _():
            gwout_o[...] = jnp.zeros_like(gwout_o)
            gwsp_o[...] = jnp.zeros_like(gwsp_o)
            vec_o[...] = jnp.zeros_like(vec_o)
            bacc_ref[...] = jnp.zeros_like(bacc_ref)

        za = gate_ref[:, 0:512].astype(F32)
        u_pre = gate_ref[:, 512:1024].astype(F32)
        v_pre = gate_ref[:, 1024:1536].astype(F32)
        zb = gate_ref[:, 1536:2048].astype(F32)
        sg = pv_ref[PV_SG:PV_SG + 1, 0:GW]
        sb = pv_ref[PV_SB:PV_SB + 1, 0:GW]
        lng = pv_ref[PV_LNG:PV_LNG + 1, :]
        lnb = pv_ref[PV_LNB:PV_LNB + 1, :]
        o = o_ref[...]

        sig_a = _sigmoid(za)
        silu_a = za * sig_a
        u, du, dgv, xh, r, vln, svb, sig_b = _gmlp_fwd(u_pre, v_pre, zb, sg, sb, wt_ref, bsp_ref, sv_ref, t)
        silu_b = zb * sig_b
        sgu = u * svb
        merged = jnp.concatenate([o * silu_a, sgu * silu_b], axis=1).astype(BF16)
        h2 = DN_ALPHA * x_ref[...] + _dot(merged, wout_ref[...])
        xh2, r2 = _ln_stats(h2)
        err = xh2 * lng + lnb - tgt_ref[...]
        d_out = err * (1.0 / D_MODEL)
        vec_o[GV_LNG:GV_LNG + 1, :] += jnp.sum(d_out * xh2, axis=0, keepdims=True)
        vec_o[GV_LNB:GV_LNB + 1, :] += jnp.sum(d_out, axis=0, keepdims=True)
        vec_o[GV_LOSS:GV_LOSS + 1, :] += jnp.sum(err * err, axis=0, keepdims=True) * (0.5 / D_MODEL)

        d_h2 = _ln_bwd(d_out, lng, xh2, r2)
        dh2_o[...] = d_h2
        dh2b = d_h2.astype(BF16)
        gwout_o[...] += _dot_tn(merged, dh2b)
        d_m = _dot_nt(dh2b, wout_ref[...])
        d_oa = d_m[:, 0:512]
        d_ob = d_m[:, 512:1024]
        do_o[...] = (d_oa * silu_a).astype(BF16)
        dgate_o[:, 0:512] = (d_oa * o * (sig_a * (1.0 + za * (1.0 - sig_a)))).astype(BF16)
        dgate_o[:, 1536:2048] = (d_ob * sgu * (sig_b * (1.0 + zb * (1.0 - sig_b)))).astype(BF16)
        d_sgu = d_ob * silu_b
        dgate_o[:, 512:1024] = (d_sgu * svb * du).astype(BF16)
        d_sv = d_sgu * u
        acc = bacc_ref[...]
        for c in range(t // CHUNK):
            acc = acc + d_sv[c * CHUNK:(c + 1) * CHUNK, :]
        bacc_ref[...] = acc
        d_svb = d_sv.astype(BF16)
        for c in range(t // CHUNK):
            for p in range(PAIRS):
                blk = d_svb[c * CHUNK:(c + 1) * CHUNK, p * HP:(p + 1) * HP]
                vblk = vln[c * CHUNK:(c + 1) * CHUNK, p * HP:(p + 1) * HP]
                first = _lane_lt64(blk.shape)
                gwsp_o[2 * p] += _dot_nt(jnp.where(first, blk, jnp.zeros_like(blk)), vblk)
                gwsp_o[2 * p + 1] += _dot_nt(jnp.where(first, jnp.zeros_like(blk), blk), vblk)
        _spatial_mix(wtt_ref, d_svb, dvln_ref, t)
        d_vln = dvln_ref[...]
        vec_o[GV_SG:GV_SG + 1, 0:GW] += jnp.sum(d_vln * xh, axis=0, keepdims=True)
        vec_o[GV_SB:GV_SB + 1, 0:GW] += jnp.sum(d_vln, axis=0, keepdims=True)
        dgate_o[:, 1024:1536] = (_ln_bwd(d_vln, sg, xh, r) * dgv).astype(BF16)


        @pl.when(i == nt - 1)
        def _():
            tri = (lax.broadcasted_iota(jnp.int32, (CHUNK, CHUNK), 1)
                   <= lax.broadcasted_iota(jnp.int32, (CHUNK, CHUNK), 0))
            for h in range(HEADS):
                gwsp_o[h] = jnp.where(tri, gwsp_o[h], 0.0)
            lane = lax.broadcasted_iota(jnp.int32, (CHUNK, HP), 1)
            res = jnp.zeros((CHUNK, HP), F32)
            for h in range(HEADS):
                p, a = divmod(h, 2)
                blk = bacc_ref[:, p * HP:(p + 1) * HP]
                part = jnp.where(_lane_lt64(blk.shape) == (a == 0), blk, 0.0)
                res = jnp.where(lane == h, jnp.sum(part, axis=-1, keepdims=True), res)
            vec_o[GV_BSP:GV_BSP + HEADS, 0:HP] = res.T[0:HEADS, :]
            lane1 = lax.broadcasted_iota(jnp.int32, (1, D_MODEL), 1)
            total = jnp.sum(vec_o[GV_LOSS:GV_LOSS + 1, :], axis=-1, keepdims=True)
            vec_o[GV_LOSS:GV_LOSS + 1, :] = jnp.where(lane1 == 0, total, 0.0)

    tile = lambda w: pl.BlockSpec((t, w), lambda i: (i, 0))
    full = lambda a: pl.BlockSpec(a.shape, lambda i: (0,) * a.ndim)
    const = lambda s: pl.BlockSpec(s, lambda i: (0,) * len(s))
    return pl.pallas_call(
        body, name="post", grid=(nt,),
        in_specs=[tile(D_MODEL), tile(D_MODEL), tile(MLA_W), tile(2048), full(wout), full(pvec),
                  full(wt), full(wtt), full(bsp)],
        out_specs=[tile(D_MODEL), tile(MLA_W), tile(2048), const((D_MODEL, D_MODEL)),
                   const((HEADS, CHUNK, CHUNK)), const((GV_ROWS, D_MODEL))],
        out_shape=[jax.ShapeDtypeStruct((seq, D_MODEL), F32), jax.ShapeDtypeStruct((seq, MLA_W), BF16),
                   jax.ShapeDtypeStruct((seq, 2048), BF16), jax.ShapeDtypeStruct((D_MODEL, D_MODEL), F32),
                   jax.ShapeDtypeStruct((HEADS, CHUNK, CHUNK), F32), jax.ShapeDtypeStruct((GV_ROWS, D_MODEL), F32)],
        scratch_shapes=[pltpu.VMEM((t, GW), F32), pltpu.VMEM((t, GW), F32), pltpu.VMEM((CHUNK, GW), F32)],
        compiler_params=pltpu.CompilerParams(dimension_semantics=("arbitrary",), vmem_limit_bytes=VMEM_LIMIT),
    )(x, tgt, o, gate, wout, pvec, wt, wtt, bsp)


def _attn_bwd(q, k, v, do, o, lse, cs, pvec):
    seq = q.shape[0]
    b = ATT_BLK
    nq = seq // b

    def body(q_ref, k_ref, v_ref, do_ref, o_ref, lse_ref, cs_ref, pv_ref, dq_o, dk_o, dv_o, dk_acc, dv_acc):
        i = pl.program_id(1)

        @pl.when(i == 0)
        def _():
            dk_acc[...] = jnp.zeros_like(dk_acc)
            dv_acc[...] = jnp.zeros_like(dv_acc)

        first = _lane_lt64((b, HP))
        do = do_ref[...]
        zero = jnp.zeros_like(do)
        dos = [jnp.where(first, do, zero), jnp.where(first, zero, do)]
        prod_t = (do.astype(F32) * o_ref[...]).T
        deltas = [jnp.sum(prod_t[0:64, :], axis=0, keepdims=True),
                  jnp.sum(prod_t[64:128, :], axis=0, keepdims=True)]
        lses = [lse_ref[0:1, :], lse_ref[1:2, :]]
        qs = [q_ref[:, a * HP:(a + 1) * HP] for a in range(2)]

        def step(j, dqs, masked, nk=b):
            rows = pl.ds(pl.multiple_of(j * b, b), nk)
            vb = v_ref[rows, :]
            new_dq = []
            dvs = []
            for a in range(2):
                kb = k_ref[rows, a * HP:(a + 1) * HP]
                pt = jnp.exp2(_dot_nt(kb, qs[a]) - lses[a])
                if masked:
                    ki = lax.broadcasted_iota(jnp.int32, pt.shape, 0)
                    qi = lax.broadcasted_iota(jnp.int32, pt.shape, 1) + (nk - b)
                    pt = jnp.where(ki <= qi, pt, 0.0)
                dvs.append(_dot(pt.astype(BF16), do))
                dpt = _dot_nt(vb, dos[a])
                dst = (pt * (dpt - deltas[a])).astype(BF16)
                dk_acc[rows, a * HP:(a + 1) * HP] += _dot(dst, qs[a])
                new_dq.append(dqs[a] + _dot_tn(dst, kb))
            dv_acc[rows, :] += jnp.where(_lane_lt64((nk, HP)), dvs[0], dvs[1])
            return tuple(new_dq)

        init = (jnp.zeros((b, HP), F32), jnp.zeros((b, HP), F32))
        dqs = lax.fori_loop(0, i // 4, lambda jj, cr: step(4 * jj, cr, False, 4 * b), init)
        last = [lambda cr, w=w: step(4 * (i // 4), cr, True, w * b) for w in (1, 2, 3, 4)]
        dqs = lax.switch(i % 4, last, dqs)
        cos = cs_ref[:, 0:HP]
        sin = cs_ref[:, HP:2 * HP]
        s1 = sin * pv_ref[PV_M1:PV_M1 + 1, 0:HP]
        s2 = sin * pv_ref[PV_M2:PV_M2 + 1, 0:HP]
        for a in range(2):
            dq_o[:, a * HP:(a + 1) * HP] = _rope_bwd(dqs[a] * SCALE, cos, s1, s2).astype(BF16)

        @pl.when(i == nq - 1)
        def _():
            dk_o[...] = (dk_acc[...] * (SCALE / SCALE_LOG2E)).astype(BF16)
            dv_o[...] = dv_acc[...].astype(BF16)

    return pl.pallas_call(
        body, name="attn_bwd", grid=(PAIRS, nq),
        in_specs=[pl.BlockSpec((b, 2 * HP), lambda p, i: (i, p)),
                  pl.BlockSpec((seq, 2 * HP), lambda p, i: (0, p)),
                  pl.BlockSpec((seq, HP), lambda p, i: (0, p)),
                  pl.BlockSpec((b, HP), lambda p, i: (i, p)),
                  pl.BlockSpec((b, HP), lambda p, i: (i, p)),
                  pl.BlockSpec((None, None, 2, b), lambda p, i: (p, i, 0, 0)),
                  pl.BlockSpec((b, 2 * HP), lambda p, i: (i, 0)),
                  pl.BlockSpec(pvec.shape, lambda p, i: (0, 0))],
        out_specs=[pl.BlockSpec((b, 2 * HP), lambda p, i: (i, p)),
                   pl.BlockSpec((seq, 2 * HP), lambda p, i: (0, p)),
                   pl.BlockSpec((seq, HP), lambda p, i: (0, p))],
        out_shape=[jax.ShapeDtypeStruct((seq, HEADS * HP), BF16),
                   jax.ShapeDtypeStruct((seq, HEADS * HP), BF16),
                   jax.ShapeDtypeStruct((seq, MLA_W), BF16)],
        scratch_shapes=[pltpu.VMEM((seq, 2 * HP), F32), pltpu.VMEM((seq, HP), F32)],
        compiler_params=pltpu.CompilerParams(dimension_semantics=("arbitrary", "arbitrary"),
                                             vmem_limit_bytes=VMEM_LIMIT),
    )(q, k, v, do, o, lse, cs, pvec)


def _bwd_pre(x, dh2, cq, ckv, cs, dq, dk, dv, dgate, win, wuq, wkv, pvec, gvec):
    seq = x.shape[0]
    t = BWD_TILE

    def body(x_ref, dh2_ref, cq_ref, ckv_ref, cs_ref, dq_ref, dk_ref, dv_ref, dgate_ref,
             win_ref, wuq_ref, wkv_ref, pv_ref, gv_ref, gx_o, gwin_o, gwuq_o, gwkv_o, vec_o):
        i = pl.program_id(0)

        @pl.when(i == 0)
        def _():
            gwin_o[...] = jnp.zeros_like(gwin_o)
            gwuq_o[...] = jnp.zeros_like(gwuq_o)
            gwkv_o[...] = jnp.zeros_like(gwkv_o)
            vec_o[...] = gv_ref[...]

        xb = x_ref[...].astype(BF16)
        dgate = dgate_ref[...]
        gwin_o[:, C_GATE:D_INR] += _dot_tn(xb, dgate)
        gx_gate = _dot_nt(dgate, win_ref[:, C_GATE:D_INR])

        qg = pv_ref[PV_QG:PV_QG + 1, 0:Q_LORA]
        kvg = pv_ref[PV_KVG:PV_KVG + 1, 0:KV_LORA]
        dq = dq_ref[...]
        cqh, rq = _rms_stats(cq_ref[...])
        d_cqn = _dot_nt(dq, wuq_ref[...])
        gwuq_o[...] += _dot_tn((cqh * qg).astype(BF16), dq)
        vec_o[GV_QG:GV_QG + 1, 0:Q_LORA] += jnp.sum(d_cqn * cqh, axis=0, keepdims=True)
        d_cq = _rms_bwd(d_cqn, qg, cqh, rq)

        dk = dk_ref[...]
        dkv = jnp.concatenate([dk, dv_ref[...]], axis=1)
        ckvh, rkv = _rms_stats(ckv_ref[...])
        d_ckvn = _dot_nt(dkv, wkv_ref[...])
        gwkv_o[...] += _dot_tn((ckvh * kvg).astype(BF16), dkv)
        vec_o[GV_KVG:GV_KVG + 1, 0:KV_LORA] += jnp.sum(d_ckvn * ckvh, axis=0, keepdims=True)
        d_ckv = _rms_bwd(d_ckvn, kvg, ckvh, rkv)

        dks = dk[:, 0:HP].astype(F32)
        for h in range(1, HEADS):
            dks = dks + dk[:, h * HP:(h + 1) * HP].astype(F32)
        cos = cs_ref[:, 0:HP]
        sin = cs_ref[:, HP:2 * HP]
        d_kr = _rope_bwd(dks, cos, sin * pv_ref[PV_M1:PV_M1 + 1, 0:HP], sin * pv_ref[PV_M2:PV_M2 + 1, 0:HP])

        d_lat = jnp.concatenate([d_cq.astype(BF16), d_ckv.astype(BF16), d_kr.astype(BF16)], axis=1)
        gwin_o[:, 0:C_GATE] += _dot_tn(xb, d_lat)
        gx_o[...] = DN_ALPHA * dh2_ref[...] + gx_gate + _dot_nt(d_lat, win_ref[:, 0:C_GATE])

    tile = lambda w: pl.BlockSpec((t, w), lambda i: (i, 0))
    full = lambda a: pl.BlockSpec(a.shape, lambda i: (0,) * a.ndim)
    const = lambda s: pl.BlockSpec(s, lambda i: (0,) * len(s))
    return pl.pallas_call(
        body, name="bwd_pre", grid=(seq // t,),
        in_specs=[tile(D_MODEL), tile(D_MODEL), tile(Q_LORA), tile(KV_LORA), tile(2 * HP), tile(HEADS * HP),
                  tile(HEADS * HP), tile(MLA_W), tile(2048), full(win), full(wuq), full(wkv), full(pvec), full(gvec)],
        out_specs=[tile(D_MODEL), const((D_MODEL, D_INR)), const((Q_LORA, HEADS * HP)),
                   const((KV_LORA, HEADS * HP + MLA_W)), const((GV_ROWS, D_MODEL))],
        out_shape=[jax.ShapeDtypeStruct((seq, D_MODEL), F32), jax.ShapeDtypeStruct((D_MODEL, D_INR), F32),
                   jax.ShapeDtypeStruct((Q_LORA, HEADS * HP), F32),
                   jax.ShapeDtypeStruct((KV_LORA, HEADS * HP + MLA_W), F32),
                   jax.ShapeDtypeStruct((GV_ROWS, D_MODEL), F32)],
        compiler_params=pltpu.CompilerParams(dimension_semantics=("arbitrary",), vmem_limit_bytes=VMEM_LIMIT),
    )(x, dh2, cq, ckv, cs, dq, dk, dv, dgate, win, wuq, wkv, pvec, gvec)


def _grad_reduce(g_win, g_wuq, g_wkv, g_wout, g_wsp, gvec):
    gate0 = C_KR + ROPE
    cw = (D_INR - C_GATE + gate0) // 4
    qw = NOPE + ROPE
    rc = 128
    half_shapes = [(D_MODEL // 2, cw), (Q_LORA // 2, 2 * qw), (KV_LORA // 2, 2 * HP), g_wout.shape[2:], g_wsp.shape[2:]]
    n_arr = len(half_shapes)
    n_lay = 3
    order = [3, 0, 1, 2, 4]
    n_big = n_arr - 1
    k1 = lambda n, blk: 4 * n + blk
    k2 = lambda n, kk: 4 * n_arr + 6 * n + kk
    k3 = lambda n: 10 * n_arr + n
    k3w = lambda k: 10 * n_arr + n_big + k
    kv = lambda k: 10 * n_arr + n_big + 7 + k
    n_sem = 10 * n_arr + n_big + 14

    def body(*refs):
        gwin, gwuq, gwkv = refs[0:n_lay]
        gv = refs[n_arr]
        outs, ov = refs[n_arr + 1:2 * n_arr + 1], refs[2 * n_arr + 1]
        r1 = refs[2 * n_arr + 2:3 * n_arr + 2]
        r2 = refs[3 * n_arr + 2:4 * n_arr + 2]
        s2 = refs[4 * n_arr + 2:5 * n_arr + 2]
        g = refs[5 * n_arr + 2:5 * n_arr + 2 + n_lay] + refs[n_lay:n_arr]
        vbuf, send_sems, recv_sems = refs[5 * n_arr + 2 + n_lay:]
        x, y, c = lax.axis_index("x"), lax.axis_index("y"), lax.axis_index("c")
        j = 2 * x + y
        me = 2 * j + c
        sib = (x, y, 1 - c)
        chips = [(1 - x, y), (x, 1 - y), (1 - x, 1 - y)]
        others = [sib] + [(px, py, pc) for (px, py) in chips for pc in (c, 1 - c)]
        near = [(1 - x, y, c), (x, 1 - y, c)]
        b_near = [2 * (1 - x) + y, 2 * x + 1 - y]
        b_far = 2 * (1 - x) + 1 - y

        def copy(k, src, dst, to):
            return pltpu.make_async_remote_copy(
                src_ref=src, dst_ref=dst, send_sem=send_sems.at[k], recv_sem=recv_sems.at[k],
                device_id=to, device_id_type=MESH)

        def lay_in(r, carry):
            src = pl.ds(pl.multiple_of(r * rc, rc), rc)
            hf = r // (D_MODEL // 2 // rc)
            dst = pl.ds(pl.multiple_of((r % (D_MODEL // 2 // rc)) * rc, rc), rc)
            g[0][0, hf, dst, 0:C_KR] = gwin[src, 0:C_KR]
            g[0][0, hf, dst, C_KR:gate0] = gwin[src, C_KR + NOPE:C_KR + NOPE + ROPE]
            g[0][0, hf, dst, gate0:cw] = gwin[src, C_GATE:C_GATE + cw - gate0]
            for blk in range(1, 4):
                g[0][blk, hf, dst, :] = gwin[src, D_INR - (4 - blk) * cw:D_INR - (3 - blk) * cw]
            return carry

        def lay_heads():
            for h in range(HEADS):
                blk, e = h // 2, h % 2
                for hf in range(2):
                    rq = slice(hf * Q_LORA // 2, (hf + 1) * Q_LORA // 2)
                    rk = slice(hf * KV_LORA // 2, (hf + 1) * KV_LORA // 2)
                    g[1][blk, hf, :, e * qw:(e + 1) * qw] = gwuq[rq, h * HP:h * HP + qw]
                    g[2][blk, hf, :, e * HP:e * HP + NOPE] = gwkv[rk, h * HP:h * HP + NOPE]
                    g[2][blk, hf, :, e * HP + NOPE:(e + 1) * HP] = gwkv[rk, HEADS * HP + h * VDIM:HEADS * HP + (h + 1) * VDIM]

        l1, l2 = [], []

        def level1(n):
            for blk in range(4):
                l1.append(copy(k1(n, blk), g[n].at[blk, 1 - c], r1[n].at[blk], sib))
                l1[-1].start()

        def level2(n):
            for blk in range(4):
                copy(k1(n, blk), g[n].at[blk, c], r1[n].at[blk], sib).wait_recv()
            for blk in range(4):
                r1[n][blk] = g[n][blk, c] + r1[n][blk]
                s2[n][blk] = r1[n][blk].astype(BF16)
            for kk in range(2):
                send2(l2copy(n, kk, b_far, kk, 2, kk, near[kk]))
                send2(l2copy(n, 2 + kk, b_near[kk], kk, kk, kk, near[kk]))

        def rows2(n, p):
            hr = half_shapes[n][0] // 2
            return pl.ds(p * hr, hr)

        def l2copy(n, kk, blk, p, slot, dst_p, to):
            return copy(k2(n, kk), s2[n].at[blk, rows2(n, p)], r2[n].at[slot, rows2(n, dst_p)], to)

        def send2(cp):
            l2.append(cp)
            cp.start()

        def merge2(n):
            for kk in range(2):
                blk, rows = b_near[1 - kk], rows2(n, kk)
                l2copy(n, kk, blk, kk, 2, kk, near[kk]).wait_recv()
                s2[n][blk, rows] = (r1[n][blk, rows] + r2[n][2, rows].astype(F32)).astype(BF16)
                send2(l2copy(n, 4 + kk, blk, kk, 1 - kk, kk, near[1 - kk]))

        lv = [copy(kv(k), gv, vbuf.at[me], to) for k, to in enumerate(others)]
        for cp in lv:
            cp.start()
        for n in range(n_lay, n_arr):
            level1(n)
        lay_heads()
        level1(1)
        level1(2)
        level2(order[0])
        lax.fori_loop(0, D_MODEL // rc, lay_in, 0)
        level1(0)
        for i, n in enumerate(order[1:]):
            level2(n)
            merge2(order[i])
        merge2(order[-1])

        l3 = []
        for n in order:
            for kk in range(2, 6):
                l2copy(n, kk, 0, 0, 0, 0, sib).wait_recv()
            red = (r1[n][j] + r2[n][0].astype(F32)) + r2[n][1].astype(F32)
            if n < n_big:
                outs[n][c] = red
                back = [copy(k3(n), outs[n].at[c], outs[n].at[c], sib)]
            else:
                outs[n][j, c] = red
                back = [copy(k3w(k), outs[n].at[j, c], outs[n].at[j, c], to) for k, to in enumerate(others)]
            for cp in back:
                cp.start()
            l3 += back
        for n in range(n_big):
            copy(k3(n), outs[n].at[1 - c], outs[n].at[1 - c], sib).wait_recv()
        for k, (px, py, pc) in enumerate(others):
            landed = outs[n_big].at[2 * px + py, pc]
            copy(k3w(k), landed, landed, (px, py, pc)).wait_recv()
            copy(kv(k), gv, vbuf.at[4 * px + 2 * py + pc], (px, py, pc)).wait_recv()
        vbuf[me] = gv[...]
        total = vbuf[0]
        for d in range(1, 8):
            total = total + vbuf[d]
        ov[...] = total
        for cp in l1 + lv + l2 + l3:
            cp.wait_send()

    vmem = pl.BlockSpec(memory_space=pltpu.VMEM)
    assert g_win.shape == (D_MODEL, D_INR) and 4 * cw == D_INR - C_GATE + gate0 and NOPE + VDIM == HP and HEADS == 8
    out_shape =[jax.ShapeDtypeStruct((2,) + s, F32) for s in half_shapes[:n_big]]
    out_shape += [jax.ShapeDtypeStruct((4, 2) + half_shapes[n_big], F32), jax.ShapeDtypeStruct(gvec.shape, F32)]
    scratch = [pltpu.VMEM((4,) + s, F32) for s in half_shapes] + [pltpu.VMEM((3,) + s, BF16) for s in half_shapes]
    scratch += [pltpu.VMEM((4,) + s, BF16) for s in half_shapes]
    scratch += [pltpu.VMEM((4, 2) + s, F32) for s in half_shapes[:n_lay]]
    scratch += [pltpu.VMEM((8,) + gvec.shape, F32), pltpu.SemaphoreType.DMA((n_sem,)), pltpu.SemaphoreType.DMA((n_sem,))]
    return pl.pallas_call(
        body, name="grad_reduce", out_shape=out_shape,
        in_specs=[vmem] * (n_arr + 1), out_specs=[vmem] * (n_arr + 1), scratch_shapes=scratch,
        compiler_params=pltpu.CompilerParams(vmem_limit_bytes=VMEM_LIMIT),
    )(g_win, g_wuq, g_wkv, g_wout, g_wsp, gvec)


SMALL_ROWS = ((GV_QG, 1, Q_LORA), (GV_KVG, 1, KV_LORA), (GV_SG, 1, GW), (GV_SB, 1, GW),
              (GV_LNG, 1, D_MODEL), (GV_LNB, 1, D_MODEL), (GV_BSP, HEADS, CHUNK))


def _adam_update(g, w, m, v):
    m_new = ADAM_B1 * m + (1.0 - ADAM_B1) * g
    v_new = ADAM_B2 * v + (1.0 - ADAM_B2) * (g * g)
    m_hat = m_new / (1.0 - ADAM_B1 ** ADAM_STEP)
    v_hat = v_new / (1.0 - ADAM_B2 ** ADAM_STEP)
    return -ADAM_LR * (m_hat / (jnp.sqrt(v_hat) + ADAM_EPS) + ADAM_WD * w), m_new, v_new


def _adamw(g_big, w_big, m_big, v_big, gvec, w_small, m_small, v_small):
    nb, ns = len(g_big), len(w_small)

    def body(*refs):
        it = iter(refs)
        take = lambda n: [next(it) for _ in range(n)]
        g_b, w_b, m_b, v_b = take(nb), take(nb), take(nb), take(nb)
        gv = next(it)
        w_s, m_s, v_s = take(ns), take(ns), take(ns)
        g_bo, d_bo, m_bo, v_bo = take(nb), take(nb), take(nb), take(nb)
        g_so, d_so, m_so, v_so = take(ns), take(ns), take(ns), take(ns)
        for n in range(nb):
            gb = g_b[n][...]
            g_bo[n][...] = gb
            d_bo[n][...], m_bo[n][...], v_bo[n][...] = _adam_update(gb, w_b[n][...], m_b[n][...], v_b[n][...])
        for n, (row, nrow, width) in enumerate(SMALL_ROWS):
            gs = gv[row:row + nrow, 0:width]
            g_so[n][...] = gs
            d_so[n][...], m_so[n][...], v_so[n][...] = _adam_update(gs, w_s[n][...], m_s[n][...], v_s[n][...])

    def part(a):
        nd = a.ndim
        if nd > 2 or a.shape[0] % (SUBLANES * ADAM_STEPS) == 0:
            return pl.BlockSpec((a.shape[0] // ADAM_STEPS,) + a.shape[1:], lambda i: (i,) + (0,) * (nd - 1))
        assert a.shape[1] % (LANES * ADAM_STEPS) == 0
        return pl.BlockSpec((a.shape[0], a.shape[1] // ADAM_STEPS), lambda i: (0, i))

    def whole(a):
        nd = a.ndim
        return pl.BlockSpec(a.shape, lambda i: (0,) * nd)

    big = [jax.ShapeDtypeStruct(a.shape, F32) for a in w_big]
    small = [jax.ShapeDtypeStruct(a.shape, F32) for a in w_small]
    return pl.pallas_call(
        body, name="adamw", grid=(ADAM_STEPS,), out_shape=big * 4 + small * 4,
        in_specs=[part(a) for a in g_big + w_big + m_big + v_big] + [whole(gvec)]
        + [whole(a) for a in w_small + m_small + v_small],
        out_specs=[part(a) for a in w_big] * 4 + [whole(a) for a in w_small] * 4,
        compiler_params=pltpu.CompilerParams(dimension_semantics=("arbitrary",), vmem_limit_bytes=VMEM_LIMIT),
    )(*g_big, *w_big, *m_big, *v_big, gvec, *w_small, *m_small, *v_small)


def kernel(x, positions, w_in, q_norm_g, w_uq, kv_norm_g, w_ukv, sgu_norm_g, sgu_norm_b, w_spatial, b_spatial, w_out, ln_g, ln_b, loss_target, m_w_in, m_q_norm_g, m_w_uq, m_kv_norm_g, m_w_ukv, m_sgu_norm_g, m_sgu_norm_b, m_w_spatial, m_b_spatial, m_w_out, m_ln_g, m_ln_b, v_w_in, v_q_norm_g, v_w_uq, v_kv_norm_g, v_w_ukv, v_sgu_norm_g, v_sgu_norm_b, v_w_spatial, v_b_spatial, v_w_out, v_ln_g, v_ln_b):
    seq = x.shape[1]
    x2 = x.reshape(seq, D_MODEL)
    tgt = loss_target.reshape(seq, D_MODEL)
    pos = positions.reshape(seq, 1)

    lane = np.arange(HP)
    half = ROPE // 2
    inv_freq = (1.0 / (ROPE_THETA ** (np.arange(half, dtype=np.float32) / half))).astype(np.float32)
    in_rope = (lane >= NOPE) & (lane < NOPE + ROPE)
    pv_const = np.zeros((PV_ROWS, D_MODEL), np.float32)
    pv_const[PV_INVF, 0:HP] = np.where(in_rope, inv_freq[(lane - NOPE) % half], 0.0)
    pv_const[PV_M1, 0:HP] = np.where((lane >= NOPE) & (lane < NOPE + half), -1.0, 0.0)
    pv_const[PV_M2, 0:HP] = np.where((lane >= NOPE + half) & (lane < NOPE + ROPE), 1.0, 0.0)
    pv_rows = [(PV_QG, q_norm_g), (PV_KVG, kv_norm_g), (PV_SG, sgu_norm_g), (PV_SB, sgu_norm_b), (PV_LNG, ln_g),
               (PV_LNB, ln_b)]
    win, wuq, wkv, a_out, pvec = _weight_gather([w_in, w_uq, w_ukv, w_out], jnp.asarray(pv_const),
                                                [(prow, vec.reshape(1, -1)) for prow, vec in pv_rows])
    wout = a_out.reshape(D_MODEL, D_MODEL)
    tri = jnp.tril(jnp.ones((CHUNK, CHUNK), dtype=bool))
    wt = jnp.where(tri[None], w_spatial, 0.0).astype(BF16)
    wtt = jnp.swapaxes(wt, 1, 2)
    bsp = jnp.repeat(b_spatial.T, VDIM, axis=1)

    cq, ckv, gate, q, k, v, vt, cs = _fwd_pre(x2, pos, win, wuq, wkv, pvec)
    o, lse = _attn_fwd(q, k, vt)
    dh2, do, dgate, g_wout, g_wsp, gvec = _post(x2, tgt, o, gate, wout, pvec, wt, wtt, bsp)
    dq, dk, dv = _attn_bwd(q, k, v, do, o, lse, cs, pvec)
    gx, g_win, g_wuq, g_wkv, gvec = _bwd_pre(x2, dh2, cq, ckv, cs, dq, dk, dv, dgate, win, wuq, wkv, pvec, gvec)

    r_in, r_uq, r_ukv, r_out, r_wsp, r_vec = _grad_reduce(
        g_win, g_wuq, g_wkv, g_wout.reshape(4, 2, D_MODEL // 8, D_MODEL), g_wsp.reshape(4, 2, CHUNK, CHUNK), gvec)

    big = [w_in, w_uq, w_ukv, w_out, w_spatial]
    flip = lambda a: a.T if a.ndim == 2 and a.shape[1] % LANES else a
    flips = lambda arrs: [flip(a) for a in arrs]
    g_big = flips([r_in.reshape(w_in.shape), r_uq.reshape(w_uq.shape), r_ukv.reshape(w_ukv.shape),
                   r_out.reshape(w_out.shape), r_wsp.reshape(w_spatial.shape)])
    small = lambda qg, kvg, sg, sb, lng, lnb, bs: [qg.reshape(1, -1), kvg.reshape(1, -1), sg.reshape(1, -1),
                                                   sb.reshape(1, -1), lng.reshape(1, -1), lnb.reshape(1, -1), bs]
    res = _adamw(g_big, flips(big), flips([m_w_in, m_w_uq, m_w_ukv, m_w_out, m_w_spatial]),
                 flips([v_w_in, v_w_uq, v_w_ukv, v_w_out, v_w_spatial]), r_vec,
                 small(q_norm_g, kv_norm_g, sgu_norm_g, sgu_norm_b, ln_g, ln_b, b_spatial),
                 small(m_q_norm_g, m_kv_norm_g, m_sgu_norm_g, m_sgu_norm_b, m_ln_g, m_ln_b, m_b_spatial),
                 small(v_q_norm_g, v_kv_norm_g, v_sgu_norm_g, v_sgu_norm_b, v_ln_g, v_ln_b, v_b_spatial))
    res = [r.T if n < 4 * len(big) and r.shape != big[n % len(big)].shape else r for n, r in enumerate(res)]

    def ordered(big, sm):
        vec = lambda n: sm[n].reshape(-1)
        return [big[0], vec(0), big[1], vec(1), big[2], vec(2), vec(3), big[4], sm[6], big[3], vec(4), vec(5)]

    loss = r_vec[GV_LOSS, 0]
    return (loss, gx.reshape(1, seq, D_MODEL), *ordered(res[0:5], res[20:27]), *ordered(res[5:10], res[27:34]),
            *ordered(res[10:15], res[34:41]), *ordered(res[15:20], res[41:48]))
```
